```python
import math
import jax
import jax.numpy as jnp
from jax import lax
import numpy as np

D_MODEL = 1024
BATCH = 8
SEQ = 8192
DEPTH = 2

N_MIXERS = 2
Q_BLOCK = 128
NORM_EPS = 1e-6
FOX_HEADS = 16
FOX_HEAD_DIM = D_MODEL // FOX_HEADS
FOX_WIDTH = FOX_HEADS * FOX_HEAD_DIM
MLA_HEADS = 16
MLA_NOPE_DIM = 64
MLA_ROPE_DIM = 32
MLA_V_DIM = 64
MLA_Q_RANK = 384
MLA_KV_RANK = 256
ROPE_BASE = 10000.0
D_FF = 2816
CONV_WIDTH = 3
N_FOX_LAYERS = (DEPTH + 1) // 2
N_MLA_LAYERS = DEPTH // 2

kernel_name = 'hybrid_fox_mla_convffn_adaln'


def rms_norm(x):
    xf = x.astype(jnp.float32)
    y = xf * lax.rsqrt(jnp.mean(xf * xf, axis=-1, keepdims=True) + NORM_EPS)
    return y.astype(x.dtype)


def ada_modulation(c, w, b):
    mod = jax.nn.silu(c) @ w + b
    shift, scale, gate = jnp.split(mod[:, None, :], 3, axis=-1)
    return shift, scale, gate


def causal_mask(start, seq):
    q_pos = start + jnp.arange(Q_BLOCK)
    return q_pos[:, None] >= jnp.arange(seq)[None, :]


def masked_softmax(s, mask):
    s = jnp.where(mask, s.astype(jnp.float32), -jnp.inf)
    return jax.nn.softmax(s, axis=-1)


def causal_block_sweep(block_fn, batch, seq):
    starts = jnp.arange(seq // Q_BLOCK, dtype=jnp.int32) * Q_BLOCK
    out = lax.map(block_fn, starts)
    return jnp.moveaxis(out, 0, 1).reshape(batch, seq, -1)


def forgetting_attention(h, w_in, b_f, w_o):
    B, S, _ = h.shape
    proj = h @ w_in
    q, k, v, f_logit = jnp.split(proj, [FOX_WIDTH, 2 * FOX_WIDTH, 3 * FOX_WIDTH], axis=-1)
    q = q.reshape(B, S, FOX_HEADS, FOX_HEAD_DIM)
    k = k.reshape(B, S, FOX_HEADS, FOX_HEAD_DIM)
    v = v.reshape(B, S, FOX_HEADS, FOX_HEAD_DIM)
    log_f = jax.nn.log_sigmoid((f_logit + b_f).astype(jnp.float32))
    cum = jnp.cumsum(log_f, axis=1).transpose(0, 2, 1)
    scale = FOX_HEAD_DIM ** -0.5

    def block(start):
        qb = lax.dynamic_slice_in_dim(q, start, Q_BLOCK, axis=1)
        cq = lax.dynamic_slice_in_dim(cum, start, Q_BLOCK, axis=2)
        s = jnp.einsum('bqhd,bkhd->bhqk', qb, k).astype(jnp.float32) * scale
        s = s + (cq[..., :, None] - cum[..., None, :])
        p = masked_softmax(s, causal_mask(start, S))
        return jnp.einsum('bhqk,bkhd->bqhd', p.astype(v.dtype), v)

    o = causal_block_sweep(block, B, S)
    return o @ w_o


def apply_rope(x, cos, sin):
    half = x.shape[-1] // 2
    x1, x2 = x[..., :half], x[..., half:]
    return jnp.concatenate([x1 * cos - x2 * sin, x2 * cos + x1 * sin], axis=-1)


def latent_attention(h, w_a, g_q, g_kv, w_uq, w_ukv, w_o):
    B, S, _ = h.shape
    a = h @ w_a
    c_q, c_kv, k_rope = jnp.split(a, [MLA_Q_RANK, MLA_Q_RANK + MLA_KV_RANK], axis=-1)
    c_q = rms_norm(c_q) * g_q
    c_kv = rms_norm(c_kv) * g_kv
    q = (c_q @ w_uq).reshape(B, S, MLA_HEADS, MLA_NOPE_DIM + MLA_ROPE_DIM)
    q_nope, q_rope = q[..., :MLA_NOPE_DIM], q[..., MLA_NOPE_DIM:]
    kv = (c_kv @ w_ukv).reshape(B, S, MLA_HEADS, MLA_NOPE_DIM + MLA_V_DIM)
    k_nope, v = kv[..., :MLA_NOPE_DIM], kv[..., MLA_NOPE_DIM:]

    pos = jnp.arange(S, dtype=jnp.float32)
    inv_freq = ROPE_BASE ** (-jnp.arange(0, MLA_ROPE_DIM, 2, dtype=jnp.float32) / MLA_ROPE_DIM)
    ang = pos[:, None] * inv_freq[None, :]
    cos = jnp.cos(ang).astype(h.dtype)
    sin = jnp.sin(ang).astype(h.dtype)
    q_rope = apply_rope(q_rope, cos[:, None, :], sin[:, None, :])
    k_rope = apply_rope(k_rope, cos, sin)
    scale = (MLA_NOPE_DIM + MLA_ROPE_DIM) ** -0.5

    def block(start):
        qn = lax.dynamic_slice_in_dim(q_nope, start, Q_BLOCK, axis=1)
        qr = lax.dynamic_slice_in_dim(q_rope, start, Q_BLOCK, axis=1)
        s = jnp.einsum('bqhd,bkhd->bhqk', qn, k_nope) + jnp.einsum('bqhr,bkr->bhqk', qr, k_rope)
        p = masked_softmax(s.astype(jnp.float32) * scale, causal_mask(start, S))
        return jnp.einsum('bhqk,bkhd->bqhd', p.astype(v.dtype), v)

    o = causal_block_sweep(block, B, S)
    return o @ w_o


def conv_gated_mlp(h, w_in, conv_w, conv_b, w_out):
    S = h.shape[1]
    u = h @ w_in
    u_pad = jnp.pad(u, ((0, 0), (CONV_WIDTH - 1, 0), (0, 0)))
    y = conv_b
    for j in range(CONV_WIDTH):
        y = y + conv_w[j] * u_pad[:, j:j + S, :]
    gate, val = jnp.split(y, 2, axis=-1)
    return (jax.nn.silu(gate) * val) @ w_out


def _fwd_setup_inputs(seed: int = 0) -> dict:
    key = jax.random.key(seed)
    ks = jax.random.split(key, 20)
    D = D_MODEL
    nrm = jax.random.normal
    f32 = jnp.float32
    x = nrm(ks[0], (BATCH, SEQ, D), f32)
    c = nrm(ks[1], (BATCH, D), f32)
    ada_w = nrm(ks[2], (DEPTH, 2, D, 3 * D), f32) * D ** -0.5
    ada_b = 0.02 * nrm(ks[3], (DEPTH, 2, 3 * D), f32)
    fox_w_in = nrm(ks[4], (N_FOX_LAYERS, D, 3 * FOX_WIDTH + FOX_HEADS), f32) * D ** -0.5
    fox_w_in = fox_w_in.at[..., 3 * FOX_WIDTH:].multiply(0.1)
    fox_b_f = jnp.linspace(2.0, 6.0, FOX_HEADS, dtype=f32)[None, :] + 0.1 * nrm(ks[5], (N_FOX_LAYERS, FOX_HEADS), f32)
    fox_w_o = nrm(ks[6], (N_FOX_LAYERS, FOX_WIDTH, D), f32) * FOX_WIDTH ** -0.5
    mla_w_a = nrm(ks[7], (N_MLA_LAYERS, D, MLA_Q_RANK + MLA_KV_RANK + MLA_ROPE_DIM), f32) * D ** -0.5
    mla_g_q = 1.0 + 0.02 * nrm(ks[8], (N_MLA_LAYERS, MLA_Q_RANK), f32)
    mla_g_kv = 1.0 + 0.02 * nrm(ks[9], (N_MLA_LAYERS, MLA_KV_RANK), f32)
    mla_w_uq = nrm(ks[10], (N_MLA_LAYERS, MLA_Q_RANK, MLA_HEADS * (MLA_NOPE_DIM + MLA_ROPE_DIM)), f32) * MLA_Q_RANK ** -0.5
    mla_w_ukv = nrm(ks[11], (N_MLA_LAYERS, MLA_KV_RANK, MLA_HEADS * (MLA_NOPE_DIM + MLA_V_DIM)), f32) * MLA_KV_RANK ** -0.5
    mla_w_o = nrm(ks[12], (N_MLA_LAYERS, MLA_HEADS * MLA_V_DIM, D), f32) * (MLA_HEADS * MLA_V_DIM) ** -0.5
    ffn_w_in = nrm(ks[13], (DEPTH, D, 2 * D_FF), f32) * D ** -0.5
    ffn_conv_w = nrm(ks[14], (DEPTH, CONV_WIDTH, 2 * D_FF), f32) * CONV_WIDTH ** -0.5
    ffn_conv_b = 0.02 * nrm(ks[15], (DEPTH, 2 * D_FF), f32)
    ffn_w_out = nrm(ks[16], (DEPTH, D_FF, D), f32) * D_FF ** -0.5
    final_g = 1.0 + 0.02 * nrm(ks[17], (D,), f32)
    return {'x': x, 'c': c, 'ada_w': ada_w, 'ada_b': ada_b,
            'fox_w_in': fox_w_in, 'fox_b_f': fox_b_f, 'fox_w_o': fox_w_o,
            'mla_w_a': mla_w_a, 'mla_g_q': mla_g_q, 'mla_g_kv': mla_g_kv,
            'mla_w_uq': mla_w_uq, 'mla_w_ukv': mla_w_ukv, 'mla_w_o': mla_w_o,
            'ffn_w_in': ffn_w_in, 'ffn_conv_w': ffn_conv_w, 'ffn_conv_b': ffn_conv_b,
            'ffn_w_out': ffn_w_out, 'final_g': final_g}


def _fwd_reference(x, c, ada_w, ada_b, fox_w_in, fox_b_f, fox_w_o, mla_w_a, mla_g_q, mla_g_kv,
              mla_w_uq, mla_w_ukv, mla_w_o, ffn_w_in, ffn_conv_w, ffn_conv_b, ffn_w_out, final_g):
    for i in range(DEPTH):
        j = i // N_MIXERS
        shift, scale, gate = ada_modulation(c, ada_w[i, 0], ada_b[i, 0])
        h = rms_norm(x) * (1.0 + scale) + shift
        if i % N_MIXERS == 0:
            y = forgetting_attention(h, fox_w_in[j], fox_b_f[j], fox_w_o[j])
        else:
            y = latent_attention(h, mla_w_a[j], mla_g_q[j], mla_g_kv[j],
                                 mla_w_uq[j], mla_w_ukv[j], mla_w_o[j])
        x = x + gate * y
        shift, scale, gate = ada_modulation(c, ada_w[i, 1], ada_b[i, 1])
        h = rms_norm(x) * (1.0 + scale) + shift
        x = x + gate * conv_gated_mlp(h, ffn_w_in[i], ffn_conv_w[i], ffn_conv_b[i], ffn_w_out[i])
    return rms_norm(x) * final_g


import jax as _jax
import jax.numpy as _jnp

TWIN_FORMAT = 'train_step'
FWD_PARAMS = ['x', 'c', 'ada_w', 'ada_b', 'fox_w_in', 'fox_b_f', 'fox_w_o', 'mla_w_a', 'mla_g_q', 'mla_g_kv', 'mla_w_uq', 'mla_w_ukv', 'mla_w_o', 'ffn_w_in', 'ffn_conv_w', 'ffn_conv_b', 'ffn_w_out', 'final_g']
TWIN_WEIGHTS = ['ada_w', 'ada_b', 'fox_w_in', 'fox_b_f', 'fox_w_o', 'mla_w_a', 'mla_g_q', 'mla_g_kv', 'mla_w_uq', 'mla_w_ukv', 'mla_w_o', 'ffn_w_in', 'ffn_conv_w', 'ffn_conv_b', 'ffn_w_out', 'final_g']
TWIN_DIFF_INPUT = 'x'
TWIN_INPUTS = ['x', 'c', 'ada_w', 'ada_b', 'fox_w_in', 'fox_b_f', 'fox_w_o', 'mla_w_a', 'mla_g_q', 'mla_g_kv', 'mla_w_uq', 'mla_w_ukv', 'mla_w_o', 'ffn_w_in', 'ffn_conv_w', 'ffn_conv_b', 'ffn_w_out', 'final_g', 'loss_target', 'm_ada_w', 'm_ada_b', 'm_fox_w_in', 'm_fox_b_f', 'm_fox_w_o', 'm_mla_w_a', 'm_mla_g_q', 'm_mla_g_kv', 'm_mla_w_uq', 'm_mla_w_ukv', 'm_mla_w_o', 'm_ffn_w_in', 'm_ffn_conv_w', 'm_ffn_conv_b', 'm_ffn_w_out', 'm_final_g', 'v_ada_w', 'v_ada_b', 'v_fox_w_in', 'v_fox_b_f', 'v_fox_w_o', 'v_mla_w_a', 'v_mla_g_q', 'v_mla_g_kv', 'v_mla_w_uq', 'v_mla_w_ukv', 'v_mla_w_o', 'v_ffn_w_in', 'v_ffn_conv_w', 'v_ffn_conv_b', 'v_ffn_w_out', 'v_final_g']
TWIN_OUTPUTS = ['loss', 'grad_x', 'grad_ada_w', 'grad_ada_b', 'grad_fox_w_in', 'grad_fox_b_f', 'grad_fox_w_o', 'grad_mla_w_a', 'grad_mla_g_q', 'grad_mla_g_kv', 'grad_mla_w_uq', 'grad_mla_w_ukv', 'grad_mla_w_o', 'grad_ffn_w_in', 'grad_ffn_conv_w', 'grad_ffn_conv_b', 'grad_ffn_w_out', 'grad_final_g', 'delta_ada_w', 'delta_ada_b', 'delta_fox_w_in', 'delta_fox_b_f', 'delta_fox_w_o', 'delta_mla_w_a', 'delta_mla_g_q', 'delta_mla_g_kv', 'delta_mla_w_uq', 'delta_mla_w_ukv', 'delta_mla_w_o', 'delta_ffn_w_in', 'delta_ffn_conv_w', 'delta_ffn_conv_b', 'delta_ffn_w_out', 'delta_final_g', 'new_m_ada_w', 'new_m_ada_b', 'new_m_fox_w_in', 'new_m_fox_b_f', 'new_m_fox_w_o', 'new_m_mla_w_a', 'new_m_mla_g_q', 'new_m_mla_g_kv', 'new_m_mla_w_uq', 'new_m_mla_w_ukv', 'new_m_mla_w_o', 'new_m_ffn_w_in', 'new_m_ffn_conv_w', 'new_m_ffn_conv_b', 'new_m_ffn_w_out', 'new_m_final_g', 'new_v_ada_w', 'new_v_ada_b', 'new_v_fox_w_in', 'new_v_fox_b_f', 'new_v_fox_w_o', 'new_v_mla_w_a', 'new_v_mla_g_q', 'new_v_mla_g_kv', 'new_v_mla_w_uq', 'new_v_mla_w_ukv', 'new_v_mla_w_o', 'new_v_ffn_w_in', 'new_v_ffn_conv_w', 'new_v_ffn_conv_b', 'new_v_ffn_w_out', 'new_v_final_g']
TWIN_LEAF_KINDS = {'loss': 'loss', 'grad_x': 'grad_x', 'grad_ada_w': 'grad_w', 'grad_ada_b': 'grad_w', 'grad_fox_w_in': 'grad_w', 'grad_fox_b_f': 'grad_w', 'grad_fox_w_o': 'grad_w', 'grad_mla_w_a': 'grad_w', 'grad_mla_g_q': 'grad_w', 'grad_mla_g_kv': 'grad_w', 'grad_mla_w_uq': 'grad_w', 'grad_mla_w_ukv': 'grad_w', 'grad_mla_w_o': 'grad_w', 'grad_ffn_w_in': 'grad_w', 'grad_ffn_conv_w': 'grad_w', 'grad_ffn_conv_b': 'grad_w', 'grad_ffn_w_out': 'grad_w', 'grad_final_g': 'grad_w', 'delta_ada_w': 'delta_w', 'delta_ada_b': 'delta_w', 'delta_fox_w_in': 'delta_w', 'delta_fox_b_f': 'delta_w', 'delta_fox_w_o': 'delta_w', 'delta_mla_w_a': 'delta_w', 'delta_mla_g_q': 'delta_w', 'delta_mla_g_kv': 'delta_w', 'delta_mla_w_uq': 'delta_w', 'delta_mla_w_ukv': 'delta_w', 'delta_mla_w_o': 'delta_w', 'delta_ffn_w_in': 'delta_w', 'delta_ffn_conv_w': 'delta_w', 'delta_ffn_conv_b': 'delta_w', 'delta_ffn_w_out': 'delta_w', 'delta_final_g': 'delta_w', 'new_m_ada_w': 'new_m', 'new_m_ada_b': 'new_m', 'new_m_fox_w_in': 'new_m', 'new_m_fox_b_f': 'new_m', 'new_m_fox_w_o': 'new_m', 'new_m_mla_w_a': 'new_m', 'new_m_mla_g_q': 'new_m', 'new_m_mla_g_kv': 'new_m', 'new_m_mla_w_uq': 'new_m', 'new_m_mla_w_ukv': 'new_m', 'new_m_mla_w_o': 'new_m', 'new_m_ffn_w_in': 'new_m', 'new_m_ffn_conv_w': 'new_m', 'new_m_ffn_conv_b': 'new_m', 'new_m_ffn_w_out': 'new_m', 'new_m_final_g': 'new_m', 'new_v_ada_w': 'new_v', 'new_v_ada_b': 'new_v', 'new_v_fox_w_in': 'new_v', 'new_v_fox_b_f': 'new_v', 'new_v_fox_w_o': 'new_v', 'new_v_mla_w_a': 'new_v', 'new_v_mla_g_q': 'new_v', 'new_v_mla_g_kv': 'new_v', 'new_v_mla_w_uq': 'new_v', 'new_v_mla_w_ukv': 'new_v', 'new_v_mla_w_o': 'new_v', 'new_v_ffn_w_in': 'new_v', 'new_v_ffn_conv_w': 'new_v', 'new_v_ffn_conv_b': 'new_v', 'new_v_ffn_w_out': 'new_v', 'new_v_final_g': 'new_v'}


def _forward(args):
    return _fwd_reference(*[args[k] for k in FWD_PARAMS])


def _output_shape():
    def fwd():
        inp = _fwd_setup_inputs(0)
        return _fwd_reference(*[inp[k] for k in FWD_PARAMS])
    out = _jax.eval_shape(fwd)
    return out.shape, out.dtype

N_MICROBATCH = 1
ADAM_LR = 0.001
ADAM_B1 = 0.9
ADAM_B2 = 0.999
ADAM_EPS = 1e-08
ADAM_WD = 0.01
ADAM_STEP = 10
PER_EXAMPLE_BATCH_AXIS = {'x': 0, 'c': 0, 'loss_target': 0}
SHARED_INPUTS = []
_WEIGHT_DTYPES = {'ada_w': _jnp.float32, 'ada_b': _jnp.float32, 'fox_w_in': _jnp.float32, 'fox_b_f': _jnp.float32, 'fox_w_o': _jnp.float32, 'mla_w_a': _jnp.float32, 'mla_g_q': _jnp.float32, 'mla_g_kv': _jnp.float32, 'mla_w_uq': _jnp.float32, 'mla_w_ukv': _jnp.float32, 'mla_w_o': _jnp.float32, 'ffn_w_in': _jnp.float32, 'ffn_conv_w': _jnp.float32, 'ffn_conv_b': _jnp.float32, 'ffn_w_out': _jnp.float32, 'final_g': _jnp.float32}
MOMENT_SCALE = {'ada_w': 1.064544e-01, 'ada_b': 1.833981e-01, 'fox_w_in': 9.249945e-02, 'fox_b_f': 3.026233e-01, 'fox_w_o': 1.294517e-01, 'mla_w_a': 1.037958e-01, 'mla_g_q': 2.904935e-02, 'mla_g_kv': 1.452803e-01, 'mla_w_uq': 1.412656e-02, 'mla_w_ukv': 5.978248e-02, 'mla_w_o': 8.664233e-02, 'ffn_w_in': 7.806114e-02, 'ffn_conv_w': 8.060023e-02, 'ffn_conv_b': 6.169291e-02, 'ffn_w_out': 1.305239e-01, 'final_g': 6.515634e+01}


def _to_microbatches(a, axis):
    t = _jnp.moveaxis(a, axis, 0)
    t = t.reshape((N_MICROBATCH, t.shape[0] // N_MICROBATCH) + t.shape[1:])
    return _jnp.moveaxis(t, 1, axis + 1)


def setup_inputs(seed: int = 0) -> dict:
    inp = _fwd_setup_inputs(seed)
    key = _jax.random.fold_in(_jax.random.key(seed), 7919)
    shape, _ = _output_shape()
    out = dict(inp)
    out["loss_target"] = _jax.random.normal(_jax.random.fold_in(key, 0), shape, _jnp.float32)
    for i, name in enumerate(TWIN_WEIGHTS):
        w = inp[name].astype(_jnp.float32)
        if MOMENT_SCALE is None:
            s = _jnp.sqrt(_jnp.mean(_jnp.square(w)) + 1e-30)
        else:
            s = MOMENT_SCALE[name]
        km, kv = _jax.random.split(_jax.random.fold_in(key, i + 1))
        out[name] = w
        out["m_" + name] = s * _jax.random.normal(km, w.shape, _jnp.float32)
        out["v_" + name] = (s * s) * _jax.random.uniform(kv, w.shape, _jnp.float32, 0.5, 1.5)
    if N_MICROBATCH > 1:
        for name, axis in PER_EXAMPLE_BATCH_AXIS.items():
            out[name] = _to_microbatches(out[name], axis)
    return {'x': out['x'], 'c': out['c'], 'ada_w': out['ada_w'], 'ada_b': out['ada_b'], 'fox_w_in': out['fox_w_in'], 'fox_b_f': out['fox_b_f'], 'fox_w_o': out['fox_w_o'], 'mla_w_a': out['mla_w_a'], 'mla_g_q': out['mla_g_q'], 'mla_g_kv': out['mla_g_kv'], 'mla_w_uq': out['mla_w_uq'], 'mla_w_ukv': out['mla_w_ukv'], 'mla_w_o': out['mla_w_o'], 'ffn_w_in': out['ffn_w_in'], 'ffn_conv_w': out['ffn_conv_w'], 'ffn_conv_b': out['ffn_conv_b'], 'ffn_w_out': out['ffn_w_out'], 'final_g': out['final_g'], 'loss_target': out['loss_target'], 'm_ada_w': out['m_ada_w'], 'm_ada_b': out['m_ada_b'], 'm_fox_w_in': out['m_fox_w_in'], 'm_fox_b_f': out['m_fox_b_f'], 'm_fox_w_o': out['m_fox_w_o'], 'm_mla_w_a': out['m_mla_w_a'], 'm_mla_g_q': out['m_mla_g_q'], 'm_mla_g_kv': out['m_mla_g_kv'], 'm_mla_w_uq': out['m_mla_w_uq'], 'm_mla_w_ukv': out['m_mla_w_ukv'], 'm_mla_w_o': out['m_mla_w_o'], 'm_ffn_w_in': out['m_ffn_w_in'], 'm_ffn_conv_w': out['m_ffn_conv_w'], 'm_ffn_conv_b': out['m_ffn_conv_b'], 'm_ffn_w_out': out['m_ffn_w_out'], 'm_final_g': out['m_final_g'], 'v_ada_w': out['v_ada_w'], 'v_ada_b': out['v_ada_b'], 'v_fox_w_in': out['v_fox_w_in'], 'v_fox_b_f': out['v_fox_b_f'], 'v_fox_w_o': out['v_fox_w_o'], 'v_mla_w_a': out['v_mla_w_a'], 'v_mla_g_q': out['v_mla_g_q'], 'v_mla_g_kv': out['v_mla_g_kv'], 'v_mla_w_uq': out['v_mla_w_uq'], 'v_mla_w_ukv': out['v_mla_w_ukv'], 'v_mla_w_o': out['v_mla_w_o'], 'v_ffn_w_in': out['v_ffn_w_in'], 'v_ffn_conv_w': out['v_ffn_conv_w'], 'v_ffn_conv_b': out['v_ffn_conv_b'], 'v_ffn_w_out': out['v_ffn_w_out'], 'v_final_g': out['v_final_g']}


def _loss(weights, diff, rest, loss_target):
    with _jax.named_scope("forward"):
        args = {**rest, TWIN_DIFF_INPUT: diff, **{k: w.astype(_WEIGHT_DTYPES[k]) for k, w in weights.items()}}
        y = _forward(args)
    with _jax.named_scope("loss_head"):
        err = _jnp.square(y.astype(_jnp.float32) - loss_target)
        return 0.5 * _jnp.sum(_jnp.mean(err, axis=-1)) if err.ndim else 0.5 * err


def _adamw(w, g, m, v):
    m = ADAM_B1 * m + (1.0 - ADAM_B1) * g
    v = ADAM_B2 * v + (1.0 - ADAM_B2) * _jnp.square(g)
    m_hat = m / (1.0 - ADAM_B1 ** ADAM_STEP)
    v_hat = v / (1.0 - ADAM_B2 ** ADAM_STEP)
    delta = -ADAM_LR * (m_hat / (_jnp.sqrt(v_hat) + ADAM_EPS) + ADAM_WD * w)
    return delta, m, v


def reference(x, c, ada_w, ada_b, fox_w_in, fox_b_f, fox_w_o, mla_w_a, mla_g_q, mla_g_kv, mla_w_uq, mla_w_ukv, mla_w_o, ffn_w_in, ffn_conv_w, ffn_conv_b, ffn_w_out, final_g, loss_target, m_ada_w, m_ada_b, m_fox_w_in, m_fox_b_f, m_fox_w_o, m_mla_w_a, m_mla_g_q, m_mla_g_kv, m_mla_w_uq, m_mla_w_ukv, m_mla_w_o, m_ffn_w_in, m_ffn_conv_w, m_ffn_conv_b, m_ffn_w_out, m_final_g, v_ada_w, v_ada_b, v_fox_w_in, v_fox_b_f, v_fox_w_o, v_mla_w_a, v_mla_g_q, v_mla_g_kv, v_mla_w_uq, v_mla_w_ukv, v_mla_w_o, v_ffn_w_in, v_ffn_conv_w, v_ffn_conv_b, v_ffn_w_out, v_final_g):
    given = dict(x=x, c=c, ada_w=ada_w, ada_b=ada_b, fox_w_in=fox_w_in, fox_b_f=fox_b_f, fox_w_o=fox_w_o, mla_w_a=mla_w_a, mla_g_q=mla_g_q, mla_g_kv=mla_g_kv, mla_w_uq=mla_w_uq, mla_w_ukv=mla_w_ukv, mla_w_o=mla_w_o, ffn_w_in=ffn_w_in, ffn_conv_w=ffn_conv_w, ffn_conv_b=ffn_conv_b, ffn_w_out=ffn_w_out, final_g=final_g, loss_target=loss_target, m_ada_w=m_ada_w, m_ada_b=m_ada_b, m_fox_w_in=m_fox_w_in, m_fox_b_f=m_fox_b_f, m_fox_w_o=m_fox_w_o, m_mla_w_a=m_mla_w_a, m_mla_g_q=m_mla_g_q, m_mla_g_kv=m_mla_g_kv, m_mla_w_uq=m_mla_w_uq, m_mla_w_ukv=m_mla_w_ukv, m_mla_w_o=m_mla_w_o, m_ffn_w_in=m_ffn_w_in, m_ffn_conv_w=m_ffn_conv_w, m_ffn_conv_b=m_ffn_conv_b, m_ffn_w_out=m_ffn_w_out, m_final_g=m_final_g, v_ada_w=v_ada_w, v_ada_b=v_ada_b, v_fox_w_in=v_fox_w_in, v_fox_b_f=v_fox_b_f, v_fox_w_o=v_fox_w_o, v_mla_w_a=v_mla_w_a, v_mla_g_q=v_mla_g_q, v_mla_g_kv=v_mla_g_kv, v_mla_w_uq=v_mla_w_uq, v_mla_w_ukv=v_mla_w_ukv, v_mla_w_o=v_mla_w_o, v_ffn_w_in=v_ffn_w_in, v_ffn_conv_w=v_ffn_conv_w, v_ffn_conv_b=v_ffn_conv_b, v_ffn_w_out=v_ffn_w_out, v_final_g=v_final_g)
    weights = {n: given[n] for n in TWIN_WEIGHTS}
    shared = {n: given[n] for n in SHARED_INPUTS}
    per_example = {n: given[n] for n in ['x', 'c']}
    grad_fn = _jax.value_and_grad(_loss, argnums=(0, 1))

    def one_microbatch(ex, loss_target):
        ex = dict(ex)
        diff = ex.pop(TWIN_DIFF_INPUT)
        return grad_fn(weights, diff, {**shared, **ex}, loss_target)

    if N_MICROBATCH == 1:
        loss, (grad_w, grad_x) = one_microbatch(per_example, given["loss_target"])
    else:
        def body(carry, xs):
            loss_sum, grad_sum = carry
            l_k, (gw_k, gx_k) = one_microbatch(xs[0], xs[1])
            with _jax.named_scope("update"):
                return (loss_sum + l_k, _jax.tree.map(_jnp.add, grad_sum, gw_k)), gx_k

        init = (_jnp.zeros((), _jnp.float32), _jax.tree.map(_jnp.zeros_like, weights))
        (loss, grad_w), grad_x = _jax.lax.scan(body, init, (per_example, given["loss_target"]))
    with _jax.named_scope("update"):
        delta_w, new_m, new_v = {}, {}, {}
        for n in TWIN_WEIGHTS:
            delta_w[n], new_m[n], new_v[n] = _adamw(weights[n], grad_w[n], given["m_" + n], given["v_" + n])
    return (loss, grad_x, *[grad_w[n] for n in TWIN_WEIGHTS], *[delta_w[n] for n in TWIN_WEIGHTS],
            *[new_m[n] for n in TWIN_WEIGHTS], *[new_v[n] for n in TWIN_WEIGHTS])
```

```python
import functools

import jax
import jax.numpy as jnp
from jax import lax
from jax.experimental import pallas as pl
from jax.experimental.pallas import tpu as pltpu

F32 = jnp.float32
BF16 = jnp.bfloat16
HIGHEST = lax.Precision.HIGHEST

N_DEV = 8
D_MODEL = 1024
N_HEADS = 16
HEAD_DIM = 64
MLA_ROPE_HALF = 16
MLA_Q_RANK = 384
MLA_KV_RANK = 256
MLA_ROPE_DIM = 32
D_FF = 2816
NORM_EPS = 1e-6
ROPE_BASE = 10000.0
ADAM_LR = 0.001
ADAM_B1 = 0.9
ADAM_B2 = 0.999
ADAM_EPS = 1e-08
ADAM_WD = 0.01
ADAM_STEP = 10

LANES = 128
VMEM_LIMIT_BYTES = 56 * 1024 * 1024
ROW_BLOCK = 512
ATT_BLOCK = 512
CONV_ROWS = 1024
MM_BM, MM_BN, MM_BK = 512, 1024, 1024


def _params(*sem):
    return pltpu.CompilerParams(dimension_semantics=sem or None, vmem_limit_bytes=VMEM_LIMIT_BYTES)


def _blk(dim, pref):
    if dim <= pref:
        return dim
    b = pref - pref % LANES
    while b >= LANES:
        if dim % b == 0:
            return b
        b -= LANES
    raise ValueError(f"no block for {dim}")


def _row_blk(rows, pref):
    if rows <= pref:
        return rows
    for b in range(pref - pref % 8, 7, -8):
        if rows % b == 0:
            return b
    return rows


def _pad_axis(a, axis, mult):
    pad = (-a.shape[axis]) % mult
    if pad == 0:
        return a
    widths = [(0, 0)] * a.ndim
    widths[axis] = (0, pad)
    return jnp.pad(a, widths)


def _matmul(a, b, *, ta=False, tb=False, out_dtype=F32, name):
    m, k = (a.shape[1], a.shape[0]) if ta else a.shape
    n = b.shape[0] if tb else b.shape[1]
    assert (b.shape[1] if tb else b.shape[0]) == k, (a.shape, b.shape, ta, tb)
    bm, bn, bk = _blk(m, MM_BM), _blk(n, MM_BN), _blk(k, MM_BK)
    nk = k // bk
    dims = (((0 if ta else 1,), (1 if tb else 0,)), ((), ()))

    def body(a_ref, b_ref, o_ref, acc_ref):
        kk = pl.program_id(2)

        @pl.when(kk == 0)
        def _():
            acc_ref[...] = jnp.zeros_like(acc_ref)

        acc_ref[...] += lax.dot_general(a_ref[...].astype(BF16), b_ref[...].astype(BF16), dims,
                                        preferred_element_type=F32)

        @pl.when(kk == nk - 1)
        def _():
            o_ref[...] = acc_ref[...].astype(out_dtype)

    a_spec = pl.BlockSpec((bk, bm), lambda i, j, kk: (kk, i)) if ta else pl.BlockSpec((bm, bk), lambda i, j, kk: (i, kk))
    b_spec = pl.BlockSpec((bn, bk), lambda i, j, kk: (j, kk)) if tb else pl.BlockSpec((bk, bn), lambda i, j, kk: (kk, j))
    return pl.pallas_call(
        body, name=name, grid=(m // bm, n // bn, nk),
        in_specs=[a_spec, b_spec], out_specs=pl.BlockSpec((bm, bn), lambda i, j, kk: (i, j)),
        out_shape=jax.ShapeDtypeStruct((m, n), out_dtype),
        scratch_shapes=[pltpu.VMEM((bm, bn), F32)],
        compiler_params=_params("parallel", "parallel", "arbitrary"),
    )(a, b)


def _norm_fwd(x, mul, add, *, plus_one, out_dtype, name):
    s, n = x.shape
    bs = _blk(s, ROW_BLOCK)

    def body(x_ref, m_ref, a_ref, o_ref):
        xv = x_ref[...]
        r = lax.rsqrt(jnp.mean(xv * xv, axis=-1, keepdims=True) + NORM_EPS)
        mv = m_ref[...] + 1.0 if plus_one else m_ref[...]
        o_ref[...] = (xv * r * mv + a_ref[...]).astype(out_dtype)

    row = pl.BlockSpec((bs, n), lambda i: (i, 0))
    vec = pl.BlockSpec((1, n), lambda i: (0, 0))
    return pl.pallas_call(
        body, name=name, grid=(s // bs,), in_specs=[row, vec, vec], out_specs=row,
        out_shape=jax.ShapeDtypeStruct((s, n), out_dtype), compiler_params=_params("parallel"),
    )(x, mul, add)


def _norm_bwd(x, mul, dy, dres, *, plus_one, name):
    s, n = x.shape
    bs = _blk(s, ROW_BLOCK)
    has_res = dres is not None

    def body(*refs):
        if has_res:
            x_ref, m_ref, dy_ref, dres_ref, dx_ref, dm_ref, da_ref = refs
        else:
            x_ref, m_ref, dy_ref, dx_ref, dm_ref, da_ref = refs
        xv = x_ref[...]
        dyv = dy_ref[...].astype(F32)
        r = lax.rsqrt(jnp.mean(xv * xv, axis=-1, keepdims=True) + NORM_EPS)
        xn = xv * r
        mv = m_ref[...] + 1.0 if plus_one else m_ref[...]
        g = dyv * mv
        dx = r * (g - xn * jnp.mean(g * xn, axis=-1, keepdims=True))
        if has_res:
            dx = dx + dres_ref[...]
        dx_ref[...] = dx

        @pl.when(pl.program_id(0) == 0)
        def _():
            dm_ref[...] = jnp.zeros_like(dm_ref)
            da_ref[...] = jnp.zeros_like(da_ref)

        dm_ref[...] += jnp.sum(dyv * xn, axis=0, keepdims=True)
        da_ref[...] += jnp.sum(dyv, axis=0, keepdims=True)

    row = pl.BlockSpec((bs, n), lambda i: (i, 0))
    vec = pl.BlockSpec((1, n), lambda i: (0, 0))
    ins = [x, mul, dy] + ([dres] if has_res else [])
    return pl.pallas_call(
        body, name=name, grid=(s // bs,),
        in_specs=[row, vec, row] + ([row] if has_res else []), out_specs=[row, vec, vec],
        out_shape=[jax.ShapeDtypeStruct((s, n), F32), jax.ShapeDtypeStruct((1, n), F32), jax.ShapeDtypeStruct((1, n), F32)],
        compiler_params=_params("arbitrary"),
    )(*ins)


def _resid_fwd(x, y, gate, *, name):
    s, n = x.shape
    bs = _blk(s, ROW_BLOCK)

    def body(x_ref, y_ref, g_ref, o_ref):
        o_ref[...] = x_ref[...] + g_ref[...] * y_ref[...]

    row = pl.BlockSpec((bs, n), lambda i: (i, 0))
    vec = pl.BlockSpec((1, n), lambda i: (0, 0))
    return pl.pallas_call(
        body, name=name, grid=(s // bs,), in_specs=[row, row, vec], out_specs=row,
        out_shape=jax.ShapeDtypeStruct((s, n), F32), compiler_params=_params("parallel"),
    )(x, y, gate)


def _resid_bwd(dx, y, gate, *, name):
    s, n = dx.shape
    bs = _blk(s, ROW_BLOCK)

    def body(dx_ref, y_ref, g_ref, dy_ref, dg_ref):
        dxv = dx_ref[...]
        dy_ref[...] = (g_ref[...] * dxv).astype(BF16)

        @pl.when(pl.program_id(0) == 0)
        def _():
            dg_ref[...] = jnp.zeros_like(dg_ref)

        dg_ref[...] += jnp.sum(dxv * y_ref[...], axis=0, keepdims=True)

    row = pl.BlockSpec((bs, n), lambda i: (i, 0))
    vec = pl.BlockSpec((1, n), lambda i: (0, 0))
    return pl.pallas_call(
        body, name=name, grid=(s // bs,), in_specs=[row, row, vec], out_specs=[row, vec],
        out_shape=[jax.ShapeDtypeStruct((s, n), BF16), jax.ShapeDtypeStruct((1, n), F32)],
        compiler_params=_params("arbitrary"),
    )(dx, y, gate)


def _final_loss(x, g, target, *, name):
    s, n = x.shape
    bs = _blk(s, ROW_BLOCK)

    def body(x_ref, g_ref, t_ref, loss_ref, dx_ref, dg_ref):
        xv = x_ref[...]
        r = lax.rsqrt(jnp.mean(xv * xv, axis=-1, keepdims=True) + NORM_EPS)
        xn = xv * r
        gv = g_ref[...]
        err = xn * gv - t_ref[...]
        dout = err * (1.0 / n)
        gg = dout * gv
        dx_ref[...] = r * (gg - xn * jnp.mean(gg * xn, axis=-1, keepdims=True))

        @pl.when(pl.program_id(0) == 0)
        def _():
            loss_ref[...] = jnp.zeros_like(loss_ref)
            dg_ref[...] = jnp.zeros_like(dg_ref)

        part = jnp.sum(jnp.sum(err * err, axis=-1, keepdims=True), axis=0, keepdims=True) * (0.5 / n)
        loss_ref[...] += jnp.broadcast_to(part, loss_ref.shape)
        dg_ref[...] += jnp.sum(dout * xn, axis=0, keepdims=True)

    row = pl.BlockSpec((bs, n), lambda i: (i, 0))
    vec = pl.BlockSpec((1, n), lambda i: (0, 0))
    return pl.pallas_call(
        body, name=name, grid=(s // bs,), in_specs=[row, vec, row],
        out_specs=[pl.BlockSpec((1, LANES), lambda i: (0, 0)), row, vec],
        out_shape=[jax.ShapeDtypeStruct((1, LANES), F32), jax.ShapeDtypeStruct((s, n), F32), jax.ShapeDtypeStruct((1, n), F32)],
        compiler_params=_params("arbitrary"),
    )(x, g, target)


def _causal_keep(n):
    row = lax.broadcasted_iota(jnp.int32, (n, n), 0)
    col = lax.broadcasted_iota(jnp.int32, (n, n), 1)
    return col <= row


def _attn_fwd(q, k, v, kb_row, *, scale, name):
    h, s, dq = q.shape
    dv = v.shape[-1]
    blk = _blk(s, ATT_BLOCK)
    nb = s // blk
    has_bias = kb_row is not None

    def body(*refs):
        if has_bias:
            q_ref, k_ref, v_ref, kb_ref, o_ref, lse_ref = refs
        else:
            q_ref, k_ref, v_ref, o_ref, lse_ref = refs
        i = pl.program_id(1)
        qv = q_ref[0]

        def step(j, carry, diag):
            m, l, acc = carry
            start = pl.multiple_of(j * blk, blk)
            kj = k_ref[0, pl.ds(start, blk), :]
            vj = v_ref[0, pl.ds(start, blk), :]
            sc = lax.dot_general(qv, kj, (((1,), (1,)), ((), ())), preferred_element_type=F32) * scale
            if has_bias:
                sc = sc + kb_ref[0, j]
            if diag:
                sc = jnp.where(_causal_keep(blk), sc, -1e30)
            m_new = jnp.maximum(m, jnp.max(sc, axis=-1, keepdims=True))
            alpha = jnp.exp(m - m_new)
            p = jnp.exp(sc - m_new)
            l = alpha * l + jnp.sum(p, axis=-1, keepdims=True)
            acc = alpha * acc + jnp.dot(p.astype(BF16), vj, preferred_element_type=F32)
            return m_new, l, acc

        init = (jnp.full((blk, 1), -1e30, F32), jnp.zeros((blk, 1), F32), jnp.zeros((blk, dv), F32))
        carry = lax.fori_loop(0, i, lambda j, c: step(j, c, False), init)
        m, l, acc = step(i, carry, True)
        o_ref[0] = acc / l
        lse_ref[0] = m + jnp.log(l)

    qs = pl.BlockSpec((1, blk, dq), lambda hh, i: (hh, i, 0))
    ks = pl.BlockSpec((1, s, dq), lambda hh, i: (hh, 0, 0))
    vs = pl.BlockSpec((1, s, dv), lambda hh, i: (hh, 0, 0))
    bs_ = pl.BlockSpec((1, nb, 1, blk), lambda hh, i: (hh, 0, 0, 0))
    ins = [q, k, v] + ([kb_row] if has_bias else [])
    return pl.pallas_call(
        body, name=name, grid=(h, nb),
        in_specs=[qs, ks, vs] + ([bs_] if has_bias else []),
        out_specs=[pl.BlockSpec((1, blk, dv), lambda hh, i: (hh, i, 0)), pl.BlockSpec((1, blk, 1), lambda hh, i: (hh, i, 0))],
        out_shape=[jax.ShapeDtypeStruct((h, s, dv), F32), jax.ShapeDtypeStruct((h, s, 1), F32)],
        compiler_params=_params("parallel", "parallel"),
    )(*ins)


def _attn_bwd_dq(q, k, v, kb_row, o, do, lse, *, scale, name):
    h, s, dq = q.shape
    dv = v.shape[-1]
    blk = _blk(s, ATT_BLOCK)
    nb = s // blk
    has_bias = kb_row is not None

    def body(*refs):
        if has_bias:
            q_ref, k_ref, v_ref, kb_ref, o_ref, do_ref, lse_ref, dq_ref, dl_ref, dr_ref = refs
        else:
            q_ref, k_ref, v_ref, o_ref, do_ref, lse_ref, dq_ref, dl_ref = refs
        i = pl.program_id(1)
        qv = q_ref[0]
        dov = do_ref[0]
        delta = jnp.sum(dov * o_ref[0], axis=-1, keepdims=True)
        dl_ref[0] = delta
        dob = dov.astype(BF16)
        lse_v = lse_ref[0]

        def step(j, carry, diag):
            acc, rsum = carry
            start = pl.multiple_of(j * blk, blk)
            kj = k_ref[0, pl.ds(start, blk), :]
            vj = v_ref[0, pl.ds(start, blk), :]
            sc = lax.dot_general(qv, kj, (((1,), (1,)), ((), ())), preferred_element_type=F32) * scale
            if has_bias:
                sc = sc + kb_ref[0, j]
            if diag:
                sc = jnp.where(_causal_keep(blk), sc, -1e30)
            p = jnp.exp(sc - lse_v)
            dp = lax.dot_general(dob, vj, (((1,), (1,)), ((), ())), preferred_element_type=F32)
            ds = p * (dp - delta)
            if has_bias:
                rsum = rsum + jnp.sum(ds, axis=-1, keepdims=True)
            return acc + jnp.dot(ds.astype(BF16), kj, preferred_element_type=F32), rsum

        init = (jnp.zeros((blk, dq), F32), jnp.zeros((blk, 1), F32))
        carry = lax.fori_loop(0, i, lambda j, c: step(j, c, False), init)
        acc, rsum = step(i, carry, True)
        dq_ref[0] = acc * scale
        if has_bias:
            dr_ref[0] = rsum

    qs = pl.BlockSpec((1, blk, dq), lambda hh, i: (hh, i, 0))
    ks = pl.BlockSpec((1, s, dq), lambda hh, i: (hh, 0, 0))
    vs = pl.BlockSpec((1, s, dv), lambda hh, i: (hh, 0, 0))
    os_ = pl.BlockSpec((1, blk, dv), lambda hh, i: (hh, i, 0))
    col = pl.BlockSpec((1, blk, 1), lambda hh, i: (hh, i, 0))
    bs_ = pl.BlockSpec((1, nb, 1, blk), lambda hh, i: (hh, 0, 0, 0))
    ins = [q, k, v] + ([kb_row] if has_bias else []) + [o, do, lse]
    return pl.pallas_call(
        body, name=name, grid=(h, nb),
        in_specs=[qs, ks, vs] + ([bs_] if has_bias else []) + [os_, os_, col],
        out_specs=[qs, col] + ([col] if has_bias else []),
        out_shape=[jax.ShapeDtypeStruct((h, s, dq), F32)] + [jax.ShapeDtypeStruct((h, s, 1), F32)] * (2 if has_bias else 1),
        compiler_params=_params("parallel", "parallel"),
    )(*ins)


def _attn_bwd_dkv(q, k, v, kb_col, do, lse_row, delta_row, *, scale, name):
    h, s, dq = q.shape
    dv = v.shape[-1]
    blk = _blk(s, ATT_BLOCK)
    nb = s // blk
    has_bias = kb_col is not None

    def body(*refs):
        if has_bias:
            q_ref, k_ref, v_ref, kb_ref, do_ref, lse_ref, dl_ref, dk_ref, dv_ref, db_ref = refs
        else:
            q_ref, k_ref, v_ref, do_ref, lse_ref, dl_ref, dk_ref, dv_ref, db_ref = refs
        j = pl.program_id(1)
        kj = k_ref[0]
        vj = v_ref[0]

        def step(i, carry, diag):
            dk, dvv, db = carry
            start = pl.multiple_of(i * blk, blk)
            qi = q_ref[0, pl.ds(start, blk), :]
            doi = do_ref[0, pl.ds(start, blk), :].astype(BF16)
            st = lax.dot_general(kj, qi, (((1,), (1,)), ((), ())), preferred_element_type=F32) * scale
            if has_bias:
                st = st + kb_ref[0]
            if diag:
                row = lax.broadcasted_iota(jnp.int32, (blk, blk), 0)
                colq = lax.broadcasted_iota(jnp.int32, (blk, blk), 1)
                st = jnp.where(row <= colq, st, -1e30)
            pt = jnp.exp(st - lse_ref[0, i])
            dvv = dvv + jnp.dot(pt.astype(BF16), doi, preferred_element_type=F32)
            dpt = lax.dot_general(vj, doi, (((1,), (1,)), ((), ())), preferred_element_type=F32)
            dst = pt * (dpt - dl_ref[0, i])
            dk = dk + jnp.dot(dst.astype(BF16), qi, preferred_element_type=F32)
            db = db + jnp.sum(dst, axis=-1, keepdims=True)
            return dk, dvv, db

        init = (jnp.zeros((blk, dq), F32), jnp.zeros((blk, dv), F32), jnp.zeros((blk, 1), F32))
        carry = step(j, init, True)
        dk, dvv, db = lax.fori_loop(j + 1, nb, lambda i, c: step(i, c, False), carry)
        dk_ref[0] = dk * scale
        dv_ref[0] = dvv
        db_ref[0] = db

    qs = pl.BlockSpec((1, s, dq), lambda hh, j: (hh, 0, 0))
    ks = pl.BlockSpec((1, blk, dq), lambda hh, j: (hh, j, 0))
    vs = pl.BlockSpec((1, blk, dv), lambda hh, j: (hh, j, 0))
    dos = pl.BlockSpec((1, s, dv), lambda hh, j: (hh, 0, 0))
    col = pl.BlockSpec((1, blk, 1), lambda hh, j: (hh, j, 0))
    rows = pl.BlockSpec((1, nb, 1, blk), lambda hh, j: (hh, 0, 0, 0))
    ins = [q, k, v] + ([kb_col] if has_bias else []) + [do, lse_row, delta_row]
    return pl.pallas_call(
        body, name=name, grid=(h, nb),
        in_specs=[qs, ks, vs] + ([col] if has_bias else []) + [dos, rows, rows],
        out_specs=[ks, vs, col],
        out_shape=[jax.ShapeDtypeStruct((h, s, dq), F32), jax.ShapeDtypeStruct((h, s, dv), F32), jax.ShapeDtypeStruct((h, s, 1), F32)],
        compiler_params=_params("parallel", "parallel"),
    )(*ins)


def _attention(q, k, v, kb, *, scale, name):
    h, s, _ = q.shape
    blk = _blk(s, ATT_BLOCK)
    kb_row = None if kb is None else kb.reshape(h, s // blk, 1, blk)
    o, lse = _attn_fwd(q, k, v, kb_row, scale=scale, name=name + "_fwd")
    return o, (q, k, v, kb, o, lse)


def _attention_bwd(res, do, *, scale, name):
    q, k, v, kb, o, lse = res
    h, s, _ = q.shape
    blk = _blk(s, ATT_BLOCK)
    kb_row = None if kb is None else kb.reshape(h, s // blk, 1, blk)
    kb_col = None if kb is None else kb.reshape(h, s, 1)
    outs = _attn_bwd_dq(q, k, v, kb_row, o, do, lse, scale=scale, name=name + "_dq")
    dq, delta = outs[0], outs[1]
    lse_row = lse.reshape(h, s // blk, 1, blk)
    delta_row = delta.reshape(h, s // blk, 1, blk)
    dk, dv, dcol = _attn_bwd_dkv(q, k, v, kb_col, do, lse_row, delta_row, scale=scale, name=name + "_dkv")
    if kb is None:
        return dq, dk, dv, None
    return dq, dk, dv, (outs[2] - dcol).reshape(h, s)


def _fox_gate_fwd(fl, bf, *, name):
    s, n = fl.shape
    bs = _blk(s, ROW_BLOCK)

    def body(fl_ref, bf_ref, cum_ref, carry_ref):
        @pl.when(pl.program_id(0) == 0)
        def _():
            carry_ref[...] = jnp.zeros_like(carry_ref)

        z = fl_ref[...] + bf_ref[...]
        lf = jnp.minimum(z, 0.0) - jnp.log1p(jnp.exp(-jnp.abs(z)))
        row = lax.broadcasted_iota(jnp.int32, (bs, bs), 0)
        col = lax.broadcasted_iota(jnp.int32, (bs, bs), 1)
        tri = (col <= row).astype(F32)
        cum_ref[...] = jnp.dot(tri, lf, preferred_element_type=F32, precision=HIGHEST) + carry_ref[...]
        carry_ref[...] += jnp.sum(lf, axis=0, keepdims=True)

    return pl.pallas_call(
        body, name=name, grid=(s // bs,),
        in_specs=[pl.BlockSpec((bs, n), lambda i: (i, 0)), pl.BlockSpec((1, n), lambda i: (0, 0))],
        out_specs=pl.BlockSpec((bs, n), lambda i: (i, 0)),
        out_shape=jax.ShapeDtypeStruct((s, n), F32), scratch_shapes=[pltpu.VMEM((1, n), F32)],
        compiler_params=_params("arbitrary"),
    )(fl, bf)


def _fox_gate_bwd(fl, bf, dcum, *, name):
    s, n = fl.shape
    bs = _blk(s, ROW_BLOCK)
    nb = s // bs

    def body(fl_ref, bf_ref, dc_ref, dz_ref, dbf_ref, carry_ref):
        @pl.when(pl.program_id(0) == 0)
        def _():
            carry_ref[...] = jnp.zeros_like(carry_ref)
            dbf_ref[...] = jnp.zeros_like(dbf_ref)

        dc = dc_ref[...]
        row = lax.broadcasted_iota(jnp.int32, (bs, bs), 0)
        col = lax.broadcasted_iota(jnp.int32, (bs, bs), 1)
        tri = (col >= row).astype(F32)
        dlf = jnp.dot(tri, dc, preferred_element_type=F32, precision=HIGHEST) + carry_ref[...]
        carry_ref[...] += jnp.sum(dc, axis=0, keepdims=True)
        z = fl_ref[...] + bf_ref[...]
        dz = dlf / (1.0 + jnp.exp(z))
        dz_ref[...] = dz
        dbf_ref[...] += jnp.sum(dz, axis=0, keepdims=True)

    rev = pl.BlockSpec((bs, n), lambda i: (nb - 1 - i, 0))
    vec = pl.BlockSpec((1, n), lambda i: (0, 0))
    return pl.pallas_call(
        body, name=name, grid=(nb,), in_specs=[rev, vec, rev], out_specs=[rev, vec],
        out_shape=[jax.ShapeDtypeStruct((s, n), F32), jax.ShapeDtypeStruct((1, n), F32)],
        scratch_shapes=[pltpu.VMEM((1, n), F32)], compiler_params=_params("arbitrary"),
    )(fl, bf, dcum)


def _rope(x1, x2, cos, sin, *, negate, name):
    s, n = x1.shape
    bs = _blk(s, ROW_BLOCK)

    def body(a_ref, b_ref, c_ref, s_ref, o1_ref, o2_ref):
        a, b, cv = a_ref[...], b_ref[...], c_ref[...]
        sv = -s_ref[...] if negate else s_ref[...]
        o1_ref[...] = a * cv - b * sv
        o2_ref[...] = b * cv + a * sv

    row = pl.BlockSpec((bs, n), lambda i: (i, 0))
    return pl.pallas_call(
        body, name=name, grid=(s // bs,), in_specs=[row] * 4, out_specs=[row, row],
        out_shape=[jax.ShapeDtypeStruct((s, n), F32)] * 2, compiler_params=_params("parallel"),
    )(x1, x2, cos, sin)


def _head_sum(x, *, name):
    h, s, n = x.shape
    bs = _blk(s, ROW_BLOCK)

    def body(x_ref, o_ref):
        o_ref[...] = jnp.sum(x_ref[...], axis=0)

    return pl.pallas_call(
        body, name=name, grid=(s // bs,), in_specs=[pl.BlockSpec((h, bs, n), lambda i: (0, i, 0))],
        out_specs=pl.BlockSpec((bs, n), lambda i: (i, 0)),
        out_shape=jax.ShapeDtypeStruct((s, n), F32), compiler_params=_params("parallel"),
    )(x)


def _shift_down(x, k):
    return pltpu.roll(x, k, 0)


def _conv_rows(ext, w_ref, b_ref, rows):
    y = b_ref[...] + w_ref[0:1, :] * _shift_down(ext, 2) + w_ref[1:2, :] * _shift_down(ext, 1) + w_ref[2:3, :] * ext
    return y[8:8 + rows]


def _conv_gate_fwd(u, cw, cb, *, name):
    s, f2 = u.shape
    f = f2 // 2
    nf = f // LANES
    r = _blk(s, CONV_ROWS)
    r8 = r // 8

    def body(ug_ref, ugp_ref, uv_ref, uvp_ref, wg_ref, wv_ref, bg_ref, bv_ref, o_ref):
        first = pl.program_id(1) == 0

        def conv(cur_ref, prev_ref, w_ref, b_ref):
            prev = jnp.where(first, 0.0, prev_ref[...])
            return _conv_rows(jnp.concatenate([prev, cur_ref[...]], axis=0), w_ref, b_ref, r)

        yg = conv(ug_ref, ugp_ref, wg_ref, bg_ref)
        yv = conv(uv_ref, uvp_ref, wv_ref, bv_ref)
        o_ref[...] = (yg * jax.nn.sigmoid(yg) * yv).astype(BF16)

    def cur(off):
        return pl.BlockSpec((r, LANES), lambda c, i: (i, c + off))

    def prev(off):
        return pl.BlockSpec((8, LANES), lambda c, i: (jnp.maximum(i * r8 - 1, 0), c + off))

    def wspec(rows, off):
        return pl.BlockSpec((rows, LANES), lambda c, i: (0, c + off))

    return pl.pallas_call(
        body, name=name, grid=(nf, s // r),
        in_specs=[cur(0), prev(0), cur(nf), prev(nf), wspec(3, 0), wspec(3, nf), wspec(1, 0), wspec(1, nf)],
        out_specs=pl.BlockSpec((r, LANES), lambda c, i: (i, c)),
        out_shape=jax.ShapeDtypeStruct((s, f), BF16), compiler_params=_params("parallel", "parallel"),
    )(u, u, u, u, cw, cw, cb, cb)


def _conv_gate_bwd(u, cw, cb, dg, *, name):
    s, f2 = u.shape
    f = f2 // 2
    nf = f // LANES
    r = _blk(s, CONV_ROWS)
    r8 = r // 8
    nr = s // r

    def body(ug_ref, ugp_ref, ugn_ref, uv_ref, uvp_ref, uvn_ref, wg_ref, wv_ref, bg_ref, bv_ref, dg_ref, dgn_ref,
             dug_ref, duv_ref, dwg_ref, dwv_ref, dbg_ref, dbv_ref):
        i = pl.program_id(1)
        first, last = i == 0, i == nr - 1

        def ext_of(cur_ref, prev_ref, next_ref):
            prev = jnp.where(first, 0.0, prev_ref[...])
            return jnp.concatenate([prev, cur_ref[...], next_ref[...]], axis=0)

        eg, ev = ext_of(ug_ref, ugp_ref, ugn_ref), ext_of(uv_ref, uvp_ref, uvn_ref)
        yg = _conv_rows(eg, wg_ref, bg_ref, r + 8)
        yv = _conv_rows(ev, wv_ref, bv_ref, r + 8)
        dgn = jnp.where(last, 0.0, dgn_ref[...])
        dgx = jnp.concatenate([dg_ref[...], dgn], axis=0)
        sg = jax.nn.sigmoid(yg)
        dyg = dgx * yv * (sg * (1.0 + yg * (1.0 - sg)))
        dyv = dgx * (yg * sg)

        @pl.when(i == 0)
        def _():
            for ref in (dwg_ref, dwv_ref, dbg_ref, dbv_ref):
                ref[...] = jnp.zeros_like(ref)

        def grads(dy, ext, w_ref, du_ref, dw_ref, db_ref):
            n = r + 8
            du = w_ref[2:3, :] * dy + w_ref[1:2, :] * pltpu.roll(dy, n - 1, 0) + w_ref[0:1, :] * pltpu.roll(dy, n - 2, 0)
            du_ref[...] = du[0:r].astype(BF16)
            dyc = dy[0:r]
            db_ref[...] += jnp.sum(dyc, axis=0, keepdims=True)
            ext_c = ext[0:r + 8]
            dw_ref[0:1, :] += jnp.sum(dyc * _shift_down(ext_c, 2)[8:], axis=0, keepdims=True)
            dw_ref[1:2, :] += jnp.sum(dyc * _shift_down(ext_c, 1)[8:], axis=0, keepdims=True)
            dw_ref[2:3, :] += jnp.sum(dyc * ext_c[8:], axis=0, keepdims=True)

        grads(dyg, eg, wg_ref, dug_ref, dwg_ref, dbg_ref)
        grads(dyv, ev, wv_ref, duv_ref, dwv_ref, dbv_ref)

    def cur(off):
        return pl.BlockSpec((r, LANES), lambda c, i: (i, c + off))

    def prev(off):
        return pl.BlockSpec((8, LANES), lambda c, i: (jnp.maximum(i * r8 - 1, 0), c + off))

    def nxt(off):
        return pl.BlockSpec((8, LANES), lambda c, i: (jnp.minimum((i + 1) * r8, s // 8 - 1), c + off))

    def wspec(rows, off):
        return pl.BlockSpec((rows, LANES), lambda c, i: (0, c + off))

    outs = pl.pallas_call(
        body, name=name, grid=(nf, nr),
        in_specs=[cur(0), prev(0), nxt(0), cur(nf), prev(nf), nxt(nf), wspec(3, 0), wspec(3, nf), wspec(1, 0), wspec(1, nf),
                  cur(0), nxt(0)],
        out_specs=[cur(0), cur(0), wspec(3, 0), wspec(3, 0), wspec(1, 0), wspec(1, 0)],
        out_shape=[jax.ShapeDtypeStruct((s, f), BF16), jax.ShapeDtypeStruct((s, f), BF16),
                   jax.ShapeDtypeStruct((3, f), F32), jax.ShapeDtypeStruct((3, f), F32),
                   jax.ShapeDtypeStruct((1, f), F32), jax.ShapeDtypeStruct((1, f), F32)],
        compiler_params=_params("parallel", "arbitrary"),
    )(u, u, u, u, u, u, cw, cw, cb, cb, dg, dg)
    dug, duv, dwg, dwv, dbg, dbv = outs
    return jnp.concatenate([dug, duv], axis=1), jnp.concatenate([dwg, dwv], axis=1), jnp.concatenate([dbg, dbv], axis=1)


def _adamw(w, g, m, v, *, name):
    shape = w.shape
    cols = shape[-1]
    rows = w.size // cols
    w2, g2, m2, v2 = (t.reshape(rows, cols) for t in (w, g, m, v))
    br = _row_blk(rows, ROW_BLOCK)

    def body(w_ref, g_ref, m_ref, v_ref, d_ref, nm_ref, nv_ref):
        gv = g_ref[...]
        nm = ADAM_B1 * m_ref[...] + (1.0 - ADAM_B1) * gv
        nv = ADAM_B2 * v_ref[...] + (1.0 - ADAM_B2) * (gv * gv)
        m_hat = nm / (1.0 - ADAM_B1 ** ADAM_STEP)
        v_hat = nv / (1.0 - ADAM_B2 ** ADAM_STEP)
        d_ref[...] = -ADAM_LR * (m_hat / (jnp.sqrt(v_hat) + ADAM_EPS) + ADAM_WD * w_ref[...])
        nm_ref[...] = nm
        nv_ref[...] = nv

    spec = pl.BlockSpec((br, cols), lambda i: (i, 0))
    outs = pl.pallas_call(
        body, name=name, grid=(rows // br,), in_specs=[spec] * 4, out_specs=[spec] * 3,
        out_shape=[jax.ShapeDtypeStruct((rows, cols), F32)] * 3, compiler_params=_params("parallel"),
    )(w2, g2, m2, v2)
    return tuple(t.reshape(shape) for t in outs)


def _exchange(x, *, same_src, name):
    slab = x.shape if same_src else x.shape[1:]

    def body(x_ref, o_ref, send_sems, recv_sems, loc_sem):
        ix, iy, ic = lax.axis_index("x"), lax.axis_index("y"), lax.axis_index("c")
        me = 4 * ix + 2 * iy + ic

        def src(p):
            return x_ref if same_src else x_ref.at[p]

        local = pltpu.make_async_copy(src(me), o_ref.at[me], loc_sem)
        local.start()
        sends, recvs = [], []
        for k in (1, 2, 4, 3, 5, 6, 7):
            px = 1 - ix if k & 4 else ix
            py = 1 - iy if k & 2 else iy
            pc = 1 - ic if k & 1 else ic
            p = 4 * px + 2 * py + pc
            sends.append(pltpu.make_async_remote_copy(
                src_ref=src(p), dst_ref=o_ref.at[me], send_sem=send_sems.at[k - 1], recv_sem=recv_sems.at[k - 1],
                device_id=(px, py, pc), device_id_type=pl.DeviceIdType.MESH))
            recvs.append(pltpu.make_async_remote_copy(
                src_ref=src(p), dst_ref=o_ref.at[p], send_sem=send_sems.at[k - 1], recv_sem=recv_sems.at[k - 1],
                device_id=(px, py, pc), device_id_type=pl.DeviceIdType.MESH))
        for cp in sends:
            cp.start()
        for cp in recvs:
            cp.wait_recv()
        for cp in sends:
            cp.wait_send()
        local.wait()

    return pl.pallas_call(
        body, name=name,
        in_specs=[pl.BlockSpec(memory_space=pl.ANY)], out_specs=pl.BlockSpec(memory_space=pl.ANY),
        out_shape=jax.ShapeDtypeStruct((N_DEV,) + tuple(slab), x.dtype),
        scratch_shapes=[pltpu.SemaphoreType.DMA((N_DEV - 1,)), pltpu.SemaphoreType.DMA((N_DEV - 1,)), pltpu.SemaphoreType.DMA],
        compiler_params=pltpu.CompilerParams(has_side_effects=True, vmem_limit_bytes=VMEM_LIMIT_BYTES),
    )(x)


def _sum_slabs(x, *, name):
    n, r, c = x.shape
    br = _row_blk(r, ROW_BLOCK)

    def body(x_ref, o_ref):
        acc = x_ref[0]
        for p in range(1, n):
            acc = acc + x_ref[p]
        o_ref[...] = acc

    return pl.pallas_call(
        body, name=name, grid=(r // br,), in_specs=[pl.BlockSpec((n, br, c), lambda i: (0, i, 0))],
        out_specs=pl.BlockSpec((br, c), lambda i: (i, 0)),
        out_shape=jax.ShapeDtypeStruct((r, c), F32), compiler_params=_params("parallel"),
    )(x)


def _to_slab(flat, row_mult):
    flat = _pad_axis(flat, flat.ndim - 1, LANES * row_mult)
    return flat.reshape(flat.shape[:-1] + (flat.shape[-1] // LANES, LANES))


class _Packer:
    def __init__(self):
        self.items = []
        self.size = 0

    def add(self, name, shape):
        n = 1
        for d in shape:
            n *= d
        self.items.append((name, tuple(shape), self.size, n))
        self.size += n

    def pack(self, arrays, lead=()):
        return jnp.concatenate([arrays[name].reshape(lead + (n,)) for name, _, _, n in self.items], axis=-1)

    def unpack(self, flat):
        lead = flat.shape[:-1]
        return {name: flat[..., off:off + n].reshape(lead + shape) for name, shape, off, n in self.items}


_BIG = {
    "fox_w_in": ((1, 1024, 386), 2), "fox_w_o": ((1, 128, 1024), 1), "mla_w_a": ((1, 128, 672), 1),
    "mla_w_uq": ((1, 384, 192), 2), "mla_w_ukv": ((1, 256, 256), 2), "mla_w_o": ((1, 128, 1024), 1),
    "ffn_w_in": ((2, 1024, 704), 2), "ffn_w_out": ((2, 352, 1024), 1),
}
_SMALL = {"mla_g_q": ((1, 48), 1), "mla_g_kv": ((1, 32), 1), "ffn_conv_w": ((2, 3, 704), 2)}
_REPL = {"fox_b_f": (1, 16), "ffn_conv_b": (2, 5632), "final_g": (1024,)}


def _gathered_to_full(g, axis):
    full = jnp.moveaxis(g, 0, axis)
    shape = list(full.shape)
    shape[axis:axis + 2] = [shape[axis] * shape[axis + 1]]
    return full.reshape(shape)


def _full_to_chunks(full, axis):
    shape = list(full.shape)
    shape[axis:axis + 1] = [N_DEV, shape[axis] // N_DEV]
    return jnp.moveaxis(full.reshape(shape), axis, 0)


def _heads(t, d):
    s = t.shape[0]
    return t.reshape(s, N_HEADS, d).transpose(1, 0, 2)


def _unheads(t):
    h, s, d = t.shape
    return t.transpose(1, 0, 2).reshape(s, h * d)


def kernel(x, c, ada_w, ada_b, fox_w_in, fox_b_f, fox_w_o, mla_w_a, mla_g_q, mla_g_kv, mla_w_uq, mla_w_ukv, mla_w_o, ffn_w_in, ffn_conv_w, ffn_conv_b, ffn_w_out, final_g, loss_target, m_ada_w, m_ada_b, m_fox_w_in, m_fox_b_f, m_fox_w_o, m_mla_w_a, m_mla_g_q, m_mla_g_kv, m_mla_w_uq, m_mla_w_ukv, m_mla_w_o, m_ffn_w_in, m_ffn_conv_w, m_ffn_conv_b, m_ffn_w_out, m_final_g, v_ada_w, v_ada_b, v_fox_w_in, v_fox_b_f, v_fox_w_o, v_mla_w_a, v_mla_g_q, v_mla_g_kv, v_mla_w_uq, v_mla_w_ukv, v_mla_w_o, v_ffn_w_in, v_ffn_conv_w, v_ffn_conv_b, v_ffn_w_out, v_final_g):
    weights = dict(ada_w=ada_w, ada_b=ada_b, fox_w_in=fox_w_in, fox_b_f=fox_b_f, fox_w_o=fox_w_o, mla_w_a=mla_w_a,
                   mla_g_q=mla_g_q, mla_g_kv=mla_g_kv, mla_w_uq=mla_w_uq, mla_w_ukv=mla_w_ukv, mla_w_o=mla_w_o,
                   ffn_w_in=ffn_w_in, ffn_conv_w=ffn_conv_w, ffn_conv_b=ffn_conv_b, ffn_w_out=ffn_w_out, final_g=final_g)
    mom_m = dict(ada_w=m_ada_w, ada_b=m_ada_b, fox_w_in=m_fox_w_in, fox_b_f=m_fox_b_f, fox_w_o=m_fox_w_o, mla_w_a=m_mla_w_a,
                 mla_g_q=m_mla_g_q, mla_g_kv=m_mla_g_kv, mla_w_uq=m_mla_w_uq, mla_w_ukv=m_mla_w_ukv, mla_w_o=m_mla_w_o,
                 ffn_w_in=m_ffn_w_in, ffn_conv_w=m_ffn_conv_w, ffn_conv_b=m_ffn_conv_b, ffn_w_out=m_ffn_w_out, final_g=m_final_g)
    mom_v = dict(ada_w=v_ada_w, ada_b=v_ada_b, fox_w_in=v_fox_w_in, fox_b_f=v_fox_b_f, fox_w_o=v_fox_w_o, mla_w_a=v_mla_w_a,
                 mla_g_q=v_mla_g_q, mla_g_kv=v_mla_g_kv, mla_w_uq=v_mla_w_uq, mla_w_ukv=v_mla_w_ukv, mla_w_o=v_mla_w_o,
                 ffn_w_in=v_ffn_w_in, ffn_conv_w=v_ffn_conv_w, ffn_conv_b=v_ffn_conv_b, ffn_w_out=v_ffn_w_out, final_g=v_final_g)
    order = list(weights)
    x0 = x[0]
    target = loss_target[0]
    s = x0.shape[0]
    d = D_MODEL
    cols = ada_w.shape[-1]

    small = _Packer()
    small.add("c", (1, d))
    for name, (shape, _) in _SMALL.items():
        small.add(name, shape)
    small_flat = small.pack(dict(c=c, **{n: weights[n] for n in _SMALL}))
    small_all = _exchange(_to_slab(small_flat, 8), same_src=True, name="gather_small")
    small_all = small.unpack(small_all.reshape(N_DEV, -1))
    c_all = small_all["c"].reshape(N_DEV, d)
    g_q = _gathered_to_full(small_all["mla_g_q"], 1)
    g_kv = _gathered_to_full(small_all["mla_g_kv"], 1)
    conv_w = _gathered_to_full(small_all["ffn_conv_w"], 2)

    c_pad = _pad_axis(c_all, 0, LANES)
    silu_c = _silu(c_pad, name="silu_c")
    w_ada = ada_w.reshape(4, d, cols)
    b_ada = ada_b.reshape(4, 1, cols)
    mods = [_matmul(silu_c, w_ada[i], name=f"ada_mod{i}")[:N_DEV] + b_ada[i] for i in range(4)]
    mod_send = jnp.stack(mods, axis=1).reshape(N_DEV, 4 * cols)
    mod_recv = _exchange(_to_slab(mod_send, 8), same_src=False, name="scatter_mod")
    mod = mod_recv.reshape(N_DEV, -1)[:, :4 * cols].reshape(N_DEV, 4, cols)
    mod = mod.transpose(1, 0, 2).reshape(4, 3 * d)
    shift = [mod[i:i + 1, 0:d] for i in range(4)]
    scale = [mod[i:i + 1, d:2 * d] for i in range(4)]
    gate = [mod[i:i + 1, 2 * d:3 * d] for i in range(4)]

    big = _Packer()
    for name, (shape, _) in _BIG.items():
        big.add(name, shape)
    big_flat = big.pack({n: weights[n].astype(BF16) for n in _BIG})
    big_all = _exchange(_to_slab(big_flat, ROW_BLOCK), same_src=True, name="gather_weights")
    big_all = big.unpack(big_all.reshape(N_DEV, -1))
    wfull = {n: _gathered_to_full(big_all[n], _BIG[n][1]) for n in _BIG}

    w_fox_in = _pad_axis(wfull["fox_w_in"][0], 1, LANES)
    w_fox_o = wfull["fox_w_o"][0]
    w_a = _pad_axis(wfull["mla_w_a"][0], 1, LANES)
    wq = wfull["mla_w_uq"][0].reshape(MLA_Q_RANK, N_HEADS, HEAD_DIM + MLA_ROPE_DIM)
    w_uq = jnp.concatenate([wq[:, :, :HEAD_DIM].reshape(MLA_Q_RANK, -1),
                            wq[:, :, HEAD_DIM:HEAD_DIM + MLA_ROPE_HALF].reshape(MLA_Q_RANK, -1),
                            wq[:, :, HEAD_DIM + MLA_ROPE_HALF:].reshape(MLA_Q_RANK, -1)], axis=1)
    wkv = wfull["mla_w_ukv"][0].reshape(MLA_KV_RANK, N_HEADS, 2 * HEAD_DIM)
    w_ukv = jnp.concatenate([wkv[:, :, :HEAD_DIM].reshape(MLA_KV_RANK, -1), wkv[:, :, HEAD_DIM:].reshape(MLA_KV_RANK, -1)], axis=1)
    w_mla_o = wfull["mla_w_o"][0]
    w_ffn_in = wfull["ffn_w_in"]
    w_ffn_out = wfull["ffn_w_out"]
    conv_b = ffn_conv_b

    pos = jnp.arange(s, dtype=F32)
    inv_freq = ROPE_BASE ** (-jnp.arange(0, MLA_ROPE_DIM, 2, dtype=F32) / MLA_ROPE_DIM)
    ang = pos[:, None] * inv_freq[None, :]
    cos16, sin16 = jnp.cos(ang), jnp.sin(ang)
    cos_q, sin_q = jnp.tile(cos16, (1, N_HEADS)), jnp.tile(sin16, (1, N_HEADS))

    zero_d = jnp.zeros((1, d), F32)
    fox_scale = HEAD_DIM ** -0.5
    mla_scale = (HEAD_DIM + MLA_ROPE_DIM) ** -0.5
    nq = N_HEADS * HEAD_DIM

    h0 = _norm_fwd(x0, scale[0], shift[0], plus_one=True, out_dtype=BF16, name="ada_fwd0")
    proj = _matmul(h0, w_fox_in, name="fox_proj")
    fl = proj[:, 3 * nq:3 * nq + N_HEADS]
    cum = _fox_gate_fwd(fl, fox_b_f, name="fox_gate_fwd")
    fq = _heads(proj[:, 0:nq].astype(BF16), HEAD_DIM)
    fk = _heads(proj[:, nq:2 * nq].astype(BF16), HEAD_DIM)
    fv = _heads(proj[:, 2 * nq:3 * nq].astype(BF16), HEAD_DIM)
    fo, fox_res = _attention(fq, fk, fv, -cum.T, scale=fox_scale, name="fox_attn")
    fo2 = _unheads(fo)
    y0 = _matmul(fo2, w_fox_o, name="fox_out")
    x1 = _resid_fwd(x0, y0, gate[0], name="resid_fwd0")

    def ffn_fwd(xin, li, sub):
        hh = _norm_fwd(xin, scale[sub], shift[sub], plus_one=True, out_dtype=BF16, name=f"ada_fwd{sub}")
        u = _matmul(hh, w_ffn_in[li], name=f"ffn_up{li}")
        g = _conv_gate_fwd(u, conv_w[li], conv_b[li:li + 1], name=f"conv_fwd{li}")
        y = _matmul(g, w_ffn_out[li], name=f"ffn_down{li}")
        return _resid_fwd(xin, y, gate[sub], name=f"resid_fwd{sub}"), (hh, u, g, y)

    x2, ffn0_res = ffn_fwd(x1, 0, 1)

    h2 = _norm_fwd(x2, scale[2], shift[2], plus_one=True, out_dtype=BF16, name="ada_fwd2")
    a = _matmul(h2, w_a, name="mla_a")
    a_q, a_kv = a[:, :MLA_Q_RANK], a[:, MLA_Q_RANK:MLA_Q_RANK + MLA_KV_RANK]
    kr1 = a[:, MLA_Q_RANK + MLA_KV_RANK:MLA_Q_RANK + MLA_KV_RANK + MLA_ROPE_HALF]
    kr2 = a[:, MLA_Q_RANK + MLA_KV_RANK + MLA_ROPE_HALF:MLA_Q_RANK + MLA_KV_RANK + MLA_ROPE_DIM]
    cq = _norm_fwd(a_q, g_q, jnp.zeros_like(g_q), plus_one=False, out_dtype=BF16, name="mla_norm_q")
    ckv = _norm_fwd(a_kv, g_kv, jnp.zeros_like(g_kv), plus_one=False, out_dtype=BF16, name="mla_norm_kv")
    qf = _matmul(cq, w_uq, name="mla_uq")
    kvf = _matmul(ckv, w_ukv, out_dtype=BF16, name="mla_ukv")
    qr1, qr2 = _rope(qf[:, nq:nq + 256], qf[:, nq + 256:], cos_q, sin_q, negate=False, name="rope_q")
    kk1, kk2 = _rope(kr1, kr2, cos16, sin16, negate=False, name="rope_k")

    def mla_assemble(nope, r1, r2):
        parts = [nope.reshape(s, N_HEADS, HEAD_DIM)]
        for r in (r1, r2):
            r = r.reshape(s, -1, MLA_ROPE_HALF)
            parts.append(jnp.broadcast_to(r, (s, N_HEADS, MLA_ROPE_HALF)))
        return jnp.concatenate([p.astype(BF16) for p in parts], axis=-1).transpose(1, 0, 2)

    mq = mla_assemble(qf[:, :nq], qr1, qr2)
    mk = mla_assemble(kvf[:, :nq], kk1, kk2)
    mv = _heads(kvf[:, nq:], HEAD_DIM)
    mo, mla_res = _attention(mq, mk, mv, None, scale=mla_scale, name="mla_attn")
    mo2 = _unheads(mo)
    y2 = _matmul(mo2, w_mla_o, name="mla_out")
    x3 = _resid_fwd(x2, y2, gate[2], name="resid_fwd2")

    x4, ffn1_res = ffn_fwd(x3, 1, 3)

    loss_vec, dx4, d_final_g = _final_loss(x4, final_g.reshape(1, d), target, name="final_loss")
    loss = lax.psum(loss_vec[0, 0], ("x", "y", "c"))

    grads = {}
    dmod = [None] * 4

    def ffn_bwd(dx_out, xin, li, sub, res):
        hh, u, g, y = res
        dy, dgate = _resid_bwd(dx_out, y, gate[sub], name=f"resid_bwd{sub}")
        gw_out = _matmul(g, dy, ta=True, name=f"ffn_down_dw{li}")
        dg = _matmul(dy, w_ffn_out[li], tb=True, name=f"ffn_down_dx{li}")
        du, dcw, dcb = _conv_gate_bwd(u, conv_w[li], conv_b[li:li + 1], dg, name=f"conv_bwd{li}")
        gw_in = _matmul(hh, du, ta=True, name=f"ffn_up_dw{li}")
        dh = _matmul(du, w_ffn_in[li], tb=True, out_dtype=BF16, name=f"ffn_up_dx{li}")
        dx_in, dscale, dshift = _norm_bwd(xin, scale[sub], dh, dx_out, plus_one=True, name=f"ada_bwd{sub}")
        dmod[sub] = jnp.concatenate([dshift, dscale, dgate], axis=1)
        return dx_in, gw_in, dcw, dcb, gw_out

    dx3, gw_in1, dcw1, dcb1, gw_out1 = ffn_bwd(dx4, x3, 1, 3, ffn1_res)

    dy2, dgate2 = _resid_bwd(dx3, y2, gate[2], name="resid_bwd2")
    grads["mla_w_o"] = _matmul(mo2, dy2, ta=True, name="mla_out_dw")[None]
    dmo = _heads(_matmul(dy2, w_mla_o, tb=True, name="mla_out_dx"), HEAD_DIM)
    dmq, dmk, dmv, _ = _attention_bwd(mla_res, dmo, scale=mla_scale, name="mla_attn")
    dmq2 = dmq.transpose(1, 0, 2)
    dqr1, dqr2 = _rope(dmq2[:, :, HEAD_DIM:HEAD_DIM + MLA_ROPE_HALF].reshape(s, -1),
                       dmq2[:, :, HEAD_DIM + MLA_ROPE_HALF:].reshape(s, -1), cos_q, sin_q, negate=True, name="rope_q_bwd")
    dqf = jnp.concatenate([dmq2[:, :, :HEAD_DIM].reshape(s, nq), dqr1, dqr2], axis=1)
    g_uq = _matmul(cq, dqf, ta=True, name="mla_uq_dw")
    dcq = _matmul(dqf, w_uq, tb=True, name="mla_uq_dx")
    dkr = _head_sum(dmk[:, :, HEAD_DIM:], name="mla_krope_sum")
    dkr1, dkr2 = _rope(dkr[:, :MLA_ROPE_HALF], dkr[:, MLA_ROPE_HALF:], cos16, sin16, negate=True, name="rope_k_bwd")
    dkvf = jnp.concatenate([_unheads(dmk[:, :, :HEAD_DIM]), _unheads(dmv)], axis=1)
    g_ukv = _matmul(ckv, dkvf, ta=True, name="mla_ukv_dw")
    dckv = _matmul(dkvf, w_ukv, tb=True, name="mla_ukv_dx")
    da_q, dg_q, _ = _norm_bwd(a_q, g_q, dcq, None, plus_one=False, name="mla_norm_q_bwd")
    da_kv, dg_kv, _ = _norm_bwd(a_kv, g_kv, dckv, None, plus_one=False, name="mla_norm_kv_bwd")
    da = jnp.concatenate([da_q, da_kv, dkr1, dkr2, jnp.zeros((s, w_a.shape[1] - 672), F32)], axis=1).astype(BF16)
    grads["mla_w_a"] = _matmul(h2, da, ta=True, name="mla_a_dw")[None, :, :672]
    dh2 = _matmul(da, w_a, tb=True, out_dtype=BF16, name="mla_a_dx")
    dx2, dscale2, dshift2 = _norm_bwd(x2, scale[2], dh2, dx3, plus_one=True, name="ada_bwd2")
    dmod[2] = jnp.concatenate([dshift2, dscale2, dgate2], axis=1)
    gq3 = [g_uq[:, :nq].reshape(MLA_Q_RANK, N_HEADS, HEAD_DIM), g_uq[:, nq:nq + 256].reshape(MLA_Q_RANK, N_HEADS, MLA_ROPE_HALF),
           g_uq[:, nq + 256:].reshape(MLA_Q_RANK, N_HEADS, MLA_ROPE_HALF)]
    grads["mla_w_uq"] = jnp.concatenate(gq3, axis=2).reshape(1, MLA_Q_RANK, -1)
    gkv3 = [g_ukv[:, :nq].reshape(MLA_KV_RANK, N_HEADS, HEAD_DIM), g_ukv[:, nq:].reshape(MLA_KV_RANK, N_HEADS, HEAD_DIM)]
    grads["mla_w_ukv"] = jnp.concatenate(gkv3, axis=2).reshape(1, MLA_KV_RANK, -1)
    grads["mla_g_q"], grads["mla_g_kv"] = dg_q, dg_kv

    dx1, gw_in0, dcw0, dcb0, gw_out0 = ffn_bwd(dx2, x1, 0, 1, ffn0_res)
    grads["ffn_w_in"] = jnp.stack([gw_in0, gw_in1])
    grads["ffn_w_out"] = jnp.stack([gw_out0, gw_out1])
    grads["ffn_conv_w"] = jnp.stack([dcw0, dcw1])
    g_conv_b = jnp.concatenate([dcb0, dcb1], axis=0)

    dy0, dgate0 = _resid_bwd(dx1, y0, gate[0], name="resid_bwd0")
    grads["fox_w_o"] = _matmul(fo2, dy0, ta=True, name="fox_out_dw")[None]
    dfo = _heads(_matmul(dy0, w_fox_o, tb=True, name="fox_out_dx"), HEAD_DIM)
    dfq, dfk, dfv, dcum = _attention_bwd(fox_res, dfo, scale=fox_scale, name="fox_attn")
    dfl, g_b_f = _fox_gate_bwd(fl, fox_b_f, dcum.T, name="fox_gate_bwd")
    dproj = jnp.concatenate([_unheads(dfq), _unheads(dfk), _unheads(dfv), dfl,
                             jnp.zeros((s, w_fox_in.shape[1] - 3 * nq - N_HEADS), F32)], axis=1).astype(BF16)
    grads["fox_w_in"] = _matmul(h0, dproj, ta=True, name="fox_proj_dw")[None, :, :3 * nq + N_HEADS]
    dh0 = _matmul(dproj, w_fox_in, tb=True, out_dtype=BF16, name="fox_proj_dx")
    dx0, dscale0, dshift0 = _norm_bwd(x0, scale[0], dh0, dx1, plus_one=True, name="ada_bwd0")
    dmod[0] = jnp.concatenate([dshift0, dscale0, dgate0], axis=1)

    dmod_send = jnp.stack(dmod, axis=0).reshape(4, N_DEV, cols).transpose(1, 0, 2).reshape(N_DEV, 4 * cols)
    dmod_recv = _exchange(_to_slab(dmod_send, 8), same_src=False, name="scatter_dmod")
    dmod_all = dmod_recv.reshape(N_DEV, -1)[:, :4 * cols].reshape(N_DEV, 4, cols)
    dmod_pad = _pad_axis(dmod_all, 0, LANES)
    g_ada_w = jnp.stack([_matmul(silu_c, dmod_pad[:, i], ta=True, name=f"ada_dw{i}") for i in range(4)])
    grads["ada_w"] = g_ada_w.reshape(ada_w.shape)
    g_ada_b = _sum_slabs(_to_slab(dmod_all.reshape(N_DEV, 4 * cols), 8), name="ada_db")
    grads["ada_b"] = g_ada_b.reshape(-1)[:4 * cols].reshape(ada_b.shape)

    rs = _Packer()
    for name, (shape, _) in list(_BIG.items()) + list(_SMALL.items()):
        rs.add(name, shape)
    chunks = {n: _full_to_chunks(grads[n], ax) for n, (_, ax) in list(_BIG.items()) + list(_SMALL.items())}
    rs_send = _to_slab(rs.pack(chunks, lead=(N_DEV,)), ROW_BLOCK)
    rs_recv = _exchange(rs_send, same_src=False, name="scatter_grads")
    rs_sum = _sum_slabs(rs_recv, name="sum_grads")
    grads.update(rs.unpack(rs_sum.reshape(-1)))

    rp = _Packer()
    for name, shape in _REPL.items():
        rp.add(name, shape)
    rp_send = _to_slab(rp.pack(dict(fox_b_f=g_b_f, ffn_conv_b=g_conv_b, final_g=d_final_g)), 8)
    rp_recv = _exchange(rp_send, same_src=True, name="gather_repl_grads")
    grads.update(rp.unpack(_sum_slabs(rp_recv, name="sum_repl_grads").reshape(-1)))

    deltas, new_m, new_v = {}, {}, {}
    for n in order:
        deltas[n], new_m[n], new_v[n] = _adamw(weights[n], grads[n], mom_m[n], mom_v[n], name=f"adamw_{n}")

    grad_x = dx0[None]
    return (loss, grad_x, *[grads[n] for n in order], *[deltas[n] for n in order],
            *[new_m[n] for n in order], *[new_v[n] for n in order])


def _silu(x, *, name):
    def body(x_ref, o_ref):
        xv = x_ref[...]
        o_ref[...] = (xv * jax.nn.sigmoid(xv)).astype(BF16)

    return pl.pallas_call(body, name=name, out_shape=jax.ShapeDtypeStruct(x.shape, BF16),
                          compiler_params=_params())(x)
```

```python
import jax
import jax.numpy as jnp
from jax import lax
from jax.experimental import pallas as pl
from jax.experimental.pallas import tpu as pltpu

F32 = jnp.float32
BF16 = jnp.bfloat16
HIGHEST = lax.Precision.HIGHEST

N_DEV = 8
D_MODEL = 1024
N_HEADS = 16
HEAD_DIM = 64
MLA_ROPE_HALF = 16
MLA_Q_RANK = 384
MLA_KV_RANK = 256
MLA_ROPE_DIM = 32
NORM_EPS = 1e-6
ROPE_BASE = 10000.0
ADAM_LR = 0.001
ADAM_B1 = 0.9
ADAM_B2 = 0.999
ADAM_EPS = 1e-08
ADAM_WD = 0.01
ADAM_STEP = 10

LANES = 128
VMEM_LIMIT_BYTES = 56 * 1024 * 1024
ROW_BLOCK = 512
ATT_BLOCK = 512
CONV_ROWS = 1024
MM_BM, MM_BN, MM_BK = 512, 1024, 1024


def _params(*sem):
    return pltpu.CompilerParams(dimension_semantics=sem or None, vmem_limit_bytes=VMEM_LIMIT_BYTES)


def _blk(dim, pref):
    if dim <= pref:
        return dim
    b = pref - pref % LANES
    while b >= LANES:
        if dim % b == 0:
            return b
        b -= LANES
    raise ValueError(f"no block for {dim}")


def _row_blk(rows, pref):
    if rows <= pref:
        return rows
    for b in range(pref - pref % 8, 7, -8):
        if rows % b == 0:
            return b
    return rows


def _pad_axis(a, axis, mult):
    pad = (-a.shape[axis]) % mult
    if pad == 0:
        return a
    widths = [(0, 0)] * a.ndim
    widths[axis] = (0, pad)
    return jnp.pad(a, widths)


def _matmul(a, b, *, ta=False, tb=False, out_dtype=F32, name):
    m, k = (a.shape[1], a.shape[0]) if ta else a.shape
    n = b.shape[0] if tb else b.shape[1]
    assert (b.shape[1] if tb else b.shape[0]) == k, (a.shape, b.shape, ta, tb)
    bm, bn, bk = _blk(m, MM_BM), _blk(n, MM_BN), _blk(k, MM_BK)
    nk = k // bk
    dims = (((0 if ta else 1,), (1 if tb else 0,)), ((), ()))

    def body(a_ref, b_ref, o_ref, acc_ref):
        kk = pl.program_id(2)

        @pl.when(kk == 0)
        def _():
            acc_ref[...] = jnp.zeros_like(acc_ref)

        acc_ref[...] += lax.dot_general(a_ref[...].astype(BF16), b_ref[...].astype(BF16), dims,
                                        preferred_element_type=F32)

        @pl.when(kk == nk - 1)
        def _():
            o_ref[...] = acc_ref[...].astype(out_dtype)

    a_spec = pl.BlockSpec((bk, bm), lambda i, j, kk: (kk, i)) if ta else pl.BlockSpec((bm, bk), lambda i, j, kk: (i, kk))
    b_spec = pl.BlockSpec((bn, bk), lambda i, j, kk: (j, kk)) if tb else pl.BlockSpec((bk, bn), lambda i, j, kk: (kk, j))
    return pl.pallas_call(
        body, name=name, grid=(m // bm, n // bn, nk),
        in_specs=[a_spec, b_spec], out_specs=pl.BlockSpec((bm, bn), lambda i, j, kk: (i, j)),
        out_shape=jax.ShapeDtypeStruct((m, n), out_dtype),
        scratch_shapes=[pltpu.VMEM((bm, bn), F32)],
        compiler_params=_params("parallel", "parallel", "arbitrary"),
    )(a, b)


def _norm_fwd(x, mul, add, *, plus_one, out_dtype, name):
    s, n = x.shape
    bs = _blk(s, ROW_BLOCK)

    def body(x_ref, m_ref, a_ref, o_ref):
        xv = x_ref[...]
        r = lax.rsqrt(jnp.mean(xv * xv, axis=-1, keepdims=True) + NORM_EPS)
        mv = m_ref[...] + 1.0 if plus_one else m_ref[...]
        o_ref[...] = (xv * r * mv + a_ref[...]).astype(out_dtype)

    row = pl.BlockSpec((bs, n), lambda i: (i, 0))
    vec = pl.BlockSpec((1, n), lambda i: (0, 0))
    return pl.pallas_call(
        body, name=name, grid=(s // bs,), in_specs=[row, vec, vec], out_specs=row,
        out_shape=jax.ShapeDtypeStruct((s, n), out_dtype), compiler_params=_params("parallel"),
    )(x, mul, add)


def _norm_bwd(x, mul, dy, dres, *, plus_one, name):
    s, n = x.shape
    bs = _blk(s, ROW_BLOCK)
    has_res = dres is not None

    def body(*refs):
        if has_res:
            x_ref, m_ref, dy_ref, dres_ref, dx_ref, dm_ref, da_ref = refs
        else:
            x_ref, m_ref, dy_ref, dx_ref, dm_ref, da_ref = refs
        xv = x_ref[...]
        dyv = dy_ref[...].astype(F32)
        r = lax.rsqrt(jnp.mean(xv * xv, axis=-1, keepdims=True) + NORM_EPS)
        xn = xv * r
        mv = m_ref[...] + 1.0 if plus_one else m_ref[...]
        g = dyv * mv
        dx = r * (g - xn * jnp.mean(g * xn, axis=-1, keepdims=True))
        if has_res:
            dx = dx + dres_ref[...]
        dx_ref[...] = dx

        @pl.when(pl.program_id(0) == 0)
        def _():
            dm_ref[...] = jnp.zeros_like(dm_ref)
            da_ref[...] = jnp.zeros_like(da_ref)

        dm_ref[...] += jnp.sum(dyv * xn, axis=0, keepdims=True)
        da_ref[...] += jnp.sum(dyv, axis=0, keepdims=True)

    row = pl.BlockSpec((bs, n), lambda i: (i, 0))
    vec = pl.BlockSpec((1, n), lambda i: (0, 0))
    ins = [x, mul, dy] + ([dres] if has_res else [])
    return pl.pallas_call(
        body, name=name, grid=(s // bs,),
        in_specs=[row, vec, row] + ([row] if has_res else []), out_specs=[row, vec, vec],
        out_shape=[jax.ShapeDtypeStruct((s, n), F32), jax.ShapeDtypeStruct((1, n), F32), jax.ShapeDtypeStruct((1, n), F32)],
        compiler_params=_params("arbitrary"),
    )(*ins)


def _resid_fwd(x, y, gate, *, name):
    s, n = x.shape
    bs = _blk(s, ROW_BLOCK)

    def body(x_ref, y_ref, g_ref, o_ref):
        o_ref[...] = x_ref[...] + g_ref[...] * y_ref[...]

    row = pl.BlockSpec((bs, n), lambda i: (i, 0))
    vec = pl.BlockSpec((1, n), lambda i: (0, 0))
    return pl.pallas_call(
        body, name=name, grid=(s // bs,), in_specs=[row, row, vec], out_specs=row,
        out_shape=jax.ShapeDtypeStruct((s, n), F32), compiler_params=_params("parallel"),
    )(x, y, gate)


def _resid_bwd(dx, y, gate, *, name):
    s, n = dx.shape
    bs = _blk(s, ROW_BLOCK)

    def body(dx_ref, y_ref, g_ref, dy_ref, dg_ref):
        dxv = dx_ref[...]
        dy_ref[...] = (g_ref[...] * dxv).astype(BF16)

        @pl.when(pl.program_id(0) == 0)
        def _():
            dg_ref[...] = jnp.zeros_like(dg_ref)

        dg_ref[...] += jnp.sum(dxv * y_ref[...], axis=0, keepdims=True)

    row = pl.BlockSpec((bs, n), lambda i: (i, 0))
    vec = pl.BlockSpec((1, n), lambda i: (0, 0))
    return pl.pallas_call(
        body, name=name, grid=(s // bs,), in_specs=[row, row, vec], out_specs=[row, vec],
        out_shape=[jax.ShapeDtypeStruct((s, n), BF16), jax.ShapeDtypeStruct((1, n), F32)],
        compiler_params=_params("arbitrary"),
    )(dx, y, gate)


def _final_loss(x, g, target, *, name):
    s, n = x.shape
    bs = _blk(s, ROW_BLOCK)

    def body(x_ref, g_ref, t_ref, loss_ref, dx_ref, dg_ref):
        xv = x_ref[...]
        r = lax.rsqrt(jnp.mean(xv * xv, axis=-1, keepdims=True) + NORM_EPS)
        xn = xv * r
        gv = g_ref[...]
        err = xn * gv - t_ref[...]
        dout = err * (1.0 / n)
        gg = dout * gv
        dx_ref[...] = r * (gg - xn * jnp.mean(gg * xn, axis=-1, keepdims=True))

        @pl.when(pl.program_id(0) == 0)
        def _():
            loss_ref[...] = jnp.zeros_like(loss_ref)
            dg_ref[...] = jnp.zeros_like(dg_ref)

        part = jnp.sum(jnp.sum(err * err, axis=-1, keepdims=True), axis=0, keepdims=True) * (0.5 / n)
        loss_ref[...] += jnp.broadcast_to(part, loss_ref.shape)
        dg_ref[...] += jnp.sum(dout * xn, axis=0, keepdims=True)

    row = pl.BlockSpec((bs, n), lambda i: (i, 0))
    vec = pl.BlockSpec((1, n), lambda i: (0, 0))
    return pl.pallas_call(
        body, name=name, grid=(s // bs,), in_specs=[row, vec, row],
        out_specs=[pl.BlockSpec((1, LANES), lambda i: (0, 0)), row, vec],
        out_shape=[jax.ShapeDtypeStruct((1, LANES), F32), jax.ShapeDtypeStruct((s, n), F32), jax.ShapeDtypeStruct((1, n), F32)],
        compiler_params=_params("arbitrary"),
    )(x, g, target)


def _lane_lt64(shape):
    return lax.broadcasted_iota(jnp.int32, shape, 1) < HEAD_DIM


def _keep_low(x):
    return jnp.where(_lane_lt64(x.shape), x.astype(F32), 0.0).astype(x.dtype)


def _keep_high(x):
    return jnp.where(_lane_lt64(x.shape), 0.0, x.astype(F32)).astype(x.dtype)


def _lane_merge(a, b):
    n = max(a.shape[0], b.shape[0])
    return jnp.where(_lane_lt64((n, LANES)), a, b)


def _pair(x, width, masked):
    if width == HEAD_DIM:
        return (_keep_low(x), _keep_high(x)) if masked else (x, x)
    return x[:, :LANES], x[:, LANES:]


def _scaled(x, scale):
    return x if scale == 1.0 else (x.astype(F32) * scale).astype(x.dtype)


def _qk_t(a, b):
    return lax.dot_general(a, b, (((1,), (1,)), ((), ())), preferred_element_type=F32)


def _attn_specs(s, blk, width, cols, resident):
    w = 2 * width
    if resident:
        return pl.BlockSpec((s, w), lambda p, i: (0, cols + p))
    return pl.BlockSpec((blk, w), lambda p, i: (i, cols + p))


def _attn_fwd(q, k, v, kb_row, *, qc, kc, vc, width, scale, name):
    s = q.shape[0]
    blk = _blk(s, ATT_BLOCK)
    nb = s // blk
    has_bias = kb_row is not None
    assert scale == 1.0 or (width == HEAD_DIM and scale == 0.125)

    def body(*refs):
        if has_bias:
            q_ref, k_ref, v_ref, kb_ref, o_ref, lse_ref = refs
        else:
            q_ref, k_ref, v_ref, o_ref, lse_ref = refs
        i = pl.program_id(1)
        qh = _pair(_scaled(q_ref[...], scale), width, True)

        def step(j, carry, diag):
            start = pl.multiple_of(j * blk, blk)
            kh = _pair(k_ref[pl.ds(start, blk), :], width, False)
            vj = v_ref[pl.ds(start, blk), :]
            out = []
            for hd in range(2):
                m, l, acc = carry[hd]
                sc = _qk_t(qh[hd], kh[hd])
                if has_bias:
                    sc = sc + kb_ref[hd, j]
                if diag:
                    sc = jnp.where(_causal_keep(blk), sc, -1e30)
                m_new = jnp.maximum(m, jnp.max(sc, axis=-1, keepdims=True))
                alpha = jnp.exp(m - m_new)
                p = jnp.exp(sc - m_new)
                l = alpha * l + jnp.sum(p, axis=-1, keepdims=True)
                acc = alpha * acc + jnp.dot(p.astype(BF16), vj, preferred_element_type=F32)
                out.append((m_new, l, acc))
            return tuple(out)

        one = (jnp.full((blk, 1), -1e30, F32), jnp.zeros((blk, 1), F32), jnp.zeros((blk, LANES), F32))
        carry = lax.fori_loop(0, i, lambda j, c: step(j, c, False), (one, one))
        (ma, la, acca), (mb, lb, accb) = step(i, carry, True)
        o_ref[...] = _lane_merge(acca / la, accb / lb)
        lse_ref[...] = _lane_merge(ma + jnp.log(la), mb + jnp.log(lb))

    ins = [q, k, v] + ([kb_row] if has_bias else [])
    out_spec = pl.BlockSpec((blk, LANES), lambda p, i: (i, p))
    return pl.pallas_call(
        body, name=name, grid=(N_HEADS // 2, nb),
        in_specs=[_attn_specs(s, blk, width, qc, False), _attn_specs(s, blk, width, kc, True),
                  _attn_specs(s, blk, HEAD_DIM, vc, True)]
                 + ([pl.BlockSpec((2, nb, 1, blk), lambda p, i: (p, 0, 0, 0))] if has_bias else []),
        out_specs=[out_spec, out_spec],
        out_shape=[jax.ShapeDtypeStruct((s, N_HEADS * HEAD_DIM), F32)] * 2,
        compiler_params=_params("parallel", "parallel"),
    )(*ins)


def _causal_keep(n):
    row = lax.broadcasted_iota(jnp.int32, (n, n), 0)
    col = lax.broadcasted_iota(jnp.int32, (n, n), 1)
    return col <= row


def _attn_bwd_dq(q, k, v, kb_row, o, do, lse, *, qc, kc, vc, width, scale, out_dtype, name):
    s = q.shape[0]
    blk = _blk(s, ATT_BLOCK)
    nb = s // blk
    has_bias = kb_row is not None

    def body(*refs):
        if has_bias:
            q_ref, k_ref, v_ref, kb_ref, o_ref, do_ref, lse_ref, dq_ref, dl_ref, dr_ref = refs
        else:
            q_ref, k_ref, v_ref, o_ref, do_ref, lse_ref, dq_ref, dl_ref = refs
        i = pl.program_id(1)
        qh = _pair(_scaled(q_ref[...], scale), width, True)
        do2 = do_ref[...]
        prod = do2.astype(F32) * o_ref[...]
        low = _lane_lt64(prod.shape)
        delta = (jnp.sum(jnp.where(low, prod, 0.0), axis=-1, keepdims=True),
                 jnp.sum(jnp.where(low, 0.0, prod), axis=-1, keepdims=True))
        dl_ref[...] = _lane_merge(delta[0], delta[1])
        doh = (_keep_low(do2), _keep_high(do2))
        lse2 = lse_ref[...]
        lse_h = (lse2[:, 0:1], lse2[:, HEAD_DIM:HEAD_DIM + 1])

        def step(j, carry, diag):
            start = pl.multiple_of(j * blk, blk)
            kh = _pair(k_ref[pl.ds(start, blk), :], width, False)
            vj = v_ref[pl.ds(start, blk), :]
            out = []
            for hd in range(2):
                acc, rsum = carry[hd]
                sc = _qk_t(qh[hd], kh[hd])
                if has_bias:
                    sc = sc + kb_ref[hd, j]
                if diag:
                    sc = jnp.where(_causal_keep(blk), sc, -1e30)
                p = jnp.exp(sc - lse_h[hd])
                ds = p * (_qk_t(doh[hd], vj) - delta[hd])
                if has_bias:
                    rsum = rsum + jnp.sum(ds, axis=-1, keepdims=True)
                out.append((acc + jnp.dot(ds.astype(BF16), kh[hd], preferred_element_type=F32), rsum))
            return tuple(out)

        one = (jnp.zeros((blk, LANES), F32), jnp.zeros((blk, 1), F32))
        carry = lax.fori_loop(0, i, lambda j, c: step(j, c, False), (one, one))
        (acca, ra), (accb, rb) = step(i, carry, True)
        if width == HEAD_DIM:
            dq = _lane_merge(acca, accb)
        else:
            dq = jnp.concatenate([acca, accb], axis=1)
        dq_ref[...] = (dq * scale).astype(out_dtype)
        if has_bias:
            dr_ref[...] = _lane_merge(ra, rb)

    stat = pl.BlockSpec((blk, LANES), lambda p, i: (i, p))
    ins = [q, k, v] + ([kb_row] if has_bias else []) + [o, do, lse]
    return pl.pallas_call(
        body, name=name, grid=(N_HEADS // 2, nb),
        in_specs=[_attn_specs(s, blk, width, qc, False), _attn_specs(s, blk, width, kc, True),
                  _attn_specs(s, blk, HEAD_DIM, vc, True)]
                 + ([pl.BlockSpec((2, nb, 1, blk), lambda p, i: (p, 0, 0, 0))] if has_bias else []) + [stat, stat, stat],
        out_specs=[pl.BlockSpec((blk, 2 * width), lambda p, i: (i, p)), stat] + ([stat] if has_bias else []),
        out_shape=[jax.ShapeDtypeStruct((s, N_HEADS * width), out_dtype)]
                  + [jax.ShapeDtypeStruct((s, N_HEADS * HEAD_DIM), F32)] * (2 if has_bias else 1),
        compiler_params=_params("parallel", "parallel"),
    )(*ins)


def _attn_bwd_dkv(q, k, v, kb_col, do, lse_row, delta_row, *, qc, kc, vc, width, scale, out_dtype, name):
    s = q.shape[0]
    blk = _blk(s, ATT_BLOCK)
    nb = s // blk
    has_bias = kb_col is not None

    def body(*refs):
        if has_bias:
            q_ref, k_ref, v_ref, kb_ref, do_ref, lse_ref, dl_ref, dk_ref, dv_ref, db_ref = refs
        else:
            q_ref, k_ref, v_ref, do_ref, lse_ref, dl_ref, dk_ref, dv_ref, db_ref = refs
        j = pl.program_id(1)
        kh = _pair(_scaled(k_ref[...], scale), width, True)
        v2 = v_ref[...]
        vh = (_keep_low(v2), _keep_high(v2))
        if has_bias:
            kb2 = kb_ref[0]
            kbh = (kb2[:, 0:1], kb2[:, 1:2])

        def step(i, carry, diag):
            start = pl.multiple_of(i * blk, blk)
            qh = _pair(q_ref[pl.ds(start, blk), :], width, False)
            doi = do_ref[pl.ds(start, blk), :]
            out = []
            for hd in range(2):
                dk, dvv, db = carry[hd]
                st = _qk_t(kh[hd], qh[hd])
                if has_bias:
                    st = st + kbh[hd]
                if diag:
                    row = lax.broadcasted_iota(jnp.int32, (blk, blk), 0)
                    colq = lax.broadcasted_iota(jnp.int32, (blk, blk), 1)
                    st = jnp.where(row <= colq, st, -1e30)
                pt = jnp.exp(st - lse_ref[hd, i])
                dvv = dvv + jnp.dot(pt.astype(BF16), doi, preferred_element_type=F32)
                dst = pt * (_qk_t(vh[hd], doi) - dl_ref[hd, i])
                dk = dk + jnp.dot(dst.astype(BF16), qh[hd], preferred_element_type=F32)
                db = db + jnp.sum(dst, axis=-1, keepdims=True)
                out.append((dk, dvv, db))
            return tuple(out)

        one = (jnp.zeros((blk, LANES), F32), jnp.zeros((blk, LANES), F32), jnp.zeros((blk, 1), F32))
        carry = step(j, (one, one), True)
        (dka, dva, dba), (dkb, dvb, dbb) = lax.fori_loop(j + 1, nb, lambda i, c: step(i, c, False), carry)
        if width == HEAD_DIM:
            dk = _lane_merge(dka, dkb)
        else:
            dk = jnp.concatenate([dka, dkb], axis=1)
        dk_ref[...] = (dk * scale).astype(out_dtype)
        dv_ref[...] = _lane_merge(dva, dvb).astype(out_dtype)
        db_ref[...] = _lane_merge(dba, dbb)

    stat = pl.BlockSpec((blk, LANES), lambda p, jj: (jj, p))
    rows = pl.BlockSpec((2, nb, 1, blk), lambda p, jj: (p, 0, 0, 0))
    ins = [q, k, v] + ([kb_col] if has_bias else []) + [do, lse_row, delta_row]
    return pl.pallas_call(
        body, name=name, grid=(N_HEADS // 2, nb),
        in_specs=[_attn_specs(s, blk, width, qc, True), _attn_specs(s, blk, width, kc, False),
                  _attn_specs(s, blk, HEAD_DIM, vc, False)]
                 + ([pl.BlockSpec((1, blk, 2), lambda p, jj: (p, jj, 0))] if has_bias else [])
                 + [pl.BlockSpec((s, LANES), lambda p, jj: (0, p)), rows, rows],
        out_specs=[pl.BlockSpec((blk, 2 * width), lambda p, jj: (jj, p)), stat, stat],
        out_shape=[jax.ShapeDtypeStruct((s, N_HEADS * width), out_dtype), jax.ShapeDtypeStruct((s, N_HEADS * HEAD_DIM), out_dtype),
                   jax.ShapeDtypeStruct((s, N_HEADS * HEAD_DIM), F32)],
        compiler_params=_params("parallel", "parallel"),
    )(*ins)


def _head_stat(t):
    return t[:, ::HEAD_DIM]


def _stat_rows(t16, blk):
    s = t16.shape[0]
    return t16.T.reshape(N_HEADS, s // blk, 1, blk)


def _attention(q, k, v, bias, *, qc, kc, vc, width, scale, name):
    blk = _blk(q.shape[0], ATT_BLOCK)
    kb_row = None if bias is None else _stat_rows(bias, blk)
    o, lse = _attn_fwd(q, k, v, kb_row, qc=qc, kc=kc, vc=vc, width=width, scale=scale, name=name + "_fwd")
    return o, (q, k, v, bias, o, lse)


def _attention_bwd(res, do, *, qc, kc, vc, width, scale, out_dtype, name):
    q, k, v, bias, o, lse = res
    s = q.shape[0]
    blk = _blk(s, ATT_BLOCK)
    cfg = dict(qc=qc, kc=kc, vc=vc, width=width, scale=scale, out_dtype=out_dtype)
    kb_row = None if bias is None else _stat_rows(bias, blk)
    kb_col = None if bias is None else bias.reshape(s, N_HEADS // 2, 2).transpose(1, 0, 2)
    outs = _attn_bwd_dq(q, k, v, kb_row, o, do, lse, name=name + "_dq", **cfg)
    dq, delta = outs[0], outs[1]
    lse_row = _stat_rows(_head_stat(lse), blk)
    delta_row = _stat_rows(_head_stat(delta), blk)
    dk, dv, dcol = _attn_bwd_dkv(q, k, v, kb_col, do, lse_row, delta_row, name=name + "_dkv", **cfg)
    if bias is None:
        return dq, dk, dv, None
    return dq, dk, dv, _head_stat(outs[2]) - _head_stat(dcol)


def _fox_gate_fwd(fl, bf, *, name):
    s, n = fl.shape
    bs = _blk(s, ROW_BLOCK)

    def body(fl_ref, bf_ref, cum_ref, carry_ref):
        @pl.when(pl.program_id(0) == 0)
        def _():
            carry_ref[...] = jnp.zeros_like(carry_ref)

        z = fl_ref[...] + bf_ref[...]
        lf = jnp.minimum(z, 0.0) - jnp.log1p(jnp.exp(-jnp.abs(z)))
        row = lax.broadcasted_iota(jnp.int32, (bs, bs), 0)
        col = lax.broadcasted_iota(jnp.int32, (bs, bs), 1)
        tri = (col <= row).astype(F32)
        cum_ref[...] = jnp.dot(tri, lf, preferred_element_type=F32, precision=HIGHEST) + carry_ref[...]
        carry_ref[...] += jnp.sum(lf, axis=0, keepdims=True)

    return pl.pallas_call(
        body, name=name, grid=(s // bs,),
        in_specs=[pl.BlockSpec((bs, n), lambda i: (i, 0)), pl.BlockSpec((1, n), lambda i: (0, 0))],
        out_specs=pl.BlockSpec((bs, n), lambda i: (i, 0)),
        out_shape=jax.ShapeDtypeStruct((s, n), F32), scratch_shapes=[pltpu.VMEM((1, n), F32)],
        compiler_params=_params("arbitrary"),
    )(fl, bf)


def _fox_gate_bwd(fl, bf, dcum, *, name):
    s, n = fl.shape
    bs = _blk(s, ROW_BLOCK)
    nb = s // bs

    def body(fl_ref, bf_ref, dc_ref, dz_ref, dbf_ref, carry_ref):
        @pl.when(pl.program_id(0) == 0)
        def _():
            carry_ref[...] = jnp.zeros_like(carry_ref)
            dbf_ref[...] = jnp.zeros_like(dbf_ref)

        dc = dc_ref[...]
        row = lax.broadcasted_iota(jnp.int32, (bs, bs), 0)
        col = lax.broadcasted_iota(jnp.int32, (bs, bs), 1)
        tri = (col >= row).astype(F32)
        dlf = jnp.dot(tri, dc, preferred_element_type=F32, precision=HIGHEST) + carry_ref[...]
        carry_ref[...] += jnp.sum(dc, axis=0, keepdims=True)
        z = fl_ref[...] + bf_ref[...]
        dz = dlf / (1.0 + jnp.exp(z))
        dz_ref[...] = dz
        dbf_ref[...] += jnp.sum(dz, axis=0, keepdims=True)

    rev = pl.BlockSpec((bs, n), lambda i: (nb - 1 - i, 0))
    vec = pl.BlockSpec((1, n), lambda i: (0, 0))
    return pl.pallas_call(
        body, name=name, grid=(nb,), in_specs=[rev, vec, rev], out_specs=[rev, vec],
        out_shape=[jax.ShapeDtypeStruct((s, n), F32), jax.ShapeDtypeStruct((1, n), F32)],
        scratch_shapes=[pltpu.VMEM((1, n), F32)], compiler_params=_params("arbitrary"),
    )(fl, bf, dcum)


def _rope(x1, x2, cos, sin, *, negate, name):
    s, n = x1.shape
    bs = _blk(s, ROW_BLOCK)

    def body(a_ref, b_ref, c_ref, s_ref, o1_ref, o2_ref):
        a, b, cv = a_ref[...], b_ref[...], c_ref[...]
        sv = -s_ref[...] if negate else s_ref[...]
        o1_ref[...] = a * cv - b * sv
        o2_ref[...] = b * cv + a * sv

    row = pl.BlockSpec((bs, n), lambda i: (i, 0))
    return pl.pallas_call(
        body, name=name, grid=(s // bs,), in_specs=[row] * 4, out_specs=[row, row],
        out_shape=[jax.ShapeDtypeStruct((s, n), F32)] * 2, compiler_params=_params("parallel"),
    )(x1, x2, cos, sin)


def _rope_heads(x, ta, tb, tc, *, out_dtype, name):
    s, n = x.shape
    bs = _blk(s, ROW_BLOCK)

    def body(x_ref, a_ref, b_ref, c_ref, o_ref):
        av, bv, cv = a_ref[...], b_ref[...], c_ref[...]
        for g in range(n // LANES):
            xg = x_ref[:, g * LANES:(g + 1) * LANES]
            og = xg * av + pltpu.roll(xg, LANES - MLA_ROPE_HALF, 1) * bv + pltpu.roll(xg, MLA_ROPE_HALF, 1) * cv
            o_ref[:, g * LANES:(g + 1) * LANES] = og.astype(out_dtype)

    row = pl.BlockSpec((bs, n), lambda i: (i, 0))
    tab = pl.BlockSpec((bs, LANES), lambda i: (i, 0))
    return pl.pallas_call(
        body, name=name, grid=(s // bs,), in_specs=[row, tab, tab, tab], out_specs=row,
        out_shape=jax.ShapeDtypeStruct((s, n), out_dtype), compiler_params=_params("parallel"),
    )(x, ta, tb, tc)


def _group_sum(x, *, name):
    s, n = x.shape
    bs = _blk(s, ROW_BLOCK)

    def body(x_ref, o_ref):
        acc = x_ref[:, 0:LANES]
        for g in range(1, n // LANES):
            acc = acc + x_ref[:, g * LANES:(g + 1) * LANES]
        o_ref[...] = acc

    return pl.pallas_call(
        body, name=name, grid=(s // bs,), in_specs=[pl.BlockSpec((bs, n), lambda i: (i, 0))],
        out_specs=pl.BlockSpec((bs, LANES), lambda i: (i, 0)),
        out_shape=jax.ShapeDtypeStruct((s, LANES), F32), compiler_params=_params("parallel"),
    )(x)


def _shift_down(x, k):
    return pltpu.roll(x, k, 0)


def _conv_rows(ext, w_ref, b_ref, rows):
    y = b_ref[...] + w_ref[0:1, :] * _shift_down(ext, 2) + w_ref[1:2, :] * _shift_down(ext, 1) + w_ref[2:3, :] * ext
    return y[8:8 + rows]


def _conv_gate_fwd(u, cw, cb, *, name):
    s, f2 = u.shape
    f = f2 // 2
    nf = f // LANES
    r = _blk(s, CONV_ROWS)
    r8 = r // 8

    def body(ug_ref, ugp_ref, uv_ref, uvp_ref, wg_ref, wv_ref, bg_ref, bv_ref, o_ref):
        first = pl.program_id(1) == 0

        def conv(cur_ref, prev_ref, w_ref, b_ref):
            prev = jnp.where(first, 0.0, prev_ref[...])
            return _conv_rows(jnp.concatenate([prev, cur_ref[...]], axis=0), w_ref, b_ref, r)

        yg = conv(ug_ref, ugp_ref, wg_ref, bg_ref)
        yv = conv(uv_ref, uvp_ref, wv_ref, bv_ref)
        o_ref[...] = (yg * jax.nn.sigmoid(yg) * yv).astype(BF16)

    def cur(off):
        return pl.BlockSpec((r, LANES), lambda c, i: (i, c + off))

    def prev(off):
        return pl.BlockSpec((8, LANES), lambda c, i: (jnp.maximum(i * r8 - 1, 0), c + off))

    def wspec(rows, off):
        return pl.BlockSpec((rows, LANES), lambda c, i: (0, c + off))

    return pl.pallas_call(
        body, name=name, grid=(nf, s // r),
        in_specs=[cur(0), prev(0), cur(nf), prev(nf), wspec(3, 0), wspec(3, nf), wspec(1, 0), wspec(1, nf)],
        out_specs=pl.BlockSpec((r, LANES), lambda c, i: (i, c)),
        out_shape=jax.ShapeDtypeStruct((s, f), BF16), compiler_params=_params("parallel", "parallel"),
    )(u, u, u, u, cw, cw, cb, cb)


def _conv_gate_bwd(u, cw, cb, dg, *, name):
    s, f2 = u.shape
    f = f2 // 2
    nf = f // LANES
    r = _blk(s, CONV_ROWS)
    r8 = r // 8
    nr = s // r

    def body(ug_ref, ugp_ref, ugn_ref, uv_ref, uvp_ref, uvn_ref, wg_ref, wv_ref, bg_ref, bv_ref, dg_ref, dgn_ref,
             dug_ref, duv_ref, dwg_ref, dwv_ref, dbg_ref, dbv_ref):
        i = pl.program_id(1)
        first, last = i == 0, i == nr - 1

        def ext_of(cur_ref, prev_ref, next_ref):
            prev = jnp.where(first, 0.0, prev_ref[...])
            return jnp.concatenate([prev, cur_ref[...], next_ref[...]], axis=0)

        eg, ev = ext_of(ug_ref, ugp_ref, ugn_ref), ext_of(uv_ref, uvp_ref, uvn_ref)
        yg = _conv_rows(eg, wg_ref, bg_ref, r + 8)
        yv = _conv_rows(ev, wv_ref, bv_ref, r + 8)
        dgn = jnp.where(last, 0.0, dgn_ref[...])
        dgx = jnp.concatenate([dg_ref[...], dgn], axis=0)
        sg = jax.nn.sigmoid(yg)
        dyg = dgx * yv * (sg * (1.0 + yg * (1.0 - sg)))
        dyv = dgx * (yg * sg)

        @pl.when(i == 0)
        def _():
            for ref in (dwg_ref, dwv_ref, dbg_ref, dbv_ref):
                ref[...] = jnp.zeros_like(ref)

        def grads(dy, ext, w_ref, du_ref, dw_ref, db_ref):
            n = r + 8
            du = w_ref[2:3, :] * dy + w_ref[1:2, :] * pltpu.roll(dy, n - 1, 0) + w_ref[0:1, :] * pltpu.roll(dy, n - 2, 0)
            du_ref[...] = du[0:r].astype(BF16)
            dyc = dy[0:r]
            db_ref[...] += jnp.sum(dyc, axis=0, keepdims=True)
            ext_c = ext[0:r + 8]
            dw_ref[0:1, :] += jnp.sum(dyc * _shift_down(ext_c, 2)[8:], axis=0, keepdims=True)
            dw_ref[1:2, :] += jnp.sum(dyc * _shift_down(ext_c, 1)[8:], axis=0, keepdims=True)
            dw_ref[2:3, :] += jnp.sum(dyc * ext_c[8:], axis=0, keepdims=True)

        grads(dyg, eg, wg_ref, dug_ref, dwg_ref, dbg_ref)
        grads(dyv, ev, wv_ref, duv_ref, dwv_ref, dbv_ref)

    def cur(off):
        return pl.BlockSpec((r, LANES), lambda c, i: (i, c + off))

    def prev(off):
        return pl.BlockSpec((8, LANES), lambda c, i: (jnp.maximum(i * r8 - 1, 0), c + off))

    def nxt(off):
        return pl.BlockSpec((8, LANES), lambda c, i: (jnp.minimum((i + 1) * r8, s // 8 - 1), c + off))

    def wspec(rows, off):
        return pl.BlockSpec((rows, LANES), lambda c, i: (0, c + off))

    outs = pl.pallas_call(
        body, name=name, grid=(nf, nr),
        in_specs=[cur(0), prev(0), nxt(0), cur(nf), prev(nf), nxt(nf), wspec(3, 0), wspec(3, nf), wspec(1, 0), wspec(1, nf),
                  cur(0), nxt(0)],
        out_specs=[cur(0), cur(0), wspec(3, 0), wspec(3, 0), wspec(1, 0), wspec(1, 0)],
        out_shape=[jax.ShapeDtypeStruct((s, f), BF16), jax.ShapeDtypeStruct((s, f), BF16),
                   jax.ShapeDtypeStruct((3, f), F32), jax.ShapeDtypeStruct((3, f), F32),
                   jax.ShapeDtypeStruct((1, f), F32), jax.ShapeDtypeStruct((1, f), F32)],
        compiler_params=_params("parallel", "arbitrary"),
    )(u, u, u, u, u, u, cw, cw, cb, cb, dg, dg)
    dug, duv, dwg, dwv, dbg, dbv = outs
    return jnp.concatenate([dug, duv], axis=1), jnp.concatenate([dwg, dwv], axis=1), jnp.concatenate([dbg, dbv], axis=1)


def _adamw(w, g, m, v, *, slabs, name):
    shape = w.shape
    cols = shape[-1]
    rows = w.size // cols
    w2, m2, v2 = (t.reshape(rows, cols) for t in (w, m, v))
    g2 = g.reshape((N_DEV, rows, cols) if slabs else (rows, cols))
    br = _row_blk(rows, ROW_BLOCK // 2 if slabs else ROW_BLOCK)

    def body(w_ref, g_ref, m_ref, v_ref, go_ref, d_ref, nm_ref, nv_ref):
        if slabs:
            gv = g_ref[0]
            for p in range(1, N_DEV):
                gv = gv + g_ref[p]
        else:
            gv = g_ref[...]
        nm = ADAM_B1 * m_ref[...] + (1.0 - ADAM_B1) * gv
        nv = ADAM_B2 * v_ref[...] + (1.0 - ADAM_B2) * (gv * gv)
        m_hat = nm / (1.0 - ADAM_B1 ** ADAM_STEP)
        v_hat = nv / (1.0 - ADAM_B2 ** ADAM_STEP)
        go_ref[...] = gv
        d_ref[...] = -ADAM_LR * (m_hat / (jnp.sqrt(v_hat) + ADAM_EPS) + ADAM_WD * w_ref[...])
        nm_ref[...] = nm
        nv_ref[...] = nv

    spec = pl.BlockSpec((br, cols), lambda i: (i, 0))
    gspec = pl.BlockSpec((N_DEV, br, cols), lambda i: (0, i, 0)) if slabs else spec
    outs = pl.pallas_call(
        body, name=name, grid=(rows // br,), in_specs=[spec, gspec, spec, spec], out_specs=[spec] * 4,
        out_shape=[jax.ShapeDtypeStruct((rows, cols), F32)] * 4, compiler_params=_params("parallel"),
    )(w2, g2, m2, v2)
    return tuple(t.reshape(shape) for t in outs)


def _exchange(xs, *, same_src, name):
    n = len(xs)
    slabs = [x.shape if same_src else x.shape[1:] for x in xs]

    def body(*refs):
        x_refs, o_refs = refs[:n], refs[n:2 * n]
        send_sems, recv_sems, loc_sems = refs[2 * n:]
        ix, iy, ic = lax.axis_index("x"), lax.axis_index("y"), lax.axis_index("c")
        me = 4 * ix + 2 * iy + ic
        local, sends, recvs = [], [], []
        for a in range(n):
            def src(p, a=a):
                return x_refs[a] if same_src else x_refs[a].at[p]

            local.append(pltpu.make_async_copy(src(me), o_refs[a].at[me], loc_sems.at[a]))
            for k in (1, 2, 4, 3, 5, 6, 7):
                px = 1 - ix if k & 4 else ix
                py = 1 - iy if k & 2 else iy
                pc = 1 - ic if k & 1 else ic
                p = 4 * px + 2 * py + pc
                for dst, out in ((me, sends), (p, recvs)):
                    out.append(pltpu.make_async_remote_copy(
                        src_ref=src(p), dst_ref=o_refs[a].at[dst], send_sem=send_sems.at[a, k - 1],
                        recv_sem=recv_sems.at[a, k - 1], device_id=(px, py, pc), device_id_type=pl.DeviceIdType.MESH))
        for cp in local + sends:
            cp.start()
        for cp in recvs:
            cp.wait_recv()
        for cp in sends:
            cp.wait_send()
        for cp in local:
            cp.wait()

    return pl.pallas_call(
        body, name=name,
        in_specs=[pl.BlockSpec(memory_space=pl.ANY)] * n, out_specs=[pl.BlockSpec(memory_space=pl.ANY)] * n,
        out_shape=[jax.ShapeDtypeStruct((N_DEV,) + tuple(sl), x.dtype) for sl, x in zip(slabs, xs)],
        scratch_shapes=[pltpu.SemaphoreType.DMA((n, N_DEV - 1)), pltpu.SemaphoreType.DMA((n, N_DEV - 1)),
                        pltpu.SemaphoreType.DMA((n,))],
        compiler_params=pltpu.CompilerParams(has_side_effects=True, vmem_limit_bytes=VMEM_LIMIT_BYTES),
    )(*xs)


def _sum_slabs(x, *, name):
    n, r, c = x.shape
    br = _row_blk(r, ROW_BLOCK)

    def body(x_ref, o_ref):
        acc = x_ref[0]
        for p in range(1, n):
            acc = acc + x_ref[p]
        o_ref[...] = acc

    return pl.pallas_call(
        body, name=name, grid=(r // br,), in_specs=[pl.BlockSpec((n, br, c), lambda i: (0, i, 0))],
        out_specs=pl.BlockSpec((br, c), lambda i: (i, 0)),
        out_shape=jax.ShapeDtypeStruct((r, c), F32), compiler_params=_params("parallel"),
    )(x)


def _silu(x, *, name):
    def body(x_ref, o_ref):
        xv = x_ref[...]
        o_ref[...] = (xv * jax.nn.sigmoid(xv)).astype(BF16)

    return pl.pallas_call(body, name=name, out_shape=jax.ShapeDtypeStruct(x.shape, BF16),
                          compiler_params=_params())(x)


_BIG = {"fox_w_in": 2, "fox_w_o": 1, "mla_w_a": 1, "mla_w_uq": 2, "mla_w_ukv": 2, "mla_w_o": 1, "ffn_w_in": 2, "ffn_w_out": 1}
_SMALL = {"mla_g_q": 1, "mla_g_kv": 1, "ffn_conv_w": 2}
_REPL = ("fox_b_f", "ffn_conv_b", "final_g")


def _gathered_to_full(g, axis):
    full = jnp.moveaxis(g, 0, axis)
    shape = list(full.shape)
    shape[axis:axis + 2] = [shape[axis] * shape[axis + 1]]
    return full.reshape(shape)


def _full_to_chunks(full, axis):
    shape = list(full.shape)
    shape[axis:axis + 1] = [N_DEV, shape[axis] // N_DEV]
    return jnp.moveaxis(full.reshape(shape), axis, 0)


def _per_head(parts, s_or_rows):
    return jnp.concatenate([p.reshape(s_or_rows, N_HEADS, -1) for p in parts], axis=-1).reshape(s_or_rows, -1)


def kernel(x, c, ada_w, ada_b, fox_w_in, fox_b_f, fox_w_o, mla_w_a, mla_g_q, mla_g_kv, mla_w_uq, mla_w_ukv, mla_w_o, ffn_w_in, ffn_conv_w, ffn_conv_b, ffn_w_out, final_g, loss_target, m_ada_w, m_ada_b, m_fox_w_in, m_fox_b_f, m_fox_w_o, m_mla_w_a, m_mla_g_q, m_mla_g_kv, m_mla_w_uq, m_mla_w_ukv, m_mla_w_o, m_ffn_w_in, m_ffn_conv_w, m_ffn_conv_b, m_ffn_w_out, m_final_g, v_ada_w, v_ada_b, v_fox_w_in, v_fox_b_f, v_fox_w_o, v_mla_w_a, v_mla_g_q, v_mla_g_kv, v_mla_w_uq, v_mla_w_ukv, v_mla_w_o, v_ffn_w_in, v_ffn_conv_w, v_ffn_conv_b, v_ffn_w_out, v_final_g):
    weights = dict(ada_w=ada_w, ada_b=ada_b, fox_w_in=fox_w_in, fox_b_f=fox_b_f, fox_w_o=fox_w_o, mla_w_a=mla_w_a,
                   mla_g_q=mla_g_q, mla_g_kv=mla_g_kv, mla_w_uq=mla_w_uq, mla_w_ukv=mla_w_ukv, mla_w_o=mla_w_o,
                   ffn_w_in=ffn_w_in, ffn_conv_w=ffn_conv_w, ffn_conv_b=ffn_conv_b, ffn_w_out=ffn_w_out, final_g=final_g)
    mom_m = dict(ada_w=m_ada_w, ada_b=m_ada_b, fox_w_in=m_fox_w_in, fox_b_f=m_fox_b_f, fox_w_o=m_fox_w_o, mla_w_a=m_mla_w_a,
                 mla_g_q=m_mla_g_q, mla_g_kv=m_mla_g_kv, mla_w_uq=m_mla_w_uq, mla_w_ukv=m_mla_w_ukv, mla_w_o=m_mla_w_o,
                 ffn_w_in=m_ffn_w_in, ffn_conv_w=m_ffn_conv_w, ffn_conv_b=m_ffn_conv_b, ffn_w_out=m_ffn_w_out, final_g=m_final_g)
    mom_v = dict(ada_w=v_ada_w, ada_b=v_ada_b, fox_w_in=v_fox_w_in, fox_b_f=v_fox_b_f, fox_w_o=v_fox_w_o, mla_w_a=v_mla_w_a,
                 mla_g_q=v_mla_g_q, mla_g_kv=v_mla_g_kv, mla_w_uq=v_mla_w_uq, mla_w_ukv=v_mla_w_ukv, mla_w_o=v_mla_w_o,
                 ffn_w_in=v_ffn_w_in, ffn_conv_w=v_ffn_conv_w, ffn_conv_b=v_ffn_conv_b, ffn_w_out=v_ffn_w_out, final_g=v_final_g)
    order = list(weights)
    x0 = x[0]
    target = loss_target[0]
    s = x0.shape[0]
    d = D_MODEL
    cols = ada_w.shape[-1]
    nq = N_HEADS * HEAD_DIM

    small_names = ["c"] + list(_SMALL)
    small_all = dict(zip(small_names, _exchange([c] + [weights[n] for n in _SMALL], same_src=True, name="gather_small")))
    c_all = small_all["c"].reshape(N_DEV, d)
    g_q = _gathered_to_full(small_all["mla_g_q"], 1)
    g_kv = _gathered_to_full(small_all["mla_g_kv"], 1)
    conv_w = _gathered_to_full(small_all["ffn_conv_w"], 2)

    c_pad = _pad_axis(c_all, 0, LANES)
    silu_c = _silu(c_pad, name="silu_c")
    w_ada = ada_w.reshape(4, d, cols)
    b_ada = ada_b.reshape(4, 1, cols)
    mods = [_matmul(silu_c, w_ada[i], name=f"ada_mod{i}")[:N_DEV] + b_ada[i] for i in range(4)]
    mod_send = _pad_axis(jnp.stack(mods, axis=1), 1, 8)
    mod_recv, = _exchange([mod_send], same_src=False, name="scatter_mod")
    mod = mod_recv[:, :4].transpose(1, 0, 2).reshape(4, 3 * d)
    shift = [mod[i:i + 1, 0:d] for i in range(4)]
    scale = [mod[i:i + 1, d:2 * d] for i in range(4)]
    gate = [mod[i:i + 1, 2 * d:3 * d] for i in range(4)]

    big_all = _exchange([weights[n].astype(BF16) for n in _BIG], same_src=True, name="gather_weights")
    wfull = {n: _gathered_to_full(g, _BIG[n]) for n, g in zip(_BIG, big_all)}

    w_fox_in = _pad_axis(wfull["fox_w_in"][0], 1, LANES)
    w_fox_qkv, w_fox_f = w_fox_in[:, :3 * nq], w_fox_in[:, 3 * nq:]
    w_fox_o = wfull["fox_w_o"][0]
    w_a = _pad_axis(wfull["mla_w_a"][0], 1, LANES)
    wq = wfull["mla_w_uq"][0].reshape(MLA_Q_RANK, N_HEADS, HEAD_DIM + MLA_ROPE_DIM)
    w_uq = _pad_axis(wq, 2, LANES).reshape(MLA_Q_RANK, N_HEADS * LANES)
    wkv = wfull["mla_w_ukv"][0].reshape(MLA_KV_RANK, N_HEADS, 2 * HEAD_DIM)
    w_ukv = jnp.concatenate([wkv[:, :, :HEAD_DIM].reshape(MLA_KV_RANK, -1), wkv[:, :, HEAD_DIM:].reshape(MLA_KV_RANK, -1)], axis=1)
    w_mla_o = wfull["mla_w_o"][0]
    w_ffn_in = wfull["ffn_w_in"]
    w_ffn_out = wfull["ffn_w_out"]
    conv_b = ffn_conv_b

    fox_scale = HEAD_DIM ** -0.5
    mla_scale = (HEAD_DIM + MLA_ROPE_DIM) ** -0.5
    pos = jnp.arange(s, dtype=F32)
    inv_freq = ROPE_BASE ** (-jnp.arange(0, MLA_ROPE_DIM, 2, dtype=F32) / MLA_ROPE_DIM)
    ang = pos[:, None] * inv_freq[None, :]
    cos16, sin16 = jnp.cos(ang), jnp.sin(ang)
    z16, z32, z64 = jnp.zeros((s, 16), F32), jnp.zeros((s, 32), F32), jnp.zeros((s, 64), F32)
    tab_a = jnp.concatenate([jnp.ones((s, 64), F32), cos16, cos16, z32], axis=1) * mla_scale
    tab_b = jnp.concatenate([z64, -sin16, z16, z32], axis=1) * mla_scale
    tab_c = jnp.concatenate([z64, z16, sin16, z32], axis=1) * mla_scale

    h0 = _norm_fwd(x0, scale[0], shift[0], plus_one=True, out_dtype=BF16, name="ada_fwd0")
    qkv = _matmul(h0, w_fox_qkv, out_dtype=BF16, name="fox_proj")
    fl = _matmul(h0, w_fox_f, name="fox_proj_f")[:, :N_HEADS]
    cum = _fox_gate_fwd(fl, fox_b_f, name="fox_gate_fwd")
    fox_cfg = dict(qc=0, kc=N_HEADS // 2, vc=N_HEADS, width=HEAD_DIM, scale=fox_scale)
    fo, fox_res = _attention(qkv, qkv, qkv, -cum, name="fox_attn", **fox_cfg)
    y0 = _matmul(fo, w_fox_o, name="fox_out")
    x1 = _resid_fwd(x0, y0, gate[0], name="resid_fwd0")

    def ffn_fwd(xin, li, sub):
        hh = _norm_fwd(xin, scale[sub], shift[sub], plus_one=True, out_dtype=BF16, name=f"ada_fwd{sub}")
        u = _matmul(hh, w_ffn_in[li], name=f"ffn_up{li}")
        g = _conv_gate_fwd(u, conv_w[li], conv_b[li:li + 1], name=f"conv_fwd{li}")
        y = _matmul(g, w_ffn_out[li], name=f"ffn_down{li}")
        return _resid_fwd(xin, y, gate[sub], name=f"resid_fwd{sub}"), (hh, u, g, y)

    x2, ffn0_res = ffn_fwd(x1, 0, 1)

    h2 = _norm_fwd(x2, scale[2], shift[2], plus_one=True, out_dtype=BF16, name="ada_fwd2")
    a = _matmul(h2, w_a, name="mla_a")
    a_q, a_kv = a[:, :MLA_Q_RANK], a[:, MLA_Q_RANK:MLA_Q_RANK + MLA_KV_RANK]
    kr1 = a[:, MLA_Q_RANK + MLA_KV_RANK:MLA_Q_RANK + MLA_KV_RANK + MLA_ROPE_HALF]
    kr2 = a[:, MLA_Q_RANK + MLA_KV_RANK + MLA_ROPE_HALF:MLA_Q_RANK + MLA_KV_RANK + MLA_ROPE_DIM]
    cq = _norm_fwd(a_q, g_q, jnp.zeros_like(g_q), plus_one=False, out_dtype=BF16, name="mla_norm_q")
    ckv = _norm_fwd(a_kv, g_kv, jnp.zeros_like(g_kv), plus_one=False, out_dtype=BF16, name="mla_norm_kv")
    qf = _matmul(cq, w_uq, name="mla_uq")
    kvf = _matmul(ckv, w_ukv, out_dtype=BF16, name="mla_ukv")
    mq = _rope_heads(qf, tab_a, tab_b, tab_c, out_dtype=BF16, name="rope_q")
    kk1, kk2 = _rope(kr1, kr2, cos16, sin16, negate=False, name="rope_k")
    k_tail = jnp.concatenate([kk1, kk2, z32], axis=1).astype(BF16)
    mk = jnp.concatenate([kvf[:, :nq].reshape(s, N_HEADS, HEAD_DIM),
                          jnp.broadcast_to(k_tail[:, None, :], (s, N_HEADS, HEAD_DIM))], axis=-1).reshape(s, N_HEADS * LANES)
    mla_cfg = dict(qc=0, kc=0, vc=N_HEADS // 2, width=LANES, scale=1.0)
    mo, mla_res = _attention(mq, mk, kvf, None, name="mla_attn", **mla_cfg)
    y2 = _matmul(mo, w_mla_o, name="mla_out")
    x3 = _resid_fwd(x2, y2, gate[2], name="resid_fwd2")

    x4, ffn1_res = ffn_fwd(x3, 1, 3)

    loss_vec, dx4, d_final_g = _final_loss(x4, final_g.reshape(1, d), target, name="final_loss")
    loss = lax.psum(loss_vec[0, 0], ("x", "y", "c"))

    grads = {}
    dmod = [None] * 4

    def ffn_bwd(dx_out, xin, li, sub, res):
        hh, u, g, y = res
        dy, dgate = _resid_bwd(dx_out, y, gate[sub], name=f"resid_bwd{sub}")
        gw_out = _matmul(g, dy, ta=True, name=f"ffn_down_dw{li}")
        dg = _matmul(dy, w_ffn_out[li], tb=True, name=f"ffn_down_dx{li}")
        du, dcw, dcb = _conv_gate_bwd(u, conv_w[li], conv_b[li:li + 1], dg, name=f"conv_bwd{li}")
        gw_in = _matmul(hh, du, ta=True, name=f"ffn_up_dw{li}")
        dh = _matmul(du, w_ffn_in[li], tb=True, out_dtype=BF16, name=f"ffn_up_dx{li}")
        dx_in, dscale, dshift = _norm_bwd(xin, scale[sub], dh, dx_out, plus_one=True, name=f"ada_bwd{sub}")
        dmod[sub] = jnp.concatenate([dshift, dscale, dgate], axis=1)
        return dx_in, gw_in, dcw, dcb, gw_out

    dx3, gw_in1, dcw1, dcb1, gw_out1 = ffn_bwd(dx4, x3, 1, 3, ffn1_res)

    dy2, dgate2 = _resid_bwd(dx3, y2, gate[2], name="resid_bwd2")
    grads["mla_w_o"] = _matmul(mo, dy2, ta=True, name="mla_out_dw")[None]
    dmo = _matmul(dy2, w_mla_o, tb=True, out_dtype=BF16, name="mla_out_dx")
    dmq, dmk, dmv, _ = _attention_bwd(mla_res, dmo, out_dtype=F32, name="mla_attn", **mla_cfg)
    dqf = _rope_heads(dmq, tab_a, -tab_b, -tab_c, out_dtype=BF16, name="rope_q_bwd")
    g_uq = _matmul(cq, dqf, ta=True, name="mla_uq_dw")
    dcq = _matmul(dqf, w_uq, tb=True, name="mla_uq_dx")
    dmk3 = dmk.reshape(s, N_HEADS, LANES)
    dkr = _group_sum(dmk, name="mla_krope_sum")
    dkr1, dkr2 = _rope(dkr[:, HEAD_DIM:HEAD_DIM + MLA_ROPE_HALF], dkr[:, HEAD_DIM + MLA_ROPE_HALF:HEAD_DIM + MLA_ROPE_DIM],
                       cos16, sin16, negate=True, name="rope_k_bwd")
    dkvf = jnp.concatenate([dmk3[:, :, :HEAD_DIM].reshape(s, nq).astype(BF16), dmv.astype(BF16)], axis=1)
    g_ukv = _matmul(ckv, dkvf, ta=True, name="mla_ukv_dw")
    dckv = _matmul(dkvf, w_ukv, tb=True, name="mla_ukv_dx")
    da_q, dg_q, _ = _norm_bwd(a_q, g_q, dcq, None, plus_one=False, name="mla_norm_q_bwd")
    da_kv, dg_kv, _ = _norm_bwd(a_kv, g_kv, dckv, None, plus_one=False, name="mla_norm_kv_bwd")
    da = jnp.concatenate([da_q, da_kv, dkr1, dkr2, jnp.zeros((s, w_a.shape[1] - 672), F32)], axis=1).astype(BF16)
    grads["mla_w_a"] = _matmul(h2, da, ta=True, name="mla_a_dw")[None, :, :672]
    dh2 = _matmul(da, w_a, tb=True, out_dtype=BF16, name="mla_a_dx")
    dx2, dscale2, dshift2 = _norm_bwd(x2, scale[2], dh2, dx3, plus_one=True, name="ada_bwd2")
    dmod[2] = jnp.concatenate([dshift2, dscale2, dgate2], axis=1)
    grads["mla_w_uq"] = g_uq.reshape(MLA_Q_RANK, N_HEADS, LANES)[:, :, :HEAD_DIM + MLA_ROPE_DIM].reshape(1, MLA_Q_RANK, -1)
    grads["mla_w_ukv"] = _per_head([g_ukv[:, :nq], g_ukv[:, nq:]], MLA_KV_RANK)[None]
    grads["mla_g_q"], grads["mla_g_kv"] = dg_q, dg_kv

    dx1, gw_in0, dcw0, dcb0, gw_out0 = ffn_bwd(dx2, x1, 0, 1, ffn0_res)
    grads["ffn_w_in"] = jnp.stack([gw_in0, gw_in1])
    grads["ffn_w_out"] = jnp.stack([gw_out0, gw_out1])
    grads["ffn_conv_w"] = jnp.stack([dcw0, dcw1])
    g_conv_b = jnp.concatenate([dcb0, dcb1], axis=0)

    dy0, dgate0 = _resid_bwd(dx1, y0, gate[0], name="resid_bwd0")
    grads["fox_w_o"] = _matmul(fo, dy0, ta=True, name="fox_out_dw")[None]
    dfo = _matmul(dy0, w_fox_o, tb=True, out_dtype=BF16, name="fox_out_dx")
    dfq, dfk, dfv, dneg = _attention_bwd(fox_res, dfo, out_dtype=BF16, name="fox_attn", **fox_cfg)
    dfl, g_b_f = _fox_gate_bwd(fl, fox_b_f, dneg, name="fox_gate_bwd")
    dproj = jnp.concatenate([dfq, dfk, dfv, _pad_axis(dfl, 1, LANES).astype(BF16)], axis=1)
    grads["fox_w_in"] = _matmul(h0, dproj, ta=True, name="fox_proj_dw")[None, :, :3 * nq + N_HEADS]
    dh0 = _matmul(dproj, w_fox_in, tb=True, out_dtype=BF16, name="fox_proj_dx")
    dx0, dscale0, dshift0 = _norm_bwd(x0, scale[0], dh0, dx1, plus_one=True, name="ada_bwd0")
    dmod[0] = jnp.concatenate([dshift0, dscale0, dgate0], axis=1)

    dmod_send = _pad_axis(jnp.stack(dmod, axis=0).reshape(4, N_DEV, cols).transpose(1, 0, 2), 1, 8)
    dmod_recv, = _exchange([dmod_send], same_src=False, name="scatter_dmod")
    dmod_all = dmod_recv[:, :4]
    dmod_pad = _pad_axis(dmod_all, 0, LANES)
    g_ada_w = jnp.stack([_matmul(silu_c, dmod_pad[:, i], ta=True, name=f"ada_dw{i}") for i in range(4)])
    grads["ada_w"] = g_ada_w.reshape(ada_w.shape)
    grads["ada_b"] = _sum_slabs(dmod_recv, name="ada_db")[:4].reshape(ada_b.shape)

    sharded = list(_BIG) + list(_SMALL)
    axes = {**_BIG, **_SMALL}
    recv = _exchange([_full_to_chunks(grads[n], axes[n]) for n in sharded], same_src=False, name="scatter_grads")
    grads.update(dict(zip(sharded, recv)))
    repl = _exchange([g_b_f, g_conv_b, d_final_g], same_src=True, name="gather_repl_grads")
    grads.update(dict(zip(_REPL, repl)))

    grad_out, deltas, new_m, new_v = {}, {}, {}, {}
    for n in order:
        grad_out[n], deltas[n], new_m[n], new_v[n] = _adamw(
            weights[n], grads[n], mom_m[n], mom_v[n], slabs=n in axes or n in _REPL, name=f"adamw_{n}")

    grad_x = dx0[None]
    return (loss, grad_x, *[grad_out[n] for n in order], *[deltas[n] for n in order],
            *[new_m[n] for n in order], *[new_v[n] for n in order])
```

```python
import jax
import jax.numpy as jnp
from jax import lax
from jax.experimental import pallas as pl
from jax.experimental.pallas import tpu as pltpu

F32 = jnp.float32
BF16 = jnp.bfloat16
HIGHEST = lax.Precision.HIGHEST

N_DEV = 8
D_MODEL = 1024
N_HEADS = 16
HEAD_DIM = 64
MLA_ROPE_HALF = 16
MLA_Q_RANK = 384
MLA_KV_RANK = 256
MLA_ROPE_DIM = 32
NORM_EPS = 1e-6
ROPE_BASE = 10000.0
ADAM_LR = 0.001
ADAM_B1 = 0.9
ADAM_B2 = 0.999
ADAM_EPS = 1e-08
ADAM_WD = 0.01
ADAM_STEP = 10

LANES = 128
VMEM_LIMIT_BYTES = 56 * 1024 * 1024
ROW_BLOCK = 512
ATT_BLOCK = 512
CONV_ROWS = 1024
MM_BM, MM_BN, MM_BK = 512, 1024, 2048
MM_K_WHOLE = 3328
LOG2E = 1.4426950408889634


def _params(*sem):
    return pltpu.CompilerParams(dimension_semantics=sem or None, vmem_limit_bytes=VMEM_LIMIT_BYTES)


def _blk(dim, pref):
    if dim <= pref:
        return dim
    b = pref - pref % LANES
    while b >= LANES:
        if dim % b == 0:
            return b
        b -= LANES
    raise ValueError(f"no block for {dim}")


def _row_blk(rows, pref):
    if rows <= pref:
        return rows
    for b in range(pref - pref % 8, 7, -8):
        if rows % b == 0:
            return b
    return rows


def _pad_axis(a, axis, mult):
    pad = (-a.shape[axis]) % mult
    if pad == 0:
        return a
    widths = [(0, 0)] * a.ndim
    widths[axis] = (0, pad)
    return jnp.pad(a, widths)


def _matmul(a, b, *, ta=False, tb=False, out_dtype=F32, col_scale=None, name):
    m, k = (a.shape[1], a.shape[0]) if ta else a.shape
    n = b.shape[0] if tb else b.shape[1]
    assert (b.shape[1] if tb else b.shape[0]) == k, (a.shape, b.shape, ta, tb)
    bm, bn = _blk(m, MM_BM), _blk(n, MM_BN)
    bk = k if k <= MM_K_WHOLE else _blk(k, MM_BK)
    nk = k // bk
    dims = (((0 if ta else 1,), (1 if tb else 0,)), ((), ()))
    has_scale = col_scale is not None
    use_acc = nk > 1 and (out_dtype != F32 or has_scale)

    def body(*refs):
        a_ref, b_ref = refs[0], refs[1]
        s_ref = refs[2] if has_scale else None
        o_ref = refs[3] if has_scale else refs[2]
        acc_ref = refs[-1] if use_acc else o_ref
        kk = pl.program_id(2)
        part = lax.dot_general(a_ref[...].astype(BF16), b_ref[...].astype(BF16), dims, preferred_element_type=F32)

        def finish(val):
            if has_scale:
                val = val * s_ref[...]
            o_ref[...] = val.astype(out_dtype)

        if nk == 1:
            finish(part)
            return

        @pl.when(kk == 0)
        def _():
            acc_ref[...] = part

        @pl.when(kk > 0)
        def _():
            acc_ref[...] += part

        if use_acc:
            @pl.when(kk == nk - 1)
            def _():
                finish(acc_ref[...])

    a_spec = pl.BlockSpec((bk, bm), lambda i, j, kk: (kk, i)) if ta else pl.BlockSpec((bm, bk), lambda i, j, kk: (i, kk))
    b_spec = pl.BlockSpec((bn, bk), lambda i, j, kk: (j, kk)) if tb else pl.BlockSpec((bk, bn), lambda i, j, kk: (kk, j))
    return pl.pallas_call(
        body, name=name, grid=(m // bm, n // bn, nk),
        in_specs=[a_spec, b_spec] + ([pl.BlockSpec((1, bn), lambda i, j, kk: (0, j))] if has_scale else []),
        out_specs=pl.BlockSpec((bm, bn), lambda i, j, kk: (i, j)),
        out_shape=jax.ShapeDtypeStruct((m, n), out_dtype),
        scratch_shapes=[pltpu.VMEM((bm, bn), F32)] if use_acc else [],
        compiler_params=_params("parallel", "parallel", "arbitrary"),
    )(*([a, b] + ([col_scale] if has_scale else [])))


def _norm_fwd(x, mul, add, *, plus_one, out_dtype, name):
    s, n = x.shape
    bs = _blk(s, ROW_BLOCK)

    def body(x_ref, m_ref, a_ref, o_ref):
        xv = x_ref[...]
        r = lax.rsqrt(jnp.mean(xv * xv, axis=-1, keepdims=True) + NORM_EPS)
        mv = m_ref[...] + 1.0 if plus_one else m_ref[...]
        o_ref[...] = (xv * r * mv + a_ref[...]).astype(out_dtype)

    row = pl.BlockSpec((bs, n), lambda i: (i, 0))
    vec = pl.BlockSpec((1, n), lambda i: (0, 0))
    return pl.pallas_call(
        body, name=name, grid=(s // bs,), in_specs=[row, vec, vec], out_specs=row,
        out_shape=jax.ShapeDtypeStruct((s, n), out_dtype), compiler_params=_params("parallel"),
    )(x, mul, add)


def _norm_bwd(x, mul, dy, dres, *, plus_one, name):
    s, n = x.shape
    bs = _blk(s, ROW_BLOCK)
    has_res = dres is not None

    def body(*refs):
        if has_res:
            x_ref, m_ref, dy_ref, dres_ref, dx_ref, dm_ref, da_ref = refs
        else:
            x_ref, m_ref, dy_ref, dx_ref, dm_ref, da_ref = refs
        xv = x_ref[...]
        dyv = dy_ref[...].astype(F32)
        r = lax.rsqrt(jnp.mean(xv * xv, axis=-1, keepdims=True) + NORM_EPS)
        xn = xv * r
        mv = m_ref[...] + 1.0 if plus_one else m_ref[...]
        g = dyv * mv
        dx = r * (g - xn * jnp.mean(g * xn, axis=-1, keepdims=True))
        if has_res:
            dx = dx + dres_ref[...]
        dx_ref[...] = dx

        @pl.when(pl.program_id(0) == 0)
        def _():
            dm_ref[...] = jnp.zeros_like(dm_ref)
            da_ref[...] = jnp.zeros_like(da_ref)

        dm_ref[...] += jnp.sum(dyv * xn, axis=0, keepdims=True)
        da_ref[...] += jnp.sum(dyv, axis=0, keepdims=True)

    row = pl.BlockSpec((bs, n), lambda i: (i, 0))
    vec = pl.BlockSpec((1, n), lambda i: (0, 0))
    ins = [x, mul, dy] + ([dres] if has_res else [])
    return pl.pallas_call(
        body, name=name, grid=(s // bs,),
        in_specs=[row, vec, row] + ([row] if has_res else []), out_specs=[row, vec, vec],
        out_shape=[jax.ShapeDtypeStruct((s, n), F32), jax.ShapeDtypeStruct((1, n), F32), jax.ShapeDtypeStruct((1, n), F32)],
        compiler_params=_params("arbitrary"),
    )(*ins)


def _resid_fwd(x, y, gate, *, name):
    s, n = x.shape
    bs = _blk(s, ROW_BLOCK)

    def body(x_ref, y_ref, g_ref, o_ref):
        o_ref[...] = x_ref[...] + g_ref[...] * y_ref[...]

    row = pl.BlockSpec((bs, n), lambda i: (i, 0))
    vec = pl.BlockSpec((1, n), lambda i: (0, 0))
    return pl.pallas_call(
        body, name=name, grid=(s // bs,), in_specs=[row, row, vec], out_specs=row,
        out_shape=jax.ShapeDtypeStruct((s, n), F32), compiler_params=_params("parallel"),
    )(x, y, gate)


def _resid_bwd(dx, y, gate, *, name):
    s, n = dx.shape
    bs = _blk(s, ROW_BLOCK)

    def body(dx_ref, y_ref, g_ref, dy_ref, dg_ref):
        dxv = dx_ref[...]
        dy_ref[...] = (g_ref[...] * dxv).astype(BF16)

        @pl.when(pl.program_id(0) == 0)
        def _():
            dg_ref[...] = jnp.zeros_like(dg_ref)

        dg_ref[...] += jnp.sum(dxv * y_ref[...], axis=0, keepdims=True)

    row = pl.BlockSpec((bs, n), lambda i: (i, 0))
    vec = pl.BlockSpec((1, n), lambda i: (0, 0))
    return pl.pallas_call(
        body, name=name, grid=(s // bs,), in_specs=[row, row, vec], out_specs=[row, vec],
        out_shape=[jax.ShapeDtypeStruct((s, n), BF16), jax.ShapeDtypeStruct((1, n), F32)],
        compiler_params=_params("arbitrary"),
    )(dx, y, gate)


def _final_loss(x, g, target, *, name):
    s, n = x.shape
    bs = _blk(s, ROW_BLOCK)

    def body(x_ref, g_ref, t_ref, loss_ref, dx_ref, dg_ref):
        xv = x_ref[...]
        r = lax.rsqrt(jnp.mean(xv * xv, axis=-1, keepdims=True) + NORM_EPS)
        xn = xv * r
        gv = g_ref[...]
        err = xn * gv - t_ref[...]
        dout = err * (1.0 / n)
        gg = dout * gv
        dx_ref[...] = r * (gg - xn * jnp.mean(gg * xn, axis=-1, keepdims=True))

        @pl.when(pl.program_id(0) == 0)
        def _():
            loss_ref[...] = jnp.zeros_like(loss_ref)
            dg_ref[...] = jnp.zeros_like(dg_ref)

        part = jnp.sum(jnp.sum(err * err, axis=-1, keepdims=True), axis=0, keepdims=True) * (0.5 / n)
        loss_ref[...] += jnp.broadcast_to(part, loss_ref.shape)
        dg_ref[...] += jnp.sum(dout * xn, axis=0, keepdims=True)

    row = pl.BlockSpec((bs, n), lambda i: (i, 0))
    vec = pl.BlockSpec((1, n), lambda i: (0, 0))
    return pl.pallas_call(
        body, name=name, grid=(s // bs,), in_specs=[row, vec, row],
        out_specs=[pl.BlockSpec((1, LANES), lambda i: (0, 0)), row, vec],
        out_shape=[jax.ShapeDtypeStruct((1, LANES), F32), jax.ShapeDtypeStruct((s, n), F32), jax.ShapeDtypeStruct((1, n), F32)],
        compiler_params=_params("arbitrary"),
    )(x, g, target)


def _lane_lt64(shape):
    return lax.broadcasted_iota(jnp.int32, shape, 1) < HEAD_DIM


def _keep_low(x):
    return jnp.where(_lane_lt64(x.shape), x.astype(F32), 0.0).astype(x.dtype)


def _keep_high(x):
    return jnp.where(_lane_lt64(x.shape), 0.0, x.astype(F32)).astype(x.dtype)


def _lane_merge(a, b):
    n = max(a.shape[0], b.shape[0])
    return jnp.where(_lane_lt64((n, LANES)), a, b)


def _pair(x, width, masked):
    if width == HEAD_DIM:
        return (_keep_low(x), _keep_high(x)) if masked else (x, x)
    return x[:, :LANES], x[:, LANES:]


def _qk_t(a, b):
    return lax.dot_general(a, b, (((1,), (1,)), ((), ())), preferred_element_type=F32)


def _attn_specs(s, blk, width, cols, resident):
    w = 2 * width
    if resident:
        return pl.BlockSpec((s, w), lambda p, i: (0, cols + p))
    return pl.BlockSpec((blk, w), lambda p, i: (i, cols + p))


def _attn_fwd(q, k, v, kb_row, *, qc, kc, vc, width, name):
    s = q.shape[0]
    blk = _blk(s, ATT_BLOCK)
    nb = s // blk
    has_bias = kb_row is not None

    def body(*refs):
        if has_bias:
            q_ref, k_ref, v_ref, kb_ref, o_ref, lse_ref = refs
        else:
            q_ref, k_ref, v_ref, o_ref, lse_ref = refs
        i = pl.program_id(1)
        qh = _pair(q_ref[...], width, True)

        def step(j, carry, diag):
            start = pl.multiple_of(j * blk, blk)
            kh = _pair(k_ref[pl.ds(start, blk), :], width, False)
            vj = v_ref[pl.ds(start, blk), :]
            out = []
            for hd in range(2):
                m, l, acc = carry[hd]
                sc = _qk_t(qh[hd], kh[hd])
                if has_bias:
                    sc = sc + kb_ref[hd, j]
                if diag:
                    sc = jnp.where(_causal_keep(blk), sc, -1e30)
                m_new = jnp.maximum(m, jnp.max(sc, axis=-1, keepdims=True))
                alpha = jnp.exp2(m - m_new)
                p = jnp.exp2(sc - m_new)
                l = alpha * l + jnp.sum(p, axis=-1, keepdims=True)
                acc = alpha * acc + jnp.dot(p.astype(BF16), vj, preferred_element_type=F32)
                out.append((m_new, l, acc))
            return tuple(out)

        one = (jnp.full((blk, 1), -1e30, F32), jnp.zeros((blk, 1), F32), jnp.zeros((blk, LANES), F32))
        carry = lax.fori_loop(0, i, lambda j, c: step(j, c, False), (one, one))
        (ma, la, acca), (mb, lb, accb) = step(i, carry, True)
        o_ref[...] = _lane_merge(acca / la, accb / lb)
        lse_ref[...] = _lane_merge(ma + jnp.log(la) * LOG2E, mb + jnp.log(lb) * LOG2E)

    ins = [q, k, v] + ([kb_row] if has_bias else [])
    out_spec = pl.BlockSpec((blk, LANES), lambda p, i: (i, p))
    return pl.pallas_call(
        body, name=name, grid=(N_HEADS // 2, nb),
        in_specs=[_attn_specs(s, blk, width, qc, False), _attn_specs(s, blk, width, kc, True),
                  _attn_specs(s, blk, HEAD_DIM, vc, True)]
                 + ([pl.BlockSpec((2, nb, 1, blk), lambda p, i: (p, 0, 0, 0))] if has_bias else []),
        out_specs=[out_spec, out_spec],
        out_shape=[jax.ShapeDtypeStruct((s, N_HEADS * HEAD_DIM), F32)] * 2,
        compiler_params=_params("parallel", "parallel"),
    )(*ins)


def _causal_keep(n):
    row = lax.broadcasted_iota(jnp.int32, (n, n), 0)
    col = lax.broadcasted_iota(jnp.int32, (n, n), 1)
    return col <= row


def _attn_bwd_dq(q, k, v, kb_row, o, do, lse, *, qc, kc, vc, width, dq_mult, out_dtype, name):
    s = q.shape[0]
    blk = _blk(s, ATT_BLOCK)
    nb = s // blk
    has_bias = kb_row is not None

    def body(*refs):
        if has_bias:
            q_ref, k_ref, v_ref, kb_ref, o_ref, do_ref, lse_ref, dq_ref, dl_ref, dr_ref = refs
        else:
            q_ref, k_ref, v_ref, o_ref, do_ref, lse_ref, dq_ref, dl_ref = refs
        i = pl.program_id(1)
        qh = _pair(q_ref[...], width, True)
        do2 = do_ref[...]
        prod = do2.astype(F32) * o_ref[...]
        low = _lane_lt64(prod.shape)
        delta = (jnp.sum(jnp.where(low, prod, 0.0), axis=-1, keepdims=True),
                 jnp.sum(jnp.where(low, 0.0, prod), axis=-1, keepdims=True))
        dl_ref[...] = _lane_merge(delta[0], delta[1])
        doh = (_keep_low(do2), _keep_high(do2))
        lse2 = lse_ref[...]
        lse_h = (lse2[:, 0:1], lse2[:, HEAD_DIM:HEAD_DIM + 1])

        def step(j, carry, diag):
            start = pl.multiple_of(j * blk, blk)
            kh = _pair(k_ref[pl.ds(start, blk), :], width, False)
            vj = v_ref[pl.ds(start, blk), :]
            out = []
            for hd in range(2):
                acc, rsum = carry[hd]
                sc = _qk_t(qh[hd], kh[hd])
                if has_bias:
                    sc = sc + kb_ref[hd, j]
                if diag:
                    sc = jnp.where(_causal_keep(blk), sc, -1e30)
                p = jnp.exp2(sc - lse_h[hd])
                ds = p * (_qk_t(doh[hd], vj) - delta[hd])
                if has_bias:
                    rsum = rsum + jnp.sum(ds, axis=-1, keepdims=True)
                out.append((acc + jnp.dot(ds.astype(BF16), kh[hd], preferred_element_type=F32), rsum))
            return tuple(out)

        one = (jnp.zeros((blk, LANES), F32), jnp.zeros((blk, 1), F32))
        carry = lax.fori_loop(0, i, lambda j, c: step(j, c, False), (one, one))
        (acca, ra), (accb, rb) = step(i, carry, True)
        if width == HEAD_DIM:
            dq = _lane_merge(acca, accb)
        else:
            dq = jnp.concatenate([acca, accb], axis=1)
        dq_ref[...] = (dq * dq_mult).astype(out_dtype)
        if has_bias:
            dr_ref[...] = _lane_merge(ra, rb)

    stat = pl.BlockSpec((blk, LANES), lambda p, i: (i, p))
    ins = [q, k, v] + ([kb_row] if has_bias else []) + [o, do, lse]
    return pl.pallas_call(
        body, name=name, grid=(N_HEADS // 2, nb),
        in_specs=[_attn_specs(s, blk, width, qc, False), _attn_specs(s, blk, width, kc, True),
                  _attn_specs(s, blk, HEAD_DIM, vc, True)]
                 + ([pl.BlockSpec((2, nb, 1, blk), lambda p, i: (p, 0, 0, 0))] if has_bias else []) + [stat, stat, stat],
        out_specs=[pl.BlockSpec((blk, 2 * width), lambda p, i: (i, p)), stat] + ([stat] if has_bias else []),
        out_shape=[jax.ShapeDtypeStruct((s, N_HEADS * width), out_dtype)]
                  + [jax.ShapeDtypeStruct((s, N_HEADS * HEAD_DIM), F32)] * (2 if has_bias else 1),
        compiler_params=_params("parallel", "parallel"),
    )(*ins)


def _attn_bwd_dkv(q, k, v, kb_col, do, lse_row, delta_row, *, qc, kc, vc, width, dk_mult, out_dtype, name):
    s = q.shape[0]
    blk = _blk(s, ATT_BLOCK)
    nb = s // blk
    has_bias = kb_col is not None

    def body(*refs):
        if has_bias:
            q_ref, k_ref, v_ref, kb_ref, do_ref, lse_ref, dl_ref, dk_ref, dv_ref, db_ref = refs
        else:
            q_ref, k_ref, v_ref, do_ref, lse_ref, dl_ref, dk_ref, dv_ref, db_ref = refs
        j = pl.program_id(1)
        kh = _pair(k_ref[...], width, True)
        v2 = v_ref[...]
        vh = (_keep_low(v2), _keep_high(v2))
        if has_bias:
            kb2 = kb_ref[0]
            kbh = (kb2[:, 0:1], kb2[:, 1:2])

        def step(i, carry, diag):
            start = pl.multiple_of(i * blk, blk)
            qh = _pair(q_ref[pl.ds(start, blk), :], width, False)
            doi = do_ref[pl.ds(start, blk), :]
            out = []
            for hd in range(2):
                dk, dvv, db = carry[hd]
                st = _qk_t(kh[hd], qh[hd])
                if has_bias:
                    st = st + kbh[hd]
                if diag:
                    row = lax.broadcasted_iota(jnp.int32, (blk, blk), 0)
                    colq = lax.broadcasted_iota(jnp.int32, (blk, blk), 1)
                    st = jnp.where(row <= colq, st, -1e30)
                pt = jnp.exp2(st - lse_ref[hd, i])
                dvv = dvv + jnp.dot(pt.astype(BF16), doi, preferred_element_type=F32)
                dst = pt * (_qk_t(vh[hd], doi) - dl_ref[hd, i])
                dk = dk + jnp.dot(dst.astype(BF16), qh[hd], preferred_element_type=F32)
                db = db + jnp.sum(dst, axis=-1, keepdims=True)
                out.append((dk, dvv, db))
            return tuple(out)

        one = (jnp.zeros((blk, LANES), F32), jnp.zeros((blk, LANES), F32), jnp.zeros((blk, 1), F32))
        carry = step(j, (one, one), True)
        (dka, dva, dba), (dkb, dvb, dbb) = lax.fori_loop(j + 1, nb, lambda i, c: step(i, c, False), carry)
        if width == HEAD_DIM:
            dk = _lane_merge(dka, dkb)
        else:
            dk = jnp.concatenate([dka, dkb], axis=1)
        dk_ref[...] = (dk * dk_mult).astype(out_dtype)
        dv_ref[...] = _lane_merge(dva, dvb).astype(out_dtype)
        db_ref[...] = _lane_merge(dba, dbb)

    stat = pl.BlockSpec((blk, LANES), lambda p, jj: (jj, p))
    rows = pl.BlockSpec((2, nb, 1, blk), lambda p, jj: (p, 0, 0, 0))
    ins = [q, k, v] + ([kb_col] if has_bias else []) + [do, lse_row, delta_row]
    return pl.pallas_call(
        body, name=name, grid=(N_HEADS // 2, nb),
        in_specs=[_attn_specs(s, blk, width, qc, True), _attn_specs(s, blk, width, kc, False),
                  _attn_specs(s, blk, HEAD_DIM, vc, False)]
                 + ([pl.BlockSpec((1, blk, 2), lambda p, jj: (p, jj, 0))] if has_bias else [])
                 + [pl.BlockSpec((s, LANES), lambda p, jj: (0, p)), rows, rows],
        out_specs=[pl.BlockSpec((blk, 2 * width), lambda p, jj: (jj, p)), stat, stat],
        out_shape=[jax.ShapeDtypeStruct((s, N_HEADS * width), out_dtype), jax.ShapeDtypeStruct((s, N_HEADS * HEAD_DIM), out_dtype),
                   jax.ShapeDtypeStruct((s, N_HEADS * HEAD_DIM), F32)],
        compiler_params=_params("parallel", "parallel"),
    )(*ins)


def _head_stat(t):
    return t[:, ::HEAD_DIM]


def _stat_rows(t16, blk):
    s = t16.shape[0]
    return t16.T.reshape(N_HEADS, s // blk, 1, blk)


def _attention(q, k, v, bias, *, qc, kc, vc, width, name):
    blk = _blk(q.shape[0], ATT_BLOCK)
    kb_row = None if bias is None else _stat_rows(bias, blk)
    o, lse = _attn_fwd(q, k, v, kb_row, qc=qc, kc=kc, vc=vc, width=width, name=name + "_fwd")
    return o, (q, k, v, bias, o, lse)


def _attention_bwd(res, do, *, qc, kc, vc, width, dq_mult, dk_mult, out_dtype, name):
    q, k, v, bias, o, lse = res
    s = q.shape[0]
    blk = _blk(s, ATT_BLOCK)
    cfg = dict(qc=qc, kc=kc, vc=vc, width=width, out_dtype=out_dtype)
    kb_row = None if bias is None else _stat_rows(bias, blk)
    kb_col = None if bias is None else bias.reshape(s, N_HEADS // 2, 2).transpose(1, 0, 2)
    outs = _attn_bwd_dq(q, k, v, kb_row, o, do, lse, dq_mult=dq_mult, name=name + "_dq", **cfg)
    dq, delta = outs[0], outs[1]
    lse_row = _stat_rows(_head_stat(lse), blk)
    delta_row = _stat_rows(_head_stat(delta), blk)
    dk, dv, dcol = _attn_bwd_dkv(q, k, v, kb_col, do, lse_row, delta_row, dk_mult=dk_mult, name=name + "_dkv", **cfg)
    if bias is None:
        return dq, dk, dv, None
    return dq, dk, dv, _head_stat(outs[2]) - _head_stat(dcol)


def _fox_gate_fwd(fl, bf, *, name):
    s, n = fl.shape
    bs = _blk(s, ROW_BLOCK)

    def body(fl_ref, bf_ref, cum_ref, carry_ref):
        @pl.when(pl.program_id(0) == 0)
        def _():
            carry_ref[...] = jnp.zeros_like(carry_ref)

        z = fl_ref[...] + bf_ref[...]
        lf = jnp.minimum(z, 0.0) - jnp.log1p(jnp.exp(-jnp.abs(z)))
        row = lax.broadcasted_iota(jnp.int32, (bs, bs), 0)
        col = lax.broadcasted_iota(jnp.int32, (bs, bs), 1)
        tri = (col <= row).astype(F32)
        cum_ref[...] = jnp.dot(tri, lf, preferred_element_type=F32, precision=HIGHEST) + carry_ref[...]
        carry_ref[...] += jnp.sum(lf, axis=0, keepdims=True)

    return pl.pallas_call(
        body, name=name, grid=(s // bs,),
        in_specs=[pl.BlockSpec((bs, n), lambda i: (i, 0)), pl.BlockSpec((1, n), lambda i: (0, 0))],
        out_specs=pl.BlockSpec((bs, n), lambda i: (i, 0)),
        out_shape=jax.ShapeDtypeStruct((s, n), F32), scratch_shapes=[pltpu.VMEM((1, n), F32)],
        compiler_params=_params("arbitrary"),
    )(fl, bf)


def _fox_gate_bwd(fl, bf, dcum, *, name):
    s, n = fl.shape
    bs = _blk(s, ROW_BLOCK)
    nb = s // bs

    def body(fl_ref, bf_ref, dc_ref, dz_ref, dbf_ref, carry_ref):
        @pl.when(pl.program_id(0) == 0)
        def _():
            carry_ref[...] = jnp.zeros_like(carry_ref)
            dbf_ref[...] = jnp.zeros_like(dbf_ref)

        dc = dc_ref[...]
        row = lax.broadcasted_iota(jnp.int32, (bs, bs), 0)
        col = lax.broadcasted_iota(jnp.int32, (bs, bs), 1)
        tri = (col >= row).astype(F32)
        dlf = jnp.dot(tri, dc, preferred_element_type=F32, precision=HIGHEST) + carry_ref[...]
        carry_ref[...] += jnp.sum(dc, axis=0, keepdims=True)
        z = fl_ref[...] + bf_ref[...]
        dz = dlf / (1.0 + jnp.exp(z))
        dz_ref[...] = dz
        dbf_ref[...] += jnp.sum(dz, axis=0, keepdims=True)

    rev = pl.BlockSpec((bs, n), lambda i: (nb - 1 - i, 0))
    vec = pl.BlockSpec((1, n), lambda i: (0, 0))
    return pl.pallas_call(
        body, name=name, grid=(nb,), in_specs=[rev, vec, rev], out_specs=[rev, vec],
        out_shape=[jax.ShapeDtypeStruct((s, n), F32), jax.ShapeDtypeStruct((1, n), F32)],
        scratch_shapes=[pltpu.VMEM((1, n), F32)], compiler_params=_params("arbitrary"),
    )(fl, bf, dcum)


def _rope(x1, x2, cos, sin, *, negate, name):
    s, n = x1.shape
    bs = _blk(s, ROW_BLOCK)

    def body(a_ref, b_ref, c_ref, s_ref, o1_ref, o2_ref):
        a, b, cv = a_ref[...], b_ref[...], c_ref[...]
        sv = -s_ref[...] if negate else s_ref[...]
        o1_ref[...] = a * cv - b * sv
        o2_ref[...] = b * cv + a * sv

    row = pl.BlockSpec((bs, n), lambda i: (i, 0))
    return pl.pallas_call(
        body, name=name, grid=(s // bs,), in_specs=[row] * 4, out_specs=[row, row],
        out_shape=[jax.ShapeDtypeStruct((s, n), F32)] * 2, compiler_params=_params("parallel"),
    )(x1, x2, cos, sin)


def _rope_heads(x, ta, tb, tc, *, out_dtype, name):
    s, n = x.shape
    bs = _blk(s, ROW_BLOCK)

    def body(x_ref, a_ref, b_ref, c_ref, o_ref):
        av, bv, cv = a_ref[...], b_ref[...], c_ref[...]
        for g in range(n // LANES):
            xg = x_ref[:, g * LANES:(g + 1) * LANES]
            og = xg * av + pltpu.roll(xg, LANES - MLA_ROPE_HALF, 1) * bv + pltpu.roll(xg, MLA_ROPE_HALF, 1) * cv
            o_ref[:, g * LANES:(g + 1) * LANES] = og.astype(out_dtype)

    row = pl.BlockSpec((bs, n), lambda i: (i, 0))
    tab = pl.BlockSpec((bs, LANES), lambda i: (i, 0))
    return pl.pallas_call(
        body, name=name, grid=(s // bs,), in_specs=[row, tab, tab, tab], out_specs=row,
        out_shape=jax.ShapeDtypeStruct((s, n), out_dtype), compiler_params=_params("parallel"),
    )(x, ta, tb, tc)


def _group_sum(x, *, name):
    s, n = x.shape
    bs = _blk(s, ROW_BLOCK)

    def body(x_ref, o_ref):
        acc = x_ref[:, 0:LANES]
        for g in range(1, n // LANES):
            acc = acc + x_ref[:, g * LANES:(g + 1) * LANES]
        o_ref[...] = acc

    return pl.pallas_call(
        body, name=name, grid=(s // bs,), in_specs=[pl.BlockSpec((bs, n), lambda i: (i, 0))],
        out_specs=pl.BlockSpec((bs, LANES), lambda i: (i, 0)),
        out_shape=jax.ShapeDtypeStruct((s, LANES), F32), compiler_params=_params("parallel"),
    )(x)


def _shift_down(x, k):
    return pltpu.roll(x, k, 0)


def _conv_rows(ext, w_ref, b_ref, rows):
    y = b_ref[...] + w_ref[0:1, :] * _shift_down(ext, 2) + w_ref[1:2, :] * _shift_down(ext, 1) + w_ref[2:3, :] * ext
    return y[8:8 + rows]


def _conv_gate_fwd(u, cw, cb, *, name):
    s, f2 = u.shape
    f = f2 // 2
    nf = f // LANES
    r = _blk(s, CONV_ROWS)
    r8 = r // 8

    def body(ug_ref, ugp_ref, uv_ref, uvp_ref, wg_ref, wv_ref, bg_ref, bv_ref, o_ref):
        first = pl.program_id(1) == 0

        def conv(cur_ref, prev_ref, w_ref, b_ref):
            prev = jnp.where(first, 0.0, prev_ref[...])
            return _conv_rows(jnp.concatenate([prev, cur_ref[...]], axis=0), w_ref, b_ref, r)

        yg = conv(ug_ref, ugp_ref, wg_ref, bg_ref)
        yv = conv(uv_ref, uvp_ref, wv_ref, bv_ref)
        o_ref[...] = (yg * jax.nn.sigmoid(yg) * yv).astype(BF16)

    def cur(off):
        return pl.BlockSpec((r, LANES), lambda c, i: (i, c + off))

    def prev(off):
        return pl.BlockSpec((8, LANES), lambda c, i: (jnp.maximum(i * r8 - 1, 0), c + off))

    def wspec(rows, off):
        return pl.BlockSpec((rows, LANES), lambda c, i: (0, c + off))

    return pl.pallas_call(
        body, name=name, grid=(nf, s // r),
        in_specs=[cur(0), prev(0), cur(nf), prev(nf), wspec(3, 0), wspec(3, nf), wspec(1, 0), wspec(1, nf)],
        out_specs=pl.BlockSpec((r, LANES), lambda c, i: (i, c)),
        out_shape=jax.ShapeDtypeStruct((s, f), BF16), compiler_params=_params("parallel", "parallel"),
    )(u, u, u, u, cw, cw, cb, cb)


def _conv_gate_bwd(u, cw, cb, dg, *, name):
    s, f2 = u.shape
    f = f2 // 2
    nf = f // LANES
    r = _blk(s, CONV_ROWS)
    r8 = r // 8
    nr = s // r

    def body(ug_ref, ugp_ref, ugn_ref, uv_ref, uvp_ref, uvn_ref, wg_ref, wv_ref, bg_ref, bv_ref, dg_ref, dgn_ref,
             dug_ref, duv_ref, dwg_ref, dwv_ref, dbg_ref, dbv_ref):
        i = pl.program_id(1)
        first, last = i == 0, i == nr - 1

        def ext_of(cur_ref, prev_ref, next_ref):
            prev = jnp.where(first, 0.0, prev_ref[...])
            return jnp.concatenate([prev, cur_ref[...], next_ref[...]], axis=0)

        eg, ev = ext_of(ug_ref, ugp_ref, ugn_ref), ext_of(uv_ref, uvp_ref, uvn_ref)
        yg = _conv_rows(eg, wg_ref, bg_ref, r + 8)
        yv = _conv_rows(ev, wv_ref, bv_ref, r + 8)
        dgn = jnp.where(last, 0.0, dgn_ref[...])
        dgx = jnp.concatenate([dg_ref[...], dgn], axis=0)
        sg = jax.nn.sigmoid(yg)
        dyg = dgx * yv * (sg * (1.0 + yg * (1.0 - sg)))
        dyv = dgx * (yg * sg)

        @pl.when(i == 0)
        def _():
            for ref in (dwg_ref, dwv_ref, dbg_ref, dbv_ref):
                ref[...] = jnp.zeros_like(ref)

        def grads(dy, ext, w_ref, du_ref, dw_ref, db_ref):
            n = r + 8
            du = w_ref[2:3, :] * dy + w_ref[1:2, :] * pltpu.roll(dy, n - 1, 0) + w_ref[0:1, :] * pltpu.roll(dy, n - 2, 0)
            du_ref[...] = du[0:r].astype(BF16)
            dyc = dy[0:r]
            db_ref[...] += jnp.sum(dyc, axis=0, keepdims=True)
            ext_c = ext[0:r + 8]
            dw_ref[0:1, :] += jnp.sum(dyc * _shift_down(ext_c, 2)[8:], axis=0, keepdims=True)
            dw_ref[1:2, :] += jnp.sum(dyc * _shift_down(ext_c, 1)[8:], axis=0, keepdims=True)
            dw_ref[2:3, :] += jnp.sum(dyc * ext_c[8:], axis=0, keepdims=True)

        grads(dyg, eg, wg_ref, dug_ref, dwg_ref, dbg_ref)
        grads(dyv, ev, wv_ref, duv_ref, dwv_ref, dbv_ref)

    def cur(off):
        return pl.BlockSpec((r, LANES), lambda c, i: (i, c + off))

    def prev(off):
        return pl.BlockSpec((8, LANES), lambda c, i: (jnp.maximum(i * r8 - 1, 0), c + off))

    def nxt(off):
        return pl.BlockSpec((8, LANES), lambda c, i: (jnp.minimum((i + 1) * r8, s // 8 - 1), c + off))

    def wspec(rows, off):
        return pl.BlockSpec((rows, LANES), lambda c, i: (0, c + off))

    outs = pl.pallas_call(
        body, name=name, grid=(nf, nr),
        in_specs=[cur(0), prev(0), nxt(0), cur(nf), prev(nf), nxt(nf), wspec(3, 0), wspec(3, nf), wspec(1, 0), wspec(1, nf),
                  cur(0), nxt(0)],
        out_specs=[cur(0), cur(0), wspec(3, 0), wspec(3, 0), wspec(1, 0), wspec(1, 0)],
        out_shape=[jax.ShapeDtypeStruct((s, f), BF16), jax.ShapeDtypeStruct((s, f), BF16),
                   jax.ShapeDtypeStruct((3, f), F32), jax.ShapeDtypeStruct((3, f), F32),
                   jax.ShapeDtypeStruct((1, f), F32), jax.ShapeDtypeStruct((1, f), F32)],
        compiler_params=_params("parallel", "arbitrary"),
    )(u, u, u, u, u, u, cw, cw, cb, cb, dg, dg)
    dug, duv, dwg, dwv, dbg, dbv = outs
    return jnp.concatenate([dug, duv], axis=1), jnp.concatenate([dwg, dwv], axis=1), jnp.concatenate([dbg, dbv], axis=1)


def _adamw(w, g, m, v, *, slabs, name):
    shape = w.shape
    cols = shape[-1]
    rows = w.size // cols
    w2, m2, v2 = (t.reshape(rows, cols) for t in (w, m, v))
    g2 = g.reshape((N_DEV, rows, cols) if slabs else (rows, cols))
    br = _row_blk(rows, ROW_BLOCK // 2 if slabs else ROW_BLOCK)

    def body(w_ref, g_ref, m_ref, v_ref, go_ref, d_ref, nm_ref, nv_ref):
        if slabs:
            gv = g_ref[0].astype(F32)
            for p in range(1, N_DEV):
                gv = gv + g_ref[p].astype(F32)
        else:
            gv = g_ref[...]
        nm = ADAM_B1 * m_ref[...] + (1.0 - ADAM_B1) * gv
        nv = ADAM_B2 * v_ref[...] + (1.0 - ADAM_B2) * (gv * gv)
        m_hat = nm / (1.0 - ADAM_B1 ** ADAM_STEP)
        v_hat = nv / (1.0 - ADAM_B2 ** ADAM_STEP)
        go_ref[...] = gv
        d_ref[...] = -ADAM_LR * (m_hat / (jnp.sqrt(v_hat) + ADAM_EPS) + ADAM_WD * w_ref[...])
        nm_ref[...] = nm
        nv_ref[...] = nv

    spec = pl.BlockSpec((br, cols), lambda i: (i, 0))
    gspec = pl.BlockSpec((N_DEV, br, cols), lambda i: (0, i, 0)) if slabs else spec
    outs = pl.pallas_call(
        body, name=name, grid=(rows // br,), in_specs=[spec, gspec, spec, spec], out_specs=[spec] * 4,
        out_shape=[jax.ShapeDtypeStruct((rows, cols), F32)] * 4, compiler_params=_params("parallel"),
    )(w2, g2, m2, v2)
    return tuple(t.reshape(shape) for t in outs)


def _exchange(xs, *, same_src, name):
    n = len(xs)
    slabs = [x.shape if same_src else x.shape[1:] for x in xs]

    def body(*refs):
        x_refs, o_refs = refs[:n], refs[n:2 * n]
        send_sems, recv_sems, loc_sems = refs[2 * n:]
        ix, iy, ic = lax.axis_index("x"), lax.axis_index("y"), lax.axis_index("c")
        me = 4 * ix + 2 * iy + ic
        local, sends, recvs = [], [], []
        for a in range(n):
            def src(p, a=a):
                return x_refs[a] if same_src else x_refs[a].at[p]

            local.append(pltpu.make_async_copy(src(me), o_refs[a].at[me], loc_sems.at[a]))
            for k in (1, 2, 4, 3, 5, 6, 7):
                px = 1 - ix if k & 4 else ix
                py = 1 - iy if k & 2 else iy
                pc = 1 - ic if k & 1 else ic
                p = 4 * px + 2 * py + pc
                for dst, out in ((me, sends), (p, recvs)):
                    out.append(pltpu.make_async_remote_copy(
                        src_ref=src(p), dst_ref=o_refs[a].at[dst], send_sem=send_sems.at[a, k - 1],
                        recv_sem=recv_sems.at[a, k - 1], device_id=(px, py, pc), device_id_type=pl.DeviceIdType.MESH))
        for cp in local + sends:
            cp.start()
        for cp in recvs:
            cp.wait_recv()
        for cp in sends:
            cp.wait_send()
        for cp in local:
            cp.wait()

    return pl.pallas_call(
        body, name=name,
        in_specs=[pl.BlockSpec(memory_space=pl.ANY)] * n, out_specs=[pl.BlockSpec(memory_space=pl.ANY)] * n,
        out_shape=[jax.ShapeDtypeStruct((N_DEV,) + tuple(sl), x.dtype) for sl, x in zip(slabs, xs)],
        scratch_shapes=[pltpu.SemaphoreType.DMA((n, N_DEV - 1)), pltpu.SemaphoreType.DMA((n, N_DEV - 1)),
                        pltpu.SemaphoreType.DMA((n,))],
        compiler_params=pltpu.CompilerParams(has_side_effects=True, vmem_limit_bytes=VMEM_LIMIT_BYTES),
    )(*xs)


def _sum_slabs(x, *, name):
    n, r, c = x.shape
    br = _row_blk(r, ROW_BLOCK)

    def body(x_ref, o_ref):
        acc = x_ref[0]
        for p in range(1, n):
            acc = acc + x_ref[p]
        o_ref[...] = acc

    return pl.pallas_call(
        body, name=name, grid=(r // br,), in_specs=[pl.BlockSpec((n, br, c), lambda i: (0, i, 0))],
        out_specs=pl.BlockSpec((br, c), lambda i: (i, 0)),
        out_shape=jax.ShapeDtypeStruct((r, c), F32), compiler_params=_params("parallel"),
    )(x)


def _silu(x, *, name):
    def body(x_ref, o_ref):
        xv = x_ref[...]
        o_ref[...] = (xv * jax.nn.sigmoid(xv)).astype(BF16)

    return pl.pallas_call(body, name=name, out_shape=jax.ShapeDtypeStruct(x.shape, BF16),
                          compiler_params=_params())(x)


_BIG = {"fox_w_in": 2, "fox_w_o": 1, "mla_w_a": 1, "mla_w_uq": 2, "mla_w_ukv": 2, "mla_w_o": 1, "ffn_w_in": 2, "ffn_w_out": 1}
_SMALL = {"mla_g_q": 1, "mla_g_kv": 1, "ffn_conv_w": 2}
_REPL = ("fox_b_f", "ffn_conv_b", "final_g")


def _gathered_to_full(g, axis):
    full = jnp.moveaxis(g, 0, axis)
    shape = list(full.shape)
    shape[axis:axis + 2] = [shape[axis] * shape[axis + 1]]
    return full.reshape(shape)


def _full_to_chunks(full, axis):
    shape = list(full.shape)
    shape[axis:axis + 1] = [N_DEV, shape[axis] // N_DEV]
    return jnp.moveaxis(full.reshape(shape), axis, 0)


def _per_head(parts, s_or_rows):
    return jnp.concatenate([p.reshape(s_or_rows, N_HEADS, -1) for p in parts], axis=-1).reshape(s_or_rows, -1)


def kernel(x, c, ada_w, ada_b, fox_w_in, fox_b_f, fox_w_o, mla_w_a, mla_g_q, mla_g_kv, mla_w_uq, mla_w_ukv, mla_w_o, ffn_w_in, ffn_conv_w, ffn_conv_b, ffn_w_out, final_g, loss_target, m_ada_w, m_ada_b, m_fox_w_in, m_fox_b_f, m_fox_w_o, m_mla_w_a, m_mla_g_q, m_mla_g_kv, m_mla_w_uq, m_mla_w_ukv, m_mla_w_o, m_ffn_w_in, m_ffn_conv_w, m_ffn_conv_b, m_ffn_w_out, m_final_g, v_ada_w, v_ada_b, v_fox_w_in, v_fox_b_f, v_fox_w_o, v_mla_w_a, v_mla_g_q, v_mla_g_kv, v_mla_w_uq, v_mla_w_ukv, v_mla_w_o, v_ffn_w_in, v_ffn_conv_w, v_ffn_conv_b, v_ffn_w_out, v_final_g):
    weights = dict(ada_w=ada_w, ada_b=ada_b, fox_w_in=fox_w_in, fox_b_f=fox_b_f, fox_w_o=fox_w_o, mla_w_a=mla_w_a,
                   mla_g_q=mla_g_q, mla_g_kv=mla_g_kv, mla_w_uq=mla_w_uq, mla_w_ukv=mla_w_ukv, mla_w_o=mla_w_o,
                   ffn_w_in=ffn_w_in, ffn_conv_w=ffn_conv_w, ffn_conv_b=ffn_conv_b, ffn_w_out=ffn_w_out, final_g=final_g)
    mom_m = dict(ada_w=m_ada_w, ada_b=m_ada_b, fox_w_in=m_fox_w_in, fox_b_f=m_fox_b_f, fox_w_o=m_fox_w_o, mla_w_a=m_mla_w_a,
                 mla_g_q=m_mla_g_q, mla_g_kv=m_mla_g_kv, mla_w_uq=m_mla_w_uq, mla_w_ukv=m_mla_w_ukv, mla_w_o=m_mla_w_o,
                 ffn_w_in=m_ffn_w_in, ffn_conv_w=m_ffn_conv_w, ffn_conv_b=m_ffn_conv_b, ffn_w_out=m_ffn_w_out, final_g=m_final_g)
    mom_v = dict(ada_w=v_ada_w, ada_b=v_ada_b, fox_w_in=v_fox_w_in, fox_b_f=v_fox_b_f, fox_w_o=v_fox_w_o, mla_w_a=v_mla_w_a,
                 mla_g_q=v_mla_g_q, mla_g_kv=v_mla_g_kv, mla_w_uq=v_mla_w_uq, mla_w_ukv=v_mla_w_ukv, mla_w_o=v_mla_w_o,
                 ffn_w_in=v_ffn_w_in, ffn_conv_w=v_ffn_conv_w, ffn_conv_b=v_ffn_conv_b, ffn_w_out=v_ffn_w_out, final_g=v_final_g)
    order = list(weights)
    x0 = x[0]
    target = loss_target[0]
    s = x0.shape[0]
    d = D_MODEL
    cols = ada_w.shape[-1]
    nq = N_HEADS * HEAD_DIM

    small_names = ["c"] + list(_SMALL)
    small_all = dict(zip(small_names, _exchange([c] + [weights[n] for n in _SMALL], same_src=True, name="gather_small")))
    c_all = small_all["c"].reshape(N_DEV, d)
    g_q = _gathered_to_full(small_all["mla_g_q"], 1)
    g_kv = _gathered_to_full(small_all["mla_g_kv"], 1)
    conv_w = _gathered_to_full(small_all["ffn_conv_w"], 2)

    c_pad = _pad_axis(c_all, 0, LANES)
    silu_c = _silu(c_pad, name="silu_c")
    w_ada = ada_w.reshape(4, d, cols)
    b_ada = ada_b.reshape(4, 1, cols)
    mods = [_matmul(silu_c, w_ada[i], name=f"ada_mod{i}")[:N_DEV] + b_ada[i] for i in range(4)]
    mod_send = _pad_axis(jnp.stack(mods, axis=1), 1, 8)
    mod_recv, = _exchange([mod_send], same_src=False, name="scatter_mod")
    mod = mod_recv[:, :4].transpose(1, 0, 2).reshape(4, 3 * d)
    shift = [mod[i:i + 1, 0:d] for i in range(4)]
    scale = [mod[i:i + 1, d:2 * d] for i in range(4)]
    gate = [mod[i:i + 1, 2 * d:3 * d] for i in range(4)]

    big_all = _exchange([weights[n].astype(BF16) for n in _BIG], same_src=True, name="gather_weights")
    wfull = {n: _gathered_to_full(g, _BIG[n]) for n, g in zip(_BIG, big_all)}

    w_fox_in = _pad_axis(wfull["fox_w_in"][0], 1, LANES)
    w_fox_qkv, w_fox_f = w_fox_in[:, :3 * nq], w_fox_in[:, 3 * nq:]
    w_fox_o = wfull["fox_w_o"][0]
    w_a = _pad_axis(wfull["mla_w_a"][0], 1, LANES)
    wq = wfull["mla_w_uq"][0].reshape(MLA_Q_RANK, N_HEADS, HEAD_DIM + MLA_ROPE_DIM)
    w_uq = _pad_axis(wq, 2, LANES).reshape(MLA_Q_RANK, N_HEADS * LANES)
    wkv = wfull["mla_w_ukv"][0].reshape(MLA_KV_RANK, N_HEADS, 2 * HEAD_DIM)
    w_ukv = jnp.concatenate([wkv[:, :, :HEAD_DIM].reshape(MLA_KV_RANK, -1), wkv[:, :, HEAD_DIM:].reshape(MLA_KV_RANK, -1)], axis=1)
    w_mla_o = wfull["mla_w_o"][0]
    w_ffn_in = wfull["ffn_w_in"]
    w_ffn_out = wfull["ffn_w_out"]
    conv_b = ffn_conv_b

    fox_scale = HEAD_DIM ** -0.5
    mla_scale = (HEAD_DIM + MLA_ROPE_DIM) ** -0.5
    pos = jnp.arange(s, dtype=F32)
    inv_freq = ROPE_BASE ** (-jnp.arange(0, MLA_ROPE_DIM, 2, dtype=F32) / MLA_ROPE_DIM)
    ang = pos[:, None] * inv_freq[None, :]
    cos16, sin16 = jnp.cos(ang), jnp.sin(ang)
    z16, z32, z64 = jnp.zeros((s, 16), F32), jnp.zeros((s, 32), F32), jnp.zeros((s, 64), F32)
    tab_a = jnp.concatenate([jnp.ones((s, 64), F32), cos16, cos16, z32], axis=1) * (mla_scale * LOG2E)
    tab_b = jnp.concatenate([z64, -sin16, z16, z32], axis=1) * (mla_scale * LOG2E)
    tab_c = jnp.concatenate([z64, z16, sin16, z32], axis=1) * (mla_scale * LOG2E)

    h0 = _norm_fwd(x0, scale[0], shift[0], plus_one=True, out_dtype=BF16, name="ada_fwd0")
    q_mult = jnp.concatenate([jnp.full((1, nq), fox_scale * LOG2E, F32), jnp.ones((1, 2 * nq), F32)], axis=1)
    qkv = _matmul(h0, w_fox_qkv, out_dtype=BF16, col_scale=q_mult, name="fox_proj")
    fl = _matmul(h0, w_fox_f, name="fox_proj_f")[:, :N_HEADS]
    cum = _fox_gate_fwd(fl, fox_b_f, name="fox_gate_fwd")
    fox_cfg = dict(qc=0, kc=N_HEADS // 2, vc=N_HEADS, width=HEAD_DIM)
    fo, fox_res = _attention(qkv, qkv, qkv, cum * -LOG2E, name="fox_attn", **fox_cfg)
    y0 = _matmul(fo, w_fox_o, name="fox_out")
    x1 = _resid_fwd(x0, y0, gate[0], name="resid_fwd0")

    def ffn_fwd(xin, li, sub):
        hh = _norm_fwd(xin, scale[sub], shift[sub], plus_one=True, out_dtype=BF16, name=f"ada_fwd{sub}")
        u = _matmul(hh, w_ffn_in[li], name=f"ffn_up{li}")
        g = _conv_gate_fwd(u, conv_w[li], conv_b[li:li + 1], name=f"conv_fwd{li}")
        y = _matmul(g, w_ffn_out[li], name=f"ffn_down{li}")
        return _resid_fwd(xin, y, gate[sub], name=f"resid_fwd{sub}"), (hh, u, g, y)

    x2, ffn0_res = ffn_fwd(x1, 0, 1)

    h2 = _norm_fwd(x2, scale[2], shift[2], plus_one=True, out_dtype=BF16, name="ada_fwd2")
    a = _matmul(h2, w_a, name="mla_a")
    a_q, a_kv = a[:, :MLA_Q_RANK], a[:, MLA_Q_RANK:MLA_Q_RANK + MLA_KV_RANK]
    kr1 = a[:, MLA_Q_RANK + MLA_KV_RANK:MLA_Q_RANK + MLA_KV_RANK + MLA_ROPE_HALF]
    kr2 = a[:, MLA_Q_RANK + MLA_KV_RANK + MLA_ROPE_HALF:MLA_Q_RANK + MLA_KV_RANK + MLA_ROPE_DIM]
    cq = _norm_fwd(a_q, g_q, jnp.zeros_like(g_q), plus_one=False, out_dtype=BF16, name="mla_norm_q")
    ckv = _norm_fwd(a_kv, g_kv, jnp.zeros_like(g_kv), plus_one=False, out_dtype=BF16, name="mla_norm_kv")
    qf = _matmul(cq, w_uq, name="mla_uq")
    kvf = _matmul(ckv, w_ukv, out_dtype=BF16, name="mla_ukv")
    mq = _rope_heads(qf, tab_a, tab_b, tab_c, out_dtype=BF16, name="rope_q")
    kk1, kk2 = _rope(kr1, kr2, cos16, sin16, negate=False, name="rope_k")
    k_tail = jnp.concatenate([kk1, kk2, z32], axis=1).astype(BF16)
    mk = jnp.concatenate([kvf[:, :nq].reshape(s, N_HEADS, HEAD_DIM),
                          jnp.broadcast_to(k_tail[:, None, :], (s, N_HEADS, HEAD_DIM))], axis=-1).reshape(s, N_HEADS * LANES)
    mla_cfg = dict(qc=0, kc=0, vc=N_HEADS // 2, width=LANES)
    mo, mla_res = _attention(mq, mk, kvf, None, name="mla_attn", **mla_cfg)
    y2 = _matmul(mo, w_mla_o, name="mla_out")
    x3 = _resid_fwd(x2, y2, gate[2], name="resid_fwd2")

    x4, ffn1_res = ffn_fwd(x3, 1, 3)

    loss_vec, dx4, d_final_g = _final_loss(x4, final_g.reshape(1, d), target, name="final_loss")
    loss = lax.psum(loss_vec[0, 0], ("x", "y", "c"))

    grads = {}
    dmod = [None] * 4

    def ffn_bwd(dx_out, xin, li, sub, res):
        hh, u, g, y = res
        dy, dgate = _resid_bwd(dx_out, y, gate[sub], name=f"resid_bwd{sub}")
        gw_out = _matmul(g, dy, ta=True, out_dtype=BF16, name=f"ffn_down_dw{li}")
        dg = _matmul(dy, w_ffn_out[li], tb=True, name=f"ffn_down_dx{li}")
        du, dcw, dcb = _conv_gate_bwd(u, conv_w[li], conv_b[li:li + 1], dg, name=f"conv_bwd{li}")
        gw_in = _matmul(hh, du, ta=True, out_dtype=BF16, name=f"ffn_up_dw{li}")
        dh = _matmul(du, w_ffn_in[li], tb=True, out_dtype=BF16, name=f"ffn_up_dx{li}")
        dx_in, dscale, dshift = _norm_bwd(xin, scale[sub], dh, dx_out, plus_one=True, name=f"ada_bwd{sub}")
        dmod[sub] = jnp.concatenate([dshift, dscale, dgate], axis=1)
        return dx_in, gw_in, dcw, dcb, gw_out

    dx3, gw_in1, dcw1, dcb1, gw_out1 = ffn_bwd(dx4, x3, 1, 3, ffn1_res)

    dy2, dgate2 = _resid_bwd(dx3, y2, gate[2], name="resid_bwd2")
    grads["mla_w_o"] = _matmul(mo, dy2, ta=True, out_dtype=BF16, name="mla_out_dw")[None]
    dmo = _matmul(dy2, w_mla_o, tb=True, out_dtype=BF16, name="mla_out_dx")
    dmq, dmk, dmv, _ = _attention_bwd(mla_res, dmo, dq_mult=1.0 / LOG2E, dk_mult=1.0 / LOG2E, out_dtype=F32,
                                      name="mla_attn", **mla_cfg)
    dqf = _rope_heads(dmq, tab_a, -tab_b, -tab_c, out_dtype=BF16, name="rope_q_bwd")
    g_uq = _matmul(cq, dqf, ta=True, out_dtype=BF16, name="mla_uq_dw")
    dcq = _matmul(dqf, w_uq, tb=True, name="mla_uq_dx")
    dmk3 = dmk.reshape(s, N_HEADS, LANES)
    dkr = _group_sum(dmk, name="mla_krope_sum")
    dkr1, dkr2 = _rope(dkr[:, HEAD_DIM:HEAD_DIM + MLA_ROPE_HALF], dkr[:, HEAD_DIM + MLA_ROPE_HALF:HEAD_DIM + MLA_ROPE_DIM],
                       cos16, sin16, negate=True, name="rope_k_bwd")
    dkvf = jnp.concatenate([dmk3[:, :, :HEAD_DIM].reshape(s, nq).astype(BF16), dmv.astype(BF16)], axis=1)
    g_ukv = _matmul(ckv, dkvf, ta=True, out_dtype=BF16, name="mla_ukv_dw")
    dckv = _matmul(dkvf, w_ukv, tb=True, name="mla_ukv_dx")
    da_q, dg_q, _ = _norm_bwd(a_q, g_q, dcq, None, plus_one=False, name="mla_norm_q_bwd")
    da_kv, dg_kv, _ = _norm_bwd(a_kv, g_kv, dckv, None, plus_one=False, name="mla_norm_kv_bwd")
    da = jnp.concatenate([da_q, da_kv, dkr1, dkr2, jnp.zeros((s, w_a.shape[1] - 672), F32)], axis=1).astype(BF16)
    grads["mla_w_a"] = _matmul(h2, da, ta=True, out_dtype=BF16, name="mla_a_dw")[None, :, :672]
    dh2 = _matmul(da, w_a, tb=True, out_dtype=BF16, name="mla_a_dx")
    dx2, dscale2, dshift2 = _norm_bwd(x2, scale[2], dh2, dx3, plus_one=True, name="ada_bwd2")
    dmod[2] = jnp.concatenate([dshift2, dscale2, dgate2], axis=1)
    grads["mla_w_uq"] = g_uq.reshape(MLA_Q_RANK, N_HEADS, LANES)[:, :, :HEAD_DIM + MLA_ROPE_DIM].reshape(1, MLA_Q_RANK, -1)
    grads["mla_w_ukv"] = _per_head([g_ukv[:, :nq], g_ukv[:, nq:]], MLA_KV_RANK)[None]
    grads["mla_g_q"], grads["mla_g_kv"] = dg_q, dg_kv

    dx1, gw_in0, dcw0, dcb0, gw_out0 = ffn_bwd(dx2, x1, 0, 1, ffn0_res)
    grads["ffn_w_in"] = jnp.stack([gw_in0, gw_in1])
    grads["ffn_w_out"] = jnp.stack([gw_out0, gw_out1])
    grads["ffn_conv_w"] = jnp.stack([dcw0, dcw1])
    g_conv_b = jnp.concatenate([dcb0, dcb1], axis=0)

    dy0, dgate0 = _resid_bwd(dx1, y0, gate[0], name="resid_bwd0")
    grads["fox_w_o"] = _matmul(fo, dy0, ta=True, out_dtype=BF16, name="fox_out_dw")[None]
    dfo = _matmul(dy0, w_fox_o, tb=True, out_dtype=BF16, name="fox_out_dx")
    dfq, dfk, dfv, dcum = _attention_bwd(fox_res, dfo, dq_mult=fox_scale, dk_mult=1.0 / LOG2E, out_dtype=BF16,
                                         name="fox_attn", **fox_cfg)
    dfl, g_b_f = _fox_gate_bwd(fl, fox_b_f, dcum, name="fox_gate_bwd")
    dproj = jnp.concatenate([dfq, dfk, dfv, _pad_axis(dfl, 1, LANES).astype(BF16)], axis=1)
    grads["fox_w_in"] = _matmul(h0, dproj, ta=True, out_dtype=BF16, name="fox_proj_dw")[None, :, :3 * nq + N_HEADS]
    dh0 = _matmul(dproj, w_fox_in, tb=True, out_dtype=BF16, name="fox_proj_dx")
    dx0, dscale0, dshift0 = _norm_bwd(x0, scale[0], dh0, dx1, plus_one=True, name="ada_bwd0")
    dmod[0] = jnp.concatenate([dshift0, dscale0, dgate0], axis=1)

    dmod_send = _pad_axis(jnp.stack(dmod, axis=0).reshape(4, N_DEV, cols).transpose(1, 0, 2), 1, 8)
    dmod_recv, = _exchange([dmod_send], same_src=False, name="scatter_dmod")
    dmod_all = dmod_recv[:, :4]
    dmod_pad = _pad_axis(dmod_all, 0, LANES)
    g_ada_w = jnp.stack([_matmul(silu_c, dmod_pad[:, i], ta=True, name=f"ada_dw{i}") for i in range(4)])
    grads["ada_w"] = g_ada_w.reshape(ada_w.shape)
    grads["ada_b"] = _sum_slabs(dmod_recv, name="ada_db")[:4].reshape(ada_b.shape)

    sharded = list(_BIG) + list(_SMALL)
    axes = {**_BIG, **_SMALL}
    recv = _exchange([_full_to_chunks(grads[n], axes[n]) for n in sharded], same_src=False, name="scatter_grads")
    grads.update(dict(zip(sharded, recv)))
    repl = _exchange([g_b_f, g_conv_b, d_final_g], same_src=True, name="gather_repl_grads")
    grads.update(dict(zip(_REPL, repl)))

    grad_out, deltas, new_m, new_v = {}, {}, {}, {}
    for n in order:
        grad_out[n], deltas[n], new_m[n], new_v[n] = _adamw(
            weights[n], grads[n], mom_m[n], mom_v[n], slabs=n in axes or n in _REPL, name=f"adamw_{n}")

    grad_x = dx0[None]
    return (loss, grad_x, *[grad_out[n] for n in order], *[deltas[n] for n in order],
            *[new_m[n] for n in order], *[new_v[n] for n in order])
```

```python
import jax
import jax.numpy as jnp
from jax import lax
from jax.experimental import pallas as pl
from jax.experimental.pallas import tpu as pltpu

F32 = jnp.float32
BF16 = jnp.bfloat16
HIGHEST = lax.Precision.HIGHEST

N_DEV = 8
D_MODEL = 1024
N_HEADS = 16
HEAD_DIM = 64
MLA_ROPE_HALF = 16
MLA_Q_RANK = 384
MLA_KV_RANK = 256
MLA_ROPE_DIM = 32
NORM_EPS = 1e-6
ROPE_BASE = 10000.0
ADAM_LR = 0.001
ADAM_B1 = 0.9
ADAM_B2 = 0.999
ADAM_EPS = 1e-08
ADAM_WD = 0.01
ADAM_STEP = 10

LANES = 128
VMEM_LIMIT_BYTES = 56 * 1024 * 1024
ROW_BLOCK = 512
ATT_BLOCK = 512
CONV_ROWS = 1024
MM_BM, MM_BN, MM_BK = 512, 1024, 2048
MM_K_WHOLE = 3328
LOG2E = 1.4426950408889634


def _params(*sem):
    return pltpu.CompilerParams(dimension_semantics=sem or None, vmem_limit_bytes=VMEM_LIMIT_BYTES)


def _blk(dim, pref):
    if dim <= pref:
        return dim
    b = pref - pref % LANES
    while b >= LANES:
        if dim % b == 0:
            return b
        b -= LANES
    raise ValueError(f"no block for {dim}")


def _row_blk(rows, pref):
    if rows <= pref:
        return rows
    for b in range(pref - pref % 8, 7, -8):
        if rows % b == 0:
            return b
    return rows


def _pad_axis(a, axis, mult):
    pad = (-a.shape[axis]) % mult
    if pad == 0:
        return a
    widths = [(0, 0)] * a.ndim
    widths[axis] = (0, pad)
    return jnp.pad(a, widths)


def _matmul(a, b, *, ta=False, tb=False, out_dtype=F32, col_scale=None, name):
    m, k = (a.shape[1], a.shape[0]) if ta else a.shape
    n = b.shape[0] if tb else b.shape[1]
    assert (b.shape[1] if tb else b.shape[0]) == k, (a.shape, b.shape, ta, tb)
    bm, bn = _blk(m, MM_BM if ta else 2 * MM_BM), _blk(n, MM_BN)
    bk = k if k <= MM_K_WHOLE else _blk(k, MM_BK)
    nk = k // bk
    dims = (((0 if ta else 1,), (1 if tb else 0,)), ((), ()))
    has_scale = col_scale is not None
    use_acc = nk > 1 and (out_dtype != F32 or has_scale)

    def body(*refs):
        a_ref, b_ref = refs[0], refs[1]
        s_ref = refs[2] if has_scale else None
        o_ref = refs[3] if has_scale else refs[2]
        acc_ref = refs[-1] if use_acc else o_ref
        kk = pl.program_id(2)
        part = lax.dot_general(a_ref[...].astype(BF16), b_ref[...].astype(BF16), dims, preferred_element_type=F32)

        def finish(val):
            if has_scale:
                val = val * s_ref[...]
            o_ref[...] = val.astype(out_dtype)

        if nk == 1:
            finish(part)
            return

        @pl.when(kk == 0)
        def _():
            acc_ref[...] = part

        @pl.when(kk > 0)
        def _():
            acc_ref[...] += part

        if use_acc:
            @pl.when(kk == nk - 1)
            def _():
                finish(acc_ref[...])

    a_spec = pl.BlockSpec((bk, bm), lambda i, j, kk: (kk, i)) if ta else pl.BlockSpec((bm, bk), lambda i, j, kk: (i, kk))
    b_spec = pl.BlockSpec((bn, bk), lambda i, j, kk: (j, kk)) if tb else pl.BlockSpec((bk, bn), lambda i, j, kk: (kk, j))
    return pl.pallas_call(
        body, name=name, grid=(m // bm, n // bn, nk),
        in_specs=[a_spec, b_spec] + ([pl.BlockSpec((1, bn), lambda i, j, kk: (0, j))] if has_scale else []),
        out_specs=pl.BlockSpec((bm, bn), lambda i, j, kk: (i, j)),
        out_shape=jax.ShapeDtypeStruct((m, n), out_dtype),
        scratch_shapes=[pltpu.VMEM((bm, bn), F32)] if use_acc else [],
        compiler_params=_params("parallel", "parallel", "arbitrary"),
    )(*([a, b] + ([col_scale] if has_scale else [])))


def _norm_fwd(x, mul, add, *, plus_one, out_dtype, name):
    s, n = x.shape
    bs = _blk(s, ROW_BLOCK)

    def body(x_ref, m_ref, a_ref, o_ref):
        xv = x_ref[...]
        r = lax.rsqrt(jnp.mean(xv * xv, axis=-1, keepdims=True) + NORM_EPS)
        mv = m_ref[...] + 1.0 if plus_one else m_ref[...]
        o_ref[...] = (xv * r * mv + a_ref[...]).astype(out_dtype)

    row = pl.BlockSpec((bs, n), lambda i: (i, 0))
    vec = pl.BlockSpec((1, n), lambda i: (0, 0))
    return pl.pallas_call(
        body, name=name, grid=(s // bs,), in_specs=[row, vec, vec], out_specs=row,
        out_shape=jax.ShapeDtypeStruct((s, n), out_dtype), compiler_params=_params("parallel"),
    )(x, mul, add)


def _norm_bwd(x, mul, dy, dres, *, plus_one, name):
    s, n = x.shape
    bs = _blk(s, ROW_BLOCK)
    has_res = dres is not None

    def body(*refs):
        if has_res:
            x_ref, m_ref, dy_ref, dres_ref, dx_ref, dm_ref, da_ref = refs
        else:
            x_ref, m_ref, dy_ref, dx_ref, dm_ref, da_ref = refs
        xv = x_ref[...]
        dyv = dy_ref[...].astype(F32)
        r = lax.rsqrt(jnp.mean(xv * xv, axis=-1, keepdims=True) + NORM_EPS)
        xn = xv * r
        mv = m_ref[...] + 1.0 if plus_one else m_ref[...]
        g = dyv * mv
        dx = r * (g - xn * jnp.mean(g * xn, axis=-1, keepdims=True))
        if has_res:
            dx = dx + dres_ref[...]
        dx_ref[...] = dx

        @pl.when(pl.program_id(0) == 0)
        def _():
            dm_ref[...] = jnp.zeros_like(dm_ref)
            da_ref[...] = jnp.zeros_like(da_ref)

        dm_ref[...] += jnp.sum(dyv * xn, axis=0, keepdims=True)
        da_ref[...] += jnp.sum(dyv, axis=0, keepdims=True)

    row = pl.BlockSpec((bs, n), lambda i: (i, 0))
    vec = pl.BlockSpec((1, n), lambda i: (0, 0))
    ins = [x, mul, dy] + ([dres] if has_res else [])
    return pl.pallas_call(
        body, name=name, grid=(s // bs,),
        in_specs=[row, vec, row] + ([row] if has_res else []), out_specs=[row, vec, vec],
        out_shape=[jax.ShapeDtypeStruct((s, n), F32), jax.ShapeDtypeStruct((1, n), F32), jax.ShapeDtypeStruct((1, n), F32)],
        compiler_params=_params("arbitrary"),
    )(*ins)


def _resid_fwd(x, y, gate, *, name):
    s, n = x.shape
    bs = _blk(s, ROW_BLOCK)

    def body(x_ref, y_ref, g_ref, o_ref):
        o_ref[...] = x_ref[...] + g_ref[...] * y_ref[...]

    row = pl.BlockSpec((bs, n), lambda i: (i, 0))
    vec = pl.BlockSpec((1, n), lambda i: (0, 0))
    return pl.pallas_call(
        body, name=name, grid=(s // bs,), in_specs=[row, row, vec], out_specs=row,
        out_shape=jax.ShapeDtypeStruct((s, n), F32), compiler_params=_params("parallel"),
    )(x, y, gate)


def _resid_bwd(dx, y, gate, *, name):
    s, n = dx.shape
    bs = _blk(s, ROW_BLOCK)

    def body(dx_ref, y_ref, g_ref, dy_ref, dg_ref):
        dxv = dx_ref[...]
        dy_ref[...] = (g_ref[...] * dxv).astype(BF16)

        @pl.when(pl.program_id(0) == 0)
        def _():
            dg_ref[...] = jnp.zeros_like(dg_ref)

        dg_ref[...] += jnp.sum(dxv * y_ref[...], axis=0, keepdims=True)

    row = pl.BlockSpec((bs, n), lambda i: (i, 0))
    vec = pl.BlockSpec((1, n), lambda i: (0, 0))
    return pl.pallas_call(
        body, name=name, grid=(s // bs,), in_specs=[row, row, vec], out_specs=[row, vec],
        out_shape=[jax.ShapeDtypeStruct((s, n), BF16), jax.ShapeDtypeStruct((1, n), F32)],
        compiler_params=_params("arbitrary"),
    )(dx, y, gate)


def _final_loss(x, g, target, *, name):
    s, n = x.shape
    bs = _blk(s, ROW_BLOCK)

    def body(x_ref, g_ref, t_ref, loss_ref, dx_ref, dg_ref):
        xv = x_ref[...]
        r = lax.rsqrt(jnp.mean(xv * xv, axis=-1, keepdims=True) + NORM_EPS)
        xn = xv * r
        gv = g_ref[...]
        err = xn * gv - t_ref[...]
        dout = err * (1.0 / n)
        gg = dout * gv
        dx_ref[...] = r * (gg - xn * jnp.mean(gg * xn, axis=-1, keepdims=True))

        @pl.when(pl.program_id(0) == 0)
        def _():
            loss_ref[...] = jnp.zeros_like(loss_ref)
            dg_ref[...] = jnp.zeros_like(dg_ref)

        part = jnp.sum(jnp.sum(err * err, axis=-1, keepdims=True), axis=0, keepdims=True) * (0.5 / n)
        loss_ref[...] += jnp.broadcast_to(part, loss_ref.shape)
        dg_ref[...] += jnp.sum(dout * xn, axis=0, keepdims=True)

    row = pl.BlockSpec((bs, n), lambda i: (i, 0))
    vec = pl.BlockSpec((1, n), lambda i: (0, 0))
    return pl.pallas_call(
        body, name=name, grid=(s // bs,), in_specs=[row, vec, row],
        out_specs=[pl.BlockSpec((1, LANES), lambda i: (0, 0)), row, vec],
        out_shape=[jax.ShapeDtypeStruct((1, LANES), F32), jax.ShapeDtypeStruct((s, n), F32), jax.ShapeDtypeStruct((1, n), F32)],
        compiler_params=_params("arbitrary"),
    )(x, g, target)


def _lane_lt64(shape):
    return lax.broadcasted_iota(jnp.int32, shape, 1) < HEAD_DIM


def _keep_low(x):
    return jnp.where(_lane_lt64(x.shape), x.astype(F32), 0.0).astype(x.dtype)


def _keep_high(x):
    return jnp.where(_lane_lt64(x.shape), 0.0, x.astype(F32)).astype(x.dtype)


def _lane_merge(a, b):
    n = max(a.shape[0], b.shape[0])
    return jnp.where(_lane_lt64((n, LANES)), a, b)


def _pair(x, width, masked):
    if width == HEAD_DIM:
        return (_keep_low(x), _keep_high(x)) if masked else (x, x)
    return x[:, :LANES], x[:, LANES:]


def _qk_t(a, b):
    return lax.dot_general(a, b, (((1,), (1,)), ((), ())), preferred_element_type=F32)


def _attn_specs(s, blk, width, cols, resident):
    w = 2 * width
    if resident:
        return pl.BlockSpec((s, w), lambda p, i: (0, cols + p))
    return pl.BlockSpec((blk, w), lambda p, i: (i, cols + p))


def _attn_fwd(q, k, v, kb_row, *, qc, kc, vc, width, name):
    s = q.shape[0]
    blk = _blk(s, ATT_BLOCK)
    nb = s // blk
    has_bias = kb_row is not None

    def body(*refs):
        if has_bias:
            q_ref, k_ref, v_ref, kb_ref, o_ref, lse_ref = refs
        else:
            q_ref, k_ref, v_ref, o_ref, lse_ref = refs
        i = pl.program_id(1)
        qh = _pair(q_ref[...], width, True)

        def step(j, carry, diag):
            start = pl.multiple_of(j * blk, blk)
            kh = _pair(k_ref[pl.ds(start, blk), :], width, False)
            vj = v_ref[pl.ds(start, blk), :]
            out = []
            for hd in range(2):
                m, l, acc = carry[hd]
                sc = _qk_t(qh[hd], kh[hd])
                if has_bias:
                    sc = sc + kb_ref[hd, j]
                if diag:
                    sc = jnp.where(_causal_keep(blk), sc, -1e30)
                m_new = jnp.maximum(m, jnp.max(sc, axis=-1, keepdims=True))
                alpha = jnp.exp2(m - m_new)
                p = jnp.exp2(sc - m_new)
                l = alpha * l + jnp.sum(p, axis=-1, keepdims=True)
                acc = alpha * acc + jnp.dot(p.astype(BF16), vj, preferred_element_type=F32)
                out.append((m_new, l, acc))
            return tuple(out)

        one = (jnp.full((blk, 1), -1e30, F32), jnp.zeros((blk, 1), F32), jnp.zeros((blk, LANES), F32))
        carry = lax.fori_loop(0, i, lambda j, c: step(j, c, False), (one, one))
        (ma, la, acca), (mb, lb, accb) = step(i, carry, True)
        o_ref[...] = _lane_merge(acca / la, accb / lb)
        lse_ref[...] = _lane_merge(ma + jnp.log(la) * LOG2E, mb + jnp.log(lb) * LOG2E)

    ins = [q, k, v] + ([kb_row] if has_bias else [])
    out_spec = pl.BlockSpec((blk, LANES), lambda p, i: (i, p))
    return pl.pallas_call(
        body, name=name, grid=(N_HEADS // 2, nb),
        in_specs=[_attn_specs(s, blk, width, qc, False), _attn_specs(s, blk, width, kc, True),
                  _attn_specs(s, blk, HEAD_DIM, vc, True)]
                 + ([pl.BlockSpec((2, nb, 1, blk), lambda p, i: (p, 0, 0, 0))] if has_bias else []),
        out_specs=[out_spec, out_spec],
        out_shape=[jax.ShapeDtypeStruct((s, N_HEADS * HEAD_DIM), F32)] * 2,
        compiler_params=_params("parallel", "parallel"),
    )(*ins)


def _causal_keep(n):
    row = lax.broadcasted_iota(jnp.int32, (n, n), 0)
    col = lax.broadcasted_iota(jnp.int32, (n, n), 1)
    return col <= row


def _attn_delta(o, do, *, name):
    s, n = o.shape
    bs = _blk(s, ROW_BLOCK)

    def body(o_ref, do_ref, d_ref):
        for g in range(n // LANES):
            prod = do_ref[:, g * LANES:(g + 1) * LANES].astype(F32) * o_ref[:, g * LANES:(g + 1) * LANES]
            low = _lane_lt64(prod.shape)
            d_ref[:, g * LANES:(g + 1) * LANES] = _lane_merge(
                jnp.sum(jnp.where(low, prod, 0.0), axis=-1, keepdims=True),
                jnp.sum(jnp.where(low, 0.0, prod), axis=-1, keepdims=True))

    row = pl.BlockSpec((bs, n), lambda i: (i, 0))
    return pl.pallas_call(
        body, name=name, grid=(s // bs,), in_specs=[row, row], out_specs=row,
        out_shape=jax.ShapeDtypeStruct((s, n), F32), compiler_params=_params("parallel"),
    )(o, do)


def _attn_bwd(q, k, v, kb_col, do, lse_row, delta_row, *, qc, kc, vc, width, dq_mult, dk_mult, out_dtype, name):
    s = q.shape[0]
    blk = _blk(s, ATT_BLOCK)
    nb = s // blk
    has_bias = kb_col is not None

    def body(*refs):
        if has_bias:
            q_ref, k_ref, v_ref, kb_ref, do_ref, lse_ref, dl_ref, dk_ref, dv_ref, db_ref, dq_ref, dr_ref = refs
        else:
            q_ref, k_ref, v_ref, do_ref, lse_ref, dl_ref, dk_ref, dv_ref, db_ref, dq_ref = refs
        j = pl.program_id(1)

        @pl.when(j == 0)
        def _():
            dq_ref[...] = jnp.zeros_like(dq_ref)
            if has_bias:
                dr_ref[...] = jnp.zeros_like(dr_ref)

        kh = _pair(k_ref[...], width, True)
        v2 = v_ref[...]
        vh = (_keep_low(v2), _keep_high(v2))
        if has_bias:
            kb2 = kb_ref[0]
            kbh = (kb2[:, 0:1], kb2[:, 1:2])

        def step(i, carry, diag):
            start = pl.multiple_of(i * blk, blk)
            qh = _pair(q_ref[pl.ds(start, blk), :], width, False)
            doi = do_ref[pl.ds(start, blk), :]
            out, dq_parts = [], []
            for hd in range(2):
                dk, dvv, db = carry[hd]
                st = _qk_t(kh[hd], qh[hd])
                if has_bias:
                    st = st + kbh[hd]
                if diag:
                    row = lax.broadcasted_iota(jnp.int32, (blk, blk), 0)
                    colq = lax.broadcasted_iota(jnp.int32, (blk, blk), 1)
                    st = jnp.where(row <= colq, st, -1e30)
                pt = jnp.exp2(st - lse_ref[hd, i])
                dvv = dvv + jnp.dot(pt.astype(BF16), doi, preferred_element_type=F32)
                dst = pt * (_qk_t(vh[hd], doi) - dl_ref[hd, i])
                dsb = dst.astype(BF16)
                dk = dk + jnp.dot(dsb, qh[hd], preferred_element_type=F32)
                db = db + jnp.sum(dst, axis=-1, keepdims=True)
                dq_parts.append(lax.dot_general(dsb, kh[hd], (((0,), (0,)), ((), ())), preferred_element_type=F32))
                if has_bias:
                    dr_ref[hd, i] += jnp.sum(dst, axis=0, keepdims=True)
                out.append((dk, dvv, db))
            rows = pl.ds(start, blk)
            if width == HEAD_DIM:
                dq_ref[rows, :] += (dq_parts[0] + dq_parts[1]) * dq_mult
            else:
                dq_ref[rows, 0:LANES] += dq_parts[0] * dq_mult
                dq_ref[rows, LANES:2 * LANES] += dq_parts[1] * dq_mult
            return tuple(out)

        one = (jnp.zeros((blk, LANES), F32), jnp.zeros((blk, LANES), F32), jnp.zeros((blk, 1), F32))
        carry = step(j, (one, one), True)
        (dka, dva, dba), (dkb, dvb, dbb) = lax.fori_loop(j + 1, nb, lambda i, c: step(i, c, False), carry)
        if width == HEAD_DIM:
            dk = _lane_merge(dka, dkb)
        else:
            dk = jnp.concatenate([dka, dkb], axis=1)
        dk_ref[...] = (dk * dk_mult).astype(out_dtype)
        dv_ref[...] = _lane_merge(dva, dvb).astype(out_dtype)
        db_ref[...] = _lane_merge(dba, dbb)

    stat = pl.BlockSpec((blk, LANES), lambda p, jj: (jj, p))
    rows = pl.BlockSpec((2, nb, 1, blk), lambda p, jj: (p, 0, 0, 0))
    ins = [q, k, v] + ([kb_col] if has_bias else []) + [do, lse_row, delta_row]
    return pl.pallas_call(
        body, name=name, grid=(N_HEADS // 2, nb),
        in_specs=[_attn_specs(s, blk, width, qc, True), _attn_specs(s, blk, width, kc, False),
                  _attn_specs(s, blk, HEAD_DIM, vc, False)]
                 + ([pl.BlockSpec((1, blk, 2), lambda p, jj: (p, jj, 0))] if has_bias else [])
                 + [pl.BlockSpec((s, LANES), lambda p, jj: (0, p)), rows, rows],
        out_specs=[pl.BlockSpec((blk, 2 * width), lambda p, jj: (jj, p)), stat, stat,
                   pl.BlockSpec((s, 2 * width), lambda p, jj: (0, p))] + ([rows] if has_bias else []),
        out_shape=[jax.ShapeDtypeStruct((s, N_HEADS * width), out_dtype), jax.ShapeDtypeStruct((s, N_HEADS * HEAD_DIM), out_dtype),
                   jax.ShapeDtypeStruct((s, N_HEADS * HEAD_DIM), F32), jax.ShapeDtypeStruct((s, N_HEADS * width), F32)]
                  + ([jax.ShapeDtypeStruct((N_HEADS, nb, 1, blk), F32)] if has_bias else []),
        compiler_params=_params("parallel", "arbitrary"),
    )(*ins)


def _head_stat(t):
    return t[:, ::HEAD_DIM]


def _stat_rows(t16, blk):
    s = t16.shape[0]
    return t16.T.reshape(N_HEADS, s // blk, 1, blk)


def _attention(q, k, v, bias, *, qc, kc, vc, width, name):
    blk = _blk(q.shape[0], ATT_BLOCK)
    kb_row = None if bias is None else _stat_rows(bias, blk)
    o, lse = _attn_fwd(q, k, v, kb_row, qc=qc, kc=kc, vc=vc, width=width, name=name + "_fwd")
    return o, (q, k, v, bias, o, lse)


def _attention_bwd(res, do, *, qc, kc, vc, width, dq_mult, dk_mult, out_dtype, name):
    q, k, v, bias, o, lse = res
    s = q.shape[0]
    blk = _blk(s, ATT_BLOCK)
    kb_col = None if bias is None else bias.reshape(s, N_HEADS // 2, 2).transpose(1, 0, 2)
    lse_row = _stat_rows(_head_stat(lse), blk)
    delta_row = _stat_rows(_head_stat(_attn_delta(o, do, name=name + "_delta")), blk)
    outs = _attn_bwd(q, k, v, kb_col, do, lse_row, delta_row, qc=qc, kc=kc, vc=vc, width=width, dq_mult=dq_mult,
                     dk_mult=dk_mult, out_dtype=out_dtype, name=name + "_bwd")
    dk, dv, dcol, dq = outs[:4]
    if bias is None:
        return dq, dk, dv, None
    return dq, dk, dv, outs[4].reshape(N_HEADS, s).T - _head_stat(dcol)


def _fox_gate_fwd(fl, bf, *, name):
    s, n = fl.shape
    bs = _blk(s, ROW_BLOCK)

    def body(fl_ref, bf_ref, cum_ref, carry_ref):
        @pl.when(pl.program_id(0) == 0)
        def _():
            carry_ref[...] = jnp.zeros_like(carry_ref)

        z = fl_ref[...] + bf_ref[...]
        lf = jnp.minimum(z, 0.0) - jnp.log1p(jnp.exp(-jnp.abs(z)))
        row = lax.broadcasted_iota(jnp.int32, (bs, bs), 0)
        col = lax.broadcasted_iota(jnp.int32, (bs, bs), 1)
        tri = (col <= row).astype(F32)
        cum_ref[...] = jnp.dot(tri, lf, preferred_element_type=F32, precision=HIGHEST) + carry_ref[...]
        carry_ref[...] += jnp.sum(lf, axis=0, keepdims=True)

    return pl.pallas_call(
        body, name=name, grid=(s // bs,),
        in_specs=[pl.BlockSpec((bs, n), lambda i: (i, 0)), pl.BlockSpec((1, n), lambda i: (0, 0))],
        out_specs=pl.BlockSpec((bs, n), lambda i: (i, 0)),
        out_shape=jax.ShapeDtypeStruct((s, n), F32), scratch_shapes=[pltpu.VMEM((1, n), F32)],
        compiler_params=_params("arbitrary"),
    )(fl, bf)


def _fox_gate_bwd(fl, bf, dcum, *, name):
    s, n = fl.shape
    bs = _blk(s, ROW_BLOCK)
    nb = s // bs

    def body(fl_ref, bf_ref, dc_ref, dz_ref, dbf_ref, carry_ref):
        @pl.when(pl.program_id(0) == 0)
        def _():
            carry_ref[...] = jnp.zeros_like(carry_ref)
            dbf_ref[...] = jnp.zeros_like(dbf_ref)

        dc = dc_ref[...]
        row = lax.broadcasted_iota(jnp.int32, (bs, bs), 0)
        col = lax.broadcasted_iota(jnp.int32, (bs, bs), 1)
        tri = (col >= row).astype(F32)
        dlf = jnp.dot(tri, dc, preferred_element_type=F32, precision=HIGHEST) + carry_ref[...]
        carry_ref[...] += jnp.sum(dc, axis=0, keepdims=True)
        z = fl_ref[...] + bf_ref[...]
        dz = dlf / (1.0 + jnp.exp(z))
        dz_ref[...] = dz
        dbf_ref[...] += jnp.sum(dz, axis=0, keepdims=True)

    rev = pl.BlockSpec((bs, n), lambda i: (nb - 1 - i, 0))
    vec = pl.BlockSpec((1, n), lambda i: (0, 0))
    return pl.pallas_call(
        body, name=name, grid=(nb,), in_specs=[rev, vec, rev], out_specs=[rev, vec],
        out_shape=[jax.ShapeDtypeStruct((s, n), F32), jax.ShapeDtypeStruct((1, n), F32)],
        scratch_shapes=[pltpu.VMEM((1, n), F32)], compiler_params=_params("arbitrary"),
    )(fl, bf, dcum)


def _rope(x1, x2, cos, sin, *, negate, name):
    s, n = x1.shape
    bs = _blk(s, ROW_BLOCK)

    def body(a_ref, b_ref, c_ref, s_ref, o1_ref, o2_ref):
        a, b, cv = a_ref[...], b_ref[...], c_ref[...]
        sv = -s_ref[...] if negate else s_ref[...]
        o1_ref[...] = a * cv - b * sv
        o2_ref[...] = b * cv + a * sv

    row = pl.BlockSpec((bs, n), lambda i: (i, 0))
    return pl.pallas_call(
        body, name=name, grid=(s // bs,), in_specs=[row] * 4, out_specs=[row, row],
        out_shape=[jax.ShapeDtypeStruct((s, n), F32)] * 2, compiler_params=_params("parallel"),
    )(x1, x2, cos, sin)


def _rope_heads(x, ta, tb, tc, *, out_dtype, name):
    s, n = x.shape
    bs = _blk(s, ROW_BLOCK)

    def body(x_ref, a_ref, b_ref, c_ref, o_ref):
        av, bv, cv = a_ref[...], b_ref[...], c_ref[...]
        for g in range(n // LANES):
            xg = x_ref[:, g * LANES:(g + 1) * LANES]
            og = xg * av + pltpu.roll(xg, LANES - MLA_ROPE_HALF, 1) * bv + pltpu.roll(xg, MLA_ROPE_HALF, 1) * cv
            o_ref[:, g * LANES:(g + 1) * LANES] = og.astype(out_dtype)

    row = pl.BlockSpec((bs, n), lambda i: (i, 0))
    tab = pl.BlockSpec((bs, LANES), lambda i: (i, 0))
    return pl.pallas_call(
        body, name=name, grid=(s // bs,), in_specs=[row, tab, tab, tab], out_specs=row,
        out_shape=jax.ShapeDtypeStruct((s, n), out_dtype), compiler_params=_params("parallel"),
    )(x, ta, tb, tc)


def _group_sum(x, *, name):
    s, n = x.shape
    bs = _blk(s, ROW_BLOCK)

    def body(x_ref, o_ref):
        acc = x_ref[:, 0:LANES]
        for g in range(1, n // LANES):
            acc = acc + x_ref[:, g * LANES:(g + 1) * LANES]
        o_ref[...] = acc

    return pl.pallas_call(
        body, name=name, grid=(s // bs,), in_specs=[pl.BlockSpec((bs, n), lambda i: (i, 0))],
        out_specs=pl.BlockSpec((bs, LANES), lambda i: (i, 0)),
        out_shape=jax.ShapeDtypeStruct((s, LANES), F32), compiler_params=_params("parallel"),
    )(x)


def _shift_down(x, k):
    return pltpu.roll(x, k, 0)


def _conv_rows(ext, w_ref, b_ref, rows):
    y = b_ref[...] + w_ref[0:1, :] * _shift_down(ext, 2) + w_ref[1:2, :] * _shift_down(ext, 1) + w_ref[2:3, :] * ext
    return y[8:8 + rows]


def _conv_gate_fwd(u, cw, cb, *, name):
    s, f2 = u.shape
    f = f2 // 2
    nf = f // LANES
    r = _blk(s, CONV_ROWS)
    r8 = r // 8

    def body(ug_ref, ugp_ref, uv_ref, uvp_ref, wg_ref, wv_ref, bg_ref, bv_ref, o_ref):
        first = pl.program_id(1) == 0

        def conv(cur_ref, prev_ref, w_ref, b_ref):
            prev = jnp.where(first, 0.0, prev_ref[...])
            return _conv_rows(jnp.concatenate([prev, cur_ref[...]], axis=0), w_ref, b_ref, r)

        yg = conv(ug_ref, ugp_ref, wg_ref, bg_ref)
        yv = conv(uv_ref, uvp_ref, wv_ref, bv_ref)
        o_ref[...] = (yg * jax.nn.sigmoid(yg) * yv).astype(BF16)

    def cur(off):
        return pl.BlockSpec((r, LANES), lambda c, i: (i, c + off))

    def prev(off):
        return pl.BlockSpec((8, LANES), lambda c, i: (jnp.maximum(i * r8 - 1, 0), c + off))

    def wspec(rows, off):
        return pl.BlockSpec((rows, LANES), lambda c, i: (0, c + off))

    return pl.pallas_call(
        body, name=name, grid=(nf, s // r),
        in_specs=[cur(0), prev(0), cur(nf), prev(nf), wspec(3, 0), wspec(3, nf), wspec(1, 0), wspec(1, nf)],
        out_specs=pl.BlockSpec((r, LANES), lambda c, i: (i, c)),
        out_shape=jax.ShapeDtypeStruct((s, f), BF16), compiler_params=_params("parallel", "parallel"),
    )(u, u, u, u, cw, cw, cb, cb)


def _conv_gate_bwd(u, cw, cb, dg, *, name):
    s, f2 = u.shape
    f = f2 // 2
    nf = f // LANES
    r = _blk(s, CONV_ROWS)
    r8 = r // 8
    nr = s // r

    def body(ug_ref, ugp_ref, ugn_ref, uv_ref, uvp_ref, uvn_ref, wg_ref, wv_ref, bg_ref, bv_ref, dg_ref, dgn_ref,
             dug_ref, duv_ref, dwg_ref, dwv_ref, dbg_ref, dbv_ref):
        i = pl.program_id(1)
        first, last = i == 0, i == nr - 1

        def ext_of(cur_ref, prev_ref, next_ref):
            prev = jnp.where(first, 0.0, prev_ref[...])
            return jnp.concatenate([prev, cur_ref[...], next_ref[...]], axis=0)

        eg, ev = ext_of(ug_ref, ugp_ref, ugn_ref), ext_of(uv_ref, uvp_ref, uvn_ref)
        yg = _conv_rows(eg, wg_ref, bg_ref, r + 8)
        yv = _conv_rows(ev, wv_ref, bv_ref, r + 8)
        dgn = jnp.where(last, 0.0, dgn_ref[...])
        dgx = jnp.concatenate([dg_ref[...], dgn], axis=0)
        sg = jax.nn.sigmoid(yg)
        dyg = dgx * yv * (sg * (1.0 + yg * (1.0 - sg)))
        dyv = dgx * (yg * sg)

        @pl.when(i == 0)
        def _():
            for ref in (dwg_ref, dwv_ref, dbg_ref, dbv_ref):
                ref[...] = jnp.zeros_like(ref)

        def grads(dy, ext, w_ref, du_ref, dw_ref, db_ref):
            n = r + 8
            du = w_ref[2:3, :] * dy + w_ref[1:2, :] * pltpu.roll(dy, n - 1, 0) + w_ref[0:1, :] * pltpu.roll(dy, n - 2, 0)
            du_ref[...] = du[0:r].astype(BF16)
            dyc = dy[0:r]
            db_ref[...] += jnp.sum(dyc, axis=0, keepdims=True)
            ext_c = ext[0:r + 8]
            dw_ref[0:1, :] += jnp.sum(dyc * _shift_down(ext_c, 2)[8:], axis=0, keepdims=True)
            dw_ref[1:2, :] += jnp.sum(dyc * _shift_down(ext_c, 1)[8:], axis=0, keepdims=True)
            dw_ref[2:3, :] += jnp.sum(dyc * ext_c[8:], axis=0, keepdims=True)

        grads(dyg, eg, wg_ref, dug_ref, dwg_ref, dbg_ref)
        grads(dyv, ev, wv_ref, duv_ref, dwv_ref, dbv_ref)

    def cur(off):
        return pl.BlockSpec((r, LANES), lambda c, i: (i, c + off))

    def prev(off):
        return pl.BlockSpec((8, LANES), lambda c, i: (jnp.maximum(i * r8 - 1, 0), c + off))

    def nxt(off):
        return pl.BlockSpec((8, LANES), lambda c, i: (jnp.minimum((i + 1) * r8, s // 8 - 1), c + off))

    def wspec(rows, off):
        return pl.BlockSpec((rows, LANES), lambda c, i: (0, c + off))

    outs = pl.pallas_call(
        body, name=name, grid=(nf, nr),
        in_specs=[cur(0), prev(0), nxt(0), cur(nf), prev(nf), nxt(nf), wspec(3, 0), wspec(3, nf), wspec(1, 0), wspec(1, nf),
                  cur(0), nxt(0)],
        out_specs=[cur(0), cur(0), wspec(3, 0), wspec(3, 0), wspec(1, 0), wspec(1, 0)],
        out_shape=[jax.ShapeDtypeStruct((s, f), BF16), jax.ShapeDtypeStruct((s, f), BF16),
                   jax.ShapeDtypeStruct((3, f), F32), jax.ShapeDtypeStruct((3, f), F32),
                   jax.ShapeDtypeStruct((1, f), F32), jax.ShapeDtypeStruct((1, f), F32)],
        compiler_params=_params("parallel", "arbitrary"),
    )(u, u, u, u, u, u, cw, cw, cb, cb, dg, dg)
    dug, duv, dwg, dwv, dbg, dbv = outs
    return jnp.concatenate([dug, duv], axis=1), jnp.concatenate([dwg, dwv], axis=1), jnp.concatenate([dbg, dbv], axis=1)


def _adamw(w, g, m, v, *, slabs, name):
    shape = w.shape
    cols = shape[-1]
    rows = w.size // cols
    w2, m2, v2 = (t.reshape(rows, cols) for t in (w, m, v))
    g2 = g.reshape((N_DEV, rows, cols) if slabs else (rows, cols))
    br = _row_blk(rows, ROW_BLOCK // 2 if slabs else ROW_BLOCK)

    def body(w_ref, g_ref, m_ref, v_ref, go_ref, d_ref, nm_ref, nv_ref):
        if slabs:
            gv = g_ref[0].astype(F32)
            for p in range(1, N_DEV):
                gv = gv + g_ref[p].astype(F32)
        else:
            gv = g_ref[...]
        nm = ADAM_B1 * m_ref[...] + (1.0 - ADAM_B1) * gv
        nv = ADAM_B2 * v_ref[...] + (1.0 - ADAM_B2) * (gv * gv)
        m_hat = nm / (1.0 - ADAM_B1 ** ADAM_STEP)
        v_hat = nv / (1.0 - ADAM_B2 ** ADAM_STEP)
        go_ref[...] = gv
        d_ref[...] = -ADAM_LR * (m_hat / (jnp.sqrt(v_hat) + ADAM_EPS) + ADAM_WD * w_ref[...])
        nm_ref[...] = nm
        nv_ref[...] = nv

    spec = pl.BlockSpec((br, cols), lambda i: (i, 0))
    gspec = pl.BlockSpec((N_DEV, br, cols), lambda i: (0, i, 0)) if slabs else spec
    outs = pl.pallas_call(
        body, name=name, grid=(rows // br,), in_specs=[spec, gspec, spec, spec], out_specs=[spec] * 4,
        out_shape=[jax.ShapeDtypeStruct((rows, cols), F32)] * 4, compiler_params=_params("parallel"),
    )(w2, g2, m2, v2)
    return tuple(t.reshape(shape) for t in outs)


def _exchange(xs, *, same_src, name):
    n = len(xs)
    slabs = [x.shape if same_src else x.shape[1:] for x in xs]

    def body(*refs):
        x_refs, o_refs = refs[:n], refs[n:2 * n]
        send_sems, recv_sems, loc_sems = refs[2 * n:]
        ix, iy, ic = lax.axis_index("x"), lax.axis_index("y"), lax.axis_index("c")
        me = 4 * ix + 2 * iy + ic
        local, sends, recvs = [], [], []
        for a in range(n):
            def src(p, a=a):
                return x_refs[a] if same_src else x_refs[a].at[p]

            local.append(pltpu.make_async_copy(src(me), o_refs[a].at[me], loc_sems.at[a]))
            for k in (1, 2, 4, 3, 5, 6, 7):
                px = 1 - ix if k & 4 else ix
                py = 1 - iy if k & 2 else iy
                pc = 1 - ic if k & 1 else ic
                p = 4 * px + 2 * py + pc
                for dst, out in ((me, sends), (p, recvs)):
                    out.append(pltpu.make_async_remote_copy(
                        src_ref=src(p), dst_ref=o_refs[a].at[dst], send_sem=send_sems.at[a, k - 1],
                        recv_sem=recv_sems.at[a, k - 1], device_id=(px, py, pc), device_id_type=pl.DeviceIdType.MESH))
        for cp in local + sends:
            cp.start()
        for cp in recvs:
            cp.wait_recv()
        for cp in sends:
            cp.wait_send()
        for cp in local:
            cp.wait()

    return pl.pallas_call(
        body, name=name,
        in_specs=[pl.BlockSpec(memory_space=pl.ANY)] * n, out_specs=[pl.BlockSpec(memory_space=pl.ANY)] * n,
        out_shape=[jax.ShapeDtypeStruct((N_DEV,) + tuple(sl), x.dtype) for sl, x in zip(slabs, xs)],
        scratch_shapes=[pltpu.SemaphoreType.DMA((n, N_DEV - 1)), pltpu.SemaphoreType.DMA((n, N_DEV - 1)),
                        pltpu.SemaphoreType.DMA((n,))],
        compiler_params=pltpu.CompilerParams(has_side_effects=True, vmem_limit_bytes=VMEM_LIMIT_BYTES),
    )(*xs)


def _sum_slabs(x, *, name):
    n, r, c = x.shape
    br = _row_blk(r, ROW_BLOCK)

    def body(x_ref, o_ref):
        acc = x_ref[0]
        for p in range(1, n):
            acc = acc + x_ref[p]
        o_ref[...] = acc

    return pl.pallas_call(
        body, name=name, grid=(r // br,), in_specs=[pl.BlockSpec((n, br, c), lambda i: (0, i, 0))],
        out_specs=pl.BlockSpec((br, c), lambda i: (i, 0)),
        out_shape=jax.ShapeDtypeStruct((r, c), F32), compiler_params=_params("parallel"),
    )(x)


def _silu(x, *, name):
    def body(x_ref, o_ref):
        xv = x_ref[...]
        o_ref[...] = (xv * jax.nn.sigmoid(xv)).astype(BF16)

    return pl.pallas_call(body, name=name, out_shape=jax.ShapeDtypeStruct(x.shape, BF16),
                          compiler_params=_params())(x)


_BIG = {"fox_w_in": 2, "fox_w_o": 1, "mla_w_a": 1, "mla_w_uq": 2, "mla_w_ukv": 2, "mla_w_o": 1, "ffn_w_in": 2, "ffn_w_out": 1}
_SMALL = {"mla_g_q": 1, "mla_g_kv": 1, "ffn_conv_w": 2}
_REPL = ("fox_b_f", "ffn_conv_b", "final_g")


def _gathered_to_full(g, axis):
    full = jnp.moveaxis(g, 0, axis)
    shape = list(full.shape)
    shape[axis:axis + 2] = [shape[axis] * shape[axis + 1]]
    return full.reshape(shape)


def _full_to_chunks(full, axis):
    shape = list(full.shape)
    shape[axis:axis + 1] = [N_DEV, shape[axis] // N_DEV]
    return jnp.moveaxis(full.reshape(shape), axis, 0)


def _per_head(parts, s_or_rows):
    return jnp.concatenate([p.reshape(s_or_rows, N_HEADS, -1) for p in parts], axis=-1).reshape(s_or_rows, -1)


def kernel(x, c, ada_w, ada_b, fox_w_in, fox_b_f, fox_w_o, mla_w_a, mla_g_q, mla_g_kv, mla_w_uq, mla_w_ukv, mla_w_o, ffn_w_in, ffn_conv_w, ffn_conv_b, ffn_w_out, final_g, loss_target, m_ada_w, m_ada_b, m_fox_w_in, m_fox_b_f, m_fox_w_o, m_mla_w_a, m_mla_g_q, m_mla_g_kv, m_mla_w_uq, m_mla_w_ukv, m_mla_w_o, m_ffn_w_in, m_ffn_conv_w, m_ffn_conv_b, m_ffn_w_out, m_final_g, v_ada_w, v_ada_b, v_fox_w_in, v_fox_b_f, v_fox_w_o, v_mla_w_a, v_mla_g_q, v_mla_g_kv, v_mla_w_uq, v_mla_w_ukv, v_mla_w_o, v_ffn_w_in, v_ffn_conv_w, v_ffn_conv_b, v_ffn_w_out, v_final_g):
    weights = dict(ada_w=ada_w, ada_b=ada_b, fox_w_in=fox_w_in, fox_b_f=fox_b_f, fox_w_o=fox_w_o, mla_w_a=mla_w_a,
                   mla_g_q=mla_g_q, mla_g_kv=mla_g_kv, mla_w_uq=mla_w_uq, mla_w_ukv=mla_w_ukv, mla_w_o=mla_w_o,
                   ffn_w_in=ffn_w_in, ffn_conv_w=ffn_conv_w, ffn_conv_b=ffn_conv_b, ffn_w_out=ffn_w_out, final_g=final_g)
    mom_m = dict(ada_w=m_ada_w, ada_b=m_ada_b, fox_w_in=m_fox_w_in, fox_b_f=m_fox_b_f, fox_w_o=m_fox_w_o, mla_w_a=m_mla_w_a,
                 mla_g_q=m_mla_g_q, mla_g_kv=m_mla_g_kv, mla_w_uq=m_mla_w_uq, mla_w_ukv=m_mla_w_ukv, mla_w_o=m_mla_w_o,
                 ffn_w_in=m_ffn_w_in, ffn_conv_w=m_ffn_conv_w, ffn_conv_b=m_ffn_conv_b, ffn_w_out=m_ffn_w_out, final_g=m_final_g)
    mom_v = dict(ada_w=v_ada_w, ada_b=v_ada_b, fox_w_in=v_fox_w_in, fox_b_f=v_fox_b_f, fox_w_o=v_fox_w_o, mla_w_a=v_mla_w_a,
                 mla_g_q=v_mla_g_q, mla_g_kv=v_mla_g_kv, mla_w_uq=v_mla_w_uq, mla_w_ukv=v_mla_w_ukv, mla_w_o=v_mla_w_o,
                 ffn_w_in=v_ffn_w_in, ffn_conv_w=v_ffn_conv_w, ffn_conv_b=v_ffn_conv_b, ffn_w_out=v_ffn_w_out, final_g=v_final_g)
    order = list(weights)
    x0 = x[0]
    target = loss_target[0]
    s = x0.shape[0]
    d = D_MODEL
    cols = ada_w.shape[-1]
    nq = N_HEADS * HEAD_DIM

    small_names = ["c"] + list(_SMALL)
    small_all = dict(zip(small_names, _exchange([c] + [weights[n] for n in _SMALL], same_src=True, name="gather_small")))
    c_all = small_all["c"].reshape(N_DEV, d)
    g_q = _gathered_to_full(small_all["mla_g_q"], 1)
    g_kv = _gathered_to_full(small_all["mla_g_kv"], 1)
    conv_w = _gathered_to_full(small_all["ffn_conv_w"], 2)

    c_pad = _pad_axis(c_all, 0, LANES)
    silu_c = _silu(c_pad, name="silu_c")
    w_ada = ada_w.reshape(4, d, cols)
    b_ada = ada_b.reshape(4, 1, cols)
    mods = [_matmul(silu_c, w_ada[i], name=f"ada_mod{i}")[:N_DEV] + b_ada[i] for i in range(4)]
    mod_send = _pad_axis(jnp.stack(mods, axis=1), 1, 8)
    mod_recv, = _exchange([mod_send], same_src=False, name="scatter_mod")
    mod = mod_recv[:, :4].transpose(1, 0, 2).reshape(4, 3 * d)
    shift = [mod[i:i + 1, 0:d] for i in range(4)]
    scale = [mod[i:i + 1, d:2 * d] for i in range(4)]
    gate = [mod[i:i + 1, 2 * d:3 * d] for i in range(4)]

    big_all = _exchange([weights[n].astype(BF16) for n in _BIG], same_src=True, name="gather_weights")
    wfull = {n: _gathered_to_full(g, _BIG[n]) for n, g in zip(_BIG, big_all)}

    w_fox_in = _pad_axis(wfull["fox_w_in"][0], 1, LANES)
    w_fox_qkv, w_fox_f = w_fox_in[:, :3 * nq], w_fox_in[:, 3 * nq:]
    w_fox_o = wfull["fox_w_o"][0]
    w_a = _pad_axis(wfull["mla_w_a"][0], 1, LANES)
    wq = wfull["mla_w_uq"][0].reshape(MLA_Q_RANK, N_HEADS, HEAD_DIM + MLA_ROPE_DIM)
    w_uq = _pad_axis(wq, 2, LANES).reshape(MLA_Q_RANK, N_HEADS * LANES)
    wkv = wfull["mla_w_ukv"][0].reshape(MLA_KV_RANK, N_HEADS, 2 * HEAD_DIM)
    w_ukv = jnp.concatenate([wkv[:, :, :HEAD_DIM].reshape(MLA_KV_RANK, -1), wkv[:, :, HEAD_DIM:].reshape(MLA_KV_RANK, -1)], axis=1)
    w_mla_o = wfull["mla_w_o"][0]
    w_ffn_in = wfull["ffn_w_in"]
    w_ffn_out = wfull["ffn_w_out"]
    conv_b = ffn_conv_b

    fox_scale = HEAD_DIM ** -0.5
    mla_scale = (HEAD_DIM + MLA_ROPE_DIM) ** -0.5
    pos = jnp.arange(s, dtype=F32)
    inv_freq = ROPE_BASE ** (-jnp.arange(0, MLA_ROPE_DIM, 2, dtype=F32) / MLA_ROPE_DIM)
    ang = pos[:, None] * inv_freq[None, :]
    cos16, sin16 = jnp.cos(ang), jnp.sin(ang)
    z16, z32, z64 = jnp.zeros((s, 16), F32), jnp.zeros((s, 32), F32), jnp.zeros((s, 64), F32)
    tab_a = jnp.concatenate([jnp.ones((s, 64), F32), cos16, cos16, z32], axis=1) * (mla_scale * LOG2E)
    tab_b = jnp.concatenate([z64, -sin16, z16, z32], axis=1) * (mla_scale * LOG2E)
    tab_c = jnp.concatenate([z64, z16, sin16, z32], axis=1) * (mla_scale * LOG2E)

    h0 = _norm_fwd(x0, scale[0], shift[0], plus_one=True, out_dtype=BF16, name="ada_fwd0")
    q_mult = jnp.concatenate([jnp.full((1, nq), fox_scale * LOG2E, F32), jnp.ones((1, 2 * nq), F32)], axis=1)
    qkv = _matmul(h0, w_fox_qkv, out_dtype=BF16, col_scale=q_mult, name="fox_proj")
    fl = _matmul(h0, w_fox_f, name="fox_proj_f")[:, :N_HEADS]
    cum = _fox_gate_fwd(fl, fox_b_f, name="fox_gate_fwd")
    fox_cfg = dict(qc=0, kc=N_HEADS // 2, vc=N_HEADS, width=HEAD_DIM)
    fo, fox_res = _attention(qkv, qkv, qkv, cum * -LOG2E, name="fox_attn", **fox_cfg)
    y0 = _matmul(fo, w_fox_o, name="fox_out")
    x1 = _resid_fwd(x0, y0, gate[0], name="resid_fwd0")

    def ffn_fwd(xin, li, sub):
        hh = _norm_fwd(xin, scale[sub], shift[sub], plus_one=True, out_dtype=BF16, name=f"ada_fwd{sub}")
        u = _matmul(hh, w_ffn_in[li], name=f"ffn_up{li}")
        g = _conv_gate_fwd(u, conv_w[li], conv_b[li:li + 1], name=f"conv_fwd{li}")
        y = _matmul(g, w_ffn_out[li], name=f"ffn_down{li}")
        return _resid_fwd(xin, y, gate[sub], name=f"resid_fwd{sub}"), (hh, u, g, y)

    x2, ffn0_res = ffn_fwd(x1, 0, 1)

    h2 = _norm_fwd(x2, scale[2], shift[2], plus_one=True, out_dtype=BF16, name="ada_fwd2")
    a = _matmul(h2, w_a, name="mla_a")
    a_q, a_kv = a[:, :MLA_Q_RANK], a[:, MLA_Q_RANK:MLA_Q_RANK + MLA_KV_RANK]
    kr1 = a[:, MLA_Q_RANK + MLA_KV_RANK:MLA_Q_RANK + MLA_KV_RANK + MLA_ROPE_HALF]
    kr2 = a[:, MLA_Q_RANK + MLA_KV_RANK + MLA_ROPE_HALF:MLA_Q_RANK + MLA_KV_RANK + MLA_ROPE_DIM]
    cq = _norm_fwd(a_q, g_q, jnp.zeros_like(g_q), plus_one=False, out_dtype=BF16, name="mla_norm_q")
    ckv = _norm_fwd(a_kv, g_kv, jnp.zeros_like(g_kv), plus_one=False, out_dtype=BF16, name="mla_norm_kv")
    qf = _matmul(cq, w_uq, name="mla_uq")
    kvf = _matmul(ckv, w_ukv, out_dtype=BF16, name="mla_ukv")
    mq = _rope_heads(qf, tab_a, tab_b, tab_c, out_dtype=BF16, name="rope_q")
    kk1, kk2 = _rope(kr1, kr2, cos16, sin16, negate=False, name="rope_k")
    k_tail = jnp.concatenate([kk1, kk2, z32], axis=1).astype(BF16)
    mk = jnp.concatenate([kvf[:, :nq].reshape(s, N_HEADS, HEAD_DIM),
                          jnp.broadcast_to(k_tail[:, None, :], (s, N_HEADS, HEAD_DIM))], axis=-1).reshape(s, N_HEADS * LANES)
    mla_cfg = dict(qc=0, kc=0, vc=N_HEADS // 2, width=LANES)
    mo, mla_res = _attention(mq, mk, kvf, None, name="mla_attn", **mla_cfg)
    y2 = _matmul(mo, w_mla_o, name="mla_out")
    x3 = _resid_fwd(x2, y2, gate[2], name="resid_fwd2")

    x4, ffn1_res = ffn_fwd(x3, 1, 3)

    loss_vec, dx4, d_final_g = _final_loss(x4, final_g.reshape(1, d), target, name="final_loss")
    loss = lax.psum(loss_vec[0, 0], ("x", "y", "c"))

    grads = {}
    dmod = [None] * 4

    def ffn_bwd(dx_out, xin, li, sub, res):
        hh, u, g, y = res
        dy, dgate = _resid_bwd(dx_out, y, gate[sub], name=f"resid_bwd{sub}")
        gw_out = _matmul(g, dy, ta=True, out_dtype=BF16, name=f"ffn_down_dw{li}")
        dg = _matmul(dy, w_ffn_out[li], tb=True, name=f"ffn_down_dx{li}")
        du, dcw, dcb = _conv_gate_bwd(u, conv_w[li], conv_b[li:li + 1], dg, name=f"conv_bwd{li}")
        gw_in = _matmul(hh, du, ta=True, out_dtype=BF16, name=f"ffn_up_dw{li}")
        dh = _matmul(du, w_ffn_in[li], tb=True, out_dtype=BF16, name=f"ffn_up_dx{li}")
        dx_in, dscale, dshift = _norm_bwd(xin, scale[sub], dh, dx_out, plus_one=True, name=f"ada_bwd{sub}")
        dmod[sub] = jnp.concatenate([dshift, dscale, dgate], axis=1)
        return dx_in, gw_in, dcw, dcb, gw_out

    dx3, gw_in1, dcw1, dcb1, gw_out1 = ffn_bwd(dx4, x3, 1, 3, ffn1_res)

    dy2, dgate2 = _resid_bwd(dx3, y2, gate[2], name="resid_bwd2")
    grads["mla_w_o"] = _matmul(mo, dy2, ta=True, out_dtype=BF16, name="mla_out_dw")[None]
    dmo = _matmul(dy2, w_mla_o, tb=True, out_dtype=BF16, name="mla_out_dx")
    dmq, dmk, dmv, _ = _attention_bwd(mla_res, dmo, dq_mult=1.0 / LOG2E, dk_mult=1.0 / LOG2E, out_dtype=F32,
                                      name="mla_attn", **mla_cfg)
    dqf = _rope_heads(dmq, tab_a, -tab_b, -tab_c, out_dtype=BF16, name="rope_q_bwd")
    g_uq = _matmul(cq, dqf, ta=True, out_dtype=BF16, name="mla_uq_dw")
    dcq = _matmul(dqf, w_uq, tb=True, name="mla_uq_dx")
    dmk3 = dmk.reshape(s, N_HEADS, LANES)
    dkr = _group_sum(dmk, name="mla_krope_sum")
    dkr1, dkr2 = _rope(dkr[:, HEAD_DIM:HEAD_DIM + MLA_ROPE_HALF], dkr[:, HEAD_DIM + MLA_ROPE_HALF:HEAD_DIM + MLA_ROPE_DIM],
                       cos16, sin16, negate=True, name="rope_k_bwd")
    dkvf = jnp.concatenate([dmk3[:, :, :HEAD_DIM].reshape(s, nq).astype(BF16), dmv.astype(BF16)], axis=1)
    g_ukv = _matmul(ckv, dkvf, ta=True, out_dtype=BF16, name="mla_ukv_dw")
    dckv = _matmul(dkvf, w_ukv, tb=True, name="mla_ukv_dx")
    da_q, dg_q, _ = _norm_bwd(a_q, g_q, dcq, None, plus_one=False, name="mla_norm_q_bwd")
    da_kv, dg_kv, _ = _norm_bwd(a_kv, g_kv, dckv, None, plus_one=False, name="mla_norm_kv_bwd")
    da = jnp.concatenate([da_q, da_kv, dkr1, dkr2, jnp.zeros((s, w_a.shape[1] - 672), F32)], axis=1).astype(BF16)
    grads["mla_w_a"] = _matmul(h2, da, ta=True, out_dtype=BF16, name="mla_a_dw")[None, :, :672]
    dh2 = _matmul(da, w_a, tb=True, out_dtype=BF16, name="mla_a_dx")
    dx2, dscale2, dshift2 = _norm_bwd(x2, scale[2], dh2, dx3, plus_one=True, name="ada_bwd2")
    dmod[2] = jnp.concatenate([dshift2, dscale2, dgate2], axis=1)
    grads["mla_w_uq"] = g_uq.reshape(MLA_Q_RANK, N_HEADS, LANES)[:, :, :HEAD_DIM + MLA_ROPE_DIM].reshape(1, MLA_Q_RANK, -1)
    grads["mla_w_ukv"] = _per_head([g_ukv[:, :nq], g_ukv[:, nq:]], MLA_KV_RANK)[None]
    grads["mla_g_q"], grads["mla_g_kv"] = dg_q, dg_kv

    dx1, gw_in0, dcw0, dcb0, gw_out0 = ffn_bwd(dx2, x1, 0, 1, ffn0_res)
    grads["ffn_w_in"] = jnp.stack([gw_in0, gw_in1])
    grads["ffn_w_out"] = jnp.stack([gw_out0, gw_out1])
    grads["ffn_conv_w"] = jnp.stack([dcw0, dcw1])
    g_conv_b = jnp.concatenate([dcb0, dcb1], axis=0)

    dy0, dgate0 = _resid_bwd(dx1, y0, gate[0], name="resid_bwd0")
    grads["fox_w_o"] = _matmul(fo, dy0, ta=True, out_dtype=BF16, name="fox_out_dw")[None]
    dfo = _matmul(dy0, w_fox_o, tb=True, out_dtype=BF16, name="fox_out_dx")
    dfq, dfk, dfv, dcum = _attention_bwd(fox_res, dfo, dq_mult=fox_scale, dk_mult=1.0 / LOG2E, out_dtype=BF16,
                                         name="fox_attn", **fox_cfg)
    dfl, g_b_f = _fox_gate_bwd(fl, fox_b_f, dcum, name="fox_gate_bwd")
    dproj = jnp.concatenate([dfq.astype(BF16), dfk, dfv, _pad_axis(dfl, 1, LANES).astype(BF16)], axis=1)
    grads["fox_w_in"] = _matmul(h0, dproj, ta=True, out_dtype=BF16, name="fox_proj_dw")[None, :, :3 * nq + N_HEADS]
    dh0 = _matmul(dproj, w_fox_in, tb=True, out_dtype=BF16, name="fox_proj_dx")
    dx0, dscale0, dshift0 = _norm_bwd(x0, scale[0], dh0, dx1, plus_one=True, name="ada_bwd0")
    dmod[0] = jnp.concatenate([dshift0, dscale0, dgate0], axis=1)

    dmod_send = _pad_axis(jnp.stack(dmod, axis=0).reshape(4, N_DEV, cols).transpose(1, 0, 2), 1, 8)
    dmod_recv, = _exchange([dmod_send], same_src=False, name="scatter_dmod")
    dmod_all = dmod_recv[:, :4]
    dmod_pad = _pad_axis(dmod_all, 0, LANES)
    g_ada_w = jnp.stack([_matmul(silu_c, dmod_pad[:, i], ta=True, name=f"ada_dw{i}") for i in range(4)])
    grads["ada_w"] = g_ada_w.reshape(ada_w.shape)
    grads["ada_b"] = _sum_slabs(dmod_recv, name="ada_db")[:4].reshape(ada_b.shape)

    sharded = list(_BIG) + list(_SMALL)
    axes = {**_BIG, **_SMALL}
    recv = _exchange([_full_to_chunks(grads[n], axes[n]) for n in sharded], same_src=False, name="scatter_grads")
    grads.update(dict(zip(sharded, recv)))
    repl = _exchange([g_b_f, g_conv_b, d_final_g], same_src=True, name="gather_repl_grads")
    grads.update(dict(zip(_REPL, repl)))

    grad_out, deltas, new_m, new_v = {}, {}, {}, {}
    for n in order:
        grad_out[n], deltas[n], new_m[n], new_v[n] = _adamw(
            weights[n], grads[n], mom_m[n], mom_v[n], slabs=n in axes or n in _REPL, name=f"adamw_{n}")

    grad_x = dx0[None]
    return (loss, grad_x, *[grad_out[n] for n in order], *[deltas[n] for n in order],
            *[new_m[n] for n in order], *[new_v[n] for n in order])
```

```python
import jax
import jax.numpy as jnp
from jax import lax
from jax.experimental import pallas as pl
from jax.experimental.pallas import tpu as pltpu

F32 = jnp.float32
BF16 = jnp.bfloat16
HIGHEST = lax.Precision.HIGHEST

N_DEV = 8
D_MODEL = 1024
N_HEADS = 16
HEAD_DIM = 64
MLA_ROPE_HALF = 16
MLA_Q_RANK = 384
MLA_KV_RANK = 256
MLA_ROPE_DIM = 32
NORM_EPS = 1e-6
ROPE_BASE = 10000.0
ADAM_LR = 0.001
ADAM_B1 = 0.9
ADAM_B2 = 0.999
ADAM_EPS = 1e-08
ADAM_WD = 0.01
ADAM_STEP = 10

LANES = 128
VMEM_LIMIT_BYTES = 56 * 1024 * 1024
ROW_BLOCK = 512
ATT_BLOCK = 512
CONV_ROWS = 1024
MM_BM, MM_BN, MM_BK = 512, 1024, 2048
MM_K_WHOLE = 3328
LOG2E = 1.4426950408889634


def _params(*sem):
    return pltpu.CompilerParams(dimension_semantics=sem or None, vmem_limit_bytes=VMEM_LIMIT_BYTES)


def _blk(dim, pref):
    if dim <= pref:
        return dim
    b = pref - pref % LANES
    while b >= LANES:
        if dim % b == 0:
            return b
        b -= LANES
    raise ValueError(f"no block for {dim}")


def _row_blk(rows, pref):
    if rows <= pref:
        return rows
    for b in range(pref - pref % 8, 7, -8):
        if rows % b == 0:
            return b
    return rows


def _pad_axis(a, axis, mult):
    pad = (-a.shape[axis]) % mult
    if pad == 0:
        return a
    widths = [(0, 0)] * a.ndim
    widths[axis] = (0, pad)
    return jnp.pad(a, widths)


def _matmul(a, b, *, ta=False, tb=False, out_dtype=F32, col_scale=None, name):
    m, k = (a.shape[1], a.shape[0]) if ta else a.shape
    n = b.shape[0] if tb else b.shape[1]
    assert (b.shape[1] if tb else b.shape[0]) == k, (a.shape, b.shape, ta, tb)
    bm, bn = _blk(m, MM_BM if ta else 2 * MM_BM), _blk(n, MM_BN)
    bk = k if k <= MM_K_WHOLE else _blk(k, MM_BK)
    nk = k // bk
    dims = (((0 if ta else 1,), (1 if tb else 0,)), ((), ()))
    has_scale = col_scale is not None
    use_acc = nk > 1 and (out_dtype != F32 or has_scale)

    def body(*refs):
        a_ref, b_ref = refs[0], refs[1]
        s_ref = refs[2] if has_scale else None
        o_ref = refs[3] if has_scale else refs[2]
        acc_ref = refs[-1] if use_acc else o_ref
        kk = pl.program_id(2)
        part = lax.dot_general(a_ref[...].astype(BF16), b_ref[...].astype(BF16), dims, preferred_element_type=F32)

        def finish(val):
            if has_scale:
                val = val * s_ref[...]
            o_ref[...] = val.astype(out_dtype)

        if nk == 1:
            finish(part)
            return

        @pl.when(kk == 0)
        def _():
            acc_ref[...] = part

        @pl.when(kk > 0)
        def _():
            acc_ref[...] += part

        if use_acc:
            @pl.when(kk == nk - 1)
            def _():
                finish(acc_ref[...])

    a_spec = pl.BlockSpec((bk, bm), lambda i, j, kk: (kk, i)) if ta else pl.BlockSpec((bm, bk), lambda i, j, kk: (i, kk))
    b_spec = pl.BlockSpec((bn, bk), lambda i, j, kk: (j, kk)) if tb else pl.BlockSpec((bk, bn), lambda i, j, kk: (kk, j))
    return pl.pallas_call(
        body, name=name, grid=(m // bm, n // bn, nk),
        in_specs=[a_spec, b_spec] + ([pl.BlockSpec((1, bn), lambda i, j, kk: (0, j))] if has_scale else []),
        out_specs=pl.BlockSpec((bm, bn), lambda i, j, kk: (i, j)),
        out_shape=jax.ShapeDtypeStruct((m, n), out_dtype),
        scratch_shapes=[pltpu.VMEM((bm, bn), F32)] if use_acc else [],
        compiler_params=_params("parallel", "parallel", "arbitrary"),
    )(*([a, b] + ([col_scale] if has_scale else [])))


def _norm_fwd(x, mul, add, *, plus_one, out_dtype, name):
    s, n = x.shape
    bs = _blk(s, ROW_BLOCK)

    def body(x_ref, m_ref, a_ref, o_ref):
        xv = x_ref[...]
        r = lax.rsqrt(jnp.mean(xv * xv, axis=-1, keepdims=True) + NORM_EPS)
        mv = m_ref[...] + 1.0 if plus_one else m_ref[...]
        o_ref[...] = (xv * r * mv + a_ref[...]).astype(out_dtype)

    row = pl.BlockSpec((bs, n), lambda i: (i, 0))
    vec = pl.BlockSpec((1, n), lambda i: (0, 0))
    return pl.pallas_call(
        body, name=name, grid=(s // bs,), in_specs=[row, vec, vec], out_specs=row,
        out_shape=jax.ShapeDtypeStruct((s, n), out_dtype), compiler_params=_params("parallel"),
    )(x, mul, add)


def _norm_bwd(x, mul, dy, dres, *, plus_one, name):
    s, n = x.shape
    bs = _blk(s, ROW_BLOCK)
    has_res = dres is not None

    def body(*refs):
        if has_res:
            x_ref, m_ref, dy_ref, dres_ref, dx_ref, dm_ref, da_ref = refs
        else:
            x_ref, m_ref, dy_ref, dx_ref, dm_ref, da_ref = refs
        xv = x_ref[...]
        dyv = dy_ref[...].astype(F32)
        r = lax.rsqrt(jnp.mean(xv * xv, axis=-1, keepdims=True) + NORM_EPS)
        xn = xv * r
        mv = m_ref[...] + 1.0 if plus_one else m_ref[...]
        g = dyv * mv
        dx = r * (g - xn * jnp.mean(g * xn, axis=-1, keepdims=True))
        if has_res:
            dx = dx + dres_ref[...]
        dx_ref[...] = dx

        @pl.when(pl.program_id(0) == 0)
        def _():
            dm_ref[...] = jnp.zeros_like(dm_ref)
            da_ref[...] = jnp.zeros_like(da_ref)

        dm_ref[...] += jnp.sum(dyv * xn, axis=0, keepdims=True)
        da_ref[...] += jnp.sum(dyv, axis=0, keepdims=True)

    row = pl.BlockSpec((bs, n), lambda i: (i, 0))
    vec = pl.BlockSpec((1, n), lambda i: (0, 0))
    ins = [x, mul, dy] + ([dres] if has_res else [])
    return pl.pallas_call(
        body, name=name, grid=(s // bs,),
        in_specs=[row, vec, row] + ([row] if has_res else []), out_specs=[row, vec, vec],
        out_shape=[jax.ShapeDtypeStruct((s, n), F32), jax.ShapeDtypeStruct((1, n), F32), jax.ShapeDtypeStruct((1, n), F32)],
        compiler_params=_params("arbitrary"),
    )(*ins)


def _resid_fwd(x, y, gate, *, name):
    s, n = x.shape
    bs = _blk(s, ROW_BLOCK)

    def body(x_ref, y_ref, g_ref, o_ref):
        o_ref[...] = x_ref[...] + g_ref[...] * y_ref[...]

    row = pl.BlockSpec((bs, n), lambda i: (i, 0))
    vec = pl.BlockSpec((1, n), lambda i: (0, 0))
    return pl.pallas_call(
        body, name=name, grid=(s // bs,), in_specs=[row, row, vec], out_specs=row,
        out_shape=jax.ShapeDtypeStruct((s, n), F32), compiler_params=_params("parallel"),
    )(x, y, gate)


def _resid_bwd(dx, y, gate, *, name):
    s, n = dx.shape
    bs = _blk(s, ROW_BLOCK)

    def body(dx_ref, y_ref, g_ref, dy_ref, dg_ref):
        dxv = dx_ref[...]
        dy_ref[...] = (g_ref[...] * dxv).astype(BF16)

        @pl.when(pl.program_id(0) == 0)
        def _():
            dg_ref[...] = jnp.zeros_like(dg_ref)

        dg_ref[...] += jnp.sum(dxv * y_ref[...], axis=0, keepdims=True)

    row = pl.BlockSpec((bs, n), lambda i: (i, 0))
    vec = pl.BlockSpec((1, n), lambda i: (0, 0))
    return pl.pallas_call(
        body, name=name, grid=(s // bs,), in_specs=[row, row, vec], out_specs=[row, vec],
        out_shape=[jax.ShapeDtypeStruct((s, n), BF16), jax.ShapeDtypeStruct((1, n), F32)],
        compiler_params=_params("arbitrary"),
    )(dx, y, gate)


def _final_loss(x, g, target, *, name):
    s, n = x.shape
    bs = _blk(s, ROW_BLOCK)

    def body(x_ref, g_ref, t_ref, loss_ref, dx_ref, dg_ref):
        xv = x_ref[...]
        r = lax.rsqrt(jnp.mean(xv * xv, axis=-1, keepdims=True) + NORM_EPS)
        xn = xv * r
        gv = g_ref[...]
        err = xn * gv - t_ref[...]
        dout = err * (1.0 / n)
        gg = dout * gv
        dx_ref[...] = r * (gg - xn * jnp.mean(gg * xn, axis=-1, keepdims=True))

        @pl.when(pl.program_id(0) == 0)
        def _():
            loss_ref[...] = jnp.zeros_like(loss_ref)
            dg_ref[...] = jnp.zeros_like(dg_ref)

        part = jnp.sum(jnp.sum(err * err, axis=-1, keepdims=True), axis=0, keepdims=True) * (0.5 / n)
        loss_ref[...] += jnp.broadcast_to(part, loss_ref.shape)
        dg_ref[...] += jnp.sum(dout * xn, axis=0, keepdims=True)

    row = pl.BlockSpec((bs, n), lambda i: (i, 0))
    vec = pl.BlockSpec((1, n), lambda i: (0, 0))
    return pl.pallas_call(
        body, name=name, grid=(s // bs,), in_specs=[row, vec, row],
        out_specs=[pl.BlockSpec((1, LANES), lambda i: (0, 0)), row, vec],
        out_shape=[jax.ShapeDtypeStruct((1, LANES), F32), jax.ShapeDtypeStruct((s, n), F32), jax.ShapeDtypeStruct((1, n), F32)],
        compiler_params=_params("arbitrary"),
    )(x, g, target)


def _lane_lt64(shape):
    return lax.broadcasted_iota(jnp.int32, shape, 1) < HEAD_DIM


def _keep_low(x):
    return jnp.where(_lane_lt64(x.shape), x.astype(F32), 0.0).astype(x.dtype)


def _keep_high(x):
    return jnp.where(_lane_lt64(x.shape), 0.0, x.astype(F32)).astype(x.dtype)


def _lane_merge(a, b):
    n = max(a.shape[0], b.shape[0])
    return jnp.where(_lane_lt64((n, LANES)), a, b)


def _pair(x, width, masked):
    if width == HEAD_DIM:
        return (_keep_low(x), _keep_high(x)) if masked else (x, x)
    return x[:, :LANES], x[:, LANES:]


def _qk_t(a, b):
    return lax.dot_general(a, b, (((1,), (1,)), ((), ())), preferred_element_type=F32)


def _attn_specs(s, blk, width, cols, resident):
    w = 2 * width
    if resident:
        return pl.BlockSpec((s, w), lambda p, i: (0, cols + p))
    return pl.BlockSpec((blk, w), lambda p, i: (i, cols + p))


def _attn_fwd(q, k, vt, *, name):
    s = q.shape[0]
    blk = _blk(s, ATT_BLOCK)
    nb = s // blk

    def body(q_ref, k_ref, vt_ref, o_ref, lse_ref):
        i = pl.program_id(1)
        q2 = q_ref[...]
        qh = (q2[:, :LANES], q2[:, LANES:])

        def step(j, carry, nblk, diag):
            start = pl.multiple_of(j * blk, blk)
            k2 = k_ref[pl.ds(start, nblk * blk), :]
            out = []
            for hd in range(2):
                m, acc = carry[hd]
                st = _qk_t(k2[:, hd * LANES:(hd + 1) * LANES], qh[hd])
                if diag:
                    row = lax.broadcasted_iota(jnp.int32, (blk, blk), 0)
                    colq = lax.broadcasted_iota(jnp.int32, (blk, blk), 1)
                    st = jnp.where(row <= colq, st, -1e30)
                m_new = jnp.maximum(m, jnp.max(st, axis=0, keepdims=True))
                pt = jnp.exp2(st - m_new).astype(BF16)
                acc = jnp.exp2(m - m_new) * acc
                for b in range(nblk):
                    acc = acc + jnp.dot(vt_ref[j + b, hd * LANES:(hd + 1) * LANES, :], pt[b * blk:(b + 1) * blk],
                                        preferred_element_type=F32)
                out.append((m_new, acc))
            return tuple(out)

        one = (jnp.full((1, blk), -1e30, F32), jnp.zeros((LANES, blk), F32))
        carry = lax.fori_loop(0, i // 2, lambda j, c: step(2 * j, c, 2, False), (one, one))
        carry = lax.fori_loop(0, i % 2, lambda _, c: step(i - 1, c, 1, False), carry)
        (ma, acca), (mb, accb) = step(i, carry, 1, True)
        la = jnp.max(acca[HEAD_DIM:HEAD_DIM + 8], axis=0, keepdims=True)
        lb = jnp.max(accb[0:8], axis=0, keepdims=True)
        low = lax.broadcasted_iota(jnp.int32, (LANES, blk), 0) < HEAD_DIM
        o_ref[...] = jnp.where(low, acca / la, accb / lb).T
        lse_ref[0, 0] = ma + jnp.log(la) * LOG2E
        lse_ref[1, 0] = mb + jnp.log(lb) * LOG2E

    return pl.pallas_call(
        body, name=name, grid=(N_HEADS // 2, nb),
        in_specs=[pl.BlockSpec((blk, 2 * LANES), lambda p, i: (i, p)), pl.BlockSpec((s, 2 * LANES), lambda p, i: (0, p)),
                  pl.BlockSpec((nb, 2 * LANES, blk), lambda p, i: (0, p, 0))],
        out_specs=[pl.BlockSpec((blk, LANES), lambda p, i: (i, p)), pl.BlockSpec((2, 1, 1, blk), lambda p, i: (p, i, 0, 0))],
        out_shape=[jax.ShapeDtypeStruct((s, N_HEADS * HEAD_DIM), F32), jax.ShapeDtypeStruct((N_HEADS, nb, 1, blk), F32)],
        compiler_params=_params("parallel", "parallel"),
    )(q, k, vt)


def _vt_ones(v, blk):
    s = v.shape[0]
    vt = v.reshape(s // blk, blk, N_HEADS // 2, 2, HEAD_DIM).transpose(0, 2, 3, 4, 1)
    ones = jnp.ones_like(vt[:, :, 0])
    both = jnp.stack([vt[:, :, 0], ones, ones, vt[:, :, 1]], axis=2)
    return both.reshape(s // blk, N_HEADS * LANES, blk)


def _bias_lanes(x, width):
    return jnp.concatenate([x, jnp.zeros(x.shape[:2] + (width - x.shape[2],), x.dtype)], axis=-1)


def _fox_operands(qkv, kb2):
    s = qkv.shape[0]
    nq = N_HEADS * HEAD_DIM
    hi = lax.reduce_precision(kb2, 8, 7)
    mid = lax.reduce_precision(kb2 - hi, 8, 7)
    lo = lax.reduce_precision(kb2 - hi - mid, 8, 7)
    hi, mid, lo = hi.astype(BF16), mid.astype(BF16), lo.astype(BF16)
    ones = jnp.ones((s, N_HEADS, 3), BF16)
    q = jnp.concatenate([qkv[:, :nq].reshape(s, N_HEADS, HEAD_DIM), _bias_lanes(ones, HEAD_DIM)], axis=-1)
    k = jnp.concatenate([qkv[:, nq:2 * nq].reshape(s, N_HEADS, HEAD_DIM),
                         _bias_lanes(jnp.stack([hi, mid, lo], axis=-1), HEAD_DIM)], axis=-1)
    return q.reshape(s, N_HEADS * LANES), k.reshape(s, N_HEADS * LANES)


def _causal_keep(n):
    row = lax.broadcasted_iota(jnp.int32, (n, n), 0)
    col = lax.broadcasted_iota(jnp.int32, (n, n), 1)
    return col <= row


def _attn_delta(o, do, *, name):
    s, n = o.shape
    bs = _blk(s, ROW_BLOCK)

    def body(o_ref, do_ref, d_ref):
        for g in range(n // LANES):
            prod = do_ref[:, g * LANES:(g + 1) * LANES].astype(F32) * o_ref[:, g * LANES:(g + 1) * LANES]
            low = _lane_lt64(prod.shape)
            d_ref[:, g * LANES:(g + 1) * LANES] = _lane_merge(
                jnp.sum(jnp.where(low, prod, 0.0), axis=-1, keepdims=True),
                jnp.sum(jnp.where(low, 0.0, prod), axis=-1, keepdims=True))

    row = pl.BlockSpec((bs, n), lambda i: (i, 0))
    return pl.pallas_call(
        body, name=name, grid=(s // bs,), in_specs=[row, row], out_specs=row,
        out_shape=jax.ShapeDtypeStruct((s, n), F32), compiler_params=_params("parallel"),
    )(o, do)


def _attn_bwd(q, k, v, kb_col, do, lse_row, delta_row, *, qc, kc, vc, width, dq_mult, dk_mult, out_dtype, name):
    s = q.shape[0]
    blk = _blk(s, ATT_BLOCK)
    nb = s // blk
    has_bias = kb_col is not None

    def body(*refs):
        if has_bias:
            q_ref, k_ref, v_ref, kb_ref, do_ref, lse_ref, dl_ref, dk_ref, dv_ref, db_ref, dq_ref, dr_ref = refs
        else:
            q_ref, k_ref, v_ref, do_ref, lse_ref, dl_ref, dk_ref, dv_ref, db_ref, dq_ref = refs
        j = pl.program_id(1)

        @pl.when(j == 0)
        def _():
            dq_ref[...] = jnp.zeros_like(dq_ref)
            if has_bias:
                dr_ref[...] = jnp.zeros_like(dr_ref)

        kh = _pair(k_ref[...], width, True)
        v2 = v_ref[...]
        vh = (_keep_low(v2), _keep_high(v2))
        if has_bias:
            kb2 = kb_ref[0]
            kbh = (kb2[:, 0:1], kb2[:, 1:2])

        def step(i, carry, diag):
            start = pl.multiple_of(i * blk, blk)
            qh = _pair(q_ref[pl.ds(start, blk), :], width, False)
            doi = do_ref[pl.ds(start, blk), :]
            out, dq_parts = [], []
            for hd in range(2):
                dk, dvv, db = carry[hd]
                st = _qk_t(kh[hd], qh[hd])
                if has_bias:
                    st = st + kbh[hd]
                if diag:
                    row = lax.broadcasted_iota(jnp.int32, (blk, blk), 0)
                    colq = lax.broadcasted_iota(jnp.int32, (blk, blk), 1)
                    st = jnp.where(row <= colq, st, -1e30)
                pt = jnp.exp2(st - lse_ref[hd, i])
                dvv = dvv + jnp.dot(pt.astype(BF16), doi, preferred_element_type=F32)
                dst = pt * (_qk_t(vh[hd], doi) - dl_ref[hd, i])
                dsb = dst.astype(BF16)
                dk = dk + jnp.dot(dsb, qh[hd], preferred_element_type=F32)
                db = db + jnp.sum(dst, axis=-1, keepdims=True)
                dq_parts.append(lax.dot_general(dsb, kh[hd], (((0,), (0,)), ((), ())), preferred_element_type=F32))
                if has_bias:
                    dr_ref[hd, i] += jnp.sum(dst, axis=0, keepdims=True)
                out.append((dk, dvv, db))
            rows = pl.ds(start, blk)
            if width == HEAD_DIM:
                dq_ref[rows, :] += (dq_parts[0] + dq_parts[1]) * dq_mult
            else:
                dq_ref[rows, 0:LANES] += dq_parts[0] * dq_mult
                dq_ref[rows, LANES:2 * LANES] += dq_parts[1] * dq_mult
            return tuple(out)

        one = (jnp.zeros((blk, LANES), F32), jnp.zeros((blk, LANES), F32), jnp.zeros((blk, 1), F32))
        carry = step(j, (one, one), True)
        (dka, dva, dba), (dkb, dvb, dbb) = lax.fori_loop(j + 1, nb, lambda i, c: step(i, c, False), carry)
        if width == HEAD_DIM:
            dk = _lane_merge(dka, dkb)
        else:
            dk = jnp.concatenate([dka, dkb], axis=1)
        dk_ref[...] = (dk * dk_mult).astype(out_dtype)
        dv_ref[...] = _lane_merge(dva, dvb).astype(out_dtype)
        db_ref[...] = _lane_merge(dba, dbb)

    stat = pl.BlockSpec((blk, LANES), lambda p, jj: (jj, p))
    rows = pl.BlockSpec((2, nb, 1, blk), lambda p, jj: (p, 0, 0, 0))
    ins = [q, k, v] + ([kb_col] if has_bias else []) + [do, lse_row, delta_row]
    return pl.pallas_call(
        body, name=name, grid=(N_HEADS // 2, nb),
        in_specs=[_attn_specs(s, blk, width, qc, True), _attn_specs(s, blk, width, kc, False),
                  _attn_specs(s, blk, HEAD_DIM, vc, False)]
                 + ([pl.BlockSpec((1, blk, 2), lambda p, jj: (p, jj, 0))] if has_bias else [])
                 + [pl.BlockSpec((s, LANES), lambda p, jj: (0, p)), rows, rows],
        out_specs=[pl.BlockSpec((blk, 2 * width), lambda p, jj: (jj, p)), stat, stat,
                   pl.BlockSpec((s, 2 * width), lambda p, jj: (0, p))] + ([rows] if has_bias else []),
        out_shape=[jax.ShapeDtypeStruct((s, N_HEADS * width), out_dtype), jax.ShapeDtypeStruct((s, N_HEADS * HEAD_DIM), out_dtype),
                   jax.ShapeDtypeStruct((s, N_HEADS * HEAD_DIM), F32), jax.ShapeDtypeStruct((s, N_HEADS * width), F32)]
                  + ([jax.ShapeDtypeStruct((N_HEADS, nb, 1, blk), F32)] if has_bias else []),
        compiler_params=_params("parallel", "arbitrary"),
    )(*ins)


def _head_stat(t):
    return t[:, ::HEAD_DIM]


def _stat_rows(t16, blk):
    s = t16.shape[0]
    return t16.T.reshape(N_HEADS, s // blk, 1, blk)


def _attention(q, k, v, *, name):
    return _attn_fwd(q, k, _vt_ones(v, _blk(q.shape[0], ATT_BLOCK)), name=name + "_fwd")


def _attention_bwd(res, do, *, qc, kc, vc, width, dq_mult, dk_mult, out_dtype, name):
    q, k, v, bias, o, lse_row = res
    s = q.shape[0]
    blk = _blk(s, ATT_BLOCK)
    kb_col = None if bias is None else bias.reshape(s, N_HEADS // 2, 2).transpose(1, 0, 2)
    delta_row = _stat_rows(_head_stat(_attn_delta(o, do, name=name + "_delta")), blk)
    outs = _attn_bwd(q, k, v, kb_col, do, lse_row, delta_row, qc=qc, kc=kc, vc=vc, width=width, dq_mult=dq_mult,
                     dk_mult=dk_mult, out_dtype=out_dtype, name=name + "_bwd")
    dk, dv, dcol, dq = outs[:4]
    if bias is None:
        return dq, dk, dv, None
    return dq, dk, dv, outs[4].reshape(N_HEADS, s).T - _head_stat(dcol)


def _fox_gate_fwd(fl, bf, *, name):
    s, n = fl.shape
    bs = _blk(s, ROW_BLOCK)

    def body(fl_ref, bf_ref, cum_ref, carry_ref):
        @pl.when(pl.program_id(0) == 0)
        def _():
            carry_ref[...] = jnp.zeros_like(carry_ref)

        z = fl_ref[...] + bf_ref[...]
        lf = jnp.minimum(z, 0.0) - jnp.log1p(jnp.exp(-jnp.abs(z)))
        row = lax.broadcasted_iota(jnp.int32, (bs, bs), 0)
        col = lax.broadcasted_iota(jnp.int32, (bs, bs), 1)
        tri = (col <= row).astype(F32)
        cum_ref[...] = jnp.dot(tri, lf, preferred_element_type=F32, precision=HIGHEST) + carry_ref[...]
        carry_ref[...] += jnp.sum(lf, axis=0, keepdims=True)

    return pl.pallas_call(
        body, name=name, grid=(s // bs,),
        in_specs=[pl.BlockSpec((bs, n), lambda i: (i, 0)), pl.BlockSpec((1, n), lambda i: (0, 0))],
        out_specs=pl.BlockSpec((bs, n), lambda i: (i, 0)),
        out_shape=jax.ShapeDtypeStruct((s, n), F32), scratch_shapes=[pltpu.VMEM((1, n), F32)],
        compiler_params=_params("arbitrary"),
    )(fl, bf)


def _fox_gate_bwd(fl, bf, dcum, *, name):
    s, n = fl.shape
    bs = _blk(s, ROW_BLOCK)
    nb = s // bs

    def body(fl_ref, bf_ref, dc_ref, dz_ref, dbf_ref, carry_ref):
        @pl.when(pl.program_id(0) == 0)
        def _():
            carry_ref[...] = jnp.zeros_like(carry_ref)
            dbf_ref[...] = jnp.zeros_like(dbf_ref)

        dc = dc_ref[...]
        row = lax.broadcasted_iota(jnp.int32, (bs, bs), 0)
        col = lax.broadcasted_iota(jnp.int32, (bs, bs), 1)
        tri = (col >= row).astype(F32)
        dlf = jnp.dot(tri, dc, preferred_element_type=F32, precision=HIGHEST) + carry_ref[...]
        carry_ref[...] += jnp.sum(dc, axis=0, keepdims=True)
        z = fl_ref[...] + bf_ref[...]
        dz = dlf / (1.0 + jnp.exp(z))
        dz_ref[...] = dz
        dbf_ref[...] += jnp.sum(dz, axis=0, keepdims=True)

    rev = pl.BlockSpec((bs, n), lambda i: (nb - 1 - i, 0))
    vec = pl.BlockSpec((1, n), lambda i: (0, 0))
    return pl.pallas_call(
        body, name=name, grid=(nb,), in_specs=[rev, vec, rev], out_specs=[rev, vec],
        out_shape=[jax.ShapeDtypeStruct((s, n), F32), jax.ShapeDtypeStruct((1, n), F32)],
        scratch_shapes=[pltpu.VMEM((1, n), F32)], compiler_params=_params("arbitrary"),
    )(fl, bf, dcum)


def _rope(x1, x2, cos, sin, *, negate, name):
    s, n = x1.shape
    bs = _blk(s, ROW_BLOCK)

    def body(a_ref, b_ref, c_ref, s_ref, o1_ref, o2_ref):
        a, b, cv = a_ref[...], b_ref[...], c_ref[...]
        sv = -s_ref[...] if negate else s_ref[...]
        o1_ref[...] = a * cv - b * sv
        o2_ref[...] = b * cv + a * sv

    row = pl.BlockSpec((bs, n), lambda i: (i, 0))
    return pl.pallas_call(
        body, name=name, grid=(s // bs,), in_specs=[row] * 4, out_specs=[row, row],
        out_shape=[jax.ShapeDtypeStruct((s, n), F32)] * 2, compiler_params=_params("parallel"),
    )(x1, x2, cos, sin)


def _rope_heads(x, ta, tb, tc, *, out_dtype, name):
    s, n = x.shape
    bs = _blk(s, ROW_BLOCK)

    def body(x_ref, a_ref, b_ref, c_ref, o_ref):
        av, bv, cv = a_ref[...], b_ref[...], c_ref[...]
        for g in range(n // LANES):
            xg = x_ref[:, g * LANES:(g + 1) * LANES]
            og = xg * av + pltpu.roll(xg, LANES - MLA_ROPE_HALF, 1) * bv + pltpu.roll(xg, MLA_ROPE_HALF, 1) * cv
            o_ref[:, g * LANES:(g + 1) * LANES] = og.astype(out_dtype)

    row = pl.BlockSpec((bs, n), lambda i: (i, 0))
    tab = pl.BlockSpec((bs, LANES), lambda i: (i, 0))
    return pl.pallas_call(
        body, name=name, grid=(s // bs,), in_specs=[row, tab, tab, tab], out_specs=row,
        out_shape=jax.ShapeDtypeStruct((s, n), out_dtype), compiler_params=_params("parallel"),
    )(x, ta, tb, tc)


def _group_sum(x, *, name):
    s, n = x.shape
    bs = _blk(s, ROW_BLOCK)

    def body(x_ref, o_ref):
        acc = x_ref[:, 0:LANES]
        for g in range(1, n // LANES):
            acc = acc + x_ref[:, g * LANES:(g + 1) * LANES]
        o_ref[...] = acc

    return pl.pallas_call(
        body, name=name, grid=(s // bs,), in_specs=[pl.BlockSpec((bs, n), lambda i: (i, 0))],
        out_specs=pl.BlockSpec((bs, LANES), lambda i: (i, 0)),
        out_shape=jax.ShapeDtypeStruct((s, LANES), F32), compiler_params=_params("parallel"),
    )(x)


def _shift_down(x, k):
    return pltpu.roll(x, k, 0)


def _conv_rows(ext, w_ref, b_ref, rows):
    y = b_ref[...] + w_ref[0:1, :] * _shift_down(ext, 2) + w_ref[1:2, :] * _shift_down(ext, 1) + w_ref[2:3, :] * ext
    return y[8:8 + rows]


def _conv_gate_fwd(u, cw, cb, *, name):
    s, f2 = u.shape
    f = f2 // 2
    nf = f // LANES
    r = _blk(s, CONV_ROWS)
    r8 = r // 8

    def body(ug_ref, ugp_ref, uv_ref, uvp_ref, wg_ref, wv_ref, bg_ref, bv_ref, o_ref):
        first = pl.program_id(1) == 0

        def conv(cur_ref, prev_ref, w_ref, b_ref):
            prev = jnp.where(first, 0.0, prev_ref[...])
            return _conv_rows(jnp.concatenate([prev, cur_ref[...]], axis=0), w_ref, b_ref, r)

        yg = conv(ug_ref, ugp_ref, wg_ref, bg_ref)
        yv = conv(uv_ref, uvp_ref, wv_ref, bv_ref)
        o_ref[...] = (yg * jax.nn.sigmoid(yg) * yv).astype(BF16)

    def cur(off):
        return pl.BlockSpec((r, LANES), lambda c, i: (i, c + off))

    def prev(off):
        return pl.BlockSpec((8, LANES), lambda c, i: (jnp.maximum(i * r8 - 1, 0), c + off))

    def wspec(rows, off):
        return pl.BlockSpec((rows, LANES), lambda c, i: (0, c + off))

    return pl.pallas_call(
        body, name=name, grid=(nf, s // r),
        in_specs=[cur(0), prev(0), cur(nf), prev(nf), wspec(3, 0), wspec(3, nf), wspec(1, 0), wspec(1, nf)],
        out_specs=pl.BlockSpec((r, LANES), lambda c, i: (i, c)),
        out_shape=jax.ShapeDtypeStruct((s, f), BF16), compiler_params=_params("parallel", "parallel"),
    )(u, u, u, u, cw, cw, cb, cb)


def _conv_gate_bwd(u, cw, cb, dg, *, name):
    s, f2 = u.shape
    f = f2 // 2
    nf = f // LANES
    r = _blk(s, CONV_ROWS)
    r8 = r // 8
    nr = s // r

    def body(ug_ref, ugp_ref, ugn_ref, uv_ref, uvp_ref, uvn_ref, wg_ref, wv_ref, bg_ref, bv_ref, dg_ref, dgn_ref,
             dug_ref, duv_ref, dwg_ref, dwv_ref, dbg_ref, dbv_ref):
        i = pl.program_id(1)
        first, last = i == 0, i == nr - 1

        def ext_of(cur_ref, prev_ref, next_ref):
            prev = jnp.where(first, 0.0, prev_ref[...])
            return jnp.concatenate([prev, cur_ref[...], next_ref[...]], axis=0)

        eg, ev = ext_of(ug_ref, ugp_ref, ugn_ref), ext_of(uv_ref, uvp_ref, uvn_ref)
        yg = _conv_rows(eg, wg_ref, bg_ref, r + 8)
        yv = _conv_rows(ev, wv_ref, bv_ref, r + 8)
        dgn = jnp.where(last, 0.0, dgn_ref[...])
        dgx = jnp.concatenate([dg_ref[...], dgn], axis=0)
        sg = jax.nn.sigmoid(yg)
        dyg = dgx * yv * (sg * (1.0 + yg * (1.0 - sg)))
        dyv = dgx * (yg * sg)

        @pl.when(i == 0)
        def _():
            for ref in (dwg_ref, dwv_ref, dbg_ref, dbv_ref):
                ref[...] = jnp.zeros_like(ref)

        def grads(dy, ext, w_ref, du_ref, dw_ref, db_ref):
            n = r + 8
            du = w_ref[2:3, :] * dy + w_ref[1:2, :] * pltpu.roll(dy, n - 1, 0) + w_ref[0:1, :] * pltpu.roll(dy, n - 2, 0)
            du_ref[...] = du[0:r].astype(BF16)
            dyc = dy[0:r]
            db_ref[...] += jnp.sum(dyc, axis=0, keepdims=True)
            ext_c = ext[0:r + 8]
            dw_ref[0:1, :] += jnp.sum(dyc * _shift_down(ext_c, 2)[8:], axis=0, keepdims=True)
            dw_ref[1:2, :] += jnp.sum(dyc * _shift_down(ext_c, 1)[8:], axis=0, keepdims=True)
            dw_ref[2:3, :] += jnp.sum(dyc * ext_c[8:], axis=0, keepdims=True)

        grads(dyg, eg, wg_ref, dug_ref, dwg_ref, dbg_ref)
        grads(dyv, ev, wv_ref, duv_ref, dwv_ref, dbv_ref)

    def cur(off):
        return pl.BlockSpec((r, LANES), lambda c, i: (i, c + off))

    def prev(off):
        return pl.BlockSpec((8, LANES), lambda c, i: (jnp.maximum(i * r8 - 1, 0), c + off))

    def nxt(off):
        return pl.BlockSpec((8, LANES), lambda c, i: (jnp.minimum((i + 1) * r8, s // 8 - 1), c + off))

    def wspec(rows, off):
        return pl.BlockSpec((rows, LANES), lambda c, i: (0, c + off))

    outs = pl.pallas_call(
        body, name=name, grid=(nf, nr),
        in_specs=[cur(0), prev(0), nxt(0), cur(nf), prev(nf), nxt(nf), wspec(3, 0), wspec(3, nf), wspec(1, 0), wspec(1, nf),
                  cur(0), nxt(0)],
        out_specs=[cur(0), cur(0), wspec(3, 0), wspec(3, 0), wspec(1, 0), wspec(1, 0)],
        out_shape=[jax.ShapeDtypeStruct((s, f), BF16), jax.ShapeDtypeStruct((s, f), BF16),
                   jax.ShapeDtypeStruct((3, f), F32), jax.ShapeDtypeStruct((3, f), F32),
                   jax.ShapeDtypeStruct((1, f), F32), jax.ShapeDtypeStruct((1, f), F32)],
        compiler_params=_params("parallel", "arbitrary"),
    )(u, u, u, u, u, u, cw, cw, cb, cb, dg, dg)
    dug, duv, dwg, dwv, dbg, dbv = outs
    return jnp.concatenate([dug, duv], axis=1), jnp.concatenate([dwg, dwv], axis=1), jnp.concatenate([dbg, dbv], axis=1)


def _adamw(w, g, m, v, *, slabs, name):
    shape = w.shape
    cols = shape[-1]
    rows = w.size // cols
    w2, m2, v2 = (t.reshape(rows, cols) for t in (w, m, v))
    g2 = g.reshape((N_DEV, rows, cols) if slabs else (rows, cols))
    br = _row_blk(rows, ROW_BLOCK // 2 if slabs else ROW_BLOCK)

    def body(w_ref, g_ref, m_ref, v_ref, go_ref, d_ref, nm_ref, nv_ref):
        if slabs:
            gv = g_ref[0].astype(F32)
            for p in range(1, N_DEV):
                gv = gv + g_ref[p].astype(F32)
        else:
            gv = g_ref[...]
        nm = ADAM_B1 * m_ref[...] + (1.0 - ADAM_B1) * gv
        nv = ADAM_B2 * v_ref[...] + (1.0 - ADAM_B2) * (gv * gv)
        m_hat = nm / (1.0 - ADAM_B1 ** ADAM_STEP)
        v_hat = nv / (1.0 - ADAM_B2 ** ADAM_STEP)
        go_ref[...] = gv
        d_ref[...] = -ADAM_LR * (m_hat / (jnp.sqrt(v_hat) + ADAM_EPS) + ADAM_WD * w_ref[...])
        nm_ref[...] = nm
        nv_ref[...] = nv

    spec = pl.BlockSpec((br, cols), lambda i: (i, 0))
    gspec = pl.BlockSpec((N_DEV, br, cols), lambda i: (0, i, 0)) if slabs else spec
    outs = pl.pallas_call(
        body, name=name, grid=(rows // br,), in_specs=[spec, gspec, spec, spec], out_specs=[spec] * 4,
        out_shape=[jax.ShapeDtypeStruct((rows, cols), F32)] * 4, compiler_params=_params("parallel"),
    )(w2, g2, m2, v2)
    return tuple(t.reshape(shape) for t in outs)


def _exchange(xs, *, same_src, name):
    n = len(xs)
    slabs = [x.shape if same_src else x.shape[1:] for x in xs]

    def body(*refs):
        x_refs, o_refs = refs[:n], refs[n:2 * n]
        send_sems, recv_sems, loc_sems = refs[2 * n:]
        ix, iy, ic = lax.axis_index("x"), lax.axis_index("y"), lax.axis_index("c")
        me = 4 * ix + 2 * iy + ic
        local, sends, recvs = [], [], []
        for a in range(n):
            def src(p, a=a):
                return x_refs[a] if same_src else x_refs[a].at[p]

            local.append(pltpu.make_async_copy(src(me), o_refs[a].at[me], loc_sems.at[a]))
            for k in (1, 2, 4, 3, 5, 6, 7):
                px = 1 - ix if k & 4 else ix
                py = 1 - iy if k & 2 else iy
                pc = 1 - ic if k & 1 else ic
                p = 4 * px + 2 * py + pc
                for dst, out in ((me, sends), (p, recvs)):
                    out.append(pltpu.make_async_remote_copy(
                        src_ref=src(p), dst_ref=o_refs[a].at[dst], send_sem=send_sems.at[a, k - 1],
                        recv_sem=recv_sems.at[a, k - 1], device_id=(px, py, pc), device_id_type=pl.DeviceIdType.MESH))
        for cp in local + sends:
            cp.start()
        for cp in recvs:
            cp.wait_recv()
        for cp in sends:
            cp.wait_send()
        for cp in local:
            cp.wait()

    return pl.pallas_call(
        body, name=name,
        in_specs=[pl.BlockSpec(memory_space=pl.ANY)] * n, out_specs=[pl.BlockSpec(memory_space=pl.ANY)] * n,
        out_shape=[jax.ShapeDtypeStruct((N_DEV,) + tuple(sl), x.dtype) for sl, x in zip(slabs, xs)],
        scratch_shapes=[pltpu.SemaphoreType.DMA((n, N_DEV - 1)), pltpu.SemaphoreType.DMA((n, N_DEV - 1)),
                        pltpu.SemaphoreType.DMA((n,))],
        compiler_params=pltpu.CompilerParams(has_side_effects=True, vmem_limit_bytes=VMEM_LIMIT_BYTES),
    )(*xs)


def _sum_slabs(x, *, name):
    n, r, c = x.shape
    br = _row_blk(r, ROW_BLOCK)

    def body(x_ref, o_ref):
        acc = x_ref[0]
        for p in range(1, n):
            acc = acc + x_ref[p]
        o_ref[...] = acc

    return pl.pallas_call(
        body, name=name, grid=(r // br,), in_specs=[pl.BlockSpec((n, br, c), lambda i: (0, i, 0))],
        out_specs=pl.BlockSpec((br, c), lambda i: (i, 0)),
        out_shape=jax.ShapeDtypeStruct((r, c), F32), compiler_params=_params("parallel"),
    )(x)


def _silu(x, *, name):
    def body(x_ref, o_ref):
        xv = x_ref[...]
        o_ref[...] = (xv * jax.nn.sigmoid(xv)).astype(BF16)

    return pl.pallas_call(body, name=name, out_shape=jax.ShapeDtypeStruct(x.shape, BF16),
                          compiler_params=_params())(x)


_BIG = {"fox_w_in": 2, "fox_w_o": 1, "mla_w_a": 1, "mla_w_uq": 2, "mla_w_ukv": 2, "mla_w_o": 1, "ffn_w_in": 2, "ffn_w_out": 1}
_SMALL = {"mla_g_q": 1, "mla_g_kv": 1, "ffn_conv_w": 2}
_REPL = ("fox_b_f", "ffn_conv_b", "final_g")


def _gathered_to_full(g, axis):
    full = jnp.moveaxis(g, 0, axis)
    shape = list(full.shape)
    shape[axis:axis + 2] = [shape[axis] * shape[axis + 1]]
    return full.reshape(shape)


def _full_to_chunks(full, axis):
    shape = list(full.shape)
    shape[axis:axis + 1] = [N_DEV, shape[axis] // N_DEV]
    return jnp.moveaxis(full.reshape(shape), axis, 0)


def _per_head(parts, s_or_rows):
    return jnp.concatenate([p.reshape(s_or_rows, N_HEADS, -1) for p in parts], axis=-1).reshape(s_or_rows, -1)


def kernel(x, c, ada_w, ada_b, fox_w_in, fox_b_f, fox_w_o, mla_w_a, mla_g_q, mla_g_kv, mla_w_uq, mla_w_ukv, mla_w_o, ffn_w_in, ffn_conv_w, ffn_conv_b, ffn_w_out, final_g, loss_target, m_ada_w, m_ada_b, m_fox_w_in, m_fox_b_f, m_fox_w_o, m_mla_w_a, m_mla_g_q, m_mla_g_kv, m_mla_w_uq, m_mla_w_ukv, m_mla_w_o, m_ffn_w_in, m_ffn_conv_w, m_ffn_conv_b, m_ffn_w_out, m_final_g, v_ada_w, v_ada_b, v_fox_w_in, v_fox_b_f, v_fox_w_o, v_mla_w_a, v_mla_g_q, v_mla_g_kv, v_mla_w_uq, v_mla_w_ukv, v_mla_w_o, v_ffn_w_in, v_ffn_conv_w, v_ffn_conv_b, v_ffn_w_out, v_final_g):
    weights = dict(ada_w=ada_w, ada_b=ada_b, fox_w_in=fox_w_in, fox_b_f=fox_b_f, fox_w_o=fox_w_o, mla_w_a=mla_w_a,
                   mla_g_q=mla_g_q, mla_g_kv=mla_g_kv, mla_w_uq=mla_w_uq, mla_w_ukv=mla_w_ukv, mla_w_o=mla_w_o,
                   ffn_w_in=ffn_w_in, ffn_conv_w=ffn_conv_w, ffn_conv_b=ffn_conv_b, ffn_w_out=ffn_w_out, final_g=final_g)
    mom_m = dict(ada_w=m_ada_w, ada_b=m_ada_b, fox_w_in=m_fox_w_in, fox_b_f=m_fox_b_f, fox_w_o=m_fox_w_o, mla_w_a=m_mla_w_a,
                 mla_g_q=m_mla_g_q, mla_g_kv=m_mla_g_kv, mla_w_uq=m_mla_w_uq, mla_w_ukv=m_mla_w_ukv, mla_w_o=m_mla_w_o,
                 ffn_w_in=m_ffn_w_in, ffn_conv_w=m_ffn_conv_w, ffn_conv_b=m_ffn_conv_b, ffn_w_out=m_ffn_w_out, final_g=m_final_g)
    mom_v = dict(ada_w=v_ada_w, ada_b=v_ada_b, fox_w_in=v_fox_w_in, fox_b_f=v_fox_b_f, fox_w_o=v_fox_w_o, mla_w_a=v_mla_w_a,
                 mla_g_q=v_mla_g_q, mla_g_kv=v_mla_g_kv, mla_w_uq=v_mla_w_uq, mla_w_ukv=v_mla_w_ukv, mla_w_o=v_mla_w_o,
                 ffn_w_in=v_ffn_w_in, ffn_conv_w=v_ffn_conv_w, ffn_conv_b=v_ffn_conv_b, ffn_w_out=v_ffn_w_out, final_g=v_final_g)
    order = list(weights)
    x0 = x[0]
    target = loss_target[0]
    s = x0.shape[0]
    d = D_MODEL
    cols = ada_w.shape[-1]
    nq = N_HEADS * HEAD_DIM

    small_names = ["c"] + list(_SMALL)
    small_all = dict(zip(small_names, _exchange([c] + [weights[n] for n in _SMALL], same_src=True, name="gather_small")))
    c_all = small_all["c"].reshape(N_DEV, d)
    g_q = _gathered_to_full(small_all["mla_g_q"], 1)
    g_kv = _gathered_to_full(small_all["mla_g_kv"], 1)
    conv_w = _gathered_to_full(small_all["ffn_conv_w"], 2)

    c_pad = _pad_axis(c_all, 0, LANES)
    silu_c = _silu(c_pad, name="silu_c")
    w_ada = ada_w.reshape(4, d, cols)
    b_ada = ada_b.reshape(4, 1, cols)
    mods = [_matmul(silu_c, w_ada[i], name=f"ada_mod{i}")[:N_DEV] + b_ada[i] for i in range(4)]
    mod_send = _pad_axis(jnp.stack(mods, axis=1), 1, 8)
    mod_recv, = _exchange([mod_send], same_src=False, name="scatter_mod")
    mod = mod_recv[:, :4].transpose(1, 0, 2).reshape(4, 3 * d)
    shift = [mod[i:i + 1, 0:d] for i in range(4)]
    scale = [mod[i:i + 1, d:2 * d] for i in range(4)]
    gate = [mod[i:i + 1, 2 * d:3 * d] for i in range(4)]

    big_all = _exchange([weights[n].astype(BF16) for n in _BIG], same_src=True, name="gather_weights")
    wfull = {n: _gathered_to_full(g, _BIG[n]) for n, g in zip(_BIG, big_all)}

    w_fox_in = _pad_axis(wfull["fox_w_in"][0], 1, LANES)
    w_fox_qkv, w_fox_f = w_fox_in[:, :3 * nq], w_fox_in[:, 3 * nq:]
    w_fox_o = wfull["fox_w_o"][0]
    w_a = _pad_axis(wfull["mla_w_a"][0], 1, LANES)
    wq = wfull["mla_w_uq"][0].reshape(MLA_Q_RANK, N_HEADS, HEAD_DIM + MLA_ROPE_DIM)
    w_uq = _pad_axis(wq, 2, LANES).reshape(MLA_Q_RANK, N_HEADS * LANES)
    wkv = wfull["mla_w_ukv"][0].reshape(MLA_KV_RANK, N_HEADS, 2 * HEAD_DIM)
    w_ukv = jnp.concatenate([wkv[:, :, :HEAD_DIM].reshape(MLA_KV_RANK, -1), wkv[:, :, HEAD_DIM:].reshape(MLA_KV_RANK, -1)], axis=1)
    w_mla_o = wfull["mla_w_o"][0]
    w_ffn_in = wfull["ffn_w_in"]
    w_ffn_out = wfull["ffn_w_out"]
    conv_b = ffn_conv_b

    fox_scale = HEAD_DIM ** -0.5
    mla_scale = (HEAD_DIM + MLA_ROPE_DIM) ** -0.5
    pos = jnp.arange(s, dtype=F32)
    inv_freq = ROPE_BASE ** (-jnp.arange(0, MLA_ROPE_DIM, 2, dtype=F32) / MLA_ROPE_DIM)
    ang = pos[:, None] * inv_freq[None, :]
    cos16, sin16 = jnp.cos(ang), jnp.sin(ang)
    z16, z32, z64 = jnp.zeros((s, 16), F32), jnp.zeros((s, 32), F32), jnp.zeros((s, 64), F32)
    tab_a = jnp.concatenate([jnp.ones((s, 64), F32), cos16, cos16, z32], axis=1) * (mla_scale * LOG2E)
    tab_b = jnp.concatenate([z64, -sin16, z16, z32], axis=1) * (mla_scale * LOG2E)
    tab_c = jnp.concatenate([z64, z16, sin16, z32], axis=1) * (mla_scale * LOG2E)

    h0 = _norm_fwd(x0, scale[0], shift[0], plus_one=True, out_dtype=BF16, name="ada_fwd0")
    q_mult = jnp.concatenate([jnp.full((1, nq), fox_scale * LOG2E, F32), jnp.ones((1, 2 * nq), F32)], axis=1)
    qkv = _matmul(h0, w_fox_qkv, out_dtype=BF16, col_scale=q_mult, name="fox_proj")
    fl = _matmul(h0, w_fox_f, name="fox_proj_f")[:, :N_HEADS]
    cum = _fox_gate_fwd(fl, fox_b_f, name="fox_gate_fwd")
    fox_cfg = dict(qc=0, kc=N_HEADS // 2, vc=N_HEADS, width=HEAD_DIM)
    kb2 = cum * -LOG2E
    fo, fox_lse = _attention(*_fox_operands(qkv, kb2), qkv[:, 2 * nq:], name="fox_attn")
    fox_res = (qkv, qkv, qkv, kb2, fo, fox_lse)
    y0 = _matmul(fo, w_fox_o, name="fox_out")
    x1 = _resid_fwd(x0, y0, gate[0], name="resid_fwd0")

    def ffn_fwd(xin, li, sub):
        hh = _norm_fwd(xin, scale[sub], shift[sub], plus_one=True, out_dtype=BF16, name=f"ada_fwd{sub}")
        u = _matmul(hh, w_ffn_in[li], name=f"ffn_up{li}")
        g = _conv_gate_fwd(u, conv_w[li], conv_b[li:li + 1], name=f"conv_fwd{li}")
        y = _matmul(g, w_ffn_out[li], name=f"ffn_down{li}")
        return _resid_fwd(xin, y, gate[sub], name=f"resid_fwd{sub}"), (hh, u, g, y)

    x2, ffn0_res = ffn_fwd(x1, 0, 1)

    h2 = _norm_fwd(x2, scale[2], shift[2], plus_one=True, out_dtype=BF16, name="ada_fwd2")
    a = _matmul(h2, w_a, name="mla_a")
    a_q, a_kv = a[:, :MLA_Q_RANK], a[:, MLA_Q_RANK:MLA_Q_RANK + MLA_KV_RANK]
    kr1 = a[:, MLA_Q_RANK + MLA_KV_RANK:MLA_Q_RANK + MLA_KV_RANK + MLA_ROPE_HALF]
    kr2 = a[:, MLA_Q_RANK + MLA_KV_RANK + MLA_ROPE_HALF:MLA_Q_RANK + MLA_KV_RANK + MLA_ROPE_DIM]
    cq = _norm_fwd(a_q, g_q, jnp.zeros_like(g_q), plus_one=False, out_dtype=BF16, name="mla_norm_q")
    ckv = _norm_fwd(a_kv, g_kv, jnp.zeros_like(g_kv), plus_one=False, out_dtype=BF16, name="mla_norm_kv")
    qf = _matmul(cq, w_uq, name="mla_uq")
    kvf = _matmul(ckv, w_ukv, out_dtype=BF16, name="mla_ukv")
    mq = _rope_heads(qf, tab_a, tab_b, tab_c, out_dtype=BF16, name="rope_q")
    kk1, kk2 = _rope(kr1, kr2, cos16, sin16, negate=False, name="rope_k")
    k_tail = jnp.concatenate([kk1, kk2, z32], axis=1).astype(BF16)
    mk = jnp.concatenate([kvf[:, :nq].reshape(s, N_HEADS, HEAD_DIM),
                          jnp.broadcast_to(k_tail[:, None, :], (s, N_HEADS, HEAD_DIM))], axis=-1).reshape(s, N_HEADS * LANES)
    mla_cfg = dict(qc=0, kc=0, vc=N_HEADS // 2, width=LANES)
    mo, mla_lse = _attention(mq, mk, kvf[:, nq:], name="mla_attn")
    mla_res = (mq, mk, kvf, None, mo, mla_lse)
    y2 = _matmul(mo, w_mla_o, name="mla_out")
    x3 = _resid_fwd(x2, y2, gate[2], name="resid_fwd2")

    x4, ffn1_res = ffn_fwd(x3, 1, 3)

    loss_vec, dx4, d_final_g = _final_loss(x4, final_g.reshape(1, d), target, name="final_loss")
    loss = lax.psum(loss_vec[0, 0], ("x", "y", "c"))

    grads = {}
    dmod = [None] * 4

    def ffn_bwd(dx_out, xin, li, sub, res):
        hh, u, g, y = res
        dy, dgate = _resid_bwd(dx_out, y, gate[sub], name=f"resid_bwd{sub}")
        gw_out = _matmul(g, dy, ta=True, out_dtype=BF16, name=f"ffn_down_dw{li}")
        dg = _matmul(dy, w_ffn_out[li], tb=True, name=f"ffn_down_dx{li}")
        du, dcw, dcb = _conv_gate_bwd(u, conv_w[li], conv_b[li:li + 1], dg, name=f"conv_bwd{li}")
        gw_in = _matmul(hh, du, ta=True, out_dtype=BF16, name=f"ffn_up_dw{li}")
        dh = _matmul(du, w_ffn_in[li], tb=True, out_dtype=BF16, name=f"ffn_up_dx{li}")
        dx_in, dscale, dshift = _norm_bwd(xin, scale[sub], dh, dx_out, plus_one=True, name=f"ada_bwd{sub}")
        dmod[sub] = jnp.concatenate([dshift, dscale, dgate], axis=1)
        return dx_in, gw_in, dcw, dcb, gw_out

    dx3, gw_in1, dcw1, dcb1, gw_out1 = ffn_bwd(dx4, x3, 1, 3, ffn1_res)

    dy2, dgate2 = _resid_bwd(dx3, y2, gate[2], name="resid_bwd2")
    grads["mla_w_o"] = _matmul(mo, dy2, ta=True, out_dtype=BF16, name="mla_out_dw")[None]
    dmo = _matmul(dy2, w_mla_o, tb=True, out_dtype=BF16, name="mla_out_dx")
    dmq, dmk, dmv, _ = _attention_bwd(mla_res, dmo, dq_mult=1.0 / LOG2E, dk_mult=1.0 / LOG2E, out_dtype=F32,
                                      name="mla_attn", **mla_cfg)
    dqf = _rope_heads(dmq, tab_a, -tab_b, -tab_c, out_dtype=BF16, name="rope_q_bwd")
    g_uq = _matmul(cq, dqf, ta=True, out_dtype=BF16, name="mla_uq_dw")
    dcq = _matmul(dqf, w_uq, tb=True, name="mla_uq_dx")
    dmk3 = dmk.reshape(s, N_HEADS, LANES)
    dkr = _group_sum(dmk, name="mla_krope_sum")
    dkr1, dkr2 = _rope(dkr[:, HEAD_DIM:HEAD_DIM + MLA_ROPE_HALF], dkr[:, HEAD_DIM + MLA_ROPE_HALF:HEAD_DIM + MLA_ROPE_DIM],
                       cos16, sin16, negate=True, name="rope_k_bwd")
    dkvf = jnp.concatenate([dmk3[:, :, :HEAD_DIM].reshape(s, nq).astype(BF16), dmv.astype(BF16)], axis=1)
    g_ukv = _matmul(ckv, dkvf, ta=True, out_dtype=BF16, name="mla_ukv_dw")
    dckv = _matmul(dkvf, w_ukv, tb=True, name="mla_ukv_dx")
    da_q, dg_q, _ = _norm_bwd(a_q, g_q, dcq, None, plus_one=False, name="mla_norm_q_bwd")
    da_kv, dg_kv, _ = _norm_bwd(a_kv, g_kv, dckv, None, plus_one=False, name="mla_norm_kv_bwd")
    da = jnp.concatenate([da_q, da_kv, dkr1, dkr2, jnp.zeros((s, w_a.shape[1] - 672), F32)], axis=1).astype(BF16)
    grads["mla_w_a"] = _matmul(h2, da, ta=True, out_dtype=BF16, name="mla_a_dw")[None, :, :672]
    dh2 = _matmul(da, w_a, tb=True, out_dtype=BF16, name="mla_a_dx")
    dx2, dscale2, dshift2 = _norm_bwd(x2, scale[2], dh2, dx3, plus_one=True, name="ada_bwd2")
    dmod[2] = jnp.concatenate([dshift2, dscale2, dgate2], axis=1)
    grads["mla_w_uq"] = g_uq.reshape(MLA_Q_RANK, N_HEADS, LANES)[:, :, :HEAD_DIM + MLA_ROPE_DIM].reshape(1, MLA_Q_RANK, -1)
    grads["mla_w_ukv"] = _per_head([g_ukv[:, :nq], g_ukv[:, nq:]], MLA_KV_RANK)[None]
    grads["mla_g_q"], grads["mla_g_kv"] = dg_q, dg_kv

    dx1, gw_in0, dcw0, dcb0, gw_out0 = ffn_bwd(dx2, x1, 0, 1, ffn0_res)
    grads["ffn_w_in"] = jnp.stack([gw_in0, gw_in1])
    grads["ffn_w_out"] = jnp.stack([gw_out0, gw_out1])
    grads["ffn_conv_w"] = jnp.stack([dcw0, dcw1])
    g_conv_b = jnp.concatenate([dcb0, dcb1], axis=0)

    dy0, dgate0 = _resid_bwd(dx1, y0, gate[0], name="resid_bwd0")
    grads["fox_w_o"] = _matmul(fo, dy0, ta=True, out_dtype=BF16, name="fox_out_dw")[None]
    dfo = _matmul(dy0, w_fox_o, tb=True, out_dtype=BF16, name="fox_out_dx")
    dfq, dfk, dfv, dcum = _attention_bwd(fox_res, dfo, dq_mult=fox_scale, dk_mult=1.0 / LOG2E, out_dtype=BF16,
                                         name="fox_attn", **fox_cfg)
    dfl, g_b_f = _fox_gate_bwd(fl, fox_b_f, dcum, name="fox_gate_bwd")
    dproj = jnp.concatenate([dfq.astype(BF16), dfk, dfv, _pad_axis(dfl, 1, LANES).astype(BF16)], axis=1)
    grads["fox_w_in"] = _matmul(h0, dproj, ta=True, out_dtype=BF16, name="fox_proj_dw")[None, :, :3 * nq + N_HEADS]
    dh0 = _matmul(dproj, w_fox_in, tb=True, out_dtype=BF16, name="fox_proj_dx")
    dx0, dscale0, dshift0 = _norm_bwd(x0, scale[0], dh0, dx1, plus_one=True, name="ada_bwd0")
    dmod[0] = jnp.concatenate([dshift0, dscale0, dgate0], axis=1)

    dmod_send = _pad_axis(jnp.stack(dmod, axis=0).reshape(4, N_DEV, cols).transpose(1, 0, 2), 1, 8)
    dmod_recv, = _exchange([dmod_send], same_src=False, name="scatter_dmod")
    dmod_all = dmod_recv[:, :4]
    dmod_pad = _pad_axis(dmod_all, 0, LANES)
    g_ada_w = jnp.stack([_matmul(silu_c, dmod_pad[:, i], ta=True, name=f"ada_dw{i}") for i in range(4)])
    grads["ada_w"] = g_ada_w.reshape(ada_w.shape)
    grads["ada_b"] = _sum_slabs(dmod_recv, name="ada_db")[:4].reshape(ada_b.shape)

    sharded = list(_BIG) + list(_SMALL)
    axes = {**_BIG, **_SMALL}
    recv = _exchange([_full_to_chunks(grads[n], axes[n]) for n in sharded], same_src=False, name="scatter_grads")
    grads.update(dict(zip(sharded, recv)))
    repl = _exchange([g_b_f, g_conv_b, d_final_g], same_src=True, name="gather_repl_grads")
    grads.update(dict(zip(_REPL, repl)))

    grad_out, deltas, new_m, new_v = {}, {}, {}, {}
    for n in order:
        grad_out[n], deltas[n], new_m[n], new_v[n] = _adamw(
            weights[n], grads[n], mom_m[n], mom_v[n], slabs=n in axes or n in _REPL, name=f"adamw_{n}")

    grad_x = dx0[None]
    return (loss, grad_x, *[grad_out[n] for n in order], *[deltas[n] for n in order],
            *[new_m[n] for n in order], *[new_v[n] for n in order])
```

```python
import jax
import jax.numpy as jnp
from jax import lax
from jax.experimental import pallas as pl
from jax.experimental.pallas import tpu as pltpu

F32 = jnp.float32
BF16 = jnp.bfloat16
HIGHEST = lax.Precision.HIGHEST

N_DEV = 8
D_MODEL = 1024
N_HEADS = 16
HEAD_DIM = 64
MLA_ROPE_HALF = 16
MLA_Q_RANK = 384
MLA_KV_RANK = 256
MLA_ROPE_DIM = 32
NORM_EPS = 1e-6
ROPE_BASE = 10000.0
ADAM_LR = 0.001
ADAM_B1 = 0.9
ADAM_B2 = 0.999
ADAM_EPS = 1e-08
ADAM_WD = 0.01
ADAM_STEP = 10

LANES = 128
VMEM_LIMIT_BYTES = 56 * 1024 * 1024
ROW_BLOCK = 512
ATT_BLOCK = 512
CONV_ROWS = 1024
MM_BM, MM_BN, MM_BK = 512, 1024, 2048
MM_K_WHOLE = 3328
LOG2E = 1.4426950408889634


def _params(*sem):
    return pltpu.CompilerParams(dimension_semantics=sem or None, vmem_limit_bytes=VMEM_LIMIT_BYTES)


def _blk(dim, pref):
    if dim <= pref:
        return dim
    b = pref - pref % LANES
    while b >= LANES:
        if dim % b == 0:
            return b
        b -= LANES
    raise ValueError(f"no block for {dim}")


def _row_blk(rows, pref):
    if rows <= pref:
        return rows
    for b in range(pref - pref % 8, 7, -8):
        if rows % b == 0:
            return b
    return rows


def _pad_axis(a, axis, mult):
    pad = (-a.shape[axis]) % mult
    if pad == 0:
        return a
    widths = [(0, 0)] * a.ndim
    widths[axis] = (0, pad)
    return jnp.pad(a, widths)


def _matmul(a, b, *, ta=False, tb=False, out_dtype=F32, col_scale=None, name):
    m, k = (a.shape[1], a.shape[0]) if ta else a.shape
    n = b.shape[0] if tb else b.shape[1]
    assert (b.shape[1] if tb else b.shape[0]) == k, (a.shape, b.shape, ta, tb)
    bm, bn = _blk(m, MM_BM if ta else 2 * MM_BM), _blk(n, MM_BN)
    bk = k if k <= MM_K_WHOLE else _blk(k, MM_BK)
    nk = k // bk
    dims = (((0 if ta else 1,), (1 if tb else 0,)), ((), ()))
    has_scale = col_scale is not None
    use_acc = nk > 1 and (out_dtype != F32 or has_scale)

    def body(*refs):
        a_ref, b_ref = refs[0], refs[1]
        s_ref = refs[2] if has_scale else None
        o_ref = refs[3] if has_scale else refs[2]
        acc_ref = refs[-1] if use_acc else o_ref
        kk = pl.program_id(2)
        part = lax.dot_general(a_ref[...].astype(BF16), b_ref[...].astype(BF16), dims, preferred_element_type=F32)

        def finish(val):
            if has_scale:
                val = val * s_ref[...]
            o_ref[...] = val.astype(out_dtype)

        if nk == 1:
            finish(part)
            return

        @pl.when(kk == 0)
        def _():
            acc_ref[...] = part

        @pl.when(kk > 0)
        def _():
            acc_ref[...] += part

        if use_acc:
            @pl.when(kk == nk - 1)
            def _():
                finish(acc_ref[...])

    a_spec = pl.BlockSpec((bk, bm), lambda i, j, kk: (kk, i)) if ta else pl.BlockSpec((bm, bk), lambda i, j, kk: (i, kk))
    b_spec = pl.BlockSpec((bn, bk), lambda i, j, kk: (j, kk)) if tb else pl.BlockSpec((bk, bn), lambda i, j, kk: (kk, j))
    return pl.pallas_call(
        body, name=name, grid=(m // bm, n // bn, nk),
        in_specs=[a_spec, b_spec] + ([pl.BlockSpec((1, bn), lambda i, j, kk: (0, j))] if has_scale else []),
        out_specs=pl.BlockSpec((bm, bn), lambda i, j, kk: (i, j)),
        out_shape=jax.ShapeDtypeStruct((m, n), out_dtype),
        scratch_shapes=[pltpu.VMEM((bm, bn), F32)] if use_acc else [],
        compiler_params=_params("parallel", "parallel", "arbitrary"),
    )(*([a, b] + ([col_scale] if has_scale else [])))


def _norm_fwd(x, mul, add, *, plus_one, out_dtype, name):
    s, n = x.shape
    bs = _blk(s, ROW_BLOCK)

    def body(x_ref, m_ref, a_ref, o_ref):
        xv = x_ref[...]
        r = lax.rsqrt(jnp.mean(xv * xv, axis=-1, keepdims=True) + NORM_EPS)
        mv = m_ref[...] + 1.0 if plus_one else m_ref[...]
        o_ref[...] = (xv * r * mv + a_ref[...]).astype(out_dtype)

    row = pl.BlockSpec((bs, n), lambda i: (i, 0))
    vec = pl.BlockSpec((1, n), lambda i: (0, 0))
    return pl.pallas_call(
        body, name=name, grid=(s // bs,), in_specs=[row, vec, vec], out_specs=row,
        out_shape=jax.ShapeDtypeStruct((s, n), out_dtype), compiler_params=_params("parallel"),
    )(x, mul, add)


def _norm_bwd(x, mul, dy, dres, *, plus_one, name):
    s, n = x.shape
    bs = _blk(s, ROW_BLOCK)
    has_res = dres is not None

    def body(*refs):
        if has_res:
            x_ref, m_ref, dy_ref, dres_ref, dx_ref, dm_ref, da_ref = refs
        else:
            x_ref, m_ref, dy_ref, dx_ref, dm_ref, da_ref = refs
        xv = x_ref[...]
        dyv = dy_ref[...].astype(F32)
        r = lax.rsqrt(jnp.mean(xv * xv, axis=-1, keepdims=True) + NORM_EPS)
        xn = xv * r
        mv = m_ref[...] + 1.0 if plus_one else m_ref[...]
        g = dyv * mv
        dx = r * (g - xn * jnp.mean(g * xn, axis=-1, keepdims=True))
        if has_res:
            dx = dx + dres_ref[...]
        dx_ref[...] = dx

        @pl.when(pl.program_id(0) == 0)
        def _():
            dm_ref[...] = jnp.zeros_like(dm_ref)
            da_ref[...] = jnp.zeros_like(da_ref)

        dm_ref[...] += jnp.sum(dyv * xn, axis=0, keepdims=True)
        da_ref[...] += jnp.sum(dyv, axis=0, keepdims=True)

    row = pl.BlockSpec((bs, n), lambda i: (i, 0))
    vec = pl.BlockSpec((1, n), lambda i: (0, 0))
    ins = [x, mul, dy] + ([dres] if has_res else [])
    return pl.pallas_call(
        body, name=name, grid=(s // bs,),
        in_specs=[row, vec, row] + ([row] if has_res else []), out_specs=[row, vec, vec],
        out_shape=[jax.ShapeDtypeStruct((s, n), F32), jax.ShapeDtypeStruct((1, n), F32), jax.ShapeDtypeStruct((1, n), F32)],
        compiler_params=_params("arbitrary"),
    )(*ins)


def _resid_fwd(x, y, gate, *, name):
    s, n = x.shape
    bs = _blk(s, ROW_BLOCK)

    def body(x_ref, y_ref, g_ref, o_ref):
        o_ref[...] = x_ref[...] + g_ref[...] * y_ref[...]

    row = pl.BlockSpec((bs, n), lambda i: (i, 0))
    vec = pl.BlockSpec((1, n), lambda i: (0, 0))
    return pl.pallas_call(
        body, name=name, grid=(s // bs,), in_specs=[row, row, vec], out_specs=row,
        out_shape=jax.ShapeDtypeStruct((s, n), F32), compiler_params=_params("parallel"),
    )(x, y, gate)


def _resid_bwd(dx, y, gate, *, name):
    s, n = dx.shape
    bs = _blk(s, ROW_BLOCK)

    def body(dx_ref, y_ref, g_ref, dy_ref, dg_ref):
        dxv = dx_ref[...]
        dy_ref[...] = (g_ref[...] * dxv).astype(BF16)

        @pl.when(pl.program_id(0) == 0)
        def _():
            dg_ref[...] = jnp.zeros_like(dg_ref)

        dg_ref[...] += jnp.sum(dxv * y_ref[...], axis=0, keepdims=True)

    row = pl.BlockSpec((bs, n), lambda i: (i, 0))
    vec = pl.BlockSpec((1, n), lambda i: (0, 0))
    return pl.pallas_call(
        body, name=name, grid=(s // bs,), in_specs=[row, row, vec], out_specs=[row, vec],
        out_shape=[jax.ShapeDtypeStruct((s, n), BF16), jax.ShapeDtypeStruct((1, n), F32)],
        compiler_params=_params("arbitrary"),
    )(dx, y, gate)


def _final_loss(x, g, target, *, name):
    s, n = x.shape
    bs = _blk(s, ROW_BLOCK)

    def body(x_ref, g_ref, t_ref, loss_ref, dx_ref, dg_ref):
        xv = x_ref[...]
        r = lax.rsqrt(jnp.mean(xv * xv, axis=-1, keepdims=True) + NORM_EPS)
        xn = xv * r
        gv = g_ref[...]
        err = xn * gv - t_ref[...]
        dout = err * (1.0 / n)
        gg = dout * gv
        dx_ref[...] = r * (gg - xn * jnp.mean(gg * xn, axis=-1, keepdims=True))

        @pl.when(pl.program_id(0) == 0)
        def _():
            loss_ref[...] = jnp.zeros_like(loss_ref)
            dg_ref[...] = jnp.zeros_like(dg_ref)

        part = jnp.sum(jnp.sum(err * err, axis=-1, keepdims=True), axis=0, keepdims=True) * (0.5 / n)
        loss_ref[...] += jnp.broadcast_to(part, loss_ref.shape)
        dg_ref[...] += jnp.sum(dout * xn, axis=0, keepdims=True)

    row = pl.BlockSpec((bs, n), lambda i: (i, 0))
    vec = pl.BlockSpec((1, n), lambda i: (0, 0))
    return pl.pallas_call(
        body, name=name, grid=(s // bs,), in_specs=[row, vec, row],
        out_specs=[pl.BlockSpec((1, LANES), lambda i: (0, 0)), row, vec],
        out_shape=[jax.ShapeDtypeStruct((1, LANES), F32), jax.ShapeDtypeStruct((s, n), F32), jax.ShapeDtypeStruct((1, n), F32)],
        compiler_params=_params("arbitrary"),
    )(x, g, target)


def _lane_lt64(shape):
    return lax.broadcasted_iota(jnp.int32, shape, 1) < HEAD_DIM


def _keep_low(x):
    return jnp.where(_lane_lt64(x.shape), x.astype(F32), 0.0).astype(x.dtype)


def _keep_high(x):
    return jnp.where(_lane_lt64(x.shape), 0.0, x.astype(F32)).astype(x.dtype)


def _lane_merge(a, b):
    n = max(a.shape[0], b.shape[0])
    return jnp.where(_lane_lt64((n, LANES)), a, b)


def _pair(x, width, masked):
    if width == HEAD_DIM:
        return (_keep_low(x), _keep_high(x)) if masked else (x, x)
    return x[:, :LANES], x[:, LANES:]


def _qk_t(a, b):
    return lax.dot_general(a, b, (((1,), (1,)), ((), ())), preferred_element_type=F32)


def _attn_specs(s, blk, width, cols, resident):
    w = 2 * width
    if resident:
        return pl.BlockSpec((s, w), lambda p, i: (0, cols + p))
    return pl.BlockSpec((blk, w), lambda p, i: (i, cols + p))


BIAS_TERMS = 3


def _attn_fwd(q, k, v, kbl, *, qc, kc, vc, width, name):
    s = q.shape[0]
    blk = _blk(s, ATT_BLOCK)
    nb = s // blk
    has_bias = kbl is not None
    assert has_bias == (width == HEAD_DIM)

    def body(*refs):
        if has_bias:
            q_ref, k_ref, v_ref, kbl_ref, o_ref, lse_ref = refs
        else:
            q_ref, k_ref, v_ref, o_ref, lse_ref = refs
        i = pl.program_id(1)
        q2 = q_ref[...]
        if has_bias:
            lane = lax.broadcasted_iota(jnp.int32, (blk, LANES), 1)
            qf = q2.astype(F32)
            qh = (jnp.where(lane < HEAD_DIM, qf, jnp.where(lane < HEAD_DIM + BIAS_TERMS, 1.0, 0.0)).astype(BF16),
                  jnp.where(lane >= HEAD_DIM, qf, jnp.where(lane < BIAS_TERMS, 1.0, 0.0)).astype(BF16))
        else:
            qh = (q2[:, :LANES], q2[:, LANES:])

        def step(j, carry, nblk, diag):
            rows = pl.ds(pl.multiple_of(j * blk, blk), nblk * blk)
            ones = jnp.ones((16, nblk * blk), BF16)
            k2 = k_ref[rows, :]
            if has_bias:
                low = _lane_lt64(k2.shape)
                kf, bf = k2.astype(F32), kbl_ref[rows, :].astype(F32)
                kh = (jnp.where(low, kf, bf).astype(BF16), jnp.where(low, bf, kf).astype(BF16))
            else:
                kh = (k2[:, :LANES], k2[:, LANES:])
            vt = v_ref[rows, :].astype(F32).T.astype(BF16)
            out = []
            for hd in range(2):
                m, l, acc = carry[hd]
                st = _qk_t(kh[hd], qh[hd])
                if diag:
                    row = lax.broadcasted_iota(jnp.int32, (blk, blk), 0)
                    colq = lax.broadcasted_iota(jnp.int32, (blk, blk), 1)
                    st = jnp.where(row <= colq, st, -1e30)
                m_new = jnp.maximum(m, jnp.max(st, axis=0, keepdims=True))
                alpha = jnp.exp2(m - m_new)
                pt = jnp.exp2(st - m_new).astype(BF16)
                l = alpha * l + jnp.dot(ones, pt, preferred_element_type=F32)
                acc = alpha * acc + jnp.dot(vt, pt, preferred_element_type=F32)
                out.append((m_new, l, acc))
            return tuple(out)

        one = (jnp.full((1, blk), -1e30, F32), jnp.zeros((16, blk), F32), jnp.zeros((LANES, blk), F32))
        carry = lax.fori_loop(0, i // 2, lambda j, c: step(2 * j, c, 2, False), (one, one))
        carry = lax.fori_loop(0, i % 2, lambda _, c: step(i - 1, c, 1, False), carry)
        (ma, la, acca), (mb, lb, accb) = step(i, carry, 1, True)
        la = jnp.max(la, axis=0, keepdims=True)
        lb = jnp.max(lb, axis=0, keepdims=True)
        low = lax.broadcasted_iota(jnp.int32, (LANES, blk), 0) < HEAD_DIM
        o_ref[...] = jnp.where(low, acca / la, accb / lb).T
        lse_ref[0, 0] = ma + jnp.log(la) * LOG2E
        lse_ref[1, 0] = mb + jnp.log(lb) * LOG2E

    ins = [q, k, v] + ([kbl] if has_bias else [])
    return pl.pallas_call(
        body, name=name, grid=(N_HEADS // 2, nb),
        in_specs=[_attn_specs(s, blk, width, qc, False), _attn_specs(s, blk, width, kc, True),
                  _attn_specs(s, blk, HEAD_DIM, vc, True)]
                 + ([_attn_specs(s, blk, HEAD_DIM, 0, True)] if has_bias else []),
        out_specs=[pl.BlockSpec((blk, LANES), lambda p, i: (i, p)), pl.BlockSpec((2, 1, 1, blk), lambda p, i: (p, i, 0, 0))],
        out_shape=[jax.ShapeDtypeStruct((s, N_HEADS * HEAD_DIM), F32), jax.ShapeDtypeStruct((N_HEADS, nb, 1, blk), F32)],
        compiler_params=_params("parallel", "parallel"),
    )(*ins)


def _bias_lane_terms(kb2):
    s = kb2.shape[0]
    terms, rest = [], kb2
    for _ in range(BIAS_TERMS):
        t = lax.reduce_precision(rest, 8, 7)
        terms.append(t.astype(BF16))
        rest = rest - t
    t3 = jnp.stack(terms, axis=-1).reshape(s, N_HEADS // 2, 2, BIAS_TERMS)
    pad = jnp.zeros((s, N_HEADS // 2, HEAD_DIM - BIAS_TERMS), BF16)
    return jnp.concatenate([t3[:, :, 1], pad, t3[:, :, 0], pad], axis=-1).reshape(s, N_HEADS * HEAD_DIM)


def _causal_keep(n):
    row = lax.broadcasted_iota(jnp.int32, (n, n), 0)
    col = lax.broadcasted_iota(jnp.int32, (n, n), 1)
    return col <= row


def _attn_delta(o, do, *, name):
    s, n = o.shape
    bs = _blk(s, ROW_BLOCK)

    def body(o_ref, do_ref, d_ref):
        for g in range(n // LANES):
            prod = do_ref[:, g * LANES:(g + 1) * LANES].astype(F32) * o_ref[:, g * LANES:(g + 1) * LANES]
            low = _lane_lt64(prod.shape)
            d_ref[:, g * LANES:(g + 1) * LANES] = _lane_merge(
                jnp.sum(jnp.where(low, prod, 0.0), axis=-1, keepdims=True),
                jnp.sum(jnp.where(low, 0.0, prod), axis=-1, keepdims=True))

    row = pl.BlockSpec((bs, n), lambda i: (i, 0))
    return pl.pallas_call(
        body, name=name, grid=(s // bs,), in_specs=[row, row], out_specs=row,
        out_shape=jax.ShapeDtypeStruct((s, n), F32), compiler_params=_params("parallel"),
    )(o, do)


def _attn_bwd(q, k, v, kb_col, do, lse_row, delta_row, *, qc, kc, vc, width, dq_mult, dk_mult, out_dtype, name):
    s = q.shape[0]
    blk = _blk(s, ATT_BLOCK)
    nb = s // blk
    has_bias = kb_col is not None

    def body(*refs):
        if has_bias:
            q_ref, k_ref, v_ref, kb_ref, do_ref, lse_ref, dl_ref, dk_ref, dv_ref, db_ref, dq_ref, dr_ref = refs
        else:
            q_ref, k_ref, v_ref, do_ref, lse_ref, dl_ref, dk_ref, dv_ref, db_ref, dq_ref = refs
        j = pl.program_id(1)

        @pl.when(j == 0)
        def _():
            dq_ref[...] = jnp.zeros_like(dq_ref)
            if has_bias:
                dr_ref[...] = jnp.zeros_like(dr_ref)

        kh = _pair(k_ref[...], width, True)
        v2 = v_ref[...]
        vh = (_keep_low(v2), _keep_high(v2))
        if has_bias:
            kb2 = kb_ref[0]
            kbh = (kb2[:, 0:1], kb2[:, 1:2])

        def step(i, carry, diag):
            start = pl.multiple_of(i * blk, blk)
            qh = _pair(q_ref[pl.ds(start, blk), :], width, False)
            doi = do_ref[pl.ds(start, blk), :]
            out, dq_parts = [], []
            for hd in range(2):
                dk, dvv, db = carry[hd]
                st = _qk_t(kh[hd], qh[hd])
                if has_bias:
                    st = st + kbh[hd]
                if diag:
                    row = lax.broadcasted_iota(jnp.int32, (blk, blk), 0)
                    colq = lax.broadcasted_iota(jnp.int32, (blk, blk), 1)
                    st = jnp.where(row <= colq, st, -1e30)
                pt = jnp.exp2(st - lse_ref[hd, i])
                dvv = dvv + jnp.dot(pt.astype(BF16), doi, preferred_element_type=F32)
                dst = pt * (_qk_t(vh[hd], doi) - dl_ref[hd, i])
                dsb = dst.astype(BF16)
                dk = dk + jnp.dot(dsb, qh[hd], preferred_element_type=F32)
                db = db + jnp.sum(dst, axis=-1, keepdims=True)
                dq_parts.append(lax.dot_general(dsb, kh[hd], (((0,), (0,)), ((), ())), preferred_element_type=F32))
                if has_bias:
                    dr_ref[hd, i] += jnp.sum(dst, axis=0, keepdims=True)
                out.append((dk, dvv, db))
            rows = pl.ds(start, blk)
            if width == HEAD_DIM:
                dq_ref[rows, :] += (dq_parts[0] + dq_parts[1]) * dq_mult
            else:
                dq_ref[rows, 0:LANES] += dq_parts[0] * dq_mult
                dq_ref[rows, LANES:2 * LANES] += dq_parts[1] * dq_mult
            return tuple(out)

        one = (jnp.zeros((blk, LANES), F32), jnp.zeros((blk, LANES), F32), jnp.zeros((blk, 1), F32))
        carry = step(j, (one, one), True)
        (dka, dva, dba), (dkb, dvb, dbb) = lax.fori_loop(j + 1, nb, lambda i, c: step(i, c, False), carry)
        if width == HEAD_DIM:
            dk = _lane_merge(dka, dkb)
        else:
            dk = jnp.concatenate([dka, dkb], axis=1)
        dk_ref[...] = (dk * dk_mult).astype(out_dtype)
        dv_ref[...] = _lane_merge(dva, dvb).astype(out_dtype)
        db_ref[...] = _lane_merge(dba, dbb)

    stat = pl.BlockSpec((blk, LANES), lambda p, jj: (jj, p))
    rows = pl.BlockSpec((2, nb, 1, blk), lambda p, jj: (p, 0, 0, 0))
    ins = [q, k, v] + ([kb_col] if has_bias else []) + [do, lse_row, delta_row]
    return pl.pallas_call(
        body, name=name, grid=(N_HEADS // 2, nb),
        in_specs=[_attn_specs(s, blk, width, qc, True), _attn_specs(s, blk, width, kc, False),
                  _attn_specs(s, blk, HEAD_DIM, vc, False)]
                 + ([pl.BlockSpec((1, blk, 2), lambda p, jj: (p, jj, 0))] if has_bias else [])
                 + [pl.BlockSpec((s, LANES), lambda p, jj: (0, p)), rows, rows],
        out_specs=[pl.BlockSpec((blk, 2 * width), lambda p, jj: (jj, p)), stat, stat,
                   pl.BlockSpec((s, 2 * width), lambda p, jj: (0, p))] + ([rows] if has_bias else []),
        out_shape=[jax.ShapeDtypeStruct((s, N_HEADS * width), out_dtype), jax.ShapeDtypeStruct((s, N_HEADS * HEAD_DIM), out_dtype),
                   jax.ShapeDtypeStruct((s, N_HEADS * HEAD_DIM), F32), jax.ShapeDtypeStruct((s, N_HEADS * width), F32)]
                  + ([jax.ShapeDtypeStruct((N_HEADS, nb, 1, blk), F32)] if has_bias else []),
        compiler_params=_params("parallel", "arbitrary"),
    )(*ins)


def _head_stat(t):
    return t[:, ::HEAD_DIM]


def _stat_rows(t16, blk):
    s = t16.shape[0]
    return t16.T.reshape(N_HEADS, s // blk, 1, blk)


def _attention_bwd(res, do, *, qc, kc, vc, width, dq_mult, dk_mult, out_dtype, name):
    q, k, v, bias, o, lse_row = res
    s = q.shape[0]
    blk = _blk(s, ATT_BLOCK)
    kb_col = None if bias is None else bias.reshape(s, N_HEADS // 2, 2).transpose(1, 0, 2)
    delta_row = _stat_rows(_head_stat(_attn_delta(o, do, name=name + "_delta")), blk)
    outs = _attn_bwd(q, k, v, kb_col, do, lse_row, delta_row, qc=qc, kc=kc, vc=vc, width=width, dq_mult=dq_mult,
                     dk_mult=dk_mult, out_dtype=out_dtype, name=name + "_bwd")
    dk, dv, dcol, dq = outs[:4]
    if bias is None:
        return dq, dk, dv, None
    return dq, dk, dv, outs[4].reshape(N_HEADS, s).T - _head_stat(dcol)


def _fox_gate_fwd(fl, bf, *, name):
    s, n = fl.shape
    bs = _blk(s, ROW_BLOCK)

    def body(fl_ref, bf_ref, cum_ref, carry_ref):
        @pl.when(pl.program_id(0) == 0)
        def _():
            carry_ref[...] = jnp.zeros_like(carry_ref)

        z = fl_ref[...] + bf_ref[...]
        lf = jnp.minimum(z, 0.0) - jnp.log1p(jnp.exp(-jnp.abs(z)))
        row = lax.broadcasted_iota(jnp.int32, (bs, bs), 0)
        col = lax.broadcasted_iota(jnp.int32, (bs, bs), 1)
        tri = (col <= row).astype(F32)
        cum_ref[...] = jnp.dot(tri, lf, preferred_element_type=F32, precision=HIGHEST) + carry_ref[...]
        carry_ref[...] += jnp.sum(lf, axis=0, keepdims=True)

    return pl.pallas_call(
        body, name=name, grid=(s // bs,),
        in_specs=[pl.BlockSpec((bs, n), lambda i: (i, 0)), pl.BlockSpec((1, n), lambda i: (0, 0))],
        out_specs=pl.BlockSpec((bs, n), lambda i: (i, 0)),
        out_shape=jax.ShapeDtypeStruct((s, n), F32), scratch_shapes=[pltpu.VMEM((1, n), F32)],
        compiler_params=_params("arbitrary"),
    )(fl, bf)


def _fox_gate_bwd(fl, bf, dcum, *, name):
    s, n = fl.shape
    bs = _blk(s, ROW_BLOCK)
    nb = s // bs

    def body(fl_ref, bf_ref, dc_ref, dz_ref, dbf_ref, carry_ref):
        @pl.when(pl.program_id(0) == 0)
        def _():
            carry_ref[...] = jnp.zeros_like(carry_ref)
            dbf_ref[...] = jnp.zeros_like(dbf_ref)

        dc = dc_ref[...]
        row = lax.broadcasted_iota(jnp.int32, (bs, bs), 0)
        col = lax.broadcasted_iota(jnp.int32, (bs, bs), 1)
        tri = (col >= row).astype(F32)
        dlf = jnp.dot(tri, dc, preferred_element_type=F32, precision=HIGHEST) + carry_ref[...]
        carry_ref[...] += jnp.sum(dc, axis=0, keepdims=True)
        z = fl_ref[...] + bf_ref[...]
        dz = dlf / (1.0 + jnp.exp(z))
        dz_ref[...] = dz
        dbf_ref[...] += jnp.sum(dz, axis=0, keepdims=True)

    rev = pl.BlockSpec((bs, n), lambda i: (nb - 1 - i, 0))
    vec = pl.BlockSpec((1, n), lambda i: (0, 0))
    return pl.pallas_call(
        body, name=name, grid=(nb,), in_specs=[rev, vec, rev], out_specs=[rev, vec],
        out_shape=[jax.ShapeDtypeStruct((s, n), F32), jax.ShapeDtypeStruct((1, n), F32)],
        scratch_shapes=[pltpu.VMEM((1, n), F32)], compiler_params=_params("arbitrary"),
    )(fl, bf, dcum)


def _rope(x1, x2, cos, sin, *, negate, name):
    s, n = x1.shape
    bs = _blk(s, ROW_BLOCK)

    def body(a_ref, b_ref, c_ref, s_ref, o1_ref, o2_ref):
        a, b, cv = a_ref[...], b_ref[...], c_ref[...]
        sv = -s_ref[...] if negate else s_ref[...]
        o1_ref[...] = a * cv - b * sv
        o2_ref[...] = b * cv + a * sv

    row = pl.BlockSpec((bs, n), lambda i: (i, 0))
    return pl.pallas_call(
        body, name=name, grid=(s // bs,), in_specs=[row] * 4, out_specs=[row, row],
        out_shape=[jax.ShapeDtypeStruct((s, n), F32)] * 2, compiler_params=_params("parallel"),
    )(x1, x2, cos, sin)


def _rope_heads(x, ta, tb, tc, *, out_dtype, name):
    s, n = x.shape
    bs = _blk(s, ROW_BLOCK)

    def body(x_ref, a_ref, b_ref, c_ref, o_ref):
        av, bv, cv = a_ref[...], b_ref[...], c_ref[...]
        for g in range(n // LANES):
            xg = x_ref[:, g * LANES:(g + 1) * LANES]
            og = xg * av + pltpu.roll(xg, LANES - MLA_ROPE_HALF, 1) * bv + pltpu.roll(xg, MLA_ROPE_HALF, 1) * cv
            o_ref[:, g * LANES:(g + 1) * LANES] = og.astype(out_dtype)

    row = pl.BlockSpec((bs, n), lambda i: (i, 0))
    tab = pl.BlockSpec((bs, LANES), lambda i: (i, 0))
    return pl.pallas_call(
        body, name=name, grid=(s // bs,), in_specs=[row, tab, tab, tab], out_specs=row,
        out_shape=jax.ShapeDtypeStruct((s, n), out_dtype), compiler_params=_params("parallel"),
    )(x, ta, tb, tc)


def _group_sum(x, *, name):
    s, n = x.shape
    bs = _blk(s, ROW_BLOCK)

    def body(x_ref, o_ref):
        acc = x_ref[:, 0:LANES]
        for g in range(1, n // LANES):
            acc = acc + x_ref[:, g * LANES:(g + 1) * LANES]
        o_ref[...] = acc

    return pl.pallas_call(
        body, name=name, grid=(s // bs,), in_specs=[pl.BlockSpec((bs, n), lambda i: (i, 0))],
        out_specs=pl.BlockSpec((bs, LANES), lambda i: (i, 0)),
        out_shape=jax.ShapeDtypeStruct((s, LANES), F32), compiler_params=_params("parallel"),
    )(x)


def _shift_down(x, k):
    return pltpu.roll(x, k, 0)


def _conv_rows(ext, w_ref, b_ref, rows):
    y = b_ref[...] + w_ref[0:1, :] * _shift_down(ext, 2) + w_ref[1:2, :] * _shift_down(ext, 1) + w_ref[2:3, :] * ext
    return y[8:8 + rows]


def _conv_gate_fwd(u, cw, cb, *, name):
    s, f2 = u.shape
    f = f2 // 2
    nf = f // LANES
    r = _blk(s, CONV_ROWS)
    r8 = r // 8

    def body(ug_ref, ugp_ref, uv_ref, uvp_ref, wg_ref, wv_ref, bg_ref, bv_ref, o_ref):
        first = pl.program_id(1) == 0

        def conv(cur_ref, prev_ref, w_ref, b_ref):
            prev = jnp.where(first, 0.0, prev_ref[...])
            return _conv_rows(jnp.concatenate([prev, cur_ref[...]], axis=0), w_ref, b_ref, r)

        yg = conv(ug_ref, ugp_ref, wg_ref, bg_ref)
        yv = conv(uv_ref, uvp_ref, wv_ref, bv_ref)
        o_ref[...] = (yg * jax.nn.sigmoid(yg) * yv).astype(BF16)

    def cur(off):
        return pl.BlockSpec((r, LANES), lambda c, i: (i, c + off))

    def prev(off):
        return pl.BlockSpec((8, LANES), lambda c, i: (jnp.maximum(i * r8 - 1, 0), c + off))

    def wspec(rows, off):
        return pl.BlockSpec((rows, LANES), lambda c, i: (0, c + off))

    return pl.pallas_call(
        body, name=name, grid=(nf, s // r),
        in_specs=[cur(0), prev(0), cur(nf), prev(nf), wspec(3, 0), wspec(3, nf), wspec(1, 0), wspec(1, nf)],
        out_specs=pl.BlockSpec((r, LANES), lambda c, i: (i, c)),
        out_shape=jax.ShapeDtypeStruct((s, f), BF16), compiler_params=_params("parallel", "parallel"),
    )(u, u, u, u, cw, cw, cb, cb)


def _conv_gate_bwd(u, cw, cb, dg, *, name):
    s, f2 = u.shape
    f = f2 // 2
    nf = f // LANES
    r = _blk(s, CONV_ROWS)
    r8 = r // 8
    nr = s // r

    def body(ug_ref, ugp_ref, ugn_ref, uv_ref, uvp_ref, uvn_ref, wg_ref, wv_ref, bg_ref, bv_ref, dg_ref, dgn_ref,
             dug_ref, duv_ref, dwg_ref, dwv_ref, dbg_ref, dbv_ref):
        i = pl.program_id(1)
        first, last = i == 0, i == nr - 1

        def ext_of(cur_ref, prev_ref, next_ref):
            prev = jnp.where(first, 0.0, prev_ref[...])
            return jnp.concatenate([prev, cur_ref[...], next_ref[...]], axis=0)

        eg, ev = ext_of(ug_ref, ugp_ref, ugn_ref), ext_of(uv_ref, uvp_ref, uvn_ref)
        yg = _conv_rows(eg, wg_ref, bg_ref, r + 8)
        yv = _conv_rows(ev, wv_ref, bv_ref, r + 8)
        dgn = jnp.where(last, 0.0, dgn_ref[...])
        dgx = jnp.concatenate([dg_ref[...], dgn], axis=0)
        sg = jax.nn.sigmoid(yg)
        dyg = dgx * yv * (sg * (1.0 + yg * (1.0 - sg)))
        dyv = dgx * (yg * sg)

        @pl.when(i == 0)
        def _():
            for ref in (dwg_ref, dwv_ref, dbg_ref, dbv_ref):
                ref[...] = jnp.zeros_like(ref)

        def grads(dy, ext, w_ref, du_ref, dw_ref, db_ref):
            n = r + 8
            du = w_ref[2:3, :] * dy + w_ref[1:2, :] * pltpu.roll(dy, n - 1, 0) + w_ref[0:1, :] * pltpu.roll(dy, n - 2, 0)
            du_ref[...] = du[0:r].astype(BF16)
            dyc = dy[0:r]
            db_ref[...] += jnp.sum(dyc, axis=0, keepdims=True)
            ext_c = ext[0:r + 8]
            dw_ref[0:1, :] += jnp.sum(dyc * _shift_down(ext_c, 2)[8:], axis=0, keepdims=True)
            dw_ref[1:2, :] += jnp.sum(dyc * _shift_down(ext_c, 1)[8:], axis=0, keepdims=True)
            dw_ref[2:3, :] += jnp.sum(dyc * ext_c[8:], axis=0, keepdims=True)

        grads(dyg, eg, wg_ref, dug_ref, dwg_ref, dbg_ref)
        grads(dyv, ev, wv_ref, duv_ref, dwv_ref, dbv_ref)

    def cur(off):
        return pl.BlockSpec((r, LANES), lambda c, i: (i, c + off))

    def prev(off):
        return pl.BlockSpec((8, LANES), lambda c, i: (jnp.maximum(i * r8 - 1, 0), c + off))

    def nxt(off):
        return pl.BlockSpec((8, LANES), lambda c, i: (jnp.minimum((i + 1) * r8, s // 8 - 1), c + off))

    def wspec(rows, off):
        return pl.BlockSpec((rows, LANES), lambda c, i: (0, c + off))

    outs = pl.pallas_call(
        body, name=name, grid=(nf, nr),
        in_specs=[cur(0), prev(0), nxt(0), cur(nf), prev(nf), nxt(nf), wspec(3, 0), wspec(3, nf), wspec(1, 0), wspec(1, nf),
                  cur(0), nxt(0)],
        out_specs=[cur(0), cur(0), wspec(3, 0), wspec(3, 0), wspec(1, 0), wspec(1, 0)],
        out_shape=[jax.ShapeDtypeStruct((s, f), BF16), jax.ShapeDtypeStruct((s, f), BF16),
                   jax.ShapeDtypeStruct((3, f), F32), jax.ShapeDtypeStruct((3, f), F32),
                   jax.ShapeDtypeStruct((1, f), F32), jax.ShapeDtypeStruct((1, f), F32)],
        compiler_params=_params("parallel", "arbitrary"),
    )(u, u, u, u, u, u, cw, cw, cb, cb, dg, dg)
    dug, duv, dwg, dwv, dbg, dbv = outs
    return jnp.concatenate([dug, duv], axis=1), jnp.concatenate([dwg, dwv], axis=1), jnp.concatenate([dbg, dbv], axis=1)


def _adamw(w, g, m, v, *, slabs, name):
    shape = w.shape
    cols = shape[-1]
    rows = w.size // cols
    w2, m2, v2 = (t.reshape(rows, cols) for t in (w, m, v))
    g2 = g.reshape((N_DEV, rows, cols) if slabs else (rows, cols))
    br = _row_blk(rows, ROW_BLOCK // 2 if slabs else ROW_BLOCK)

    def body(w_ref, g_ref, m_ref, v_ref, go_ref, d_ref, nm_ref, nv_ref):
        if slabs:
            gv = g_ref[0].astype(F32)
            for p in range(1, N_DEV):
                gv = gv + g_ref[p].astype(F32)
        else:
            gv = g_ref[...]
        nm = ADAM_B1 * m_ref[...] + (1.0 - ADAM_B1) * gv
        nv = ADAM_B2 * v_ref[...] + (1.0 - ADAM_B2) * (gv * gv)
        m_hat = nm / (1.0 - ADAM_B1 ** ADAM_STEP)
        v_hat = nv / (1.0 - ADAM_B2 ** ADAM_STEP)
        go_ref[...] = gv
        d_ref[...] = -ADAM_LR * (m_hat / (jnp.sqrt(v_hat) + ADAM_EPS) + ADAM_WD * w_ref[...])
        nm_ref[...] = nm
        nv_ref[...] = nv

    spec = pl.BlockSpec((br, cols), lambda i: (i, 0))
    gspec = pl.BlockSpec((N_DEV, br, cols), lambda i: (0, i, 0)) if slabs else spec
    outs = pl.pallas_call(
        body, name=name, grid=(rows // br,), in_specs=[spec, gspec, spec, spec], out_specs=[spec] * 4,
        out_shape=[jax.ShapeDtypeStruct((rows, cols), F32)] * 4, compiler_params=_params("parallel"),
    )(w2, g2, m2, v2)
    return tuple(t.reshape(shape) for t in outs)


def _exchange(xs, *, same_src, name):
    n = len(xs)
    slabs = [x.shape if same_src else x.shape[1:] for x in xs]

    def body(*refs):
        x_refs, o_refs = refs[:n], refs[n:2 * n]
        send_sems, recv_sems, loc_sems = refs[2 * n:]
        ix, iy, ic = lax.axis_index("x"), lax.axis_index("y"), lax.axis_index("c")
        me = 4 * ix + 2 * iy + ic
        local, sends, recvs = [], [], []
        for a in range(n):
            def src(p, a=a):
                return x_refs[a] if same_src else x_refs[a].at[p]

            local.append(pltpu.make_async_copy(src(me), o_refs[a].at[me], loc_sems.at[a]))
            for k in (1, 2, 4, 3, 5, 6, 7):
                px = 1 - ix if k & 4 else ix
                py = 1 - iy if k & 2 else iy
                pc = 1 - ic if k & 1 else ic
                p = 4 * px + 2 * py + pc
                for dst, out in ((me, sends), (p, recvs)):
                    out.append(pltpu.make_async_remote_copy(
                        src_ref=src(p), dst_ref=o_refs[a].at[dst], send_sem=send_sems.at[a, k - 1],
                        recv_sem=recv_sems.at[a, k - 1], device_id=(px, py, pc), device_id_type=pl.DeviceIdType.MESH))
        for cp in local + sends:
            cp.start()
        for cp in recvs:
            cp.wait_recv()
        for cp in sends:
            cp.wait_send()
        for cp in local:
            cp.wait()

    return pl.pallas_call(
        body, name=name,
        in_specs=[pl.BlockSpec(memory_space=pl.ANY)] * n, out_specs=[pl.BlockSpec(memory_space=pl.ANY)] * n,
        out_shape=[jax.ShapeDtypeStruct((N_DEV,) + tuple(sl), x.dtype) for sl, x in zip(slabs, xs)],
        scratch_shapes=[pltpu.SemaphoreType.DMA((n, N_DEV - 1)), pltpu.SemaphoreType.DMA((n, N_DEV - 1)),
                        pltpu.SemaphoreType.DMA((n,))],
        compiler_params=pltpu.CompilerParams(has_side_effects=True, vmem_limit_bytes=VMEM_LIMIT_BYTES),
    )(*xs)


def _sum_slabs(x, *, name):
    n, r, c = x.shape
    br = _row_blk(r, ROW_BLOCK)

    def body(x_ref, o_ref):
        acc = x_ref[0]
        for p in range(1, n):
            acc = acc + x_ref[p]
        o_ref[...] = acc

    return pl.pallas_call(
        body, name=name, grid=(r // br,), in_specs=[pl.BlockSpec((n, br, c), lambda i: (0, i, 0))],
        out_specs=pl.BlockSpec((br, c), lambda i: (i, 0)),
        out_shape=jax.ShapeDtypeStruct((r, c), F32), compiler_params=_params("parallel"),
    )(x)


def _silu(x, *, name):
    def body(x_ref, o_ref):
        xv = x_ref[...]
        o_ref[...] = (xv * jax.nn.sigmoid(xv)).astype(BF16)

    return pl.pallas_call(body, name=name, out_shape=jax.ShapeDtypeStruct(x.shape, BF16),
                          compiler_params=_params())(x)


_BIG = {"fox_w_in": 2, "fox_w_o": 1, "mla_w_a": 1, "mla_w_uq": 2, "mla_w_ukv": 2, "mla_w_o": 1, "ffn_w_in": 2, "ffn_w_out": 1}
_SMALL = {"mla_g_q": 1, "mla_g_kv": 1, "ffn_conv_w": 2}
_REPL = ("fox_b_f", "ffn_conv_b", "final_g")


def _gathered_to_full(g, axis):
    full = jnp.moveaxis(g, 0, axis)
    shape = list(full.shape)
    shape[axis:axis + 2] = [shape[axis] * shape[axis + 1]]
    return full.reshape(shape)


def _full_to_chunks(full, axis):
    shape = list(full.shape)
    shape[axis:axis + 1] = [N_DEV, shape[axis] // N_DEV]
    return jnp.moveaxis(full.reshape(shape), axis, 0)


def _per_head(parts, s_or_rows):
    return jnp.concatenate([p.reshape(s_or_rows, N_HEADS, -1) for p in parts], axis=-1).reshape(s_or_rows, -1)


def kernel(x, c, ada_w, ada_b, fox_w_in, fox_b_f, fox_w_o, mla_w_a, mla_g_q, mla_g_kv, mla_w_uq, mla_w_ukv, mla_w_o, ffn_w_in, ffn_conv_w, ffn_conv_b, ffn_w_out, final_g, loss_target, m_ada_w, m_ada_b, m_fox_w_in, m_fox_b_f, m_fox_w_o, m_mla_w_a, m_mla_g_q, m_mla_g_kv, m_mla_w_uq, m_mla_w_ukv, m_mla_w_o, m_ffn_w_in, m_ffn_conv_w, m_ffn_conv_b, m_ffn_w_out, m_final_g, v_ada_w, v_ada_b, v_fox_w_in, v_fox_b_f, v_fox_w_o, v_mla_w_a, v_mla_g_q, v_mla_g_kv, v_mla_w_uq, v_mla_w_ukv, v_mla_w_o, v_ffn_w_in, v_ffn_conv_w, v_ffn_conv_b, v_ffn_w_out, v_final_g):
    weights = dict(ada_w=ada_w, ada_b=ada_b, fox_w_in=fox_w_in, fox_b_f=fox_b_f, fox_w_o=fox_w_o, mla_w_a=mla_w_a,
                   mla_g_q=mla_g_q, mla_g_kv=mla_g_kv, mla_w_uq=mla_w_uq, mla_w_ukv=mla_w_ukv, mla_w_o=mla_w_o,
                   ffn_w_in=ffn_w_in, ffn_conv_w=ffn_conv_w, ffn_conv_b=ffn_conv_b, ffn_w_out=ffn_w_out, final_g=final_g)
    mom_m = dict(ada_w=m_ada_w, ada_b=m_ada_b, fox_w_in=m_fox_w_in, fox_b_f=m_fox_b_f, fox_w_o=m_fox_w_o, mla_w_a=m_mla_w_a,
                 mla_g_q=m_mla_g_q, mla_g_kv=m_mla_g_kv, mla_w_uq=m_mla_w_uq, mla_w_ukv=m_mla_w_ukv, mla_w_o=m_mla_w_o,
                 ffn_w_in=m_ffn_w_in, ffn_conv_w=m_ffn_conv_w, ffn_conv_b=m_ffn_conv_b, ffn_w_out=m_ffn_w_out, final_g=m_final_g)
    mom_v = dict(ada_w=v_ada_w, ada_b=v_ada_b, fox_w_in=v_fox_w_in, fox_b_f=v_fox_b_f, fox_w_o=v_fox_w_o, mla_w_a=v_mla_w_a,
                 mla_g_q=v_mla_g_q, mla_g_kv=v_mla_g_kv, mla_w_uq=v_mla_w_uq, mla_w_ukv=v_mla_w_ukv, mla_w_o=v_mla_w_o,
                 ffn_w_in=v_ffn_w_in, ffn_conv_w=v_ffn_conv_w, ffn_conv_b=v_ffn_conv_b, ffn_w_out=v_ffn_w_out, final_g=v_final_g)
    order = list(weights)
    x0 = x[0]
    target = loss_target[0]
    s = x0.shape[0]
    d = D_MODEL
    cols = ada_w.shape[-1]
    nq = N_HEADS * HEAD_DIM

    small_names = ["c"] + list(_SMALL)
    small_all = dict(zip(small_names, _exchange([c] + [weights[n] for n in _SMALL], same_src=True, name="gather_small")))
    c_all = small_all["c"].reshape(N_DEV, d)
    g_q = _gathered_to_full(small_all["mla_g_q"], 1)
    g_kv = _gathered_to_full(small_all["mla_g_kv"], 1)
    conv_w = _gathered_to_full(small_all["ffn_conv_w"], 2)

    c_pad = _pad_axis(c_all, 0, LANES)
    silu_c = _silu(c_pad, name="silu_c")
    w_ada = ada_w.reshape(4, d, cols)
    b_ada = ada_b.reshape(4, 1, cols)
    mods = [_matmul(silu_c, w_ada[i], name=f"ada_mod{i}")[:N_DEV] + b_ada[i] for i in range(4)]
    mod_send = _pad_axis(jnp.stack(mods, axis=1), 1, 8)
    mod_recv, = _exchange([mod_send], same_src=False, name="scatter_mod")
    mod = mod_recv[:, :4].transpose(1, 0, 2).reshape(4, 3 * d)
    shift = [mod[i:i + 1, 0:d] for i in range(4)]
    scale = [mod[i:i + 1, d:2 * d] for i in range(4)]
    gate = [mod[i:i + 1, 2 * d:3 * d] for i in range(4)]

    big_all = _exchange([weights[n].astype(BF16) for n in _BIG], same_src=True, name="gather_weights")
    wfull = {n: _gathered_to_full(g, _BIG[n]) for n, g in zip(_BIG, big_all)}

    w_fox_in = _pad_axis(wfull["fox_w_in"][0], 1, LANES)
    w_fox_qkv, w_fox_f = w_fox_in[:, :3 * nq], w_fox_in[:, 3 * nq:]
    w_fox_o = wfull["fox_w_o"][0]
    w_a = _pad_axis(wfull["mla_w_a"][0], 1, LANES)
    wq = wfull["mla_w_uq"][0].reshape(MLA_Q_RANK, N_HEADS, HEAD_DIM + MLA_ROPE_DIM)
    w_uq = _pad_axis(wq, 2, LANES).reshape(MLA_Q_RANK, N_HEADS * LANES)
    wkv = wfull["mla_w_ukv"][0].reshape(MLA_KV_RANK, N_HEADS, 2 * HEAD_DIM)
    w_ukv = jnp.concatenate([wkv[:, :, :HEAD_DIM].reshape(MLA_KV_RANK, -1), wkv[:, :, HEAD_DIM:].reshape(MLA_KV_RANK, -1)], axis=1)
    w_mla_o = wfull["mla_w_o"][0]
    w_ffn_in = wfull["ffn_w_in"]
    w_ffn_out = wfull["ffn_w_out"]
    conv_b = ffn_conv_b

    fox_scale = HEAD_DIM ** -0.5
    mla_scale = (HEAD_DIM + MLA_ROPE_DIM) ** -0.5
    pos = jnp.arange(s, dtype=F32)
    inv_freq = ROPE_BASE ** (-jnp.arange(0, MLA_ROPE_DIM, 2, dtype=F32) / MLA_ROPE_DIM)
    ang = pos[:, None] * inv_freq[None, :]
    cos16, sin16 = jnp.cos(ang), jnp.sin(ang)
    z16, z32, z64 = jnp.zeros((s, 16), F32), jnp.zeros((s, 32), F32), jnp.zeros((s, 64), F32)
    tab_a = jnp.concatenate([jnp.ones((s, 64), F32), cos16, cos16, z32], axis=1) * (mla_scale * LOG2E)
    tab_b = jnp.concatenate([z64, -sin16, z16, z32], axis=1) * (mla_scale * LOG2E)
    tab_c = jnp.concatenate([z64, z16, sin16, z32], axis=1) * (mla_scale * LOG2E)

    h0 = _norm_fwd(x0, scale[0], shift[0], plus_one=True, out_dtype=BF16, name="ada_fwd0")
    q_mult = jnp.concatenate([jnp.full((1, nq), fox_scale * LOG2E, F32), jnp.ones((1, 2 * nq), F32)], axis=1)
    qkv = _matmul(h0, w_fox_qkv, out_dtype=BF16, col_scale=q_mult, name="fox_proj")
    fl = _matmul(h0, w_fox_f, name="fox_proj_f")[:, :N_HEADS]
    cum = _fox_gate_fwd(fl, fox_b_f, name="fox_gate_fwd")
    fox_cfg = dict(qc=0, kc=N_HEADS // 2, vc=N_HEADS, width=HEAD_DIM)
    kb2 = cum * -LOG2E
    fo, fox_lse = _attn_fwd(qkv, qkv, qkv, _bias_lane_terms(kb2), name="fox_attn_fwd", **fox_cfg)
    fox_res = (qkv, qkv, qkv, kb2, fo, fox_lse)
    y0 = _matmul(fo, w_fox_o, name="fox_out")
    x1 = _resid_fwd(x0, y0, gate[0], name="resid_fwd0")

    def ffn_fwd(xin, li, sub):
        hh = _norm_fwd(xin, scale[sub], shift[sub], plus_one=True, out_dtype=BF16, name=f"ada_fwd{sub}")
        u = _matmul(hh, w_ffn_in[li], name=f"ffn_up{li}")
        g = _conv_gate_fwd(u, conv_w[li], conv_b[li:li + 1], name=f"conv_fwd{li}")
        y = _matmul(g, w_ffn_out[li], name=f"ffn_down{li}")
        return _resid_fwd(xin, y, gate[sub], name=f"resid_fwd{sub}"), (hh, u, g, y)

    x2, ffn0_res = ffn_fwd(x1, 0, 1)

    h2 = _norm_fwd(x2, scale[2], shift[2], plus_one=True, out_dtype=BF16, name="ada_fwd2")
    a = _matmul(h2, w_a, name="mla_a")
    a_q, a_kv = a[:, :MLA_Q_RANK], a[:, MLA_Q_RANK:MLA_Q_RANK + MLA_KV_RANK]
    kr1 = a[:, MLA_Q_RANK + MLA_KV_RANK:MLA_Q_RANK + MLA_KV_RANK + MLA_ROPE_HALF]
    kr2 = a[:, MLA_Q_RANK + MLA_KV_RANK + MLA_ROPE_HALF:MLA_Q_RANK + MLA_KV_RANK + MLA_ROPE_DIM]
    cq = _norm_fwd(a_q, g_q, jnp.zeros_like(g_q), plus_one=False, out_dtype=BF16, name="mla_norm_q")
    ckv = _norm_fwd(a_kv, g_kv, jnp.zeros_like(g_kv), plus_one=False, out_dtype=BF16, name="mla_norm_kv")
    qf = _matmul(cq, w_uq, name="mla_uq")
    kvf = _matmul(ckv, w_ukv, out_dtype=BF16, name="mla_ukv")
    mq = _rope_heads(qf, tab_a, tab_b, tab_c, out_dtype=BF16, name="rope_q")
    kk1, kk2 = _rope(kr1, kr2, cos16, sin16, negate=False, name="rope_k")
    k_tail = jnp.concatenate([kk1, kk2, z32], axis=1).astype(BF16)
    mk = jnp.concatenate([kvf[:, :nq].reshape(s, N_HEADS, HEAD_DIM),
                          jnp.broadcast_to(k_tail[:, None, :], (s, N_HEADS, HEAD_DIM))], axis=-1).reshape(s, N_HEADS * LANES)
    mla_cfg = dict(qc=0, kc=0, vc=N_HEADS // 2, width=LANES)
    mo, mla_lse = _attn_fwd(mq, mk, kvf, None, name="mla_attn_fwd", **mla_cfg)
    mla_res = (mq, mk, kvf, None, mo, mla_lse)
    y2 = _matmul(mo, w_mla_o, name="mla_out")
    x3 = _resid_fwd(x2, y2, gate[2], name="resid_fwd2")

    x4, ffn1_res = ffn_fwd(x3, 1, 3)

    loss_vec, dx4, d_final_g = _final_loss(x4, final_g.reshape(1, d), target, name="final_loss")
    loss = lax.psum(loss_vec[0, 0], ("x", "y", "c"))

    grads = {}
    dmod = [None] * 4

    def ffn_bwd(dx_out, xin, li, sub, res):
        hh, u, g, y = res
        dy, dgate = _resid_bwd(dx_out, y, gate[sub], name=f"resid_bwd{sub}")
        gw_out = _matmul(g, dy, ta=True, out_dtype=BF16, name=f"ffn_down_dw{li}")
        dg = _matmul(dy, w_ffn_out[li], tb=True, name=f"ffn_down_dx{li}")
        du, dcw, dcb = _conv_gate_bwd(u, conv_w[li], conv_b[li:li + 1], dg, name=f"conv_bwd{li}")
        gw_in = _matmul(hh, du, ta=True, out_dtype=BF16, name=f"ffn_up_dw{li}")
        dh = _matmul(du, w_ffn_in[li], tb=True, out_dtype=BF16, name=f"ffn_up_dx{li}")
        dx_in, dscale, dshift = _norm_bwd(xin, scale[sub], dh, dx_out, plus_one=True, name=f"ada_bwd{sub}")
        dmod[sub] = jnp.concatenate([dshift, dscale, dgate], axis=1)
        return dx_in, gw_in, dcw, dcb, gw_out

    dx3, gw_in1, dcw1, dcb1, gw_out1 = ffn_bwd(dx4, x3, 1, 3, ffn1_res)

    dy2, dgate2 = _resid_bwd(dx3, y2, gate[2], name="resid_bwd2")
    grads["mla_w_o"] = _matmul(mo, dy2, ta=True, out_dtype=BF16, name="mla_out_dw")[None]
    dmo = _matmul(dy2, w_mla_o, tb=True, out_dtype=BF16, name="mla_out_dx")
    dmq, dmk, dmv, _ = _attention_bwd(mla_res, dmo, dq_mult=1.0 / LOG2E, dk_mult=1.0 / LOG2E, out_dtype=F32,
                                      name="mla_attn", **mla_cfg)
    dqf = _rope_heads(dmq, tab_a, -tab_b, -tab_c, out_dtype=BF16, name="rope_q_bwd")
    g_uq = _matmul(cq, dqf, ta=True, out_dtype=BF16, name="mla_uq_dw")
    dcq = _matmul(dqf, w_uq, tb=True, name="mla_uq_dx")
    dmk3 = dmk.reshape(s, N_HEADS, LANES)
    dkr = _group_sum(dmk, name="mla_krope_sum")
    dkr1, dkr2 = _rope(dkr[:, HEAD_DIM:HEAD_DIM + MLA_ROPE_HALF], dkr[:, HEAD_DIM + MLA_ROPE_HALF:HEAD_DIM + MLA_ROPE_DIM],
                       cos16, sin16, negate=True, name="rope_k_bwd")
    dkvf = jnp.concatenate([dmk3[:, :, :HEAD_DIM].reshape(s, nq).astype(BF16), dmv.astype(BF16)], axis=1)
    g_ukv = _matmul(ckv, dkvf, ta=True, out_dtype=BF16, name="mla_ukv_dw")
    dckv = _matmul(dkvf, w_ukv, tb=True, name="mla_ukv_dx")
    da_q, dg_q, _ = _norm_bwd(a_q, g_q, dcq, None, plus_one=False, name="mla_norm_q_bwd")
    da_kv, dg_kv, _ = _norm_bwd(a_kv, g_kv, dckv, None, plus_one=False, name="mla_norm_kv_bwd")
    da = jnp.concatenate([da_q, da_kv, dkr1, dkr2, jnp.zeros((s, w_a.shape[1] - 672), F32)], axis=1).astype(BF16)
    grads["mla_w_a"] = _matmul(h2, da, ta=True, out_dtype=BF16, name="mla_a_dw")[None, :, :672]
    dh2 = _matmul(da, w_a, tb=True, out_dtype=BF16, name="mla_a_dx")
    dx2, dscale2, dshift2 = _norm_bwd(x2, scale[2], dh2, dx3, plus_one=True, name="ada_bwd2")
    dmod[2] = jnp.concatenate([dshift2, dscale2, dgate2], axis=1)
    grads["mla_w_uq"] = g_uq.reshape(MLA_Q_RANK, N_HEADS, LANES)[:, :, :HEAD_DIM + MLA_ROPE_DIM].reshape(1, MLA_Q_RANK, -1)
    grads["mla_w_ukv"] = _per_head([g_ukv[:, :nq], g_ukv[:, nq:]], MLA_KV_RANK)[None]
    grads["mla_g_q"], grads["mla_g_kv"] = dg_q, dg_kv

    dx1, gw_in0, dcw0, dcb0, gw_out0 = ffn_bwd(dx2, x1, 0, 1, ffn0_res)
    grads["ffn_w_in"] = jnp.stack([gw_in0, gw_in1])
    grads["ffn_w_out"] = jnp.stack([gw_out0, gw_out1])
    grads["ffn_conv_w"] = jnp.stack([dcw0, dcw1])
    g_conv_b = jnp.concatenate([dcb0, dcb1], axis=0)

    dy0, dgate0 = _resid_bwd(dx1, y0, gate[0], name="resid_bwd0")
    grads["fox_w_o"] = _matmul(fo, dy0, ta=True, out_dtype=BF16, name="fox_out_dw")[None]
    dfo = _matmul(dy0, w_fox_o, tb=True, out_dtype=BF16, name="fox_out_dx")
    dfq, dfk, dfv, dcum = _attention_bwd(fox_res, dfo, dq_mult=fox_scale, dk_mult=1.0 / LOG2E, out_dtype=BF16,
                                         name="fox_attn", **fox_cfg)
    dfl, g_b_f = _fox_gate_bwd(fl, fox_b_f, dcum, name="fox_gate_bwd")
    dproj = jnp.concatenate([dfq.astype(BF16), dfk, dfv, _pad_axis(dfl, 1, LANES).astype(BF16)], axis=1)
    grads["fox_w_in"] = _matmul(h0, dproj, ta=True, out_dtype=BF16, name="fox_proj_dw")[None, :, :3 * nq + N_HEADS]
    dh0 = _matmul(dproj, w_fox_in, tb=True, out_dtype=BF16, name="fox_proj_dx")
    dx0, dscale0, dshift0 = _norm_bwd(x0, scale[0], dh0, dx1, plus_one=True, name="ada_bwd0")
    dmod[0] = jnp.concatenate([dshift0, dscale0, dgate0], axis=1)

    dmod_send = _pad_axis(jnp.stack(dmod, axis=0).reshape(4, N_DEV, cols).transpose(1, 0, 2), 1, 8)
    dmod_recv, = _exchange([dmod_send], same_src=False, name="scatter_dmod")
    dmod_all = dmod_recv[:, :4]
    dmod_pad = _pad_axis(dmod_all, 0, LANES)
    g_ada_w = jnp.stack([_matmul(silu_c, dmod_pad[:, i], ta=True, name=f"ada_dw{i}") for i in range(4)])
    grads["ada_w"] = g_ada_w.reshape(ada_w.shape)
    grads["ada_b"] = _sum_slabs(dmod_recv, name="ada_db")[:4].reshape(ada_b.shape)

    sharded = list(_BIG) + list(_SMALL)
    axes = {**_BIG, **_SMALL}
    recv = _exchange([_full_to_chunks(grads[n], axes[n]) for n in sharded], same_src=False, name="scatter_grads")
    grads.update(dict(zip(sharded, recv)))
    repl = _exchange([g_b_f, g_conv_b, d_final_g], same_src=True, name="gather_repl_grads")
    grads.update(dict(zip(_REPL, repl)))

    grad_out, deltas, new_m, new_v = {}, {}, {}, {}
    for n in order:
        grad_out[n], deltas[n], new_m[n], new_v[n] = _adamw(
            weights[n], grads[n], mom_m[n], mom_v[n], slabs=n in axes or n in _REPL, name=f"adamw_{n}")

    grad_x = dx0[None]
    return (loss, grad_x, *[grad_out[n] for n in order], *[deltas[n] for n in order],
            *[new_m[n] for n in order], *[new_v[n] for n in order])
```

```python
import jax
import jax.numpy as jnp
from jax import lax
from jax.experimental import pallas as pl
from jax.experimental.pallas import tpu as pltpu

F32 = jnp.float32
BF16 = jnp.bfloat16
HIGHEST = lax.Precision.HIGHEST

N_DEV = 8
D_MODEL = 1024
N_HEADS = 16
HEAD_DIM = 64
MLA_ROPE_HALF = 16
MLA_Q_RANK = 384
MLA_KV_RANK = 256
MLA_ROPE_DIM = 32
NORM_EPS = 1e-6
ROPE_BASE = 10000.0
ADAM_LR = 0.001
ADAM_B1 = 0.9
ADAM_B2 = 0.999
ADAM_EPS = 1e-08
ADAM_WD = 0.01
ADAM_STEP = 10

LANES = 128
VMEM_LIMIT_BYTES = 56 * 1024 * 1024
ROW_BLOCK = 512
ATT_BLOCK = 512
CONV_ROWS = 1024
MM_BM, MM_BN, MM_BK = 512, 1024, 2048
MM_K_WHOLE = 3328
LOG2E = 1.4426950408889634


def _params(*sem):
    return pltpu.CompilerParams(dimension_semantics=sem or None, vmem_limit_bytes=VMEM_LIMIT_BYTES)


def _blk(dim, pref):
    if dim <= pref:
        return dim
    b = pref - pref % LANES
    while b >= LANES:
        if dim % b == 0:
            return b
        b -= LANES
    raise ValueError(f"no block for {dim}")


def _row_blk(rows, pref):
    if rows <= pref:
        return rows
    for b in range(pref - pref % 8, 7, -8):
        if rows % b == 0:
            return b
    return rows


def _pad_axis(a, axis, mult):
    pad = (-a.shape[axis]) % mult
    if pad == 0:
        return a
    widths = [(0, 0)] * a.ndim
    widths[axis] = (0, pad)
    return jnp.pad(a, widths)


def _matmul(a, b, *, ta=False, tb=False, out_dtype=F32, col_scale=None, name):
    m, k = (a.shape[1], a.shape[0]) if ta else a.shape
    n = b.shape[0] if tb else b.shape[1]
    assert (b.shape[1] if tb else b.shape[0]) == k, (a.shape, b.shape, ta, tb)
    bm, bn = _blk(m, MM_BM if ta else 2 * MM_BM), _blk(n, MM_BN)
    bk = k if k <= MM_K_WHOLE else _blk(k, MM_BK)
    nk = k // bk
    dims = (((0 if ta else 1,), (1 if tb else 0,)), ((), ()))
    has_scale = col_scale is not None
    use_acc = nk > 1 and (out_dtype != F32 or has_scale)

    def body(*refs):
        a_ref, b_ref = refs[0], refs[1]
        s_ref = refs[2] if has_scale else None
        o_ref = refs[3] if has_scale else refs[2]
        acc_ref = refs[-1] if use_acc else o_ref
        kk = pl.program_id(2)
        part = lax.dot_general(a_ref[...].astype(BF16), b_ref[...].astype(BF16), dims, preferred_element_type=F32)

        def finish(val):
            if has_scale:
                val = val * s_ref[...]
            o_ref[...] = val.astype(out_dtype)

        if nk == 1:
            finish(part)
            return

        @pl.when(kk == 0)
        def _():
            acc_ref[...] = part

        @pl.when(kk > 0)
        def _():
            acc_ref[...] += part

        if use_acc:
            @pl.when(kk == nk - 1)
            def _():
                finish(acc_ref[...])

    a_spec = pl.BlockSpec((bk, bm), lambda i, j, kk: (kk, i)) if ta else pl.BlockSpec((bm, bk), lambda i, j, kk: (i, kk))
    b_spec = pl.BlockSpec((bn, bk), lambda i, j, kk: (j, kk)) if tb else pl.BlockSpec((bk, bn), lambda i, j, kk: (kk, j))
    return pl.pallas_call(
        body, name=name, grid=(m // bm, n // bn, nk),
        in_specs=[a_spec, b_spec] + ([pl.BlockSpec((1, bn), lambda i, j, kk: (0, j))] if has_scale else []),
        out_specs=pl.BlockSpec((bm, bn), lambda i, j, kk: (i, j)),
        out_shape=jax.ShapeDtypeStruct((m, n), out_dtype),
        scratch_shapes=[pltpu.VMEM((bm, bn), F32)] if use_acc else [],
        compiler_params=_params("parallel", "parallel", "arbitrary"),
    )(*([a, b] + ([col_scale] if has_scale else [])))


def _norm_fwd(x, mul, add, *, plus_one, out_dtype, name):
    s, n = x.shape
    bs = _blk(s, ROW_BLOCK)

    def body(x_ref, m_ref, a_ref, o_ref):
        xv = x_ref[...]
        r = lax.rsqrt(jnp.mean(xv * xv, axis=-1, keepdims=True) + NORM_EPS)
        mv = m_ref[...] + 1.0 if plus_one else m_ref[...]
        o_ref[...] = (xv * r * mv + a_ref[...]).astype(out_dtype)

    row = pl.BlockSpec((bs, n), lambda i: (i, 0))
    vec = pl.BlockSpec((1, n), lambda i: (0, 0))
    return pl.pallas_call(
        body, name=name, grid=(s // bs,), in_specs=[row, vec, vec], out_specs=row,
        out_shape=jax.ShapeDtypeStruct((s, n), out_dtype), compiler_params=_params("parallel"),
    )(x, mul, add)


def _norm_bwd(x, mul, dy, dres, *, plus_one, name):
    s, n = x.shape
    bs = _blk(s, ROW_BLOCK)
    has_res = dres is not None

    def body(*refs):
        if has_res:
            x_ref, m_ref, dy_ref, dres_ref, dx_ref, dm_ref, da_ref = refs
        else:
            x_ref, m_ref, dy_ref, dx_ref, dm_ref, da_ref = refs
        xv = x_ref[...]
        dyv = dy_ref[...].astype(F32)
        r = lax.rsqrt(jnp.mean(xv * xv, axis=-1, keepdims=True) + NORM_EPS)
        xn = xv * r
        mv = m_ref[...] + 1.0 if plus_one else m_ref[...]
        g = dyv * mv
        dx = r * (g - xn * jnp.mean(g * xn, axis=-1, keepdims=True))
        if has_res:
            dx = dx + dres_ref[...]
        dx_ref[...] = dx

        @pl.when(pl.program_id(0) == 0)
        def _():
            dm_ref[...] = jnp.zeros_like(dm_ref)
            da_ref[...] = jnp.zeros_like(da_ref)

        dm_ref[...] += jnp.sum(dyv * xn, axis=0, keepdims=True)
        da_ref[...] += jnp.sum(dyv, axis=0, keepdims=True)

    row = pl.BlockSpec((bs, n), lambda i: (i, 0))
    vec = pl.BlockSpec((1, n), lambda i: (0, 0))
    ins = [x, mul, dy] + ([dres] if has_res else [])
    return pl.pallas_call(
        body, name=name, grid=(s // bs,),
        in_specs=[row, vec, row] + ([row] if has_res else []), out_specs=[row, vec, vec],
        out_shape=[jax.ShapeDtypeStruct((s, n), F32), jax.ShapeDtypeStruct((1, n), F32), jax.ShapeDtypeStruct((1, n), F32)],
        compiler_params=_params("arbitrary"),
    )(*ins)


def _resid_fwd(x, y, gate, *, name):
    s, n = x.shape
    bs = _blk(s, ROW_BLOCK)

    def body(x_ref, y_ref, g_ref, o_ref):
        o_ref[...] = x_ref[...] + g_ref[...] * y_ref[...]

    row = pl.BlockSpec((bs, n), lambda i: (i, 0))
    vec = pl.BlockSpec((1, n), lambda i: (0, 0))
    return pl.pallas_call(
        body, name=name, grid=(s // bs,), in_specs=[row, row, vec], out_specs=row,
        out_shape=jax.ShapeDtypeStruct((s, n), F32), compiler_params=_params("parallel"),
    )(x, y, gate)


def _resid_bwd(dx, y, gate, *, name):
    s, n = dx.shape
    bs = _blk(s, ROW_BLOCK)

    def body(dx_ref, y_ref, g_ref, dy_ref, dg_ref):
        dxv = dx_ref[...]
        dy_ref[...] = (g_ref[...] * dxv).astype(BF16)

        @pl.when(pl.program_id(0) == 0)
        def _():
            dg_ref[...] = jnp.zeros_like(dg_ref)

        dg_ref[...] += jnp.sum(dxv * y_ref[...], axis=0, keepdims=True)

    row = pl.BlockSpec((bs, n), lambda i: (i, 0))
    vec = pl.BlockSpec((1, n), lambda i: (0, 0))
    return pl.pallas_call(
        body, name=name, grid=(s // bs,), in_specs=[row, row, vec], out_specs=[row, vec],
        out_shape=[jax.ShapeDtypeStruct((s, n), BF16), jax.ShapeDtypeStruct((1, n), F32)],
        compiler_params=_params("arbitrary"),
    )(dx, y, gate)


def _final_loss(x, g, target, *, name):
    s, n = x.shape
    bs = _blk(s, ROW_BLOCK)

    def body(x_ref, g_ref, t_ref, loss_ref, dx_ref, dg_ref):
        xv = x_ref[...]
        r = lax.rsqrt(jnp.mean(xv * xv, axis=-1, keepdims=True) + NORM_EPS)
        xn = xv * r
        gv = g_ref[...]
        err = xn * gv - t_ref[...]
        dout = err * (1.0 / n)
        gg = dout * gv
        dx_ref[...] = r * (gg - xn * jnp.mean(gg * xn, axis=-1, keepdims=True))

        @pl.when(pl.program_id(0) == 0)
        def _():
            loss_ref[...] = jnp.zeros_like(loss_ref)
            dg_ref[...] = jnp.zeros_like(dg_ref)

        part = jnp.sum(jnp.sum(err * err, axis=-1, keepdims=True), axis=0, keepdims=True) * (0.5 / n)
        loss_ref[...] += jnp.broadcast_to(part, loss_ref.shape)
        dg_ref[...] += jnp.sum(dout * xn, axis=0, keepdims=True)

    row = pl.BlockSpec((bs, n), lambda i: (i, 0))
    vec = pl.BlockSpec((1, n), lambda i: (0, 0))
    return pl.pallas_call(
        body, name=name, grid=(s // bs,), in_specs=[row, vec, row],
        out_specs=[pl.BlockSpec((1, LANES), lambda i: (0, 0)), row, vec],
        out_shape=[jax.ShapeDtypeStruct((1, LANES), F32), jax.ShapeDtypeStruct((s, n), F32), jax.ShapeDtypeStruct((1, n), F32)],
        compiler_params=_params("arbitrary"),
    )(x, g, target)


def _lane_lt64(shape):
    return lax.broadcasted_iota(jnp.int32, shape, 1) < HEAD_DIM


def _keep_low(x):
    return jnp.where(_lane_lt64(x.shape), x.astype(F32), 0.0).astype(x.dtype)


def _keep_high(x):
    return jnp.where(_lane_lt64(x.shape), 0.0, x.astype(F32)).astype(x.dtype)


def _lane_merge(a, b):
    n = max(a.shape[0], b.shape[0])
    return jnp.where(_lane_lt64((n, LANES)), a, b)


def _pair(x, width, masked):
    if width == HEAD_DIM:
        return (_keep_low(x), _keep_high(x)) if masked else (x, x)
    return x[:, :LANES], x[:, LANES:]


def _qk_t(a, b):
    return lax.dot_general(a, b, (((1,), (1,)), ((), ())), preferred_element_type=F32)


def _attn_specs(s, blk, width, cols, resident):
    w = 2 * width
    if resident:
        return pl.BlockSpec((s, w), lambda p, i: (0, cols + p))
    return pl.BlockSpec((blk, w), lambda p, i: (i, cols + p))


BIAS_TERMS = 3


def _key_blocks(vt, blk):
    return vt.reshape(vt.shape[0], vt.shape[1] // blk, blk).transpose(1, 0, 2)


def _attn_fwd(q, k, vt, kbl, *, qc, kc, width, name):
    s = q.shape[0]
    blk = _blk(s, ATT_BLOCK)
    nb = s // blk
    has_bias = kbl is not None
    assert has_bias == (width == HEAD_DIM)

    def body(*refs):
        if has_bias:
            q_ref, k_ref, vt_ref, kbl_ref, o_ref, lse_ref = refs
        else:
            q_ref, k_ref, vt_ref, o_ref, lse_ref = refs
        i = pl.program_id(1)
        q2 = q_ref[...]
        if has_bias:
            lane = lax.broadcasted_iota(jnp.int32, (blk, LANES), 1)
            qf = q2.astype(F32)
            qh = (jnp.where(lane < HEAD_DIM, qf, jnp.where(lane < HEAD_DIM + BIAS_TERMS, 1.0, 0.0)).astype(BF16),
                  jnp.where(lane >= HEAD_DIM, qf, jnp.where(lane < BIAS_TERMS, 1.0, 0.0)).astype(BF16))
        else:
            qh = (q2[:, :LANES], q2[:, LANES:])

        def step(j, carry, nblk, diag):
            rows = pl.ds(pl.multiple_of(j * blk, blk), nblk * blk)
            ones = jnp.ones((16, nblk * blk), BF16)
            k2 = k_ref[rows, :]
            if has_bias:
                low = _lane_lt64(k2.shape)
                kf, bf = k2.astype(F32), kbl_ref[rows, :].astype(F32)
                kh = (jnp.where(low, kf, bf).astype(BF16), jnp.where(low, bf, kf).astype(BF16))
            else:
                kh = (k2[:, :LANES], k2[:, LANES:])
            out = []
            for hd in range(2):
                m, l, acc = carry[hd]
                st = _qk_t(kh[hd], qh[hd])
                if diag:
                    row = lax.broadcasted_iota(jnp.int32, (blk, blk), 0)
                    colq = lax.broadcasted_iota(jnp.int32, (blk, blk), 1)
                    st = jnp.where(row <= colq, st, -1e30)
                m_new = jnp.maximum(m, jnp.max(st, axis=0, keepdims=True))
                alpha = jnp.exp2(m - m_new)
                pt = jnp.exp2(st - m_new).astype(BF16)
                l = alpha * l + jnp.dot(ones, pt, preferred_element_type=F32)
                acc = alpha * acc
                for b in range(nblk):
                    acc = acc + jnp.dot(vt_ref[j + b], pt[b * blk:(b + 1) * blk], preferred_element_type=F32)
                out.append((m_new, l, acc))
            return tuple(out)

        one = (jnp.full((1, blk), -1e30, F32), jnp.zeros((16, blk), F32), jnp.zeros((LANES, blk), F32))
        carry = lax.fori_loop(0, i // 2, lambda j, c: step(2 * j, c, 2, False), (one, one))
        carry = lax.fori_loop(0, i % 2, lambda _, c: step(i - 1, c, 1, False), carry)
        (ma, la, acca), (mb, lb, accb) = step(i, carry, 1, True)
        la = jnp.max(la, axis=0, keepdims=True)
        lb = jnp.max(lb, axis=0, keepdims=True)
        low = lax.broadcasted_iota(jnp.int32, (LANES, blk), 0) < HEAD_DIM
        o_ref[...] = jnp.where(low, acca / la, accb / lb).T
        lse_ref[0, 0] = ma + jnp.log(la) * LOG2E
        lse_ref[1, 0] = mb + jnp.log(lb) * LOG2E

    ins = [q, k, vt] + ([kbl] if has_bias else [])
    return pl.pallas_call(
        body, name=name, grid=(N_HEADS // 2, nb),
        in_specs=[_attn_specs(s, blk, width, qc, False), _attn_specs(s, blk, width, kc, True),
                  pl.BlockSpec((nb, LANES, blk), lambda p, i: (0, p, 0))]
                 + ([_attn_specs(s, blk, HEAD_DIM, 0, True)] if has_bias else []),
        out_specs=[pl.BlockSpec((blk, LANES), lambda p, i: (i, p)), pl.BlockSpec((2, 1, 1, blk), lambda p, i: (p, i, 0, 0))],
        out_shape=[jax.ShapeDtypeStruct((s, N_HEADS * HEAD_DIM), F32), jax.ShapeDtypeStruct((N_HEADS, nb, 1, blk), F32)],
        compiler_params=_params("parallel", "parallel"),
    )(*ins)


def _bias_lane_terms(kb2):
    s = kb2.shape[0]
    terms, rest = [], kb2
    for _ in range(BIAS_TERMS):
        t = lax.reduce_precision(rest, 8, 7)
        terms.append(t.astype(BF16))
        rest = rest - t
    t3 = jnp.stack(terms, axis=-1).reshape(s, N_HEADS // 2, 2, BIAS_TERMS)
    pad = jnp.zeros((s, N_HEADS // 2, HEAD_DIM - BIAS_TERMS), BF16)
    return jnp.concatenate([t3[:, :, 1], pad, t3[:, :, 0], pad], axis=-1).reshape(s, N_HEADS * HEAD_DIM)


def _causal_keep(n):
    row = lax.broadcasted_iota(jnp.int32, (n, n), 0)
    col = lax.broadcasted_iota(jnp.int32, (n, n), 1)
    return col <= row


def _attn_delta(o, do, *, name):
    s, n = o.shape
    bs = _blk(s, ROW_BLOCK)

    def body(o_ref, do_ref, d_ref):
        for g in range(n // LANES):
            prod = do_ref[:, g * LANES:(g + 1) * LANES].astype(F32) * o_ref[:, g * LANES:(g + 1) * LANES]
            low = _lane_lt64(prod.shape)
            d_ref[:, g * LANES:(g + 1) * LANES] = _lane_merge(
                jnp.sum(jnp.where(low, prod, 0.0), axis=-1, keepdims=True),
                jnp.sum(jnp.where(low, 0.0, prod), axis=-1, keepdims=True))

    row = pl.BlockSpec((bs, n), lambda i: (i, 0))
    return pl.pallas_call(
        body, name=name, grid=(s // bs,), in_specs=[row, row], out_specs=row,
        out_shape=jax.ShapeDtypeStruct((s, n), F32), compiler_params=_params("parallel"),
    )(o, do)


def _attn_bwd(q, k, v, kb_col, do, lse_row, delta_row, *, qc, kc, vc, width, dq_mult, dk_mult, out_dtype, name):
    s = q.shape[0]
    blk = _blk(s, ATT_BLOCK)
    nb = s // blk
    has_bias = kb_col is not None

    def body(*refs):
        if has_bias:
            q_ref, k_ref, v_ref, kb_ref, do_ref, lse_ref, dl_ref, dk_ref, dv_ref, db_ref, dq_ref, dr_ref = refs
        else:
            q_ref, k_ref, v_ref, do_ref, lse_ref, dl_ref, dk_ref, dv_ref, db_ref, dq_ref = refs
        j = pl.program_id(1)

        @pl.when(j == 0)
        def _():
            dq_ref[...] = jnp.zeros_like(dq_ref)
            if has_bias:
                dr_ref[...] = jnp.zeros_like(dr_ref)

        kh = _pair(k_ref[...], width, True)
        v2 = v_ref[...]
        vh = (_keep_low(v2), _keep_high(v2))
        if has_bias:
            kb2 = kb_ref[0]
            kbh = (kb2[:, 0:1], kb2[:, 1:2])

        def step(i, carry, nblk, diag):
            rows = pl.ds(pl.multiple_of(i * blk, blk), nblk * blk)
            qh = _pair(q_ref[rows, :], width, False)
            doi = do_ref[rows, :]
            out, dq_parts = [], []
            for hd in range(2):
                dk, dvv, db = carry[hd]
                st = _qk_t(kh[hd], qh[hd])
                if has_bias:
                    st = st + kbh[hd]
                if diag:
                    row = lax.broadcasted_iota(jnp.int32, (blk, blk), 0)
                    colq = lax.broadcasted_iota(jnp.int32, (blk, blk), 1)
                    st = jnp.where(row <= colq, st, -1e30)
                lse_i = jnp.concatenate([lse_ref[hd, i + b] for b in range(nblk)], axis=1)
                delta_i = jnp.concatenate([dl_ref[hd, i + b] for b in range(nblk)], axis=1)
                pt = jnp.exp2(st - lse_i)
                dvv = dvv + jnp.dot(pt.astype(BF16), doi, preferred_element_type=F32)
                dst = pt * (_qk_t(vh[hd], doi) - delta_i)
                dsb = dst.astype(BF16)
                dk = dk + jnp.dot(dsb, qh[hd], preferred_element_type=F32)
                db = db + jnp.sum(dst, axis=-1, keepdims=True)
                dq_parts.append(lax.dot_general(dsb, kh[hd], (((0,), (0,)), ((), ())), preferred_element_type=F32))
                if has_bias:
                    rsum = jnp.sum(dst, axis=0, keepdims=True)
                    for b in range(nblk):
                        dr_ref[hd, i + b] += rsum[:, b * blk:(b + 1) * blk]
                out.append((dk, dvv, db))
            if width == HEAD_DIM:
                dq_ref[rows, :] += (dq_parts[0] + dq_parts[1]) * dq_mult
            else:
                dq_ref[rows, 0:LANES] += dq_parts[0] * dq_mult
                dq_ref[rows, LANES:2 * LANES] += dq_parts[1] * dq_mult
            return tuple(out)

        one = (jnp.zeros((blk, LANES), F32), jnp.zeros((blk, LANES), F32), jnp.zeros((blk, 1), F32))
        carry = step(j, (one, one), 1, True)
        rest = nb - 1 - j
        carry = lax.fori_loop(0, rest // 2, lambda t, c: step(j + 1 + 2 * t, c, 2, False), carry)
        (dka, dva, dba), (dkb, dvb, dbb) = lax.fori_loop(0, rest % 2, lambda _, c: step(nb - 1, c, 1, False), carry)
        if width == HEAD_DIM:
            dk = _lane_merge(dka, dkb)
        else:
            dk = jnp.concatenate([dka, dkb], axis=1)
        dk_ref[...] = (dk * dk_mult).astype(out_dtype)
        dv_ref[...] = _lane_merge(dva, dvb).astype(out_dtype)
        db_ref[...] = _lane_merge(dba, dbb)

    stat = pl.BlockSpec((blk, LANES), lambda p, jj: (jj, p))
    rows = pl.BlockSpec((2, nb, 1, blk), lambda p, jj: (p, 0, 0, 0))
    ins = [q, k, v] + ([kb_col] if has_bias else []) + [do, lse_row, delta_row]
    return pl.pallas_call(
        body, name=name, grid=(N_HEADS // 2, nb),
        in_specs=[_attn_specs(s, blk, width, qc, True), _attn_specs(s, blk, width, kc, False),
                  _attn_specs(s, blk, HEAD_DIM, vc, False)]
                 + ([pl.BlockSpec((1, blk, 2), lambda p, jj: (p, jj, 0))] if has_bias else [])
                 + [pl.BlockSpec((s, LANES), lambda p, jj: (0, p)), rows, rows],
        out_specs=[pl.BlockSpec((blk, 2 * width), lambda p, jj: (jj, p)), stat, stat,
                   pl.BlockSpec((s, 2 * width), lambda p, jj: (0, p))] + ([rows] if has_bias else []),
        out_shape=[jax.ShapeDtypeStruct((s, N_HEADS * width), out_dtype), jax.ShapeDtypeStruct((s, N_HEADS * HEAD_DIM), out_dtype),
                   jax.ShapeDtypeStruct((s, N_HEADS * HEAD_DIM), F32), jax.ShapeDtypeStruct((s, N_HEADS * width), F32)]
                  + ([jax.ShapeDtypeStruct((N_HEADS, nb, 1, blk), F32)] if has_bias else []),
        compiler_params=_params("parallel", "arbitrary"),
    )(*ins)


def _head_stat(t):
    return t[:, ::HEAD_DIM]


def _stat_rows(t16, blk):
    s = t16.shape[0]
    return t16.T.reshape(N_HEADS, s // blk, 1, blk)


def _attention_bwd(res, do, *, qc, kc, vc, width, dq_mult, dk_mult, out_dtype, name):
    q, k, v, bias, o, lse_row = res
    s = q.shape[0]
    blk = _blk(s, ATT_BLOCK)
    kb_col = None if bias is None else bias.reshape(s, N_HEADS // 2, 2).transpose(1, 0, 2)
    delta_row = _stat_rows(_head_stat(_attn_delta(o, do, name=name + "_delta")), blk)
    outs = _attn_bwd(q, k, v, kb_col, do, lse_row, delta_row, qc=qc, kc=kc, vc=vc, width=width, dq_mult=dq_mult,
                     dk_mult=dk_mult, out_dtype=out_dtype, name=name + "_bwd")
    dk, dv, dcol, dq = outs[:4]
    if bias is None:
        return dq, dk, dv, None
    return dq, dk, dv, outs[4].reshape(N_HEADS, s).T - _head_stat(dcol)


def _fox_gate_fwd(fl, bf, *, name):
    s, n = fl.shape
    bs = _blk(s, ROW_BLOCK)

    def body(fl_ref, bf_ref, cum_ref, carry_ref):
        @pl.when(pl.program_id(0) == 0)
        def _():
            carry_ref[...] = jnp.zeros_like(carry_ref)

        z = fl_ref[...] + bf_ref[...]
        lf = jnp.minimum(z, 0.0) - jnp.log1p(jnp.exp(-jnp.abs(z)))
        row = lax.broadcasted_iota(jnp.int32, (bs, bs), 0)
        col = lax.broadcasted_iota(jnp.int32, (bs, bs), 1)
        tri = (col <= row).astype(F32)
        cum_ref[...] = jnp.dot(tri, lf, preferred_element_type=F32, precision=HIGHEST) + carry_ref[...]
        carry_ref[...] += jnp.sum(lf, axis=0, keepdims=True)

    return pl.pallas_call(
        body, name=name, grid=(s // bs,),
        in_specs=[pl.BlockSpec((bs, n), lambda i: (i, 0)), pl.BlockSpec((1, n), lambda i: (0, 0))],
        out_specs=pl.BlockSpec((bs, n), lambda i: (i, 0)),
        out_shape=jax.ShapeDtypeStruct((s, n), F32), scratch_shapes=[pltpu.VMEM((1, n), F32)],
        compiler_params=_params("arbitrary"),
    )(fl, bf)


def _fox_gate_bwd(fl, bf, dcum, *, name):
    s, n = fl.shape
    bs = _blk(s, ROW_BLOCK)
    nb = s // bs

    def body(fl_ref, bf_ref, dc_ref, dz_ref, dbf_ref, carry_ref):
        @pl.when(pl.program_id(0) == 0)
        def _():
            carry_ref[...] = jnp.zeros_like(carry_ref)
            dbf_ref[...] = jnp.zeros_like(dbf_ref)

        dc = dc_ref[...]
        row = lax.broadcasted_iota(jnp.int32, (bs, bs), 0)
        col = lax.broadcasted_iota(jnp.int32, (bs, bs), 1)
        tri = (col >= row).astype(F32)
        dlf = jnp.dot(tri, dc, preferred_element_type=F32, precision=HIGHEST) + carry_ref[...]
        carry_ref[...] += jnp.sum(dc, axis=0, keepdims=True)
        z = fl_ref[...] + bf_ref[...]
        dz = dlf / (1.0 + jnp.exp(z))
        dz_ref[...] = dz
        dbf_ref[...] += jnp.sum(dz, axis=0, keepdims=True)

    rev = pl.BlockSpec((bs, n), lambda i: (nb - 1 - i, 0))
    vec = pl.BlockSpec((1, n), lambda i: (0, 0))
    return pl.pallas_call(
        body, name=name, grid=(nb,), in_specs=[rev, vec, rev], out_specs=[rev, vec],
        out_shape=[jax.ShapeDtypeStruct((s, n), F32), jax.ShapeDtypeStruct((1, n), F32)],
        scratch_shapes=[pltpu.VMEM((1, n), F32)], compiler_params=_params("arbitrary"),
    )(fl, bf, dcum)


def _rope(x1, x2, cos, sin, *, negate, name):
    s, n = x1.shape
    bs = _blk(s, ROW_BLOCK)

    def body(a_ref, b_ref, c_ref, s_ref, o1_ref, o2_ref):
        a, b, cv = a_ref[...], b_ref[...], c_ref[...]
        sv = -s_ref[...] if negate else s_ref[...]
        o1_ref[...] = a * cv - b * sv
        o2_ref[...] = b * cv + a * sv

    row = pl.BlockSpec((bs, n), lambda i: (i, 0))
    return pl.pallas_call(
        body, name=name, grid=(s // bs,), in_specs=[row] * 4, out_specs=[row, row],
        out_shape=[jax.ShapeDtypeStruct((s, n), F32)] * 2, compiler_params=_params("parallel"),
    )(x1, x2, cos, sin)


def _rope_heads(x, ta, tb, tc, *, out_dtype, name):
    s, n = x.shape
    bs = _blk(s, ROW_BLOCK)

    def body(x_ref, a_ref, b_ref, c_ref, o_ref):
        av, bv, cv = a_ref[...], b_ref[...], c_ref[...]
        for g in range(n // LANES):
            xg = x_ref[:, g * LANES:(g + 1) * LANES]
            og = xg * av + pltpu.roll(xg, LANES - MLA_ROPE_HALF, 1) * bv + pltpu.roll(xg, MLA_ROPE_HALF, 1) * cv
            o_ref[:, g * LANES:(g + 1) * LANES] = og.astype(out_dtype)

    row = pl.BlockSpec((bs, n), lambda i: (i, 0))
    tab = pl.BlockSpec((bs, LANES), lambda i: (i, 0))
    return pl.pallas_call(
        body, name=name, grid=(s // bs,), in_specs=[row, tab, tab, tab], out_specs=row,
        out_shape=jax.ShapeDtypeStruct((s, n), out_dtype), compiler_params=_params("parallel"),
    )(x, ta, tb, tc)


def _group_sum(x, *, name):
    s, n = x.shape
    bs = _blk(s, ROW_BLOCK)

    def body(x_ref, o_ref):
        acc = x_ref[:, 0:LANES]
        for g in range(1, n // LANES):
            acc = acc + x_ref[:, g * LANES:(g + 1) * LANES]
        o_ref[...] = acc

    return pl.pallas_call(
        body, name=name, grid=(s // bs,), in_specs=[pl.BlockSpec((bs, n), lambda i: (i, 0))],
        out_specs=pl.BlockSpec((bs, LANES), lambda i: (i, 0)),
        out_shape=jax.ShapeDtypeStruct((s, LANES), F32), compiler_params=_params("parallel"),
    )(x)


def _shift_down(x, k):
    return pltpu.roll(x, k, 0)


def _conv_rows(ext, w_ref, b_ref, rows):
    y = b_ref[...] + w_ref[0:1, :] * _shift_down(ext, 2) + w_ref[1:2, :] * _shift_down(ext, 1) + w_ref[2:3, :] * ext
    return y[8:8 + rows]


def _conv_gate_fwd(u, cw, cb, *, name):
    s, f2 = u.shape
    f = f2 // 2
    nf = f // LANES
    r = _blk(s, CONV_ROWS)
    r8 = r // 8

    def body(ug_ref, ugp_ref, uv_ref, uvp_ref, wg_ref, wv_ref, bg_ref, bv_ref, o_ref):
        first = pl.program_id(1) == 0

        def conv(cur_ref, prev_ref, w_ref, b_ref):
            prev = jnp.where(first, 0.0, prev_ref[...])
            return _conv_rows(jnp.concatenate([prev, cur_ref[...]], axis=0), w_ref, b_ref, r)

        yg = conv(ug_ref, ugp_ref, wg_ref, bg_ref)
        yv = conv(uv_ref, uvp_ref, wv_ref, bv_ref)
        o_ref[...] = (yg * jax.nn.sigmoid(yg) * yv).astype(BF16)

    def cur(off):
        return pl.BlockSpec((r, LANES), lambda c, i: (i, c + off))

    def prev(off):
        return pl.BlockSpec((8, LANES), lambda c, i: (jnp.maximum(i * r8 - 1, 0), c + off))

    def wspec(rows, off):
        return pl.BlockSpec((rows, LANES), lambda c, i: (0, c + off))

    return pl.pallas_call(
        body, name=name, grid=(nf, s // r),
        in_specs=[cur(0), prev(0), cur(nf), prev(nf), wspec(3, 0), wspec(3, nf), wspec(1, 0), wspec(1, nf)],
        out_specs=pl.BlockSpec((r, LANES), lambda c, i: (i, c)),
        out_shape=jax.ShapeDtypeStruct((s, f), BF16), compiler_params=_params("parallel", "parallel"),
    )(u, u, u, u, cw, cw, cb, cb)


def _conv_gate_bwd(u, cw, cb, dg, *, name):
    s, f2 = u.shape
    f = f2 // 2
    nf = f // LANES
    r = _blk(s, CONV_ROWS)
    r8 = r // 8
    nr = s // r

    def body(ug_ref, ugp_ref, ugn_ref, uv_ref, uvp_ref, uvn_ref, wg_ref, wv_ref, bg_ref, bv_ref, dg_ref, dgn_ref,
             dug_ref, duv_ref, dwg_ref, dwv_ref, dbg_ref, dbv_ref):
        i = pl.program_id(1)
        first, last = i == 0, i == nr - 1

        def ext_of(cur_ref, prev_ref, next_ref):
            prev = jnp.where(first, 0.0, prev_ref[...])
            return jnp.concatenate([prev, cur_ref[...], next_ref[...]], axis=0)

        eg, ev = ext_of(ug_ref, ugp_ref, ugn_ref), ext_of(uv_ref, uvp_ref, uvn_ref)
        yg = _conv_rows(eg, wg_ref, bg_ref, r + 8)
        yv = _conv_rows(ev, wv_ref, bv_ref, r + 8)
        dgn = jnp.where(last, 0.0, dgn_ref[...])
        dgx = jnp.concatenate([dg_ref[...], dgn], axis=0)
        sg = jax.nn.sigmoid(yg)
        dyg = dgx * yv * (sg * (1.0 + yg * (1.0 - sg)))
        dyv = dgx * (yg * sg)

        @pl.when(i == 0)
        def _():
            for ref in (dwg_ref, dwv_ref, dbg_ref, dbv_ref):
                ref[...] = jnp.zeros_like(ref)

        def grads(dy, ext, w_ref, du_ref, dw_ref, db_ref):
            n = r + 8
            du = w_ref[2:3, :] * dy + w_ref[1:2, :] * pltpu.roll(dy, n - 1, 0) + w_ref[0:1, :] * pltpu.roll(dy, n - 2, 0)
            du_ref[...] = du[0:r].astype(BF16)
            dyc = dy[0:r]
            db_ref[...] += jnp.sum(dyc, axis=0, keepdims=True)
            ext_c = ext[0:r + 8]
            dw_ref[0:1, :] += jnp.sum(dyc * _shift_down(ext_c, 2)[8:], axis=0, keepdims=True)
            dw_ref[1:2, :] += jnp.sum(dyc * _shift_down(ext_c, 1)[8:], axis=0, keepdims=True)
            dw_ref[2:3, :] += jnp.sum(dyc * ext_c[8:], axis=0, keepdims=True)

        grads(dyg, eg, wg_ref, dug_ref, dwg_ref, dbg_ref)
        grads(dyv, ev, wv_ref, duv_ref, dwv_ref, dbv_ref)

    def cur(off):
        return pl.BlockSpec((r, LANES), lambda c, i: (i, c + off))

    def prev(off):
        return pl.BlockSpec((8, LANES), lambda c, i: (jnp.maximum(i * r8 - 1, 0), c + off))

    def nxt(off):
        return pl.BlockSpec((8, LANES), lambda c, i: (jnp.minimum((i + 1) * r8, s // 8 - 1), c + off))

    def wspec(rows, off):
        return pl.BlockSpec((rows, LANES), lambda c, i: (0, c + off))

    outs = pl.pallas_call(
        body, name=name, grid=(nf, nr),
        in_specs=[cur(0), prev(0), nxt(0), cur(nf), prev(nf), nxt(nf), wspec(3, 0), wspec(3, nf), wspec(1, 0), wspec(1, nf),
                  cur(0), nxt(0)],
        out_specs=[cur(0), cur(0), wspec(3, 0), wspec(3, 0), wspec(1, 0), wspec(1, 0)],
        out_shape=[jax.ShapeDtypeStruct((s, f), BF16), jax.ShapeDtypeStruct((s, f), BF16),
                   jax.ShapeDtypeStruct((3, f), F32), jax.ShapeDtypeStruct((3, f), F32),
                   jax.ShapeDtypeStruct((1, f), F32), jax.ShapeDtypeStruct((1, f), F32)],
        compiler_params=_params("parallel", "arbitrary"),
    )(u, u, u, u, u, u, cw, cw, cb, cb, dg, dg)
    dug, duv, dwg, dwv, dbg, dbv = outs
    return jnp.concatenate([dug, duv], axis=1), jnp.concatenate([dwg, dwv], axis=1), jnp.concatenate([dbg, dbv], axis=1)


def _adamw(w, g, m, v, *, slabs, name):
    shape = w.shape
    cols = shape[-1]
    rows = w.size // cols
    w2, m2, v2 = (t.reshape(rows, cols) for t in (w, m, v))
    g2 = g.reshape((N_DEV, rows, cols) if slabs else (rows, cols))
    br = _row_blk(rows, ROW_BLOCK // 2 if slabs else ROW_BLOCK)

    def body(w_ref, g_ref, m_ref, v_ref, go_ref, d_ref, nm_ref, nv_ref):
        if slabs:
            gv = g_ref[0].astype(F32)
            for p in range(1, N_DEV):
                gv = gv + g_ref[p].astype(F32)
        else:
            gv = g_ref[...]
        nm = ADAM_B1 * m_ref[...] + (1.0 - ADAM_B1) * gv
        nv = ADAM_B2 * v_ref[...] + (1.0 - ADAM_B2) * (gv * gv)
        m_hat = nm / (1.0 - ADAM_B1 ** ADAM_STEP)
        v_hat = nv / (1.0 - ADAM_B2 ** ADAM_STEP)
        go_ref[...] = gv
        d_ref[...] = -ADAM_LR * (m_hat / (jnp.sqrt(v_hat) + ADAM_EPS) + ADAM_WD * w_ref[...])
        nm_ref[...] = nm
        nv_ref[...] = nv

    spec = pl.BlockSpec((br, cols), lambda i: (i, 0))
    gspec = pl.BlockSpec((N_DEV, br, cols), lambda i: (0, i, 0)) if slabs else spec
    outs = pl.pallas_call(
        body, name=name, grid=(rows // br,), in_specs=[spec, gspec, spec, spec], out_specs=[spec] * 4,
        out_shape=[jax.ShapeDtypeStruct((rows, cols), F32)] * 4, compiler_params=_params("parallel"),
    )(w2, g2, m2, v2)
    return tuple(t.reshape(shape) for t in outs)


def _exchange(xs, *, same_src, name):
    n = len(xs)
    slabs = [x.shape if same_src else x.shape[1:] for x in xs]

    def body(*refs):
        x_refs, o_refs = refs[:n], refs[n:2 * n]
        send_sems, recv_sems, loc_sems = refs[2 * n:]
        ix, iy, ic = lax.axis_index("x"), lax.axis_index("y"), lax.axis_index("c")
        me = 4 * ix + 2 * iy + ic
        local, sends, recvs = [], [], []
        for a in range(n):
            def src(p, a=a):
                return x_refs[a] if same_src else x_refs[a].at[p]

            local.append(pltpu.make_async_copy(src(me), o_refs[a].at[me], loc_sems.at[a]))
            for k in (1, 2, 4, 3, 5, 6, 7):
                px = 1 - ix if k & 4 else ix
                py = 1 - iy if k & 2 else iy
                pc = 1 - ic if k & 1 else ic
                p = 4 * px + 2 * py + pc
                for dst, out in ((me, sends), (p, recvs)):
                    out.append(pltpu.make_async_remote_copy(
                        src_ref=src(p), dst_ref=o_refs[a].at[dst], send_sem=send_sems.at[a, k - 1],
                        recv_sem=recv_sems.at[a, k - 1], device_id=(px, py, pc), device_id_type=pl.DeviceIdType.MESH))
        for cp in local + sends:
            cp.start()
        for cp in recvs:
            cp.wait_recv()
        for cp in sends:
            cp.wait_send()
        for cp in local:
            cp.wait()

    return pl.pallas_call(
        body, name=name,
        in_specs=[pl.BlockSpec(memory_space=pl.ANY)] * n, out_specs=[pl.BlockSpec(memory_space=pl.ANY)] * n,
        out_shape=[jax.ShapeDtypeStruct((N_DEV,) + tuple(sl), x.dtype) for sl, x in zip(slabs, xs)],
        scratch_shapes=[pltpu.SemaphoreType.DMA((n, N_DEV - 1)), pltpu.SemaphoreType.DMA((n, N_DEV - 1)),
                        pltpu.SemaphoreType.DMA((n,))],
        compiler_params=pltpu.CompilerParams(has_side_effects=True, vmem_limit_bytes=VMEM_LIMIT_BYTES),
    )(*xs)


def _sum_slabs(x, *, name):
    n, r, c = x.shape
    br = _row_blk(r, ROW_BLOCK)

    def body(x_ref, o_ref):
        acc = x_ref[0]
        for p in range(1, n):
            acc = acc + x_ref[p]
        o_ref[...] = acc

    return pl.pallas_call(
        body, name=name, grid=(r // br,), in_specs=[pl.BlockSpec((n, br, c), lambda i: (0, i, 0))],
        out_specs=pl.BlockSpec((br, c), lambda i: (i, 0)),
        out_shape=jax.ShapeDtypeStruct((r, c), F32), compiler_params=_params("parallel"),
    )(x)


def _silu(x, *, name):
    def body(x_ref, o_ref):
        xv = x_ref[...]
        o_ref[...] = (xv * jax.nn.sigmoid(xv)).astype(BF16)

    return pl.pallas_call(body, name=name, out_shape=jax.ShapeDtypeStruct(x.shape, BF16),
                          compiler_params=_params())(x)


_BIG = {"fox_w_in": 2, "fox_w_o": 1, "mla_w_a": 1, "mla_w_uq": 2, "mla_w_ukv": 2, "mla_w_o": 1, "ffn_w_in": 2, "ffn_w_out": 1}
_SMALL = {"mla_g_q": 1, "mla_g_kv": 1, "ffn_conv_w": 2}
_REPL = ("fox_b_f", "ffn_conv_b", "final_g")


def _gathered_to_full(g, axis):
    full = jnp.moveaxis(g, 0, axis)
    shape = list(full.shape)
    shape[axis:axis + 2] = [shape[axis] * shape[axis + 1]]
    return full.reshape(shape)


def _full_to_chunks(full, axis):
    shape = list(full.shape)
    shape[axis:axis + 1] = [N_DEV, shape[axis] // N_DEV]
    return jnp.moveaxis(full.reshape(shape), axis, 0)


def _per_head(parts, s_or_rows):
    return jnp.concatenate([p.reshape(s_or_rows, N_HEADS, -1) for p in parts], axis=-1).reshape(s_or_rows, -1)


def kernel(x, c, ada_w, ada_b, fox_w_in, fox_b_f, fox_w_o, mla_w_a, mla_g_q, mla_g_kv, mla_w_uq, mla_w_ukv, mla_w_o, ffn_w_in, ffn_conv_w, ffn_conv_b, ffn_w_out, final_g, loss_target, m_ada_w, m_ada_b, m_fox_w_in, m_fox_b_f, m_fox_w_o, m_mla_w_a, m_mla_g_q, m_mla_g_kv, m_mla_w_uq, m_mla_w_ukv, m_mla_w_o, m_ffn_w_in, m_ffn_conv_w, m_ffn_conv_b, m_ffn_w_out, m_final_g, v_ada_w, v_ada_b, v_fox_w_in, v_fox_b_f, v_fox_w_o, v_mla_w_a, v_mla_g_q, v_mla_g_kv, v_mla_w_uq, v_mla_w_ukv, v_mla_w_o, v_ffn_w_in, v_ffn_conv_w, v_ffn_conv_b, v_ffn_w_out, v_final_g):
    weights = dict(ada_w=ada_w, ada_b=ada_b, fox_w_in=fox_w_in, fox_b_f=fox_b_f, fox_w_o=fox_w_o, mla_w_a=mla_w_a,
                   mla_g_q=mla_g_q, mla_g_kv=mla_g_kv, mla_w_uq=mla_w_uq, mla_w_ukv=mla_w_ukv, mla_w_o=mla_w_o,
                   ffn_w_in=ffn_w_in, ffn_conv_w=ffn_conv_w, ffn_conv_b=ffn_conv_b, ffn_w_out=ffn_w_out, final_g=final_g)
    mom_m = dict(ada_w=m_ada_w, ada_b=m_ada_b, fox_w_in=m_fox_w_in, fox_b_f=m_fox_b_f, fox_w_o=m_fox_w_o, mla_w_a=m_mla_w_a,
                 mla_g_q=m_mla_g_q, mla_g_kv=m_mla_g_kv, mla_w_uq=m_mla_w_uq, mla_w_ukv=m_mla_w_ukv, mla_w_o=m_mla_w_o,
                 ffn_w_in=m_ffn_w_in, ffn_conv_w=m_ffn_conv_w, ffn_conv_b=m_ffn_conv_b, ffn_w_out=m_ffn_w_out, final_g=m_final_g)
    mom_v = dict(ada_w=v_ada_w, ada_b=v_ada_b, fox_w_in=v_fox_w_in, fox_b_f=v_fox_b_f, fox_w_o=v_fox_w_o, mla_w_a=v_mla_w_a,
                 mla_g_q=v_mla_g_q, mla_g_kv=v_mla_g_kv, mla_w_uq=v_mla_w_uq, mla_w_ukv=v_mla_w_ukv, mla_w_o=v_mla_w_o,
                 ffn_w_in=v_ffn_w_in, ffn_conv_w=v_ffn_conv_w, ffn_conv_b=v_ffn_conv_b, ffn_w_out=v_ffn_w_out, final_g=v_final_g)
    order = list(weights)
    x0 = x[0]
    target = loss_target[0]
    s = x0.shape[0]
    d = D_MODEL
    cols = ada_w.shape[-1]
    nq = N_HEADS * HEAD_DIM

    small_names = ["c"] + list(_SMALL)
    small_all = dict(zip(small_names, _exchange([c] + [weights[n] for n in _SMALL], same_src=True, name="gather_small")))
    c_all = small_all["c"].reshape(N_DEV, d)
    g_q = _gathered_to_full(small_all["mla_g_q"], 1)
    g_kv = _gathered_to_full(small_all["mla_g_kv"], 1)
    conv_w = _gathered_to_full(small_all["ffn_conv_w"], 2)

    c_pad = _pad_axis(c_all, 0, LANES)
    silu_c = _silu(c_pad, name="silu_c")
    w_ada = ada_w.reshape(4, d, cols)
    b_ada = ada_b.reshape(4, 1, cols)
    mods = [_matmul(silu_c, w_ada[i], name=f"ada_mod{i}")[:N_DEV] + b_ada[i] for i in range(4)]
    mod_send = _pad_axis(jnp.stack(mods, axis=1), 1, 8)
    mod_recv, = _exchange([mod_send], same_src=False, name="scatter_mod")
    mod = mod_recv[:, :4].transpose(1, 0, 2).reshape(4, 3 * d)
    shift = [mod[i:i + 1, 0:d] for i in range(4)]
    scale = [mod[i:i + 1, d:2 * d] for i in range(4)]
    gate = [mod[i:i + 1, 2 * d:3 * d] for i in range(4)]

    big_all = _exchange([weights[n].astype(BF16) for n in _BIG], same_src=True, name="gather_weights")
    wfull = {n: _gathered_to_full(g, _BIG[n]) for n, g in zip(_BIG, big_all)}

    w_fox_in = _pad_axis(wfull["fox_w_in"][0], 1, LANES)
    w_fox_qkv, w_fox_f = w_fox_in[:, :3 * nq], w_fox_in[:, 3 * nq:]
    w_fox_o = wfull["fox_w_o"][0]
    w_a = _pad_axis(wfull["mla_w_a"][0], 1, LANES)
    wq = wfull["mla_w_uq"][0].reshape(MLA_Q_RANK, N_HEADS, HEAD_DIM + MLA_ROPE_DIM)
    w_uq = _pad_axis(wq, 2, LANES).reshape(MLA_Q_RANK, N_HEADS * LANES)
    wkv = wfull["mla_w_ukv"][0].reshape(MLA_KV_RANK, N_HEADS, 2 * HEAD_DIM)
    w_ukv = jnp.concatenate([wkv[:, :, :HEAD_DIM].reshape(MLA_KV_RANK, -1), wkv[:, :, HEAD_DIM:].reshape(MLA_KV_RANK, -1)], axis=1)
    w_mla_o = wfull["mla_w_o"][0]
    w_ffn_in = wfull["ffn_w_in"]
    w_ffn_out = wfull["ffn_w_out"]
    conv_b = ffn_conv_b

    fox_scale = HEAD_DIM ** -0.5
    mla_scale = (HEAD_DIM + MLA_ROPE_DIM) ** -0.5
    pos = jnp.arange(s, dtype=F32)
    inv_freq = ROPE_BASE ** (-jnp.arange(0, MLA_ROPE_DIM, 2, dtype=F32) / MLA_ROPE_DIM)
    ang = pos[:, None] * inv_freq[None, :]
    cos16, sin16 = jnp.cos(ang), jnp.sin(ang)
    z16, z32, z64 = jnp.zeros((s, 16), F32), jnp.zeros((s, 32), F32), jnp.zeros((s, 64), F32)
    tab_a = jnp.concatenate([jnp.ones((s, 64), F32), cos16, cos16, z32], axis=1) * (mla_scale * LOG2E)
    tab_b = jnp.concatenate([z64, -sin16, z16, z32], axis=1) * (mla_scale * LOG2E)
    tab_c = jnp.concatenate([z64, z16, sin16, z32], axis=1) * (mla_scale * LOG2E)

    h0 = _norm_fwd(x0, scale[0], shift[0], plus_one=True, out_dtype=BF16, name="ada_fwd0")
    q_mult = jnp.concatenate([jnp.full((1, nq), fox_scale * LOG2E, F32), jnp.ones((1, 2 * nq), F32)], axis=1)
    qkv = _matmul(h0, w_fox_qkv, out_dtype=BF16, col_scale=q_mult, name="fox_proj")
    fl = _matmul(h0, w_fox_f, name="fox_proj_f")[:, :N_HEADS]
    cum = _fox_gate_fwd(fl, fox_b_f, name="fox_gate_fwd")
    fox_cfg = dict(qc=0, kc=N_HEADS // 2, vc=N_HEADS, width=HEAD_DIM)
    kb2 = cum * -LOG2E
    att_blk = _blk(s, ATT_BLOCK)
    fvt = _matmul(w_fox_qkv[:, 2 * nq:], h0, ta=True, tb=True, out_dtype=BF16, name="fox_proj_vt")
    fo, fox_lse = _attn_fwd(qkv, qkv, _key_blocks(fvt, att_blk), _bias_lane_terms(kb2), qc=0, kc=N_HEADS // 2,
                            width=HEAD_DIM, name="fox_attn_fwd")
    fox_res = (qkv, qkv, qkv, kb2, fo, fox_lse)
    y0 = _matmul(fo, w_fox_o, name="fox_out")
    x1 = _resid_fwd(x0, y0, gate[0], name="resid_fwd0")

    def ffn_fwd(xin, li, sub):
        hh = _norm_fwd(xin, scale[sub], shift[sub], plus_one=True, out_dtype=BF16, name=f"ada_fwd{sub}")
        u = _matmul(hh, w_ffn_in[li], name=f"ffn_up{li}")
        g = _conv_gate_fwd(u, conv_w[li], conv_b[li:li + 1], name=f"conv_fwd{li}")
        y = _matmul(g, w_ffn_out[li], name=f"ffn_down{li}")
        return _resid_fwd(xin, y, gate[sub], name=f"resid_fwd{sub}"), (hh, u, g, y)

    x2, ffn0_res = ffn_fwd(x1, 0, 1)

    h2 = _norm_fwd(x2, scale[2], shift[2], plus_one=True, out_dtype=BF16, name="ada_fwd2")
    a = _matmul(h2, w_a, name="mla_a")
    a_q, a_kv = a[:, :MLA_Q_RANK], a[:, MLA_Q_RANK:MLA_Q_RANK + MLA_KV_RANK]
    kr1 = a[:, MLA_Q_RANK + MLA_KV_RANK:MLA_Q_RANK + MLA_KV_RANK + MLA_ROPE_HALF]
    kr2 = a[:, MLA_Q_RANK + MLA_KV_RANK + MLA_ROPE_HALF:MLA_Q_RANK + MLA_KV_RANK + MLA_ROPE_DIM]
    cq = _norm_fwd(a_q, g_q, jnp.zeros_like(g_q), plus_one=False, out_dtype=BF16, name="mla_norm_q")
    ckv = _norm_fwd(a_kv, g_kv, jnp.zeros_like(g_kv), plus_one=False, out_dtype=BF16, name="mla_norm_kv")
    qf = _matmul(cq, w_uq, name="mla_uq")
    kvf = _matmul(ckv, w_ukv, out_dtype=BF16, name="mla_ukv")
    mq = _rope_heads(qf, tab_a, tab_b, tab_c, out_dtype=BF16, name="rope_q")
    kk1, kk2 = _rope(kr1, kr2, cos16, sin16, negate=False, name="rope_k")
    k_tail = jnp.concatenate([kk1, kk2, z32], axis=1).astype(BF16)
    mk = jnp.concatenate([kvf[:, :nq].reshape(s, N_HEADS, HEAD_DIM),
                          jnp.broadcast_to(k_tail[:, None, :], (s, N_HEADS, HEAD_DIM))], axis=-1).reshape(s, N_HEADS * LANES)
    mla_cfg = dict(qc=0, kc=0, vc=N_HEADS // 2, width=LANES)
    mvt = _matmul(w_ukv[:, nq:], ckv, ta=True, tb=True, out_dtype=BF16, name="mla_ukv_vt")
    mo, mla_lse = _attn_fwd(mq, mk, _key_blocks(mvt, att_blk), None, qc=0, kc=0, width=LANES, name="mla_attn_fwd")
    mla_res = (mq, mk, kvf, None, mo, mla_lse)
    y2 = _matmul(mo, w_mla_o, name="mla_out")
    x3 = _resid_fwd(x2, y2, gate[2], name="resid_fwd2")

    x4, ffn1_res = ffn_fwd(x3, 1, 3)

    loss_vec, dx4, d_final_g = _final_loss(x4, final_g.reshape(1, d), target, name="final_loss")
    loss = lax.psum(loss_vec[0, 0], ("x", "y", "c"))

    grads = {}
    dmod = [None] * 4

    def ffn_bwd(dx_out, xin, li, sub, res):
        hh, u, g, y = res
        dy, dgate = _resid_bwd(dx_out, y, gate[sub], name=f"resid_bwd{sub}")
        gw_out = _matmul(g, dy, ta=True, out_dtype=BF16, name=f"ffn_down_dw{li}")
        dg = _matmul(dy, w_ffn_out[li], tb=True, name=f"ffn_down_dx{li}")
        du, dcw, dcb = _conv_gate_bwd(u, conv_w[li], conv_b[li:li + 1], dg, name=f"conv_bwd{li}")
        gw_in = _matmul(hh, du, ta=True, out_dtype=BF16, name=f"ffn_up_dw{li}")
        dh = _matmul(du, w_ffn_in[li], tb=True, out_dtype=BF16, name=f"ffn_up_dx{li}")
        dx_in, dscale, dshift = _norm_bwd(xin, scale[sub], dh, dx_out, plus_one=True, name=f"ada_bwd{sub}")
        dmod[sub] = jnp.concatenate([dshift, dscale, dgate], axis=1)
        return dx_in, gw_in, dcw, dcb, gw_out

    dx3, gw_in1, dcw1, dcb1, gw_out1 = ffn_bwd(dx4, x3, 1, 3, ffn1_res)

    dy2, dgate2 = _resid_bwd(dx3, y2, gate[2], name="resid_bwd2")
    grads["mla_w_o"] = _matmul(mo, dy2, ta=True, out_dtype=BF16, name="mla_out_dw")[None]
    dmo = _matmul(dy2, w_mla_o, tb=True, out_dtype=BF16, name="mla_out_dx")
    dmq, dmk, dmv, _ = _attention_bwd(mla_res, dmo, dq_mult=1.0 / LOG2E, dk_mult=1.0 / LOG2E, out_dtype=F32,
                                      name="mla_attn", **mla_cfg)
    dqf = _rope_heads(dmq, tab_a, -tab_b, -tab_c, out_dtype=BF16, name="rope_q_bwd")
    g_uq = _matmul(cq, dqf, ta=True, out_dtype=BF16, name="mla_uq_dw")
    dcq = _matmul(dqf, w_uq, tb=True, name="mla_uq_dx")
    dmk3 = dmk.reshape(s, N_HEADS, LANES)
    dkr = _group_sum(dmk, name="mla_krope_sum")
    dkr1, dkr2 = _rope(dkr[:, HEAD_DIM:HEAD_DIM + MLA_ROPE_HALF], dkr[:, HEAD_DIM + MLA_ROPE_HALF:HEAD_DIM + MLA_ROPE_DIM],
                       cos16, sin16, negate=True, name="rope_k_bwd")
    dkvf = jnp.concatenate([dmk3[:, :, :HEAD_DIM].reshape(s, nq).astype(BF16), dmv.astype(BF16)], axis=1)
    g_ukv = _matmul(ckv, dkvf, ta=True, out_dtype=BF16, name="mla_ukv_dw")
    dckv = _matmul(dkvf, w_ukv, tb=True, name="mla_ukv_dx")
    da_q, dg_q, _ = _norm_bwd(a_q, g_q, dcq, None, plus_one=False, name="mla_norm_q_bwd")
    da_kv, dg_kv, _ = _norm_bwd(a_kv, g_kv, dckv, None, plus_one=False, name="mla_norm_kv_bwd")
    da = jnp.concatenate([da_q, da_kv, dkr1, dkr2, jnp.zeros((s, w_a.shape[1] - 672), F32)], axis=1).astype(BF16)
    grads["mla_w_a"] = _matmul(h2, da, ta=True, out_dtype=BF16, name="mla_a_dw")[None, :, :672]
    dh2 = _matmul(da, w_a, tb=True, out_dtype=BF16, name="mla_a_dx")
    dx2, dscale2, dshift2 = _norm_bwd(x2, scale[2], dh2, dx3, plus_one=True, name="ada_bwd2")
    dmod[2] = jnp.concatenate([dshift2, dscale2, dgate2], axis=1)
    grads["mla_w_uq"] = g_uq.reshape(MLA_Q_RANK, N_HEADS, LANES)[:, :, :HEAD_DIM + MLA_ROPE_DIM].reshape(1, MLA_Q_RANK, -1)
    grads["mla_w_ukv"] = _per_head([g_ukv[:, :nq], g_ukv[:, nq:]], MLA_KV_RANK)[None]
    grads["mla_g_q"], grads["mla_g_kv"] = dg_q, dg_kv

    dx1, gw_in0, dcw0, dcb0, gw_out0 = ffn_bwd(dx2, x1, 0, 1, ffn0_res)
    grads["ffn_w_in"] = jnp.stack([gw_in0, gw_in1])
    grads["ffn_w_out"] = jnp.stack([gw_out0, gw_out1])
    grads["ffn_conv_w"] = jnp.stack([dcw0, dcw1])
    g_conv_b = jnp.concatenate([dcb0, dcb1], axis=0)

    dy0, dgate0 = _resid_bwd(dx1, y0, gate[0], name="resid_bwd0")
    grads["fox_w_o"] = _matmul(fo, dy0, ta=True, out_dtype=BF16, name="fox_out_dw")[None]
    dfo = _matmul(dy0, w_fox_o, tb=True, out_dtype=BF16, name="fox_out_dx")
    dfq, dfk, dfv, dcum = _attention_bwd(fox_res, dfo, dq_mult=fox_scale, dk_mult=1.0 / LOG2E, out_dtype=BF16,
                                         name="fox_attn", **fox_cfg)
    dfl, g_b_f = _fox_gate_bwd(fl, fox_b_f, dcum, name="fox_gate_bwd")
    dproj = jnp.concatenate([dfq.astype(BF16), dfk, dfv, _pad_axis(dfl, 1, LANES).astype(BF16)], axis=1)
    grads["fox_w_in"] = _matmul(h0, dproj, ta=True, out_dtype=BF16, name="fox_proj_dw")[None, :, :3 * nq + N_HEADS]
    dh0 = _matmul(dproj, w_fox_in, tb=True, out_dtype=BF16, name="fox_proj_dx")
    dx0, dscale0, dshift0 = _norm_bwd(x0, scale[0], dh0, dx1, plus_one=True, name="ada_bwd0")
    dmod[0] = jnp.concatenate([dshift0, dscale0, dgate0], axis=1)

    dmod_send = _pad_axis(jnp.stack(dmod, axis=0).reshape(4, N_DEV, cols).transpose(1, 0, 2), 1, 8)
    dmod_recv, = _exchange([dmod_send], same_src=False, name="scatter_dmod")
    dmod_all = dmod_recv[:, :4]
    dmod_pad = _pad_axis(dmod_all, 0, LANES)
    g_ada_w = jnp.stack([_matmul(silu_c, dmod_pad[:, i], ta=True, name=f"ada_dw{i}") for i in range(4)])
    grads["ada_w"] = g_ada_w.reshape(ada_w.shape)
    grads["ada_b"] = _sum_slabs(dmod_recv, name="ada_db")[:4].reshape(ada_b.shape)

    sharded = list(_BIG) + list(_SMALL)
    axes = {**_BIG, **_SMALL}
    recv = _exchange([_full_to_chunks(grads[n], axes[n]) for n in sharded], same_src=False, name="scatter_grads")
    grads.update(dict(zip(sharded, recv)))
    repl = _exchange([g_b_f, g_conv_b, d_final_g], same_src=True, name="gather_repl_grads")
    grads.update(dict(zip(_REPL, repl)))

    grad_out, deltas, new_m, new_v = {}, {}, {}, {}
    for n in order:
        grad_out[n], deltas[n], new_m[n], new_v[n] = _adamw(
            weights[n], grads[n], mom_m[n], mom_v[n], slabs=n in axes or n in _REPL, name=f"adamw_{n}")

    grad_x = dx0[None]
    return (loss, grad_x, *[grad_out[n] for n in order], *[deltas[n] for n in order],
            *[new_m[n] for n in order], *[new_v[n] for n in order])
```

```python
import jax
import jax.numpy as jnp
from jax import lax
from jax.experimental import pallas as pl
from jax.experimental.pallas import tpu as pltpu

F32 = jnp.float32
BF16 = jnp.bfloat16
HIGHEST = lax.Precision.HIGHEST

N_DEV = 8
D_MODEL = 1024
N_HEADS = 16
HEAD_DIM = 64
MLA_ROPE_HALF = 16
MLA_Q_RANK = 384
MLA_KV_RANK = 256
MLA_ROPE_DIM = 32
NORM_EPS = 1e-6
ROPE_BASE = 10000.0
ADAM_LR = 0.001
ADAM_B1 = 0.9
ADAM_B2 = 0.999
ADAM_EPS = 1e-08
ADAM_WD = 0.01
ADAM_STEP = 10

LANES = 128
VMEM_LIMIT_BYTES = 56 * 1024 * 1024
ROW_BLOCK = 512
ATT_BLOCK = 512
CONV_ROWS = 1024
MM_BM, MM_BN, MM_BK = 512, 1024, 2048
MM_K_WHOLE = 3328
LOG2E = 1.4426950408889634


def _params(*sem):
    return pltpu.CompilerParams(dimension_semantics=sem or None, vmem_limit_bytes=VMEM_LIMIT_BYTES)


def _blk(dim, pref):
    if dim <= pref:
        return dim
    b = pref - pref % LANES
    while b >= LANES:
        if dim % b == 0:
            return b
        b -= LANES
    raise ValueError(f"no block for {dim}")


def _row_blk(rows, pref):
    if rows <= pref:
        return rows
    for b in range(pref - pref % 8, 7, -8):
        if rows % b == 0:
            return b
    return rows


def _pad_axis(a, axis, mult):
    pad = (-a.shape[axis]) % mult
    if pad == 0:
        return a
    widths = [(0, 0)] * a.ndim
    widths[axis] = (0, pad)
    return jnp.pad(a, widths)


def _matmul(a, b, *, ta=False, tb=False, out_dtype=F32, col_scale=None, name):
    m, k = (a.shape[1], a.shape[0]) if ta else a.shape
    n = b.shape[0] if tb else b.shape[1]
    assert (b.shape[1] if tb else b.shape[0]) == k, (a.shape, b.shape, ta, tb)
    bm, bn = _blk(m, MM_BM if ta else 2 * MM_BM), _blk(n, MM_BN)
    bk = k if k <= MM_K_WHOLE else _blk(k, MM_BK)
    nk = k // bk
    dims = (((0 if ta else 1,), (1 if tb else 0,)), ((), ()))
    has_scale = col_scale is not None
    use_acc = nk > 1 and (out_dtype != F32 or has_scale)

    def body(*refs):
        a_ref, b_ref = refs[0], refs[1]
        s_ref = refs[2] if has_scale else None
        o_ref = refs[3] if has_scale else refs[2]
        acc_ref = refs[-1] if use_acc else o_ref
        kk = pl.program_id(2)
        part = lax.dot_general(a_ref[...].astype(BF16), b_ref[...].astype(BF16), dims, preferred_element_type=F32)

        def finish(val):
            if has_scale:
                val = val * s_ref[...]
            o_ref[...] = val.astype(out_dtype)

        if nk == 1:
            finish(part)
            return

        @pl.when(kk == 0)
        def _():
            acc_ref[...] = part

        @pl.when(kk > 0)
        def _():
            acc_ref[...] += part

        if use_acc:
            @pl.when(kk == nk - 1)
            def _():
                finish(acc_ref[...])

    a_spec = pl.BlockSpec((bk, bm), lambda i, j, kk: (kk, i)) if ta else pl.BlockSpec((bm, bk), lambda i, j, kk: (i, kk))
    b_spec = pl.BlockSpec((bn, bk), lambda i, j, kk: (j, kk)) if tb else pl.BlockSpec((bk, bn), lambda i, j, kk: (kk, j))
    return pl.pallas_call(
        body, name=name, grid=(m // bm, n // bn, nk),
        in_specs=[a_spec, b_spec] + ([pl.BlockSpec((1, bn), lambda i, j, kk: (0, j))] if has_scale else []),
        out_specs=pl.BlockSpec((bm, bn), lambda i, j, kk: (i, j)),
        out_shape=jax.ShapeDtypeStruct((m, n), out_dtype),
        scratch_shapes=[pltpu.VMEM((bm, bn), F32)] if use_acc else [],
        compiler_params=_params("parallel", "parallel", "arbitrary"),
    )(*([a, b] + ([col_scale] if has_scale else [])))


def _norm_fwd(x, mul, add, *, plus_one, out_dtype, name):
    s, n = x.shape
    bs = _blk(s, ROW_BLOCK)

    def body(x_ref, m_ref, a_ref, o_ref):
        xv = x_ref[...]
        r = lax.rsqrt(jnp.mean(xv * xv, axis=-1, keepdims=True) + NORM_EPS)
        mv = m_ref[...] + 1.0 if plus_one else m_ref[...]
        o_ref[...] = (xv * r * mv + a_ref[...]).astype(out_dtype)

    row = pl.BlockSpec((bs, n), lambda i: (i, 0))
    vec = pl.BlockSpec((1, n), lambda i: (0, 0))
    return pl.pallas_call(
        body, name=name, grid=(s // bs,), in_specs=[row, vec, vec], out_specs=row,
        out_shape=jax.ShapeDtypeStruct((s, n), out_dtype), compiler_params=_params("parallel"),
    )(x, mul, add)


def _norm_bwd(x, mul, dy, dres, *, plus_one, name):
    s, n = x.shape
    bs = _blk(s, ROW_BLOCK)
    has_res = dres is not None

    def body(*refs):
        if has_res:
            x_ref, m_ref, dy_ref, dres_ref, dx_ref, dm_ref, da_ref = refs
        else:
            x_ref, m_ref, dy_ref, dx_ref, dm_ref, da_ref = refs
        xv = x_ref[...]
        dyv = dy_ref[...].astype(F32)
        r = lax.rsqrt(jnp.mean(xv * xv, axis=-1, keepdims=True) + NORM_EPS)
        xn = xv * r
        mv = m_ref[...] + 1.0 if plus_one else m_ref[...]
        g = dyv * mv
        dx = r * (g - xn * jnp.mean(g * xn, axis=-1, keepdims=True))
        if has_res:
            dx = dx + dres_ref[...]
        dx_ref[...] = dx

        @pl.when(pl.program_id(0) == 0)
        def _():
            dm_ref[...] = jnp.zeros_like(dm_ref)
            da_ref[...] = jnp.zeros_like(da_ref)

        dm_ref[...] += jnp.sum(dyv * xn, axis=0, keepdims=True)
        da_ref[...] += jnp.sum(dyv, axis=0, keepdims=True)

    row = pl.BlockSpec((bs, n), lambda i: (i, 0))
    vec = pl.BlockSpec((1, n), lambda i: (0, 0))
    ins = [x, mul, dy] + ([dres] if has_res else [])
    return pl.pallas_call(
        body, name=name, grid=(s // bs,),
        in_specs=[row, vec, row] + ([row] if has_res else []), out_specs=[row, vec, vec],
        out_shape=[jax.ShapeDtypeStruct((s, n), F32), jax.ShapeDtypeStruct((1, n), F32), jax.ShapeDtypeStruct((1, n), F32)],
        compiler_params=_params("arbitrary"),
    )(*ins)


def _resid_fwd(x, y, gate, *, name):
    s, n = x.shape
    bs = _blk(s, ROW_BLOCK)

    def body(x_ref, y_ref, g_ref, o_ref):
        o_ref[...] = x_ref[...] + g_ref[...] * y_ref[...]

    row = pl.BlockSpec((bs, n), lambda i: (i, 0))
    vec = pl.BlockSpec((1, n), lambda i: (0, 0))
    return pl.pallas_call(
        body, name=name, grid=(s // bs,), in_specs=[row, row, vec], out_specs=row,
        out_shape=jax.ShapeDtypeStruct((s, n), F32), compiler_params=_params("parallel"),
    )(x, y, gate)


def _resid_bwd(dx, y, gate, *, name):
    s, n = dx.shape
    bs = _blk(s, ROW_BLOCK)

    def body(dx_ref, y_ref, g_ref, dy_ref, dg_ref):
        dxv = dx_ref[...]
        dy_ref[...] = (g_ref[...] * dxv).astype(BF16)

        @pl.when(pl.program_id(0) == 0)
        def _():
            dg_ref[...] = jnp.zeros_like(dg_ref)

        dg_ref[...] += jnp.sum(dxv * y_ref[...], axis=0, keepdims=True)

    row = pl.BlockSpec((bs, n), lambda i: (i, 0))
    vec = pl.BlockSpec((1, n), lambda i: (0, 0))
    return pl.pallas_call(
        body, name=name, grid=(s // bs,), in_specs=[row, row, vec], out_specs=[row, vec],
        out_shape=[jax.ShapeDtypeStruct((s, n), BF16), jax.ShapeDtypeStruct((1, n), F32)],
        compiler_params=_params("arbitrary"),
    )(dx, y, gate)


def _final_loss(x, g, target, *, name):
    s, n = x.shape
    bs = _blk(s, ROW_BLOCK)

    def body(x_ref, g_ref, t_ref, loss_ref, dx_ref, dg_ref):
        xv = x_ref[...]
        r = lax.rsqrt(jnp.mean(xv * xv, axis=-1, keepdims=True) + NORM_EPS)
        xn = xv * r
        gv = g_ref[...]
        err = xn * gv - t_ref[...]
        dout = err * (1.0 / n)
        gg = dout * gv
        dx_ref[...] = r * (gg - xn * jnp.mean(gg * xn, axis=-1, keepdims=True))

        @pl.when(pl.program_id(0) == 0)
        def _():
            loss_ref[...] = jnp.zeros_like(loss_ref)
            dg_ref[...] = jnp.zeros_like(dg_ref)

        part = jnp.sum(jnp.sum(err * err, axis=-1, keepdims=True), axis=0, keepdims=True) * (0.5 / n)
        loss_ref[...] += jnp.broadcast_to(part, loss_ref.shape)
        dg_ref[...] += jnp.sum(dout * xn, axis=0, keepdims=True)

    row = pl.BlockSpec((bs, n), lambda i: (i, 0))
    vec = pl.BlockSpec((1, n), lambda i: (0, 0))
    return pl.pallas_call(
        body, name=name, grid=(s // bs,), in_specs=[row, vec, row],
        out_specs=[pl.BlockSpec((1, LANES), lambda i: (0, 0)), row, vec],
        out_shape=[jax.ShapeDtypeStruct((1, LANES), F32), jax.ShapeDtypeStruct((s, n), F32), jax.ShapeDtypeStruct((1, n), F32)],
        compiler_params=_params("arbitrary"),
    )(x, g, target)


def _lane_lt64(shape):
    return lax.broadcasted_iota(jnp.int32, shape, 1) < HEAD_DIM


def _keep_low(x):
    return jnp.where(_lane_lt64(x.shape), x.astype(F32), 0.0).astype(x.dtype)


def _keep_high(x):
    return jnp.where(_lane_lt64(x.shape), 0.0, x.astype(F32)).astype(x.dtype)


def _lane_merge(a, b):
    n = max(a.shape[0], b.shape[0])
    return jnp.where(_lane_lt64((n, LANES)), a, b)


def _pair(x, width, masked):
    if width == HEAD_DIM:
        return (_keep_low(x), _keep_high(x)) if masked else (x, x)
    return x[:, :LANES], x[:, LANES:]


def _qk_t(a, b):
    return lax.dot_general(a, b, (((1,), (1,)), ((), ())), preferred_element_type=F32)


def _attn_specs(s, blk, width, cols, resident):
    w = 2 * width
    if resident:
        return pl.BlockSpec((s, w), lambda p, i: (0, cols + p))
    return pl.BlockSpec((blk, w), lambda p, i: (i, cols + p))


BIAS_TERMS = 3


def _key_blocks(vt, blk):
    return vt.reshape(vt.shape[0], vt.shape[1] // blk, blk).transpose(1, 0, 2)


def _attn_fwd(q, k, vt, kbl, *, qc, kc, width, name):
    s = q.shape[0]
    blk = _blk(s, ATT_BLOCK)
    nb = s // blk
    has_bias = kbl is not None
    assert has_bias == (width == HEAD_DIM)

    def body(*refs):
        if has_bias:
            q_ref, k_ref, vt_ref, kbl_ref, o_ref, lse_ref = refs
        else:
            q_ref, k_ref, vt_ref, o_ref, lse_ref = refs
        i = pl.program_id(1)
        q2 = q_ref[...]
        if has_bias:
            lane = lax.broadcasted_iota(jnp.int32, (blk, LANES), 1)
            qf = q2.astype(F32)
            qh = (jnp.where(lane < HEAD_DIM, qf, jnp.where(lane < HEAD_DIM + BIAS_TERMS, 1.0, 0.0)).astype(BF16),
                  jnp.where(lane >= HEAD_DIM, qf, jnp.where(lane < BIAS_TERMS, 1.0, 0.0)).astype(BF16))
        else:
            qh = (q2[:, :LANES], q2[:, LANES:])

        def step(j, carry, nblk, diag):
            rows = pl.ds(pl.multiple_of(j * blk, blk), nblk * blk)
            vt1 = [jnp.concatenate([vt_ref[j + b], jnp.ones((16, blk), BF16)], axis=0) for b in range(nblk)]
            k2 = k_ref[rows, :]
            if has_bias:
                low = _lane_lt64(k2.shape)
                kf, bf = k2.astype(F32), kbl_ref[rows, :].astype(F32)
                kh = (jnp.where(low, kf, bf).astype(BF16), jnp.where(low, bf, kf).astype(BF16))
            else:
                kh = (k2[:, :LANES], k2[:, LANES:])
            out = []
            for hd in range(2):
                m, acc = carry[hd]
                st = _qk_t(kh[hd], qh[hd])
                if diag:
                    row = lax.broadcasted_iota(jnp.int32, (blk, blk), 0)
                    colq = lax.broadcasted_iota(jnp.int32, (blk, blk), 1)
                    st = jnp.where(row <= colq, st, -1e30)
                m_new = jnp.maximum(m, jnp.max(st, axis=0, keepdims=True))
                alpha = jnp.exp2(m - m_new)
                pt = jnp.exp2(st - m_new).astype(BF16)
                acc = alpha * acc
                for b in range(nblk):
                    acc = acc + jnp.dot(vt1[b], pt[b * blk:(b + 1) * blk], preferred_element_type=F32)
                out.append((m_new, acc))
            return tuple(out)

        one = (jnp.full((1, blk), -1e30, F32), jnp.zeros((LANES + 16, blk), F32))
        carry = lax.fori_loop(0, i // 2, lambda j, c: step(2 * j, c, 2, False), (one, one))
        carry = lax.fori_loop(0, i % 2, lambda _, c: step(i - 1, c, 1, False), carry)
        (ma, acca), (mb, accb) = step(i, carry, 1, True)
        la = jnp.max(acca[LANES:LANES + 8], axis=0, keepdims=True)
        lb = jnp.max(accb[LANES:LANES + 8], axis=0, keepdims=True)
        acca, accb = acca[0:LANES], accb[0:LANES]
        low = lax.broadcasted_iota(jnp.int32, (LANES, blk), 0) < HEAD_DIM
        o_ref[...] = jnp.where(low, acca / la, accb / lb).T
        lse_ref[0, 0] = ma + jnp.log(la) * LOG2E
        lse_ref[1, 0] = mb + jnp.log(lb) * LOG2E

    ins = [q, k, vt] + ([kbl] if has_bias else [])
    return pl.pallas_call(
        body, name=name, grid=(N_HEADS // 2, nb),
        in_specs=[_attn_specs(s, blk, width, qc, False), _attn_specs(s, blk, width, kc, True),
                  pl.BlockSpec((nb, LANES, blk), lambda p, i: (0, p, 0))]
                 + ([_attn_specs(s, blk, HEAD_DIM, 0, True)] if has_bias else []),
        out_specs=[pl.BlockSpec((blk, LANES), lambda p, i: (i, p)), pl.BlockSpec((2, 1, 1, blk), lambda p, i: (p, i, 0, 0))],
        out_shape=[jax.ShapeDtypeStruct((s, N_HEADS * HEAD_DIM), F32), jax.ShapeDtypeStruct((N_HEADS, nb, 1, blk), F32)],
        compiler_params=_params("parallel", "parallel"),
    )(*ins)


def _bias_lane_terms(kb2):
    s = kb2.shape[0]
    terms, rest = [], kb2
    for _ in range(BIAS_TERMS):
        t = lax.reduce_precision(rest, 8, 7)
        terms.append(t.astype(BF16))
        rest = rest - t
    t3 = jnp.stack(terms, axis=-1).reshape(s, N_HEADS // 2, 2, BIAS_TERMS)
    pad = jnp.zeros((s, N_HEADS // 2, HEAD_DIM - BIAS_TERMS), BF16)
    return jnp.concatenate([t3[:, :, 1], pad, t3[:, :, 0], pad], axis=-1).reshape(s, N_HEADS * HEAD_DIM)


def _causal_keep(n):
    row = lax.broadcasted_iota(jnp.int32, (n, n), 0)
    col = lax.broadcasted_iota(jnp.int32, (n, n), 1)
    return col <= row


def _attn_delta(o, do, *, name):
    s, n = o.shape
    bs = _blk(s, ROW_BLOCK)

    def body(o_ref, do_ref, d_ref):
        for g in range(n // LANES):
            prod = do_ref[:, g * LANES:(g + 1) * LANES].astype(F32) * o_ref[:, g * LANES:(g + 1) * LANES]
            low = _lane_lt64(prod.shape)
            d_ref[:, g * LANES:(g + 1) * LANES] = _lane_merge(
                jnp.sum(jnp.where(low, prod, 0.0), axis=-1, keepdims=True),
                jnp.sum(jnp.where(low, 0.0, prod), axis=-1, keepdims=True))

    row = pl.BlockSpec((bs, n), lambda i: (i, 0))
    return pl.pallas_call(
        body, name=name, grid=(s // bs,), in_specs=[row, row], out_specs=row,
        out_shape=jax.ShapeDtypeStruct((s, n), F32), compiler_params=_params("parallel"),
    )(o, do)


def _attn_bwd(q, k, v, kb_col, do, lse_row, delta_row, *, qc, kc, vc, width, dq_mult, dk_mult, out_dtype, name):
    s = q.shape[0]
    blk = _blk(s, ATT_BLOCK)
    nb = s // blk
    has_bias = kb_col is not None

    def body(*refs):
        if has_bias:
            q_ref, k_ref, v_ref, kb_ref, do_ref, lse_ref, dl_ref, dk_ref, dv_ref, db_ref, dq_ref, dr_ref = refs
        else:
            q_ref, k_ref, v_ref, do_ref, lse_ref, dl_ref, dk_ref, dv_ref, db_ref, dq_ref = refs
        j = pl.program_id(1)

        @pl.when(j == 0)
        def _():
            dq_ref[...] = jnp.zeros_like(dq_ref)
            if has_bias:
                dr_ref[...] = jnp.zeros_like(dr_ref)

        kh = _pair(k_ref[...], width, True)
        v2 = v_ref[...]
        vh = (_keep_low(v2), _keep_high(v2))
        if has_bias:
            kb2 = kb_ref[0]
            kbh = (kb2[:, 0:1], kb2[:, 1:2])

        def step(i, carry, nblk, diag):
            rows = pl.ds(pl.multiple_of(i * blk, blk), nblk * blk)
            qh = _pair(q_ref[rows, :], width, False)
            doi = do_ref[rows, :]
            out, dq_parts = [], []
            for hd in range(2):
                dk, dvv, db = carry[hd]
                st = _qk_t(kh[hd], qh[hd])
                if has_bias:
                    st = st + kbh[hd]
                if diag:
                    row = lax.broadcasted_iota(jnp.int32, (blk, blk), 0)
                    colq = lax.broadcasted_iota(jnp.int32, (blk, blk), 1)
                    st = jnp.where(row <= colq, st, -1e30)
                lse_i = jnp.concatenate([lse_ref[hd, i + b] for b in range(nblk)], axis=1)
                delta_i = jnp.concatenate([dl_ref[hd, i + b] for b in range(nblk)], axis=1)
                pt = jnp.exp2(st - lse_i)
                dvv = dvv + jnp.dot(pt.astype(BF16), doi, preferred_element_type=F32)
                dst = pt * (_qk_t(vh[hd], doi) - delta_i)
                dsb = dst.astype(BF16)
                dk = dk + jnp.dot(dsb, qh[hd], preferred_element_type=F32)
                db = db + jnp.sum(dst, axis=-1, keepdims=True)
                dq_parts.append(lax.dot_general(dsb, kh[hd], (((0,), (0,)), ((), ())), preferred_element_type=F32))
                if has_bias:
                    rsum = jnp.sum(dst, axis=0, keepdims=True)
                    for b in range(nblk):
                        dr_ref[hd, i + b] += rsum[:, b * blk:(b + 1) * blk]
                out.append((dk, dvv, db))
            if width == HEAD_DIM:
                dq_ref[rows, :] += (dq_parts[0] + dq_parts[1]) * dq_mult
            else:
                dq_ref[rows, 0:LANES] += dq_parts[0] * dq_mult
                dq_ref[rows, LANES:2 * LANES] += dq_parts[1] * dq_mult
            return tuple(out)

        one = (jnp.zeros((blk, LANES), F32), jnp.zeros((blk, LANES), F32), jnp.zeros((blk, 1), F32))
        carry = step(j, (one, one), 1, True)
        rest = nb - 1 - j
        carry = lax.fori_loop(0, rest // 2, lambda t, c: step(j + 1 + 2 * t, c, 2, False), carry)
        (dka, dva, dba), (dkb, dvb, dbb) = lax.fori_loop(0, rest % 2, lambda _, c: step(nb - 1, c, 1, False), carry)
        if width == HEAD_DIM:
            dk = _lane_merge(dka, dkb)
        else:
            dk = jnp.concatenate([dka, dkb], axis=1)
        dk_ref[...] = (dk * dk_mult).astype(out_dtype)
        dv_ref[...] = _lane_merge(dva, dvb).astype(out_dtype)
        db_ref[...] = _lane_merge(dba, dbb)

    stat = pl.BlockSpec((blk, LANES), lambda p, jj: (jj, p))
    rows = pl.BlockSpec((2, nb, 1, blk), lambda p, jj: (p, 0, 0, 0))
    ins = [q, k, v] + ([kb_col] if has_bias else []) + [do, lse_row, delta_row]
    return pl.pallas_call(
        body, name=name, grid=(N_HEADS // 2, nb),
        in_specs=[_attn_specs(s, blk, width, qc, True), _attn_specs(s, blk, width, kc, False),
                  _attn_specs(s, blk, HEAD_DIM, vc, False)]
                 + ([pl.BlockSpec((1, blk, 2), lambda p, jj: (p, jj, 0))] if has_bias else [])
                 + [pl.BlockSpec((s, LANES), lambda p, jj: (0, p)), rows, rows],
        out_specs=[pl.BlockSpec((blk, 2 * width), lambda p, jj: (jj, p)), stat, stat,
                   pl.BlockSpec((s, 2 * width), lambda p, jj: (0, p))] + ([rows] if has_bias else []),
        out_shape=[jax.ShapeDtypeStruct((s, N_HEADS * width), out_dtype), jax.ShapeDtypeStruct((s, N_HEADS * HEAD_DIM), out_dtype),
                   jax.ShapeDtypeStruct((s, N_HEADS * HEAD_DIM), F32), jax.ShapeDtypeStruct((s, N_HEADS * width), F32)]
                  + ([jax.ShapeDtypeStruct((N_HEADS, nb, 1, blk), F32)] if has_bias else []),
        compiler_params=_params("parallel", "arbitrary"),
    )(*ins)


def _head_stat(t):
    return t[:, ::HEAD_DIM]


def _stat_rows(t16, blk):
    s = t16.shape[0]
    return t16.T.reshape(N_HEADS, s // blk, 1, blk)


def _attention_bwd(res, do, *, qc, kc, vc, width, dq_mult, dk_mult, out_dtype, name):
    q, k, v, bias, o, lse_row = res
    s = q.shape[0]
    blk = _blk(s, ATT_BLOCK)
    kb_col = None if bias is None else bias.reshape(s, N_HEADS // 2, 2).transpose(1, 0, 2)
    delta_row = _stat_rows(_head_stat(_attn_delta(o, do, name=name + "_delta")), blk)
    outs = _attn_bwd(q, k, v, kb_col, do, lse_row, delta_row, qc=qc, kc=kc, vc=vc, width=width, dq_mult=dq_mult,
                     dk_mult=dk_mult, out_dtype=out_dtype, name=name + "_bwd")
    dk, dv, dcol, dq = outs[:4]
    if bias is None:
        return dq, dk, dv, None
    return dq, dk, dv, outs[4].reshape(N_HEADS, s).T - _head_stat(dcol)


def _fox_gate_fwd(fl, bf, *, name):
    s, n = fl.shape
    bs = _blk(s, ROW_BLOCK)

    def body(fl_ref, bf_ref, cum_ref, carry_ref):
        @pl.when(pl.program_id(0) == 0)
        def _():
            carry_ref[...] = jnp.zeros_like(carry_ref)

        z = fl_ref[...] + bf_ref[...]
        lf = jnp.minimum(z, 0.0) - jnp.log1p(jnp.exp(-jnp.abs(z)))
        row = lax.broadcasted_iota(jnp.int32, (bs, bs), 0)
        col = lax.broadcasted_iota(jnp.int32, (bs, bs), 1)
        tri = (col <= row).astype(F32)
        cum_ref[...] = jnp.dot(tri, lf, preferred_element_type=F32, precision=HIGHEST) + carry_ref[...]
        carry_ref[...] += jnp.sum(lf, axis=0, keepdims=True)

    return pl.pallas_call(
        body, name=name, grid=(s // bs,),
        in_specs=[pl.BlockSpec((bs, n), lambda i: (i, 0)), pl.BlockSpec((1, n), lambda i: (0, 0))],
        out_specs=pl.BlockSpec((bs, n), lambda i: (i, 0)),
        out_shape=jax.ShapeDtypeStruct((s, n), F32), scratch_shapes=[pltpu.VMEM((1, n), F32)],
        compiler_params=_params("arbitrary"),
    )(fl, bf)


def _fox_gate_bwd(fl, bf, dcum, *, name):
    s, n = fl.shape
    bs = _blk(s, ROW_BLOCK)
    nb = s // bs

    def body(fl_ref, bf_ref, dc_ref, dz_ref, dbf_ref, carry_ref):
        @pl.when(pl.program_id(0) == 0)
        def _():
            carry_ref[...] = jnp.zeros_like(carry_ref)
            dbf_ref[...] = jnp.zeros_like(dbf_ref)

        dc = dc_ref[...]
        row = lax.broadcasted_iota(jnp.int32, (bs, bs), 0)
        col = lax.broadcasted_iota(jnp.int32, (bs, bs), 1)
        tri = (col >= row).astype(F32)
        dlf = jnp.dot(tri, dc, preferred_element_type=F32, precision=HIGHEST) + carry_ref[...]
        carry_ref[...] += jnp.sum(dc, axis=0, keepdims=True)
        z = fl_ref[...] + bf_ref[...]
        dz = dlf / (1.0 + jnp.exp(z))
        dz_ref[...] = dz
        dbf_ref[...] += jnp.sum(dz, axis=0, keepdims=True)

    rev = pl.BlockSpec((bs, n), lambda i: (nb - 1 - i, 0))
    vec = pl.BlockSpec((1, n), lambda i: (0, 0))
    return pl.pallas_call(
        body, name=name, grid=(nb,), in_specs=[rev, vec, rev], out_specs=[rev, vec],
        out_shape=[jax.ShapeDtypeStruct((s, n), F32), jax.ShapeDtypeStruct((1, n), F32)],
        scratch_shapes=[pltpu.VMEM((1, n), F32)], compiler_params=_params("arbitrary"),
    )(fl, bf, dcum)


def _rope(x1, x2, cos, sin, *, negate, name):
    s, n = x1.shape
    bs = _blk(s, ROW_BLOCK)

    def body(a_ref, b_ref, c_ref, s_ref, o1_ref, o2_ref):
        a, b, cv = a_ref[...], b_ref[...], c_ref[...]
        sv = -s_ref[...] if negate else s_ref[...]
        o1_ref[...] = a * cv - b * sv
        o2_ref[...] = b * cv + a * sv

    row = pl.BlockSpec((bs, n), lambda i: (i, 0))
    return pl.pallas_call(
        body, name=name, grid=(s // bs,), in_specs=[row] * 4, out_specs=[row, row],
        out_shape=[jax.ShapeDtypeStruct((s, n), F32)] * 2, compiler_params=_params("parallel"),
    )(x1, x2, cos, sin)


def _rope_heads(x, ta, tb, tc, *, out_dtype, name):
    s, n = x.shape
    bs = _blk(s, ROW_BLOCK)

    def body(x_ref, a_ref, b_ref, c_ref, o_ref):
        av, bv, cv = a_ref[...], b_ref[...], c_ref[...]
        for g in range(n // LANES):
            xg = x_ref[:, g * LANES:(g + 1) * LANES]
            og = xg * av + pltpu.roll(xg, LANES - MLA_ROPE_HALF, 1) * bv + pltpu.roll(xg, MLA_ROPE_HALF, 1) * cv
            o_ref[:, g * LANES:(g + 1) * LANES] = og.astype(out_dtype)

    row = pl.BlockSpec((bs, n), lambda i: (i, 0))
    tab = pl.BlockSpec((bs, LANES), lambda i: (i, 0))
    return pl.pallas_call(
        body, name=name, grid=(s // bs,), in_specs=[row, tab, tab, tab], out_specs=row,
        out_shape=jax.ShapeDtypeStruct((s, n), out_dtype), compiler_params=_params("parallel"),
    )(x, ta, tb, tc)


def _group_sum(x, *, name):
    s, n = x.shape
    bs = _blk(s, ROW_BLOCK)

    def body(x_ref, o_ref):
        acc = x_ref[:, 0:LANES]
        for g in range(1, n // LANES):
            acc = acc + x_ref[:, g * LANES:(g + 1) * LANES]
        o_ref[...] = acc

    return pl.pallas_call(
        body, name=name, grid=(s // bs,), in_specs=[pl.BlockSpec((bs, n), lambda i: (i, 0))],
        out_specs=pl.BlockSpec((bs, LANES), lambda i: (i, 0)),
        out_shape=jax.ShapeDtypeStruct((s, LANES), F32), compiler_params=_params("parallel"),
    )(x)


def _shift_down(x, k):
    return pltpu.roll(x, k, 0)


def _conv_rows(ext, w_ref, b_ref, rows):
    y = b_ref[...] + w_ref[0:1, :] * _shift_down(ext, 2) + w_ref[1:2, :] * _shift_down(ext, 1) + w_ref[2:3, :] * ext
    return y[8:8 + rows]


def _conv_gate_fwd(u, cw, cb, *, name):
    s, f2 = u.shape
    f = f2 // 2
    nf = f // LANES
    r = _blk(s, CONV_ROWS)
    r8 = r // 8

    def body(ug_ref, ugp_ref, uv_ref, uvp_ref, wg_ref, wv_ref, bg_ref, bv_ref, o_ref):
        first = pl.program_id(1) == 0

        def conv(cur_ref, prev_ref, w_ref, b_ref):
            prev = jnp.where(first, 0.0, prev_ref[...])
            return _conv_rows(jnp.concatenate([prev, cur_ref[...]], axis=0), w_ref, b_ref, r)

        yg = conv(ug_ref, ugp_ref, wg_ref, bg_ref)
        yv = conv(uv_ref, uvp_ref, wv_ref, bv_ref)
        o_ref[...] = (yg * jax.nn.sigmoid(yg) * yv).astype(BF16)

    def cur(off):
        return pl.BlockSpec((r, LANES), lambda c, i: (i, c + off))

    def prev(off):
        return pl.BlockSpec((8, LANES), lambda c, i: (jnp.maximum(i * r8 - 1, 0), c + off))

    def wspec(rows, off):
        return pl.BlockSpec((rows, LANES), lambda c, i: (0, c + off))

    return pl.pallas_call(
        body, name=name, grid=(nf, s // r),
        in_specs=[cur(0), prev(0), cur(nf), prev(nf), wspec(3, 0), wspec(3, nf), wspec(1, 0), wspec(1, nf)],
        out_specs=pl.BlockSpec((r, LANES), lambda c, i: (i, c)),
        out_shape=jax.ShapeDtypeStruct((s, f), BF16), compiler_params=_params("parallel", "parallel"),
    )(u, u, u, u, cw, cw, cb, cb)


def _conv_gate_bwd(u, cw, cb, dg, *, name):
    s, f2 = u.shape
    f = f2 // 2
    nf = f // LANES
    r = _blk(s, CONV_ROWS)
    r8 = r // 8
    nr = s // r

    def body(ug_ref, ugp_ref, ugn_ref, uv_ref, uvp_ref, uvn_ref, wg_ref, wv_ref, bg_ref, bv_ref, dg_ref, dgn_ref,
             dug_ref, duv_ref, dwg_ref, dwv_ref, dbg_ref, dbv_ref):
        i = pl.program_id(1)
        first, last = i == 0, i == nr - 1

        def ext_of(cur_ref, prev_ref, next_ref):
            prev = jnp.where(first, 0.0, prev_ref[...])
            return jnp.concatenate([prev, cur_ref[...], next_ref[...]], axis=0)

        eg, ev = ext_of(ug_ref, ugp_ref, ugn_ref), ext_of(uv_ref, uvp_ref, uvn_ref)
        yg = _conv_rows(eg, wg_ref, bg_ref, r + 8)
        yv = _conv_rows(ev, wv_ref, bv_ref, r + 8)
        dgn = jnp.where(last, 0.0, dgn_ref[...])
        dgx = jnp.concatenate([dg_ref[...], dgn], axis=0)
        sg = jax.nn.sigmoid(yg)
        dyg = dgx * yv * (sg * (1.0 + yg * (1.0 - sg)))
        dyv = dgx * (yg * sg)

        @pl.when(i == 0)
        def _():
            for ref in (dwg_ref, dwv_ref, dbg_ref, dbv_ref):
                ref[...] = jnp.zeros_like(ref)

        def grads(dy, ext, w_ref, du_ref, dw_ref, db_ref):
            n = r + 8
            du = w_ref[2:3, :] * dy + w_ref[1:2, :] * pltpu.roll(dy, n - 1, 0) + w_ref[0:1, :] * pltpu.roll(dy, n - 2, 0)
            du_ref[...] = du[0:r].astype(BF16)
            dyc = dy[0:r]
            db_ref[...] += jnp.sum(dyc, axis=0, keepdims=True)
            ext_c = ext[0:r + 8]
            dw_ref[0:1, :] += jnp.sum(dyc * _shift_down(ext_c, 2)[8:], axis=0, keepdims=True)
            dw_ref[1:2, :] += jnp.sum(dyc * _shift_down(ext_c, 1)[8:], axis=0, keepdims=True)
            dw_ref[2:3, :] += jnp.sum(dyc * ext_c[8:], axis=0, keepdims=True)

        grads(dyg, eg, wg_ref, dug_ref, dwg_ref, dbg_ref)
        grads(dyv, ev, wv_ref, duv_ref, dwv_ref, dbv_ref)

    def cur(off):
        return pl.BlockSpec((r, LANES), lambda c, i: (i, c + off))

    def prev(off):
        return pl.BlockSpec((8, LANES), lambda c, i: (jnp.maximum(i * r8 - 1, 0), c + off))

    def nxt(off):
        return pl.BlockSpec((8, LANES), lambda c, i: (jnp.minimum((i + 1) * r8, s // 8 - 1), c + off))

    def wspec(rows, off):
        return pl.BlockSpec((rows, LANES), lambda c, i: (0, c + off))

    outs = pl.pallas_call(
        body, name=name, grid=(nf, nr),
        in_specs=[cur(0), prev(0), nxt(0), cur(nf), prev(nf), nxt(nf), wspec(3, 0), wspec(3, nf), wspec(1, 0), wspec(1, nf),
                  cur(0), nxt(0)],
        out_specs=[cur(0), cur(0), wspec(3, 0), wspec(3, 0), wspec(1, 0), wspec(1, 0)],
        out_shape=[jax.ShapeDtypeStruct((s, f), BF16), jax.ShapeDtypeStruct((s, f), BF16),
                   jax.ShapeDtypeStruct((3, f), F32), jax.ShapeDtypeStruct((3, f), F32),
                   jax.ShapeDtypeStruct((1, f), F32), jax.ShapeDtypeStruct((1, f), F32)],
        compiler_params=_params("parallel", "arbitrary"),
    )(u, u, u, u, u, u, cw, cw, cb, cb, dg, dg)
    dug, duv, dwg, dwv, dbg, dbv = outs
    return jnp.concatenate([dug, duv], axis=1), jnp.concatenate([dwg, dwv], axis=1), jnp.concatenate([dbg, dbv], axis=1)


def _adamw(w, g, m, v, *, slabs, name):
    shape = w.shape
    cols = shape[-1]
    rows = w.size // cols
    w2, m2, v2 = (t.reshape(rows, cols) for t in (w, m, v))
    g2 = g.reshape((N_DEV, rows, cols) if slabs else (rows, cols))
    br = _row_blk(rows, ROW_BLOCK // 2 if slabs else ROW_BLOCK)

    def body(w_ref, g_ref, m_ref, v_ref, go_ref, d_ref, nm_ref, nv_ref):
        if slabs:
            gv = g_ref[0].astype(F32)
            for p in range(1, N_DEV):
                gv = gv + g_ref[p].astype(F32)
        else:
            gv = g_ref[...]
        nm = ADAM_B1 * m_ref[...] + (1.0 - ADAM_B1) * gv
        nv = ADAM_B2 * v_ref[...] + (1.0 - ADAM_B2) * (gv * gv)
        m_hat = nm / (1.0 - ADAM_B1 ** ADAM_STEP)
        v_hat = nv / (1.0 - ADAM_B2 ** ADAM_STEP)
        go_ref[...] = gv
        d_ref[...] = -ADAM_LR * (m_hat / (jnp.sqrt(v_hat) + ADAM_EPS) + ADAM_WD * w_ref[...])
        nm_ref[...] = nm
        nv_ref[...] = nv

    spec = pl.BlockSpec((br, cols), lambda i: (i, 0))
    gspec = pl.BlockSpec((N_DEV, br, cols), lambda i: (0, i, 0)) if slabs else spec
    outs = pl.pallas_call(
        body, name=name, grid=(rows // br,), in_specs=[spec, gspec, spec, spec], out_specs=[spec] * 4,
        out_shape=[jax.ShapeDtypeStruct((rows, cols), F32)] * 4, compiler_params=_params("parallel"),
    )(w2, g2, m2, v2)
    return tuple(t.reshape(shape) for t in outs)


def _exchange(xs, *, same_src, name):
    n = len(xs)
    slabs = [x.shape if same_src else x.shape[1:] for x in xs]

    def body(*refs):
        x_refs, o_refs = refs[:n], refs[n:2 * n]
        send_sems, recv_sems, loc_sems = refs[2 * n:]
        ix, iy, ic = lax.axis_index("x"), lax.axis_index("y"), lax.axis_index("c")
        me = 4 * ix + 2 * iy + ic
        local, sends, recvs = [], [], []
        for a in range(n):
            def src(p, a=a):
                return x_refs[a] if same_src else x_refs[a].at[p]

            local.append(pltpu.make_async_copy(src(me), o_refs[a].at[me], loc_sems.at[a]))
            for k in (1, 2, 4, 3, 5, 6, 7):
                px = 1 - ix if k & 4 else ix
                py = 1 - iy if k & 2 else iy
                pc = 1 - ic if k & 1 else ic
                p = 4 * px + 2 * py + pc
                for dst, out in ((me, sends), (p, recvs)):
                    out.append(pltpu.make_async_remote_copy(
                        src_ref=src(p), dst_ref=o_refs[a].at[dst], send_sem=send_sems.at[a, k - 1],
                        recv_sem=recv_sems.at[a, k - 1], device_id=(px, py, pc), device_id_type=pl.DeviceIdType.MESH))
        for cp in local + sends:
            cp.start()
        for cp in recvs:
            cp.wait_recv()
        for cp in sends:
            cp.wait_send()
        for cp in local:
            cp.wait()

    return pl.pallas_call(
        body, name=name,
        in_specs=[pl.BlockSpec(memory_space=pl.ANY)] * n, out_specs=[pl.BlockSpec(memory_space=pl.ANY)] * n,
        out_shape=[jax.ShapeDtypeStruct((N_DEV,) + tuple(sl), x.dtype) for sl, x in zip(slabs, xs)],
        scratch_shapes=[pltpu.SemaphoreType.DMA((n, N_DEV - 1)), pltpu.SemaphoreType.DMA((n, N_DEV - 1)),
                        pltpu.SemaphoreType.DMA((n,))],
        compiler_params=pltpu.CompilerParams(has_side_effects=True, vmem_limit_bytes=VMEM_LIMIT_BYTES),
    )(*xs)


def _sum_slabs(x, *, name):
    n, r, c = x.shape
    br = _row_blk(r, ROW_BLOCK)

    def body(x_ref, o_ref):
        acc = x_ref[0]
        for p in range(1, n):
            acc = acc + x_ref[p]
        o_ref[...] = acc

    return pl.pallas_call(
        body, name=name, grid=(r // br,), in_specs=[pl.BlockSpec((n, br, c), lambda i: (0, i, 0))],
        out_specs=pl.BlockSpec((br, c), lambda i: (i, 0)),
        out_shape=jax.ShapeDtypeStruct((r, c), F32), compiler_params=_params("parallel"),
    )(x)


def _silu(x, *, name):
    def body(x_ref, o_ref):
        xv = x_ref[...]
        o_ref[...] = (xv * jax.nn.sigmoid(xv)).astype(BF16)

    return pl.pallas_call(body, name=name, out_shape=jax.ShapeDtypeStruct(x.shape, BF16),
                          compiler_params=_params())(x)


_BIG = {"fox_w_in": 2, "fox_w_o": 1, "mla_w_a": 1, "mla_w_uq": 2, "mla_w_ukv": 2, "mla_w_o": 1, "ffn_w_in": 2, "ffn_w_out": 1}
_SMALL = {"mla_g_q": 1, "mla_g_kv": 1, "ffn_conv_w": 2}
_REPL = ("fox_b_f", "ffn_conv_b", "final_g")


def _gathered_to_full(g, axis):
    full = jnp.moveaxis(g, 0, axis)
    shape = list(full.shape)
    shape[axis:axis + 2] = [shape[axis] * shape[axis + 1]]
    return full.reshape(shape)


def _full_to_chunks(full, axis):
    shape = list(full.shape)
    shape[axis:axis + 1] = [N_DEV, shape[axis] // N_DEV]
    return jnp.moveaxis(full.reshape(shape), axis, 0)


def _per_head(parts, s_or_rows):
    return jnp.concatenate([p.reshape(s_or_rows, N_HEADS, -1) for p in parts], axis=-1).reshape(s_or_rows, -1)


def kernel(x, c, ada_w, ada_b, fox_w_in, fox_b_f, fox_w_o, mla_w_a, mla_g_q, mla_g_kv, mla_w_uq, mla_w_ukv, mla_w_o, ffn_w_in, ffn_conv_w, ffn_conv_b, ffn_w_out, final_g, loss_target, m_ada_w, m_ada_b, m_fox_w_in, m_fox_b_f, m_fox_w_o, m_mla_w_a, m_mla_g_q, m_mla_g_kv, m_mla_w_uq, m_mla_w_ukv, m_mla_w_o, m_ffn_w_in, m_ffn_conv_w, m_ffn_conv_b, m_ffn_w_out, m_final_g, v_ada_w, v_ada_b, v_fox_w_in, v_fox_b_f, v_fox_w_o, v_mla_w_a, v_mla_g_q, v_mla_g_kv, v_mla_w_uq, v_mla_w_ukv, v_mla_w_o, v_ffn_w_in, v_ffn_conv_w, v_ffn_conv_b, v_ffn_w_out, v_final_g):
    weights = dict(ada_w=ada_w, ada_b=ada_b, fox_w_in=fox_w_in, fox_b_f=fox_b_f, fox_w_o=fox_w_o, mla_w_a=mla_w_a,
                   mla_g_q=mla_g_q, mla_g_kv=mla_g_kv, mla_w_uq=mla_w_uq, mla_w_ukv=mla_w_ukv, mla_w_o=mla_w_o,
                   ffn_w_in=ffn_w_in, ffn_conv_w=ffn_conv_w, ffn_conv_b=ffn_conv_b, ffn_w_out=ffn_w_out, final_g=final_g)
    mom_m = dict(ada_w=m_ada_w, ada_b=m_ada_b, fox_w_in=m_fox_w_in, fox_b_f=m_fox_b_f, fox_w_o=m_fox_w_o, mla_w_a=m_mla_w_a,
                 mla_g_q=m_mla_g_q, mla_g_kv=m_mla_g_kv, mla_w_uq=m_mla_w_uq, mla_w_ukv=m_mla_w_ukv, mla_w_o=m_mla_w_o,
                 ffn_w_in=m_ffn_w_in, ffn_conv_w=m_ffn_conv_w, ffn_conv_b=m_ffn_conv_b, ffn_w_out=m_ffn_w_out, final_g=m_final_g)
    mom_v = dict(ada_w=v_ada_w, ada_b=v_ada_b, fox_w_in=v_fox_w_in, fox_b_f=v_fox_b_f, fox_w_o=v_fox_w_o, mla_w_a=v_mla_w_a,
                 mla_g_q=v_mla_g_q, mla_g_kv=v_mla_g_kv, mla_w_uq=v_mla_w_uq, mla_w_ukv=v_mla_w_ukv, mla_w_o=v_mla_w_o,
                 ffn_w_in=v_ffn_w_in, ffn_conv_w=v_ffn_conv_w, ffn_conv_b=v_ffn_conv_b, ffn_w_out=v_ffn_w_out, final_g=v_final_g)
    order = list(weights)
    x0 = x[0]
    target = loss_target[0]
    s = x0.shape[0]
    d = D_MODEL
    cols = ada_w.shape[-1]
    nq = N_HEADS * HEAD_DIM

    small_names = ["c"] + list(_SMALL)
    small_all = dict(zip(small_names, _exchange([c] + [weights[n] for n in _SMALL], same_src=True, name="gather_small")))
    c_all = small_all["c"].reshape(N_DEV, d)
    g_q = _gathered_to_full(small_all["mla_g_q"], 1)
    g_kv = _gathered_to_full(small_all["mla_g_kv"], 1)
    conv_w = _gathered_to_full(small_all["ffn_conv_w"], 2)

    c_pad = _pad_axis(c_all, 0, LANES)
    silu_c = _silu(c_pad, name="silu_c")
    w_ada = ada_w.reshape(4, d, cols)
    b_ada = ada_b.reshape(4, 1, cols)
    mods = [_matmul(silu_c, w_ada[i], name=f"ada_mod{i}")[:N_DEV] + b_ada[i] for i in range(4)]
    mod_send = _pad_axis(jnp.stack(mods, axis=1), 1, 8)
    mod_recv, = _exchange([mod_send], same_src=False, name="scatter_mod")
    mod = mod_recv[:, :4].transpose(1, 0, 2).reshape(4, 3 * d)
    shift = [mod[i:i + 1, 0:d] for i in range(4)]
    scale = [mod[i:i + 1, d:2 * d] for i in range(4)]
    gate = [mod[i:i + 1, 2 * d:3 * d] for i in range(4)]

    big_all = _exchange([weights[n].astype(BF16) for n in _BIG], same_src=True, name="gather_weights")
    wfull = {n: _gathered_to_full(g, _BIG[n]) for n, g in zip(_BIG, big_all)}

    w_fox_in = _pad_axis(wfull["fox_w_in"][0], 1, LANES)
    w_fox_qkv, w_fox_f = w_fox_in[:, :3 * nq], w_fox_in[:, 3 * nq:]
    w_fox_o = wfull["fox_w_o"][0]
    w_a = _pad_axis(wfull["mla_w_a"][0], 1, LANES)
    wq = wfull["mla_w_uq"][0].reshape(MLA_Q_RANK, N_HEADS, HEAD_DIM + MLA_ROPE_DIM)
    w_uq = _pad_axis(wq, 2, LANES).reshape(MLA_Q_RANK, N_HEADS * LANES)
    wkv = wfull["mla_w_ukv"][0].reshape(MLA_KV_RANK, N_HEADS, 2 * HEAD_DIM)
    w_ukv = jnp.concatenate([wkv[:, :, :HEAD_DIM].reshape(MLA_KV_RANK, -1), wkv[:, :, HEAD_DIM:].reshape(MLA_KV_RANK, -1)], axis=1)
    w_mla_o = wfull["mla_w_o"][0]
    w_ffn_in = wfull["ffn_w_in"]
    w_ffn_out = wfull["ffn_w_out"]
    conv_b = ffn_conv_b

    fox_scale = HEAD_DIM ** -0.5
    mla_scale = (HEAD_DIM + MLA_ROPE_DIM) ** -0.5
    pos = jnp.arange(s, dtype=F32)
    inv_freq = ROPE_BASE ** (-jnp.arange(0, MLA_ROPE_DIM, 2, dtype=F32) / MLA_ROPE_DIM)
    ang = pos[:, None] * inv_freq[None, :]
    cos16, sin16 = jnp.cos(ang), jnp.sin(ang)
    z16, z32, z64 = jnp.zeros((s, 16), F32), jnp.zeros((s, 32), F32), jnp.zeros((s, 64), F32)
    tab_a = jnp.concatenate([jnp.ones((s, 64), F32), cos16, cos16, z32], axis=1) * (mla_scale * LOG2E)
    tab_b = jnp.concatenate([z64, -sin16, z16, z32], axis=1) * (mla_scale * LOG2E)
    tab_c = jnp.concatenate([z64, z16, sin16, z32], axis=1) * (mla_scale * LOG2E)

    h0 = _norm_fwd(x0, scale[0], shift[0], plus_one=True, out_dtype=BF16, name="ada_fwd0")
    q_mult = jnp.concatenate([jnp.full((1, nq), fox_scale * LOG2E, F32), jnp.ones((1, 2 * nq), F32)], axis=1)
    qkv = _matmul(h0, w_fox_qkv, out_dtype=BF16, col_scale=q_mult, name="fox_proj")
    fl = _matmul(h0, w_fox_f, name="fox_proj_f")[:, :N_HEADS]
    cum = _fox_gate_fwd(fl, fox_b_f, name="fox_gate_fwd")
    fox_cfg = dict(qc=0, kc=N_HEADS // 2, vc=N_HEADS, width=HEAD_DIM)
    kb2 = cum * -LOG2E
    att_blk = _blk(s, ATT_BLOCK)
    fvt = _matmul(w_fox_qkv[:, 2 * nq:], h0, ta=True, tb=True, out_dtype=BF16, name="fox_proj_vt")
    fo, fox_lse = _attn_fwd(qkv, qkv, _key_blocks(fvt, att_blk), _bias_lane_terms(kb2), qc=0, kc=N_HEADS // 2,
                            width=HEAD_DIM, name="fox_attn_fwd")
    fox_res = (qkv, qkv, qkv, kb2, fo, fox_lse)
    y0 = _matmul(fo, w_fox_o, name="fox_out")
    x1 = _resid_fwd(x0, y0, gate[0], name="resid_fwd0")

    def ffn_fwd(xin, li, sub):
        hh = _norm_fwd(xin, scale[sub], shift[sub], plus_one=True, out_dtype=BF16, name=f"ada_fwd{sub}")
        u = _matmul(hh, w_ffn_in[li], name=f"ffn_up{li}")
        g = _conv_gate_fwd(u, conv_w[li], conv_b[li:li + 1], name=f"conv_fwd{li}")
        y = _matmul(g, w_ffn_out[li], name=f"ffn_down{li}")
        return _resid_fwd(xin, y, gate[sub], name=f"resid_fwd{sub}"), (hh, u, g, y)

    x2, ffn0_res = ffn_fwd(x1, 0, 1)

    h2 = _norm_fwd(x2, scale[2], shift[2], plus_one=True, out_dtype=BF16, name="ada_fwd2")
    a = _matmul(h2, w_a, name="mla_a")
    a_q, a_kv = a[:, :MLA_Q_RANK], a[:, MLA_Q_RANK:MLA_Q_RANK + MLA_KV_RANK]
    kr1 = a[:, MLA_Q_RANK + MLA_KV_RANK:MLA_Q_RANK + MLA_KV_RANK + MLA_ROPE_HALF]
    kr2 = a[:, MLA_Q_RANK + MLA_KV_RANK + MLA_ROPE_HALF:MLA_Q_RANK + MLA_KV_RANK + MLA_ROPE_DIM]
    cq = _norm_fwd(a_q, g_q, jnp.zeros_like(g_q), plus_one=False, out_dtype=BF16, name="mla_norm_q")
    ckv = _norm_fwd(a_kv, g_kv, jnp.zeros_like(g_kv), plus_one=False, out_dtype=BF16, name="mla_norm_kv")
    qf = _matmul(cq, w_uq, name="mla_uq")
    kvf = _matmul(ckv, w_ukv, out_dtype=BF16, name="mla_ukv")
    mq = _rope_heads(qf, tab_a, tab_b, tab_c, out_dtype=BF16, name="rope_q")
    kk1, kk2 = _rope(kr1, kr2, cos16, sin16, negate=False, name="rope_k")
    k_tail = jnp.concatenate([kk1, kk2, z32], axis=1).astype(BF16)
    mk = jnp.concatenate([kvf[:, :nq].reshape(s, N_HEADS, HEAD_DIM),
                          jnp.broadcast_to(k_tail[:, None, :], (s, N_HEADS, HEAD_DIM))], axis=-1).reshape(s, N_HEADS * LANES)
    mla_cfg = dict(qc=0, kc=0, vc=N_HEADS // 2, width=LANES)
    mvt = _matmul(w_ukv[:, nq:], ckv, ta=True, tb=True, out_dtype=BF16, name="mla_ukv_vt")
    mo, mla_lse = _attn_fwd(mq, mk, _key_blocks(mvt, att_blk), None, qc=0, kc=0, width=LANES, name="mla_attn_fwd")
    mla_res = (mq, mk, kvf, None, mo, mla_lse)
    y2 = _matmul(mo, w_mla_o, name="mla_out")
    x3 = _resid_fwd(x2, y2, gate[2], name="resid_fwd2")

    x4, ffn1_res = ffn_fwd(x3, 1, 3)

    loss_vec, dx4, d_final_g = _final_loss(x4, final_g.reshape(1, d), target, name="final_loss")
    loss = lax.psum(loss_vec[0, 0], ("x", "y", "c"))

    grads = {}
    dmod = [None] * 4

    def ffn_bwd(dx_out, xin, li, sub, res):
        hh, u, g, y = res
        dy, dgate = _resid_bwd(dx_out, y, gate[sub], name=f"resid_bwd{sub}")
        gw_out = _matmul(g, dy, ta=True, out_dtype=BF16, name=f"ffn_down_dw{li}")
        dg = _matmul(dy, w_ffn_out[li], tb=True, name=f"ffn_down_dx{li}")
        du, dcw, dcb = _conv_gate_bwd(u, conv_w[li], conv_b[li:li + 1], dg, name=f"conv_bwd{li}")
        gw_in = _matmul(hh, du, ta=True, out_dtype=BF16, name=f"ffn_up_dw{li}")
        dh = _matmul(du, w_ffn_in[li], tb=True, out_dtype=BF16, name=f"ffn_up_dx{li}")
        dx_in, dscale, dshift = _norm_bwd(xin, scale[sub], dh, dx_out, plus_one=True, name=f"ada_bwd{sub}")
        dmod[sub] = jnp.concatenate([dshift, dscale, dgate], axis=1)
        return dx_in, gw_in, dcw, dcb, gw_out

    dx3, gw_in1, dcw1, dcb1, gw_out1 = ffn_bwd(dx4, x3, 1, 3, ffn1_res)

    dy2, dgate2 = _resid_bwd(dx3, y2, gate[2], name="resid_bwd2")
    grads["mla_w_o"] = _matmul(mo, dy2, ta=True, out_dtype=BF16, name="mla_out_dw")[None]
    dmo = _matmul(dy2, w_mla_o, tb=True, out_dtype=BF16, name="mla_out_dx")
    dmq, dmk, dmv, _ = _attention_bwd(mla_res, dmo, dq_mult=1.0 / LOG2E, dk_mult=1.0 / LOG2E, out_dtype=F32,
                                      name="mla_attn", **mla_cfg)
    dqf = _rope_heads(dmq, tab_a, -tab_b, -tab_c, out_dtype=BF16, name="rope_q_bwd")
    g_uq = _matmul(cq, dqf, ta=True, out_dtype=BF16, name="mla_uq_dw")
    dcq = _matmul(dqf, w_uq, tb=True, name="mla_uq_dx")
    dmk3 = dmk.reshape(s, N_HEADS, LANES)
    dkr = _group_sum(dmk, name="mla_krope_sum")
    dkr1, dkr2 = _rope(dkr[:, HEAD_DIM:HEAD_DIM + MLA_ROPE_HALF], dkr[:, HEAD_DIM + MLA_ROPE_HALF:HEAD_DIM + MLA_ROPE_DIM],
                       cos16, sin16, negate=True, name="rope_k_bwd")
    dkvf = jnp.concatenate([dmk3[:, :, :HEAD_DIM].reshape(s, nq).astype(BF16), dmv.astype(BF16)], axis=1)
    g_ukv = _matmul(ckv, dkvf, ta=True, out_dtype=BF16, name="mla_ukv_dw")
    dckv = _matmul(dkvf, w_ukv, tb=True, name="mla_ukv_dx")
    da_q, dg_q, _ = _norm_bwd(a_q, g_q, dcq, None, plus_one=False, name="mla_norm_q_bwd")
    da_kv, dg_kv, _ = _norm_bwd(a_kv, g_kv, dckv, None, plus_one=False, name="mla_norm_kv_bwd")
    da = jnp.concatenate([da_q, da_kv, dkr1, dkr2, jnp.zeros((s, w_a.shape[1] - 672), F32)], axis=1).astype(BF16)
    grads["mla_w_a"] = _matmul(h2, da, ta=True, out_dtype=BF16, name="mla_a_dw")[None, :, :672]
    dh2 = _matmul(da, w_a, tb=True, out_dtype=BF16, name="mla_a_dx")
    dx2, dscale2, dshift2 = _norm_bwd(x2, scale[2], dh2, dx3, plus_one=True, name="ada_bwd2")
    dmod[2] = jnp.concatenate([dshift2, dscale2, dgate2], axis=1)
    grads["mla_w_uq"] = g_uq.reshape(MLA_Q_RANK, N_HEADS, LANES)[:, :, :HEAD_DIM + MLA_ROPE_DIM].reshape(1, MLA_Q_RANK, -1)
    grads["mla_w_ukv"] = _per_head([g_ukv[:, :nq], g_ukv[:, nq:]], MLA_KV_RANK)[None]
    grads["mla_g_q"], grads["mla_g_kv"] = dg_q, dg_kv

    dx1, gw_in0, dcw0, dcb0, gw_out0 = ffn_bwd(dx2, x1, 0, 1, ffn0_res)
    grads["ffn_w_in"] = jnp.stack([gw_in0, gw_in1])
    grads["ffn_w_out"] = jnp.stack([gw_out0, gw_out1])
    grads["ffn_conv_w"] = jnp.stack([dcw0, dcw1])
    g_conv_b = jnp.concatenate([dcb0, dcb1], axis=0)

    dy0, dgate0 = _resid_bwd(dx1, y0, gate[0], name="resid_bwd0")
    grads["fox_w_o"] = _matmul(fo, dy0, ta=True, out_dtype=BF16, name="fox_out_dw")[None]
    dfo = _matmul(dy0, w_fox_o, tb=True, out_dtype=BF16, name="fox_out_dx")
    dfq, dfk, dfv, dcum = _attention_bwd(fox_res, dfo, dq_mult=fox_scale, dk_mult=1.0 / LOG2E, out_dtype=BF16,
                                         name="fox_attn", **fox_cfg)
    dfl, g_b_f = _fox_gate_bwd(fl, fox_b_f, dcum, name="fox_gate_bwd")
    dproj = jnp.concatenate([dfq.astype(BF16), dfk, dfv, _pad_axis(dfl, 1, LANES).astype(BF16)], axis=1)
    grads["fox_w_in"] = _matmul(h0, dproj, ta=True, out_dtype=BF16, name="fox_proj_dw")[None, :, :3 * nq + N_HEADS]
    dh0 = _matmul(dproj, w_fox_in, tb=True, out_dtype=BF16, name="fox_proj_dx")
    dx0, dscale0, dshift0 = _norm_bwd(x0, scale[0], dh0, dx1, plus_one=True, name="ada_bwd0")
    dmod[0] = jnp.concatenate([dshift0, dscale0, dgate0], axis=1)

    dmod_send = _pad_axis(jnp.stack(dmod, axis=0).reshape(4, N_DEV, cols).transpose(1, 0, 2), 1, 8)
    dmod_recv, = _exchange([dmod_send], same_src=False, name="scatter_dmod")
    dmod_all = dmod_recv[:, :4]
    dmod_pad = _pad_axis(dmod_all, 0, LANES)
    g_ada_w = jnp.stack([_matmul(silu_c, dmod_pad[:, i], ta=True, name=f"ada_dw{i}") for i in range(4)])
    grads["ada_w"] = g_ada_w.reshape(ada_w.shape)
    grads["ada_b"] = _sum_slabs(dmod_recv, name="ada_db")[:4].reshape(ada_b.shape)

    sharded = list(_BIG) + list(_SMALL)
    axes = {**_BIG, **_SMALL}
    recv = _exchange([_full_to_chunks(grads[n], axes[n]) for n in sharded], same_src=False, name="scatter_grads")
    grads.update(dict(zip(sharded, recv)))
    repl = _exchange([g_b_f, g_conv_b, d_final_g], same_src=True, name="gather_repl_grads")
    grads.update(dict(zip(_REPL, repl)))

    grad_out, deltas, new_m, new_v = {}, {}, {}, {}
    for n in order:
        grad_out[n], deltas[n], new_m[n], new_v[n] = _adamw(
            weights[n], grads[n], mom_m[n], mom_v[n], slabs=n in axes or n in _REPL, name=f"adamw_{n}")

    grad_x = dx0[None]
    return (loss, grad_x, *[grad_out[n] for n in order], *[deltas[n] for n in order],
            *[new_m[n] for n in order], *[new_v[n] for n in order])
```

```python
import jax
import jax.numpy as jnp
from jax import lax
from jax.experimental import pallas as pl
from jax.experimental.pallas import tpu as pltpu

F32 = jnp.float32
BF16 = jnp.bfloat16
HIGHEST = lax.Precision.HIGHEST

N_DEV = 8
D_MODEL = 1024
N_HEADS = 16
HEAD_DIM = 64
MLA_ROPE_HALF = 16
MLA_Q_RANK = 384
MLA_KV_RANK = 256
MLA_ROPE_DIM = 32
NORM_EPS = 1e-6
ROPE_BASE = 10000.0
ADAM_LR = 0.001
ADAM_B1 = 0.9
ADAM_B2 = 0.999
ADAM_EPS = 1e-08
ADAM_WD = 0.01
ADAM_STEP = 10

LANES = 128
VMEM_LIMIT_BYTES = 56 * 1024 * 1024
ROW_BLOCK = 512
ATT_BLOCK = 512
CONV_ROWS = 1024
MM_BM, MM_BN, MM_BK = 512, 1024, 2048
MM_K_WHOLE = 3328
LOG2E = 1.4426950408889634


def _params(*sem):
    return pltpu.CompilerParams(dimension_semantics=sem or None, vmem_limit_bytes=VMEM_LIMIT_BYTES)


def _blk(dim, pref):
    if dim <= pref:
        return dim
    b = pref - pref % LANES
    while b >= LANES:
        if dim % b == 0:
            return b
        b -= LANES
    raise ValueError(f"no block for {dim}")


def _row_blk(rows, pref):
    if rows <= pref:
        return rows
    for b in range(pref - pref % 8, 7, -8):
        if rows % b == 0:
            return b
    return rows


def _pad_axis(a, axis, mult):
    pad = (-a.shape[axis]) % mult
    if pad == 0:
        return a
    widths = [(0, 0)] * a.ndim
    widths[axis] = (0, pad)
    return jnp.pad(a, widths)


def _matmul(a, b, *, ta=False, tb=False, out_dtype=F32, col_scale=None, name):
    m, k = (a.shape[1], a.shape[0]) if ta else a.shape
    n = b.shape[0] if tb else b.shape[1]
    assert (b.shape[1] if tb else b.shape[0]) == k, (a.shape, b.shape, ta, tb)
    bm, bn = _blk(m, MM_BM if ta else 2 * MM_BM), _blk(n, MM_BN)
    bk = k if k <= MM_K_WHOLE else _blk(k, MM_BK)
    nk = k // bk
    dims = (((0 if ta else 1,), (1 if tb else 0,)), ((), ()))
    has_scale = col_scale is not None
    use_acc = nk > 1 and (out_dtype != F32 or has_scale)

    def body(*refs):
        a_ref, b_ref = refs[0], refs[1]
        s_ref = refs[2] if has_scale else None
        o_ref = refs[3] if has_scale else refs[2]
        acc_ref = refs[-1] if use_acc else o_ref
        kk = pl.program_id(2)
        part = lax.dot_general(a_ref[...].astype(BF16), b_ref[...].astype(BF16), dims, preferred_element_type=F32)

        def finish(val):
            if has_scale:
                val = val * s_ref[...]
            o_ref[...] = val.astype(out_dtype)

        if nk == 1:
            finish(part)
            return

        @pl.when(kk == 0)
        def _():
            acc_ref[...] = part

        @pl.when(kk > 0)
        def _():
            acc_ref[...] += part

        if use_acc:
            @pl.when(kk == nk - 1)
            def _():
                finish(acc_ref[...])

    a_spec = pl.BlockSpec((bk, bm), lambda i, j, kk: (kk, i)) if ta else pl.BlockSpec((bm, bk), lambda i, j, kk: (i, kk))
    b_spec = pl.BlockSpec((bn, bk), lambda i, j, kk: (j, kk)) if tb else pl.BlockSpec((bk, bn), lambda i, j, kk: (kk, j))
    return pl.pallas_call(
        body, name=name, grid=(m // bm, n // bn, nk),
        in_specs=[a_spec, b_spec] + ([pl.BlockSpec((1, bn), lambda i, j, kk: (0, j))] if has_scale else []),
        out_specs=pl.BlockSpec((bm, bn), lambda i, j, kk: (i, j)),
        out_shape=jax.ShapeDtypeStruct((m, n), out_dtype),
        scratch_shapes=[pltpu.VMEM((bm, bn), F32)] if use_acc else [],
        compiler_params=_params("parallel", "parallel", "arbitrary"),
    )(*([a, b] + ([col_scale] if has_scale else [])))


def _norm_fwd(x, mul, add, *, plus_one, out_dtype, name):
    s, n = x.shape
    bs = _blk(s, ROW_BLOCK)

    def body(x_ref, m_ref, a_ref, o_ref):
        xv = x_ref[...]
        r = lax.rsqrt(jnp.mean(xv * xv, axis=-1, keepdims=True) + NORM_EPS)
        mv = m_ref[...] + 1.0 if plus_one else m_ref[...]
        o_ref[...] = (xv * r * mv + a_ref[...]).astype(out_dtype)

    row = pl.BlockSpec((bs, n), lambda i: (i, 0))
    vec = pl.BlockSpec((1, n), lambda i: (0, 0))
    return pl.pallas_call(
        body, name=name, grid=(s // bs,), in_specs=[row, vec, vec], out_specs=row,
        out_shape=jax.ShapeDtypeStruct((s, n), out_dtype), compiler_params=_params("parallel"),
    )(x, mul, add)


def _norm_bwd(x, mul, dy, dres, *, plus_one, name):
    s, n = x.shape
    bs = _blk(s, ROW_BLOCK)
    has_res = dres is not None

    def body(*refs):
        if has_res:
            x_ref, m_ref, dy_ref, dres_ref, dx_ref, dm_ref, da_ref = refs
        else:
            x_ref, m_ref, dy_ref, dx_ref, dm_ref, da_ref = refs
        xv = x_ref[...]
        dyv = dy_ref[...].astype(F32)
        r = lax.rsqrt(jnp.mean(xv * xv, axis=-1, keepdims=True) + NORM_EPS)
        xn = xv * r
        mv = m_ref[...] + 1.0 if plus_one else m_ref[...]
        g = dyv * mv
        dx = r * (g - xn * jnp.mean(g * xn, axis=-1, keepdims=True))
        if has_res:
            dx = dx + dres_ref[...]
        dx_ref[...] = dx

        @pl.when(pl.program_id(0) == 0)
        def _():
            dm_ref[...] = jnp.zeros_like(dm_ref)
            da_ref[...] = jnp.zeros_like(da_ref)

        dm_ref[...] += jnp.sum(dyv * xn, axis=0, keepdims=True)
        da_ref[...] += jnp.sum(dyv, axis=0, keepdims=True)

    row = pl.BlockSpec((bs, n), lambda i: (i, 0))
    vec = pl.BlockSpec((1, n), lambda i: (0, 0))
    ins = [x, mul, dy] + ([dres] if has_res else [])
    return pl.pallas_call(
        body, name=name, grid=(s // bs,),
        in_specs=[row, vec, row] + ([row] if has_res else []), out_specs=[row, vec, vec],
        out_shape=[jax.ShapeDtypeStruct((s, n), F32), jax.ShapeDtypeStruct((1, n), F32), jax.ShapeDtypeStruct((1, n), F32)],
        compiler_params=_params("arbitrary"),
    )(*ins)


def _resid_fwd(x, y, gate, *, name):
    s, n = x.shape
    bs = _blk(s, ROW_BLOCK)

    def body(x_ref, y_ref, g_ref, o_ref):
        o_ref[...] = x_ref[...] + g_ref[...] * y_ref[...]

    row = pl.BlockSpec((bs, n), lambda i: (i, 0))
    vec = pl.BlockSpec((1, n), lambda i: (0, 0))
    return pl.pallas_call(
        body, name=name, grid=(s // bs,), in_specs=[row, row, vec], out_specs=row,
        out_shape=jax.ShapeDtypeStruct((s, n), F32), compiler_params=_params("parallel"),
    )(x, y, gate)


def _resid_bwd(dx, y, gate, *, name):
    s, n = dx.shape
    bs = _blk(s, ROW_BLOCK)

    def body(dx_ref, y_ref, g_ref, dy_ref, dg_ref):
        dxv = dx_ref[...]
        dy_ref[...] = (g_ref[...] * dxv).astype(BF16)

        @pl.when(pl.program_id(0) == 0)
        def _():
            dg_ref[...] = jnp.zeros_like(dg_ref)

        dg_ref[...] += jnp.sum(dxv * y_ref[...], axis=0, keepdims=True)

    row = pl.BlockSpec((bs, n), lambda i: (i, 0))
    vec = pl.BlockSpec((1, n), lambda i: (0, 0))
    return pl.pallas_call(
        body, name=name, grid=(s // bs,), in_specs=[row, row, vec], out_specs=[row, vec],
        out_shape=[jax.ShapeDtypeStruct((s, n), BF16), jax.ShapeDtypeStruct((1, n), F32)],
        compiler_params=_params("arbitrary"),
    )(dx, y, gate)


def _final_loss(x, g, target, *, name):
    s, n = x.shape
    bs = _blk(s, ROW_BLOCK)

    def body(x_ref, g_ref, t_ref, loss_ref, dx_ref, dg_ref):
        xv = x_ref[...]
        r = lax.rsqrt(jnp.mean(xv * xv, axis=-1, keepdims=True) + NORM_EPS)
        xn = xv * r
        gv = g_ref[...]
        err = xn * gv - t_ref[...]
        dout = err * (1.0 / n)
        gg = dout * gv
        dx_ref[...] = r * (gg - xn * jnp.mean(gg * xn, axis=-1, keepdims=True))

        @pl.when(pl.program_id(0) == 0)
        def _():
            loss_ref[...] = jnp.zeros_like(loss_ref)
            dg_ref[...] = jnp.zeros_like(dg_ref)

        part = jnp.sum(jnp.sum(err * err, axis=-1, keepdims=True), axis=0, keepdims=True) * (0.5 / n)
        loss_ref[...] += jnp.broadcast_to(part, loss_ref.shape)
        dg_ref[...] += jnp.sum(dout * xn, axis=0, keepdims=True)

    row = pl.BlockSpec((bs, n), lambda i: (i, 0))
    vec = pl.BlockSpec((1, n), lambda i: (0, 0))
    return pl.pallas_call(
        body, name=name, grid=(s // bs,), in_specs=[row, vec, row],
        out_specs=[pl.BlockSpec((1, LANES), lambda i: (0, 0)), row, vec],
        out_shape=[jax.ShapeDtypeStruct((1, LANES), F32), jax.ShapeDtypeStruct((s, n), F32), jax.ShapeDtypeStruct((1, n), F32)],
        compiler_params=_params("arbitrary"),
    )(x, g, target)


def _lane_lt64(shape):
    return lax.broadcasted_iota(jnp.int32, shape, 1) < HEAD_DIM


def _keep_low(x):
    return jnp.where(_lane_lt64(x.shape), x.astype(F32), 0.0).astype(x.dtype)


def _keep_high(x):
    return jnp.where(_lane_lt64(x.shape), 0.0, x.astype(F32)).astype(x.dtype)


def _lane_merge(a, b):
    n = max(a.shape[0], b.shape[0])
    return jnp.where(_lane_lt64((n, LANES)), a, b)


def _pair(x, width, masked):
    if width == HEAD_DIM:
        return (_keep_low(x), _keep_high(x)) if masked else (x, x)
    return x[:, :LANES], x[:, LANES:]


def _qk_t(a, b):
    return lax.dot_general(a, b, (((1,), (1,)), ((), ())), preferred_element_type=F32)


def _attn_specs(s, blk, width, cols, resident):
    w = 2 * width
    if resident:
        return pl.BlockSpec((s, w), lambda p, i: (0, cols + p))
    return pl.BlockSpec((blk, w), lambda p, i: (i, cols + p))


BIAS_TERMS = 3


def _key_blocks(vt, blk):
    return vt.reshape(vt.shape[0], vt.shape[1] // blk, blk).transpose(1, 0, 2)


def _attn_fwd(q, k, vt, kbl, *, qc, kc, width, name):
    s = q.shape[0]
    blk = _blk(s, ATT_BLOCK)
    nb = s // blk
    has_bias = kbl is not None
    assert has_bias == (width == HEAD_DIM)

    def body(*refs):
        if has_bias:
            q_ref, k_ref, vt_ref, kbl_ref, o_ref, lse_ref = refs
        else:
            q_ref, k_ref, vt_ref, o_ref, lse_ref = refs
        i = pl.program_id(1)
        q2 = q_ref[...]
        if has_bias:
            lane = lax.broadcasted_iota(jnp.int32, (blk, LANES), 1)
            qf = q2.astype(F32)
            qh = (jnp.where(lane < HEAD_DIM, qf, jnp.where(lane < HEAD_DIM + BIAS_TERMS, 1.0, 0.0)).astype(BF16),
                  jnp.where(lane >= HEAD_DIM, qf, jnp.where(lane < BIAS_TERMS, 1.0, 0.0)).astype(BF16))
        else:
            qh = (q2[:, :LANES], q2[:, LANES:])

        def step(j, carry, nblk, diag):
            rows = pl.ds(pl.multiple_of(j * blk, blk), nblk * blk)
            vt1 = [jnp.concatenate([vt_ref[j + b], jnp.ones((16, blk), BF16)], axis=0) for b in range(nblk)]
            k2 = k_ref[rows, :]
            if has_bias:
                low = _lane_lt64(k2.shape)
                kf, bf = k2.astype(F32), kbl_ref[rows, :].astype(F32)
                kh = (jnp.where(low, kf, bf).astype(BF16), jnp.where(low, bf, kf).astype(BF16))
            else:
                kh = (k2[:, :LANES], k2[:, LANES:])
            out = []
            for hd in range(2):
                m, acc = carry[hd]
                st = _qk_t(kh[hd], qh[hd])
                if diag:
                    row = lax.broadcasted_iota(jnp.int32, (blk, blk), 0)
                    colq = lax.broadcasted_iota(jnp.int32, (blk, blk), 1)
                    st = jnp.where(row <= colq, st, -1e30)
                m_new = jnp.maximum(m, jnp.max(st, axis=0, keepdims=True))
                alpha = jnp.exp2(m - m_new)
                pt = jnp.exp2(st - m_new).astype(BF16)
                acc = alpha * acc
                for b in range(nblk):
                    acc = acc + jnp.dot(vt1[b], pt[b * blk:(b + 1) * blk], preferred_element_type=F32)
                out.append((m_new, acc))
            return tuple(out)

        one = (jnp.full((1, blk), -1e30, F32), jnp.zeros((LANES + 16, blk), F32))
        carry = lax.fori_loop(0, i // 4, lambda j, c: step(4 * j, c, 4, False), (one, one))
        carry = lax.fori_loop(0, (i % 4) // 2, lambda _, c: step(i - i % 4, c, 2, False), carry)
        carry = lax.fori_loop(0, i % 2, lambda _, c: step(i - 1, c, 1, False), carry)
        (ma, acca), (mb, accb) = step(i, carry, 1, True)
        la = jnp.max(acca[LANES:LANES + 8], axis=0, keepdims=True)
        lb = jnp.max(accb[LANES:LANES + 8], axis=0, keepdims=True)
        acca, accb = acca[0:LANES], accb[0:LANES]
        low = lax.broadcasted_iota(jnp.int32, (LANES, blk), 0) < HEAD_DIM
        o_ref[...] = jnp.where(low, acca / la, accb / lb).T
        lse_ref[0, 0] = ma + jnp.log(la) * LOG2E
        lse_ref[1, 0] = mb + jnp.log(lb) * LOG2E

    ins = [q, k, vt] + ([kbl] if has_bias else [])
    return pl.pallas_call(
        body, name=name, grid=(N_HEADS // 2, nb),
        in_specs=[_attn_specs(s, blk, width, qc, False), _attn_specs(s, blk, width, kc, True),
                  pl.BlockSpec((nb, LANES, blk), lambda p, i: (0, p, 0))]
                 + ([_attn_specs(s, blk, HEAD_DIM, 0, True)] if has_bias else []),
        out_specs=[pl.BlockSpec((blk, LANES), lambda p, i: (i, p)), pl.BlockSpec((2, 1, 1, blk), lambda p, i: (p, i, 0, 0))],
        out_shape=[jax.ShapeDtypeStruct((s, N_HEADS * HEAD_DIM), F32), jax.ShapeDtypeStruct((N_HEADS, nb, 1, blk), F32)],
        compiler_params=_params("parallel", "parallel"),
    )(*ins)


def _bias_lane_terms(kb2):
    s = kb2.shape[0]
    terms, rest = [], kb2
    for _ in range(BIAS_TERMS):
        t = lax.reduce_precision(rest, 8, 7)
        terms.append(t.astype(BF16))
        rest = rest - t
    t3 = jnp.stack(terms, axis=-1).reshape(s, N_HEADS // 2, 2, BIAS_TERMS)
    pad = jnp.zeros((s, N_HEADS // 2, HEAD_DIM - BIAS_TERMS), BF16)
    return jnp.concatenate([t3[:, :, 1], pad, t3[:, :, 0], pad], axis=-1).reshape(s, N_HEADS * HEAD_DIM)


def _causal_keep(n):
    row = lax.broadcasted_iota(jnp.int32, (n, n), 0)
    col = lax.broadcasted_iota(jnp.int32, (n, n), 1)
    return col <= row


def _attn_delta(o, do, *, name):
    s, n = o.shape
    bs = _blk(s, ROW_BLOCK)

    def body(o_ref, do_ref, d_ref):
        for g in range(n // LANES):
            prod = do_ref[:, g * LANES:(g + 1) * LANES].astype(F32) * o_ref[:, g * LANES:(g + 1) * LANES]
            low = _lane_lt64(prod.shape)
            d_ref[:, g * LANES:(g + 1) * LANES] = _lane_merge(
                jnp.sum(jnp.where(low, prod, 0.0), axis=-1, keepdims=True),
                jnp.sum(jnp.where(low, 0.0, prod), axis=-1, keepdims=True))

    row = pl.BlockSpec((bs, n), lambda i: (i, 0))
    return pl.pallas_call(
        body, name=name, grid=(s // bs,), in_specs=[row, row], out_specs=row,
        out_shape=jax.ShapeDtypeStruct((s, n), F32), compiler_params=_params("parallel"),
    )(o, do)


def _attn_bwd(q, k, v, kb_col, do, lse_row, delta_row, *, qc, kc, vc, width, dq_mult, dk_mult, out_dtype, name):
    s = q.shape[0]
    blk = _blk(s, ATT_BLOCK)
    nb = s // blk
    has_bias = kb_col is not None

    def body(*refs):
        if has_bias:
            q_ref, k_ref, v_ref, kb_ref, do_ref, lse_ref, dl_ref, dk_ref, dv_ref, db_ref, dq_ref, dr_ref = refs
        else:
            q_ref, k_ref, v_ref, do_ref, lse_ref, dl_ref, dk_ref, dv_ref, db_ref, dq_ref = refs
        j = pl.program_id(1)

        @pl.when(j == 0)
        def _():
            dq_ref[...] = jnp.zeros_like(dq_ref)
            if has_bias:
                dr_ref[...] = jnp.zeros_like(dr_ref)

        kh = _pair(k_ref[...], width, True)
        v2 = v_ref[...]
        vh = (_keep_low(v2), _keep_high(v2))
        if has_bias:
            kb2 = kb_ref[0]
            kbh = (kb2[:, 0:1], kb2[:, 1:2])

        def step(i, carry, nblk, diag):
            rows = pl.ds(pl.multiple_of(i * blk, blk), nblk * blk)
            qh = _pair(q_ref[rows, :], width, False)
            doi = do_ref[rows, :]
            out, dq_parts = [], []
            for hd in range(2):
                dk, dvv, db = carry[hd]
                st = _qk_t(kh[hd], qh[hd])
                if has_bias:
                    st = st + kbh[hd]
                if diag:
                    row = lax.broadcasted_iota(jnp.int32, (blk, blk), 0)
                    colq = lax.broadcasted_iota(jnp.int32, (blk, blk), 1)
                    st = jnp.where(row <= colq, st, -1e30)
                lse_i = jnp.concatenate([lse_ref[hd, i + b] for b in range(nblk)], axis=1)
                delta_i = jnp.concatenate([dl_ref[hd, i + b] for b in range(nblk)], axis=1)
                pt = jnp.exp2(st - lse_i)
                dvv = dvv + jnp.dot(pt.astype(BF16), doi, preferred_element_type=F32)
                dst = pt * (_qk_t(vh[hd], doi) - delta_i)
                dsb = dst.astype(BF16)
                dk = dk + jnp.dot(dsb, qh[hd], preferred_element_type=F32)
                db = db + jnp.sum(dst, axis=-1, keepdims=True)
                dq_parts.append(lax.dot_general(dsb, kh[hd], (((0,), (0,)), ((), ())), preferred_element_type=F32))
                if has_bias:
                    rsum = jnp.sum(dst, axis=0, keepdims=True)
                    for b in range(nblk):
                        dr_ref[hd, i + b] += rsum[:, b * blk:(b + 1) * blk]
                out.append((dk, dvv, db))
            if width == HEAD_DIM:
                dq_ref[rows, :] += (dq_parts[0] + dq_parts[1]) * dq_mult
            else:
                dq_ref[rows, 0:LANES] += dq_parts[0] * dq_mult
                dq_ref[rows, LANES:2 * LANES] += dq_parts[1] * dq_mult
            return tuple(out)

        one = (jnp.zeros((blk, LANES), F32), jnp.zeros((blk, LANES), F32), jnp.zeros((blk, 1), F32))
        carry = step(j, (one, one), 1, True)
        rest = nb - 1 - j
        carry = lax.fori_loop(0, rest // 2, lambda t, c: step(j + 1 + 2 * t, c, 2, False), carry)
        (dka, dva, dba), (dkb, dvb, dbb) = lax.fori_loop(0, rest % 2, lambda _, c: step(nb - 1, c, 1, False), carry)
        if width == HEAD_DIM:
            dk = _lane_merge(dka, dkb)
        else:
            dk = jnp.concatenate([dka, dkb], axis=1)
        dk_ref[...] = (dk * dk_mult).astype(out_dtype)
        dv_ref[...] = _lane_merge(dva, dvb).astype(out_dtype)
        db_ref[...] = _lane_merge(dba, dbb)

    stat = pl.BlockSpec((blk, LANES), lambda p, jj: (jj, p))
    rows = pl.BlockSpec((2, nb, 1, blk), lambda p, jj: (p, 0, 0, 0))
    ins = [q, k, v] + ([kb_col] if has_bias else []) + [do, lse_row, delta_row]
    return pl.pallas_call(
        body, name=name, grid=(N_HEADS // 2, nb),
        in_specs=[_attn_specs(s, blk, width, qc, True), _attn_specs(s, blk, width, kc, False),
                  _attn_specs(s, blk, HEAD_DIM, vc, False)]
                 + ([pl.BlockSpec((1, blk, 2), lambda p, jj: (p, jj, 0))] if has_bias else [])
                 + [pl.BlockSpec((s, LANES), lambda p, jj: (0, p)), rows, rows],
        out_specs=[pl.BlockSpec((blk, 2 * width), lambda p, jj: (jj, p)), stat, stat,
                   pl.BlockSpec((s, 2 * width), lambda p, jj: (0, p))] + ([rows] if has_bias else []),
        out_shape=[jax.ShapeDtypeStruct((s, N_HEADS * width), out_dtype), jax.ShapeDtypeStruct((s, N_HEADS * HEAD_DIM), out_dtype),
                   jax.ShapeDtypeStruct((s, N_HEADS * HEAD_DIM), F32), jax.ShapeDtypeStruct((s, N_HEADS * width), F32)]
                  + ([jax.ShapeDtypeStruct((N_HEADS, nb, 1, blk), F32)] if has_bias else []),
        compiler_params=_params("parallel", "arbitrary"),
    )(*ins)


def _head_stat(t):
    return t[:, ::HEAD_DIM]


def _stat_rows(t16, blk):
    s = t16.shape[0]
    return t16.T.reshape(N_HEADS, s // blk, 1, blk)


def _attention_bwd(res, do, *, qc, kc, vc, width, dq_mult, dk_mult, out_dtype, name):
    q, k, v, bias, o, lse_row = res
    s = q.shape[0]
    blk = _blk(s, ATT_BLOCK)
    kb_col = None if bias is None else bias.reshape(s, N_HEADS // 2, 2).transpose(1, 0, 2)
    delta_row = _stat_rows(_head_stat(_attn_delta(o, do, name=name + "_delta")), blk)
    outs = _attn_bwd(q, k, v, kb_col, do, lse_row, delta_row, qc=qc, kc=kc, vc=vc, width=width, dq_mult=dq_mult,
                     dk_mult=dk_mult, out_dtype=out_dtype, name=name + "_bwd")
    dk, dv, dcol, dq = outs[:4]
    if bias is None:
        return dq, dk, dv, None
    return dq, dk, dv, outs[4].reshape(N_HEADS, s).T - _head_stat(dcol)


def _fox_gate_fwd(fl, bf, *, name):
    s, n = fl.shape
    bs = _blk(s, ROW_BLOCK)

    def body(fl_ref, bf_ref, cum_ref, carry_ref):
        @pl.when(pl.program_id(0) == 0)
        def _():
            carry_ref[...] = jnp.zeros_like(carry_ref)

        z = fl_ref[...] + bf_ref[...]
        lf = jnp.minimum(z, 0.0) - jnp.log1p(jnp.exp(-jnp.abs(z)))
        row = lax.broadcasted_iota(jnp.int32, (bs, bs), 0)
        col = lax.broadcasted_iota(jnp.int32, (bs, bs), 1)
        tri = (col <= row).astype(F32)
        cum_ref[...] = jnp.dot(tri, lf, preferred_element_type=F32, precision=HIGHEST) + carry_ref[...]
        carry_ref[...] += jnp.sum(lf, axis=0, keepdims=True)

    return pl.pallas_call(
        body, name=name, grid=(s // bs,),
        in_specs=[pl.BlockSpec((bs, n), lambda i: (i, 0)), pl.BlockSpec((1, n), lambda i: (0, 0))],
        out_specs=pl.BlockSpec((bs, n), lambda i: (i, 0)),
        out_shape=jax.ShapeDtypeStruct((s, n), F32), scratch_shapes=[pltpu.VMEM((1, n), F32)],
        compiler_params=_params("arbitrary"),
    )(fl, bf)


def _fox_gate_bwd(fl, bf, dcum, *, name):
    s, n = fl.shape
    bs = _blk(s, ROW_BLOCK)
    nb = s // bs

    def body(fl_ref, bf_ref, dc_ref, dz_ref, dbf_ref, carry_ref):
        @pl.when(pl.program_id(0) == 0)
        def _():
            carry_ref[...] = jnp.zeros_like(carry_ref)
            dbf_ref[...] = jnp.zeros_like(dbf_ref)

        dc = dc_ref[...]
        row = lax.broadcasted_iota(jnp.int32, (bs, bs), 0)
        col = lax.broadcasted_iota(jnp.int32, (bs, bs), 1)
        tri = (col >= row).astype(F32)
        dlf = jnp.dot(tri, dc, preferred_element_type=F32, precision=HIGHEST) + carry_ref[...]
        carry_ref[...] += jnp.sum(dc, axis=0, keepdims=True)
        z = fl_ref[...] + bf_ref[...]
        dz = dlf / (1.0 + jnp.exp(z))
        dz_ref[...] = dz
        dbf_ref[...] += jnp.sum(dz, axis=0, keepdims=True)

    rev = pl.BlockSpec((bs, n), lambda i: (nb - 1 - i, 0))
    vec = pl.BlockSpec((1, n), lambda i: (0, 0))
    return pl.pallas_call(
        body, name=name, grid=(nb,), in_specs=[rev, vec, rev], out_specs=[rev, vec],
        out_shape=[jax.ShapeDtypeStruct((s, n), F32), jax.ShapeDtypeStruct((1, n), F32)],
        scratch_shapes=[pltpu.VMEM((1, n), F32)], compiler_params=_params("arbitrary"),
    )(fl, bf, dcum)


def _rope(x1, x2, cos, sin, *, negate, name):
    s, n = x1.shape
    bs = _blk(s, ROW_BLOCK)

    def body(a_ref, b_ref, c_ref, s_ref, o1_ref, o2_ref):
        a, b, cv = a_ref[...], b_ref[...], c_ref[...]
        sv = -s_ref[...] if negate else s_ref[...]
        o1_ref[...] = a * cv - b * sv
        o2_ref[...] = b * cv + a * sv

    row = pl.BlockSpec((bs, n), lambda i: (i, 0))
    return pl.pallas_call(
        body, name=name, grid=(s // bs,), in_specs=[row] * 4, out_specs=[row, row],
        out_shape=[jax.ShapeDtypeStruct((s, n), F32)] * 2, compiler_params=_params("parallel"),
    )(x1, x2, cos, sin)


def _rope_heads(x, ta, tb, tc, *, out_dtype, name):
    s, n = x.shape
    bs = _blk(s, ROW_BLOCK)

    def body(x_ref, a_ref, b_ref, c_ref, o_ref):
        av, bv, cv = a_ref[...], b_ref[...], c_ref[...]
        for g in range(n // LANES):
            xg = x_ref[:, g * LANES:(g + 1) * LANES]
            og = xg * av + pltpu.roll(xg, LANES - MLA_ROPE_HALF, 1) * bv + pltpu.roll(xg, MLA_ROPE_HALF, 1) * cv
            o_ref[:, g * LANES:(g + 1) * LANES] = og.astype(out_dtype)

    row = pl.BlockSpec((bs, n), lambda i: (i, 0))
    tab = pl.BlockSpec((bs, LANES), lambda i: (i, 0))
    return pl.pallas_call(
        body, name=name, grid=(s // bs,), in_specs=[row, tab, tab, tab], out_specs=row,
        out_shape=jax.ShapeDtypeStruct((s, n), out_dtype), compiler_params=_params("parallel"),
    )(x, ta, tb, tc)


def _group_sum(x, *, name):
    s, n = x.shape
    bs = _blk(s, ROW_BLOCK)

    def body(x_ref, o_ref):
        acc = x_ref[:, 0:LANES]
        for g in range(1, n // LANES):
            acc = acc + x_ref[:, g * LANES:(g + 1) * LANES]
        o_ref[...] = acc

    return pl.pallas_call(
        body, name=name, grid=(s // bs,), in_specs=[pl.BlockSpec((bs, n), lambda i: (i, 0))],
        out_specs=pl.BlockSpec((bs, LANES), lambda i: (i, 0)),
        out_shape=jax.ShapeDtypeStruct((s, LANES), F32), compiler_params=_params("parallel"),
    )(x)


def _shift_down(x, k):
    return pltpu.roll(x, k, 0)


def _conv_rows(ext, w_ref, b_ref, rows):
    y = b_ref[...] + w_ref[0:1, :] * _shift_down(ext, 2) + w_ref[1:2, :] * _shift_down(ext, 1) + w_ref[2:3, :] * ext
    return y[8:8 + rows]


def _conv_gate_fwd(u, cw, cb, *, name):
    s, f2 = u.shape
    f = f2 // 2
    nf = f // LANES
    r = _blk(s, CONV_ROWS)
    r8 = r // 8

    def body(ug_ref, ugp_ref, uv_ref, uvp_ref, wg_ref, wv_ref, bg_ref, bv_ref, o_ref):
        first = pl.program_id(1) == 0

        def conv(cur_ref, prev_ref, w_ref, b_ref):
            prev = jnp.where(first, 0.0, prev_ref[...])
            return _conv_rows(jnp.concatenate([prev, cur_ref[...]], axis=0), w_ref, b_ref, r)

        yg = conv(ug_ref, ugp_ref, wg_ref, bg_ref)
        yv = conv(uv_ref, uvp_ref, wv_ref, bv_ref)
        o_ref[...] = (yg * jax.nn.sigmoid(yg) * yv).astype(BF16)

    def cur(off):
        return pl.BlockSpec((r, LANES), lambda c, i: (i, c + off))

    def prev(off):
        return pl.BlockSpec((8, LANES), lambda c, i: (jnp.maximum(i * r8 - 1, 0), c + off))

    def wspec(rows, off):
        return pl.BlockSpec((rows, LANES), lambda c, i: (0, c + off))

    return pl.pallas_call(
        body, name=name, grid=(nf, s // r),
        in_specs=[cur(0), prev(0), cur(nf), prev(nf), wspec(3, 0), wspec(3, nf), wspec(1, 0), wspec(1, nf)],
        out_specs=pl.BlockSpec((r, LANES), lambda c, i: (i, c)),
        out_shape=jax.ShapeDtypeStruct((s, f), BF16), compiler_params=_params("parallel", "parallel"),
    )(u, u, u, u, cw, cw, cb, cb)


def _conv_gate_bwd(u, cw, cb, dg, *, name):
    s, f2 = u.shape
    f = f2 // 2
    nf = f // LANES
    r = _blk(s, CONV_ROWS)
    r8 = r // 8
    nr = s // r

    def body(ug_ref, ugp_ref, ugn_ref, uv_ref, uvp_ref, uvn_ref, wg_ref, wv_ref, bg_ref, bv_ref, dg_ref, dgn_ref,
             dug_ref, duv_ref, dwg_ref, dwv_ref, dbg_ref, dbv_ref):
        i = pl.program_id(1)
        first, last = i == 0, i == nr - 1

        def ext_of(cur_ref, prev_ref, next_ref):
            prev = jnp.where(first, 0.0, prev_ref[...])
            return jnp.concatenate([prev, cur_ref[...], next_ref[...]], axis=0)

        eg, ev = ext_of(ug_ref, ugp_ref, ugn_ref), ext_of(uv_ref, uvp_ref, uvn_ref)
        yg = _conv_rows(eg, wg_ref, bg_ref, r + 8)
        yv = _conv_rows(ev, wv_ref, bv_ref, r + 8)
        dgn = jnp.where(last, 0.0, dgn_ref[...])
        dgx = jnp.concatenate([dg_ref[...], dgn], axis=0)
        sg = jax.nn.sigmoid(yg)
        dyg = dgx * yv * (sg * (1.0 + yg * (1.0 - sg)))
        dyv = dgx * (yg * sg)

        @pl.when(i == 0)
        def _():
            for ref in (dwg_ref, dwv_ref, dbg_ref, dbv_ref):
                ref[...] = jnp.zeros_like(ref)

        def grads(dy, ext, w_ref, du_ref, dw_ref, db_ref):
            n = r + 8
            du = w_ref[2:3, :] * dy + w_ref[1:2, :] * pltpu.roll(dy, n - 1, 0) + w_ref[0:1, :] * pltpu.roll(dy, n - 2, 0)
            du_ref[...] = du[0:r].astype(BF16)
            dyc = dy[0:r]
            db_ref[...] += jnp.sum(dyc, axis=0, keepdims=True)
            ext_c = ext[0:r + 8]
            dw_ref[0:1, :] += jnp.sum(dyc * _shift_down(ext_c, 2)[8:], axis=0, keepdims=True)
            dw_ref[1:2, :] += jnp.sum(dyc * _shift_down(ext_c, 1)[8:], axis=0, keepdims=True)
            dw_ref[2:3, :] += jnp.sum(dyc * ext_c[8:], axis=0, keepdims=True)

        grads(dyg, eg, wg_ref, dug_ref, dwg_ref, dbg_ref)
        grads(dyv, ev, wv_ref, duv_ref, dwv_ref, dbv_ref)

    def cur(off):
        return pl.BlockSpec((r, LANES), lambda c, i: (i, c + off))

    def prev(off):
        return pl.BlockSpec((8, LANES), lambda c, i: (jnp.maximum(i * r8 - 1, 0), c + off))

    def nxt(off):
        return pl.BlockSpec((8, LANES), lambda c, i: (jnp.minimum((i + 1) * r8, s // 8 - 1), c + off))

    def wspec(rows, off):
        return pl.BlockSpec((rows, LANES), lambda c, i: (0, c + off))

    outs = pl.pallas_call(
        body, name=name, grid=(nf, nr),
        in_specs=[cur(0), prev(0), nxt(0), cur(nf), prev(nf), nxt(nf), wspec(3, 0), wspec(3, nf), wspec(1, 0), wspec(1, nf),
                  cur(0), nxt(0)],
        out_specs=[cur(0), cur(0), wspec(3, 0), wspec(3, 0), wspec(1, 0), wspec(1, 0)],
        out_shape=[jax.ShapeDtypeStruct((s, f), BF16), jax.ShapeDtypeStruct((s, f), BF16),
                   jax.ShapeDtypeStruct((3, f), F32), jax.ShapeDtypeStruct((3, f), F32),
                   jax.ShapeDtypeStruct((1, f), F32), jax.ShapeDtypeStruct((1, f), F32)],
        compiler_params=_params("parallel", "arbitrary"),
    )(u, u, u, u, u, u, cw, cw, cb, cb, dg, dg)
    dug, duv, dwg, dwv, dbg, dbv = outs
    return jnp.concatenate([dug, duv], axis=1), jnp.concatenate([dwg, dwv], axis=1), jnp.concatenate([dbg, dbv], axis=1)


def _adamw(w, g, m, v, *, slabs, name):
    shape = w.shape
    cols = shape[-1]
    rows = w.size // cols
    w2, m2, v2 = (t.reshape(rows, cols) for t in (w, m, v))
    g2 = g.reshape((N_DEV, rows, cols) if slabs else (rows, cols))
    br = _row_blk(rows, ROW_BLOCK // 2 if slabs else ROW_BLOCK)

    def body(w_ref, g_ref, m_ref, v_ref, go_ref, d_ref, nm_ref, nv_ref):
        if slabs:
            gv = g_ref[0].astype(F32)
            for p in range(1, N_DEV):
                gv = gv + g_ref[p].astype(F32)
        else:
            gv = g_ref[...]
        nm = ADAM_B1 * m_ref[...] + (1.0 - ADAM_B1) * gv
        nv = ADAM_B2 * v_ref[...] + (1.0 - ADAM_B2) * (gv * gv)
        m_hat = nm / (1.0 - ADAM_B1 ** ADAM_STEP)
        v_hat = nv / (1.0 - ADAM_B2 ** ADAM_STEP)
        go_ref[...] = gv
        d_ref[...] = -ADAM_LR * (m_hat / (jnp.sqrt(v_hat) + ADAM_EPS) + ADAM_WD * w_ref[...])
        nm_ref[...] = nm
        nv_ref[...] = nv

    spec = pl.BlockSpec((br, cols), lambda i: (i, 0))
    gspec = pl.BlockSpec((N_DEV, br, cols), lambda i: (0, i, 0)) if slabs else spec
    outs = pl.pallas_call(
        body, name=name, grid=(rows // br,), in_specs=[spec, gspec, spec, spec], out_specs=[spec] * 4,
        out_shape=[jax.ShapeDtypeStruct((rows, cols), F32)] * 4, compiler_params=_params("parallel"),
    )(w2, g2, m2, v2)
    return tuple(t.reshape(shape) for t in outs)


def _exchange(xs, *, same_src, name):
    n = len(xs)
    slabs = [x.shape if same_src else x.shape[1:] for x in xs]

    def body(*refs):
        x_refs, o_refs = refs[:n], refs[n:2 * n]
        send_sems, recv_sems, loc_sems = refs[2 * n:]
        ix, iy, ic = lax.axis_index("x"), lax.axis_index("y"), lax.axis_index("c")
        me = 4 * ix + 2 * iy + ic
        local, sends, recvs = [], [], []
        for a in range(n):
            def src(p, a=a):
                return x_refs[a] if same_src else x_refs[a].at[p]

            local.append(pltpu.make_async_copy(src(me), o_refs[a].at[me], loc_sems.at[a]))
            for k in (1, 2, 4, 3, 5, 6, 7):
                px = 1 - ix if k & 4 else ix
                py = 1 - iy if k & 2 else iy
                pc = 1 - ic if k & 1 else ic
                p = 4 * px + 2 * py + pc
                for dst, out in ((me, sends), (p, recvs)):
                    out.append(pltpu.make_async_remote_copy(
                        src_ref=src(p), dst_ref=o_refs[a].at[dst], send_sem=send_sems.at[a, k - 1],
                        recv_sem=recv_sems.at[a, k - 1], device_id=(px, py, pc), device_id_type=pl.DeviceIdType.MESH))
        for cp in local + sends:
            cp.start()
        for cp in recvs:
            cp.wait_recv()
        for cp in sends:
            cp.wait_send()
        for cp in local:
            cp.wait()

    return pl.pallas_call(
        body, name=name,
        in_specs=[pl.BlockSpec(memory_space=pl.ANY)] * n, out_specs=[pl.BlockSpec(memory_space=pl.ANY)] * n,
        out_shape=[jax.ShapeDtypeStruct((N_DEV,) + tuple(sl), x.dtype) for sl, x in zip(slabs, xs)],
        scratch_shapes=[pltpu.SemaphoreType.DMA((n, N_DEV - 1)), pltpu.SemaphoreType.DMA((n, N_DEV - 1)),
                        pltpu.SemaphoreType.DMA((n,))],
        compiler_params=pltpu.CompilerParams(has_side_effects=True, vmem_limit_bytes=VMEM_LIMIT_BYTES),
    )(*xs)


def _gather_two_level(xs, *, name):
    n = len(xs)

    def body(*refs):
        x_refs, o_refs = refs[:n], refs[n:2 * n]
        send_sems, recv_sems, loc_sems = refs[2 * n:]
        ix, iy, ic = lax.axis_index("x"), lax.axis_index("y"), lax.axis_index("c")
        me = 4 * ix + 2 * iy + ic
        sib = 4 * ix + 2 * iy + (1 - ic)
        chips = [(1 - ix if ch & 2 else ix, 1 - iy if ch & 1 else iy) for ch in (1, 2, 3)]

        def copy(a, pos, src, slot, to):
            return pltpu.make_async_remote_copy(
                src_ref=src, dst_ref=o_refs[a].at[slot], send_sem=send_sems.at[a, pos], recv_sem=recv_sems.at[a, pos],
                device_id=to, device_id_type=pl.DeviceIdType.MESH)

        local = [pltpu.make_async_copy(x_refs[a], o_refs[a].at[me], loc_sems.at[a]) for a in range(n)]
        first, passed, arrive = [], [], []
        for a in range(n):
            first.append(copy(a, 0, x_refs[a], me, (ix, iy, 1 - ic)))
            arrive.append(copy(a, 0, x_refs[a], sib, (ix, iy, 1 - ic)))
            for ch, (px, py) in enumerate(chips, start=1):
                same, other = 4 * px + 2 * py + ic, 4 * px + 2 * py + (1 - ic)
                first.append(copy(a, 2 * ch - 1, x_refs[a], me, (px, py, ic)))
                passed.append((copy(a, 2 * ch - 1, x_refs[a], same, (px, py, ic)),
                               copy(a, 2 * ch, o_refs[a].at[same], same, (ix, iy, 1 - ic))))
                arrive.append(copy(a, 2 * ch, x_refs[a], other, (ix, iy, 1 - ic)))
        for cp in local + first:
            cp.start()
        for landed, onward in passed:
            landed.wait_recv()
            onward.start()
        for cp in arrive:
            cp.wait_recv()
        for cp in first + [onward for _, onward in passed]:
            cp.wait_send()
        for cp in local:
            cp.wait()

    return pl.pallas_call(
        body, name=name,
        in_specs=[pl.BlockSpec(memory_space=pl.ANY)] * n, out_specs=[pl.BlockSpec(memory_space=pl.ANY)] * n,
        out_shape=[jax.ShapeDtypeStruct((N_DEV,) + tuple(x.shape), x.dtype) for x in xs],
        scratch_shapes=[pltpu.SemaphoreType.DMA((n, N_DEV - 1)), pltpu.SemaphoreType.DMA((n, N_DEV - 1)),
                        pltpu.SemaphoreType.DMA((n,))],
        compiler_params=pltpu.CompilerParams(has_side_effects=True, vmem_limit_bytes=VMEM_LIMIT_BYTES),
    )(*xs)


def _sum_slabs(x, *, name):
    n, r, c = x.shape
    br = _row_blk(r, ROW_BLOCK)

    def body(x_ref, o_ref):
        acc = x_ref[0]
        for p in range(1, n):
            acc = acc + x_ref[p]
        o_ref[...] = acc

    return pl.pallas_call(
        body, name=name, grid=(r // br,), in_specs=[pl.BlockSpec((n, br, c), lambda i: (0, i, 0))],
        out_specs=pl.BlockSpec((br, c), lambda i: (i, 0)),
        out_shape=jax.ShapeDtypeStruct((r, c), F32), compiler_params=_params("parallel"),
    )(x)


def _silu(x, *, name):
    def body(x_ref, o_ref):
        xv = x_ref[...]
        o_ref[...] = (xv * jax.nn.sigmoid(xv)).astype(BF16)

    return pl.pallas_call(body, name=name, out_shape=jax.ShapeDtypeStruct(x.shape, BF16),
                          compiler_params=_params())(x)


_BIG = {"fox_w_in": 2, "fox_w_o": 1, "mla_w_a": 1, "mla_w_uq": 2, "mla_w_ukv": 2, "mla_w_o": 1, "ffn_w_in": 2, "ffn_w_out": 1}
_SMALL = {"mla_g_q": 1, "mla_g_kv": 1, "ffn_conv_w": 2}
_REPL = ("fox_b_f", "ffn_conv_b", "final_g")


def _gathered_to_full(g, axis):
    full = jnp.moveaxis(g, 0, axis)
    shape = list(full.shape)
    shape[axis:axis + 2] = [shape[axis] * shape[axis + 1]]
    return full.reshape(shape)


def _full_to_chunks(full, axis):
    shape = list(full.shape)
    shape[axis:axis + 1] = [N_DEV, shape[axis] // N_DEV]
    return jnp.moveaxis(full.reshape(shape), axis, 0)


def _per_head(parts, s_or_rows):
    return jnp.concatenate([p.reshape(s_or_rows, N_HEADS, -1) for p in parts], axis=-1).reshape(s_or_rows, -1)


def kernel(x, c, ada_w, ada_b, fox_w_in, fox_b_f, fox_w_o, mla_w_a, mla_g_q, mla_g_kv, mla_w_uq, mla_w_ukv, mla_w_o, ffn_w_in, ffn_conv_w, ffn_conv_b, ffn_w_out, final_g, loss_target, m_ada_w, m_ada_b, m_fox_w_in, m_fox_b_f, m_fox_w_o, m_mla_w_a, m_mla_g_q, m_mla_g_kv, m_mla_w_uq, m_mla_w_ukv, m_mla_w_o, m_ffn_w_in, m_ffn_conv_w, m_ffn_conv_b, m_ffn_w_out, m_final_g, v_ada_w, v_ada_b, v_fox_w_in, v_fox_b_f, v_fox_w_o, v_mla_w_a, v_mla_g_q, v_mla_g_kv, v_mla_w_uq, v_mla_w_ukv, v_mla_w_o, v_ffn_w_in, v_ffn_conv_w, v_ffn_conv_b, v_ffn_w_out, v_final_g):
    weights = dict(ada_w=ada_w, ada_b=ada_b, fox_w_in=fox_w_in, fox_b_f=fox_b_f, fox_w_o=fox_w_o, mla_w_a=mla_w_a,
                   mla_g_q=mla_g_q, mla_g_kv=mla_g_kv, mla_w_uq=mla_w_uq, mla_w_ukv=mla_w_ukv, mla_w_o=mla_w_o,
                   ffn_w_in=ffn_w_in, ffn_conv_w=ffn_conv_w, ffn_conv_b=ffn_conv_b, ffn_w_out=ffn_w_out, final_g=final_g)
    mom_m = dict(ada_w=m_ada_w, ada_b=m_ada_b, fox_w_in=m_fox_w_in, fox_b_f=m_fox_b_f, fox_w_o=m_fox_w_o, mla_w_a=m_mla_w_a,
                 mla_g_q=m_mla_g_q, mla_g_kv=m_mla_g_kv, mla_w_uq=m_mla_w_uq, mla_w_ukv=m_mla_w_ukv, mla_w_o=m_mla_w_o,
                 ffn_w_in=m_ffn_w_in, ffn_conv_w=m_ffn_conv_w, ffn_conv_b=m_ffn_conv_b, ffn_w_out=m_ffn_w_out, final_g=m_final_g)
    mom_v = dict(ada_w=v_ada_w, ada_b=v_ada_b, fox_w_in=v_fox_w_in, fox_b_f=v_fox_b_f, fox_w_o=v_fox_w_o, mla_w_a=v_mla_w_a,
                 mla_g_q=v_mla_g_q, mla_g_kv=v_mla_g_kv, mla_w_uq=v_mla_w_uq, mla_w_ukv=v_mla_w_ukv, mla_w_o=v_mla_w_o,
                 ffn_w_in=v_ffn_w_in, ffn_conv_w=v_ffn_conv_w, ffn_conv_b=v_ffn_conv_b, ffn_w_out=v_ffn_w_out, final_g=v_final_g)
    order = list(weights)
    x0 = x[0]
    target = loss_target[0]
    s = x0.shape[0]
    d = D_MODEL
    cols = ada_w.shape[-1]
    nq = N_HEADS * HEAD_DIM

    small_names = ["c"] + list(_SMALL)
    small_all = dict(zip(small_names, _exchange([c] + [weights[n] for n in _SMALL], same_src=True, name="gather_small")))
    c_all = small_all["c"].reshape(N_DEV, d)
    g_q = _gathered_to_full(small_all["mla_g_q"], 1)
    g_kv = _gathered_to_full(small_all["mla_g_kv"], 1)
    conv_w = _gathered_to_full(small_all["ffn_conv_w"], 2)

    c_pad = _pad_axis(c_all, 0, LANES)
    silu_c = _silu(c_pad, name="silu_c")
    w_ada = ada_w.reshape(4, d, cols)
    b_ada = ada_b.reshape(4, 1, cols)
    mods = [_matmul(silu_c, w_ada[i], name=f"ada_mod{i}")[:N_DEV] + b_ada[i] for i in range(4)]
    mod_send = _pad_axis(jnp.stack(mods, axis=1), 1, 8)
    mod_recv, = _exchange([mod_send], same_src=False, name="scatter_mod")
    mod = mod_recv[:, :4].transpose(1, 0, 2).reshape(4, 3 * d)
    shift = [mod[i:i + 1, 0:d] for i in range(4)]
    scale = [mod[i:i + 1, d:2 * d] for i in range(4)]
    gate = [mod[i:i + 1, 2 * d:3 * d] for i in range(4)]

    big_all = _gather_two_level([weights[n].astype(BF16) for n in _BIG], name="gather_weights")
    wfull = {n: _gathered_to_full(g, _BIG[n]) for n, g in zip(_BIG, big_all)}

    w_fox_in = _pad_axis(wfull["fox_w_in"][0], 1, LANES)
    w_fox_qkv, w_fox_f = w_fox_in[:, :3 * nq], w_fox_in[:, 3 * nq:]
    w_fox_o = wfull["fox_w_o"][0]
    w_a = _pad_axis(wfull["mla_w_a"][0], 1, LANES)
    wq = wfull["mla_w_uq"][0].reshape(MLA_Q_RANK, N_HEADS, HEAD_DIM + MLA_ROPE_DIM)
    w_uq = _pad_axis(wq, 2, LANES).reshape(MLA_Q_RANK, N_HEADS * LANES)
    wkv = wfull["mla_w_ukv"][0].reshape(MLA_KV_RANK, N_HEADS, 2 * HEAD_DIM)
    w_ukv = jnp.concatenate([wkv[:, :, :HEAD_DIM].reshape(MLA_KV_RANK, -1), wkv[:, :, HEAD_DIM:].reshape(MLA_KV_RANK, -1)], axis=1)
    w_mla_o = wfull["mla_w_o"][0]
    w_ffn_in = wfull["ffn_w_in"]
    w_ffn_out = wfull["ffn_w_out"]
    conv_b = ffn_conv_b

    fox_scale = HEAD_DIM ** -0.5
    mla_scale = (HEAD_DIM + MLA_ROPE_DIM) ** -0.5
    pos = jnp.arange(s, dtype=F32)
    inv_freq = ROPE_BASE ** (-jnp.arange(0, MLA_ROPE_DIM, 2, dtype=F32) / MLA_ROPE_DIM)
    ang = pos[:, None] * inv_freq[None, :]
    cos16, sin16 = jnp.cos(ang), jnp.sin(ang)
    z16, z32, z64 = jnp.zeros((s, 16), F32), jnp.zeros((s, 32), F32), jnp.zeros((s, 64), F32)
    tab_a = jnp.concatenate([jnp.ones((s, 64), F32), cos16, cos16, z32], axis=1) * (mla_scale * LOG2E)
    tab_b = jnp.concatenate([z64, -sin16, z16, z32], axis=1) * (mla_scale * LOG2E)
    tab_c = jnp.concatenate([z64, z16, sin16, z32], axis=1) * (mla_scale * LOG2E)

    h0 = _norm_fwd(x0, scale[0], shift[0], plus_one=True, out_dtype=BF16, name="ada_fwd0")
    q_mult = jnp.concatenate([jnp.full((1, nq), fox_scale * LOG2E, F32), jnp.ones((1, 2 * nq), F32)], axis=1)
    qkv = _matmul(h0, w_fox_qkv, out_dtype=BF16, col_scale=q_mult, name="fox_proj")
    fl = _matmul(h0, w_fox_f, name="fox_proj_f")[:, :N_HEADS]
    cum = _fox_gate_fwd(fl, fox_b_f, name="fox_gate_fwd")
    fox_cfg = dict(qc=0, kc=N_HEADS // 2, vc=N_HEADS, width=HEAD_DIM)
    kb2 = cum * -LOG2E
    att_blk = _blk(s, ATT_BLOCK)
    fvt = _matmul(w_fox_qkv[:, 2 * nq:], h0, ta=True, tb=True, out_dtype=BF16, name="fox_proj_vt")
    fo, fox_lse = _attn_fwd(qkv, qkv, _key_blocks(fvt, att_blk), _bias_lane_terms(kb2), qc=0, kc=N_HEADS // 2,
                            width=HEAD_DIM, name="fox_attn_fwd")
    fox_res = (qkv, qkv, qkv, kb2, fo, fox_lse)
    y0 = _matmul(fo, w_fox_o, name="fox_out")
    x1 = _resid_fwd(x0, y0, gate[0], name="resid_fwd0")

    def ffn_fwd(xin, li, sub):
        hh = _norm_fwd(xin, scale[sub], shift[sub], plus_one=True, out_dtype=BF16, name=f"ada_fwd{sub}")
        u = _matmul(hh, w_ffn_in[li], name=f"ffn_up{li}")
        g = _conv_gate_fwd(u, conv_w[li], conv_b[li:li + 1], name=f"conv_fwd{li}")
        y = _matmul(g, w_ffn_out[li], name=f"ffn_down{li}")
        return _resid_fwd(xin, y, gate[sub], name=f"resid_fwd{sub}"), (hh, u, g, y)

    x2, ffn0_res = ffn_fwd(x1, 0, 1)

    h2 = _norm_fwd(x2, scale[2], shift[2], plus_one=True, out_dtype=BF16, name="ada_fwd2")
    a = _matmul(h2, w_a, name="mla_a")
    a_q, a_kv = a[:, :MLA_Q_RANK], a[:, MLA_Q_RANK:MLA_Q_RANK + MLA_KV_RANK]
    kr1 = a[:, MLA_Q_RANK + MLA_KV_RANK:MLA_Q_RANK + MLA_KV_RANK + MLA_ROPE_HALF]
    kr2 = a[:, MLA_Q_RANK + MLA_KV_RANK + MLA_ROPE_HALF:MLA_Q_RANK + MLA_KV_RANK + MLA_ROPE_DIM]
    cq = _norm_fwd(a_q, g_q, jnp.zeros_like(g_q), plus_one=False, out_dtype=BF16, name="mla_norm_q")
    ckv = _norm_fwd(a_kv, g_kv, jnp.zeros_like(g_kv), plus_one=False, out_dtype=BF16, name="mla_norm_kv")
    qf = _matmul(cq, w_uq, name="mla_uq")
    kvf = _matmul(ckv, w_ukv, out_dtype=BF16, name="mla_ukv")
    mq = _rope_heads(qf, tab_a, tab_b, tab_c, out_dtype=BF16, name="rope_q")
    kk1, kk2 = _rope(kr1, kr2, cos16, sin16, negate=False, name="rope_k")
    k_tail = jnp.concatenate([kk1, kk2, z32], axis=1).astype(BF16)
    mk = jnp.concatenate([kvf[:, :nq].reshape(s, N_HEADS, HEAD_DIM),
                          jnp.broadcast_to(k_tail[:, None, :], (s, N_HEADS, HEAD_DIM))], axis=-1).reshape(s, N_HEADS * LANES)
    mla_cfg = dict(qc=0, kc=0, vc=N_HEADS // 2, width=LANES)
    mvt = _matmul(w_ukv[:, nq:], ckv, ta=True, tb=True, out_dtype=BF16, name="mla_ukv_vt")
    mo, mla_lse = _attn_fwd(mq, mk, _key_blocks(mvt, att_blk), None, qc=0, kc=0, width=LANES, name="mla_attn_fwd")
    mla_res = (mq, mk, kvf, None, mo, mla_lse)
    y2 = _matmul(mo, w_mla_o, name="mla_out")
    x3 = _resid_fwd(x2, y2, gate[2], name="resid_fwd2")

    x4, ffn1_res = ffn_fwd(x3, 1, 3)

    loss_vec, dx4, d_final_g = _final_loss(x4, final_g.reshape(1, d), target, name="final_loss")
    loss = lax.psum(loss_vec[0, 0], ("x", "y", "c"))

    grads = {}
    dmod = [None] * 4

    def ffn_bwd(dx_out, xin, li, sub, res):
        hh, u, g, y = res
        dy, dgate = _resid_bwd(dx_out, y, gate[sub], name=f"resid_bwd{sub}")
        gw_out = _matmul(g, dy, ta=True, out_dtype=BF16, name=f"ffn_down_dw{li}")
        dg = _matmul(dy, w_ffn_out[li], tb=True, name=f"ffn_down_dx{li}")
        du, dcw, dcb = _conv_gate_bwd(u, conv_w[li], conv_b[li:li + 1], dg, name=f"conv_bwd{li}")
        gw_in = _matmul(hh, du, ta=True, out_dtype=BF16, name=f"ffn_up_dw{li}")
        dh = _matmul(du, w_ffn_in[li], tb=True, out_dtype=BF16, name=f"ffn_up_dx{li}")
        dx_in, dscale, dshift = _norm_bwd(xin, scale[sub], dh, dx_out, plus_one=True, name=f"ada_bwd{sub}")
        dmod[sub] = jnp.concatenate([dshift, dscale, dgate], axis=1)
        return dx_in, gw_in, dcw, dcb, gw_out

    dx3, gw_in1, dcw1, dcb1, gw_out1 = ffn_bwd(dx4, x3, 1, 3, ffn1_res)

    dy2, dgate2 = _resid_bwd(dx3, y2, gate[2], name="resid_bwd2")
    grads["mla_w_o"] = _matmul(mo, dy2, ta=True, out_dtype=BF16, name="mla_out_dw")[None]
    dmo = _matmul(dy2, w_mla_o, tb=True, out_dtype=BF16, name="mla_out_dx")
    dmq, dmk, dmv, _ = _attention_bwd(mla_res, dmo, dq_mult=1.0 / LOG2E, dk_mult=1.0 / LOG2E, out_dtype=F32,
                                      name="mla_attn", **mla_cfg)
    dqf = _rope_heads(dmq, tab_a, -tab_b, -tab_c, out_dtype=BF16, name="rope_q_bwd")
    g_uq = _matmul(cq, dqf, ta=True, out_dtype=BF16, name="mla_uq_dw")
    dcq = _matmul(dqf, w_uq, tb=True, name="mla_uq_dx")
    dmk3 = dmk.reshape(s, N_HEADS, LANES)
    dkr = _group_sum(dmk, name="mla_krope_sum")
    dkr1, dkr2 = _rope(dkr[:, HEAD_DIM:HEAD_DIM + MLA_ROPE_HALF], dkr[:, HEAD_DIM + MLA_ROPE_HALF:HEAD_DIM + MLA_ROPE_DIM],
                       cos16, sin16, negate=True, name="rope_k_bwd")
    dkvf = jnp.concatenate([dmk3[:, :, :HEAD_DIM].reshape(s, nq).astype(BF16), dmv.astype(BF16)], axis=1)
    g_ukv = _matmul(ckv, dkvf, ta=True, out_dtype=BF16, name="mla_ukv_dw")
    dckv = _matmul(dkvf, w_ukv, tb=True, name="mla_ukv_dx")
    da_q, dg_q, _ = _norm_bwd(a_q, g_q, dcq, None, plus_one=False, name="mla_norm_q_bwd")
    da_kv, dg_kv, _ = _norm_bwd(a_kv, g_kv, dckv, None, plus_one=False, name="mla_norm_kv_bwd")
    da = jnp.concatenate([da_q, da_kv, dkr1, dkr2, jnp.zeros((s, w_a.shape[1] - 672), F32)], axis=1).astype(BF16)
    grads["mla_w_a"] = _matmul(h2, da, ta=True, out_dtype=BF16, name="mla_a_dw")[None, :, :672]
    dh2 = _matmul(da, w_a, tb=True, out_dtype=BF16, name="mla_a_dx")
    dx2, dscale2, dshift2 = _norm_bwd(x2, scale[2], dh2, dx3, plus_one=True, name="ada_bwd2")
    dmod[2] = jnp.concatenate([dshift2, dscale2, dgate2], axis=1)
    grads["mla_w_uq"] = g_uq.reshape(MLA_Q_RANK, N_HEADS, LANES)[:, :, :HEAD_DIM + MLA_ROPE_DIM].reshape(1, MLA_Q_RANK, -1)
    grads["mla_w_ukv"] = _per_head([g_ukv[:, :nq], g_ukv[:, nq:]], MLA_KV_RANK)[None]
    grads["mla_g_q"], grads["mla_g_kv"] = dg_q, dg_kv

    dx1, gw_in0, dcw0, dcb0, gw_out0 = ffn_bwd(dx2, x1, 0, 1, ffn0_res)
    grads["ffn_w_in"] = jnp.stack([gw_in0, gw_in1])
    grads["ffn_w_out"] = jnp.stack([gw_out0, gw_out1])
    grads["ffn_conv_w"] = jnp.stack([dcw0, dcw1])
    g_conv_b = jnp.concatenate([dcb0, dcb1], axis=0)

    dy0, dgate0 = _resid_bwd(dx1, y0, gate[0], name="resid_bwd0")
    grads["fox_w_o"] = _matmul(fo, dy0, ta=True, out_dtype=BF16, name="fox_out_dw")[None]
    dfo = _matmul(dy0, w_fox_o, tb=True, out_dtype=BF16, name="fox_out_dx")
    dfq, dfk, dfv, dcum = _attention_bwd(fox_res, dfo, dq_mult=fox_scale, dk_mult=1.0 / LOG2E, out_dtype=BF16,
                                         name="fox_attn", **fox_cfg)
    dfl, g_b_f = _fox_gate_bwd(fl, fox_b_f, dcum, name="fox_gate_bwd")
    dproj = jnp.concatenate([dfq.astype(BF16), dfk, dfv, _pad_axis(dfl, 1, LANES).astype(BF16)], axis=1)
    grads["fox_w_in"] = _matmul(h0, dproj, ta=True, out_dtype=BF16, name="fox_proj_dw")[None, :, :3 * nq + N_HEADS]
    dh0 = _matmul(dproj, w_fox_in, tb=True, out_dtype=BF16, name="fox_proj_dx")
    dx0, dscale0, dshift0 = _norm_bwd(x0, scale[0], dh0, dx1, plus_one=True, name="ada_bwd0")
    dmod[0] = jnp.concatenate([dshift0, dscale0, dgate0], axis=1)

    dmod_send = _pad_axis(jnp.stack(dmod, axis=0).reshape(4, N_DEV, cols).transpose(1, 0, 2), 1, 8)
    dmod_recv, = _exchange([dmod_send], same_src=False, name="scatter_dmod")
    dmod_all = dmod_recv[:, :4]
    dmod_pad = _pad_axis(dmod_all, 0, LANES)
    g_ada_w = jnp.stack([_matmul(silu_c, dmod_pad[:, i], ta=True, name=f"ada_dw{i}") for i in range(4)])
    grads["ada_w"] = g_ada_w.reshape(ada_w.shape)
    grads["ada_b"] = _sum_slabs(dmod_recv, name="ada_db")[:4].reshape(ada_b.shape)

    sharded = list(_BIG) + list(_SMALL)
    axes = {**_BIG, **_SMALL}
    recv = _exchange([_full_to_chunks(grads[n], axes[n]) for n in sharded], same_src=False, name="scatter_grads")
    grads.update(dict(zip(sharded, recv)))
    repl = _exchange([g_b_f, g_conv_b, d_final_g], same_src=True, name="gather_repl_grads")
    grads.update(dict(zip(_REPL, repl)))

    grad_out, deltas, new_m, new_v = {}, {}, {}, {}
    for n in order:
        grad_out[n], deltas[n], new_m[n], new_v[n] = _adamw(
            weights[n], grads[n], mom_m[n], mom_v[n], slabs=n in axes or n in _REPL, name=f"adamw_{n}")

    grad_x = dx0[None]
    return (loss, grad_x, *[grad_out[n] for n in order], *[deltas[n] for n in order],
            *[new_m[n] for n in order], *[new_v[n] for n in order])
```

```python
import jax
import jax.numpy as jnp
from jax import lax
from jax.experimental import pallas as pl
from jax.experimental.pallas import tpu as pltpu

F32 = jnp.float32
BF16 = jnp.bfloat16
HIGHEST = lax.Precision.HIGHEST

N_DEV = 8
D_MODEL = 1024
N_HEADS = 16
HEAD_DIM = 64
MLA_ROPE_HALF = 16
MLA_Q_RANK = 384
MLA_KV_RANK = 256
MLA_ROPE_DIM = 32
NORM_EPS = 1e-6
ROPE_BASE = 10000.0
ADAM_LR = 0.001
ADAM_B1 = 0.9
ADAM_B2 = 0.999
ADAM_EPS = 1e-08
ADAM_WD = 0.01
ADAM_STEP = 10

LANES = 128
VMEM_LIMIT_BYTES = 56 * 1024 * 1024
ROW_BLOCK = 512
ATT_BLOCK = 512
CONV_ROWS = 1024
MM_BM, MM_BN, MM_BK = 512, 1408, 2048
MM_K_WHOLE = 3328
LOG2E = 1.4426950408889634


def _params(*sem):
    return pltpu.CompilerParams(dimension_semantics=sem or None, vmem_limit_bytes=VMEM_LIMIT_BYTES)


def _blk(dim, pref):
    if dim <= pref:
        return dim
    b = pref - pref % LANES
    while b >= LANES:
        if dim % b == 0:
            return b
        b -= LANES
    raise ValueError(f"no block for {dim}")


def _row_blk(rows, pref):
    if rows <= pref:
        return rows
    for b in range(pref - pref % 8, 7, -8):
        if rows % b == 0:
            return b
    return rows


def _pad_axis(a, axis, mult):
    pad = (-a.shape[axis]) % mult
    if pad == 0:
        return a
    widths = [(0, 0)] * a.ndim
    widths[axis] = (0, pad)
    return jnp.pad(a, widths)


def _matmul(a, b, *, ta=False, tb=False, out_dtype=F32, col_scale=None, name):
    m, k = (a.shape[1], a.shape[0]) if ta else a.shape
    n = b.shape[0] if tb else b.shape[1]
    assert (b.shape[1] if tb else b.shape[0]) == k, (a.shape, b.shape, ta, tb)
    bm, bn = _blk(m, MM_BM if ta else 2 * MM_BM), _blk(n, MM_BN)
    bk = k if k <= MM_K_WHOLE else _blk(k, MM_BK)
    nk = k // bk
    dims = (((0 if ta else 1,), (1 if tb else 0,)), ((), ()))
    has_scale = col_scale is not None
    use_acc = nk > 1 and (out_dtype != F32 or has_scale)

    def body(*refs):
        a_ref, b_ref = refs[0], refs[1]
        s_ref = refs[2] if has_scale else None
        o_ref = refs[3] if has_scale else refs[2]
        acc_ref = refs[-1] if use_acc else o_ref
        kk = pl.program_id(2)
        part = lax.dot_general(a_ref[...].astype(BF16), b_ref[...].astype(BF16), dims, preferred_element_type=F32)

        def finish(val):
            if has_scale:
                val = val * s_ref[...]
            o_ref[...] = val.astype(out_dtype)

        if nk == 1:
            finish(part)
            return

        @pl.when(kk == 0)
        def _():
            acc_ref[...] = part

        @pl.when(kk > 0)
        def _():
            acc_ref[...] += part

        if use_acc:
            @pl.when(kk == nk - 1)
            def _():
                finish(acc_ref[...])

    a_spec = pl.BlockSpec((bk, bm), lambda i, j, kk: (kk, i)) if ta else pl.BlockSpec((bm, bk), lambda i, j, kk: (i, kk))
    b_spec = pl.BlockSpec((bn, bk), lambda i, j, kk: (j, kk)) if tb else pl.BlockSpec((bk, bn), lambda i, j, kk: (kk, j))
    return pl.pallas_call(
        body, name=name, grid=(m // bm, n // bn, nk),
        in_specs=[a_spec, b_spec] + ([pl.BlockSpec((1, bn), lambda i, j, kk: (0, j))] if has_scale else []),
        out_specs=pl.BlockSpec((bm, bn), lambda i, j, kk: (i, j)),
        out_shape=jax.ShapeDtypeStruct((m, n), out_dtype),
        scratch_shapes=[pltpu.VMEM((bm, bn), F32)] if use_acc else [],
        compiler_params=_params("parallel", "parallel", "arbitrary"),
    )(*([a, b] + ([col_scale] if has_scale else [])))


def _norm_fwd(x, mul, add, *, plus_one, out_dtype, name):
    s, n = x.shape
    bs = _blk(s, ROW_BLOCK)

    def body(x_ref, m_ref, a_ref, o_ref):
        xv = x_ref[...]
        r = lax.rsqrt(jnp.mean(xv * xv, axis=-1, keepdims=True) + NORM_EPS)
        mv = m_ref[...] + 1.0 if plus_one else m_ref[...]
        o_ref[...] = (xv * r * mv + a_ref[...]).astype(out_dtype)

    row = pl.BlockSpec((bs, n), lambda i: (i, 0))
    vec = pl.BlockSpec((1, n), lambda i: (0, 0))
    return pl.pallas_call(
        body, name=name, grid=(s // bs,), in_specs=[row, vec, vec], out_specs=row,
        out_shape=jax.ShapeDtypeStruct((s, n), out_dtype), compiler_params=_params("parallel"),
    )(x, mul, add)


def _norm_bwd(x, mul, dy, dres, *, plus_one, name):
    s, n = x.shape
    bs = _blk(s, ROW_BLOCK)
    has_res = dres is not None

    def body(*refs):
        if has_res:
            x_ref, m_ref, dy_ref, dres_ref, dx_ref, dm_ref, da_ref = refs
        else:
            x_ref, m_ref, dy_ref, dx_ref, dm_ref, da_ref = refs
        xv = x_ref[...]
        dyv = dy_ref[...].astype(F32)
        r = lax.rsqrt(jnp.mean(xv * xv, axis=-1, keepdims=True) + NORM_EPS)
        xn = xv * r
        mv = m_ref[...] + 1.0 if plus_one else m_ref[...]
        g = dyv * mv
        dx = r * (g - xn * jnp.mean(g * xn, axis=-1, keepdims=True))
        if has_res:
            dx = dx + dres_ref[...]
        dx_ref[...] = dx

        @pl.when(pl.program_id(0) == 0)
        def _():
            dm_ref[...] = jnp.zeros_like(dm_ref)
            da_ref[...] = jnp.zeros_like(da_ref)

        dm_ref[...] += jnp.sum(dyv * xn, axis=0, keepdims=True)
        da_ref[...] += jnp.sum(dyv, axis=0, keepdims=True)

    row = pl.BlockSpec((bs, n), lambda i: (i, 0))
    vec = pl.BlockSpec((1, n), lambda i: (0, 0))
    ins = [x, mul, dy] + ([dres] if has_res else [])
    return pl.pallas_call(
        body, name=name, grid=(s // bs,),
        in_specs=[row, vec, row] + ([row] if has_res else []), out_specs=[row, vec, vec],
        out_shape=[jax.ShapeDtypeStruct((s, n), F32), jax.ShapeDtypeStruct((1, n), F32), jax.ShapeDtypeStruct((1, n), F32)],
        compiler_params=_params("arbitrary"),
    )(*ins)


def _resid_fwd(x, y, gate, *, name):
    s, n = x.shape
    bs = _blk(s, ROW_BLOCK)

    def body(x_ref, y_ref, g_ref, o_ref):
        o_ref[...] = x_ref[...] + g_ref[...] * y_ref[...]

    row = pl.BlockSpec((bs, n), lambda i: (i, 0))
    vec = pl.BlockSpec((1, n), lambda i: (0, 0))
    return pl.pallas_call(
        body, name=name, grid=(s // bs,), in_specs=[row, row, vec], out_specs=row,
        out_shape=jax.ShapeDtypeStruct((s, n), F32), compiler_params=_params("parallel"),
    )(x, y, gate)


def _resid_bwd(dx, y, gate, *, name):
    s, n = dx.shape
    bs = _blk(s, ROW_BLOCK)

    def body(dx_ref, y_ref, g_ref, dy_ref, dg_ref):
        dxv = dx_ref[...]
        dy_ref[...] = (g_ref[...] * dxv).astype(BF16)

        @pl.when(pl.program_id(0) == 0)
        def _():
            dg_ref[...] = jnp.zeros_like(dg_ref)

        dg_ref[...] += jnp.sum(dxv * y_ref[...], axis=0, keepdims=True)

    row = pl.BlockSpec((bs, n), lambda i: (i, 0))
    vec = pl.BlockSpec((1, n), lambda i: (0, 0))
    return pl.pallas_call(
        body, name=name, grid=(s // bs,), in_specs=[row, row, vec], out_specs=[row, vec],
        out_shape=[jax.ShapeDtypeStruct((s, n), BF16), jax.ShapeDtypeStruct((1, n), F32)],
        compiler_params=_params("arbitrary"),
    )(dx, y, gate)


def _final_loss(x, g, target, *, name):
    s, n = x.shape
    bs = _blk(s, ROW_BLOCK)

    def body(x_ref, g_ref, t_ref, loss_ref, dx_ref, dg_ref):
        xv = x_ref[...]
        r = lax.rsqrt(jnp.mean(xv * xv, axis=-1, keepdims=True) + NORM_EPS)
        xn = xv * r
        gv = g_ref[...]
        err = xn * gv - t_ref[...]
        dout = err * (1.0 / n)
        gg = dout * gv
        dx_ref[...] = r * (gg - xn * jnp.mean(gg * xn, axis=-1, keepdims=True))

        @pl.when(pl.program_id(0) == 0)
        def _():
            loss_ref[...] = jnp.zeros_like(loss_ref)
            dg_ref[...] = jnp.zeros_like(dg_ref)

        part = jnp.sum(jnp.sum(err * err, axis=-1, keepdims=True), axis=0, keepdims=True) * (0.5 / n)
        loss_ref[...] += jnp.broadcast_to(part, loss_ref.shape)
        dg_ref[...] += jnp.sum(dout * xn, axis=0, keepdims=True)

    row = pl.BlockSpec((bs, n), lambda i: (i, 0))
    vec = pl.BlockSpec((1, n), lambda i: (0, 0))
    return pl.pallas_call(
        body, name=name, grid=(s // bs,), in_specs=[row, vec, row],
        out_specs=[pl.BlockSpec((1, LANES), lambda i: (0, 0)), row, vec],
        out_shape=[jax.ShapeDtypeStruct((1, LANES), F32), jax.ShapeDtypeStruct((s, n), F32), jax.ShapeDtypeStruct((1, n), F32)],
        compiler_params=_params("arbitrary"),
    )(x, g, target)


def _lane_lt64(shape):
    return lax.broadcasted_iota(jnp.int32, shape, 1) < HEAD_DIM


def _keep_low(x):
    return jnp.where(_lane_lt64(x.shape), x.astype(F32), 0.0).astype(x.dtype)


def _keep_high(x):
    return jnp.where(_lane_lt64(x.shape), 0.0, x.astype(F32)).astype(x.dtype)


def _lane_merge(a, b):
    n = max(a.shape[0], b.shape[0])
    return jnp.where(_lane_lt64((n, LANES)), a, b)


def _pair(x, width, masked):
    if width == HEAD_DIM:
        return (_keep_low(x), _keep_high(x)) if masked else (x, x)
    return x[:, :LANES], x[:, LANES:]


def _qk_t(a, b):
    return lax.dot_general(a, b, (((1,), (1,)), ((), ())), preferred_element_type=F32)


def _attn_specs(s, blk, width, cols, resident):
    w = 2 * width
    if resident:
        return pl.BlockSpec((s, w), lambda p, i: (0, cols + p))
    return pl.BlockSpec((blk, w), lambda p, i: (i, cols + p))


BIAS_TERMS = 3


def _key_blocks(vt, blk):
    return vt.reshape(vt.shape[0], vt.shape[1] // blk, blk).transpose(1, 0, 2)


def _attn_fwd(q, k, vt, kbl, *, qc, kc, width, name):
    s = q.shape[0]
    blk = _blk(s, ATT_BLOCK)
    nb = s // blk
    has_bias = kbl is not None
    assert has_bias == (width == HEAD_DIM)

    def body(*refs):
        if has_bias:
            q_ref, k_ref, vt_ref, kbl_ref, o_ref, lse_ref = refs
        else:
            q_ref, k_ref, vt_ref, o_ref, lse_ref = refs
        i = pl.program_id(1)
        q2 = q_ref[...]
        if has_bias:
            lane = lax.broadcasted_iota(jnp.int32, (blk, LANES), 1)
            qf = q2.astype(F32)
            qh = (jnp.where(lane < HEAD_DIM, qf, jnp.where(lane < HEAD_DIM + BIAS_TERMS, 1.0, 0.0)).astype(BF16),
                  jnp.where(lane >= HEAD_DIM, qf, jnp.where(lane < BIAS_TERMS, 1.0, 0.0)).astype(BF16))
        else:
            qh = (q2[:, :LANES], q2[:, LANES:])

        def step(j, carry, nblk, diag):
            rows = pl.ds(pl.multiple_of(j * blk, blk), nblk * blk)
            vt1 = [jnp.concatenate([vt_ref[j + b], jnp.ones((16, blk), BF16)], axis=0) for b in range(nblk)]
            k2 = k_ref[rows, :]
            if has_bias:
                low = _lane_lt64(k2.shape)
                kf, bf = k2.astype(F32), kbl_ref[rows, :].astype(F32)
                kh = (jnp.where(low, kf, bf).astype(BF16), jnp.where(low, bf, kf).astype(BF16))
            else:
                kh = (k2[:, :LANES], k2[:, LANES:])
            out = []
            for hd in range(2):
                m, acc = carry[hd]
                st = _qk_t(kh[hd], qh[hd])
                if diag:
                    row = lax.broadcasted_iota(jnp.int32, (blk, blk), 0)
                    colq = lax.broadcasted_iota(jnp.int32, (blk, blk), 1)
                    st = jnp.where(row <= colq, st, -1e30)
                m_new = jnp.maximum(m, jnp.max(st, axis=0, keepdims=True))
                alpha = jnp.exp2(m - m_new)
                pt = jnp.exp2(st - m_new).astype(BF16)
                acc = alpha * acc
                for b in range(nblk):
                    acc = acc + jnp.dot(vt1[b], pt[b * blk:(b + 1) * blk], preferred_element_type=F32)
                out.append((m_new, acc))
            return tuple(out)

        one = (jnp.full((1, blk), -1e30, F32), jnp.zeros((LANES + 16, blk), F32))
        carry = lax.fori_loop(0, i // 4, lambda j, c: step(4 * j, c, 4, False), (one, one))
        carry = lax.fori_loop(0, (i % 4) // 2, lambda _, c: step(i - i % 4, c, 2, False), carry)
        carry = lax.fori_loop(0, i % 2, lambda _, c: step(i - 1, c, 1, False), carry)
        (ma, acca), (mb, accb) = step(i, carry, 1, True)
        la = jnp.max(acca[LANES:LANES + 8], axis=0, keepdims=True)
        lb = jnp.max(accb[LANES:LANES + 8], axis=0, keepdims=True)
        acca, accb = acca[0:LANES], accb[0:LANES]
        low = lax.broadcasted_iota(jnp.int32, (LANES, blk), 0) < HEAD_DIM
        o_ref[...] = jnp.where(low, acca / la, accb / lb).T
        lse_ref[0, 0] = ma + jnp.log(la) * LOG2E
        lse_ref[1, 0] = mb + jnp.log(lb) * LOG2E

    ins = [q, k, vt] + ([kbl] if has_bias else [])
    return pl.pallas_call(
        body, name=name, grid=(N_HEADS // 2, nb),
        in_specs=[_attn_specs(s, blk, width, qc, False), _attn_specs(s, blk, width, kc, True),
                  pl.BlockSpec((nb, LANES, blk), lambda p, i: (0, p, 0))]
                 + ([_attn_specs(s, blk, HEAD_DIM, 0, True)] if has_bias else []),
        out_specs=[pl.BlockSpec((blk, LANES), lambda p, i: (i, p)), pl.BlockSpec((2, 1, 1, blk), lambda p, i: (p, i, 0, 0))],
        out_shape=[jax.ShapeDtypeStruct((s, N_HEADS * HEAD_DIM), F32), jax.ShapeDtypeStruct((N_HEADS, nb, 1, blk), F32)],
        compiler_params=_params("parallel", "parallel"),
    )(*ins)


def _bias_lane_terms(kb2):
    s = kb2.shape[0]
    terms, rest = [], kb2
    for _ in range(BIAS_TERMS):
        t = lax.reduce_precision(rest, 8, 7)
        terms.append(t.astype(BF16))
        rest = rest - t
    t3 = jnp.stack(terms, axis=-1).reshape(s, N_HEADS // 2, 2, BIAS_TERMS)
    pad = jnp.zeros((s, N_HEADS // 2, HEAD_DIM - BIAS_TERMS), BF16)
    return jnp.concatenate([t3[:, :, 1], pad, t3[:, :, 0], pad], axis=-1).reshape(s, N_HEADS * HEAD_DIM)


def _causal_keep(n):
    row = lax.broadcasted_iota(jnp.int32, (n, n), 0)
    col = lax.broadcasted_iota(jnp.int32, (n, n), 1)
    return col <= row


def _attn_delta(o, do, *, name):
    s, n = o.shape
    bs = _blk(s, ROW_BLOCK)

    def body(o_ref, do_ref, d_ref):
        for g in range(n // LANES):
            prod = do_ref[:, g * LANES:(g + 1) * LANES].astype(F32) * o_ref[:, g * LANES:(g + 1) * LANES]
            low = _lane_lt64(prod.shape)
            d_ref[:, g * LANES:(g + 1) * LANES] = _lane_merge(
                jnp.sum(jnp.where(low, prod, 0.0), axis=-1, keepdims=True),
                jnp.sum(jnp.where(low, 0.0, prod), axis=-1, keepdims=True))

    row = pl.BlockSpec((bs, n), lambda i: (i, 0))
    return pl.pallas_call(
        body, name=name, grid=(s // bs,), in_specs=[row, row], out_specs=row,
        out_shape=jax.ShapeDtypeStruct((s, n), F32), compiler_params=_params("parallel"),
    )(o, do)


def _attn_bwd(q, k, v, kb_col, do, lse_row, delta_row, *, qc, kc, vc, width, dq_mult, dk_mult, out_dtype, name):
    s = q.shape[0]
    blk = _blk(s, ATT_BLOCK)
    nb = s // blk
    has_bias = kb_col is not None

    def body(*refs):
        if has_bias:
            q_ref, k_ref, v_ref, kb_ref, do_ref, lse_ref, dl_ref, dk_ref, dv_ref, db_ref, dq_ref, dr_ref = refs
        else:
            q_ref, k_ref, v_ref, do_ref, lse_ref, dl_ref, dk_ref, dv_ref, db_ref, dq_ref = refs
        j = pl.program_id(1)

        @pl.when(j == 0)
        def _():
            dq_ref[...] = jnp.zeros_like(dq_ref)
            if has_bias:
                dr_ref[...] = jnp.zeros_like(dr_ref)

        kh = _pair(k_ref[...], width, True)
        v2 = v_ref[...]
        vh = (_keep_low(v2), _keep_high(v2))
        if has_bias:
            kb2 = kb_ref[0]
            kbh = (kb2[:, 0:1], kb2[:, 1:2])

        def step(i, carry, nblk, diag):
            rows = pl.ds(pl.multiple_of(i * blk, blk), nblk * blk)
            qh = _pair(q_ref[rows, :], width, False)
            doi = do_ref[rows, :]
            out, dq_parts = [], []
            for hd in range(2):
                dk, dvv, db = carry[hd]
                st = _qk_t(kh[hd], qh[hd])
                if has_bias:
                    st = st + kbh[hd]
                if diag:
                    row = lax.broadcasted_iota(jnp.int32, (blk, blk), 0)
                    colq = lax.broadcasted_iota(jnp.int32, (blk, blk), 1)
                    st = jnp.where(row <= colq, st, -1e30)
                lse_i = jnp.concatenate([lse_ref[hd, i + b] for b in range(nblk)], axis=1)
                delta_i = jnp.concatenate([dl_ref[hd, i + b] for b in range(nblk)], axis=1)
                pt = jnp.exp2(st - lse_i)
                dvv = dvv + jnp.dot(pt.astype(BF16), doi, preferred_element_type=F32)
                dst = pt * (_qk_t(vh[hd], doi) - delta_i)
                dsb = dst.astype(BF16)
                dk = dk + jnp.dot(dsb, qh[hd], preferred_element_type=F32)
                db = db + jnp.sum(dst, axis=-1, keepdims=True)
                dq_parts.append(lax.dot_general(dsb, kh[hd], (((0,), (0,)), ((), ())), preferred_element_type=F32))
                if has_bias:
                    rsum = jnp.sum(dst, axis=0, keepdims=True)
                    for b in range(nblk):
                        dr_ref[hd, i + b] += rsum[:, b * blk:(b + 1) * blk]
                out.append((dk, dvv, db))
            if width == HEAD_DIM:
                dq_ref[rows, :] += (dq_parts[0] + dq_parts[1]) * dq_mult
            else:
                dq_ref[rows, 0:LANES] += dq_parts[0] * dq_mult
                dq_ref[rows, LANES:2 * LANES] += dq_parts[1] * dq_mult
            return tuple(out)

        one = (jnp.zeros((blk, LANES), F32), jnp.zeros((blk, LANES), F32), jnp.zeros((blk, 1), F32))
        carry = step(j, (one, one), 1, True)
        rest = nb - 1 - j
        carry = lax.fori_loop(0, rest // 4, lambda t, c: step(j + 1 + 4 * t, c, 4, False), carry)
        carry = lax.fori_loop(0, (rest % 4) // 2, lambda _, c: step(nb - rest % 4, c, 2, False), carry)
        (dka, dva, dba), (dkb, dvb, dbb) = lax.fori_loop(0, rest % 2, lambda _, c: step(nb - 1, c, 1, False), carry)
        if width == HEAD_DIM:
            dk = _lane_merge(dka, dkb)
        else:
            dk = jnp.concatenate([dka, dkb], axis=1)
        dk_ref[...] = (dk * dk_mult).astype(out_dtype)
        dv_ref[...] = _lane_merge(dva, dvb).astype(out_dtype)
        db_ref[...] = _lane_merge(dba, dbb)

    stat = pl.BlockSpec((blk, LANES), lambda p, jj: (jj, p))
    rows = pl.BlockSpec((2, nb, 1, blk), lambda p, jj: (p, 0, 0, 0))
    ins = [q, k, v] + ([kb_col] if has_bias else []) + [do, lse_row, delta_row]
    return pl.pallas_call(
        body, name=name, grid=(N_HEADS // 2, nb),
        in_specs=[_attn_specs(s, blk, width, qc, True), _attn_specs(s, blk, width, kc, False),
                  _attn_specs(s, blk, HEAD_DIM, vc, False)]
                 + ([pl.BlockSpec((1, blk, 2), lambda p, jj: (p, jj, 0))] if has_bias else [])
                 + [pl.BlockSpec((s, LANES), lambda p, jj: (0, p)), rows, rows],
        out_specs=[pl.BlockSpec((blk, 2 * width), lambda p, jj: (jj, p)), stat, stat,
                   pl.BlockSpec((s, 2 * width), lambda p, jj: (0, p))] + ([rows] if has_bias else []),
        out_shape=[jax.ShapeDtypeStruct((s, N_HEADS * width), out_dtype), jax.ShapeDtypeStruct((s, N_HEADS * HEAD_DIM), out_dtype),
                   jax.ShapeDtypeStruct((s, N_HEADS * HEAD_DIM), F32), jax.ShapeDtypeStruct((s, N_HEADS * width), F32)]
                  + ([jax.ShapeDtypeStruct((N_HEADS, nb, 1, blk), F32)] if has_bias else []),
        compiler_params=_params("parallel", "arbitrary"),
    )(*ins)


def _head_stat(t):
    return t[:, ::HEAD_DIM]


def _stat_rows(t16, blk):
    s = t16.shape[0]
    return t16.T.reshape(N_HEADS, s // blk, 1, blk)


def _attention_bwd(res, do, *, qc, kc, vc, width, dq_mult, dk_mult, out_dtype, name):
    q, k, v, bias, o, lse_row = res
    s = q.shape[0]
    blk = _blk(s, ATT_BLOCK)
    kb_col = None if bias is None else bias.reshape(s, N_HEADS // 2, 2).transpose(1, 0, 2)
    delta_row = _stat_rows(_head_stat(_attn_delta(o, do, name=name + "_delta")), blk)
    outs = _attn_bwd(q, k, v, kb_col, do, lse_row, delta_row, qc=qc, kc=kc, vc=vc, width=width, dq_mult=dq_mult,
                     dk_mult=dk_mult, out_dtype=out_dtype, name=name + "_bwd")
    dk, dv, dcol, dq = outs[:4]
    if bias is None:
        return dq, dk, dv, None
    return dq, dk, dv, outs[4].reshape(N_HEADS, s).T - _head_stat(dcol)


def _fox_gate_fwd(fl, bf, *, name):
    s, n = fl.shape
    bs = _blk(s, ROW_BLOCK)

    def body(fl_ref, bf_ref, cum_ref, carry_ref):
        @pl.when(pl.program_id(0) == 0)
        def _():
            carry_ref[...] = jnp.zeros_like(carry_ref)

        z = fl_ref[...] + bf_ref[...]
        lf = jnp.minimum(z, 0.0) - jnp.log1p(jnp.exp(-jnp.abs(z)))
        row = lax.broadcasted_iota(jnp.int32, (bs, bs), 0)
        col = lax.broadcasted_iota(jnp.int32, (bs, bs), 1)
        tri = (col <= row).astype(F32)
        cum_ref[...] = jnp.dot(tri, lf, preferred_element_type=F32, precision=HIGHEST) + carry_ref[...]
        carry_ref[...] += jnp.sum(lf, axis=0, keepdims=True)

    return pl.pallas_call(
        body, name=name, grid=(s // bs,),
        in_specs=[pl.BlockSpec((bs, n), lambda i: (i, 0)), pl.BlockSpec((1, n), lambda i: (0, 0))],
        out_specs=pl.BlockSpec((bs, n), lambda i: (i, 0)),
        out_shape=jax.ShapeDtypeStruct((s, n), F32), scratch_shapes=[pltpu.VMEM((1, n), F32)],
        compiler_params=_params("arbitrary"),
    )(fl, bf)


def _fox_gate_bwd(fl, bf, dcum, *, name):
    s, n = fl.shape
    bs = _blk(s, ROW_BLOCK)
    nb = s // bs

    def body(fl_ref, bf_ref, dc_ref, dz_ref, dbf_ref, carry_ref):
        @pl.when(pl.program_id(0) == 0)
        def _():
            carry_ref[...] = jnp.zeros_like(carry_ref)
            dbf_ref[...] = jnp.zeros_like(dbf_ref)

        dc = dc_ref[...]
        row = lax.broadcasted_iota(jnp.int32, (bs, bs), 0)
        col = lax.broadcasted_iota(jnp.int32, (bs, bs), 1)
        tri = (col >= row).astype(F32)
        dlf = jnp.dot(tri, dc, preferred_element_type=F32, precision=HIGHEST) + carry_ref[...]
        carry_ref[...] += jnp.sum(dc, axis=0, keepdims=True)
        z = fl_ref[...] + bf_ref[...]
        dz = dlf / (1.0 + jnp.exp(z))
        dz_ref[...] = dz
        dbf_ref[...] += jnp.sum(dz, axis=0, keepdims=True)

    rev = pl.BlockSpec((bs, n), lambda i: (nb - 1 - i, 0))
    vec = pl.BlockSpec((1, n), lambda i: (0, 0))
    return pl.pallas_call(
        body, name=name, grid=(nb,), in_specs=[rev, vec, rev], out_specs=[rev, vec],
        out_shape=[jax.ShapeDtypeStruct((s, n), F32), jax.ShapeDtypeStruct((1, n), F32)],
        scratch_shapes=[pltpu.VMEM((1, n), F32)], compiler_params=_params("arbitrary"),
    )(fl, bf, dcum)


def _rope(x1, x2, cos, sin, *, negate, name):
    s, n = x1.shape
    bs = _blk(s, ROW_BLOCK)

    def body(a_ref, b_ref, c_ref, s_ref, o1_ref, o2_ref):
        a, b, cv = a_ref[...], b_ref[...], c_ref[...]
        sv = -s_ref[...] if negate else s_ref[...]
        o1_ref[...] = a * cv - b * sv
        o2_ref[...] = b * cv + a * sv

    row = pl.BlockSpec((bs, n), lambda i: (i, 0))
    return pl.pallas_call(
        body, name=name, grid=(s // bs,), in_specs=[row] * 4, out_specs=[row, row],
        out_shape=[jax.ShapeDtypeStruct((s, n), F32)] * 2, compiler_params=_params("parallel"),
    )(x1, x2, cos, sin)


def _rope_heads(x, ta, tb, tc, *, out_dtype, name):
    s, n = x.shape
    bs = _blk(s, ROW_BLOCK)

    def body(x_ref, a_ref, b_ref, c_ref, o_ref):
        av, bv, cv = a_ref[...], b_ref[...], c_ref[...]
        for g in range(n // LANES):
            xg = x_ref[:, g * LANES:(g + 1) * LANES]
            og = xg * av + pltpu.roll(xg, LANES - MLA_ROPE_HALF, 1) * bv + pltpu.roll(xg, MLA_ROPE_HALF, 1) * cv
            o_ref[:, g * LANES:(g + 1) * LANES] = og.astype(out_dtype)

    row = pl.BlockSpec((bs, n), lambda i: (i, 0))
    tab = pl.BlockSpec((bs, LANES), lambda i: (i, 0))
    return pl.pallas_call(
        body, name=name, grid=(s // bs,), in_specs=[row, tab, tab, tab], out_specs=row,
        out_shape=jax.ShapeDtypeStruct((s, n), out_dtype), compiler_params=_params("parallel"),
    )(x, ta, tb, tc)


def _group_sum(x, *, name):
    s, n = x.shape
    bs = _blk(s, ROW_BLOCK)

    def body(x_ref, o_ref):
        acc = x_ref[:, 0:LANES]
        for g in range(1, n // LANES):
            acc = acc + x_ref[:, g * LANES:(g + 1) * LANES]
        o_ref[...] = acc

    return pl.pallas_call(
        body, name=name, grid=(s // bs,), in_specs=[pl.BlockSpec((bs, n), lambda i: (i, 0))],
        out_specs=pl.BlockSpec((bs, LANES), lambda i: (i, 0)),
        out_shape=jax.ShapeDtypeStruct((s, LANES), F32), compiler_params=_params("parallel"),
    )(x)


def _shift_down(x, k):
    return pltpu.roll(x, k, 0)


def _conv_rows(ext, w_ref, b_ref, rows):
    y = b_ref[...] + w_ref[0:1, :] * _shift_down(ext, 2) + w_ref[1:2, :] * _shift_down(ext, 1) + w_ref[2:3, :] * ext
    return y[8:8 + rows]


def _conv_gate_fwd(u, cw, cb, *, name):
    s, f2 = u.shape
    f = f2 // 2
    nf = f // LANES
    r = _blk(s, CONV_ROWS)
    r8 = r // 8

    def body(ug_ref, ugp_ref, uv_ref, uvp_ref, wg_ref, wv_ref, bg_ref, bv_ref, o_ref):
        first = pl.program_id(1) == 0

        def conv(cur_ref, prev_ref, w_ref, b_ref):
            prev = jnp.where(first, 0.0, prev_ref[...])
            return _conv_rows(jnp.concatenate([prev, cur_ref[...]], axis=0), w_ref, b_ref, r)

        yg = conv(ug_ref, ugp_ref, wg_ref, bg_ref)
        yv = conv(uv_ref, uvp_ref, wv_ref, bv_ref)
        o_ref[...] = (yg * jax.nn.sigmoid(yg) * yv).astype(BF16)

    def cur(off):
        return pl.BlockSpec((r, LANES), lambda c, i: (i, c + off))

    def prev(off):
        return pl.BlockSpec((8, LANES), lambda c, i: (jnp.maximum(i * r8 - 1, 0), c + off))

    def wspec(rows, off):
        return pl.BlockSpec((rows, LANES), lambda c, i: (0, c + off))

    return pl.pallas_call(
        body, name=name, grid=(nf, s // r),
        in_specs=[cur(0), prev(0), cur(nf), prev(nf), wspec(3, 0), wspec(3, nf), wspec(1, 0), wspec(1, nf)],
        out_specs=pl.BlockSpec((r, LANES), lambda c, i: (i, c)),
        out_shape=jax.ShapeDtypeStruct((s, f), BF16), compiler_params=_params("parallel", "parallel"),
    )(u, u, u, u, cw, cw, cb, cb)


def _conv_gate_bwd(u, cw, cb, dg, *, name):
    s, f2 = u.shape
    f = f2 // 2
    nf = f // LANES
    r = _blk(s, CONV_ROWS)
    r8 = r // 8
    nr = s // r

    def body(ug_ref, ugp_ref, ugn_ref, uv_ref, uvp_ref, uvn_ref, wg_ref, wv_ref, bg_ref, bv_ref, dg_ref, dgn_ref,
             dug_ref, duv_ref, dwg_ref, dwv_ref, dbg_ref, dbv_ref):
        i = pl.program_id(1)
        first, last = i == 0, i == nr - 1

        def ext_of(cur_ref, prev_ref, next_ref):
            prev = jnp.where(first, 0.0, prev_ref[...])
            return jnp.concatenate([prev, cur_ref[...], next_ref[...]], axis=0)

        eg, ev = ext_of(ug_ref, ugp_ref, ugn_ref), ext_of(uv_ref, uvp_ref, uvn_ref)
        yg = _conv_rows(eg, wg_ref, bg_ref, r + 8)
        yv = _conv_rows(ev, wv_ref, bv_ref, r + 8)
        dgn = jnp.where(last, 0.0, dgn_ref[...])
        dgx = jnp.concatenate([dg_ref[...], dgn], axis=0)
        sg = jax.nn.sigmoid(yg)
        dyg = dgx * yv * (sg * (1.0 + yg * (1.0 - sg)))
        dyv = dgx * (yg * sg)

        @pl.when(i == 0)
        def _():
            for ref in (dwg_ref, dwv_ref, dbg_ref, dbv_ref):
                ref[...] = jnp.zeros_like(ref)

        def grads(dy, ext, w_ref, du_ref, dw_ref, db_ref):
            n = r + 8
            du = w_ref[2:3, :] * dy + w_ref[1:2, :] * pltpu.roll(dy, n - 1, 0) + w_ref[0:1, :] * pltpu.roll(dy, n - 2, 0)
            du_ref[...] = du[0:r].astype(BF16)
            dyc = dy[0:r]
            db_ref[...] += jnp.sum(dyc, axis=0, keepdims=True)
            ext_c = ext[0:r + 8]
            dw_ref[0:1, :] += jnp.sum(dyc * _shift_down(ext_c, 2)[8:], axis=0, keepdims=True)
            dw_ref[1:2, :] += jnp.sum(dyc * _shift_down(ext_c, 1)[8:], axis=0, keepdims=True)
            dw_ref[2:3, :] += jnp.sum(dyc * ext_c[8:], axis=0, keepdims=True)

        grads(dyg, eg, wg_ref, dug_ref, dwg_ref, dbg_ref)
        grads(dyv, ev, wv_ref, duv_ref, dwv_ref, dbv_ref)

    def cur(off):
        return pl.BlockSpec((r, LANES), lambda c, i: (i, c + off))

    def prev(off):
        return pl.BlockSpec((8, LANES), lambda c, i: (jnp.maximum(i * r8 - 1, 0), c + off))

    def nxt(off):
        return pl.BlockSpec((8, LANES), lambda c, i: (jnp.minimum((i + 1) * r8, s // 8 - 1), c + off))

    def wspec(rows, off):
        return pl.BlockSpec((rows, LANES), lambda c, i: (0, c + off))

    outs = pl.pallas_call(
        body, name=name, grid=(nf, nr),
        in_specs=[cur(0), prev(0), nxt(0), cur(nf), prev(nf), nxt(nf), wspec(3, 0), wspec(3, nf), wspec(1, 0), wspec(1, nf),
                  cur(0), nxt(0)],
        out_specs=[cur(0), cur(0), wspec(3, 0), wspec(3, 0), wspec(1, 0), wspec(1, 0)],
        out_shape=[jax.ShapeDtypeStruct((s, f), BF16), jax.ShapeDtypeStruct((s, f), BF16),
                   jax.ShapeDtypeStruct((3, f), F32), jax.ShapeDtypeStruct((3, f), F32),
                   jax.ShapeDtypeStruct((1, f), F32), jax.ShapeDtypeStruct((1, f), F32)],
        compiler_params=_params("parallel", "arbitrary"),
    )(u, u, u, u, u, u, cw, cw, cb, cb, dg, dg)
    dug, duv, dwg, dwv, dbg, dbv = outs
    return jnp.concatenate([dug, duv], axis=1), jnp.concatenate([dwg, dwv], axis=1), jnp.concatenate([dbg, dbv], axis=1)


def _adamw(w, g, m, v, *, slabs, name):
    shape = w.shape
    cols = shape[-1]
    rows = w.size // cols
    w2, m2, v2 = (t.reshape(rows, cols) for t in (w, m, v))
    g2 = g.reshape((N_DEV, rows, cols) if slabs else (rows, cols))
    br = _row_blk(rows, ROW_BLOCK // 2 if slabs else ROW_BLOCK)

    def body(w_ref, g_ref, m_ref, v_ref, go_ref, d_ref, nm_ref, nv_ref):
        if slabs:
            gv = g_ref[0].astype(F32)
            for p in range(1, N_DEV):
                gv = gv + g_ref[p].astype(F32)
        else:
            gv = g_ref[...]
        nm = ADAM_B1 * m_ref[...] + (1.0 - ADAM_B1) * gv
        nv = ADAM_B2 * v_ref[...] + (1.0 - ADAM_B2) * (gv * gv)
        m_hat = nm / (1.0 - ADAM_B1 ** ADAM_STEP)
        v_hat = nv / (1.0 - ADAM_B2 ** ADAM_STEP)
        go_ref[...] = gv
        d_ref[...] = -ADAM_LR * (m_hat / (jnp.sqrt(v_hat) + ADAM_EPS) + ADAM_WD * w_ref[...])
        nm_ref[...] = nm
        nv_ref[...] = nv

    spec = pl.BlockSpec((br, cols), lambda i: (i, 0))
    gspec = pl.BlockSpec((N_DEV, br, cols), lambda i: (0, i, 0)) if slabs else spec
    outs = pl.pallas_call(
        body, name=name, grid=(rows // br,), in_specs=[spec, gspec, spec, spec], out_specs=[spec] * 4,
        out_shape=[jax.ShapeDtypeStruct((rows, cols), F32)] * 4, compiler_params=_params("parallel"),
    )(w2, g2, m2, v2)
    return tuple(t.reshape(shape) for t in outs)


def _exchange(xs, *, same_src, name):
    n = len(xs)
    slabs = [x.shape if same_src else x.shape[1:] for x in xs]

    def body(*refs):
        x_refs, o_refs = refs[:n], refs[n:2 * n]
        send_sems, recv_sems, loc_sems = refs[2 * n:]
        ix, iy, ic = lax.axis_index("x"), lax.axis_index("y"), lax.axis_index("c")
        me = 4 * ix + 2 * iy + ic
        local, sends, recvs = [], [], []
        for a in range(n):
            def src(p, a=a):
                return x_refs[a] if same_src else x_refs[a].at[p]

            local.append(pltpu.make_async_copy(src(me), o_refs[a].at[me], loc_sems.at[a]))
            for k in (1, 2, 4, 3, 5, 6, 7):
                px = 1 - ix if k & 4 else ix
                py = 1 - iy if k & 2 else iy
                pc = 1 - ic if k & 1 else ic
                p = 4 * px + 2 * py + pc
                for dst, out in ((me, sends), (p, recvs)):
                    out.append(pltpu.make_async_remote_copy(
                        src_ref=src(p), dst_ref=o_refs[a].at[dst], send_sem=send_sems.at[a, k - 1],
                        recv_sem=recv_sems.at[a, k - 1], device_id=(px, py, pc), device_id_type=pl.DeviceIdType.MESH))
        for cp in local + sends:
            cp.start()
        for cp in recvs:
            cp.wait_recv()
        for cp in sends:
            cp.wait_send()
        for cp in local:
            cp.wait()

    return pl.pallas_call(
        body, name=name,
        in_specs=[pl.BlockSpec(memory_space=pl.ANY)] * n, out_specs=[pl.BlockSpec(memory_space=pl.ANY)] * n,
        out_shape=[jax.ShapeDtypeStruct((N_DEV,) + tuple(sl), x.dtype) for sl, x in zip(slabs, xs)],
        scratch_shapes=[pltpu.SemaphoreType.DMA((n, N_DEV - 1)), pltpu.SemaphoreType.DMA((n, N_DEV - 1)),
                        pltpu.SemaphoreType.DMA((n,))],
        compiler_params=pltpu.CompilerParams(has_side_effects=True, vmem_limit_bytes=VMEM_LIMIT_BYTES),
    )(*xs)


def _gather_two_level(xs, *, name):
    n = len(xs)

    def body(*refs):
        x_refs, o_refs = refs[:n], refs[n:2 * n]
        send_sems, recv_sems, loc_sems = refs[2 * n:]
        ix, iy, ic = lax.axis_index("x"), lax.axis_index("y"), lax.axis_index("c")
        me = 4 * ix + 2 * iy + ic
        sib = 4 * ix + 2 * iy + (1 - ic)
        chips = [(1 - ix if ch & 2 else ix, 1 - iy if ch & 1 else iy) for ch in (1, 2, 3)]

        def copy(a, pos, src, slot, to):
            return pltpu.make_async_remote_copy(
                src_ref=src, dst_ref=o_refs[a].at[slot], send_sem=send_sems.at[a, pos], recv_sem=recv_sems.at[a, pos],
                device_id=to, device_id_type=pl.DeviceIdType.MESH)

        local = [pltpu.make_async_copy(x_refs[a], o_refs[a].at[me], loc_sems.at[a]) for a in range(n)]
        first, passed, arrive = [], [], []
        for a in range(n):
            first.append(copy(a, 0, x_refs[a], me, (ix, iy, 1 - ic)))
            arrive.append(copy(a, 0, x_refs[a], sib, (ix, iy, 1 - ic)))
            for ch, (px, py) in enumerate(chips, start=1):
                same, other = 4 * px + 2 * py + ic, 4 * px + 2 * py + (1 - ic)
                first.append(copy(a, 2 * ch - 1, x_refs[a], me, (px, py, ic)))
                passed.append((copy(a, 2 * ch - 1, x_refs[a], same, (px, py, ic)),
                               copy(a, 2 * ch, o_refs[a].at[same], same, (ix, iy, 1 - ic))))
                arrive.append(copy(a, 2 * ch, x_refs[a], other, (ix, iy, 1 - ic)))
        for cp in local + first:
            cp.start()
        for landed, onward in passed:
            landed.wait_recv()
            onward.start()
        for cp in arrive:
            cp.wait_recv()
        for cp in first + [onward for _, onward in passed]:
            cp.wait_send()
        for cp in local:
            cp.wait()

    return pl.pallas_call(
        body, name=name,
        in_specs=[pl.BlockSpec(memory_space=pl.ANY)] * n, out_specs=[pl.BlockSpec(memory_space=pl.ANY)] * n,
        out_shape=[jax.ShapeDtypeStruct((N_DEV,) + tuple(x.shape), x.dtype) for x in xs],
        scratch_shapes=[pltpu.SemaphoreType.DMA((n, N_DEV - 1)), pltpu.SemaphoreType.DMA((n, N_DEV - 1)),
                        pltpu.SemaphoreType.DMA((n,))],
        compiler_params=pltpu.CompilerParams(has_side_effects=True, vmem_limit_bytes=VMEM_LIMIT_BYTES),
    )(*xs)


def _sum_slabs(x, *, name):
    n, r, c = x.shape
    br = _row_blk(r, ROW_BLOCK)

    def body(x_ref, o_ref):
        acc = x_ref[0]
        for p in range(1, n):
            acc = acc + x_ref[p]
        o_ref[...] = acc

    return pl.pallas_call(
        body, name=name, grid=(r // br,), in_specs=[pl.BlockSpec((n, br, c), lambda i: (0, i, 0))],
        out_specs=pl.BlockSpec((br, c), lambda i: (i, 0)),
        out_shape=jax.ShapeDtypeStruct((r, c), F32), compiler_params=_params("parallel"),
    )(x)


def _silu(x, *, name):
    def body(x_ref, o_ref):
        xv = x_ref[...]
        o_ref[...] = (xv * jax.nn.sigmoid(xv)).astype(BF16)

    return pl.pallas_call(body, name=name, out_shape=jax.ShapeDtypeStruct(x.shape, BF16),
                          compiler_params=_params())(x)


_BIG = {"fox_w_in": 2, "fox_w_o": 1, "mla_w_a": 1, "mla_w_uq": 2, "mla_w_ukv": 2, "mla_w_o": 1, "ffn_w_in": 2, "ffn_w_out": 1}
_SMALL = {"mla_g_q": 1, "mla_g_kv": 1, "ffn_conv_w": 2}
_REPL = ("fox_b_f", "ffn_conv_b", "final_g")


def _gathered_to_full(g, axis):
    full = jnp.moveaxis(g, 0, axis)
    shape = list(full.shape)
    shape[axis:axis + 2] = [shape[axis] * shape[axis + 1]]
    return full.reshape(shape)


def _full_to_chunks(full, axis):
    shape = list(full.shape)
    shape[axis:axis + 1] = [N_DEV, shape[axis] // N_DEV]
    return jnp.moveaxis(full.reshape(shape), axis, 0)


def _per_head(parts, s_or_rows):
    return jnp.concatenate([p.reshape(s_or_rows, N_HEADS, -1) for p in parts], axis=-1).reshape(s_or_rows, -1)


def kernel(x, c, ada_w, ada_b, fox_w_in, fox_b_f, fox_w_o, mla_w_a, mla_g_q, mla_g_kv, mla_w_uq, mla_w_ukv, mla_w_o, ffn_w_in, ffn_conv_w, ffn_conv_b, ffn_w_out, final_g, loss_target, m_ada_w, m_ada_b, m_fox_w_in, m_fox_b_f, m_fox_w_o, m_mla_w_a, m_mla_g_q, m_mla_g_kv, m_mla_w_uq, m_mla_w_ukv, m_mla_w_o, m_ffn_w_in, m_ffn_conv_w, m_ffn_conv_b, m_ffn_w_out, m_final_g, v_ada_w, v_ada_b, v_fox_w_in, v_fox_b_f, v_fox_w_o, v_mla_w_a, v_mla_g_q, v_mla_g_kv, v_mla_w_uq, v_mla_w_ukv, v_mla_w_o, v_ffn_w_in, v_ffn_conv_w, v_ffn_conv_b, v_ffn_w_out, v_final_g):
    weights = dict(ada_w=ada_w, ada_b=ada_b, fox_w_in=fox_w_in, fox_b_f=fox_b_f, fox_w_o=fox_w_o, mla_w_a=mla_w_a,
                   mla_g_q=mla_g_q, mla_g_kv=mla_g_kv, mla_w_uq=mla_w_uq, mla_w_ukv=mla_w_ukv, mla_w_o=mla_w_o,
                   ffn_w_in=ffn_w_in, ffn_conv_w=ffn_conv_w, ffn_conv_b=ffn_conv_b, ffn_w_out=ffn_w_out, final_g=final_g)
    mom_m = dict(ada_w=m_ada_w, ada_b=m_ada_b, fox_w_in=m_fox_w_in, fox_b_f=m_fox_b_f, fox_w_o=m_fox_w_o, mla_w_a=m_mla_w_a,
                 mla_g_q=m_mla_g_q, mla_g_kv=m_mla_g_kv, mla_w_uq=m_mla_w_uq, mla_w_ukv=m_mla_w_ukv, mla_w_o=m_mla_w_o,
                 ffn_w_in=m_ffn_w_in, ffn_conv_w=m_ffn_conv_w, ffn_conv_b=m_ffn_conv_b, ffn_w_out=m_ffn_w_out, final_g=m_final_g)
    mom_v = dict(ada_w=v_ada_w, ada_b=v_ada_b, fox_w_in=v_fox_w_in, fox_b_f=v_fox_b_f, fox_w_o=v_fox_w_o, mla_w_a=v_mla_w_a,
                 mla_g_q=v_mla_g_q, mla_g_kv=v_mla_g_kv, mla_w_uq=v_mla_w_uq, mla_w_ukv=v_mla_w_ukv, mla_w_o=v_mla_w_o,
                 ffn_w_in=v_ffn_w_in, ffn_conv_w=v_ffn_conv_w, ffn_conv_b=v_ffn_conv_b, ffn_w_out=v_ffn_w_out, final_g=v_final_g)
    order = list(weights)
    x0 = x[0]
    target = loss_target[0]
    s = x0.shape[0]
    d = D_MODEL
    cols = ada_w.shape[-1]
    nq = N_HEADS * HEAD_DIM

    small_names = ["c"] + list(_SMALL)
    small_all = dict(zip(small_names, _exchange([c] + [weights[n] for n in _SMALL], same_src=True, name="gather_small")))
    c_all = small_all["c"].reshape(N_DEV, d)
    g_q = _gathered_to_full(small_all["mla_g_q"], 1)
    g_kv = _gathered_to_full(small_all["mla_g_kv"], 1)
    conv_w = _gathered_to_full(small_all["ffn_conv_w"], 2)

    c_pad = _pad_axis(c_all, 0, LANES)
    silu_c = _silu(c_pad, name="silu_c")
    w_ada = ada_w.reshape(4, d, cols)
    b_ada = ada_b.reshape(4, 1, cols)
    mods = [_matmul(silu_c, w_ada[i], name=f"ada_mod{i}")[:N_DEV] + b_ada[i] for i in range(4)]
    mod_send = _pad_axis(jnp.stack(mods, axis=1), 1, 8)
    mod_recv, = _exchange([mod_send], same_src=False, name="scatter_mod")
    mod = mod_recv[:, :4].transpose(1, 0, 2).reshape(4, 3 * d)
    shift = [mod[i:i + 1, 0:d] for i in range(4)]
    scale = [mod[i:i + 1, d:2 * d] for i in range(4)]
    gate = [mod[i:i + 1, 2 * d:3 * d] for i in range(4)]

    big_all = _gather_two_level([weights[n].astype(BF16) for n in _BIG], name="gather_weights")
    wfull = {n: _gathered_to_full(g, _BIG[n]) for n, g in zip(_BIG, big_all)}

    w_fox_in = _pad_axis(wfull["fox_w_in"][0], 1, LANES)
    w_fox_qkv, w_fox_f = w_fox_in[:, :3 * nq], w_fox_in[:, 3 * nq:]
    w_fox_o = wfull["fox_w_o"][0]
    w_a = _pad_axis(wfull["mla_w_a"][0], 1, LANES)
    wq = wfull["mla_w_uq"][0].reshape(MLA_Q_RANK, N_HEADS, HEAD_DIM + MLA_ROPE_DIM)
    w_uq = _pad_axis(wq, 2, LANES).reshape(MLA_Q_RANK, N_HEADS * LANES)
    wkv = wfull["mla_w_ukv"][0].reshape(MLA_KV_RANK, N_HEADS, 2 * HEAD_DIM)
    w_ukv = jnp.concatenate([wkv[:, :, :HEAD_DIM].reshape(MLA_KV_RANK, -1), wkv[:, :, HEAD_DIM:].reshape(MLA_KV_RANK, -1)], axis=1)
    w_mla_o = wfull["mla_w_o"][0]
    w_ffn_in = wfull["ffn_w_in"]
    w_ffn_out = wfull["ffn_w_out"]
    conv_b = ffn_conv_b

    fox_scale = HEAD_DIM ** -0.5
    mla_scale = (HEAD_DIM + MLA_ROPE_DIM) ** -0.5
    pos = jnp.arange(s, dtype=F32)
    inv_freq = ROPE_BASE ** (-jnp.arange(0, MLA_ROPE_DIM, 2, dtype=F32) / MLA_ROPE_DIM)
    ang = pos[:, None] * inv_freq[None, :]
    cos16, sin16 = jnp.cos(ang), jnp.sin(ang)
    z16, z32, z64 = jnp.zeros((s, 16), F32), jnp.zeros((s, 32), F32), jnp.zeros((s, 64), F32)
    tab_a = jnp.concatenate([jnp.ones((s, 64), F32), cos16, cos16, z32], axis=1) * (mla_scale * LOG2E)
    tab_b = jnp.concatenate([z64, -sin16, z16, z32], axis=1) * (mla_scale * LOG2E)
    tab_c = jnp.concatenate([z64, z16, sin16, z32], axis=1) * (mla_scale * LOG2E)

    h0 = _norm_fwd(x0, scale[0], shift[0], plus_one=True, out_dtype=BF16, name="ada_fwd0")
    q_mult = jnp.concatenate([jnp.full((1, nq), fox_scale * LOG2E, F32), jnp.ones((1, 2 * nq), F32)], axis=1)
    qkv = _matmul(h0, w_fox_qkv, out_dtype=BF16, col_scale=q_mult, name="fox_proj")
    fl = _matmul(h0, w_fox_f, name="fox_proj_f")[:, :N_HEADS]
    cum = _fox_gate_fwd(fl, fox_b_f, name="fox_gate_fwd")
    fox_cfg = dict(qc=0, kc=N_HEADS // 2, vc=N_HEADS, width=HEAD_DIM)
    kb2 = cum * -LOG2E
    att_blk = _blk(s, ATT_BLOCK)
    fvt = _matmul(w_fox_qkv[:, 2 * nq:], h0, ta=True, tb=True, out_dtype=BF16, name="fox_proj_vt")
    fo, fox_lse = _attn_fwd(qkv, qkv, _key_blocks(fvt, att_blk), _bias_lane_terms(kb2), qc=0, kc=N_HEADS // 2,
                            width=HEAD_DIM, name="fox_attn_fwd")
    fox_res = (qkv, qkv, qkv, kb2, fo, fox_lse)
    y0 = _matmul(fo, w_fox_o, name="fox_out")
    x1 = _resid_fwd(x0, y0, gate[0], name="resid_fwd0")

    def ffn_fwd(xin, li, sub):
        hh = _norm_fwd(xin, scale[sub], shift[sub], plus_one=True, out_dtype=BF16, name=f"ada_fwd{sub}")
        u = _matmul(hh, w_ffn_in[li], name=f"ffn_up{li}")
        g = _conv_gate_fwd(u, conv_w[li], conv_b[li:li + 1], name=f"conv_fwd{li}")
        y = _matmul(g, w_ffn_out[li], name=f"ffn_down{li}")
        return _resid_fwd(xin, y, gate[sub], name=f"resid_fwd{sub}"), (hh, u, g, y)

    x2, ffn0_res = ffn_fwd(x1, 0, 1)

    h2 = _norm_fwd(x2, scale[2], shift[2], plus_one=True, out_dtype=BF16, name="ada_fwd2")
    a = _matmul(h2, w_a, name="mla_a")
    a_q, a_kv = a[:, :MLA_Q_RANK], a[:, MLA_Q_RANK:MLA_Q_RANK + MLA_KV_RANK]
    kr1 = a[:, MLA_Q_RANK + MLA_KV_RANK:MLA_Q_RANK + MLA_KV_RANK + MLA_ROPE_HALF]
    kr2 = a[:, MLA_Q_RANK + MLA_KV_RANK + MLA_ROPE_HALF:MLA_Q_RANK + MLA_KV_RANK + MLA_ROPE_DIM]
    cq = _norm_fwd(a_q, g_q, jnp.zeros_like(g_q), plus_one=False, out_dtype=BF16, name="mla_norm_q")
    ckv = _norm_fwd(a_kv, g_kv, jnp.zeros_like(g_kv), plus_one=False, out_dtype=BF16, name="mla_norm_kv")
    qf = _matmul(cq, w_uq, name="mla_uq")
    kvf = _matmul(ckv, w_ukv, out_dtype=BF16, name="mla_ukv")
    mq = _rope_heads(qf, tab_a, tab_b, tab_c, out_dtype=BF16, name="rope_q")
    kk1, kk2 = _rope(kr1, kr2, cos16, sin16, negate=False, name="rope_k")
    k_tail = jnp.concatenate([kk1, kk2, z32], axis=1).astype(BF16)
    mk = jnp.concatenate([kvf[:, :nq].reshape(s, N_HEADS, HEAD_DIM),
                          jnp.broadcast_to(k_tail[:, None, :], (s, N_HEADS, HEAD_DIM))], axis=-1).reshape(s, N_HEADS * LANES)
    mla_cfg = dict(qc=0, kc=0, vc=N_HEADS // 2, width=LANES)
    mvt = _matmul(w_ukv[:, nq:], ckv, ta=True, tb=True, out_dtype=BF16, name="mla_ukv_vt")
    mo, mla_lse = _attn_fwd(mq, mk, _key_blocks(mvt, att_blk), None, qc=0, kc=0, width=LANES, name="mla_attn_fwd")
    mla_res = (mq, mk, kvf, None, mo, mla_lse)
    y2 = _matmul(mo, w_mla_o, name="mla_out")
    x3 = _resid_fwd(x2, y2, gate[2], name="resid_fwd2")

    x4, ffn1_res = ffn_fwd(x3, 1, 3)

    loss_vec, dx4, d_final_g = _final_loss(x4, final_g.reshape(1, d), target, name="final_loss")
    loss = lax.psum(loss_vec[0, 0], ("x", "y", "c"))

    grads = {}
    dmod = [None] * 4

    def ffn_bwd(dx_out, xin, li, sub, res):
        hh, u, g, y = res
        dy, dgate = _resid_bwd(dx_out, y, gate[sub], name=f"resid_bwd{sub}")
        gw_out = _matmul(g, dy, ta=True, out_dtype=BF16, name=f"ffn_down_dw{li}")
        dg = _matmul(dy, w_ffn_out[li], tb=True, name=f"ffn_down_dx{li}")
        du, dcw, dcb = _conv_gate_bwd(u, conv_w[li], conv_b[li:li + 1], dg, name=f"conv_bwd{li}")
        gw_in = _matmul(hh, du, ta=True, out_dtype=BF16, name=f"ffn_up_dw{li}")
        dh = _matmul(du, w_ffn_in[li], tb=True, out_dtype=BF16, name=f"ffn_up_dx{li}")
        dx_in, dscale, dshift = _norm_bwd(xin, scale[sub], dh, dx_out, plus_one=True, name=f"ada_bwd{sub}")
        dmod[sub] = jnp.concatenate([dshift, dscale, dgate], axis=1)
        return dx_in, gw_in, dcw, dcb, gw_out

    dx3, gw_in1, dcw1, dcb1, gw_out1 = ffn_bwd(dx4, x3, 1, 3, ffn1_res)

    dy2, dgate2 = _resid_bwd(dx3, y2, gate[2], name="resid_bwd2")
    grads["mla_w_o"] = _matmul(mo, dy2, ta=True, out_dtype=BF16, name="mla_out_dw")[None]
    dmo = _matmul(dy2, w_mla_o, tb=True, out_dtype=BF16, name="mla_out_dx")
    dmq, dmk, dmv, _ = _attention_bwd(mla_res, dmo, dq_mult=1.0 / LOG2E, dk_mult=1.0 / LOG2E, out_dtype=F32,
                                      name="mla_attn", **mla_cfg)
    dqf = _rope_heads(dmq, tab_a, -tab_b, -tab_c, out_dtype=BF16, name="rope_q_bwd")
    g_uq = _matmul(cq, dqf, ta=True, out_dtype=BF16, name="mla_uq_dw")
    dcq = _matmul(dqf, w_uq, tb=True, name="mla_uq_dx")
    dmk3 = dmk.reshape(s, N_HEADS, LANES)
    dkr = _group_sum(dmk, name="mla_krope_sum")
    dkr1, dkr2 = _rope(dkr[:, HEAD_DIM:HEAD_DIM + MLA_ROPE_HALF], dkr[:, HEAD_DIM + MLA_ROPE_HALF:HEAD_DIM + MLA_ROPE_DIM],
                       cos16, sin16, negate=True, name="rope_k_bwd")
    dkvf = jnp.concatenate([dmk3[:, :, :HEAD_DIM].reshape(s, nq).astype(BF16), dmv.astype(BF16)], axis=1)
    g_ukv = _matmul(ckv, dkvf, ta=True, out_dtype=BF16, name="mla_ukv_dw")
    dckv = _matmul(dkvf, w_ukv, tb=True, name="mla_ukv_dx")
    da_q, dg_q, _ = _norm_bwd(a_q, g_q, dcq, None, plus_one=False, name="mla_norm_q_bwd")
    da_kv, dg_kv, _ = _norm_bwd(a_kv, g_kv, dckv, None, plus_one=False, name="mla_norm_kv_bwd")
    da = jnp.concatenate([da_q, da_kv, dkr1, dkr2, jnp.zeros((s, w_a.shape[1] - 672), F32)], axis=1).astype(BF16)
    grads["mla_w_a"] = _matmul(h2, da, ta=True, out_dtype=BF16, name="mla_a_dw")[None, :, :672]
    dh2 = _matmul(da, w_a, tb=True, out_dtype=BF16, name="mla_a_dx")
    dx2, dscale2, dshift2 = _norm_bwd(x2, scale[2], dh2, dx3, plus_one=True, name="ada_bwd2")
    dmod[2] = jnp.concatenate([dshift2, dscale2, dgate2], axis=1)
    grads["mla_w_uq"] = g_uq.reshape(MLA_Q_RANK, N_HEADS, LANES)[:, :, :HEAD_DIM + MLA_ROPE_DIM].reshape(1, MLA_Q_RANK, -1)
    grads["mla_w_ukv"] = _per_head([g_ukv[:, :nq], g_ukv[:, nq:]], MLA_KV_RANK)[None]
    grads["mla_g_q"], grads["mla_g_kv"] = dg_q, dg_kv

    dx1, gw_in0, dcw0, dcb0, gw_out0 = ffn_bwd(dx2, x1, 0, 1, ffn0_res)
    grads["ffn_w_in"] = jnp.stack([gw_in0, gw_in1])
    grads["ffn_w_out"] = jnp.stack([gw_out0, gw_out1])
    grads["ffn_conv_w"] = jnp.stack([dcw0, dcw1])
    g_conv_b = jnp.concatenate([dcb0, dcb1], axis=0)

    dy0, dgate0 = _resid_bwd(dx1, y0, gate[0], name="resid_bwd0")
    grads["fox_w_o"] = _matmul(fo, dy0, ta=True, out_dtype=BF16, name="fox_out_dw")[None]
    dfo = _matmul(dy0, w_fox_o, tb=True, out_dtype=BF16, name="fox_out_dx")
    dfq, dfk, dfv, dcum = _attention_bwd(fox_res, dfo, dq_mult=fox_scale, dk_mult=1.0 / LOG2E, out_dtype=BF16,
                                         name="fox_attn", **fox_cfg)
    dfl, g_b_f = _fox_gate_bwd(fl, fox_b_f, dcum, name="fox_gate_bwd")
    dproj = jnp.concatenate([dfq.astype(BF16), dfk, dfv, _pad_axis(dfl, 1, LANES).astype(BF16)], axis=1)
    grads["fox_w_in"] = _matmul(h0, dproj, ta=True, out_dtype=BF16, name="fox_proj_dw")[None, :, :3 * nq + N_HEADS]
    dh0 = _matmul(dproj, w_fox_in, tb=True, out_dtype=BF16, name="fox_proj_dx")
    dx0, dscale0, dshift0 = _norm_bwd(x0, scale[0], dh0, dx1, plus_one=True, name="ada_bwd0")
    dmod[0] = jnp.concatenate([dshift0, dscale0, dgate0], axis=1)

    dmod_send = _pad_axis(jnp.stack(dmod, axis=0).reshape(4, N_DEV, cols).transpose(1, 0, 2), 1, 8)
    dmod_recv, = _exchange([dmod_send], same_src=False, name="scatter_dmod")
    dmod_all = dmod_recv[:, :4]
    dmod_pad = _pad_axis(dmod_all, 0, LANES)
    g_ada_w = jnp.stack([_matmul(silu_c, dmod_pad[:, i], ta=True, name=f"ada_dw{i}") for i in range(4)])
    grads["ada_w"] = g_ada_w.reshape(ada_w.shape)
    grads["ada_b"] = _sum_slabs(dmod_recv, name="ada_db")[:4].reshape(ada_b.shape)

    sharded = list(_BIG) + list(_SMALL)
    axes = {**_BIG, **_SMALL}
    recv = _exchange([_full_to_chunks(grads[n], axes[n]) for n in sharded], same_src=False, name="scatter_grads")
    grads.update(dict(zip(sharded, recv)))
    repl = _exchange([g_b_f, g_conv_b, d_final_g], same_src=True, name="gather_repl_grads")
    grads.update(dict(zip(_REPL, repl)))

    grad_out, deltas, new_m, new_v = {}, {}, {}, {}
    for n in order:
        grad_out[n], deltas[n], new_m[n], new_v[n] = _adamw(
            weights[n], grads[n], mom_m[n], mom_v[n], slabs=n in axes or n in _REPL, name=f"adamw_{n}")

    grad_x = dx0[None]
    return (loss, grad_x, *[grad_out[n] for n in order], *[deltas[n] for n in order],
            *[new_m[n] for n in order], *[new_v[n] for n in order])
```

```python
import jax
import jax.numpy as jnp
from jax import lax
from jax.experimental import pallas as pl
from jax.experimental.pallas import tpu as pltpu

F32 = jnp.float32
BF16 = jnp.bfloat16
HIGHEST = lax.Precision.HIGHEST

N_DEV = 8
D_MODEL = 1024
N_HEADS = 16
HEAD_DIM = 64
MLA_ROPE_HALF = 16
MLA_Q_RANK = 384
MLA_KV_RANK = 256
MLA_ROPE_DIM = 32
NORM_EPS = 1e-6
ROPE_BASE = 10000.0
ADAM_LR = 0.001
ADAM_B1 = 0.9
ADAM_B2 = 0.999
ADAM_EPS = 1e-08
ADAM_WD = 0.01
ADAM_STEP = 10

LANES = 128
VMEM_LIMIT_BYTES = 56 * 1024 * 1024
ROW_BLOCK = 512
ATT_BLOCK = 512
CONV_ROWS = 1024
MM_BM, MM_BN, MM_BK = 512, 1408, 2048
MM_K_WHOLE = 3328
LOG2E = 1.4426950408889634


def _params(*sem):
    return pltpu.CompilerParams(dimension_semantics=sem or None, vmem_limit_bytes=VMEM_LIMIT_BYTES)


def _blk(dim, pref):
    if dim <= pref:
        return dim
    b = pref - pref % LANES
    while b >= LANES:
        if dim % b == 0:
            return b
        b -= LANES
    raise ValueError(f"no block for {dim}")


def _row_blk(rows, pref):
    if rows <= pref:
        return rows
    for b in range(pref - pref % 8, 7, -8):
        if rows % b == 0:
            return b
    return rows


def _pad_axis(a, axis, mult):
    pad = (-a.shape[axis]) % mult
    if pad == 0:
        return a
    widths = [(0, 0)] * a.ndim
    widths[axis] = (0, pad)
    return jnp.pad(a, widths)


def _matmul(a, b, *, ta=False, tb=False, out_dtype=F32, col_scale=None, name):
    m, k = (a.shape[1], a.shape[0]) if ta else a.shape
    n = b.shape[0] if tb else b.shape[1]
    assert (b.shape[1] if tb else b.shape[0]) == k, (a.shape, b.shape, ta, tb)
    bm, bn = _blk(m, MM_BM if ta else 2 * MM_BM), _blk(n, MM_BN)
    bk = k if k <= MM_K_WHOLE else _blk(k, MM_BK)
    nk = k // bk
    dims = (((0 if ta else 1,), (1 if tb else 0,)), ((), ()))
    has_scale = col_scale is not None
    use_acc = nk > 1 and (out_dtype != F32 or has_scale)

    def body(*refs):
        a_ref, b_ref = refs[0], refs[1]
        s_ref = refs[2] if has_scale else None
        o_ref = refs[3] if has_scale else refs[2]
        acc_ref = refs[-1] if use_acc else o_ref
        kk = pl.program_id(2)
        part = lax.dot_general(a_ref[...].astype(BF16), b_ref[...].astype(BF16), dims, preferred_element_type=F32)

        def finish(val):
            if has_scale:
                val = val * s_ref[...]
            o_ref[...] = val.astype(out_dtype)

        if nk == 1:
            finish(part)
            return

        @pl.when(kk == 0)
        def _():
            acc_ref[...] = part

        @pl.when(kk > 0)
        def _():
            acc_ref[...] += part

        if use_acc:
            @pl.when(kk == nk - 1)
            def _():
                finish(acc_ref[...])

    a_spec = pl.BlockSpec((bk, bm), lambda i, j, kk: (kk, i)) if ta else pl.BlockSpec((bm, bk), lambda i, j, kk: (i, kk))
    b_spec = pl.BlockSpec((bn, bk), lambda i, j, kk: (j, kk)) if tb else pl.BlockSpec((bk, bn), lambda i, j, kk: (kk, j))
    return pl.pallas_call(
        body, name=name, grid=(m // bm, n // bn, nk),
        in_specs=[a_spec, b_spec] + ([pl.BlockSpec((1, bn), lambda i, j, kk: (0, j))] if has_scale else []),
        out_specs=pl.BlockSpec((bm, bn), lambda i, j, kk: (i, j)),
        out_shape=jax.ShapeDtypeStruct((m, n), out_dtype),
        scratch_shapes=[pltpu.VMEM((bm, bn), F32)] if use_acc else [],
        compiler_params=_params("parallel", "parallel", "arbitrary"),
    )(*([a, b] + ([col_scale] if has_scale else [])))


def _norm_fwd(x, mul, add, *, plus_one, out_dtype, name):
    s, n = x.shape
    bs = _blk(s, ROW_BLOCK)

    def body(x_ref, m_ref, a_ref, o_ref):
        xv = x_ref[...]
        r = lax.rsqrt(jnp.mean(xv * xv, axis=-1, keepdims=True) + NORM_EPS)
        mv = m_ref[...] + 1.0 if plus_one else m_ref[...]
        o_ref[...] = (xv * r * mv + a_ref[...]).astype(out_dtype)

    row = pl.BlockSpec((bs, n), lambda i: (i, 0))
    vec = pl.BlockSpec((1, n), lambda i: (0, 0))
    return pl.pallas_call(
        body, name=name, grid=(s // bs,), in_specs=[row, vec, vec], out_specs=row,
        out_shape=jax.ShapeDtypeStruct((s, n), out_dtype), compiler_params=_params("parallel"),
    )(x, mul, add)


def _norm_bwd(x, mul, dy, dres, *, plus_one, name):
    s, n = x.shape
    bs = _blk(s, ROW_BLOCK)
    has_res = dres is not None

    def body(*refs):
        if has_res:
            x_ref, m_ref, dy_ref, dres_ref, dx_ref, dm_ref, da_ref = refs
        else:
            x_ref, m_ref, dy_ref, dx_ref, dm_ref, da_ref = refs
        xv = x_ref[...]
        dyv = dy_ref[...].astype(F32)
        r = lax.rsqrt(jnp.mean(xv * xv, axis=-1, keepdims=True) + NORM_EPS)
        xn = xv * r
        mv = m_ref[...] + 1.0 if plus_one else m_ref[...]
        g = dyv * mv
        dx = r * (g - xn * jnp.mean(g * xn, axis=-1, keepdims=True))
        if has_res:
            dx = dx + dres_ref[...]
        dx_ref[...] = dx

        @pl.when(pl.program_id(0) == 0)
        def _():
            dm_ref[...] = jnp.zeros_like(dm_ref)
            da_ref[...] = jnp.zeros_like(da_ref)

        dm_ref[...] += jnp.sum(dyv * xn, axis=0, keepdims=True)
        da_ref[...] += jnp.sum(dyv, axis=0, keepdims=True)

    row = pl.BlockSpec((bs, n), lambda i: (i, 0))
    vec = pl.BlockSpec((1, n), lambda i: (0, 0))
    ins = [x, mul, dy] + ([dres] if has_res else [])
    return pl.pallas_call(
        body, name=name, grid=(s // bs,),
        in_specs=[row, vec, row] + ([row] if has_res else []), out_specs=[row, vec, vec],
        out_shape=[jax.ShapeDtypeStruct((s, n), F32), jax.ShapeDtypeStruct((1, n), F32), jax.ShapeDtypeStruct((1, n), F32)],
        compiler_params=_params("arbitrary"),
    )(*ins)


def _resid_fwd(x, y, gate, *, name):
    s, n = x.shape
    bs = _blk(s, ROW_BLOCK)

    def body(x_ref, y_ref, g_ref, o_ref):
        o_ref[...] = x_ref[...] + g_ref[...] * y_ref[...]

    row = pl.BlockSpec((bs, n), lambda i: (i, 0))
    vec = pl.BlockSpec((1, n), lambda i: (0, 0))
    return pl.pallas_call(
        body, name=name, grid=(s // bs,), in_specs=[row, row, vec], out_specs=row,
        out_shape=jax.ShapeDtypeStruct((s, n), F32), compiler_params=_params("parallel"),
    )(x, y, gate)


def _resid_bwd(dx, y, gate, *, name):
    s, n = dx.shape
    bs = _blk(s, ROW_BLOCK)

    def body(dx_ref, y_ref, g_ref, dy_ref, dg_ref):
        dxv = dx_ref[...]
        dy_ref[...] = (g_ref[...] * dxv).astype(BF16)

        @pl.when(pl.program_id(0) == 0)
        def _():
            dg_ref[...] = jnp.zeros_like(dg_ref)

        dg_ref[...] += jnp.sum(dxv * y_ref[...], axis=0, keepdims=True)

    row = pl.BlockSpec((bs, n), lambda i: (i, 0))
    vec = pl.BlockSpec((1, n), lambda i: (0, 0))
    return pl.pallas_call(
        body, name=name, grid=(s // bs,), in_specs=[row, row, vec], out_specs=[row, vec],
        out_shape=[jax.ShapeDtypeStruct((s, n), BF16), jax.ShapeDtypeStruct((1, n), F32)],
        compiler_params=_params("arbitrary"),
    )(dx, y, gate)


def _final_loss(x, g, target, *, name):
    s, n = x.shape
    bs = _blk(s, ROW_BLOCK)

    def body(x_ref, g_ref, t_ref, loss_ref, dx_ref, dg_ref):
        xv = x_ref[...]
        r = lax.rsqrt(jnp.mean(xv * xv, axis=-1, keepdims=True) + NORM_EPS)
        xn = xv * r
        gv = g_ref[...]
        err = xn * gv - t_ref[...]
        dout = err * (1.0 / n)
        gg = dout * gv
        dx_ref[...] = r * (gg - xn * jnp.mean(gg * xn, axis=-1, keepdims=True))

        @pl.when(pl.program_id(0) == 0)
        def _():
            loss_ref[...] = jnp.zeros_like(loss_ref)
            dg_ref[...] = jnp.zeros_like(dg_ref)

        part = jnp.sum(jnp.sum(err * err, axis=-1, keepdims=True), axis=0, keepdims=True) * (0.5 / n)
        loss_ref[...] += jnp.broadcast_to(part, loss_ref.shape)
        dg_ref[...] += jnp.sum(dout * xn, axis=0, keepdims=True)

    row = pl.BlockSpec((bs, n), lambda i: (i, 0))
    vec = pl.BlockSpec((1, n), lambda i: (0, 0))
    return pl.pallas_call(
        body, name=name, grid=(s // bs,), in_specs=[row, vec, row],
        out_specs=[pl.BlockSpec((1, LANES), lambda i: (0, 0)), row, vec],
        out_shape=[jax.ShapeDtypeStruct((1, LANES), F32), jax.ShapeDtypeStruct((s, n), F32), jax.ShapeDtypeStruct((1, n), F32)],
        compiler_params=_params("arbitrary"),
    )(x, g, target)


def _lane_lt64(shape):
    return lax.broadcasted_iota(jnp.int32, shape, 1) < HEAD_DIM


def _keep_low(x):
    return jnp.where(_lane_lt64(x.shape), x.astype(F32), 0.0).astype(x.dtype)


def _keep_high(x):
    return jnp.where(_lane_lt64(x.shape), 0.0, x.astype(F32)).astype(x.dtype)


def _lane_merge(a, b):
    n = max(a.shape[0], b.shape[0])
    return jnp.where(_lane_lt64((n, LANES)), a, b)


def _pair(x, width, masked):
    if width == HEAD_DIM:
        return (_keep_low(x), _keep_high(x)) if masked else (x, x)
    return x[:, :LANES], x[:, LANES:]


def _qk_t(a, b):
    return lax.dot_general(a, b, (((1,), (1,)), ((), ())), preferred_element_type=F32)


def _attn_specs(s, blk, width, cols, resident):
    w = 2 * width
    if resident:
        return pl.BlockSpec((s, w), lambda p, i: (0, cols + p))
    return pl.BlockSpec((blk, w), lambda p, i: (i, cols + p))


BIAS_TERMS = 3


def _key_blocks(vt, blk):
    return vt.reshape(vt.shape[0], vt.shape[1] // blk, blk).transpose(1, 0, 2)


def _attn_fwd(q, k, vt, kbl, *, qc, kc, width, name):
    s = q.shape[0]
    blk = _blk(s, ATT_BLOCK)
    nb = s // blk
    has_bias = kbl is not None
    assert has_bias == (width == HEAD_DIM)

    def body(*refs):
        if has_bias:
            q_ref, k_ref, vt_ref, kbl_ref, o_ref, lse_ref = refs
        else:
            q_ref, k_ref, vt_ref, o_ref, lse_ref = refs
        i = pl.program_id(1)
        q2 = q_ref[...]
        if has_bias:
            lane = lax.broadcasted_iota(jnp.int32, (blk, LANES), 1)
            qf = q2.astype(F32)
            qh = (jnp.where(lane < HEAD_DIM, qf, jnp.where(lane < HEAD_DIM + BIAS_TERMS, 1.0, 0.0)).astype(BF16),
                  jnp.where(lane >= HEAD_DIM, qf, jnp.where(lane < BIAS_TERMS, 1.0, 0.0)).astype(BF16))
        else:
            qh = (q2[:, :LANES], q2[:, LANES:])

        def step(j, carry, nblk, diag):
            rows = pl.ds(pl.multiple_of(j * blk, blk), nblk * blk)
            vt1 = [jnp.concatenate([vt_ref[j + b], jnp.ones((16, blk), BF16)], axis=0) for b in range(nblk)]
            k2 = k_ref[rows, :]
            if has_bias:
                low = _lane_lt64(k2.shape)
                kf, bf = k2.astype(F32), kbl_ref[rows, :].astype(F32)
                kh = (jnp.where(low, kf, bf).astype(BF16), jnp.where(low, bf, kf).astype(BF16))
            else:
                kh = (k2[:, :LANES], k2[:, LANES:])
            out = []
            for hd in range(2):
                m, acc = carry[hd]
                st = _qk_t(kh[hd], qh[hd])
                if diag:
                    row = lax.broadcasted_iota(jnp.int32, (blk, blk), 0)
                    colq = lax.broadcasted_iota(jnp.int32, (blk, blk), 1)
                    st = jnp.where(row <= colq, st, -1e30)
                m_new = jnp.maximum(m, jnp.max(st, axis=0, keepdims=True))
                alpha = jnp.exp2(m - m_new)
                pt = jnp.exp2(st - m_new).astype(BF16)
                acc = alpha * acc
                for b in range(nblk):
                    acc = acc + jnp.dot(vt1[b], pt[b * blk:(b + 1) * blk], preferred_element_type=F32)
                out.append((m_new, acc))
            return tuple(out)

        one = (jnp.full((1, blk), -1e30, F32), jnp.zeros((LANES + 16, blk), F32))
        carry = lax.fori_loop(0, i // 4, lambda j, c: step(4 * j, c, 4, False), (one, one))
        carry = lax.fori_loop(0, (i % 4) // 2, lambda _, c: step(i - i % 4, c, 2, False), carry)
        carry = lax.fori_loop(0, i % 2, lambda _, c: step(i - 1, c, 1, False), carry)
        (ma, acca), (mb, accb) = step(i, carry, 1, True)
        la = jnp.max(acca[LANES:LANES + 8], axis=0, keepdims=True)
        lb = jnp.max(accb[LANES:LANES + 8], axis=0, keepdims=True)
        acca, accb = acca[0:LANES], accb[0:LANES]
        low = lax.broadcasted_iota(jnp.int32, (LANES, blk), 0) < HEAD_DIM
        o_ref[...] = jnp.where(low, acca / la, accb / lb).T
        lse_ref[0, 0] = ma + jnp.log(la) * LOG2E
        lse_ref[1, 0] = mb + jnp.log(lb) * LOG2E

    ins = [q, k, vt] + ([kbl] if has_bias else [])
    return pl.pallas_call(
        body, name=name, grid=(N_HEADS // 2, nb),
        in_specs=[_attn_specs(s, blk, width, qc, False), _attn_specs(s, blk, width, kc, True),
                  pl.BlockSpec((nb, LANES, blk), lambda p, i: (0, p, 0))]
                 + ([_attn_specs(s, blk, HEAD_DIM, 0, True)] if has_bias else []),
        out_specs=[pl.BlockSpec((blk, LANES), lambda p, i: (i, p)), pl.BlockSpec((2, 1, 1, blk), lambda p, i: (p, i, 0, 0))],
        out_shape=[jax.ShapeDtypeStruct((s, N_HEADS * HEAD_DIM), F32), jax.ShapeDtypeStruct((N_HEADS, nb, 1, blk), F32)],
        compiler_params=_params("parallel", "parallel"),
    )(*ins)


def _bias_lane_terms(kb2):
    terms, rest = [], kb2
    for _ in range(BIAS_TERMS):
        t = lax.reduce_precision(rest, 8, 7)
        terms.append(t.astype(BF16))
        rest = rest - t
    place = [[0.0] * (N_HEADS * HEAD_DIM) for _ in range(LANES)]
    for t in range(BIAS_TERMS):
        for h in range(N_HEADS):
            place[t * N_HEADS + h][(h // 2) * LANES + (HEAD_DIM if h % 2 == 0 else 0) + t] = 1.0
    return _matmul(_pad_axis(jnp.concatenate(terms, axis=1), 1, LANES), jnp.asarray(place, BF16), out_dtype=BF16,
                   name="fox_bias_lanes")


def _causal_keep(n):
    row = lax.broadcasted_iota(jnp.int32, (n, n), 0)
    col = lax.broadcasted_iota(jnp.int32, (n, n), 1)
    return col <= row


def _attn_delta(o, do, *, name):
    s, n = o.shape
    bs = _blk(s, ROW_BLOCK)

    def body(o_ref, do_ref, d_ref):
        for g in range(n // LANES):
            prod = do_ref[:, g * LANES:(g + 1) * LANES].astype(F32) * o_ref[:, g * LANES:(g + 1) * LANES]
            low = _lane_lt64(prod.shape)
            d_ref[:, g * LANES:(g + 1) * LANES] = _lane_merge(
                jnp.sum(jnp.where(low, prod, 0.0), axis=-1, keepdims=True),
                jnp.sum(jnp.where(low, 0.0, prod), axis=-1, keepdims=True))

    row = pl.BlockSpec((bs, n), lambda i: (i, 0))
    return pl.pallas_call(
        body, name=name, grid=(s // bs,), in_specs=[row, row], out_specs=row,
        out_shape=jax.ShapeDtypeStruct((s, n), F32), compiler_params=_params("parallel"),
    )(o, do)


def _attn_bwd(q, k, v, kb_col, do, lse_row, delta_row, *, qc, kc, vc, width, dq_mult, dk_mult, out_dtype, name):
    s = q.shape[0]
    blk = _blk(s, ATT_BLOCK)
    nb = s // blk
    has_bias = kb_col is not None

    def body(*refs):
        if has_bias:
            q_ref, k_ref, v_ref, kb_ref, do_ref, lse_ref, dl_ref, dk_ref, dv_ref, db_ref, dq_ref, dr_ref = refs
        else:
            q_ref, k_ref, v_ref, do_ref, lse_ref, dl_ref, dk_ref, dv_ref, db_ref, dq_ref = refs
        j = pl.program_id(1)

        @pl.when(j == 0)
        def _():
            dq_ref[...] = jnp.zeros_like(dq_ref)
            if has_bias:
                dr_ref[...] = jnp.zeros_like(dr_ref)

        kh = _pair(k_ref[...], width, True)
        v2 = v_ref[...]
        vh = (_keep_low(v2), _keep_high(v2))
        if has_bias:
            kb2 = kb_ref[0]
            kbh = (kb2[:, 0:1], kb2[:, 1:2])

        def step(i, carry, nblk, diag):
            rows = pl.ds(pl.multiple_of(i * blk, blk), nblk * blk)
            qh = _pair(q_ref[rows, :], width, False)
            doi = do_ref[rows, :]
            out, dq_parts = [], []
            for hd in range(2):
                dk, dvv, db = carry[hd]
                st = _qk_t(kh[hd], qh[hd])
                if has_bias:
                    st = st + kbh[hd]
                if diag:
                    row = lax.broadcasted_iota(jnp.int32, (blk, blk), 0)
                    colq = lax.broadcasted_iota(jnp.int32, (blk, blk), 1)
                    st = jnp.where(row <= colq, st, -1e30)
                lse_i = jnp.concatenate([lse_ref[hd, i + b] for b in range(nblk)], axis=1)
                delta_i = jnp.concatenate([dl_ref[hd, i + b] for b in range(nblk)], axis=1)
                pt = jnp.exp2(st - lse_i)
                dvv = dvv + jnp.dot(pt.astype(BF16), doi, preferred_element_type=F32)
                dst = pt * (_qk_t(vh[hd], doi) - delta_i)
                dsb = dst.astype(BF16)
                dk = dk + jnp.dot(dsb, qh[hd], preferred_element_type=F32)
                db = db + jnp.sum(dst, axis=-1, keepdims=True)
                dq_parts.append(lax.dot_general(dsb, kh[hd], (((0,), (0,)), ((), ())), preferred_element_type=F32))
                if has_bias:
                    rsum = jnp.sum(dst, axis=0, keepdims=True)
                    for b in range(nblk):
                        dr_ref[hd, i + b] += rsum[:, b * blk:(b + 1) * blk]
                out.append((dk, dvv, db))
            if width == HEAD_DIM:
                dq_ref[rows, :] += (dq_parts[0] + dq_parts[1]) * dq_mult
            else:
                dq_ref[rows, 0:LANES] += dq_parts[0] * dq_mult
                dq_ref[rows, LANES:2 * LANES] += dq_parts[1] * dq_mult
            return tuple(out)

        one = (jnp.zeros((blk, LANES), F32), jnp.zeros((blk, LANES), F32), jnp.zeros((blk, 1), F32))
        carry = step(j, (one, one), 1, True)
        rest = nb - 1 - j
        carry = lax.fori_loop(0, rest // 4, lambda t, c: step(j + 1 + 4 * t, c, 4, False), carry)
        carry = lax.fori_loop(0, (rest % 4) // 2, lambda _, c: step(nb - rest % 4, c, 2, False), carry)
        (dka, dva, dba), (dkb, dvb, dbb) = lax.fori_loop(0, rest % 2, lambda _, c: step(nb - 1, c, 1, False), carry)
        if width == HEAD_DIM:
            dk = _lane_merge(dka, dkb)
        else:
            dk = jnp.concatenate([dka, dkb], axis=1)
        dk_ref[...] = (dk * dk_mult).astype(out_dtype)
        dv_ref[...] = _lane_merge(dva, dvb).astype(out_dtype)
        db_ref[...] = _lane_merge(dba, dbb)

    stat = pl.BlockSpec((blk, LANES), lambda p, jj: (jj, p))
    rows = pl.BlockSpec((2, nb, 1, blk), lambda p, jj: (p, 0, 0, 0))
    ins = [q, k, v] + ([kb_col] if has_bias else []) + [do, lse_row, delta_row]
    return pl.pallas_call(
        body, name=name, grid=(N_HEADS // 2, nb),
        in_specs=[_attn_specs(s, blk, width, qc, True), _attn_specs(s, blk, width, kc, False),
                  _attn_specs(s, blk, HEAD_DIM, vc, False)]
                 + ([pl.BlockSpec((1, blk, 2), lambda p, jj: (p, jj, 0))] if has_bias else [])
                 + [pl.BlockSpec((s, LANES), lambda p, jj: (0, p)), rows, rows],
        out_specs=[pl.BlockSpec((blk, 2 * width), lambda p, jj: (jj, p)), stat, stat,
                   pl.BlockSpec((s, 2 * width), lambda p, jj: (0, p))] + ([rows] if has_bias else []),
        out_shape=[jax.ShapeDtypeStruct((s, N_HEADS * width), out_dtype), jax.ShapeDtypeStruct((s, N_HEADS * HEAD_DIM), out_dtype),
                   jax.ShapeDtypeStruct((s, N_HEADS * HEAD_DIM), F32), jax.ShapeDtypeStruct((s, N_HEADS * width), F32)]
                  + ([jax.ShapeDtypeStruct((N_HEADS, nb, 1, blk), F32)] if has_bias else []),
        compiler_params=_params("parallel", "arbitrary"),
    )(*ins)


def _head_stat(t):
    return t[:, ::HEAD_DIM]


def _stat_rows(t16, blk):
    s = t16.shape[0]
    return t16.T.reshape(N_HEADS, s // blk, 1, blk)


def _attention_bwd(res, do, *, qc, kc, vc, width, dq_mult, dk_mult, out_dtype, name):
    q, k, v, bias, o, lse_row = res
    s = q.shape[0]
    blk = _blk(s, ATT_BLOCK)
    kb_col = None if bias is None else bias.reshape(s, N_HEADS // 2, 2).transpose(1, 0, 2)
    delta_row = _stat_rows(_head_stat(_attn_delta(o, do, name=name + "_delta")), blk)
    outs = _attn_bwd(q, k, v, kb_col, do, lse_row, delta_row, qc=qc, kc=kc, vc=vc, width=width, dq_mult=dq_mult,
                     dk_mult=dk_mult, out_dtype=out_dtype, name=name + "_bwd")
    dk, dv, dcol, dq = outs[:4]
    if bias is None:
        return dq, dk, dv, None
    return dq, dk, dv, outs[4].reshape(N_HEADS, s).T - _head_stat(dcol)


def _fox_gate_fwd(fl, bf, *, name):
    s, n = fl.shape
    bs = _blk(s, ROW_BLOCK)

    def body(fl_ref, bf_ref, cum_ref, carry_ref):
        @pl.when(pl.program_id(0) == 0)
        def _():
            carry_ref[...] = jnp.zeros_like(carry_ref)

        z = fl_ref[...] + bf_ref[...]
        lf = jnp.minimum(z, 0.0) - jnp.log1p(jnp.exp(-jnp.abs(z)))
        row = lax.broadcasted_iota(jnp.int32, (bs, bs), 0)
        col = lax.broadcasted_iota(jnp.int32, (bs, bs), 1)
        tri = (col <= row).astype(F32)
        cum_ref[...] = jnp.dot(tri, lf, preferred_element_type=F32, precision=HIGHEST) + carry_ref[...]
        carry_ref[...] += jnp.sum(lf, axis=0, keepdims=True)

    return pl.pallas_call(
        body, name=name, grid=(s // bs,),
        in_specs=[pl.BlockSpec((bs, n), lambda i: (i, 0)), pl.BlockSpec((1, n), lambda i: (0, 0))],
        out_specs=pl.BlockSpec((bs, n), lambda i: (i, 0)),
        out_shape=jax.ShapeDtypeStruct((s, n), F32), scratch_shapes=[pltpu.VMEM((1, n), F32)],
        compiler_params=_params("arbitrary"),
    )(fl, bf)


def _fox_gate_bwd(fl, bf, dcum, *, name):
    s, n = fl.shape
    bs = _blk(s, ROW_BLOCK)
    nb = s // bs

    def body(fl_ref, bf_ref, dc_ref, dz_ref, dbf_ref, carry_ref):
        @pl.when(pl.program_id(0) == 0)
        def _():
            carry_ref[...] = jnp.zeros_like(carry_ref)
            dbf_ref[...] = jnp.zeros_like(dbf_ref)

        dc = dc_ref[...]
        row = lax.broadcasted_iota(jnp.int32, (bs, bs), 0)
        col = lax.broadcasted_iota(jnp.int32, (bs, bs), 1)
        tri = (col >= row).astype(F32)
        dlf = jnp.dot(tri, dc, preferred_element_type=F32, precision=HIGHEST) + carry_ref[...]
        carry_ref[...] += jnp.sum(dc, axis=0, keepdims=True)
        z = fl_ref[...] + bf_ref[...]
        dz = dlf / (1.0 + jnp.exp(z))
        dz_ref[...] = dz
        dbf_ref[...] += jnp.sum(dz, axis=0, keepdims=True)

    rev = pl.BlockSpec((bs, n), lambda i: (nb - 1 - i, 0))
    vec = pl.BlockSpec((1, n), lambda i: (0, 0))
    return pl.pallas_call(
        body, name=name, grid=(nb,), in_specs=[rev, vec, rev], out_specs=[rev, vec],
        out_shape=[jax.ShapeDtypeStruct((s, n), F32), jax.ShapeDtypeStruct((1, n), F32)],
        scratch_shapes=[pltpu.VMEM((1, n), F32)], compiler_params=_params("arbitrary"),
    )(fl, bf, dcum)


def _rope(x1, x2, cos, sin, *, negate, name):
    s, n = x1.shape
    bs = _blk(s, ROW_BLOCK)

    def body(a_ref, b_ref, c_ref, s_ref, o1_ref, o2_ref):
        a, b, cv = a_ref[...], b_ref[...], c_ref[...]
        sv = -s_ref[...] if negate else s_ref[...]
        o1_ref[...] = a * cv - b * sv
        o2_ref[...] = b * cv + a * sv

    row = pl.BlockSpec((bs, n), lambda i: (i, 0))
    return pl.pallas_call(
        body, name=name, grid=(s // bs,), in_specs=[row] * 4, out_specs=[row, row],
        out_shape=[jax.ShapeDtypeStruct((s, n), F32)] * 2, compiler_params=_params("parallel"),
    )(x1, x2, cos, sin)


def _rope_heads(x, ta, tb, tc, *, out_dtype, name):
    s, n = x.shape
    bs = _blk(s, ROW_BLOCK)

    def body(x_ref, a_ref, b_ref, c_ref, o_ref):
        av, bv, cv = a_ref[...], b_ref[...], c_ref[...]
        for g in range(n // LANES):
            xg = x_ref[:, g * LANES:(g + 1) * LANES]
            og = xg * av + pltpu.roll(xg, LANES - MLA_ROPE_HALF, 1) * bv + pltpu.roll(xg, MLA_ROPE_HALF, 1) * cv
            o_ref[:, g * LANES:(g + 1) * LANES] = og.astype(out_dtype)

    row = pl.BlockSpec((bs, n), lambda i: (i, 0))
    tab = pl.BlockSpec((bs, LANES), lambda i: (i, 0))
    return pl.pallas_call(
        body, name=name, grid=(s // bs,), in_specs=[row, tab, tab, tab], out_specs=row,
        out_shape=jax.ShapeDtypeStruct((s, n), out_dtype), compiler_params=_params("parallel"),
    )(x, ta, tb, tc)


def _group_sum(x, *, name):
    s, n = x.shape
    bs = _blk(s, ROW_BLOCK)

    def body(x_ref, o_ref):
        acc = x_ref[:, 0:LANES]
        for g in range(1, n // LANES):
            acc = acc + x_ref[:, g * LANES:(g + 1) * LANES]
        o_ref[...] = acc

    return pl.pallas_call(
        body, name=name, grid=(s // bs,), in_specs=[pl.BlockSpec((bs, n), lambda i: (i, 0))],
        out_specs=pl.BlockSpec((bs, LANES), lambda i: (i, 0)),
        out_shape=jax.ShapeDtypeStruct((s, LANES), F32), compiler_params=_params("parallel"),
    )(x)


def _shift_down(x, k):
    return pltpu.roll(x, k, 0)


def _conv_rows(ext, w_ref, b_ref, rows):
    y = b_ref[...] + w_ref[0:1, :] * _shift_down(ext, 2) + w_ref[1:2, :] * _shift_down(ext, 1) + w_ref[2:3, :] * ext
    return y[8:8 + rows]


def _conv_gate_fwd(u, cw, cb, *, name):
    s, f2 = u.shape
    f = f2 // 2
    nf = f // LANES
    r = _blk(s, CONV_ROWS)
    r8 = r // 8

    def body(ug_ref, ugp_ref, uv_ref, uvp_ref, wg_ref, wv_ref, bg_ref, bv_ref, o_ref):
        first = pl.program_id(1) == 0

        def conv(cur_ref, prev_ref, w_ref, b_ref):
            prev = jnp.where(first, 0.0, prev_ref[...])
            return _conv_rows(jnp.concatenate([prev, cur_ref[...]], axis=0), w_ref, b_ref, r)

        yg = conv(ug_ref, ugp_ref, wg_ref, bg_ref)
        yv = conv(uv_ref, uvp_ref, wv_ref, bv_ref)
        o_ref[...] = (yg * jax.nn.sigmoid(yg) * yv).astype(BF16)

    def cur(off):
        return pl.BlockSpec((r, LANES), lambda c, i: (i, c + off))

    def prev(off):
        return pl.BlockSpec((8, LANES), lambda c, i: (jnp.maximum(i * r8 - 1, 0), c + off))

    def wspec(rows, off):
        return pl.BlockSpec((rows, LANES), lambda c, i: (0, c + off))

    return pl.pallas_call(
        body, name=name, grid=(nf, s // r),
        in_specs=[cur(0), prev(0), cur(nf), prev(nf), wspec(3, 0), wspec(3, nf), wspec(1, 0), wspec(1, nf)],
        out_specs=pl.BlockSpec((r, LANES), lambda c, i: (i, c)),
        out_shape=jax.ShapeDtypeStruct((s, f), BF16), compiler_params=_params("parallel", "parallel"),
    )(u, u, u, u, cw, cw, cb, cb)


def _conv_gate_bwd(u, cw, cb, dg, *, name):
    s, f2 = u.shape
    f = f2 // 2
    nf = f // LANES
    r = _blk(s, CONV_ROWS)
    r8 = r // 8
    nr = s // r

    def body(ug_ref, ugp_ref, ugn_ref, uv_ref, uvp_ref, uvn_ref, wg_ref, wv_ref, bg_ref, bv_ref, dg_ref, dgn_ref,
             dug_ref, duv_ref, dwg_ref, dwv_ref, dbg_ref, dbv_ref):
        i = pl.program_id(1)
        first, last = i == 0, i == nr - 1

        def ext_of(cur_ref, prev_ref, next_ref):
            prev = jnp.where(first, 0.0, prev_ref[...])
            return jnp.concatenate([prev, cur_ref[...], next_ref[...]], axis=0)

        eg, ev = ext_of(ug_ref, ugp_ref, ugn_ref), ext_of(uv_ref, uvp_ref, uvn_ref)
        yg = _conv_rows(eg, wg_ref, bg_ref, r + 8)
        yv = _conv_rows(ev, wv_ref, bv_ref, r + 8)
        dgn = jnp.where(last, 0.0, dgn_ref[...])
        dgx = jnp.concatenate([dg_ref[...], dgn], axis=0)
        sg = jax.nn.sigmoid(yg)
        dyg = dgx * yv * (sg * (1.0 + yg * (1.0 - sg)))
        dyv = dgx * (yg * sg)

        @pl.when(i == 0)
        def _():
            for ref in (dwg_ref, dwv_ref, dbg_ref, dbv_ref):
                ref[...] = jnp.zeros_like(ref)

        def grads(dy, ext, w_ref, du_ref, dw_ref, db_ref):
            n = r + 8
            du = w_ref[2:3, :] * dy + w_ref[1:2, :] * pltpu.roll(dy, n - 1, 0) + w_ref[0:1, :] * pltpu.roll(dy, n - 2, 0)
            du_ref[...] = du[0:r].astype(BF16)
            dyc = dy[0:r]
            db_ref[...] += jnp.sum(dyc, axis=0, keepdims=True)
            ext_c = ext[0:r + 8]
            dw_ref[0:1, :] += jnp.sum(dyc * _shift_down(ext_c, 2)[8:], axis=0, keepdims=True)
            dw_ref[1:2, :] += jnp.sum(dyc * _shift_down(ext_c, 1)[8:], axis=0, keepdims=True)
            dw_ref[2:3, :] += jnp.sum(dyc * ext_c[8:], axis=0, keepdims=True)

        grads(dyg, eg, wg_ref, dug_ref, dwg_ref, dbg_ref)
        grads(dyv, ev, wv_ref, duv_ref, dwv_ref, dbv_ref)

    def cur(off):
        return pl.BlockSpec((r, LANES), lambda c, i: (i, c + off))

    def prev(off):
        return pl.BlockSpec((8, LANES), lambda c, i: (jnp.maximum(i * r8 - 1, 0), c + off))

    def nxt(off):
        return pl.BlockSpec((8, LANES), lambda c, i: (jnp.minimum((i + 1) * r8, s // 8 - 1), c + off))

    def wspec(rows, off):
        return pl.BlockSpec((rows, LANES), lambda c, i: (0, c + off))

    outs = pl.pallas_call(
        body, name=name, grid=(nf, nr),
        in_specs=[cur(0), prev(0), nxt(0), cur(nf), prev(nf), nxt(nf), wspec(3, 0), wspec(3, nf), wspec(1, 0), wspec(1, nf),
                  cur(0), nxt(0)],
        out_specs=[cur(0), cur(0), wspec(3, 0), wspec(3, 0), wspec(1, 0), wspec(1, 0)],
        out_shape=[jax.ShapeDtypeStruct((s, f), BF16), jax.ShapeDtypeStruct((s, f), BF16),
                   jax.ShapeDtypeStruct((3, f), F32), jax.ShapeDtypeStruct((3, f), F32),
                   jax.ShapeDtypeStruct((1, f), F32), jax.ShapeDtypeStruct((1, f), F32)],
        compiler_params=_params("parallel", "arbitrary"),
    )(u, u, u, u, u, u, cw, cw, cb, cb, dg, dg)
    dug, duv, dwg, dwv, dbg, dbv = outs
    return jnp.concatenate([dug, duv], axis=1), jnp.concatenate([dwg, dwv], axis=1), jnp.concatenate([dbg, dbv], axis=1)


def _adamw(w, g, m, v, *, slabs, name):
    shape = w.shape
    cols = shape[-1]
    rows = w.size // cols
    w2, m2, v2 = (t.reshape(rows, cols) for t in (w, m, v))
    g2 = g.reshape((N_DEV, rows, cols) if slabs else (rows, cols))
    br = _row_blk(rows, ROW_BLOCK // 2 if slabs else ROW_BLOCK)

    def body(w_ref, g_ref, m_ref, v_ref, go_ref, d_ref, nm_ref, nv_ref):
        if slabs:
            gv = g_ref[0].astype(F32)
            for p in range(1, N_DEV):
                gv = gv + g_ref[p].astype(F32)
        else:
            gv = g_ref[...]
        nm = ADAM_B1 * m_ref[...] + (1.0 - ADAM_B1) * gv
        nv = ADAM_B2 * v_ref[...] + (1.0 - ADAM_B2) * (gv * gv)
        m_hat = nm / (1.0 - ADAM_B1 ** ADAM_STEP)
        v_hat = nv / (1.0 - ADAM_B2 ** ADAM_STEP)
        go_ref[...] = gv
        d_ref[...] = -ADAM_LR * (m_hat / (jnp.sqrt(v_hat) + ADAM_EPS) + ADAM_WD * w_ref[...])
        nm_ref[...] = nm
        nv_ref[...] = nv

    spec = pl.BlockSpec((br, cols), lambda i: (i, 0))
    gspec = pl.BlockSpec((N_DEV, br, cols), lambda i: (0, i, 0)) if slabs else spec
    outs = pl.pallas_call(
        body, name=name, grid=(rows // br,), in_specs=[spec, gspec, spec, spec], out_specs=[spec] * 4,
        out_shape=[jax.ShapeDtypeStruct((rows, cols), F32)] * 4, compiler_params=_params("parallel"),
    )(w2, g2, m2, v2)
    return tuple(t.reshape(shape) for t in outs)


def _exchange(xs, *, same_src, name):
    n = len(xs)
    slabs = [x.shape if same_src else x.shape[1:] for x in xs]

    def body(*refs):
        x_refs, o_refs = refs[:n], refs[n:2 * n]
        send_sems, recv_sems, loc_sems = refs[2 * n:]
        ix, iy, ic = lax.axis_index("x"), lax.axis_index("y"), lax.axis_index("c")
        me = 4 * ix + 2 * iy + ic
        local, sends, recvs = [], [], []
        for a in range(n):
            def src(p, a=a):
                return x_refs[a] if same_src else x_refs[a].at[p]

            local.append(pltpu.make_async_copy(src(me), o_refs[a].at[me], loc_sems.at[a]))
            for k in (1, 2, 4, 3, 5, 6, 7):
                px = 1 - ix if k & 4 else ix
                py = 1 - iy if k & 2 else iy
                pc = 1 - ic if k & 1 else ic
                p = 4 * px + 2 * py + pc
                for dst, out in ((me, sends), (p, recvs)):
                    out.append(pltpu.make_async_remote_copy(
                        src_ref=src(p), dst_ref=o_refs[a].at[dst], send_sem=send_sems.at[a, k - 1],
                        recv_sem=recv_sems.at[a, k - 1], device_id=(px, py, pc), device_id_type=pl.DeviceIdType.MESH))
        for cp in local + sends:
            cp.start()
        for cp in recvs:
            cp.wait_recv()
        for cp in sends:
            cp.wait_send()
        for cp in local:
            cp.wait()

    return pl.pallas_call(
        body, name=name,
        in_specs=[pl.BlockSpec(memory_space=pl.ANY)] * n, out_specs=[pl.BlockSpec(memory_space=pl.ANY)] * n,
        out_shape=[jax.ShapeDtypeStruct((N_DEV,) + tuple(sl), x.dtype) for sl, x in zip(slabs, xs)],
        scratch_shapes=[pltpu.SemaphoreType.DMA((n, N_DEV - 1)), pltpu.SemaphoreType.DMA((n, N_DEV - 1)),
                        pltpu.SemaphoreType.DMA((n,))],
        compiler_params=pltpu.CompilerParams(has_side_effects=True, vmem_limit_bytes=VMEM_LIMIT_BYTES),
    )(*xs)


def _gather_two_level(xs, *, name):
    n = len(xs)

    def body(*refs):
        x_refs, o_refs = refs[:n], refs[n:2 * n]
        send_sems, recv_sems, loc_sems = refs[2 * n:]
        ix, iy, ic = lax.axis_index("x"), lax.axis_index("y"), lax.axis_index("c")
        me = 4 * ix + 2 * iy + ic
        sib = 4 * ix + 2 * iy + (1 - ic)
        chips = [(1 - ix if ch & 2 else ix, 1 - iy if ch & 1 else iy) for ch in (1, 2, 3)]

        def copy(a, pos, src, slot, to):
            return pltpu.make_async_remote_copy(
                src_ref=src, dst_ref=o_refs[a].at[slot], send_sem=send_sems.at[a, pos], recv_sem=recv_sems.at[a, pos],
                device_id=to, device_id_type=pl.DeviceIdType.MESH)

        local = [pltpu.make_async_copy(x_refs[a], o_refs[a].at[me], loc_sems.at[a]) for a in range(n)]
        first, passed, arrive = [], [], []
        for a in range(n):
            first.append(copy(a, 0, x_refs[a], me, (ix, iy, 1 - ic)))
            arrive.append(copy(a, 0, x_refs[a], sib, (ix, iy, 1 - ic)))
            for ch, (px, py) in enumerate(chips, start=1):
                same, other = 4 * px + 2 * py + ic, 4 * px + 2 * py + (1 - ic)
                first.append(copy(a, 2 * ch - 1, x_refs[a], me, (px, py, ic)))
                passed.append((copy(a, 2 * ch - 1, x_refs[a], same, (px, py, ic)),
                               copy(a, 2 * ch, o_refs[a].at[same], same, (ix, iy, 1 - ic))))
                arrive.append(copy(a, 2 * ch, x_refs[a], other, (ix, iy, 1 - ic)))
        for cp in local + first:
            cp.start()
        for landed, onward in passed:
            landed.wait_recv()
            onward.start()
        for cp in arrive:
            cp.wait_recv()
        for cp in first + [onward for _, onward in passed]:
            cp.wait_send()
        for cp in local:
            cp.wait()

    return pl.pallas_call(
        body, name=name,
        in_specs=[pl.BlockSpec(memory_space=pl.ANY)] * n, out_specs=[pl.BlockSpec(memory_space=pl.ANY)] * n,
        out_shape=[jax.ShapeDtypeStruct((N_DEV,) + tuple(x.shape), x.dtype) for x in xs],
        scratch_shapes=[pltpu.SemaphoreType.DMA((n, N_DEV - 1)), pltpu.SemaphoreType.DMA((n, N_DEV - 1)),
                        pltpu.SemaphoreType.DMA((n,))],
        compiler_params=pltpu.CompilerParams(has_side_effects=True, vmem_limit_bytes=VMEM_LIMIT_BYTES),
    )(*xs)


def _sum_slabs(x, *, name):
    n, r, c = x.shape
    br = _row_blk(r, ROW_BLOCK)

    def body(x_ref, o_ref):
        acc = x_ref[0]
        for p in range(1, n):
            acc = acc + x_ref[p]
        o_ref[...] = acc

    return pl.pallas_call(
        body, name=name, grid=(r // br,), in_specs=[pl.BlockSpec((n, br, c), lambda i: (0, i, 0))],
        out_specs=pl.BlockSpec((br, c), lambda i: (i, 0)),
        out_shape=jax.ShapeDtypeStruct((r, c), F32), compiler_params=_params("parallel"),
    )(x)


def _silu(x, *, name):
    def body(x_ref, o_ref):
        xv = x_ref[...]
        o_ref[...] = (xv * jax.nn.sigmoid(xv)).astype(BF16)

    return pl.pallas_call(body, name=name, out_shape=jax.ShapeDtypeStruct(x.shape, BF16),
                          compiler_params=_params())(x)


_BIG = {"fox_w_in": 2, "fox_w_o": 1, "mla_w_a": 1, "mla_w_uq": 2, "mla_w_ukv": 2, "mla_w_o": 1, "ffn_w_in": 2, "ffn_w_out": 1}
_SMALL = {"mla_g_q": 1, "mla_g_kv": 1, "ffn_conv_w": 2}
_REPL = ("fox_b_f", "ffn_conv_b", "final_g")


def _gathered_to_full(g, axis):
    full = jnp.moveaxis(g, 0, axis)
    shape = list(full.shape)
    shape[axis:axis + 2] = [shape[axis] * shape[axis + 1]]
    return full.reshape(shape)


def _full_to_chunks(full, axis):
    shape = list(full.shape)
    shape[axis:axis + 1] = [N_DEV, shape[axis] // N_DEV]
    return jnp.moveaxis(full.reshape(shape), axis, 0)


def _per_head(parts, s_or_rows):
    return jnp.concatenate([p.reshape(s_or_rows, N_HEADS, -1) for p in parts], axis=-1).reshape(s_or_rows, -1)


def kernel(x, c, ada_w, ada_b, fox_w_in, fox_b_f, fox_w_o, mla_w_a, mla_g_q, mla_g_kv, mla_w_uq, mla_w_ukv, mla_w_o, ffn_w_in, ffn_conv_w, ffn_conv_b, ffn_w_out, final_g, loss_target, m_ada_w, m_ada_b, m_fox_w_in, m_fox_b_f, m_fox_w_o, m_mla_w_a, m_mla_g_q, m_mla_g_kv, m_mla_w_uq, m_mla_w_ukv, m_mla_w_o, m_ffn_w_in, m_ffn_conv_w, m_ffn_conv_b, m_ffn_w_out, m_final_g, v_ada_w, v_ada_b, v_fox_w_in, v_fox_b_f, v_fox_w_o, v_mla_w_a, v_mla_g_q, v_mla_g_kv, v_mla_w_uq, v_mla_w_ukv, v_mla_w_o, v_ffn_w_in, v_ffn_conv_w, v_ffn_conv_b, v_ffn_w_out, v_final_g):
    weights = dict(ada_w=ada_w, ada_b=ada_b, fox_w_in=fox_w_in, fox_b_f=fox_b_f, fox_w_o=fox_w_o, mla_w_a=mla_w_a,
                   mla_g_q=mla_g_q, mla_g_kv=mla_g_kv, mla_w_uq=mla_w_uq, mla_w_ukv=mla_w_ukv, mla_w_o=mla_w_o,
                   ffn_w_in=ffn_w_in, ffn_conv_w=ffn_conv_w, ffn_conv_b=ffn_conv_b, ffn_w_out=ffn_w_out, final_g=final_g)
    mom_m = dict(ada_w=m_ada_w, ada_b=m_ada_b, fox_w_in=m_fox_w_in, fox_b_f=m_fox_b_f, fox_w_o=m_fox_w_o, mla_w_a=m_mla_w_a,
                 mla_g_q=m_mla_g_q, mla_g_kv=m_mla_g_kv, mla_w_uq=m_mla_w_uq, mla_w_ukv=m_mla_w_ukv, mla_w_o=m_mla_w_o,
                 ffn_w_in=m_ffn_w_in, ffn_conv_w=m_ffn_conv_w, ffn_conv_b=m_ffn_conv_b, ffn_w_out=m_ffn_w_out, final_g=m_final_g)
    mom_v = dict(ada_w=v_ada_w, ada_b=v_ada_b, fox_w_in=v_fox_w_in, fox_b_f=v_fox_b_f, fox_w_o=v_fox_w_o, mla_w_a=v_mla_w_a,
                 mla_g_q=v_mla_g_q, mla_g_kv=v_mla_g_kv, mla_w_uq=v_mla_w_uq, mla_w_ukv=v_mla_w_ukv, mla_w_o=v_mla_w_o,
                 ffn_w_in=v_ffn_w_in, ffn_conv_w=v_ffn_conv_w, ffn_conv_b=v_ffn_conv_b, ffn_w_out=v_ffn_w_out, final_g=v_final_g)
    order = list(weights)
    x0 = x[0]
    target = loss_target[0]
    s = x0.shape[0]
    d = D_MODEL
    cols = ada_w.shape[-1]
    nq = N_HEADS * HEAD_DIM

    small_names = ["c"] + list(_SMALL)
    small_all = dict(zip(small_names, _exchange([c] + [weights[n] for n in _SMALL], same_src=True, name="gather_small")))
    c_all = small_all["c"].reshape(N_DEV, d)
    g_q = _gathered_to_full(small_all["mla_g_q"], 1)
    g_kv = _gathered_to_full(small_all["mla_g_kv"], 1)
    conv_w = _gathered_to_full(small_all["ffn_conv_w"], 2)

    c_pad = _pad_axis(c_all, 0, LANES)
    silu_c = _silu(c_pad, name="silu_c")
    w_ada = ada_w.reshape(4, d, cols)
    b_ada = ada_b.reshape(4, 1, cols)
    mods = [_matmul(silu_c, w_ada[i], name=f"ada_mod{i}")[:N_DEV] + b_ada[i] for i in range(4)]
    mod_send = _pad_axis(jnp.stack(mods, axis=1), 1, 8)
    mod_recv, = _exchange([mod_send], same_src=False, name="scatter_mod")
    mod = mod_recv[:, :4].transpose(1, 0, 2).reshape(4, 3 * d)
    shift = [mod[i:i + 1, 0:d] for i in range(4)]
    scale = [mod[i:i + 1, d:2 * d] for i in range(4)]
    gate = [mod[i:i + 1, 2 * d:3 * d] for i in range(4)]

    big_all = _gather_two_level([weights[n].astype(BF16) for n in _BIG], name="gather_weights")
    wfull = {n: _gathered_to_full(g, _BIG[n]) for n, g in zip(_BIG, big_all)}

    w_fox_in = _pad_axis(wfull["fox_w_in"][0], 1, LANES)
    w_fox_qkv, w_fox_f = w_fox_in[:, :3 * nq], w_fox_in[:, 3 * nq:]
    w_fox_o = wfull["fox_w_o"][0]
    w_a = _pad_axis(wfull["mla_w_a"][0], 1, LANES)
    wq = wfull["mla_w_uq"][0].reshape(MLA_Q_RANK, N_HEADS, HEAD_DIM + MLA_ROPE_DIM)
    w_uq = _pad_axis(wq, 2, LANES).reshape(MLA_Q_RANK, N_HEADS * LANES)
    wkv = wfull["mla_w_ukv"][0].reshape(MLA_KV_RANK, N_HEADS, 2 * HEAD_DIM)
    w_ukv = jnp.concatenate([wkv[:, :, :HEAD_DIM].reshape(MLA_KV_RANK, -1), wkv[:, :, HEAD_DIM:].reshape(MLA_KV_RANK, -1)], axis=1)
    w_mla_o = wfull["mla_w_o"][0]
    w_ffn_in = wfull["ffn_w_in"]
    w_ffn_out = wfull["ffn_w_out"]
    conv_b = ffn_conv_b

    fox_scale = HEAD_DIM ** -0.5
    mla_scale = (HEAD_DIM + MLA_ROPE_DIM) ** -0.5
    pos = jnp.arange(s, dtype=F32)
    inv_freq = ROPE_BASE ** (-jnp.arange(0, MLA_ROPE_DIM, 2, dtype=F32) / MLA_ROPE_DIM)
    ang = pos[:, None] * inv_freq[None, :]
    cos16, sin16 = jnp.cos(ang), jnp.sin(ang)
    z16, z32, z64 = jnp.zeros((s, 16), F32), jnp.zeros((s, 32), F32), jnp.zeros((s, 64), F32)
    tab_a = jnp.concatenate([jnp.ones((s, 64), F32), cos16, cos16, z32], axis=1) * (mla_scale * LOG2E)
    tab_b = jnp.concatenate([z64, -sin16, z16, z32], axis=1) * (mla_scale * LOG2E)
    tab_c = jnp.concatenate([z64, z16, sin16, z32], axis=1) * (mla_scale * LOG2E)

    h0 = _norm_fwd(x0, scale[0], shift[0], plus_one=True, out_dtype=BF16, name="ada_fwd0")
    q_mult = jnp.concatenate([jnp.full((1, nq), fox_scale * LOG2E, F32), jnp.ones((1, 2 * nq), F32)], axis=1)
    qkv = _matmul(h0, w_fox_qkv, out_dtype=BF16, col_scale=q_mult, name="fox_proj")
    fl = _matmul(h0, w_fox_f, name="fox_proj_f")[:, :N_HEADS]
    cum = _fox_gate_fwd(fl, fox_b_f, name="fox_gate_fwd")
    fox_cfg = dict(qc=0, kc=N_HEADS // 2, vc=N_HEADS, width=HEAD_DIM)
    kb2 = cum * -LOG2E
    att_blk = _blk(s, ATT_BLOCK)
    fvt = _matmul(w_fox_qkv[:, 2 * nq:], h0, ta=True, tb=True, out_dtype=BF16, name="fox_proj_vt")
    fo, fox_lse = _attn_fwd(qkv, qkv, _key_blocks(fvt, att_blk), _bias_lane_terms(kb2), qc=0, kc=N_HEADS // 2,
                            width=HEAD_DIM, name="fox_attn_fwd")
    fox_res = (qkv, qkv, qkv, kb2, fo, fox_lse)
    y0 = _matmul(fo, w_fox_o, name="fox_out")
    x1 = _resid_fwd(x0, y0, gate[0], name="resid_fwd0")

    def ffn_fwd(xin, li, sub):
        hh = _norm_fwd(xin, scale[sub], shift[sub], plus_one=True, out_dtype=BF16, name=f"ada_fwd{sub}")
        u = _matmul(hh, w_ffn_in[li], name=f"ffn_up{li}")
        g = _conv_gate_fwd(u, conv_w[li], conv_b[li:li + 1], name=f"conv_fwd{li}")
        y = _matmul(g, w_ffn_out[li], name=f"ffn_down{li}")
        return _resid_fwd(xin, y, gate[sub], name=f"resid_fwd{sub}"), (hh, u, g, y)

    x2, ffn0_res = ffn_fwd(x1, 0, 1)

    h2 = _norm_fwd(x2, scale[2], shift[2], plus_one=True, out_dtype=BF16, name="ada_fwd2")
    a = _matmul(h2, w_a, name="mla_a")
    a_q, a_kv = a[:, :MLA_Q_RANK], a[:, MLA_Q_RANK:MLA_Q_RANK + MLA_KV_RANK]
    kr1 = a[:, MLA_Q_RANK + MLA_KV_RANK:MLA_Q_RANK + MLA_KV_RANK + MLA_ROPE_HALF]
    kr2 = a[:, MLA_Q_RANK + MLA_KV_RANK + MLA_ROPE_HALF:MLA_Q_RANK + MLA_KV_RANK + MLA_ROPE_DIM]
    cq = _norm_fwd(a_q, g_q, jnp.zeros_like(g_q), plus_one=False, out_dtype=BF16, name="mla_norm_q")
    ckv = _norm_fwd(a_kv, g_kv, jnp.zeros_like(g_kv), plus_one=False, out_dtype=BF16, name="mla_norm_kv")
    qf = _matmul(cq, w_uq, name="mla_uq")
    kvf = _matmul(ckv, w_ukv, out_dtype=BF16, name="mla_ukv")
    mq = _rope_heads(qf, tab_a, tab_b, tab_c, out_dtype=BF16, name="rope_q")
    kk1, kk2 = _rope(kr1, kr2, cos16, sin16, negate=False, name="rope_k")
    k_tail = jnp.concatenate([kk1, kk2, z32], axis=1).astype(BF16)
    mk = jnp.concatenate([kvf[:, :nq].reshape(s, N_HEADS, HEAD_DIM),
                          jnp.broadcast_to(k_tail[:, None, :], (s, N_HEADS, HEAD_DIM))], axis=-1).reshape(s, N_HEADS * LANES)
    mla_cfg = dict(qc=0, kc=0, vc=N_HEADS // 2, width=LANES)
    mvt = _matmul(w_ukv[:, nq:], ckv, ta=True, tb=True, out_dtype=BF16, name="mla_ukv_vt")
    mo, mla_lse = _attn_fwd(mq, mk, _key_blocks(mvt, att_blk), None, qc=0, kc=0, width=LANES, name="mla_attn_fwd")
    mla_res = (mq, mk, kvf, None, mo, mla_lse)
    y2 = _matmul(mo, w_mla_o, name="mla_out")
    x3 = _resid_fwd(x2, y2, gate[2], name="resid_fwd2")

    x4, ffn1_res = ffn_fwd(x3, 1, 3)

    loss_vec, dx4, d_final_g = _final_loss(x4, final_g.reshape(1, d), target, name="final_loss")
    loss = lax.psum(loss_vec[0, 0], ("x", "y", "c"))

    grads = {}
    dmod = [None] * 4

    def ffn_bwd(dx_out, xin, li, sub, res):
        hh, u, g, y = res
        dy, dgate = _resid_bwd(dx_out, y, gate[sub], name=f"resid_bwd{sub}")
        gw_out = _matmul(g, dy, ta=True, out_dtype=BF16, name=f"ffn_down_dw{li}")
        dg = _matmul(dy, w_ffn_out[li], tb=True, name=f"ffn_down_dx{li}")
        du, dcw, dcb = _conv_gate_bwd(u, conv_w[li], conv_b[li:li + 1], dg, name=f"conv_bwd{li}")
        gw_in = _matmul(hh, du, ta=True, out_dtype=BF16, name=f"ffn_up_dw{li}")
        dh = _matmul(du, w_ffn_in[li], tb=True, out_dtype=BF16, name=f"ffn_up_dx{li}")
        dx_in, dscale, dshift = _norm_bwd(xin, scale[sub], dh, dx_out, plus_one=True, name=f"ada_bwd{sub}")
        dmod[sub] = jnp.concatenate([dshift, dscale, dgate], axis=1)
        return dx_in, gw_in, dcw, dcb, gw_out

    dx3, gw_in1, dcw1, dcb1, gw_out1 = ffn_bwd(dx4, x3, 1, 3, ffn1_res)

    dy2, dgate2 = _resid_bwd(dx3, y2, gate[2], name="resid_bwd2")
    grads["mla_w_o"] = _matmul(mo, dy2, ta=True, out_dtype=BF16, name="mla_out_dw")[None]
    dmo = _matmul(dy2, w_mla_o, tb=True, out_dtype=BF16, name="mla_out_dx")
    dmq, dmk, dmv, _ = _attention_bwd(mla_res, dmo, dq_mult=1.0 / LOG2E, dk_mult=1.0 / LOG2E, out_dtype=F32,
                                      name="mla_attn", **mla_cfg)
    dqf = _rope_heads(dmq, tab_a, -tab_b, -tab_c, out_dtype=BF16, name="rope_q_bwd")
    g_uq = _matmul(cq, dqf, ta=True, out_dtype=BF16, name="mla_uq_dw")
    dcq = _matmul(dqf, w_uq, tb=True, name="mla_uq_dx")
    dmk3 = dmk.reshape(s, N_HEADS, LANES)
    dkr = _group_sum(dmk, name="mla_krope_sum")
    dkr1, dkr2 = _rope(dkr[:, HEAD_DIM:HEAD_DIM + MLA_ROPE_HALF], dkr[:, HEAD_DIM + MLA_ROPE_HALF:HEAD_DIM + MLA_ROPE_DIM],
                       cos16, sin16, negate=True, name="rope_k_bwd")
    dkvf = jnp.concatenate([dmk3[:, :, :HEAD_DIM].reshape(s, nq).astype(BF16), dmv.astype(BF16)], axis=1)
    g_ukv = _matmul(ckv, dkvf, ta=True, out_dtype=BF16, name="mla_ukv_dw")
    dckv = _matmul(dkvf, w_ukv, tb=True, name="mla_ukv_dx")
    da_q, dg_q, _ = _norm_bwd(a_q, g_q, dcq, None, plus_one=False, name="mla_norm_q_bwd")
    da_kv, dg_kv, _ = _norm_bwd(a_kv, g_kv, dckv, None, plus_one=False, name="mla_norm_kv_bwd")
    da = jnp.concatenate([da_q, da_kv, dkr1, dkr2, jnp.zeros((s, w_a.shape[1] - 672), F32)], axis=1).astype(BF16)
    grads["mla_w_a"] = _matmul(h2, da, ta=True, out_dtype=BF16, name="mla_a_dw")[None, :, :672]
    dh2 = _matmul(da, w_a, tb=True, out_dtype=BF16, name="mla_a_dx")
    dx2, dscale2, dshift2 = _norm_bwd(x2, scale[2], dh2, dx3, plus_one=True, name="ada_bwd2")
    dmod[2] = jnp.concatenate([dshift2, dscale2, dgate2], axis=1)
    grads["mla_w_uq"] = g_uq.reshape(MLA_Q_RANK, N_HEADS, LANES)[:, :, :HEAD_DIM + MLA_ROPE_DIM].reshape(1, MLA_Q_RANK, -1)
    grads["mla_w_ukv"] = _per_head([g_ukv[:, :nq], g_ukv[:, nq:]], MLA_KV_RANK)[None]
    grads["mla_g_q"], grads["mla_g_kv"] = dg_q, dg_kv

    dx1, gw_in0, dcw0, dcb0, gw_out0 = ffn_bwd(dx2, x1, 0, 1, ffn0_res)
    grads["ffn_w_in"] = jnp.stack([gw_in0, gw_in1])
    grads["ffn_w_out"] = jnp.stack([gw_out0, gw_out1])
    grads["ffn_conv_w"] = jnp.stack([dcw0, dcw1])
    g_conv_b = jnp.concatenate([dcb0, dcb1], axis=0)

    dy0, dgate0 = _resid_bwd(dx1, y0, gate[0], name="resid_bwd0")
    grads["fox_w_o"] = _matmul(fo, dy0, ta=True, out_dtype=BF16, name="fox_out_dw")[None]
    dfo = _matmul(dy0, w_fox_o, tb=True, out_dtype=BF16, name="fox_out_dx")
    dfq, dfk, dfv, dcum = _attention_bwd(fox_res, dfo, dq_mult=fox_scale, dk_mult=1.0 / LOG2E, out_dtype=BF16,
                                         name="fox_attn", **fox_cfg)
    dfl, g_b_f = _fox_gate_bwd(fl, fox_b_f, dcum, name="fox_gate_bwd")
    dproj = jnp.concatenate([dfq.astype(BF16), dfk, dfv, _pad_axis(dfl, 1, LANES).astype(BF16)], axis=1)
    grads["fox_w_in"] = _matmul(h0, dproj, ta=True, out_dtype=BF16, name="fox_proj_dw")[None, :, :3 * nq + N_HEADS]
    dh0 = _matmul(dproj, w_fox_in, tb=True, out_dtype=BF16, name="fox_proj_dx")
    dx0, dscale0, dshift0 = _norm_bwd(x0, scale[0], dh0, dx1, plus_one=True, name="ada_bwd0")
    dmod[0] = jnp.concatenate([dshift0, dscale0, dgate0], axis=1)

    dmod_send = _pad_axis(jnp.stack(dmod, axis=0).reshape(4, N_DEV, cols).transpose(1, 0, 2), 1, 8)
    dmod_recv, = _exchange([dmod_send], same_src=False, name="scatter_dmod")
    dmod_all = dmod_recv[:, :4]
    dmod_pad = _pad_axis(dmod_all, 0, LANES)
    g_ada_w = jnp.stack([_matmul(silu_c, dmod_pad[:, i], ta=True, name=f"ada_dw{i}") for i in range(4)])
    grads["ada_w"] = g_ada_w.reshape(ada_w.shape)
    grads["ada_b"] = _sum_slabs(dmod_recv, name="ada_db")[:4].reshape(ada_b.shape)

    sharded = list(_BIG) + list(_SMALL)
    axes = {**_BIG, **_SMALL}
    recv = _exchange([_full_to_chunks(grads[n], axes[n]) for n in sharded], same_src=False, name="scatter_grads")
    grads.update(dict(zip(sharded, recv)))
    repl = _exchange([g_b_f, g_conv_b, d_final_g], same_src=True, name="gather_repl_grads")
    grads.update(dict(zip(_REPL, repl)))

    grad_out, deltas, new_m, new_v = {}, {}, {}, {}
    for n in order:
        grad_out[n], deltas[n], new_m[n], new_v[n] = _adamw(
            weights[n], grads[n], mom_m[n], mom_v[n], slabs=n in axes or n in _REPL, name=f"adamw_{n}")

    grad_x = dx0[None]
    return (loss, grad_x, *[grad_out[n] for n in order], *[deltas[n] for n in order],
            *[new_m[n] for n in order], *[new_v[n] for n in order])
```

```python
import jax
import jax.numpy as jnp
from jax import lax
from jax.experimental import pallas as pl
from jax.experimental.pallas import tpu as pltpu

F32 = jnp.float32
BF16 = jnp.bfloat16
HIGHEST = lax.Precision.HIGHEST

N_DEV = 8
D_MODEL = 1024
N_HEADS = 16
HEAD_DIM = 64
MLA_ROPE_HALF = 16
MLA_Q_RANK = 384
MLA_KV_RANK = 256
MLA_ROPE_DIM = 32
NORM_EPS = 1e-6
ROPE_BASE = 10000.0
ADAM_LR = 0.001
ADAM_B1 = 0.9
ADAM_B2 = 0.999
ADAM_EPS = 1e-08
ADAM_WD = 0.01
ADAM_STEP = 10

LANES = 128
VMEM_LIMIT_BYTES = 56 * 1024 * 1024
ROW_BLOCK = 512
ATT_BLOCK = 512
CONV_ROWS = 1024
MM_BM, MM_BN, MM_BK = 512, 1408, 2048
MM_K_WHOLE = 3328
LOG2E = 1.4426950408889634


def _params(*sem):
    return pltpu.CompilerParams(dimension_semantics=sem or None, vmem_limit_bytes=VMEM_LIMIT_BYTES)


def _blk(dim, pref):
    if dim <= pref:
        return dim
    b = pref - pref % LANES
    while b >= LANES:
        if dim % b == 0:
            return b
        b -= LANES
    raise ValueError(f"no block for {dim}")


def _row_blk(rows, pref):
    if rows <= pref:
        return rows
    for b in range(pref - pref % 8, 7, -8):
        if rows % b == 0:
            return b
    return rows


def _pad_axis(a, axis, mult):
    pad = (-a.shape[axis]) % mult
    if pad == 0:
        return a
    widths = [(0, 0)] * a.ndim
    widths[axis] = (0, pad)
    return jnp.pad(a, widths)


def _matmul(a, b, *, ta=False, tb=False, out_dtype=F32, col_scale=None, name):
    m, k = (a.shape[1], a.shape[0]) if ta else a.shape
    n = b.shape[0] if tb else b.shape[1]
    assert (b.shape[1] if tb else b.shape[0]) == k, (a.shape, b.shape, ta, tb)
    bm, bn = _blk(m, MM_BM if ta else 2 * MM_BM), _blk(n, MM_BN)
    bk = k if k <= MM_K_WHOLE else _blk(k, MM_BK)
    nk = k // bk
    dims = (((0 if ta else 1,), (1 if tb else 0,)), ((), ()))
    has_scale = col_scale is not None
    use_acc = nk > 1 and (out_dtype != F32 or has_scale)

    def body(*refs):
        a_ref, b_ref = refs[0], refs[1]
        s_ref = refs[2] if has_scale else None
        o_ref = refs[3] if has_scale else refs[2]
        acc_ref = refs[-1] if use_acc else o_ref
        kk = pl.program_id(2)
        part = lax.dot_general(a_ref[...].astype(BF16), b_ref[...].astype(BF16), dims, preferred_element_type=F32)

        def finish(val):
            if has_scale:
                val = val * s_ref[...]
            o_ref[...] = val.astype(out_dtype)

        if nk == 1:
            finish(part)
            return

        @pl.when(kk == 0)
        def _():
            acc_ref[...] = part

        @pl.when(kk > 0)
        def _():
            acc_ref[...] += part

        if use_acc:
            @pl.when(kk == nk - 1)
            def _():
                finish(acc_ref[...])

    a_spec = pl.BlockSpec((bk, bm), lambda i, j, kk: (kk, i)) if ta else pl.BlockSpec((bm, bk), lambda i, j, kk: (i, kk))
    b_spec = pl.BlockSpec((bn, bk), lambda i, j, kk: (j, kk)) if tb else pl.BlockSpec((bk, bn), lambda i, j, kk: (kk, j))
    return pl.pallas_call(
        body, name=name, grid=(m // bm, n // bn, nk),
        in_specs=[a_spec, b_spec] + ([pl.BlockSpec((1, bn), lambda i, j, kk: (0, j))] if has_scale else []),
        out_specs=pl.BlockSpec((bm, bn), lambda i, j, kk: (i, j)),
        out_shape=jax.ShapeDtypeStruct((m, n), out_dtype),
        scratch_shapes=[pltpu.VMEM((bm, bn), F32)] if use_acc else [],
        compiler_params=_params("parallel", "parallel", "arbitrary"),
    )(*([a, b] + ([col_scale] if has_scale else [])))


def _norm_fwd(x, mul, add, *, plus_one, out_dtype, name):
    s, n = x.shape
    bs = _blk(s, ROW_BLOCK)

    def body(x_ref, m_ref, a_ref, o_ref):
        xv = x_ref[...]
        r = lax.rsqrt(jnp.mean(xv * xv, axis=-1, keepdims=True) + NORM_EPS)
        mv = m_ref[...] + 1.0 if plus_one else m_ref[...]
        o_ref[...] = (xv * r * mv + a_ref[...]).astype(out_dtype)

    row = pl.BlockSpec((bs, n), lambda i: (i, 0))
    vec = pl.BlockSpec((1, n), lambda i: (0, 0))
    return pl.pallas_call(
        body, name=name, grid=(s // bs,), in_specs=[row, vec, vec], out_specs=row,
        out_shape=jax.ShapeDtypeStruct((s, n), out_dtype), compiler_params=_params("parallel"),
    )(x, mul, add)


def _norm_bwd(x, mul, dy, dres, *, plus_one, name):
    s, n = x.shape
    bs = _blk(s, ROW_BLOCK)
    has_res = dres is not None

    def body(*refs):
        if has_res:
            x_ref, m_ref, dy_ref, dres_ref, dx_ref, dm_ref, da_ref = refs
        else:
            x_ref, m_ref, dy_ref, dx_ref, dm_ref, da_ref = refs
        xv = x_ref[...]
        dyv = dy_ref[...].astype(F32)
        r = lax.rsqrt(jnp.mean(xv * xv, axis=-1, keepdims=True) + NORM_EPS)
        xn = xv * r
        mv = m_ref[...] + 1.0 if plus_one else m_ref[...]
        g = dyv * mv
        dx = r * (g - xn * jnp.mean(g * xn, axis=-1, keepdims=True))
        if has_res:
            dx = dx + dres_ref[...]
        dx_ref[...] = dx

        @pl.when(pl.program_id(0) == 0)
        def _():
            dm_ref[...] = jnp.zeros_like(dm_ref)
            da_ref[...] = jnp.zeros_like(da_ref)

        dm_ref[...] += jnp.sum(dyv * xn, axis=0, keepdims=True)
        da_ref[...] += jnp.sum(dyv, axis=0, keepdims=True)

    row = pl.BlockSpec((bs, n), lambda i: (i, 0))
    vec = pl.BlockSpec((1, n), lambda i: (0, 0))
    ins = [x, mul, dy] + ([dres] if has_res else [])
    return pl.pallas_call(
        body, name=name, grid=(s // bs,),
        in_specs=[row, vec, row] + ([row] if has_res else []), out_specs=[row, vec, vec],
        out_shape=[jax.ShapeDtypeStruct((s, n), F32), jax.ShapeDtypeStruct((1, n), F32), jax.ShapeDtypeStruct((1, n), F32)],
        compiler_params=_params("arbitrary"),
    )(*ins)


def _resid_fwd(x, y, gate, *, name):
    s, n = x.shape
    bs = _blk(s, ROW_BLOCK)

    def body(x_ref, y_ref, g_ref, o_ref):
        o_ref[...] = x_ref[...] + g_ref[...] * y_ref[...]

    row = pl.BlockSpec((bs, n), lambda i: (i, 0))
    vec = pl.BlockSpec((1, n), lambda i: (0, 0))
    return pl.pallas_call(
        body, name=name, grid=(s // bs,), in_specs=[row, row, vec], out_specs=row,
        out_shape=jax.ShapeDtypeStruct((s, n), F32), compiler_params=_params("parallel"),
    )(x, y, gate)


def _resid_bwd(dx, y, gate, *, name):
    s, n = dx.shape
    bs = _blk(s, ROW_BLOCK)

    def body(dx_ref, y_ref, g_ref, dy_ref, dg_ref):
        dxv = dx_ref[...]
        dy_ref[...] = (g_ref[...] * dxv).astype(BF16)

        @pl.when(pl.program_id(0) == 0)
        def _():
            dg_ref[...] = jnp.zeros_like(dg_ref)

        dg_ref[...] += jnp.sum(dxv * y_ref[...], axis=0, keepdims=True)

    row = pl.BlockSpec((bs, n), lambda i: (i, 0))
    vec = pl.BlockSpec((1, n), lambda i: (0, 0))
    return pl.pallas_call(
        body, name=name, grid=(s // bs,), in_specs=[row, row, vec], out_specs=[row, vec],
        out_shape=[jax.ShapeDtypeStruct((s, n), BF16), jax.ShapeDtypeStruct((1, n), F32)],
        compiler_params=_params("arbitrary"),
    )(dx, y, gate)


def _final_loss(x, g, target, *, name):
    s, n = x.shape
    bs = _blk(s, ROW_BLOCK)

    def body(x_ref, g_ref, t_ref, loss_ref, dx_ref, dg_ref):
        xv = x_ref[...]
        r = lax.rsqrt(jnp.mean(xv * xv, axis=-1, keepdims=True) + NORM_EPS)
        xn = xv * r
        gv = g_ref[...]
        err = xn * gv - t_ref[...]
        dout = err * (1.0 / n)
        gg = dout * gv
        dx_ref[...] = r * (gg - xn * jnp.mean(gg * xn, axis=-1, keepdims=True))

        @pl.when(pl.program_id(0) == 0)
        def _():
            loss_ref[...] = jnp.zeros_like(loss_ref)
            dg_ref[...] = jnp.zeros_like(dg_ref)

        part = jnp.sum(jnp.sum(err * err, axis=-1, keepdims=True), axis=0, keepdims=True) * (0.5 / n)
        loss_ref[...] += jnp.broadcast_to(part, loss_ref.shape)
        dg_ref[...] += jnp.sum(dout * xn, axis=0, keepdims=True)

    row = pl.BlockSpec((bs, n), lambda i: (i, 0))
    vec = pl.BlockSpec((1, n), lambda i: (0, 0))
    return pl.pallas_call(
        body, name=name, grid=(s // bs,), in_specs=[row, vec, row],
        out_specs=[pl.BlockSpec((1, LANES), lambda i: (0, 0)), row, vec],
        out_shape=[jax.ShapeDtypeStruct((1, LANES), F32), jax.ShapeDtypeStruct((s, n), F32), jax.ShapeDtypeStruct((1, n), F32)],
        compiler_params=_params("arbitrary"),
    )(x, g, target)


def _lane_lt64(shape):
    return lax.broadcasted_iota(jnp.int32, shape, 1) < HEAD_DIM


def _keep_low(x):
    return jnp.where(_lane_lt64(x.shape), x.astype(F32), 0.0).astype(x.dtype)


def _keep_high(x):
    return jnp.where(_lane_lt64(x.shape), 0.0, x.astype(F32)).astype(x.dtype)


def _lane_merge(a, b):
    n = max(a.shape[0], b.shape[0])
    return jnp.where(_lane_lt64((n, LANES)), a, b)


def _pair(x, width, masked):
    if width == HEAD_DIM:
        return (_keep_low(x), _keep_high(x)) if masked else (x, x)
    return x[:, :LANES], x[:, LANES:]


def _qk_t(a, b):
    return lax.dot_general(a, b, (((1,), (1,)), ((), ())), preferred_element_type=F32)


def _attn_specs(s, blk, width, cols, resident):
    w = 2 * width
    if resident:
        return pl.BlockSpec((s, w), lambda p, i: (0, cols + p))
    return pl.BlockSpec((blk, w), lambda p, i: (i, cols + p))


BIAS_TERMS = 3


def _key_blocks(vt, blk):
    return vt.reshape(vt.shape[0], vt.shape[1] // blk, blk).transpose(1, 0, 2)


def _attn_fwd(q, k, vt, kbl, *, qc, kc, width, name):
    s = q.shape[0]
    blk = _blk(s, ATT_BLOCK)
    nb = s // blk
    has_bias = kbl is not None
    assert has_bias == (width == HEAD_DIM)

    def body(*refs):
        if has_bias:
            q_ref, k_ref, vt_ref, kbl_ref, o_ref, lse_ref = refs
        else:
            q_ref, k_ref, vt_ref, o_ref, lse_ref = refs
        i = pl.program_id(1)
        q2 = q_ref[...]
        if has_bias:
            lane = lax.broadcasted_iota(jnp.int32, (blk, LANES), 1)
            qf = q2.astype(F32)
            qh = (jnp.where(lane < HEAD_DIM, qf, jnp.where(lane < HEAD_DIM + BIAS_TERMS, 1.0, 0.0)).astype(BF16),
                  jnp.where(lane >= HEAD_DIM, qf, jnp.where(lane < BIAS_TERMS, 1.0, 0.0)).astype(BF16))
        else:
            qh = (q2[:, :LANES], q2[:, LANES:])

        def step(j, carry, nblk, diag):
            rows = pl.ds(pl.multiple_of(j * blk, blk), nblk * blk)
            vt1 = [jnp.concatenate([vt_ref[j + b], jnp.ones((16, blk), BF16)], axis=0) for b in range(nblk)]
            k2 = k_ref[rows, :]
            if has_bias:
                low = _lane_lt64(k2.shape)
                kf, bf = k2.astype(F32), kbl_ref[rows, :].astype(F32)
                kh = (jnp.where(low, kf, bf).astype(BF16), jnp.where(low, bf, kf).astype(BF16))
            else:
                kh = (k2[:, :LANES], k2[:, LANES:])
            out = []
            for hd in range(2):
                m, acc = carry[hd]
                st = _qk_t(kh[hd], qh[hd])
                if diag:
                    row = lax.broadcasted_iota(jnp.int32, (blk, blk), 0)
                    colq = lax.broadcasted_iota(jnp.int32, (blk, blk), 1)
                    st = jnp.where(row <= colq, st, -1e30)
                m_new = jnp.maximum(m, jnp.max(st, axis=0, keepdims=True))
                alpha = jnp.exp2(m - m_new)
                pt = jnp.exp2(st - m_new).astype(BF16)
                acc = alpha * acc
                for b in range(nblk):
                    acc = acc + jnp.dot(vt1[b], pt[b * blk:(b + 1) * blk], preferred_element_type=F32)
                out.append((m_new, acc))
            return tuple(out)

        one = (jnp.full((1, blk), -1e30, F32), jnp.zeros((LANES + 16, blk), F32))
        carry = lax.fori_loop(0, i // 8, lambda j, c: step(8 * j, c, 8, False), (one, one))
        carry = lax.fori_loop(0, (i % 8) // 4, lambda _, c: step(i - i % 8, c, 4, False), carry)
        carry = lax.fori_loop(0, (i % 4) // 2, lambda _, c: step(i - i % 4, c, 2, False), carry)
        carry = lax.fori_loop(0, i % 2, lambda _, c: step(i - 1, c, 1, False), carry)
        (ma, acca), (mb, accb) = step(i, carry, 1, True)
        la = jnp.max(acca[LANES:LANES + 8], axis=0, keepdims=True)
        lb = jnp.max(accb[LANES:LANES + 8], axis=0, keepdims=True)
        acca, accb = acca[0:LANES], accb[0:LANES]
        low = lax.broadcasted_iota(jnp.int32, (LANES, blk), 0) < HEAD_DIM
        o_ref[...] = jnp.where(low, acca / la, accb / lb).T
        lse_ref[0, 0] = ma + jnp.log(la) * LOG2E
        lse_ref[1, 0] = mb + jnp.log(lb) * LOG2E

    ins = [q, k, vt] + ([kbl] if has_bias else [])
    return pl.pallas_call(
        body, name=name, grid=(N_HEADS // 2, nb),
        in_specs=[_attn_specs(s, blk, width, qc, False), _attn_specs(s, blk, width, kc, True),
                  pl.BlockSpec((nb, LANES, blk), lambda p, i: (0, p, 0))]
                 + ([_attn_specs(s, blk, HEAD_DIM, 0, True)] if has_bias else []),
        out_specs=[pl.BlockSpec((blk, LANES), lambda p, i: (i, p)), pl.BlockSpec((2, 1, 1, blk), lambda p, i: (p, i, 0, 0))],
        out_shape=[jax.ShapeDtypeStruct((s, N_HEADS * HEAD_DIM), F32), jax.ShapeDtypeStruct((N_HEADS, nb, 1, blk), F32)],
        compiler_params=_params("parallel", "parallel"),
    )(*ins)


def _bias_lane_terms(kb2):
    terms, rest = [], kb2
    for _ in range(BIAS_TERMS):
        t = lax.reduce_precision(rest, 8, 7)
        terms.append(t.astype(BF16))
        rest = rest - t
    place = [[0.0] * (N_HEADS * HEAD_DIM) for _ in range(LANES)]
    for t in range(BIAS_TERMS):
        for h in range(N_HEADS):
            place[t * N_HEADS + h][(h // 2) * LANES + (HEAD_DIM if h % 2 == 0 else 0) + t] = 1.0
    return _matmul(_pad_axis(jnp.concatenate(terms, axis=1), 1, LANES), jnp.asarray(place, BF16), out_dtype=BF16,
                   name="fox_bias_lanes")


def _causal_keep(n):
    row = lax.broadcasted_iota(jnp.int32, (n, n), 0)
    col = lax.broadcasted_iota(jnp.int32, (n, n), 1)
    return col <= row


def _attn_delta(o, do, *, name):
    s, n = o.shape
    bs = _blk(s, ROW_BLOCK)

    def body(o_ref, do_ref, d_ref):
        for g in range(n // LANES):
            prod = do_ref[:, g * LANES:(g + 1) * LANES].astype(F32) * o_ref[:, g * LANES:(g + 1) * LANES]
            low = _lane_lt64(prod.shape)
            d_ref[:, g * LANES:(g + 1) * LANES] = _lane_merge(
                jnp.sum(jnp.where(low, prod, 0.0), axis=-1, keepdims=True),
                jnp.sum(jnp.where(low, 0.0, prod), axis=-1, keepdims=True))

    row = pl.BlockSpec((bs, n), lambda i: (i, 0))
    return pl.pallas_call(
        body, name=name, grid=(s // bs,), in_specs=[row, row], out_specs=row,
        out_shape=jax.ShapeDtypeStruct((s, n), F32), compiler_params=_params("parallel"),
    )(o, do)


def _attn_bwd(q, k, v, kb_col, do, lse_row, delta_row, *, qc, kc, vc, width, dq_mult, dk_mult, out_dtype, name):
    s = q.shape[0]
    blk = _blk(s, ATT_BLOCK)
    nb = s // blk
    has_bias = kb_col is not None

    def body(*refs):
        if has_bias:
            q_ref, k_ref, v_ref, kb_ref, do_ref, lse_ref, dl_ref, dk_ref, dv_ref, db_ref, dq_ref, dr_ref = refs
        else:
            q_ref, k_ref, v_ref, do_ref, lse_ref, dl_ref, dk_ref, dv_ref, db_ref, dq_ref = refs
        j = pl.program_id(1)

        @pl.when(j == 0)
        def _():
            dq_ref[...] = jnp.zeros_like(dq_ref)
            if has_bias:
                dr_ref[...] = jnp.zeros_like(dr_ref)

        kh = _pair(k_ref[...], width, True)
        v2 = v_ref[...]
        vh = (_keep_low(v2), _keep_high(v2))
        if has_bias:
            kb2 = kb_ref[0]
            kbh = (kb2[:, 0:1], kb2[:, 1:2])

        def step(i, carry, nblk, diag):
            rows = pl.ds(pl.multiple_of(i * blk, blk), nblk * blk)
            qh = _pair(q_ref[rows, :], width, False)
            doi = do_ref[rows, :]
            out, dq_parts = [], []
            for hd in range(2):
                dk, dvv, db = carry[hd]
                st = _qk_t(kh[hd], qh[hd])
                if has_bias:
                    st = st + kbh[hd]
                if diag:
                    row = lax.broadcasted_iota(jnp.int32, (blk, blk), 0)
                    colq = lax.broadcasted_iota(jnp.int32, (blk, blk), 1)
                    st = jnp.where(row <= colq, st, -1e30)
                lse_i = jnp.concatenate([lse_ref[hd, i + b] for b in range(nblk)], axis=1)
                delta_i = jnp.concatenate([dl_ref[hd, i + b] for b in range(nblk)], axis=1)
                pt = jnp.exp2(st - lse_i)
                dvv = dvv + jnp.dot(pt.astype(BF16), doi, preferred_element_type=F32)
                dst = pt * (_qk_t(vh[hd], doi) - delta_i)
                dsb = dst.astype(BF16)
                dk = dk + jnp.dot(dsb, qh[hd], preferred_element_type=F32)
                db = db + jnp.sum(dst, axis=-1, keepdims=True)
                dq_parts.append(lax.dot_general(dsb, kh[hd], (((0,), (0,)), ((), ())), preferred_element_type=F32))
                if has_bias:
                    rsum = jnp.sum(dst, axis=0, keepdims=True)
                    for b in range(nblk):
                        dr_ref[hd, i + b] += rsum[:, b * blk:(b + 1) * blk]
                out.append((dk, dvv, db))
            if width == HEAD_DIM:
                dq_ref[rows, :] += (dq_parts[0] + dq_parts[1]) * dq_mult
            else:
                dq_ref[rows, 0:LANES] += dq_parts[0] * dq_mult
                dq_ref[rows, LANES:2 * LANES] += dq_parts[1] * dq_mult
            return tuple(out)

        one = (jnp.zeros((blk, LANES), F32), jnp.zeros((blk, LANES), F32), jnp.zeros((blk, 1), F32))
        carry = step(j, (one, one), 1, True)
        rest = nb - 1 - j
        carry = lax.fori_loop(0, rest // 4, lambda t, c: step(j + 1 + 4 * t, c, 4, False), carry)
        carry = lax.fori_loop(0, (rest % 4) // 2, lambda _, c: step(nb - rest % 4, c, 2, False), carry)
        (dka, dva, dba), (dkb, dvb, dbb) = lax.fori_loop(0, rest % 2, lambda _, c: step(nb - 1, c, 1, False), carry)
        if width == HEAD_DIM:
            dk = _lane_merge(dka, dkb)
        else:
            dk = jnp.concatenate([dka, dkb], axis=1)
        dk_ref[...] = (dk * dk_mult).astype(out_dtype)
        dv_ref[...] = _lane_merge(dva, dvb).astype(out_dtype)
        db_ref[...] = _lane_merge(dba, dbb)

    stat = pl.BlockSpec((blk, LANES), lambda p, jj: (jj, p))
    rows = pl.BlockSpec((2, nb, 1, blk), lambda p, jj: (p, 0, 0, 0))
    ins = [q, k, v] + ([kb_col] if has_bias else []) + [do, lse_row, delta_row]
    return pl.pallas_call(
        body, name=name, grid=(N_HEADS // 2, nb),
        in_specs=[_attn_specs(s, blk, width, qc, True), _attn_specs(s, blk, width, kc, False),
                  _attn_specs(s, blk, HEAD_DIM, vc, False)]
                 + ([pl.BlockSpec((1, blk, 2), lambda p, jj: (p, jj, 0))] if has_bias else [])
                 + [pl.BlockSpec((s, LANES), lambda p, jj: (0, p)), rows, rows],
        out_specs=[pl.BlockSpec((blk, 2 * width), lambda p, jj: (jj, p)), stat, stat,
                   pl.BlockSpec((s, 2 * width), lambda p, jj: (0, p))] + ([rows] if has_bias else []),
        out_shape=[jax.ShapeDtypeStruct((s, N_HEADS * width), out_dtype), jax.ShapeDtypeStruct((s, N_HEADS * HEAD_DIM), out_dtype),
                   jax.ShapeDtypeStruct((s, N_HEADS * HEAD_DIM), F32), jax.ShapeDtypeStruct((s, N_HEADS * width), F32)]
                  + ([jax.ShapeDtypeStruct((N_HEADS, nb, 1, blk), F32)] if has_bias else []),
        compiler_params=_params("parallel", "arbitrary"),
    )(*ins)


def _head_stat(t):
    return t[:, ::HEAD_DIM]


def _stat_rows(t16, blk):
    s = t16.shape[0]
    return t16.T.reshape(N_HEADS, s // blk, 1, blk)


def _attention_bwd(res, do, *, qc, kc, vc, width, dq_mult, dk_mult, out_dtype, name):
    q, k, v, bias, o, lse_row = res
    s = q.shape[0]
    blk = _blk(s, ATT_BLOCK)
    kb_col = None if bias is None else bias.reshape(s, N_HEADS // 2, 2).transpose(1, 0, 2)
    delta_row = _stat_rows(_head_stat(_attn_delta(o, do, name=name + "_delta")), blk)
    outs = _attn_bwd(q, k, v, kb_col, do, lse_row, delta_row, qc=qc, kc=kc, vc=vc, width=width, dq_mult=dq_mult,
                     dk_mult=dk_mult, out_dtype=out_dtype, name=name + "_bwd")
    dk, dv, dcol, dq = outs[:4]
    if bias is None:
        return dq, dk, dv, None
    return dq, dk, dv, outs[4].reshape(N_HEADS, s).T - _head_stat(dcol)


def _fox_gate_fwd(fl, bf, *, name):
    s, n = fl.shape
    bs = _blk(s, ROW_BLOCK)

    def body(fl_ref, bf_ref, cum_ref, carry_ref):
        @pl.when(pl.program_id(0) == 0)
        def _():
            carry_ref[...] = jnp.zeros_like(carry_ref)

        z = fl_ref[...] + bf_ref[...]
        lf = jnp.minimum(z, 0.0) - jnp.log1p(jnp.exp(-jnp.abs(z)))
        row = lax.broadcasted_iota(jnp.int32, (bs, bs), 0)
        col = lax.broadcasted_iota(jnp.int32, (bs, bs), 1)
        tri = (col <= row).astype(F32)
        cum_ref[...] = jnp.dot(tri, lf, preferred_element_type=F32, precision=HIGHEST) + carry_ref[...]
        carry_ref[...] += jnp.sum(lf, axis=0, keepdims=True)

    return pl.pallas_call(
        body, name=name, grid=(s // bs,),
        in_specs=[pl.BlockSpec((bs, n), lambda i: (i, 0)), pl.BlockSpec((1, n), lambda i: (0, 0))],
        out_specs=pl.BlockSpec((bs, n), lambda i: (i, 0)),
        out_shape=jax.ShapeDtypeStruct((s, n), F32), scratch_shapes=[pltpu.VMEM((1, n), F32)],
        compiler_params=_params("arbitrary"),
    )(fl, bf)


def _fox_gate_bwd(fl, bf, dcum, *, name):
    s, n = fl.shape
    bs = _blk(s, ROW_BLOCK)
    nb = s // bs

    def body(fl_ref, bf_ref, dc_ref, dz_ref, dbf_ref, carry_ref):
        @pl.when(pl.program_id(0) == 0)
        def _():
            carry_ref[...] = jnp.zeros_like(carry_ref)
            dbf_ref[...] = jnp.zeros_like(dbf_ref)

        dc = dc_ref[...]
        row = lax.broadcasted_iota(jnp.int32, (bs, bs), 0)
        col = lax.broadcasted_iota(jnp.int32, (bs, bs), 1)
        tri = (col >= row).astype(F32)
        dlf = jnp.dot(tri, dc, preferred_element_type=F32, precision=HIGHEST) + carry_ref[...]
        carry_ref[...] += jnp.sum(dc, axis=0, keepdims=True)
        z = fl_ref[...] + bf_ref[...]
        dz = dlf / (1.0 + jnp.exp(z))
        dz_ref[...] = dz
        dbf_ref[...] += jnp.sum(dz, axis=0, keepdims=True)

    rev = pl.BlockSpec((bs, n), lambda i: (nb - 1 - i, 0))
    vec = pl.BlockSpec((1, n), lambda i: (0, 0))
    return pl.pallas_call(
        body, name=name, grid=(nb,), in_specs=[rev, vec, rev], out_specs=[rev, vec],
        out_shape=[jax.ShapeDtypeStruct((s, n), F32), jax.ShapeDtypeStruct((1, n), F32)],
        scratch_shapes=[pltpu.VMEM((1, n), F32)], compiler_params=_params("arbitrary"),
    )(fl, bf, dcum)


def _rope(x1, x2, cos, sin, *, negate, name):
    s, n = x1.shape
    bs = _blk(s, ROW_BLOCK)

    def body(a_ref, b_ref, c_ref, s_ref, o1_ref, o2_ref):
        a, b, cv = a_ref[...], b_ref[...], c_ref[...]
        sv = -s_ref[...] if negate else s_ref[...]
        o1_ref[...] = a * cv - b * sv
        o2_ref[...] = b * cv + a * sv

    row = pl.BlockSpec((bs, n), lambda i: (i, 0))
    return pl.pallas_call(
        body, name=name, grid=(s // bs,), in_specs=[row] * 4, out_specs=[row, row],
        out_shape=[jax.ShapeDtypeStruct((s, n), F32)] * 2, compiler_params=_params("parallel"),
    )(x1, x2, cos, sin)


def _rope_heads(x, ta, tb, tc, *, out_dtype, name):
    s, n = x.shape
    bs = _blk(s, ROW_BLOCK)

    def body(x_ref, a_ref, b_ref, c_ref, o_ref):
        av, bv, cv = a_ref[...], b_ref[...], c_ref[...]
        for g in range(n // LANES):
            xg = x_ref[:, g * LANES:(g + 1) * LANES]
            og = xg * av + pltpu.roll(xg, LANES - MLA_ROPE_HALF, 1) * bv + pltpu.roll(xg, MLA_ROPE_HALF, 1) * cv
            o_ref[:, g * LANES:(g + 1) * LANES] = og.astype(out_dtype)

    row = pl.BlockSpec((bs, n), lambda i: (i, 0))
    tab = pl.BlockSpec((bs, LANES), lambda i: (i, 0))
    return pl.pallas_call(
        body, name=name, grid=(s // bs,), in_specs=[row, tab, tab, tab], out_specs=row,
        out_shape=jax.ShapeDtypeStruct((s, n), out_dtype), compiler_params=_params("parallel"),
    )(x, ta, tb, tc)


def _group_sum(x, *, name):
    s, n = x.shape
    bs = _blk(s, ROW_BLOCK)

    def body(x_ref, o_ref):
        acc = x_ref[:, 0:LANES]
        for g in range(1, n // LANES):
            acc = acc + x_ref[:, g * LANES:(g + 1) * LANES]
        o_ref[...] = acc

    return pl.pallas_call(
        body, name=name, grid=(s // bs,), in_specs=[pl.BlockSpec((bs, n), lambda i: (i, 0))],
        out_specs=pl.BlockSpec((bs, LANES), lambda i: (i, 0)),
        out_shape=jax.ShapeDtypeStruct((s, LANES), F32), compiler_params=_params("parallel"),
    )(x)


def _shift_down(x, k):
    return pltpu.roll(x, k, 0)


def _conv_rows(ext, w_ref, b_ref, rows):
    y = b_ref[...] + w_ref[0:1, :] * _shift_down(ext, 2) + w_ref[1:2, :] * _shift_down(ext, 1) + w_ref[2:3, :] * ext
    return y[8:8 + rows]


def _conv_gate_fwd(u, cw, cb, *, name):
    s, f2 = u.shape
    f = f2 // 2
    nf = f // LANES
    r = _blk(s, CONV_ROWS)
    r8 = r // 8

    def body(ug_ref, ugp_ref, uv_ref, uvp_ref, wg_ref, wv_ref, bg_ref, bv_ref, o_ref):
        first = pl.program_id(1) == 0

        def conv(cur_ref, prev_ref, w_ref, b_ref):
            prev = jnp.where(first, 0.0, prev_ref[...])
            return _conv_rows(jnp.concatenate([prev, cur_ref[...]], axis=0), w_ref, b_ref, r)

        yg = conv(ug_ref, ugp_ref, wg_ref, bg_ref)
        yv = conv(uv_ref, uvp_ref, wv_ref, bv_ref)
        o_ref[...] = (yg * jax.nn.sigmoid(yg) * yv).astype(BF16)

    def cur(off):
        return pl.BlockSpec((r, LANES), lambda c, i: (i, c + off))

    def prev(off):
        return pl.BlockSpec((8, LANES), lambda c, i: (jnp.maximum(i * r8 - 1, 0), c + off))

    def wspec(rows, off):
        return pl.BlockSpec((rows, LANES), lambda c, i: (0, c + off))

    return pl.pallas_call(
        body, name=name, grid=(nf, s // r),
        in_specs=[cur(0), prev(0), cur(nf), prev(nf), wspec(3, 0), wspec(3, nf), wspec(1, 0), wspec(1, nf)],
        out_specs=pl.BlockSpec((r, LANES), lambda c, i: (i, c)),
        out_shape=jax.ShapeDtypeStruct((s, f), BF16), compiler_params=_params("parallel", "parallel"),
    )(u, u, u, u, cw, cw, cb, cb)


def _conv_gate_bwd(u, cw, cb, dg, *, name):
    s, f2 = u.shape
    f = f2 // 2
    nf = f // LANES
    r = _blk(s, CONV_ROWS)
    r8 = r // 8
    nr = s // r

    def body(ug_ref, ugp_ref, ugn_ref, uv_ref, uvp_ref, uvn_ref, wg_ref, wv_ref, bg_ref, bv_ref, dg_ref, dgn_ref,
             dug_ref, duv_ref, dwg_ref, dwv_ref, dbg_ref, dbv_ref):
        i = pl.program_id(1)
        first, last = i == 0, i == nr - 1

        def ext_of(cur_ref, prev_ref, next_ref):
            prev = jnp.where(first, 0.0, prev_ref[...])
            return jnp.concatenate([prev, cur_ref[...], next_ref[...]], axis=0)

        eg, ev = ext_of(ug_ref, ugp_ref, ugn_ref), ext_of(uv_ref, uvp_ref, uvn_ref)
        yg = _conv_rows(eg, wg_ref, bg_ref, r + 8)
        yv = _conv_rows(ev, wv_ref, bv_ref, r + 8)
        dgn = jnp.where(last, 0.0, dgn_ref[...])
        dgx = jnp.concatenate([dg_ref[...], dgn], axis=0)
        sg = jax.nn.sigmoid(yg)
        dyg = dgx * yv * (sg * (1.0 + yg * (1.0 - sg)))
        dyv = dgx * (yg * sg)

        @pl.when(i == 0)
        def _():
            for ref in (dwg_ref, dwv_ref, dbg_ref, dbv_ref):
                ref[...] = jnp.zeros_like(ref)

        def grads(dy, ext, w_ref, du_ref, dw_ref, db_ref):
            n = r + 8
            du = w_ref[2:3, :] * dy + w_ref[1:2, :] * pltpu.roll(dy, n - 1, 0) + w_ref[0:1, :] * pltpu.roll(dy, n - 2, 0)
            du_ref[...] = du[0:r].astype(BF16)
            dyc = dy[0:r]
            db_ref[...] += jnp.sum(dyc, axis=0, keepdims=True)
            ext_c = ext[0:r + 8]
            dw_ref[0:1, :] += jnp.sum(dyc * _shift_down(ext_c, 2)[8:], axis=0, keepdims=True)
            dw_ref[1:2, :] += jnp.sum(dyc * _shift_down(ext_c, 1)[8:], axis=0, keepdims=True)
            dw_ref[2:3, :] += jnp.sum(dyc * ext_c[8:], axis=0, keepdims=True)

        grads(dyg, eg, wg_ref, dug_ref, dwg_ref, dbg_ref)
        grads(dyv, ev, wv_ref, duv_ref, dwv_ref, dbv_ref)

    def cur(off):
        return pl.BlockSpec((r, LANES), lambda c, i: (i, c + off))

    def prev(off):
        return pl.BlockSpec((8, LANES), lambda c, i: (jnp.maximum(i * r8 - 1, 0), c + off))

    def nxt(off):
        return pl.BlockSpec((8, LANES), lambda c, i: (jnp.minimum((i + 1) * r8, s // 8 - 1), c + off))

    def wspec(rows, off):
        return pl.BlockSpec((rows, LANES), lambda c, i: (0, c + off))

    outs = pl.pallas_call(
        body, name=name, grid=(nf, nr),
        in_specs=[cur(0), prev(0), nxt(0), cur(nf), prev(nf), nxt(nf), wspec(3, 0), wspec(3, nf), wspec(1, 0), wspec(1, nf),
                  cur(0), nxt(0)],
        out_specs=[cur(0), cur(0), wspec(3, 0), wspec(3, 0), wspec(1, 0), wspec(1, 0)],
        out_shape=[jax.ShapeDtypeStruct((s, f), BF16), jax.ShapeDtypeStruct((s, f), BF16),
                   jax.ShapeDtypeStruct((3, f), F32), jax.ShapeDtypeStruct((3, f), F32),
                   jax.ShapeDtypeStruct((1, f), F32), jax.ShapeDtypeStruct((1, f), F32)],
        compiler_params=_params("parallel", "arbitrary"),
    )(u, u, u, u, u, u, cw, cw, cb, cb, dg, dg)
    dug, duv, dwg, dwv, dbg, dbv = outs
    return jnp.concatenate([dug, duv], axis=1), jnp.concatenate([dwg, dwv], axis=1), jnp.concatenate([dbg, dbv], axis=1)


def _adamw(w, g, m, v, *, slabs, name):
    shape = w.shape
    cols = shape[-1]
    rows = w.size // cols
    w2, m2, v2 = (t.reshape(rows, cols) for t in (w, m, v))
    g2 = g.reshape((N_DEV, rows, cols) if slabs else (rows, cols))
    br = _row_blk(rows, ROW_BLOCK // 2 if slabs else ROW_BLOCK)

    def body(w_ref, g_ref, m_ref, v_ref, go_ref, d_ref, nm_ref, nv_ref):
        if slabs:
            gv = g_ref[0].astype(F32)
            for p in range(1, N_DEV):
                gv = gv + g_ref[p].astype(F32)
        else:
            gv = g_ref[...]
        nm = ADAM_B1 * m_ref[...] + (1.0 - ADAM_B1) * gv
        nv = ADAM_B2 * v_ref[...] + (1.0 - ADAM_B2) * (gv * gv)
        m_hat = nm / (1.0 - ADAM_B1 ** ADAM_STEP)
        v_hat = nv / (1.0 - ADAM_B2 ** ADAM_STEP)
        go_ref[...] = gv
        d_ref[...] = -ADAM_LR * (m_hat / (jnp.sqrt(v_hat) + ADAM_EPS) + ADAM_WD * w_ref[...])
        nm_ref[...] = nm
        nv_ref[...] = nv

    spec = pl.BlockSpec((br, cols), lambda i: (i, 0))
    gspec = pl.BlockSpec((N_DEV, br, cols), lambda i: (0, i, 0)) if slabs else spec
    outs = pl.pallas_call(
        body, name=name, grid=(rows // br,), in_specs=[spec, gspec, spec, spec], out_specs=[spec] * 4,
        out_shape=[jax.ShapeDtypeStruct((rows, cols), F32)] * 4, compiler_params=_params("parallel"),
    )(w2, g2, m2, v2)
    return tuple(t.reshape(shape) for t in outs)


def _exchange(xs, *, same_src, name):
    n = len(xs)
    slabs = [x.shape if same_src else x.shape[1:] for x in xs]

    def body(*refs):
        x_refs, o_refs = refs[:n], refs[n:2 * n]
        send_sems, recv_sems, loc_sems = refs[2 * n:]
        ix, iy, ic = lax.axis_index("x"), lax.axis_index("y"), lax.axis_index("c")
        me = 4 * ix + 2 * iy + ic
        local, sends, recvs = [], [], []
        for a in range(n):
            def src(p, a=a):
                return x_refs[a] if same_src else x_refs[a].at[p]

            local.append(pltpu.make_async_copy(src(me), o_refs[a].at[me], loc_sems.at[a]))
            for k in (1, 2, 4, 3, 5, 6, 7):
                px = 1 - ix if k & 4 else ix
                py = 1 - iy if k & 2 else iy
                pc = 1 - ic if k & 1 else ic
                p = 4 * px + 2 * py + pc
                for dst, out in ((me, sends), (p, recvs)):
                    out.append(pltpu.make_async_remote_copy(
                        src_ref=src(p), dst_ref=o_refs[a].at[dst], send_sem=send_sems.at[a, k - 1],
                        recv_sem=recv_sems.at[a, k - 1], device_id=(px, py, pc), device_id_type=pl.DeviceIdType.MESH))
        for cp in local + sends:
            cp.start()
        for cp in recvs:
            cp.wait_recv()
        for cp in sends:
            cp.wait_send()
        for cp in local:
            cp.wait()

    return pl.pallas_call(
        body, name=name,
        in_specs=[pl.BlockSpec(memory_space=pl.ANY)] * n, out_specs=[pl.BlockSpec(memory_space=pl.ANY)] * n,
        out_shape=[jax.ShapeDtypeStruct((N_DEV,) + tuple(sl), x.dtype) for sl, x in zip(slabs, xs)],
        scratch_shapes=[pltpu.SemaphoreType.DMA((n, N_DEV - 1)), pltpu.SemaphoreType.DMA((n, N_DEV - 1)),
                        pltpu.SemaphoreType.DMA((n,))],
        compiler_params=pltpu.CompilerParams(has_side_effects=True, vmem_limit_bytes=VMEM_LIMIT_BYTES),
    )(*xs)


def _gather_two_level(xs, *, name):
    n = len(xs)

    def body(*refs):
        x_refs, o_refs = refs[:n], refs[n:2 * n]
        send_sems, recv_sems, loc_sems = refs[2 * n:]
        ix, iy, ic = lax.axis_index("x"), lax.axis_index("y"), lax.axis_index("c")
        me = 4 * ix + 2 * iy + ic
        sib = 4 * ix + 2 * iy + (1 - ic)
        chips = [(1 - ix if ch & 2 else ix, 1 - iy if ch & 1 else iy) for ch in (1, 2, 3)]

        def copy(a, pos, src, slot, to):
            return pltpu.make_async_remote_copy(
                src_ref=src, dst_ref=o_refs[a].at[slot], send_sem=send_sems.at[a, pos], recv_sem=recv_sems.at[a, pos],
                device_id=to, device_id_type=pl.DeviceIdType.MESH)

        local = [pltpu.make_async_copy(x_refs[a], o_refs[a].at[me], loc_sems.at[a]) for a in range(n)]
        first, passed, arrive = [], [], []
        for a in range(n):
            first.append(copy(a, 0, x_refs[a], me, (ix, iy, 1 - ic)))
            arrive.append(copy(a, 0, x_refs[a], sib, (ix, iy, 1 - ic)))
            for ch, (px, py) in enumerate(chips, start=1):
                same, other = 4 * px + 2 * py + ic, 4 * px + 2 * py + (1 - ic)
                first.append(copy(a, 2 * ch - 1, x_refs[a], me, (px, py, ic)))
                passed.append((copy(a, 2 * ch - 1, x_refs[a], same, (px, py, ic)),
                               copy(a, 2 * ch, o_refs[a].at[same], same, (ix, iy, 1 - ic))))
                arrive.append(copy(a, 2 * ch, x_refs[a], other, (ix, iy, 1 - ic)))
        for cp in local + first:
            cp.start()
        for landed, onward in passed:
            landed.wait_recv()
            onward.start()
        for cp in arrive:
            cp.wait_recv()
        for cp in first + [onward for _, onward in passed]:
            cp.wait_send()
        for cp in local:
            cp.wait()

    return pl.pallas_call(
        body, name=name,
        in_specs=[pl.BlockSpec(memory_space=pl.ANY)] * n, out_specs=[pl.BlockSpec(memory_space=pl.ANY)] * n,
        out_shape=[jax.ShapeDtypeStruct((N_DEV,) + tuple(x.shape), x.dtype) for x in xs],
        scratch_shapes=[pltpu.SemaphoreType.DMA((n, N_DEV - 1)), pltpu.SemaphoreType.DMA((n, N_DEV - 1)),
                        pltpu.SemaphoreType.DMA((n,))],
        compiler_params=pltpu.CompilerParams(has_side_effects=True, vmem_limit_bytes=VMEM_LIMIT_BYTES),
    )(*xs)


def _sum_slabs(x, *, name):
    n, r, c = x.shape
    br = _row_blk(r, ROW_BLOCK)

    def body(x_ref, o_ref):
        acc = x_ref[0]
        for p in range(1, n):
            acc = acc + x_ref[p]
        o_ref[...] = acc

    return pl.pallas_call(
        body, name=name, grid=(r // br,), in_specs=[pl.BlockSpec((n, br, c), lambda i: (0, i, 0))],
        out_specs=pl.BlockSpec((br, c), lambda i: (i, 0)),
        out_shape=jax.ShapeDtypeStruct((r, c), F32), compiler_params=_params("parallel"),
    )(x)


def _silu(x, *, name):
    def body(x_ref, o_ref):
        xv = x_ref[...]
        o_ref[...] = (xv * jax.nn.sigmoid(xv)).astype(BF16)

    return pl.pallas_call(body, name=name, out_shape=jax.ShapeDtypeStruct(x.shape, BF16),
                          compiler_params=_params())(x)


_BIG = {"fox_w_in": 2, "fox_w_o": 1, "mla_w_a": 1, "mla_w_uq": 2, "mla_w_ukv": 2, "mla_w_o": 1, "ffn_w_in": 2, "ffn_w_out": 1}
_SMALL = {"mla_g_q": 1, "mla_g_kv": 1, "ffn_conv_w": 2}
_REPL = ("fox_b_f", "ffn_conv_b", "final_g")


def _gathered_to_full(g, axis):
    full = jnp.moveaxis(g, 0, axis)
    shape = list(full.shape)
    shape[axis:axis + 2] = [shape[axis] * shape[axis + 1]]
    return full.reshape(shape)


def _full_to_chunks(full, axis):
    shape = list(full.shape)
    shape[axis:axis + 1] = [N_DEV, shape[axis] // N_DEV]
    return jnp.moveaxis(full.reshape(shape), axis, 0)


def _per_head(parts, s_or_rows):
    return jnp.concatenate([p.reshape(s_or_rows, N_HEADS, -1) for p in parts], axis=-1).reshape(s_or_rows, -1)


def kernel(x, c, ada_w, ada_b, fox_w_in, fox_b_f, fox_w_o, mla_w_a, mla_g_q, mla_g_kv, mla_w_uq, mla_w_ukv, mla_w_o, ffn_w_in, ffn_conv_w, ffn_conv_b, ffn_w_out, final_g, loss_target, m_ada_w, m_ada_b, m_fox_w_in, m_fox_b_f, m_fox_w_o, m_mla_w_a, m_mla_g_q, m_mla_g_kv, m_mla_w_uq, m_mla_w_ukv, m_mla_w_o, m_ffn_w_in, m_ffn_conv_w, m_ffn_conv_b, m_ffn_w_out, m_final_g, v_ada_w, v_ada_b, v_fox_w_in, v_fox_b_f, v_fox_w_o, v_mla_w_a, v_mla_g_q, v_mla_g_kv, v_mla_w_uq, v_mla_w_ukv, v_mla_w_o, v_ffn_w_in, v_ffn_conv_w, v_ffn_conv_b, v_ffn_w_out, v_final_g):
    weights = dict(ada_w=ada_w, ada_b=ada_b, fox_w_in=fox_w_in, fox_b_f=fox_b_f, fox_w_o=fox_w_o, mla_w_a=mla_w_a,
                   mla_g_q=mla_g_q, mla_g_kv=mla_g_kv, mla_w_uq=mla_w_uq, mla_w_ukv=mla_w_ukv, mla_w_o=mla_w_o,
                   ffn_w_in=ffn_w_in, ffn_conv_w=ffn_conv_w, ffn_conv_b=ffn_conv_b, ffn_w_out=ffn_w_out, final_g=final_g)
    mom_m = dict(ada_w=m_ada_w, ada_b=m_ada_b, fox_w_in=m_fox_w_in, fox_b_f=m_fox_b_f, fox_w_o=m_fox_w_o, mla_w_a=m_mla_w_a,
                 mla_g_q=m_mla_g_q, mla_g_kv=m_mla_g_kv, mla_w_uq=m_mla_w_uq, mla_w_ukv=m_mla_w_ukv, mla_w_o=m_mla_w_o,
                 ffn_w_in=m_ffn_w_in, ffn_conv_w=m_ffn_conv_w, ffn_conv_b=m_ffn_conv_b, ffn_w_out=m_ffn_w_out, final_g=m_final_g)
    mom_v = dict(ada_w=v_ada_w, ada_b=v_ada_b, fox_w_in=v_fox_w_in, fox_b_f=v_fox_b_f, fox_w_o=v_fox_w_o, mla_w_a=v_mla_w_a,
                 mla_g_q=v_mla_g_q, mla_g_kv=v_mla_g_kv, mla_w_uq=v_mla_w_uq, mla_w_ukv=v_mla_w_ukv, mla_w_o=v_mla_w_o,
                 ffn_w_in=v_ffn_w_in, ffn_conv_w=v_ffn_conv_w, ffn_conv_b=v_ffn_conv_b, ffn_w_out=v_ffn_w_out, final_g=v_final_g)
    order = list(weights)
    x0 = x[0]
    target = loss_target[0]
    s = x0.shape[0]
    d = D_MODEL
    cols = ada_w.shape[-1]
    nq = N_HEADS * HEAD_DIM

    small_names = ["c"] + list(_SMALL)
    small_all = dict(zip(small_names, _exchange([c] + [weights[n] for n in _SMALL], same_src=True, name="gather_small")))
    c_all = small_all["c"].reshape(N_DEV, d)
    g_q = _gathered_to_full(small_all["mla_g_q"], 1)
    g_kv = _gathered_to_full(small_all["mla_g_kv"], 1)
    conv_w = _gathered_to_full(small_all["ffn_conv_w"], 2)

    c_pad = _pad_axis(c_all, 0, LANES)
    silu_c = _silu(c_pad, name="silu_c")
    w_ada = ada_w.reshape(4, d, cols)
    b_ada = ada_b.reshape(4, 1, cols)
    mods = [_matmul(silu_c, w_ada[i], name=f"ada_mod{i}")[:N_DEV] + b_ada[i] for i in range(4)]
    mod_send = _pad_axis(jnp.stack(mods, axis=1), 1, 8)
    mod_recv, = _exchange([mod_send], same_src=False, name="scatter_mod")
    mod = mod_recv[:, :4].transpose(1, 0, 2).reshape(4, 3 * d)
    shift = [mod[i:i + 1, 0:d] for i in range(4)]
    scale = [mod[i:i + 1, d:2 * d] for i in range(4)]
    gate = [mod[i:i + 1, 2 * d:3 * d] for i in range(4)]

    big_all = _gather_two_level([weights[n].astype(BF16) for n in _BIG], name="gather_weights")
    wfull = {n: _gathered_to_full(g, _BIG[n]) for n, g in zip(_BIG, big_all)}

    w_fox_in = _pad_axis(wfull["fox_w_in"][0], 1, LANES)
    w_fox_qkv, w_fox_f = w_fox_in[:, :3 * nq], w_fox_in[:, 3 * nq:]
    w_fox_o = wfull["fox_w_o"][0]
    w_a = _pad_axis(wfull["mla_w_a"][0], 1, LANES)
    wq = wfull["mla_w_uq"][0].reshape(MLA_Q_RANK, N_HEADS, HEAD_DIM + MLA_ROPE_DIM)
    w_uq = _pad_axis(wq, 2, LANES).reshape(MLA_Q_RANK, N_HEADS * LANES)
    wkv = wfull["mla_w_ukv"][0].reshape(MLA_KV_RANK, N_HEADS, 2 * HEAD_DIM)
    w_ukv = jnp.concatenate([wkv[:, :, :HEAD_DIM].reshape(MLA_KV_RANK, -1), wkv[:, :, HEAD_DIM:].reshape(MLA_KV_RANK, -1)], axis=1)
    w_mla_o = wfull["mla_w_o"][0]
    w_ffn_in = wfull["ffn_w_in"]
    w_ffn_out = wfull["ffn_w_out"]
    conv_b = ffn_conv_b

    fox_scale = HEAD_DIM ** -0.5
    mla_scale = (HEAD_DIM + MLA_ROPE_DIM) ** -0.5
    pos = jnp.arange(s, dtype=F32)
    inv_freq = ROPE_BASE ** (-jnp.arange(0, MLA_ROPE_DIM, 2, dtype=F32) / MLA_ROPE_DIM)
    ang = pos[:, None] * inv_freq[None, :]
    cos16, sin16 = jnp.cos(ang), jnp.sin(ang)
    z16, z32, z64 = jnp.zeros((s, 16), F32), jnp.zeros((s, 32), F32), jnp.zeros((s, 64), F32)
    tab_a = jnp.concatenate([jnp.ones((s, 64), F32), cos16, cos16, z32], axis=1) * (mla_scale * LOG2E)
    tab_b = jnp.concatenate([z64, -sin16, z16, z32], axis=1) * (mla_scale * LOG2E)
    tab_c = jnp.concatenate([z64, z16, sin16, z32], axis=1) * (mla_scale * LOG2E)

    h0 = _norm_fwd(x0, scale[0], shift[0], plus_one=True, out_dtype=BF16, name="ada_fwd0")
    q_mult = jnp.concatenate([jnp.full((1, nq), fox_scale * LOG2E, F32), jnp.ones((1, 2 * nq), F32)], axis=1)
    qkv = _matmul(h0, w_fox_qkv, out_dtype=BF16, col_scale=q_mult, name="fox_proj")
    fl = _matmul(h0, w_fox_f, name="fox_proj_f")[:, :N_HEADS]
    cum = _fox_gate_fwd(fl, fox_b_f, name="fox_gate_fwd")
    fox_cfg = dict(qc=0, kc=N_HEADS // 2, vc=N_HEADS, width=HEAD_DIM)
    kb2 = cum * -LOG2E
    att_blk = _blk(s, ATT_BLOCK)
    fvt = _matmul(w_fox_qkv[:, 2 * nq:], h0, ta=True, tb=True, out_dtype=BF16, name="fox_proj_vt")
    fo, fox_lse = _attn_fwd(qkv, qkv, _key_blocks(fvt, att_blk), _bias_lane_terms(kb2), qc=0, kc=N_HEADS // 2,
                            width=HEAD_DIM, name="fox_attn_fwd")
    fox_res = (qkv, qkv, qkv, kb2, fo, fox_lse)
    y0 = _matmul(fo, w_fox_o, name="fox_out")
    x1 = _resid_fwd(x0, y0, gate[0], name="resid_fwd0")

    def ffn_fwd(xin, li, sub):
        hh = _norm_fwd(xin, scale[sub], shift[sub], plus_one=True, out_dtype=BF16, name=f"ada_fwd{sub}")
        u = _matmul(hh, w_ffn_in[li], name=f"ffn_up{li}")
        g = _conv_gate_fwd(u, conv_w[li], conv_b[li:li + 1], name=f"conv_fwd{li}")
        y = _matmul(g, w_ffn_out[li], name=f"ffn_down{li}")
        return _resid_fwd(xin, y, gate[sub], name=f"resid_fwd{sub}"), (hh, u, g, y)

    x2, ffn0_res = ffn_fwd(x1, 0, 1)

    h2 = _norm_fwd(x2, scale[2], shift[2], plus_one=True, out_dtype=BF16, name="ada_fwd2")
    a = _matmul(h2, w_a, name="mla_a")
    a_q, a_kv = a[:, :MLA_Q_RANK], a[:, MLA_Q_RANK:MLA_Q_RANK + MLA_KV_RANK]
    kr1 = a[:, MLA_Q_RANK + MLA_KV_RANK:MLA_Q_RANK + MLA_KV_RANK + MLA_ROPE_HALF]
    kr2 = a[:, MLA_Q_RANK + MLA_KV_RANK + MLA_ROPE_HALF:MLA_Q_RANK + MLA_KV_RANK + MLA_ROPE_DIM]
    cq = _norm_fwd(a_q, g_q, jnp.zeros_like(g_q), plus_one=False, out_dtype=BF16, name="mla_norm_q")
    ckv = _norm_fwd(a_kv, g_kv, jnp.zeros_like(g_kv), plus_one=False, out_dtype=BF16, name="mla_norm_kv")
    qf = _matmul(cq, w_uq, name="mla_uq")
    kvf = _matmul(ckv, w_ukv, out_dtype=BF16, name="mla_ukv")
    mq = _rope_heads(qf, tab_a, tab_b, tab_c, out_dtype=BF16, name="rope_q")
    kk1, kk2 = _rope(kr1, kr2, cos16, sin16, negate=False, name="rope_k")
    k_tail = jnp.concatenate([kk1, kk2, z32], axis=1).astype(BF16)
    mk = jnp.concatenate([kvf[:, :nq].reshape(s, N_HEADS, HEAD_DIM),
                          jnp.broadcast_to(k_tail[:, None, :], (s, N_HEADS, HEAD_DIM))], axis=-1).reshape(s, N_HEADS * LANES)
    mla_cfg = dict(qc=0, kc=0, vc=N_HEADS // 2, width=LANES)
    mvt = _matmul(w_ukv[:, nq:], ckv, ta=True, tb=True, out_dtype=BF16, name="mla_ukv_vt")
    mo, mla_lse = _attn_fwd(mq, mk, _key_blocks(mvt, att_blk), None, qc=0, kc=0, width=LANES, name="mla_attn_fwd")
    mla_res = (mq, mk, kvf, None, mo, mla_lse)
    y2 = _matmul(mo, w_mla_o, name="mla_out")
    x3 = _resid_fwd(x2, y2, gate[2], name="resid_fwd2")

    x4, ffn1_res = ffn_fwd(x3, 1, 3)

    loss_vec, dx4, d_final_g = _final_loss(x4, final_g.reshape(1, d), target, name="final_loss")
    loss = lax.psum(loss_vec[0, 0], ("x", "y", "c"))

    grads = {}
    dmod = [None] * 4

    def ffn_bwd(dx_out, xin, li, sub, res):
        hh, u, g, y = res
        dy, dgate = _resid_bwd(dx_out, y, gate[sub], name=f"resid_bwd{sub}")
        gw_out = _matmul(g, dy, ta=True, out_dtype=BF16, name=f"ffn_down_dw{li}")
        dg = _matmul(dy, w_ffn_out[li], tb=True, name=f"ffn_down_dx{li}")
        du, dcw, dcb = _conv_gate_bwd(u, conv_w[li], conv_b[li:li + 1], dg, name=f"conv_bwd{li}")
        gw_in = _matmul(hh, du, ta=True, out_dtype=BF16, name=f"ffn_up_dw{li}")
        dh = _matmul(du, w_ffn_in[li], tb=True, out_dtype=BF16, name=f"ffn_up_dx{li}")
        dx_in, dscale, dshift = _norm_bwd(xin, scale[sub], dh, dx_out, plus_one=True, name=f"ada_bwd{sub}")
        dmod[sub] = jnp.concatenate([dshift, dscale, dgate], axis=1)
        return dx_in, gw_in, dcw, dcb, gw_out

    dx3, gw_in1, dcw1, dcb1, gw_out1 = ffn_bwd(dx4, x3, 1, 3, ffn1_res)

    dy2, dgate2 = _resid_bwd(dx3, y2, gate[2], name="resid_bwd2")
    grads["mla_w_o"] = _matmul(mo, dy2, ta=True, out_dtype=BF16, name="mla_out_dw")[None]
    dmo = _matmul(dy2, w_mla_o, tb=True, out_dtype=BF16, name="mla_out_dx")
    dmq, dmk, dmv, _ = _attention_bwd(mla_res, dmo, dq_mult=1.0 / LOG2E, dk_mult=1.0 / LOG2E, out_dtype=F32,
                                      name="mla_attn", **mla_cfg)
    dqf = _rope_heads(dmq, tab_a, -tab_b, -tab_c, out_dtype=BF16, name="rope_q_bwd")
    g_uq = _matmul(cq, dqf, ta=True, out_dtype=BF16, name="mla_uq_dw")
    dcq = _matmul(dqf, w_uq, tb=True, name="mla_uq_dx")
    dmk3 = dmk.reshape(s, N_HEADS, LANES)
    dkr = _group_sum(dmk, name="mla_krope_sum")
    dkr1, dkr2 = _rope(dkr[:, HEAD_DIM:HEAD_DIM + MLA_ROPE_HALF], dkr[:, HEAD_DIM + MLA_ROPE_HALF:HEAD_DIM + MLA_ROPE_DIM],
                       cos16, sin16, negate=True, name="rope_k_bwd")
    dkvf = jnp.concatenate([dmk3[:, :, :HEAD_DIM].reshape(s, nq).astype(BF16), dmv.astype(BF16)], axis=1)
    g_ukv = _matmul(ckv, dkvf, ta=True, out_dtype=BF16, name="mla_ukv_dw")
    dckv = _matmul(dkvf, w_ukv, tb=True, name="mla_ukv_dx")
    da_q, dg_q, _ = _norm_bwd(a_q, g_q, dcq, None, plus_one=False, name="mla_norm_q_bwd")
    da_kv, dg_kv, _ = _norm_bwd(a_kv, g_kv, dckv, None, plus_one=False, name="mla_norm_kv_bwd")
    da = jnp.concatenate([da_q, da_kv, dkr1, dkr2, jnp.zeros((s, w_a.shape[1] - 672), F32)], axis=1).astype(BF16)
    grads["mla_w_a"] = _matmul(h2, da, ta=True, out_dtype=BF16, name="mla_a_dw")[None, :, :672]
    dh2 = _matmul(da, w_a, tb=True, out_dtype=BF16, name="mla_a_dx")
    dx2, dscale2, dshift2 = _norm_bwd(x2, scale[2], dh2, dx3, plus_one=True, name="ada_bwd2")
    dmod[2] = jnp.concatenate([dshift2, dscale2, dgate2], axis=1)
    grads["mla_w_uq"] = g_uq.reshape(MLA_Q_RANK, N_HEADS, LANES)[:, :, :HEAD_DIM + MLA_ROPE_DIM].reshape(1, MLA_Q_RANK, -1)
    grads["mla_w_ukv"] = _per_head([g_ukv[:, :nq], g_ukv[:, nq:]], MLA_KV_RANK)[None]
    grads["mla_g_q"], grads["mla_g_kv"] = dg_q, dg_kv

    dx1, gw_in0, dcw0, dcb0, gw_out0 = ffn_bwd(dx2, x1, 0, 1, ffn0_res)
    grads["ffn_w_in"] = jnp.stack([gw_in0, gw_in1])
    grads["ffn_w_out"] = jnp.stack([gw_out0, gw_out1])
    grads["ffn_conv_w"] = jnp.stack([dcw0, dcw1])
    g_conv_b = jnp.concatenate([dcb0, dcb1], axis=0)

    dy0, dgate0 = _resid_bwd(dx1, y0, gate[0], name="resid_bwd0")
    grads["fox_w_o"] = _matmul(fo, dy0, ta=True, out_dtype=BF16, name="fox_out_dw")[None]
    dfo = _matmul(dy0, w_fox_o, tb=True, out_dtype=BF16, name="fox_out_dx")
    dfq, dfk, dfv, dcum = _attention_bwd(fox_res, dfo, dq_mult=fox_scale, dk_mult=1.0 / LOG2E, out_dtype=BF16,
                                         name="fox_attn", **fox_cfg)
    dfl, g_b_f = _fox_gate_bwd(fl, fox_b_f, dcum, name="fox_gate_bwd")
    dproj = jnp.concatenate([dfq.astype(BF16), dfk, dfv, _pad_axis(dfl, 1, LANES).astype(BF16)], axis=1)
    grads["fox_w_in"] = _matmul(h0, dproj, ta=True, out_dtype=BF16, name="fox_proj_dw")[None, :, :3 * nq + N_HEADS]
    dh0 = _matmul(dproj, w_fox_in, tb=True, out_dtype=BF16, name="fox_proj_dx")
    dx0, dscale0, dshift0 = _norm_bwd(x0, scale[0], dh0, dx1, plus_one=True, name="ada_bwd0")
    dmod[0] = jnp.concatenate([dshift0, dscale0, dgate0], axis=1)

    dmod_send = _pad_axis(jnp.stack(dmod, axis=0).reshape(4, N_DEV, cols).transpose(1, 0, 2), 1, 8)
    dmod_recv, = _exchange([dmod_send], same_src=False, name="scatter_dmod")
    dmod_all = dmod_recv[:, :4]
    dmod_pad = _pad_axis(dmod_all, 0, LANES)
    g_ada_w = jnp.stack([_matmul(silu_c, dmod_pad[:, i], ta=True, name=f"ada_dw{i}") for i in range(4)])
    grads["ada_w"] = g_ada_w.reshape(ada_w.shape)
    grads["ada_b"] = _sum_slabs(dmod_recv, name="ada_db")[:4].reshape(ada_b.shape)

    sharded = list(_BIG) + list(_SMALL)
    axes = {**_BIG, **_SMALL}
    recv = _exchange([_full_to_chunks(grads[n], axes[n]) for n in sharded], same_src=False, name="scatter_grads")
    grads.update(dict(zip(sharded, recv)))
    repl = _exchange([g_b_f, g_conv_b, d_final_g], same_src=True, name="gather_repl_grads")
    grads.update(dict(zip(_REPL, repl)))

    grad_out, deltas, new_m, new_v = {}, {}, {}, {}
    for n in order:
        grad_out[n], deltas[n], new_m[n], new_v[n] = _adamw(
            weights[n], grads[n], mom_m[n], mom_v[n], slabs=n in axes or n in _REPL, name=f"adamw_{n}")

    grad_x = dx0[None]
    return (loss, grad_x, *[grad_out[n] for n in order], *[deltas[n] for n in order],
            *[new_m[n] for n in order], *[new_v[n] for n in order])
```

```python
import jax
import jax.numpy as jnp
from jax import lax
from jax.experimental import pallas as pl
from jax.experimental.pallas import tpu as pltpu

F32 = jnp.float32
BF16 = jnp.bfloat16
HIGHEST = lax.Precision.HIGHEST

N_DEV = 8
D_MODEL = 1024
N_HEADS = 16
HEAD_DIM = 64
MLA_ROPE_HALF = 16
MLA_Q_RANK = 384
MLA_KV_RANK = 256
MLA_ROPE_DIM = 32
NORM_EPS = 1e-6
ROPE_BASE = 10000.0
ADAM_LR = 0.001
ADAM_B1 = 0.9
ADAM_B2 = 0.999
ADAM_EPS = 1e-08
ADAM_WD = 0.01
ADAM_STEP = 10

LANES = 128
VMEM_LIMIT_BYTES = 56 * 1024 * 1024
ROW_BLOCK = 512
ATT_BLOCK = 512
CONV_ROWS = 1024
MM_BM, MM_BN, MM_BK = 512, 1408, 2048
MM_K_WHOLE = 3328
LOG2E = 1.4426950408889634


def _params(*sem):
    return pltpu.CompilerParams(dimension_semantics=sem or None, vmem_limit_bytes=VMEM_LIMIT_BYTES)


def _blk(dim, pref):
    if dim <= pref:
        return dim
    b = pref - pref % LANES
    while b >= LANES:
        if dim % b == 0:
            return b
        b -= LANES
    raise ValueError(f"no block for {dim}")


def _row_blk(rows, pref):
    if rows <= pref:
        return rows
    for b in range(pref - pref % 8, 7, -8):
        if rows % b == 0:
            return b
    return rows


def _pad_axis(a, axis, mult):
    pad = (-a.shape[axis]) % mult
    if pad == 0:
        return a
    widths = [(0, 0)] * a.ndim
    widths[axis] = (0, pad)
    return jnp.pad(a, widths)


def _matmul(a, b, *, ta=False, tb=False, out_dtype=F32, col_scale=None, name):
    m, k = (a.shape[1], a.shape[0]) if ta else a.shape
    n = b.shape[0] if tb else b.shape[1]
    assert (b.shape[1] if tb else b.shape[0]) == k, (a.shape, b.shape, ta, tb)
    bm, bn = _blk(m, MM_BM if ta else 2 * MM_BM), _blk(n, MM_BN)
    if ta and bm < MM_BM and m % MM_BN == 0:
        bm = MM_BN
    bk = k if k <= MM_K_WHOLE else _blk(k, MM_BK)
    nk = k // bk
    dims = (((0 if ta else 1,), (1 if tb else 0,)), ((), ()))
    has_scale = col_scale is not None
    use_acc = nk > 1 and (out_dtype != F32 or has_scale)

    def body(*refs):
        a_ref, b_ref = refs[0], refs[1]
        s_ref = refs[2] if has_scale else None
        o_ref = refs[3] if has_scale else refs[2]
        acc_ref = refs[-1] if use_acc else o_ref
        kk = pl.program_id(2)
        part = lax.dot_general(a_ref[...].astype(BF16), b_ref[...].astype(BF16), dims, preferred_element_type=F32)

        def finish(val):
            if has_scale:
                val = val * s_ref[...]
            o_ref[...] = val.astype(out_dtype)

        if nk == 1:
            finish(part)
            return

        @pl.when(kk == 0)
        def _():
            acc_ref[...] = part

        @pl.when(kk > 0)
        def _():
            acc_ref[...] += part

        if use_acc:
            @pl.when(kk == nk - 1)
            def _():
                finish(acc_ref[...])

    a_spec = pl.BlockSpec((bk, bm), lambda i, j, kk: (kk, i)) if ta else pl.BlockSpec((bm, bk), lambda i, j, kk: (i, kk))
    b_spec = pl.BlockSpec((bn, bk), lambda i, j, kk: (j, kk)) if tb else pl.BlockSpec((bk, bn), lambda i, j, kk: (kk, j))
    return pl.pallas_call(
        body, name=name, grid=(m // bm, n // bn, nk),
        in_specs=[a_spec, b_spec] + ([pl.BlockSpec((1, bn), lambda i, j, kk: (0, j))] if has_scale else []),
        out_specs=pl.BlockSpec((bm, bn), lambda i, j, kk: (i, j)),
        out_shape=jax.ShapeDtypeStruct((m, n), out_dtype),
        scratch_shapes=[pltpu.VMEM((bm, bn), F32)] if use_acc else [],
        compiler_params=_params("parallel", "parallel", "arbitrary"),
    )(*([a, b] + ([col_scale] if has_scale else [])))


def _norm_fwd(x, mul, add, *, plus_one, out_dtype, name):
    s, n = x.shape
    bs = _blk(s, ROW_BLOCK)

    def body(x_ref, m_ref, a_ref, o_ref):
        xv = x_ref[...]
        r = lax.rsqrt(jnp.mean(xv * xv, axis=-1, keepdims=True) + NORM_EPS)
        mv = m_ref[...] + 1.0 if plus_one else m_ref[...]
        o_ref[...] = (xv * r * mv + a_ref[...]).astype(out_dtype)

    row = pl.BlockSpec((bs, n), lambda i: (i, 0))
    vec = pl.BlockSpec((1, n), lambda i: (0, 0))
    return pl.pallas_call(
        body, name=name, grid=(s // bs,), in_specs=[row, vec, vec], out_specs=row,
        out_shape=jax.ShapeDtypeStruct((s, n), out_dtype), compiler_params=_params("parallel"),
    )(x, mul, add)


def _norm_bwd(x, mul, dy, dres, *, plus_one, name):
    s, n = x.shape
    bs = _blk(s, ROW_BLOCK)
    has_res = dres is not None

    def body(*refs):
        if has_res:
            x_ref, m_ref, dy_ref, dres_ref, dx_ref, dm_ref, da_ref = refs
        else:
            x_ref, m_ref, dy_ref, dx_ref, dm_ref, da_ref = refs
        xv = x_ref[...]
        dyv = dy_ref[...].astype(F32)
        r = lax.rsqrt(jnp.mean(xv * xv, axis=-1, keepdims=True) + NORM_EPS)
        xn = xv * r
        mv = m_ref[...] + 1.0 if plus_one else m_ref[...]
        g = dyv * mv
        dx = r * (g - xn * jnp.mean(g * xn, axis=-1, keepdims=True))
        if has_res:
            dx = dx + dres_ref[...]
        dx_ref[...] = dx

        @pl.when(pl.program_id(0) == 0)
        def _():
            dm_ref[...] = jnp.zeros_like(dm_ref)
            da_ref[...] = jnp.zeros_like(da_ref)

        dm_ref[...] += jnp.sum(dyv * xn, axis=0, keepdims=True)
        da_ref[...] += jnp.sum(dyv, axis=0, keepdims=True)

    row = pl.BlockSpec((bs, n), lambda i: (i, 0))
    vec = pl.BlockSpec((1, n), lambda i: (0, 0))
    ins = [x, mul, dy] + ([dres] if has_res else [])
    return pl.pallas_call(
        body, name=name, grid=(s // bs,),
        in_specs=[row, vec, row] + ([row] if has_res else []), out_specs=[row, vec, vec],
        out_shape=[jax.ShapeDtypeStruct((s, n), F32), jax.ShapeDtypeStruct((1, n), F32), jax.ShapeDtypeStruct((1, n), F32)],
        compiler_params=_params("arbitrary"),
    )(*ins)


def _resid_fwd(x, y, gate, *, name):
    s, n = x.shape
    bs = _blk(s, ROW_BLOCK)

    def body(x_ref, y_ref, g_ref, o_ref):
        o_ref[...] = x_ref[...] + g_ref[...] * y_ref[...]

    row = pl.BlockSpec((bs, n), lambda i: (i, 0))
    vec = pl.BlockSpec((1, n), lambda i: (0, 0))
    return pl.pallas_call(
        body, name=name, grid=(s // bs,), in_specs=[row, row, vec], out_specs=row,
        out_shape=jax.ShapeDtypeStruct((s, n), F32), compiler_params=_params("parallel"),
    )(x, y, gate)


def _resid_bwd(dx, y, gate, *, name):
    s, n = dx.shape
    bs = _blk(s, ROW_BLOCK)

    def body(dx_ref, y_ref, g_ref, dy_ref, dg_ref):
        dxv = dx_ref[...]
        dy_ref[...] = (g_ref[...] * dxv).astype(BF16)

        @pl.when(pl.program_id(0) == 0)
        def _():
            dg_ref[...] = jnp.zeros_like(dg_ref)

        dg_ref[...] += jnp.sum(dxv * y_ref[...], axis=0, keepdims=True)

    row = pl.BlockSpec((bs, n), lambda i: (i, 0))
    vec = pl.BlockSpec((1, n), lambda i: (0, 0))
    return pl.pallas_call(
        body, name=name, grid=(s // bs,), in_specs=[row, row, vec], out_specs=[row, vec],
        out_shape=[jax.ShapeDtypeStruct((s, n), BF16), jax.ShapeDtypeStruct((1, n), F32)],
        compiler_params=_params("arbitrary"),
    )(dx, y, gate)


def _final_loss(x, g, target, *, name):
    s, n = x.shape
    bs = _blk(s, ROW_BLOCK)

    def body(x_ref, g_ref, t_ref, loss_ref, dx_ref, dg_ref):
        xv = x_ref[...]
        r = lax.rsqrt(jnp.mean(xv * xv, axis=-1, keepdims=True) + NORM_EPS)
        xn = xv * r
        gv = g_ref[...]
        err = xn * gv - t_ref[...]
        dout = err * (1.0 / n)
        gg = dout * gv
        dx_ref[...] = r * (gg - xn * jnp.mean(gg * xn, axis=-1, keepdims=True))

        @pl.when(pl.program_id(0) == 0)
        def _():
            loss_ref[...] = jnp.zeros_like(loss_ref)
            dg_ref[...] = jnp.zeros_like(dg_ref)

        part = jnp.sum(jnp.sum(err * err, axis=-1, keepdims=True), axis=0, keepdims=True) * (0.5 / n)
        loss_ref[...] += jnp.broadcast_to(part, loss_ref.shape)
        dg_ref[...] += jnp.sum(dout * xn, axis=0, keepdims=True)

    row = pl.BlockSpec((bs, n), lambda i: (i, 0))
    vec = pl.BlockSpec((1, n), lambda i: (0, 0))
    return pl.pallas_call(
        body, name=name, grid=(s // bs,), in_specs=[row, vec, row],
        out_specs=[pl.BlockSpec((1, LANES), lambda i: (0, 0)), row, vec],
        out_shape=[jax.ShapeDtypeStruct((1, LANES), F32), jax.ShapeDtypeStruct((s, n), F32), jax.ShapeDtypeStruct((1, n), F32)],
        compiler_params=_params("arbitrary"),
    )(x, g, target)


def _lane_lt64(shape):
    return lax.broadcasted_iota(jnp.int32, shape, 1) < HEAD_DIM


def _keep_low(x):
    return jnp.where(_lane_lt64(x.shape), x.astype(F32), 0.0).astype(x.dtype)


def _keep_high(x):
    return jnp.where(_lane_lt64(x.shape), 0.0, x.astype(F32)).astype(x.dtype)


def _lane_merge(a, b):
    n = max(a.shape[0], b.shape[0])
    return jnp.where(_lane_lt64((n, LANES)), a, b)


def _pair(x, width, masked):
    if width == HEAD_DIM:
        return (_keep_low(x), _keep_high(x)) if masked else (x, x)
    return x[:, :LANES], x[:, LANES:]


def _qk_t(a, b):
    return lax.dot_general(a, b, (((1,), (1,)), ((), ())), preferred_element_type=F32)


def _attn_specs(s, blk, width, cols, resident):
    w = 2 * width
    if resident:
        return pl.BlockSpec((s, w), lambda p, i: (0, cols + p))
    return pl.BlockSpec((blk, w), lambda p, i: (i, cols + p))


BIAS_TERMS = 3


def _key_blocks(vt, blk):
    return vt.reshape(vt.shape[0], vt.shape[1] // blk, blk).transpose(1, 0, 2)


def _attn_fwd(q, k, vt, kbl, *, qc, kc, width, name):
    s = q.shape[0]
    blk = _blk(s, ATT_BLOCK)
    nb = s // blk
    has_bias = kbl is not None
    assert has_bias == (width == HEAD_DIM)

    def body(*refs):
        if has_bias:
            q_ref, k_ref, vt_ref, kbl_ref, o_ref, lse_ref = refs
        else:
            q_ref, k_ref, vt_ref, o_ref, lse_ref = refs
        i = pl.program_id(1)
        q2 = q_ref[...]
        if has_bias:
            lane = lax.broadcasted_iota(jnp.int32, (blk, LANES), 1)
            qf = q2.astype(F32)
            qh = (jnp.where(lane < HEAD_DIM, qf, jnp.where(lane < HEAD_DIM + BIAS_TERMS, 1.0, 0.0)).astype(BF16),
                  jnp.where(lane >= HEAD_DIM, qf, jnp.where(lane < BIAS_TERMS, 1.0, 0.0)).astype(BF16))
        else:
            qh = (q2[:, :LANES], q2[:, LANES:])

        def step(j, carry, nblk, diag):
            rows = pl.ds(pl.multiple_of(j * blk, blk), nblk * blk)
            vt1 = [jnp.concatenate([vt_ref[j + b], jnp.ones((16, blk), BF16)], axis=0) for b in range(nblk)]
            k2 = k_ref[rows, :]
            if has_bias:
                low = _lane_lt64(k2.shape)
                kf, bf = k2.astype(F32), kbl_ref[rows, :].astype(F32)
                kh = (jnp.where(low, kf, bf).astype(BF16), jnp.where(low, bf, kf).astype(BF16))
            else:
                kh = (k2[:, :LANES], k2[:, LANES:])
            out = []
            for hd in range(2):
                m, acc = carry[hd]
                st = _qk_t(kh[hd], qh[hd])
                if diag:
                    row = lax.broadcasted_iota(jnp.int32, (blk, blk), 0)
                    colq = lax.broadcasted_iota(jnp.int32, (blk, blk), 1)
                    st = jnp.where(row <= colq, st, -1e30)
                m_new = jnp.maximum(m, jnp.max(st, axis=0, keepdims=True))
                alpha = jnp.exp2(m - m_new)
                pt = jnp.exp2(st - m_new).astype(BF16)
                acc = alpha * acc
                for b in range(nblk):
                    acc = acc + jnp.dot(vt1[b], pt[b * blk:(b + 1) * blk], preferred_element_type=F32)
                out.append((m_new, acc))
            return tuple(out)

        one = (jnp.full((1, blk), -1e30, F32), jnp.zeros((LANES + 16, blk), F32))
        carry = lax.fori_loop(0, i // 8, lambda j, c: step(8 * j, c, 8, False), (one, one))
        carry = lax.fori_loop(0, (i % 8) // 4, lambda _, c: step(i - i % 8, c, 4, False), carry)
        carry = lax.fori_loop(0, (i % 4) // 2, lambda _, c: step(i - i % 4, c, 2, False), carry)
        carry = lax.fori_loop(0, i % 2, lambda _, c: step(i - 1, c, 1, False), carry)
        (ma, acca), (mb, accb) = step(i, carry, 1, True)
        la = jnp.max(acca[LANES:LANES + 8], axis=0, keepdims=True)
        lb = jnp.max(accb[LANES:LANES + 8], axis=0, keepdims=True)
        acca, accb = acca[0:LANES], accb[0:LANES]
        low = lax.broadcasted_iota(jnp.int32, (LANES, blk), 0) < HEAD_DIM
        o_ref[...] = jnp.where(low, acca / la, accb / lb).T
        lse_ref[0, 0] = ma + jnp.log(la) * LOG2E
        lse_ref[1, 0] = mb + jnp.log(lb) * LOG2E

    ins = [q, k, vt] + ([kbl] if has_bias else [])
    return pl.pallas_call(
        body, name=name, grid=(N_HEADS // 2, nb),
        in_specs=[_attn_specs(s, blk, width, qc, False), _attn_specs(s, blk, width, kc, True),
                  pl.BlockSpec((nb, LANES, blk), lambda p, i: (0, p, 0))]
                 + ([_attn_specs(s, blk, HEAD_DIM, 0, True)] if has_bias else []),
        out_specs=[pl.BlockSpec((blk, LANES), lambda p, i: (i, p)), pl.BlockSpec((2, 1, 1, blk), lambda p, i: (p, i, 0, 0))],
        out_shape=[jax.ShapeDtypeStruct((s, N_HEADS * HEAD_DIM), F32), jax.ShapeDtypeStruct((N_HEADS, nb, 1, blk), F32)],
        compiler_params=_params("parallel", "parallel"),
    )(*ins)


def _bias_lane_terms(kb2):
    terms, rest = [], kb2
    for _ in range(BIAS_TERMS):
        t = lax.reduce_precision(rest, 8, 7)
        terms.append(t.astype(BF16))
        rest = rest - t
    place = [[0.0] * (N_HEADS * HEAD_DIM) for _ in range(LANES)]
    for t in range(BIAS_TERMS):
        for h in range(N_HEADS):
            place[t * N_HEADS + h][(h // 2) * LANES + (HEAD_DIM if h % 2 == 0 else 0) + t] = 1.0
    return _matmul(_pad_axis(jnp.concatenate(terms, axis=1), 1, LANES), jnp.asarray(place, BF16), out_dtype=BF16,
                   name="fox_bias_lanes")


def _causal_keep(n):
    row = lax.broadcasted_iota(jnp.int32, (n, n), 0)
    col = lax.broadcasted_iota(jnp.int32, (n, n), 1)
    return col <= row


def _attn_delta(o, do, *, name):
    s, n = o.shape
    bs = _blk(s, ROW_BLOCK)

    def body(o_ref, do_ref, d_ref):
        for g in range(n // LANES):
            prod = do_ref[:, g * LANES:(g + 1) * LANES].astype(F32) * o_ref[:, g * LANES:(g + 1) * LANES]
            low = _lane_lt64(prod.shape)
            d_ref[:, g * LANES:(g + 1) * LANES] = _lane_merge(
                jnp.sum(jnp.where(low, prod, 0.0), axis=-1, keepdims=True),
                jnp.sum(jnp.where(low, 0.0, prod), axis=-1, keepdims=True))

    row = pl.BlockSpec((bs, n), lambda i: (i, 0))
    return pl.pallas_call(
        body, name=name, grid=(s // bs,), in_specs=[row, row], out_specs=row,
        out_shape=jax.ShapeDtypeStruct((s, n), F32), compiler_params=_params("parallel"),
    )(o, do)


def _attn_bwd(q, k, v, kb_col, do, lse_row, delta_row, *, qc, kc, vc, width, dq_mult, dk_mult, out_dtype, name):
    s = q.shape[0]
    blk = _blk(s, ATT_BLOCK)
    nb = s // blk
    has_bias = kb_col is not None

    def body(*refs):
        if has_bias:
            q_ref, k_ref, v_ref, kb_ref, do_ref, lse_ref, dl_ref, dk_ref, dv_ref, db_ref, dq_ref, dr_ref = refs
        else:
            q_ref, k_ref, v_ref, do_ref, lse_ref, dl_ref, dk_ref, dv_ref, db_ref, dq_ref = refs
        j = pl.program_id(1)

        @pl.when(j == 0)
        def _():
            dq_ref[...] = jnp.zeros_like(dq_ref)
            if has_bias:
                dr_ref[...] = jnp.zeros_like(dr_ref)

        kh = _pair(k_ref[...], width, True)
        v2 = v_ref[...]
        vh = (_keep_low(v2), _keep_high(v2))
        if has_bias:
            kb2 = kb_ref[0]
            kbh = (kb2[:, 0:1], kb2[:, 1:2])

        def step(i, carry, nblk, diag):
            rows = pl.ds(pl.multiple_of(i * blk, blk), nblk * blk)
            qh = _pair(q_ref[rows, :], width, False)
            doi = do_ref[rows, :]
            out, dq_parts = [], []
            for hd in range(2):
                dk, dvv, db = carry[hd]
                st = _qk_t(kh[hd], qh[hd])
                if has_bias:
                    st = st + kbh[hd]
                if diag:
                    row = lax.broadcasted_iota(jnp.int32, (blk, blk), 0)
                    colq = lax.broadcasted_iota(jnp.int32, (blk, blk), 1)
                    st = jnp.where(row <= colq, st, -1e30)
                lse_i = jnp.concatenate([lse_ref[hd, i + b] for b in range(nblk)], axis=1)
                delta_i = jnp.concatenate([dl_ref[hd, i + b] for b in range(nblk)], axis=1)
                pt = jnp.exp2(st - lse_i)
                dvv = dvv + jnp.dot(pt.astype(BF16), doi, preferred_element_type=F32)
                dst = pt * (_qk_t(vh[hd], doi) - delta_i)
                dsb = dst.astype(BF16)
                dk = dk + jnp.dot(dsb, qh[hd], preferred_element_type=F32)
                db = db + jnp.sum(dst, axis=-1, keepdims=True)
                dq_parts.append(lax.dot_general(dsb, kh[hd], (((0,), (0,)), ((), ())), preferred_element_type=F32))
                if has_bias:
                    rsum = jnp.sum(dst, axis=0, keepdims=True)
                    for b in range(nblk):
                        dr_ref[hd, i + b] += rsum[:, b * blk:(b + 1) * blk]
                out.append((dk, dvv, db))
            if width == HEAD_DIM:
                dq_ref[rows, :] += (dq_parts[0] + dq_parts[1]) * dq_mult
            else:
                dq_ref[rows, 0:LANES] += dq_parts[0] * dq_mult
                dq_ref[rows, LANES:2 * LANES] += dq_parts[1] * dq_mult
            return tuple(out)

        one = (jnp.zeros((blk, LANES), F32), jnp.zeros((blk, LANES), F32), jnp.zeros((blk, 1), F32))
        carry = step(j, (one, one), 1, True)
        rest = nb - 1 - j
        carry = lax.fori_loop(0, rest // 4, lambda t, c: step(j + 1 + 4 * t, c, 4, False), carry)
        carry = lax.fori_loop(0, (rest % 4) // 2, lambda _, c: step(nb - rest % 4, c, 2, False), carry)
        (dka, dva, dba), (dkb, dvb, dbb) = lax.fori_loop(0, rest % 2, lambda _, c: step(nb - 1, c, 1, False), carry)
        if width == HEAD_DIM:
            dk = _lane_merge(dka, dkb)
        else:
            dk = jnp.concatenate([dka, dkb], axis=1)
        dk_ref[...] = (dk * dk_mult).astype(out_dtype)
        dv_ref[...] = _lane_merge(dva, dvb).astype(out_dtype)
        db_ref[...] = _lane_merge(dba, dbb)

    stat = pl.BlockSpec((blk, LANES), lambda p, jj: (jj, p))
    rows = pl.BlockSpec((2, nb, 1, blk), lambda p, jj: (p, 0, 0, 0))
    ins = [q, k, v] + ([kb_col] if has_bias else []) + [do, lse_row, delta_row]
    return pl.pallas_call(
        body, name=name, grid=(N_HEADS // 2, nb),
        in_specs=[_attn_specs(s, blk, width, qc, True), _attn_specs(s, blk, width, kc, False),
                  _attn_specs(s, blk, HEAD_DIM, vc, False)]
                 + ([pl.BlockSpec((1, blk, 2), lambda p, jj: (p, jj, 0))] if has_bias else [])
                 + [pl.BlockSpec((s, LANES), lambda p, jj: (0, p)), rows, rows],
        out_specs=[pl.BlockSpec((blk, 2 * width), lambda p, jj: (jj, p)), stat, stat,
                   pl.BlockSpec((s, 2 * width), lambda p, jj: (0, p))] + ([rows] if has_bias else []),
        out_shape=[jax.ShapeDtypeStruct((s, N_HEADS * width), out_dtype), jax.ShapeDtypeStruct((s, N_HEADS * HEAD_DIM), out_dtype),
                   jax.ShapeDtypeStruct((s, N_HEADS * HEAD_DIM), F32), jax.ShapeDtypeStruct((s, N_HEADS * width), F32)]
                  + ([jax.ShapeDtypeStruct((N_HEADS, nb, 1, blk), F32)] if has_bias else []),
        compiler_params=_params("parallel", "arbitrary"),
    )(*ins)


def _head_stat(t):
    return t[:, ::HEAD_DIM]


def _stat_rows(t16, blk):
    s = t16.shape[0]
    return t16.T.reshape(N_HEADS, s // blk, 1, blk)


def _attention_bwd(res, do, *, qc, kc, vc, width, dq_mult, dk_mult, out_dtype, name):
    q, k, v, bias, o, lse_row = res
    s = q.shape[0]
    blk = _blk(s, ATT_BLOCK)
    kb_col = None if bias is None else bias.reshape(s, N_HEADS // 2, 2).transpose(1, 0, 2)
    delta_row = _stat_rows(_head_stat(_attn_delta(o, do, name=name + "_delta")), blk)
    outs = _attn_bwd(q, k, v, kb_col, do, lse_row, delta_row, qc=qc, kc=kc, vc=vc, width=width, dq_mult=dq_mult,
                     dk_mult=dk_mult, out_dtype=out_dtype, name=name + "_bwd")
    dk, dv, dcol, dq = outs[:4]
    if bias is None:
        return dq, dk, dv, None
    return dq, dk, dv, outs[4].reshape(N_HEADS, s).T - _head_stat(dcol)


def _fox_gate_fwd(fl, bf, *, name):
    s, n = fl.shape
    bs = _blk(s, ROW_BLOCK)

    def body(fl_ref, bf_ref, cum_ref, carry_ref):
        @pl.when(pl.program_id(0) == 0)
        def _():
            carry_ref[...] = jnp.zeros_like(carry_ref)

        z = fl_ref[...] + bf_ref[...]
        lf = jnp.minimum(z, 0.0) - jnp.log1p(jnp.exp(-jnp.abs(z)))
        row = lax.broadcasted_iota(jnp.int32, (bs, bs), 0)
        col = lax.broadcasted_iota(jnp.int32, (bs, bs), 1)
        tri = (col <= row).astype(F32)
        cum_ref[...] = jnp.dot(tri, lf, preferred_element_type=F32, precision=HIGHEST) + carry_ref[...]
        carry_ref[...] += jnp.sum(lf, axis=0, keepdims=True)

    return pl.pallas_call(
        body, name=name, grid=(s // bs,),
        in_specs=[pl.BlockSpec((bs, n), lambda i: (i, 0)), pl.BlockSpec((1, n), lambda i: (0, 0))],
        out_specs=pl.BlockSpec((bs, n), lambda i: (i, 0)),
        out_shape=jax.ShapeDtypeStruct((s, n), F32), scratch_shapes=[pltpu.VMEM((1, n), F32)],
        compiler_params=_params("arbitrary"),
    )(fl, bf)


def _fox_gate_bwd(fl, bf, dcum, *, name):
    s, n = fl.shape
    bs = _blk(s, ROW_BLOCK)
    nb = s // bs

    def body(fl_ref, bf_ref, dc_ref, dz_ref, dbf_ref, carry_ref):
        @pl.when(pl.program_id(0) == 0)
        def _():
            carry_ref[...] = jnp.zeros_like(carry_ref)
            dbf_ref[...] = jnp.zeros_like(dbf_ref)

        dc = dc_ref[...]
        row = lax.broadcasted_iota(jnp.int32, (bs, bs), 0)
        col = lax.broadcasted_iota(jnp.int32, (bs, bs), 1)
        tri = (col >= row).astype(F32)
        dlf = jnp.dot(tri, dc, preferred_element_type=F32, precision=HIGHEST) + carry_ref[...]
        carry_ref[...] += jnp.sum(dc, axis=0, keepdims=True)
        z = fl_ref[...] + bf_ref[...]
        dz = dlf / (1.0 + jnp.exp(z))
        dz_ref[...] = dz
        dbf_ref[...] += jnp.sum(dz, axis=0, keepdims=True)

    rev = pl.BlockSpec((bs, n), lambda i: (nb - 1 - i, 0))
    vec = pl.BlockSpec((1, n), lambda i: (0, 0))
    return pl.pallas_call(
        body, name=name, grid=(nb,), in_specs=[rev, vec, rev], out_specs=[rev, vec],
        out_shape=[jax.ShapeDtypeStruct((s, n), F32), jax.ShapeDtypeStruct((1, n), F32)],
        scratch_shapes=[pltpu.VMEM((1, n), F32)], compiler_params=_params("arbitrary"),
    )(fl, bf, dcum)


def _rope(x1, x2, cos, sin, *, negate, name):
    s, n = x1.shape
    bs = _blk(s, ROW_BLOCK)

    def body(a_ref, b_ref, c_ref, s_ref, o1_ref, o2_ref):
        a, b, cv = a_ref[...], b_ref[...], c_ref[...]
        sv = -s_ref[...] if negate else s_ref[...]
        o1_ref[...] = a * cv - b * sv
        o2_ref[...] = b * cv + a * sv

    row = pl.BlockSpec((bs, n), lambda i: (i, 0))
    return pl.pallas_call(
        body, name=name, grid=(s // bs,), in_specs=[row] * 4, out_specs=[row, row],
        out_shape=[jax.ShapeDtypeStruct((s, n), F32)] * 2, compiler_params=_params("parallel"),
    )(x1, x2, cos, sin)


def _rope_heads(x, ta, tb, tc, *, out_dtype, name):
    s, n = x.shape
    bs = _blk(s, ROW_BLOCK)

    def body(x_ref, a_ref, b_ref, c_ref, o_ref):
        av, bv, cv = a_ref[...], b_ref[...], c_ref[...]
        for g in range(n // LANES):
            xg = x_ref[:, g * LANES:(g + 1) * LANES]
            og = xg * av + pltpu.roll(xg, LANES - MLA_ROPE_HALF, 1) * bv + pltpu.roll(xg, MLA_ROPE_HALF, 1) * cv
            o_ref[:, g * LANES:(g + 1) * LANES] = og.astype(out_dtype)

    row = pl.BlockSpec((bs, n), lambda i: (i, 0))
    tab = pl.BlockSpec((bs, LANES), lambda i: (i, 0))
    return pl.pallas_call(
        body, name=name, grid=(s // bs,), in_specs=[row, tab, tab, tab], out_specs=row,
        out_shape=jax.ShapeDtypeStruct((s, n), out_dtype), compiler_params=_params("parallel"),
    )(x, ta, tb, tc)


def _group_sum(x, *, name):
    s, n = x.shape
    bs = _blk(s, ROW_BLOCK)

    def body(x_ref, o_ref):
        acc = x_ref[:, 0:LANES]
        for g in range(1, n // LANES):
            acc = acc + x_ref[:, g * LANES:(g + 1) * LANES]
        o_ref[...] = acc

    return pl.pallas_call(
        body, name=name, grid=(s // bs,), in_specs=[pl.BlockSpec((bs, n), lambda i: (i, 0))],
        out_specs=pl.BlockSpec((bs, LANES), lambda i: (i, 0)),
        out_shape=jax.ShapeDtypeStruct((s, LANES), F32), compiler_params=_params("parallel"),
    )(x)


def _shift_down(x, k):
    return pltpu.roll(x, k, 0)


def _conv_rows(ext, w_ref, b_ref, rows):
    y = b_ref[...] + w_ref[0:1, :] * _shift_down(ext, 2) + w_ref[1:2, :] * _shift_down(ext, 1) + w_ref[2:3, :] * ext
    return y[8:8 + rows]


def _conv_gate_fwd(u, cw, cb, *, name):
    s, f2 = u.shape
    f = f2 // 2
    nf = f // LANES
    r = _blk(s, CONV_ROWS)
    r8 = r // 8

    def body(ug_ref, ugp_ref, uv_ref, uvp_ref, wg_ref, wv_ref, bg_ref, bv_ref, o_ref):
        first = pl.program_id(1) == 0

        def conv(cur_ref, prev_ref, w_ref, b_ref):
            prev = jnp.where(first, 0.0, prev_ref[...])
            return _conv_rows(jnp.concatenate([prev, cur_ref[...]], axis=0), w_ref, b_ref, r)

        yg = conv(ug_ref, ugp_ref, wg_ref, bg_ref)
        yv = conv(uv_ref, uvp_ref, wv_ref, bv_ref)
        o_ref[...] = (yg * jax.nn.sigmoid(yg) * yv).astype(BF16)

    def cur(off):
        return pl.BlockSpec((r, LANES), lambda c, i: (i, c + off))

    def prev(off):
        return pl.BlockSpec((8, LANES), lambda c, i: (jnp.maximum(i * r8 - 1, 0), c + off))

    def wspec(rows, off):
        return pl.BlockSpec((rows, LANES), lambda c, i: (0, c + off))

    return pl.pallas_call(
        body, name=name, grid=(nf, s // r),
        in_specs=[cur(0), prev(0), cur(nf), prev(nf), wspec(3, 0), wspec(3, nf), wspec(1, 0), wspec(1, nf)],
        out_specs=pl.BlockSpec((r, LANES), lambda c, i: (i, c)),
        out_shape=jax.ShapeDtypeStruct((s, f), BF16), compiler_params=_params("parallel", "parallel"),
    )(u, u, u, u, cw, cw, cb, cb)


def _conv_gate_bwd(u, cw, cb, dg, *, name):
    s, f2 = u.shape
    f = f2 // 2
    nf = f // LANES
    r = _blk(s, CONV_ROWS)
    r8 = r // 8
    nr = s // r

    def body(ug_ref, ugp_ref, ugn_ref, uv_ref, uvp_ref, uvn_ref, wg_ref, wv_ref, bg_ref, bv_ref, dg_ref, dgn_ref,
             dug_ref, duv_ref, dwg_ref, dwv_ref, dbg_ref, dbv_ref):
        i = pl.program_id(1)
        first, last = i == 0, i == nr - 1

        def ext_of(cur_ref, prev_ref, next_ref):
            prev = jnp.where(first, 0.0, prev_ref[...])
            return jnp.concatenate([prev, cur_ref[...], next_ref[...]], axis=0)

        eg, ev = ext_of(ug_ref, ugp_ref, ugn_ref), ext_of(uv_ref, uvp_ref, uvn_ref)
        yg = _conv_rows(eg, wg_ref, bg_ref, r + 8)
        yv = _conv_rows(ev, wv_ref, bv_ref, r + 8)
        dgn = jnp.where(last, 0.0, dgn_ref[...])
        dgx = jnp.concatenate([dg_ref[...], dgn], axis=0)
        sg = jax.nn.sigmoid(yg)
        dyg = dgx * yv * (sg * (1.0 + yg * (1.0 - sg)))
        dyv = dgx * (yg * sg)

        @pl.when(i == 0)
        def _():
            for ref in (dwg_ref, dwv_ref, dbg_ref, dbv_ref):
                ref[...] = jnp.zeros_like(ref)

        def grads(dy, ext, w_ref, du_ref, dw_ref, db_ref):
            n = r + 8
            du = w_ref[2:3, :] * dy + w_ref[1:2, :] * pltpu.roll(dy, n - 1, 0) + w_ref[0:1, :] * pltpu.roll(dy, n - 2, 0)
            du_ref[...] = du[0:r].astype(BF16)
            dyc = dy[0:r]
            db_ref[...] += jnp.sum(dyc, axis=0, keepdims=True)
            ext_c = ext[0:r + 8]
            dw_ref[0:1, :] += jnp.sum(dyc * _shift_down(ext_c, 2)[8:], axis=0, keepdims=True)
            dw_ref[1:2, :] += jnp.sum(dyc * _shift_down(ext_c, 1)[8:], axis=0, keepdims=True)
            dw_ref[2:3, :] += jnp.sum(dyc * ext_c[8:], axis=0, keepdims=True)

        grads(dyg, eg, wg_ref, dug_ref, dwg_ref, dbg_ref)
        grads(dyv, ev, wv_ref, duv_ref, dwv_ref, dbv_ref)

    def cur(off):
        return pl.BlockSpec((r, LANES), lambda c, i: (i, c + off))

    def prev(off):
        return pl.BlockSpec((8, LANES), lambda c, i: (jnp.maximum(i * r8 - 1, 0), c + off))

    def nxt(off):
        return pl.BlockSpec((8, LANES), lambda c, i: (jnp.minimum((i + 1) * r8, s // 8 - 1), c + off))

    def wspec(rows, off):
        return pl.BlockSpec((rows, LANES), lambda c, i: (0, c + off))

    outs = pl.pallas_call(
        body, name=name, grid=(nf, nr),
        in_specs=[cur(0), prev(0), nxt(0), cur(nf), prev(nf), nxt(nf), wspec(3, 0), wspec(3, nf), wspec(1, 0), wspec(1, nf),
                  cur(0), nxt(0)],
        out_specs=[cur(0), cur(0), wspec(3, 0), wspec(3, 0), wspec(1, 0), wspec(1, 0)],
        out_shape=[jax.ShapeDtypeStruct((s, f), BF16), jax.ShapeDtypeStruct((s, f), BF16),
                   jax.ShapeDtypeStruct((3, f), F32), jax.ShapeDtypeStruct((3, f), F32),
                   jax.ShapeDtypeStruct((1, f), F32), jax.ShapeDtypeStruct((1, f), F32)],
        compiler_params=_params("parallel", "arbitrary"),
    )(u, u, u, u, u, u, cw, cw, cb, cb, dg, dg)
    dug, duv, dwg, dwv, dbg, dbv = outs
    return jnp.concatenate([dug, duv], axis=1), jnp.concatenate([dwg, dwv], axis=1), jnp.concatenate([dbg, dbv], axis=1)


def _adamw(w, g, m, v, *, slabs, name):
    shape = w.shape
    cols = shape[-1]
    rows = w.size // cols
    w2, m2, v2 = (t.reshape(rows, cols) for t in (w, m, v))
    g2 = g.reshape((N_DEV, rows, cols) if slabs else (rows, cols))
    br = _row_blk(rows, ROW_BLOCK // 2 if slabs else ROW_BLOCK)

    def body(w_ref, g_ref, m_ref, v_ref, go_ref, d_ref, nm_ref, nv_ref):
        if slabs:
            gv = g_ref[0].astype(F32)
            for p in range(1, N_DEV):
                gv = gv + g_ref[p].astype(F32)
        else:
            gv = g_ref[...]
        nm = ADAM_B1 * m_ref[...] + (1.0 - ADAM_B1) * gv
        nv = ADAM_B2 * v_ref[...] + (1.0 - ADAM_B2) * (gv * gv)
        m_hat = nm / (1.0 - ADAM_B1 ** ADAM_STEP)
        v_hat = nv / (1.0 - ADAM_B2 ** ADAM_STEP)
        go_ref[...] = gv
        d_ref[...] = -ADAM_LR * (m_hat / (jnp.sqrt(v_hat) + ADAM_EPS) + ADAM_WD * w_ref[...])
        nm_ref[...] = nm
        nv_ref[...] = nv

    spec = pl.BlockSpec((br, cols), lambda i: (i, 0))
    gspec = pl.BlockSpec((N_DEV, br, cols), lambda i: (0, i, 0)) if slabs else spec
    outs = pl.pallas_call(
        body, name=name, grid=(rows // br,), in_specs=[spec, gspec, spec, spec], out_specs=[spec] * 4,
        out_shape=[jax.ShapeDtypeStruct((rows, cols), F32)] * 4, compiler_params=_params("parallel"),
    )(w2, g2, m2, v2)
    return tuple(t.reshape(shape) for t in outs)


def _exchange(xs, *, same_src, name):
    n = len(xs)
    slabs = [x.shape if same_src else x.shape[1:] for x in xs]

    def body(*refs):
        x_refs, o_refs = refs[:n], refs[n:2 * n]
        send_sems, recv_sems, loc_sems = refs[2 * n:]
        ix, iy, ic = lax.axis_index("x"), lax.axis_index("y"), lax.axis_index("c")
        me = 4 * ix + 2 * iy + ic
        local, sends, recvs = [], [], []
        for a in range(n):
            def src(p, a=a):
                return x_refs[a] if same_src else x_refs[a].at[p]

            local.append(pltpu.make_async_copy(src(me), o_refs[a].at[me], loc_sems.at[a]))
            for k in (1, 2, 4, 3, 5, 6, 7):
                px = 1 - ix if k & 4 else ix
                py = 1 - iy if k & 2 else iy
                pc = 1 - ic if k & 1 else ic
                p = 4 * px + 2 * py + pc
                for dst, out in ((me, sends), (p, recvs)):
                    out.append(pltpu.make_async_remote_copy(
                        src_ref=src(p), dst_ref=o_refs[a].at[dst], send_sem=send_sems.at[a, k - 1],
                        recv_sem=recv_sems.at[a, k - 1], device_id=(px, py, pc), device_id_type=pl.DeviceIdType.MESH))
        for cp in local + sends:
            cp.start()
        for cp in recvs:
            cp.wait_recv()
        for cp in sends:
            cp.wait_send()
        for cp in local:
            cp.wait()

    return pl.pallas_call(
        body, name=name,
        in_specs=[pl.BlockSpec(memory_space=pl.ANY)] * n, out_specs=[pl.BlockSpec(memory_space=pl.ANY)] * n,
        out_shape=[jax.ShapeDtypeStruct((N_DEV,) + tuple(sl), x.dtype) for sl, x in zip(slabs, xs)],
        scratch_shapes=[pltpu.SemaphoreType.DMA((n, N_DEV - 1)), pltpu.SemaphoreType.DMA((n, N_DEV - 1)),
                        pltpu.SemaphoreType.DMA((n,))],
        compiler_params=pltpu.CompilerParams(has_side_effects=True, vmem_limit_bytes=VMEM_LIMIT_BYTES),
    )(*xs)


def _gather_two_level(xs, *, name):
    n = len(xs)

    def body(*refs):
        x_refs, o_refs = refs[:n], refs[n:2 * n]
        send_sems, recv_sems, loc_sems = refs[2 * n:]
        ix, iy, ic = lax.axis_index("x"), lax.axis_index("y"), lax.axis_index("c")
        me = 4 * ix + 2 * iy + ic
        sib = 4 * ix + 2 * iy + (1 - ic)
        chips = [(1 - ix if ch & 2 else ix, 1 - iy if ch & 1 else iy) for ch in (1, 2, 3)]

        def copy(a, pos, src, slot, to):
            return pltpu.make_async_remote_copy(
                src_ref=src, dst_ref=o_refs[a].at[slot], send_sem=send_sems.at[a, pos], recv_sem=recv_sems.at[a, pos],
                device_id=to, device_id_type=pl.DeviceIdType.MESH)

        local = [pltpu.make_async_copy(x_refs[a], o_refs[a].at[me], loc_sems.at[a]) for a in range(n)]
        first, passed, arrive = [], [], []
        for a in range(n):
            first.append(copy(a, 0, x_refs[a], me, (ix, iy, 1 - ic)))
            arrive.append(copy(a, 0, x_refs[a], sib, (ix, iy, 1 - ic)))
            for ch, (px, py) in enumerate(chips, start=1):
                same, other = 4 * px + 2 * py + ic, 4 * px + 2 * py + (1 - ic)
                first.append(copy(a, 2 * ch - 1, x_refs[a], me, (px, py, ic)))
                passed.append((copy(a, 2 * ch - 1, x_refs[a], same, (px, py, ic)),
                               copy(a, 2 * ch, o_refs[a].at[same], same, (ix, iy, 1 - ic))))
                arrive.append(copy(a, 2 * ch, x_refs[a], other, (ix, iy, 1 - ic)))
        for cp in local + first:
            cp.start()
        for landed, onward in passed:
            landed.wait_recv()
            onward.start()
        for cp in arrive:
            cp.wait_recv()
        for cp in first + [onward for _, onward in passed]:
            cp.wait_send()
        for cp in local:
            cp.wait()

    return pl.pallas_call(
        body, name=name,
        in_specs=[pl.BlockSpec(memory_space=pl.ANY)] * n, out_specs=[pl.BlockSpec(memory_space=pl.ANY)] * n,
        out_shape=[jax.ShapeDtypeStruct((N_DEV,) + tuple(x.shape), x.dtype) for x in xs],
        scratch_shapes=[pltpu.SemaphoreType.DMA((n, N_DEV - 1)), pltpu.SemaphoreType.DMA((n, N_DEV - 1)),
                        pltpu.SemaphoreType.DMA((n,))],
        compiler_params=pltpu.CompilerParams(has_side_effects=True, vmem_limit_bytes=VMEM_LIMIT_BYTES),
    )(*xs)


def _sum_slabs(x, *, name):
    n, r, c = x.shape
    br = _row_blk(r, ROW_BLOCK)

    def body(x_ref, o_ref):
        acc = x_ref[0]
        for p in range(1, n):
            acc = acc + x_ref[p]
        o_ref[...] = acc

    return pl.pallas_call(
        body, name=name, grid=(r // br,), in_specs=[pl.BlockSpec((n, br, c), lambda i: (0, i, 0))],
        out_specs=pl.BlockSpec((br, c), lambda i: (i, 0)),
        out_shape=jax.ShapeDtypeStruct((r, c), F32), compiler_params=_params("parallel"),
    )(x)


def _silu(x, *, name):
    def body(x_ref, o_ref):
        xv = x_ref[...]
        o_ref[...] = (xv * jax.nn.sigmoid(xv)).astype(BF16)

    return pl.pallas_call(body, name=name, out_shape=jax.ShapeDtypeStruct(x.shape, BF16),
                          compiler_params=_params())(x)


_BIG = {"fox_w_in": 2, "fox_w_o": 1, "mla_w_a": 1, "mla_w_uq": 2, "mla_w_ukv": 2, "mla_w_o": 1, "ffn_w_in": 2, "ffn_w_out": 1}
_SMALL = {"mla_g_q": 1, "mla_g_kv": 1, "ffn_conv_w": 2}
_REPL = ("fox_b_f", "ffn_conv_b", "final_g")


def _gathered_to_full(g, axis):
    full = jnp.moveaxis(g, 0, axis)
    shape = list(full.shape)
    shape[axis:axis + 2] = [shape[axis] * shape[axis + 1]]
    return full.reshape(shape)


def _full_to_chunks(full, axis):
    shape = list(full.shape)
    shape[axis:axis + 1] = [N_DEV, shape[axis] // N_DEV]
    return jnp.moveaxis(full.reshape(shape), axis, 0)


def _per_head(parts, s_or_rows):
    return jnp.concatenate([p.reshape(s_or_rows, N_HEADS, -1) for p in parts], axis=-1).reshape(s_or_rows, -1)


def kernel(x, c, ada_w, ada_b, fox_w_in, fox_b_f, fox_w_o, mla_w_a, mla_g_q, mla_g_kv, mla_w_uq, mla_w_ukv, mla_w_o, ffn_w_in, ffn_conv_w, ffn_conv_b, ffn_w_out, final_g, loss_target, m_ada_w, m_ada_b, m_fox_w_in, m_fox_b_f, m_fox_w_o, m_mla_w_a, m_mla_g_q, m_mla_g_kv, m_mla_w_uq, m_mla_w_ukv, m_mla_w_o, m_ffn_w_in, m_ffn_conv_w, m_ffn_conv_b, m_ffn_w_out, m_final_g, v_ada_w, v_ada_b, v_fox_w_in, v_fox_b_f, v_fox_w_o, v_mla_w_a, v_mla_g_q, v_mla_g_kv, v_mla_w_uq, v_mla_w_ukv, v_mla_w_o, v_ffn_w_in, v_ffn_conv_w, v_ffn_conv_b, v_ffn_w_out, v_final_g):
    weights = dict(ada_w=ada_w, ada_b=ada_b, fox_w_in=fox_w_in, fox_b_f=fox_b_f, fox_w_o=fox_w_o, mla_w_a=mla_w_a,
                   mla_g_q=mla_g_q, mla_g_kv=mla_g_kv, mla_w_uq=mla_w_uq, mla_w_ukv=mla_w_ukv, mla_w_o=mla_w_o,
                   ffn_w_in=ffn_w_in, ffn_conv_w=ffn_conv_w, ffn_conv_b=ffn_conv_b, ffn_w_out=ffn_w_out, final_g=final_g)
    mom_m = dict(ada_w=m_ada_w, ada_b=m_ada_b, fox_w_in=m_fox_w_in, fox_b_f=m_fox_b_f, fox_w_o=m_fox_w_o, mla_w_a=m_mla_w_a,
                 mla_g_q=m_mla_g_q, mla_g_kv=m_mla_g_kv, mla_w_uq=m_mla_w_uq, mla_w_ukv=m_mla_w_ukv, mla_w_o=m_mla_w_o,
                 ffn_w_in=m_ffn_w_in, ffn_conv_w=m_ffn_conv_w, ffn_conv_b=m_ffn_conv_b, ffn_w_out=m_ffn_w_out, final_g=m_final_g)
    mom_v = dict(ada_w=v_ada_w, ada_b=v_ada_b, fox_w_in=v_fox_w_in, fox_b_f=v_fox_b_f, fox_w_o=v_fox_w_o, mla_w_a=v_mla_w_a,
                 mla_g_q=v_mla_g_q, mla_g_kv=v_mla_g_kv, mla_w_uq=v_mla_w_uq, mla_w_ukv=v_mla_w_ukv, mla_w_o=v_mla_w_o,
                 ffn_w_in=v_ffn_w_in, ffn_conv_w=v_ffn_conv_w, ffn_conv_b=v_ffn_conv_b, ffn_w_out=v_ffn_w_out, final_g=v_final_g)
    order = list(weights)
    x0 = x[0]
    target = loss_target[0]
    s = x0.shape[0]
    d = D_MODEL
    cols = ada_w.shape[-1]
    nq = N_HEADS * HEAD_DIM

    small_names = ["c"] + list(_SMALL)
    small_all = dict(zip(small_names, _exchange([c] + [weights[n] for n in _SMALL], same_src=True, name="gather_small")))
    c_all = small_all["c"].reshape(N_DEV, d)
    g_q = _gathered_to_full(small_all["mla_g_q"], 1)
    g_kv = _gathered_to_full(small_all["mla_g_kv"], 1)
    conv_w = _gathered_to_full(small_all["ffn_conv_w"], 2)

    c_pad = _pad_axis(c_all, 0, LANES)
    silu_c = _silu(c_pad, name="silu_c")
    w_ada = ada_w.reshape(4, d, cols)
    b_ada = ada_b.reshape(4, 1, cols)
    mods = [_matmul(silu_c, w_ada[i], name=f"ada_mod{i}")[:N_DEV] + b_ada[i] for i in range(4)]
    mod_send = _pad_axis(jnp.stack(mods, axis=1), 1, 8)
    mod_recv, = _exchange([mod_send], same_src=False, name="scatter_mod")
    mod = mod_recv[:, :4].transpose(1, 0, 2).reshape(4, 3 * d)
    shift = [mod[i:i + 1, 0:d] for i in range(4)]
    scale = [mod[i:i + 1, d:2 * d] for i in range(4)]
    gate = [mod[i:i + 1, 2 * d:3 * d] for i in range(4)]

    big_all = _gather_two_level([weights[n].astype(BF16) for n in _BIG], name="gather_weights")
    wfull = {n: _gathered_to_full(g, _BIG[n]) for n, g in zip(_BIG, big_all)}

    w_fox_in = _pad_axis(wfull["fox_w_in"][0], 1, LANES)
    w_fox_qkv, w_fox_f = w_fox_in[:, :3 * nq], w_fox_in[:, 3 * nq:]
    w_fox_o = wfull["fox_w_o"][0]
    w_a = _pad_axis(wfull["mla_w_a"][0], 1, LANES)
    wq = wfull["mla_w_uq"][0].reshape(MLA_Q_RANK, N_HEADS, HEAD_DIM + MLA_ROPE_DIM)
    w_uq = _pad_axis(wq, 2, LANES).reshape(MLA_Q_RANK, N_HEADS * LANES)
    wkv = wfull["mla_w_ukv"][0].reshape(MLA_KV_RANK, N_HEADS, 2 * HEAD_DIM)
    w_ukv = jnp.concatenate([wkv[:, :, :HEAD_DIM].reshape(MLA_KV_RANK, -1), wkv[:, :, HEAD_DIM:].reshape(MLA_KV_RANK, -1)], axis=1)
    w_mla_o = wfull["mla_w_o"][0]
    w_ffn_in = wfull["ffn_w_in"]
    w_ffn_out = wfull["ffn_w_out"]
    conv_b = ffn_conv_b

    fox_scale = HEAD_DIM ** -0.5
    mla_scale = (HEAD_DIM + MLA_ROPE_DIM) ** -0.5
    pos = jnp.arange(s, dtype=F32)
    inv_freq = ROPE_BASE ** (-jnp.arange(0, MLA_ROPE_DIM, 2, dtype=F32) / MLA_ROPE_DIM)
    ang = pos[:, None] * inv_freq[None, :]
    cos16, sin16 = jnp.cos(ang), jnp.sin(ang)
    z16, z32, z64 = jnp.zeros((s, 16), F32), jnp.zeros((s, 32), F32), jnp.zeros((s, 64), F32)
    tab_a = jnp.concatenate([jnp.ones((s, 64), F32), cos16, cos16, z32], axis=1) * (mla_scale * LOG2E)
    tab_b = jnp.concatenate([z64, -sin16, z16, z32], axis=1) * (mla_scale * LOG2E)
    tab_c = jnp.concatenate([z64, z16, sin16, z32], axis=1) * (mla_scale * LOG2E)

    h0 = _norm_fwd(x0, scale[0], shift[0], plus_one=True, out_dtype=BF16, name="ada_fwd0")
    q_mult = jnp.concatenate([jnp.full((1, nq), fox_scale * LOG2E, F32), jnp.ones((1, 2 * nq), F32)], axis=1)
    qkv = _matmul(h0, w_fox_qkv, out_dtype=BF16, col_scale=q_mult, name="fox_proj")
    fl = _matmul(h0, w_fox_f, name="fox_proj_f")[:, :N_HEADS]
    cum = _fox_gate_fwd(fl, fox_b_f, name="fox_gate_fwd")
    fox_cfg = dict(qc=0, kc=N_HEADS // 2, vc=N_HEADS, width=HEAD_DIM)
    kb2 = cum * -LOG2E
    att_blk = _blk(s, ATT_BLOCK)
    fvt = _matmul(w_fox_qkv[:, 2 * nq:], h0, ta=True, tb=True, out_dtype=BF16, name="fox_proj_vt")
    fo, fox_lse = _attn_fwd(qkv, qkv, _key_blocks(fvt, att_blk), _bias_lane_terms(kb2), qc=0, kc=N_HEADS // 2,
                            width=HEAD_DIM, name="fox_attn_fwd")
    fox_res = (qkv, qkv, qkv, kb2, fo, fox_lse)
    y0 = _matmul(fo, w_fox_o, name="fox_out")
    x1 = _resid_fwd(x0, y0, gate[0], name="resid_fwd0")

    def ffn_fwd(xin, li, sub):
        hh = _norm_fwd(xin, scale[sub], shift[sub], plus_one=True, out_dtype=BF16, name=f"ada_fwd{sub}")
        u = _matmul(hh, w_ffn_in[li], name=f"ffn_up{li}")
        g = _conv_gate_fwd(u, conv_w[li], conv_b[li:li + 1], name=f"conv_fwd{li}")
        y = _matmul(g, w_ffn_out[li], name=f"ffn_down{li}")
        return _resid_fwd(xin, y, gate[sub], name=f"resid_fwd{sub}"), (hh, u, g, y)

    x2, ffn0_res = ffn_fwd(x1, 0, 1)

    h2 = _norm_fwd(x2, scale[2], shift[2], plus_one=True, out_dtype=BF16, name="ada_fwd2")
    a = _matmul(h2, w_a, name="mla_a")
    a_q, a_kv = a[:, :MLA_Q_RANK], a[:, MLA_Q_RANK:MLA_Q_RANK + MLA_KV_RANK]
    kr1 = a[:, MLA_Q_RANK + MLA_KV_RANK:MLA_Q_RANK + MLA_KV_RANK + MLA_ROPE_HALF]
    kr2 = a[:, MLA_Q_RANK + MLA_KV_RANK + MLA_ROPE_HALF:MLA_Q_RANK + MLA_KV_RANK + MLA_ROPE_DIM]
    cq = _norm_fwd(a_q, g_q, jnp.zeros_like(g_q), plus_one=False, out_dtype=BF16, name="mla_norm_q")
    ckv = _norm_fwd(a_kv, g_kv, jnp.zeros_like(g_kv), plus_one=False, out_dtype=BF16, name="mla_norm_kv")
    qf = _matmul(cq, w_uq, name="mla_uq")
    kvf = _matmul(ckv, w_ukv, out_dtype=BF16, name="mla_ukv")
    mq = _rope_heads(qf, tab_a, tab_b, tab_c, out_dtype=BF16, name="rope_q")
    kk1, kk2 = _rope(kr1, kr2, cos16, sin16, negate=False, name="rope_k")
    k_tail = jnp.concatenate([kk1, kk2, z32], axis=1).astype(BF16)
    mk = jnp.concatenate([kvf[:, :nq].reshape(s, N_HEADS, HEAD_DIM),
                          jnp.broadcast_to(k_tail[:, None, :], (s, N_HEADS, HEAD_DIM))], axis=-1).reshape(s, N_HEADS * LANES)
    mla_cfg = dict(qc=0, kc=0, vc=N_HEADS // 2, width=LANES)
    mvt = _matmul(w_ukv[:, nq:], ckv, ta=True, tb=True, out_dtype=BF16, name="mla_ukv_vt")
    mo, mla_lse = _attn_fwd(mq, mk, _key_blocks(mvt, att_blk), None, qc=0, kc=0, width=LANES, name="mla_attn_fwd")
    mla_res = (mq, mk, kvf, None, mo, mla_lse)
    y2 = _matmul(mo, w_mla_o, name="mla_out")
    x3 = _resid_fwd(x2, y2, gate[2], name="resid_fwd2")

    x4, ffn1_res = ffn_fwd(x3, 1, 3)

    loss_vec, dx4, d_final_g = _final_loss(x4, final_g.reshape(1, d), target, name="final_loss")
    loss = lax.psum(loss_vec[0, 0], ("x", "y", "c"))

    grads = {}
    dmod = [None] * 4

    def ffn_bwd(dx_out, xin, li, sub, res):
        hh, u, g, y = res
        dy, dgate = _resid_bwd(dx_out, y, gate[sub], name=f"resid_bwd{sub}")
        gw_out = _matmul(g, dy, ta=True, out_dtype=BF16, name=f"ffn_down_dw{li}")
        dg = _matmul(dy, w_ffn_out[li], tb=True, name=f"ffn_down_dx{li}")
        du, dcw, dcb = _conv_gate_bwd(u, conv_w[li], conv_b[li:li + 1], dg, name=f"conv_bwd{li}")
        gw_in = _matmul(hh, du, ta=True, out_dtype=BF16, name=f"ffn_up_dw{li}")
        dh = _matmul(du, w_ffn_in[li], tb=True, out_dtype=BF16, name=f"ffn_up_dx{li}")
        dx_in, dscale, dshift = _norm_bwd(xin, scale[sub], dh, dx_out, plus_one=True, name=f"ada_bwd{sub}")
        dmod[sub] = jnp.concatenate([dshift, dscale, dgate], axis=1)
        return dx_in, gw_in, dcw, dcb, gw_out

    dx3, gw_in1, dcw1, dcb1, gw_out1 = ffn_bwd(dx4, x3, 1, 3, ffn1_res)

    dy2, dgate2 = _resid_bwd(dx3, y2, gate[2], name="resid_bwd2")
    grads["mla_w_o"] = _matmul(mo, dy2, ta=True, out_dtype=BF16, name="mla_out_dw")[None]
    dmo = _matmul(dy2, w_mla_o, tb=True, out_dtype=BF16, name="mla_out_dx")
    dmq, dmk, dmv, _ = _attention_bwd(mla_res, dmo, dq_mult=1.0 / LOG2E, dk_mult=1.0 / LOG2E, out_dtype=F32,
                                      name="mla_attn", **mla_cfg)
    dqf = _rope_heads(dmq, tab_a, -tab_b, -tab_c, out_dtype=BF16, name="rope_q_bwd")
    g_uq = _matmul(cq, dqf, ta=True, out_dtype=BF16, name="mla_uq_dw")
    dcq = _matmul(dqf, w_uq, tb=True, name="mla_uq_dx")
    dmk3 = dmk.reshape(s, N_HEADS, LANES)
    dkr = _group_sum(dmk, name="mla_krope_sum")
    dkr1, dkr2 = _rope(dkr[:, HEAD_DIM:HEAD_DIM + MLA_ROPE_HALF], dkr[:, HEAD_DIM + MLA_ROPE_HALF:HEAD_DIM + MLA_ROPE_DIM],
                       cos16, sin16, negate=True, name="rope_k_bwd")
    dkvf = jnp.concatenate([dmk3[:, :, :HEAD_DIM].reshape(s, nq).astype(BF16), dmv.astype(BF16)], axis=1)
    g_ukv = _matmul(ckv, dkvf, ta=True, out_dtype=BF16, name="mla_ukv_dw")
    dckv = _matmul(dkvf, w_ukv, tb=True, name="mla_ukv_dx")
    da_q, dg_q, _ = _norm_bwd(a_q, g_q, dcq, None, plus_one=False, name="mla_norm_q_bwd")
    da_kv, dg_kv, _ = _norm_bwd(a_kv, g_kv, dckv, None, plus_one=False, name="mla_norm_kv_bwd")
    da = jnp.concatenate([da_q, da_kv, dkr1, dkr2, jnp.zeros((s, w_a.shape[1] - 672), F32)], axis=1).astype(BF16)
    grads["mla_w_a"] = _matmul(h2, da, ta=True, out_dtype=BF16, name="mla_a_dw")[None, :, :672]
    dh2 = _matmul(da, w_a, tb=True, out_dtype=BF16, name="mla_a_dx")
    dx2, dscale2, dshift2 = _norm_bwd(x2, scale[2], dh2, dx3, plus_one=True, name="ada_bwd2")
    dmod[2] = jnp.concatenate([dshift2, dscale2, dgate2], axis=1)
    grads["mla_w_uq"] = g_uq.reshape(MLA_Q_RANK, N_HEADS, LANES)[:, :, :HEAD_DIM + MLA_ROPE_DIM].reshape(1, MLA_Q_RANK, -1)
    grads["mla_w_ukv"] = _per_head([g_ukv[:, :nq], g_ukv[:, nq:]], MLA_KV_RANK)[None]
    grads["mla_g_q"], grads["mla_g_kv"] = dg_q, dg_kv

    dx1, gw_in0, dcw0, dcb0, gw_out0 = ffn_bwd(dx2, x1, 0, 1, ffn0_res)
    grads["ffn_w_in"] = jnp.stack([gw_in0, gw_in1])
    grads["ffn_w_out"] = jnp.stack([gw_out0, gw_out1])
    grads["ffn_conv_w"] = jnp.stack([dcw0, dcw1])
    g_conv_b = jnp.concatenate([dcb0, dcb1], axis=0)

    dy0, dgate0 = _resid_bwd(dx1, y0, gate[0], name="resid_bwd0")
    grads["fox_w_o"] = _matmul(fo, dy0, ta=True, out_dtype=BF16, name="fox_out_dw")[None]
    dfo = _matmul(dy0, w_fox_o, tb=True, out_dtype=BF16, name="fox_out_dx")
    dfq, dfk, dfv, dcum = _attention_bwd(fox_res, dfo, dq_mult=fox_scale, dk_mult=1.0 / LOG2E, out_dtype=BF16,
                                         name="fox_attn", **fox_cfg)
    dfl, g_b_f = _fox_gate_bwd(fl, fox_b_f, dcum, name="fox_gate_bwd")
    dproj = jnp.concatenate([dfq.astype(BF16), dfk, dfv, _pad_axis(dfl, 1, LANES).astype(BF16)], axis=1)
    grads["fox_w_in"] = _matmul(h0, dproj, ta=True, out_dtype=BF16, name="fox_proj_dw")[None, :, :3 * nq + N_HEADS]
    dh0 = _matmul(dproj, w_fox_in, tb=True, out_dtype=BF16, name="fox_proj_dx")
    dx0, dscale0, dshift0 = _norm_bwd(x0, scale[0], dh0, dx1, plus_one=True, name="ada_bwd0")
    dmod[0] = jnp.concatenate([dshift0, dscale0, dgate0], axis=1)

    dmod_send = _pad_axis(jnp.stack(dmod, axis=0).reshape(4, N_DEV, cols).transpose(1, 0, 2), 1, 8)
    dmod_recv, = _exchange([dmod_send], same_src=False, name="scatter_dmod")
    dmod_all = dmod_recv[:, :4]
    dmod_pad = _pad_axis(dmod_all, 0, LANES)
    g_ada_w = jnp.stack([_matmul(silu_c, dmod_pad[:, i], ta=True, name=f"ada_dw{i}") for i in range(4)])
    grads["ada_w"] = g_ada_w.reshape(ada_w.shape)
    grads["ada_b"] = _sum_slabs(dmod_recv, name="ada_db")[:4].reshape(ada_b.shape)

    sharded = list(_BIG) + list(_SMALL)
    axes = {**_BIG, **_SMALL}
    recv = _exchange([_full_to_chunks(grads[n], axes[n]) for n in sharded], same_src=False, name="scatter_grads")
    grads.update(dict(zip(sharded, recv)))
    repl = _exchange([g_b_f, g_conv_b, d_final_g], same_src=True, name="gather_repl_grads")
    grads.update(dict(zip(_REPL, repl)))

    grad_out, deltas, new_m, new_v = {}, {}, {}, {}
    for n in order:
        grad_out[n], deltas[n], new_m[n], new_v[n] = _adamw(
            weights[n], grads[n], mom_m[n], mom_v[n], slabs=n in axes or n in _REPL, name=f"adamw_{n}")

    grad_x = dx0[None]
    return (loss, grad_x, *[grad_out[n] for n in order], *[deltas[n] for n in order],
            *[new_m[n] for n in order], *[new_v[n] for n in order])
```

```python
import jax
import jax.numpy as jnp
from jax import lax
from jax.experimental import pallas as pl
from jax.experimental.pallas import tpu as pltpu

F32 = jnp.float32
BF16 = jnp.bfloat16
HIGHEST = lax.Precision.HIGHEST

N_DEV = 8
D_MODEL = 1024
N_HEADS = 16
HEAD_DIM = 64
MLA_ROPE_HALF = 16
MLA_Q_RANK = 384
MLA_KV_RANK = 256
MLA_ROPE_DIM = 32
NORM_EPS = 1e-6
ROPE_BASE = 10000.0
ADAM_LR = 0.001
ADAM_B1 = 0.9
ADAM_B2 = 0.999
ADAM_EPS = 1e-08
ADAM_WD = 0.01
ADAM_STEP = 10

LANES = 128
VMEM_LIMIT_BYTES = 56 * 1024 * 1024
ROW_BLOCK = 512
ATT_BLOCK = 512
CONV_ROWS = 1024
MM_BM, MM_BN, MM_BK = 512, 1408, 2048
MM_K_WHOLE = 3328
LOG2E = 1.4426950408889634


def _params(*sem):
    return pltpu.CompilerParams(dimension_semantics=sem or None, vmem_limit_bytes=VMEM_LIMIT_BYTES)


def _blk(dim, pref):
    if dim <= pref:
        return dim
    b = pref - pref % LANES
    while b >= LANES:
        if dim % b == 0:
            return b
        b -= LANES
    raise ValueError(f"no block for {dim}")


def _row_blk(rows, pref):
    if rows <= pref:
        return rows
    for b in range(pref - pref % 8, 7, -8):
        if rows % b == 0:
            return b
    return rows


def _pad_axis(a, axis, mult):
    pad = (-a.shape[axis]) % mult
    if pad == 0:
        return a
    widths = [(0, 0)] * a.ndim
    widths[axis] = (0, pad)
    return jnp.pad(a, widths)


def _matmul(a, b, *, ta=False, tb=False, out_dtype=F32, col_scale=None, name):
    m, k = (a.shape[1], a.shape[0]) if ta else a.shape
    n = b.shape[0] if tb else b.shape[1]
    assert (b.shape[1] if tb else b.shape[0]) == k, (a.shape, b.shape, ta, tb)
    bm, bn = _blk(m, 2 * MM_BM), _blk(n, MM_BN)
    if ta and bm < MM_BM and m % MM_BN == 0:
        bm = MM_BN
    bk = k if k <= MM_K_WHOLE else _blk(k, MM_BK)
    nk = k // bk
    dims = (((0 if ta else 1,), (1 if tb else 0,)), ((), ()))
    has_scale = col_scale is not None
    use_acc = nk > 1 and (out_dtype != F32 or has_scale)

    def body(*refs):
        a_ref, b_ref = refs[0], refs[1]
        s_ref = refs[2] if has_scale else None
        o_ref = refs[3] if has_scale else refs[2]
        acc_ref = refs[-1] if use_acc else o_ref
        kk = pl.program_id(2)
        part = lax.dot_general(a_ref[...].astype(BF16), b_ref[...].astype(BF16), dims, preferred_element_type=F32)

        def finish(val):
            if has_scale:
                val = val * s_ref[...]
            o_ref[...] = val.astype(out_dtype)

        if nk == 1:
            finish(part)
            return

        @pl.when(kk == 0)
        def _():
            acc_ref[...] = part

        @pl.when(kk > 0)
        def _():
            acc_ref[...] += part

        if use_acc:
            @pl.when(kk == nk - 1)
            def _():
                finish(acc_ref[...])

    a_spec = pl.BlockSpec((bk, bm), lambda i, j, kk: (kk, i)) if ta else pl.BlockSpec((bm, bk), lambda i, j, kk: (i, kk))
    b_spec = pl.BlockSpec((bn, bk), lambda i, j, kk: (j, kk)) if tb else pl.BlockSpec((bk, bn), lambda i, j, kk: (kk, j))
    return pl.pallas_call(
        body, name=name, grid=(m // bm, n // bn, nk),
        in_specs=[a_spec, b_spec] + ([pl.BlockSpec((1, bn), lambda i, j, kk: (0, j))] if has_scale else []),
        out_specs=pl.BlockSpec((bm, bn), lambda i, j, kk: (i, j)),
        out_shape=jax.ShapeDtypeStruct((m, n), out_dtype),
        scratch_shapes=[pltpu.VMEM((bm, bn), F32)] if use_acc else [],
        compiler_params=_params("parallel", "parallel", "arbitrary"),
    )(*([a, b] + ([col_scale] if has_scale else [])))


def _norm_fwd(x, mul, add, *, plus_one, out_dtype, name):
    s, n = x.shape
    bs = _blk(s, ROW_BLOCK)

    def body(x_ref, m_ref, a_ref, o_ref):
        xv = x_ref[...]
        r = lax.rsqrt(jnp.mean(xv * xv, axis=-1, keepdims=True) + NORM_EPS)
        mv = m_ref[...] + 1.0 if plus_one else m_ref[...]
        o_ref[...] = (xv * r * mv + a_ref[...]).astype(out_dtype)

    row = pl.BlockSpec((bs, n), lambda i: (i, 0))
    vec = pl.BlockSpec((1, n), lambda i: (0, 0))
    return pl.pallas_call(
        body, name=name, grid=(s // bs,), in_specs=[row, vec, vec], out_specs=row,
        out_shape=jax.ShapeDtypeStruct((s, n), out_dtype), compiler_params=_params("parallel"),
    )(x, mul, add)


def _norm_bwd(x, mul, dy, dres, *, plus_one, name):
    s, n = x.shape
    bs = _blk(s, ROW_BLOCK)
    has_res = dres is not None

    def body(*refs):
        if has_res:
            x_ref, m_ref, dy_ref, dres_ref, dx_ref, dm_ref, da_ref = refs
        else:
            x_ref, m_ref, dy_ref, dx_ref, dm_ref, da_ref = refs
        xv = x_ref[...]
        dyv = dy_ref[...].astype(F32)
        r = lax.rsqrt(jnp.mean(xv * xv, axis=-1, keepdims=True) + NORM_EPS)
        xn = xv * r
        mv = m_ref[...] + 1.0 if plus_one else m_ref[...]
        g = dyv * mv
        dx = r * (g - xn * jnp.mean(g * xn, axis=-1, keepdims=True))
        if has_res:
            dx = dx + dres_ref[...]
        dx_ref[...] = dx

        @pl.when(pl.program_id(0) == 0)
        def _():
            dm_ref[...] = jnp.zeros_like(dm_ref)
            da_ref[...] = jnp.zeros_like(da_ref)

        dm_ref[...] += jnp.sum(dyv * xn, axis=0, keepdims=True)
        da_ref[...] += jnp.sum(dyv, axis=0, keepdims=True)

    row = pl.BlockSpec((bs, n), lambda i: (i, 0))
    vec = pl.BlockSpec((1, n), lambda i: (0, 0))
    ins = [x, mul, dy] + ([dres] if has_res else [])
    return pl.pallas_call(
        body, name=name, grid=(s // bs,),
        in_specs=[row, vec, row] + ([row] if has_res else []), out_specs=[row, vec, vec],
        out_shape=[jax.ShapeDtypeStruct((s, n), F32), jax.ShapeDtypeStruct((1, n), F32), jax.ShapeDtypeStruct((1, n), F32)],
        compiler_params=_params("arbitrary"),
    )(*ins)


def _resid_fwd(x, y, gate, *, name):
    s, n = x.shape
    bs = _blk(s, ROW_BLOCK)

    def body(x_ref, y_ref, g_ref, o_ref):
        o_ref[...] = x_ref[...] + g_ref[...] * y_ref[...]

    row = pl.BlockSpec((bs, n), lambda i: (i, 0))
    vec = pl.BlockSpec((1, n), lambda i: (0, 0))
    return pl.pallas_call(
        body, name=name, grid=(s // bs,), in_specs=[row, row, vec], out_specs=row,
        out_shape=jax.ShapeDtypeStruct((s, n), F32), compiler_params=_params("parallel"),
    )(x, y, gate)


def _resid_bwd(dx, y, gate, *, name):
    s, n = dx.shape
    bs = _blk(s, ROW_BLOCK)

    def body(dx_ref, y_ref, g_ref, dy_ref, dg_ref):
        dxv = dx_ref[...]
        dy_ref[...] = (g_ref[...] * dxv).astype(BF16)

        @pl.when(pl.program_id(0) == 0)
        def _():
            dg_ref[...] = jnp.zeros_like(dg_ref)

        dg_ref[...] += jnp.sum(dxv * y_ref[...], axis=0, keepdims=True)

    row = pl.BlockSpec((bs, n), lambda i: (i, 0))
    vec = pl.BlockSpec((1, n), lambda i: (0, 0))
    return pl.pallas_call(
        body, name=name, grid=(s // bs,), in_specs=[row, row, vec], out_specs=[row, vec],
        out_shape=[jax.ShapeDtypeStruct((s, n), BF16), jax.ShapeDtypeStruct((1, n), F32)],
        compiler_params=_params("arbitrary"),
    )(dx, y, gate)


def _final_loss(x, g, target, *, name):
    s, n = x.shape
    bs = _blk(s, ROW_BLOCK)

    def body(x_ref, g_ref, t_ref, loss_ref, dx_ref, dg_ref):
        xv = x_ref[...]
        r = lax.rsqrt(jnp.mean(xv * xv, axis=-1, keepdims=True) + NORM_EPS)
        xn = xv * r
        gv = g_ref[...]
        err = xn * gv - t_ref[...]
        dout = err * (1.0 / n)
        gg = dout * gv
        dx_ref[...] = r * (gg - xn * jnp.mean(gg * xn, axis=-1, keepdims=True))

        @pl.when(pl.program_id(0) == 0)
        def _():
            loss_ref[...] = jnp.zeros_like(loss_ref)
            dg_ref[...] = jnp.zeros_like(dg_ref)

        part = jnp.sum(jnp.sum(err * err, axis=-1, keepdims=True), axis=0, keepdims=True) * (0.5 / n)
        loss_ref[...] += jnp.broadcast_to(part, loss_ref.shape)
        dg_ref[...] += jnp.sum(dout * xn, axis=0, keepdims=True)

    row = pl.BlockSpec((bs, n), lambda i: (i, 0))
    vec = pl.BlockSpec((1, n), lambda i: (0, 0))
    return pl.pallas_call(
        body, name=name, grid=(s // bs,), in_specs=[row, vec, row],
        out_specs=[pl.BlockSpec((1, LANES), lambda i: (0, 0)), row, vec],
        out_shape=[jax.ShapeDtypeStruct((1, LANES), F32), jax.ShapeDtypeStruct((s, n), F32), jax.ShapeDtypeStruct((1, n), F32)],
        compiler_params=_params("arbitrary"),
    )(x, g, target)


def _lane_lt64(shape):
    return lax.broadcasted_iota(jnp.int32, shape, 1) < HEAD_DIM


def _keep_low(x):
    return jnp.where(_lane_lt64(x.shape), x.astype(F32), 0.0).astype(x.dtype)


def _keep_high(x):
    return jnp.where(_lane_lt64(x.shape), 0.0, x.astype(F32)).astype(x.dtype)


def _lane_merge(a, b):
    n = max(a.shape[0], b.shape[0])
    return jnp.where(_lane_lt64((n, LANES)), a, b)


def _pair(x, width, masked):
    if width == HEAD_DIM:
        return (_keep_low(x), _keep_high(x)) if masked else (x, x)
    return x[:, :LANES], x[:, LANES:]


def _qk_t(a, b):
    return lax.dot_general(a, b, (((1,), (1,)), ((), ())), preferred_element_type=F32)


def _attn_specs(s, blk, width, cols, resident):
    w = 2 * width
    if resident:
        return pl.BlockSpec((s, w), lambda p, i: (0, cols + p))
    return pl.BlockSpec((blk, w), lambda p, i: (i, cols + p))


BIAS_TERMS = 3


def _key_blocks(vt, blk):
    return vt.reshape(vt.shape[0], vt.shape[1] // blk, blk).transpose(1, 0, 2)


def _attn_fwd(q, k, vt, kbl, *, qc, kc, width, name):
    s = q.shape[0]
    blk = _blk(s, ATT_BLOCK)
    nb = s // blk
    has_bias = kbl is not None
    assert has_bias == (width == HEAD_DIM)

    def body(*refs):
        if has_bias:
            q_ref, k_ref, vt_ref, kbl_ref, o_ref, lse_ref = refs
        else:
            q_ref, k_ref, vt_ref, o_ref, lse_ref = refs
        i = pl.program_id(1)
        q2 = q_ref[...]
        if has_bias:
            lane = lax.broadcasted_iota(jnp.int32, (blk, LANES), 1)
            qf = q2.astype(F32)
            qh = (jnp.where(lane < HEAD_DIM, qf, jnp.where(lane < HEAD_DIM + BIAS_TERMS, 1.0, 0.0)).astype(BF16),
                  jnp.where(lane >= HEAD_DIM, qf, jnp.where(lane < BIAS_TERMS, 1.0, 0.0)).astype(BF16))
        else:
            qh = (q2[:, :LANES], q2[:, LANES:])

        def step(j, carry, nblk, diag):
            rows = pl.ds(pl.multiple_of(j * blk, blk), nblk * blk)
            vt1 = [jnp.concatenate([vt_ref[j + b], jnp.ones((16, blk), BF16)], axis=0) for b in range(nblk)]
            k2 = k_ref[rows, :]
            if has_bias:
                low = _lane_lt64(k2.shape)
                kf, bf = k2.astype(F32), kbl_ref[rows, :].astype(F32)
                kh = (jnp.where(low, kf, bf).astype(BF16), jnp.where(low, bf, kf).astype(BF16))
            else:
                kh = (k2[:, :LANES], k2[:, LANES:])
            out = []
            for hd in range(2):
                m, acc = carry[hd]
                st = _qk_t(kh[hd], qh[hd])
                if diag:
                    row = lax.broadcasted_iota(jnp.int32, (blk, blk), 0)
                    colq = lax.broadcasted_iota(jnp.int32, (blk, blk), 1)
                    st = jnp.where(row <= colq, st, -1e30)
                m_new = jnp.maximum(m, jnp.max(st, axis=0, keepdims=True))
                alpha = jnp.exp2(m - m_new)
                pt = jnp.exp2(st - m_new).astype(BF16)
                acc = alpha * acc
                for b in range(nblk):
                    acc = acc + jnp.dot(vt1[b], pt[b * blk:(b + 1) * blk], preferred_element_type=F32)
                out.append((m_new, acc))
            return tuple(out)

        one = (jnp.full((1, blk), -1e30, F32), jnp.zeros((LANES + 16, blk), F32))
        carry = lax.fori_loop(0, i // 8, lambda j, c: step(8 * j, c, 8, False), (one, one))
        carry = lax.fori_loop(0, (i % 8) // 4, lambda _, c: step(i - i % 8, c, 4, False), carry)
        carry = lax.fori_loop(0, (i % 4) // 2, lambda _, c: step(i - i % 4, c, 2, False), carry)
        carry = lax.fori_loop(0, i % 2, lambda _, c: step(i - 1, c, 1, False), carry)
        (ma, acca), (mb, accb) = step(i, carry, 1, True)
        la = jnp.max(acca[LANES:LANES + 8], axis=0, keepdims=True)
        lb = jnp.max(accb[LANES:LANES + 8], axis=0, keepdims=True)
        acca, accb = acca[0:LANES], accb[0:LANES]
        low = lax.broadcasted_iota(jnp.int32, (LANES, blk), 0) < HEAD_DIM
        o_ref[...] = jnp.where(low, acca / la, accb / lb).T
        lse_ref[0, 0] = ma + jnp.log(la) * LOG2E
        lse_ref[1, 0] = mb + jnp.log(lb) * LOG2E

    ins = [q, k, vt] + ([kbl] if has_bias else [])
    return pl.pallas_call(
        body, name=name, grid=(N_HEADS // 2, nb),
        in_specs=[_attn_specs(s, blk, width, qc, False), _attn_specs(s, blk, width, kc, True),
                  pl.BlockSpec((nb, LANES, blk), lambda p, i: (0, p, 0))]
                 + ([_attn_specs(s, blk, HEAD_DIM, 0, True)] if has_bias else []),
        out_specs=[pl.BlockSpec((blk, LANES), lambda p, i: (i, p)), pl.BlockSpec((2, 1, 1, blk), lambda p, i: (p, i, 0, 0))],
        out_shape=[jax.ShapeDtypeStruct((s, N_HEADS * HEAD_DIM), F32), jax.ShapeDtypeStruct((N_HEADS, nb, 1, blk), F32)],
        compiler_params=_params("parallel", "parallel"),
    )(*ins)


def _bias_lane_terms(kb2):
    terms, rest = [], kb2
    for _ in range(BIAS_TERMS):
        t = lax.reduce_precision(rest, 8, 7)
        terms.append(t.astype(BF16))
        rest = rest - t
    place = [[0.0] * (N_HEADS * HEAD_DIM) for _ in range(LANES)]
    for t in range(BIAS_TERMS):
        for h in range(N_HEADS):
            place[t * N_HEADS + h][(h // 2) * LANES + (HEAD_DIM if h % 2 == 0 else 0) + t] = 1.0
    return _matmul(_pad_axis(jnp.concatenate(terms, axis=1), 1, LANES), jnp.asarray(place, BF16), out_dtype=BF16,
                   name="fox_bias_lanes")


def _causal_keep(n):
    row = lax.broadcasted_iota(jnp.int32, (n, n), 0)
    col = lax.broadcasted_iota(jnp.int32, (n, n), 1)
    return col <= row


def _attn_delta(o, do, *, name):
    s, n = o.shape
    bs = _blk(s, ROW_BLOCK)

    def body(o_ref, do_ref, d_ref):
        for g in range(n // LANES):
            prod = do_ref[:, g * LANES:(g + 1) * LANES].astype(F32) * o_ref[:, g * LANES:(g + 1) * LANES]
            low = _lane_lt64(prod.shape)
            d_ref[:, g * LANES:(g + 1) * LANES] = _lane_merge(
                jnp.sum(jnp.where(low, prod, 0.0), axis=-1, keepdims=True),
                jnp.sum(jnp.where(low, 0.0, prod), axis=-1, keepdims=True))

    row = pl.BlockSpec((bs, n), lambda i: (i, 0))
    return pl.pallas_call(
        body, name=name, grid=(s // bs,), in_specs=[row, row], out_specs=row,
        out_shape=jax.ShapeDtypeStruct((s, n), F32), compiler_params=_params("parallel"),
    )(o, do)


def _attn_bwd(q, k, v, kb_col, do, lse_row, delta_row, *, qc, kc, vc, width, dq_mult, dk_mult, out_dtype, name):
    s = q.shape[0]
    blk = _blk(s, ATT_BLOCK)
    nb = s // blk
    has_bias = kb_col is not None

    def body(*refs):
        if has_bias:
            q_ref, k_ref, v_ref, kb_ref, do_ref, lse_ref, dl_ref, dk_ref, dv_ref, db_ref, dq_ref, dr_ref = refs
        else:
            q_ref, k_ref, v_ref, do_ref, lse_ref, dl_ref, dk_ref, dv_ref, db_ref, dq_ref = refs
        j = pl.program_id(1)

        @pl.when(j == 0)
        def _():
            dq_ref[...] = jnp.zeros_like(dq_ref)
            if has_bias:
                dr_ref[...] = jnp.zeros_like(dr_ref)

        kh = _pair(k_ref[...], width, True)
        v2 = v_ref[...]
        vh = (_keep_low(v2), _keep_high(v2))
        if has_bias:
            kb2 = kb_ref[0]
            kbh = (kb2[:, 0:1], kb2[:, 1:2])

        def step(i, carry, nblk, diag):
            rows = pl.ds(pl.multiple_of(i * blk, blk), nblk * blk)
            qh = _pair(q_ref[rows, :], width, False)
            doi = do_ref[rows, :]
            out, dq_parts = [], []
            for hd in range(2):
                dk, dvv, db = carry[hd]
                st = _qk_t(kh[hd], qh[hd])
                if has_bias:
                    st = st + kbh[hd]
                if diag:
                    row = lax.broadcasted_iota(jnp.int32, (blk, blk), 0)
                    colq = lax.broadcasted_iota(jnp.int32, (blk, blk), 1)
                    st = jnp.where(row <= colq, st, -1e30)
                lse_i = jnp.concatenate([lse_ref[hd, i + b] for b in range(nblk)], axis=1)
                delta_i = jnp.concatenate([dl_ref[hd, i + b] for b in range(nblk)], axis=1)
                pt = jnp.exp2(st - lse_i)
                dvv = dvv + jnp.dot(pt.astype(BF16), doi, preferred_element_type=F32)
                dst = pt * (_qk_t(vh[hd], doi) - delta_i)
                dsb = dst.astype(BF16)
                dk = dk + jnp.dot(dsb, qh[hd], preferred_element_type=F32)
                db = db + jnp.sum(dst, axis=-1, keepdims=True)
                dq_parts.append(lax.dot_general(dsb, kh[hd], (((0,), (0,)), ((), ())), preferred_element_type=F32))
                if has_bias:
                    rsum = jnp.sum(dst, axis=0, keepdims=True)
                    for b in range(nblk):
                        dr_ref[hd, i + b] += rsum[:, b * blk:(b + 1) * blk]
                out.append((dk, dvv, db))
            if width == HEAD_DIM:
                dq_ref[rows, :] += (dq_parts[0] + dq_parts[1]) * dq_mult
            else:
                dq_ref[rows, 0:LANES] += dq_parts[0] * dq_mult
                dq_ref[rows, LANES:2 * LANES] += dq_parts[1] * dq_mult
            return tuple(out)

        one = (jnp.zeros((blk, LANES), F32), jnp.zeros((blk, LANES), F32), jnp.zeros((blk, 1), F32))
        carry = step(j, (one, one), 1, True)
        rest = nb - 1 - j
        carry = lax.fori_loop(0, rest // 4, lambda t, c: step(j + 1 + 4 * t, c, 4, False), carry)
        carry = lax.fori_loop(0, (rest % 4) // 2, lambda _, c: step(nb - rest % 4, c, 2, False), carry)
        (dka, dva, dba), (dkb, dvb, dbb) = lax.fori_loop(0, rest % 2, lambda _, c: step(nb - 1, c, 1, False), carry)
        if width == HEAD_DIM:
            dk = _lane_merge(dka, dkb)
        else:
            dk = jnp.concatenate([dka, dkb], axis=1)
        dk_ref[...] = (dk * dk_mult).astype(out_dtype)
        dv_ref[...] = _lane_merge(dva, dvb).astype(out_dtype)
        db_ref[...] = _lane_merge(dba, dbb)

    stat = pl.BlockSpec((blk, LANES), lambda p, jj: (jj, p))
    rows = pl.BlockSpec((2, nb, 1, blk), lambda p, jj: (p, 0, 0, 0))
    ins = [q, k, v] + ([kb_col] if has_bias else []) + [do, lse_row, delta_row]
    return pl.pallas_call(
        body, name=name, grid=(N_HEADS // 2, nb),
        in_specs=[_attn_specs(s, blk, width, qc, True), _attn_specs(s, blk, width, kc, False),
                  _attn_specs(s, blk, HEAD_DIM, vc, False)]
                 + ([pl.BlockSpec((1, blk, 2), lambda p, jj: (p, jj, 0))] if has_bias else [])
                 + [pl.BlockSpec((s, LANES), lambda p, jj: (0, p)), rows, rows],
        out_specs=[pl.BlockSpec((blk, 2 * width), lambda p, jj: (jj, p)), stat, stat,
                   pl.BlockSpec((s, 2 * width), lambda p, jj: (0, p))] + ([rows] if has_bias else []),
        out_shape=[jax.ShapeDtypeStruct((s, N_HEADS * width), out_dtype), jax.ShapeDtypeStruct((s, N_HEADS * HEAD_DIM), out_dtype),
                   jax.ShapeDtypeStruct((s, N_HEADS * HEAD_DIM), F32), jax.ShapeDtypeStruct((s, N_HEADS * width), F32)]
                  + ([jax.ShapeDtypeStruct((N_HEADS, nb, 1, blk), F32)] if has_bias else []),
        compiler_params=_params("parallel", "arbitrary"),
    )(*ins)


def _head_stat(t):
    return t[:, ::HEAD_DIM]


def _stat_rows(t16, blk):
    s = t16.shape[0]
    return t16.T.reshape(N_HEADS, s // blk, 1, blk)


def _attention_bwd(res, do, *, qc, kc, vc, width, dq_mult, dk_mult, out_dtype, name):
    q, k, v, bias, o, lse_row = res
    s = q.shape[0]
    blk = _blk(s, ATT_BLOCK)
    kb_col = None if bias is None else bias.reshape(s, N_HEADS // 2, 2).transpose(1, 0, 2)
    delta_row = _stat_rows(_head_stat(_attn_delta(o, do, name=name + "_delta")), blk)
    outs = _attn_bwd(q, k, v, kb_col, do, lse_row, delta_row, qc=qc, kc=kc, vc=vc, width=width, dq_mult=dq_mult,
                     dk_mult=dk_mult, out_dtype=out_dtype, name=name + "_bwd")
    dk, dv, dcol, dq = outs[:4]
    if bias is None:
        return dq, dk, dv, None
    return dq, dk, dv, outs[4].reshape(N_HEADS, s).T - _head_stat(dcol)


def _fox_gate_fwd(fl, bf, *, name):
    s, n = fl.shape
    bs = _blk(s, ROW_BLOCK)

    def body(fl_ref, bf_ref, cum_ref, carry_ref):
        @pl.when(pl.program_id(0) == 0)
        def _():
            carry_ref[...] = jnp.zeros_like(carry_ref)

        z = fl_ref[...] + bf_ref[...]
        lf = jnp.minimum(z, 0.0) - jnp.log1p(jnp.exp(-jnp.abs(z)))
        row = lax.broadcasted_iota(jnp.int32, (bs, bs), 0)
        col = lax.broadcasted_iota(jnp.int32, (bs, bs), 1)
        tri = (col <= row).astype(F32)
        cum_ref[...] = jnp.dot(tri, lf, preferred_element_type=F32, precision=HIGHEST) + carry_ref[...]
        carry_ref[...] += jnp.sum(lf, axis=0, keepdims=True)

    return pl.pallas_call(
        body, name=name, grid=(s // bs,),
        in_specs=[pl.BlockSpec((bs, n), lambda i: (i, 0)), pl.BlockSpec((1, n), lambda i: (0, 0))],
        out_specs=pl.BlockSpec((bs, n), lambda i: (i, 0)),
        out_shape=jax.ShapeDtypeStruct((s, n), F32), scratch_shapes=[pltpu.VMEM((1, n), F32)],
        compiler_params=_params("arbitrary"),
    )(fl, bf)


def _fox_gate_bwd(fl, bf, dcum, *, name):
    s, n = fl.shape
    bs = _blk(s, ROW_BLOCK)
    nb = s // bs

    def body(fl_ref, bf_ref, dc_ref, dz_ref, dbf_ref, carry_ref):
        @pl.when(pl.program_id(0) == 0)
        def _():
            carry_ref[...] = jnp.zeros_like(carry_ref)
            dbf_ref[...] = jnp.zeros_like(dbf_ref)

        dc = dc_ref[...]
        row = lax.broadcasted_iota(jnp.int32, (bs, bs), 0)
        col = lax.broadcasted_iota(jnp.int32, (bs, bs), 1)
        tri = (col >= row).astype(F32)
        dlf = jnp.dot(tri, dc, preferred_element_type=F32, precision=HIGHEST) + carry_ref[...]
        carry_ref[...] += jnp.sum(dc, axis=0, keepdims=True)
        z = fl_ref[...] + bf_ref[...]
        dz = dlf / (1.0 + jnp.exp(z))
        dz_ref[...] = dz
        dbf_ref[...] += jnp.sum(dz, axis=0, keepdims=True)

    rev = pl.BlockSpec((bs, n), lambda i: (nb - 1 - i, 0))
    vec = pl.BlockSpec((1, n), lambda i: (0, 0))
    return pl.pallas_call(
        body, name=name, grid=(nb,), in_specs=[rev, vec, rev], out_specs=[rev, vec],
        out_shape=[jax.ShapeDtypeStruct((s, n), F32), jax.ShapeDtypeStruct((1, n), F32)],
        scratch_shapes=[pltpu.VMEM((1, n), F32)], compiler_params=_params("arbitrary"),
    )(fl, bf, dcum)


def _rope(x1, x2, cos, sin, *, negate, name):
    s, n = x1.shape
    bs = _blk(s, ROW_BLOCK)

    def body(a_ref, b_ref, c_ref, s_ref, o1_ref, o2_ref):
        a, b, cv = a_ref[...], b_ref[...], c_ref[...]
        sv = -s_ref[...] if negate else s_ref[...]
        o1_ref[...] = a * cv - b * sv
        o2_ref[...] = b * cv + a * sv

    row = pl.BlockSpec((bs, n), lambda i: (i, 0))
    return pl.pallas_call(
        body, name=name, grid=(s // bs,), in_specs=[row] * 4, out_specs=[row, row],
        out_shape=[jax.ShapeDtypeStruct((s, n), F32)] * 2, compiler_params=_params("parallel"),
    )(x1, x2, cos, sin)


def _rope_heads(x, ta, tb, tc, *, out_dtype, name):
    s, n = x.shape
    bs = _blk(s, ROW_BLOCK)

    def body(x_ref, a_ref, b_ref, c_ref, o_ref):
        av, bv, cv = a_ref[...], b_ref[...], c_ref[...]
        for g in range(n // LANES):
            xg = x_ref[:, g * LANES:(g + 1) * LANES]
            og = xg * av + pltpu.roll(xg, LANES - MLA_ROPE_HALF, 1) * bv + pltpu.roll(xg, MLA_ROPE_HALF, 1) * cv
            o_ref[:, g * LANES:(g + 1) * LANES] = og.astype(out_dtype)

    row = pl.BlockSpec((bs, n), lambda i: (i, 0))
    tab = pl.BlockSpec((bs, LANES), lambda i: (i, 0))
    return pl.pallas_call(
        body, name=name, grid=(s // bs,), in_specs=[row, tab, tab, tab], out_specs=row,
        out_shape=jax.ShapeDtypeStruct((s, n), out_dtype), compiler_params=_params("parallel"),
    )(x, ta, tb, tc)


def _group_sum(x, *, name):
    s, n = x.shape
    bs = _blk(s, ROW_BLOCK)

    def body(x_ref, o_ref):
        acc = x_ref[:, 0:LANES]
        for g in range(1, n // LANES):
            acc = acc + x_ref[:, g * LANES:(g + 1) * LANES]
        o_ref[...] = acc

    return pl.pallas_call(
        body, name=name, grid=(s // bs,), in_specs=[pl.BlockSpec((bs, n), lambda i: (i, 0))],
        out_specs=pl.BlockSpec((bs, LANES), lambda i: (i, 0)),
        out_shape=jax.ShapeDtypeStruct((s, LANES), F32), compiler_params=_params("parallel"),
    )(x)


def _shift_down(x, k):
    return pltpu.roll(x, k, 0)


def _conv_rows(ext, w_ref, b_ref, rows):
    y = b_ref[...] + w_ref[0:1, :] * _shift_down(ext, 2) + w_ref[1:2, :] * _shift_down(ext, 1) + w_ref[2:3, :] * ext
    return y[8:8 + rows]


def _conv_gate_fwd(u, cw, cb, *, name):
    s, f2 = u.shape
    f = f2 // 2
    nf = f // LANES
    r = _blk(s, CONV_ROWS)
    r8 = r // 8

    def body(ug_ref, ugp_ref, uv_ref, uvp_ref, wg_ref, wv_ref, bg_ref, bv_ref, o_ref):
        first = pl.program_id(1) == 0

        def conv(cur_ref, prev_ref, w_ref, b_ref):
            prev = jnp.where(first, 0.0, prev_ref[...])
            return _conv_rows(jnp.concatenate([prev, cur_ref[...]], axis=0), w_ref, b_ref, r)

        yg = conv(ug_ref, ugp_ref, wg_ref, bg_ref)
        yv = conv(uv_ref, uvp_ref, wv_ref, bv_ref)
        o_ref[...] = (yg * jax.nn.sigmoid(yg) * yv).astype(BF16)

    def cur(off):
        return pl.BlockSpec((r, LANES), lambda c, i: (i, c + off))

    def prev(off):
        return pl.BlockSpec((8, LANES), lambda c, i: (jnp.maximum(i * r8 - 1, 0), c + off))

    def wspec(rows, off):
        return pl.BlockSpec((rows, LANES), lambda c, i: (0, c + off))

    return pl.pallas_call(
        body, name=name, grid=(nf, s // r),
        in_specs=[cur(0), prev(0), cur(nf), prev(nf), wspec(3, 0), wspec(3, nf), wspec(1, 0), wspec(1, nf)],
        out_specs=pl.BlockSpec((r, LANES), lambda c, i: (i, c)),
        out_shape=jax.ShapeDtypeStruct((s, f), BF16), compiler_params=_params("parallel", "parallel"),
    )(u, u, u, u, cw, cw, cb, cb)


def _conv_gate_bwd(u, cw, cb, dg, *, name):
    s, f2 = u.shape
    f = f2 // 2
    nf = f // LANES
    r = _blk(s, CONV_ROWS)
    r8 = r // 8
    nr = s // r

    def body(ug_ref, ugp_ref, ugn_ref, uv_ref, uvp_ref, uvn_ref, wg_ref, wv_ref, bg_ref, bv_ref, dg_ref, dgn_ref,
             dug_ref, duv_ref, dwg_ref, dwv_ref, dbg_ref, dbv_ref):
        i = pl.program_id(1)
        first, last = i == 0, i == nr - 1

        def ext_of(cur_ref, prev_ref, next_ref):
            prev = jnp.where(first, 0.0, prev_ref[...])
            return jnp.concatenate([prev, cur_ref[...], next_ref[...]], axis=0)

        eg, ev = ext_of(ug_ref, ugp_ref, ugn_ref), ext_of(uv_ref, uvp_ref, uvn_ref)
        yg = _conv_rows(eg, wg_ref, bg_ref, r + 8)
        yv = _conv_rows(ev, wv_ref, bv_ref, r + 8)
        dgn = jnp.where(last, 0.0, dgn_ref[...])
        dgx = jnp.concatenate([dg_ref[...], dgn], axis=0)
        sg = jax.nn.sigmoid(yg)
        dyg = dgx * yv * (sg * (1.0 + yg * (1.0 - sg)))
        dyv = dgx * (yg * sg)

        @pl.when(i == 0)
        def _():
            for ref in (dwg_ref, dwv_ref, dbg_ref, dbv_ref):
                ref[...] = jnp.zeros_like(ref)

        def grads(dy, ext, w_ref, du_ref, dw_ref, db_ref):
            n = r + 8
            du = w_ref[2:3, :] * dy + w_ref[1:2, :] * pltpu.roll(dy, n - 1, 0) + w_ref[0:1, :] * pltpu.roll(dy, n - 2, 0)
            du_ref[...] = du[0:r].astype(BF16)
            dyc = dy[0:r]
            db_ref[...] += jnp.sum(dyc, axis=0, keepdims=True)
            ext_c = ext[0:r + 8]
            dw_ref[0:1, :] += jnp.sum(dyc * _shift_down(ext_c, 2)[8:], axis=0, keepdims=True)
            dw_ref[1:2, :] += jnp.sum(dyc * _shift_down(ext_c, 1)[8:], axis=0, keepdims=True)
            dw_ref[2:3, :] += jnp.sum(dyc * ext_c[8:], axis=0, keepdims=True)

        grads(dyg, eg, wg_ref, dug_ref, dwg_ref, dbg_ref)
        grads(dyv, ev, wv_ref, duv_ref, dwv_ref, dbv_ref)

    def cur(off):
        return pl.BlockSpec((r, LANES), lambda c, i: (i, c + off))

    def prev(off):
        return pl.BlockSpec((8, LANES), lambda c, i: (jnp.maximum(i * r8 - 1, 0), c + off))

    def nxt(off):
        return pl.BlockSpec((8, LANES), lambda c, i: (jnp.minimum((i + 1) * r8, s // 8 - 1), c + off))

    def wspec(rows, off):
        return pl.BlockSpec((rows, LANES), lambda c, i: (0, c + off))

    outs = pl.pallas_call(
        body, name=name, grid=(nf, nr),
        in_specs=[cur(0), prev(0), nxt(0), cur(nf), prev(nf), nxt(nf), wspec(3, 0), wspec(3, nf), wspec(1, 0), wspec(1, nf),
                  cur(0), nxt(0)],
        out_specs=[cur(0), cur(0), wspec(3, 0), wspec(3, 0), wspec(1, 0), wspec(1, 0)],
        out_shape=[jax.ShapeDtypeStruct((s, f), BF16), jax.ShapeDtypeStruct((s, f), BF16),
                   jax.ShapeDtypeStruct((3, f), F32), jax.ShapeDtypeStruct((3, f), F32),
                   jax.ShapeDtypeStruct((1, f), F32), jax.ShapeDtypeStruct((1, f), F32)],
        compiler_params=_params("parallel", "arbitrary"),
    )(u, u, u, u, u, u, cw, cw, cb, cb, dg, dg)
    dug, duv, dwg, dwv, dbg, dbv = outs
    return jnp.concatenate([dug, duv], axis=1), jnp.concatenate([dwg, dwv], axis=1), jnp.concatenate([dbg, dbv], axis=1)


def _adamw(w, g, m, v, *, slabs, name):
    shape = w.shape
    cols = shape[-1]
    rows = w.size // cols
    w2, m2, v2 = (t.reshape(rows, cols) for t in (w, m, v))
    g2 = g.reshape((N_DEV, rows, cols) if slabs else (rows, cols))
    br = _row_blk(rows, ROW_BLOCK // 2 if slabs else ROW_BLOCK)

    def body(w_ref, g_ref, m_ref, v_ref, go_ref, d_ref, nm_ref, nv_ref):
        if slabs:
            gv = g_ref[0].astype(F32)
            for p in range(1, N_DEV):
                gv = gv + g_ref[p].astype(F32)
        else:
            gv = g_ref[...]
        nm = ADAM_B1 * m_ref[...] + (1.0 - ADAM_B1) * gv
        nv = ADAM_B2 * v_ref[...] + (1.0 - ADAM_B2) * (gv * gv)
        m_hat = nm / (1.0 - ADAM_B1 ** ADAM_STEP)
        v_hat = nv / (1.0 - ADAM_B2 ** ADAM_STEP)
        go_ref[...] = gv
        d_ref[...] = -ADAM_LR * (m_hat / (jnp.sqrt(v_hat) + ADAM_EPS) + ADAM_WD * w_ref[...])
        nm_ref[...] = nm
        nv_ref[...] = nv

    spec = pl.BlockSpec((br, cols), lambda i: (i, 0))
    gspec = pl.BlockSpec((N_DEV, br, cols), lambda i: (0, i, 0)) if slabs else spec
    outs = pl.pallas_call(
        body, name=name, grid=(rows // br,), in_specs=[spec, gspec, spec, spec], out_specs=[spec] * 4,
        out_shape=[jax.ShapeDtypeStruct((rows, cols), F32)] * 4, compiler_params=_params("parallel"),
    )(w2, g2, m2, v2)
    return tuple(t.reshape(shape) for t in outs)


def _exchange(xs, *, same_src, name):
    n = len(xs)
    slabs = [x.shape if same_src else x.shape[1:] for x in xs]

    def body(*refs):
        x_refs, o_refs = refs[:n], refs[n:2 * n]
        send_sems, recv_sems, loc_sems = refs[2 * n:]
        ix, iy, ic = lax.axis_index("x"), lax.axis_index("y"), lax.axis_index("c")
        me = 4 * ix + 2 * iy + ic
        local, sends, recvs = [], [], []
        for a in range(n):
            def src(p, a=a):
                return x_refs[a] if same_src else x_refs[a].at[p]

            local.append(pltpu.make_async_copy(src(me), o_refs[a].at[me], loc_sems.at[a]))
            for k in (1, 2, 4, 3, 5, 6, 7):
                px = 1 - ix if k & 4 else ix
                py = 1 - iy if k & 2 else iy
                pc = 1 - ic if k & 1 else ic
                p = 4 * px + 2 * py + pc
                for dst, out in ((me, sends), (p, recvs)):
                    out.append(pltpu.make_async_remote_copy(
                        src_ref=src(p), dst_ref=o_refs[a].at[dst], send_sem=send_sems.at[a, k - 1],
                        recv_sem=recv_sems.at[a, k - 1], device_id=(px, py, pc), device_id_type=pl.DeviceIdType.MESH))
        for cp in local + sends:
            cp.start()
        for cp in recvs:
            cp.wait_recv()
        for cp in sends:
            cp.wait_send()
        for cp in local:
            cp.wait()

    return pl.pallas_call(
        body, name=name,
        in_specs=[pl.BlockSpec(memory_space=pl.ANY)] * n, out_specs=[pl.BlockSpec(memory_space=pl.ANY)] * n,
        out_shape=[jax.ShapeDtypeStruct((N_DEV,) + tuple(sl), x.dtype) for sl, x in zip(slabs, xs)],
        scratch_shapes=[pltpu.SemaphoreType.DMA((n, N_DEV - 1)), pltpu.SemaphoreType.DMA((n, N_DEV - 1)),
                        pltpu.SemaphoreType.DMA((n,))],
        compiler_params=pltpu.CompilerParams(has_side_effects=True, vmem_limit_bytes=VMEM_LIMIT_BYTES),
    )(*xs)


def _gather_two_level(xs, *, name):
    n = len(xs)

    def body(*refs):
        x_refs, o_refs = refs[:n], refs[n:2 * n]
        send_sems, recv_sems, loc_sems = refs[2 * n:]
        ix, iy, ic = lax.axis_index("x"), lax.axis_index("y"), lax.axis_index("c")
        me = 4 * ix + 2 * iy + ic
        sib = 4 * ix + 2 * iy + (1 - ic)
        chips = [(1 - ix if ch & 2 else ix, 1 - iy if ch & 1 else iy) for ch in (1, 2, 3)]

        def copy(a, pos, src, slot, to):
            return pltpu.make_async_remote_copy(
                src_ref=src, dst_ref=o_refs[a].at[slot], send_sem=send_sems.at[a, pos], recv_sem=recv_sems.at[a, pos],
                device_id=to, device_id_type=pl.DeviceIdType.MESH)

        local = [pltpu.make_async_copy(x_refs[a], o_refs[a].at[me], loc_sems.at[a]) for a in range(n)]
        first, passed, arrive = [], [], []
        for a in range(n):
            first.append(copy(a, 0, x_refs[a], me, (ix, iy, 1 - ic)))
            arrive.append(copy(a, 0, x_refs[a], sib, (ix, iy, 1 - ic)))
            for ch, (px, py) in enumerate(chips, start=1):
                same, other = 4 * px + 2 * py + ic, 4 * px + 2 * py + (1 - ic)
                first.append(copy(a, 2 * ch - 1, x_refs[a], me, (px, py, ic)))
                passed.append((copy(a, 2 * ch - 1, x_refs[a], same, (px, py, ic)),
                               copy(a, 2 * ch, o_refs[a].at[same], same, (ix, iy, 1 - ic))))
                arrive.append(copy(a, 2 * ch, x_refs[a], other, (ix, iy, 1 - ic)))
        for cp in local + first:
            cp.start()
        for landed, onward in passed:
            landed.wait_recv()
            onward.start()
        for cp in arrive:
            cp.wait_recv()
        for cp in first + [onward for _, onward in passed]:
            cp.wait_send()
        for cp in local:
            cp.wait()

    return pl.pallas_call(
        body, name=name,
        in_specs=[pl.BlockSpec(memory_space=pl.ANY)] * n, out_specs=[pl.BlockSpec(memory_space=pl.ANY)] * n,
        out_shape=[jax.ShapeDtypeStruct((N_DEV,) + tuple(x.shape), x.dtype) for x in xs],
        scratch_shapes=[pltpu.SemaphoreType.DMA((n, N_DEV - 1)), pltpu.SemaphoreType.DMA((n, N_DEV - 1)),
                        pltpu.SemaphoreType.DMA((n,))],
        compiler_params=pltpu.CompilerParams(has_side_effects=True, vmem_limit_bytes=VMEM_LIMIT_BYTES),
    )(*xs)


def _sum_slabs(x, *, name):
    n, r, c = x.shape
    br = _row_blk(r, ROW_BLOCK)

    def body(x_ref, o_ref):
        acc = x_ref[0]
        for p in range(1, n):
            acc = acc + x_ref[p]
        o_ref[...] = acc

    return pl.pallas_call(
        body, name=name, grid=(r // br,), in_specs=[pl.BlockSpec((n, br, c), lambda i: (0, i, 0))],
        out_specs=pl.BlockSpec((br, c), lambda i: (i, 0)),
        out_shape=jax.ShapeDtypeStruct((r, c), F32), compiler_params=_params("parallel"),
    )(x)


def _silu(x, *, name):
    def body(x_ref, o_ref):
        xv = x_ref[...]
        o_ref[...] = (xv * jax.nn.sigmoid(xv)).astype(BF16)

    return pl.pallas_call(body, name=name, out_shape=jax.ShapeDtypeStruct(x.shape, BF16),
                          compiler_params=_params())(x)


_BIG = {"fox_w_in": 2, "fox_w_o": 1, "mla_w_a": 1, "mla_w_uq": 2, "mla_w_ukv": 2, "mla_w_o": 1, "ffn_w_in": 2, "ffn_w_out": 1}
_SMALL = {"mla_g_q": 1, "mla_g_kv": 1, "ffn_conv_w": 2}
_REPL = ("fox_b_f", "ffn_conv_b", "final_g")


def _gathered_to_full(g, axis):
    full = jnp.moveaxis(g, 0, axis)
    shape = list(full.shape)
    shape[axis:axis + 2] = [shape[axis] * shape[axis + 1]]
    return full.reshape(shape)


def _full_to_chunks(full, axis):
    shape = list(full.shape)
    shape[axis:axis + 1] = [N_DEV, shape[axis] // N_DEV]
    return jnp.moveaxis(full.reshape(shape), axis, 0)


def _per_head(parts, s_or_rows):
    return jnp.concatenate([p.reshape(s_or_rows, N_HEADS, -1) for p in parts], axis=-1).reshape(s_or_rows, -1)


def kernel(x, c, ada_w, ada_b, fox_w_in, fox_b_f, fox_w_o, mla_w_a, mla_g_q, mla_g_kv, mla_w_uq, mla_w_ukv, mla_w_o, ffn_w_in, ffn_conv_w, ffn_conv_b, ffn_w_out, final_g, loss_target, m_ada_w, m_ada_b, m_fox_w_in, m_fox_b_f, m_fox_w_o, m_mla_w_a, m_mla_g_q, m_mla_g_kv, m_mla_w_uq, m_mla_w_ukv, m_mla_w_o, m_ffn_w_in, m_ffn_conv_w, m_ffn_conv_b, m_ffn_w_out, m_final_g, v_ada_w, v_ada_b, v_fox_w_in, v_fox_b_f, v_fox_w_o, v_mla_w_a, v_mla_g_q, v_mla_g_kv, v_mla_w_uq, v_mla_w_ukv, v_mla_w_o, v_ffn_w_in, v_ffn_conv_w, v_ffn_conv_b, v_ffn_w_out, v_final_g):
    weights = dict(ada_w=ada_w, ada_b=ada_b, fox_w_in=fox_w_in, fox_b_f=fox_b_f, fox_w_o=fox_w_o, mla_w_a=mla_w_a,
                   mla_g_q=mla_g_q, mla_g_kv=mla_g_kv, mla_w_uq=mla_w_uq, mla_w_ukv=mla_w_ukv, mla_w_o=mla_w_o,
                   ffn_w_in=ffn_w_in, ffn_conv_w=ffn_conv_w, ffn_conv_b=ffn_conv_b, ffn_w_out=ffn_w_out, final_g=final_g)
    mom_m = dict(ada_w=m_ada_w, ada_b=m_ada_b, fox_w_in=m_fox_w_in, fox_b_f=m_fox_b_f, fox_w_o=m_fox_w_o, mla_w_a=m_mla_w_a,
                 mla_g_q=m_mla_g_q, mla_g_kv=m_mla_g_kv, mla_w_uq=m_mla_w_uq, mla_w_ukv=m_mla_w_ukv, mla_w_o=m_mla_w_o,
                 ffn_w_in=m_ffn_w_in, ffn_conv_w=m_ffn_conv_w, ffn_conv_b=m_ffn_conv_b, ffn_w_out=m_ffn_w_out, final_g=m_final_g)
    mom_v = dict(ada_w=v_ada_w, ada_b=v_ada_b, fox_w_in=v_fox_w_in, fox_b_f=v_fox_b_f, fox_w_o=v_fox_w_o, mla_w_a=v_mla_w_a,
                 mla_g_q=v_mla_g_q, mla_g_kv=v_mla_g_kv, mla_w_uq=v_mla_w_uq, mla_w_ukv=v_mla_w_ukv, mla_w_o=v_mla_w_o,
                 ffn_w_in=v_ffn_w_in, ffn_conv_w=v_ffn_conv_w, ffn_conv_b=v_ffn_conv_b, ffn_w_out=v_ffn_w_out, final_g=v_final_g)
    order = list(weights)
    x0 = x[0]
    target = loss_target[0]
    s = x0.shape[0]
    d = D_MODEL
    cols = ada_w.shape[-1]
    nq = N_HEADS * HEAD_DIM

    small_names = ["c"] + list(_SMALL)
    small_all = dict(zip(small_names, _exchange([c] + [weights[n] for n in _SMALL], same_src=True, name="gather_small")))
    c_all = small_all["c"].reshape(N_DEV, d)
    g_q = _gathered_to_full(small_all["mla_g_q"], 1)
    g_kv = _gathered_to_full(small_all["mla_g_kv"], 1)
    conv_w = _gathered_to_full(small_all["ffn_conv_w"], 2)

    c_pad = _pad_axis(c_all, 0, LANES)
    silu_c = _silu(c_pad, name="silu_c")
    w_ada = ada_w.reshape(4, d, cols)
    b_ada = ada_b.reshape(4, 1, cols)
    mods = [_matmul(silu_c, w_ada[i], name=f"ada_mod{i}")[:N_DEV] + b_ada[i] for i in range(4)]
    mod_send = _pad_axis(jnp.stack(mods, axis=1), 1, 8)
    mod_recv, = _exchange([mod_send], same_src=False, name="scatter_mod")
    mod = mod_recv[:, :4].transpose(1, 0, 2).reshape(4, 3 * d)
    shift = [mod[i:i + 1, 0:d] for i in range(4)]
    scale = [mod[i:i + 1, d:2 * d] for i in range(4)]
    gate = [mod[i:i + 1, 2 * d:3 * d] for i in range(4)]

    big_all = _gather_two_level([weights[n].astype(BF16) for n in _BIG], name="gather_weights")
    wfull = {n: _gathered_to_full(g, _BIG[n]) for n, g in zip(_BIG, big_all)}

    w_fox_in = _pad_axis(wfull["fox_w_in"][0], 1, LANES)
    w_fox_qkv, w_fox_f = w_fox_in[:, :3 * nq], w_fox_in[:, 3 * nq:]
    w_fox_o = wfull["fox_w_o"][0]
    w_a = _pad_axis(wfull["mla_w_a"][0], 1, LANES)
    wq = wfull["mla_w_uq"][0].reshape(MLA_Q_RANK, N_HEADS, HEAD_DIM + MLA_ROPE_DIM)
    w_uq = _pad_axis(wq, 2, LANES).reshape(MLA_Q_RANK, N_HEADS * LANES)
    wkv = wfull["mla_w_ukv"][0].reshape(MLA_KV_RANK, N_HEADS, 2 * HEAD_DIM)
    w_ukv = jnp.concatenate([wkv[:, :, :HEAD_DIM].reshape(MLA_KV_RANK, -1), wkv[:, :, HEAD_DIM:].reshape(MLA_KV_RANK, -1)], axis=1)
    w_mla_o = wfull["mla_w_o"][0]
    w_ffn_in = wfull["ffn_w_in"]
    w_ffn_out = wfull["ffn_w_out"]
    conv_b = ffn_conv_b

    fox_scale = HEAD_DIM ** -0.5
    mla_scale = (HEAD_DIM + MLA_ROPE_DIM) ** -0.5
    pos = jnp.arange(s, dtype=F32)
    inv_freq = ROPE_BASE ** (-jnp.arange(0, MLA_ROPE_DIM, 2, dtype=F32) / MLA_ROPE_DIM)
    ang = pos[:, None] * inv_freq[None, :]
    cos16, sin16 = jnp.cos(ang), jnp.sin(ang)
    z16, z32, z64 = jnp.zeros((s, 16), F32), jnp.zeros((s, 32), F32), jnp.zeros((s, 64), F32)
    tab_a = jnp.concatenate([jnp.ones((s, 64), F32), cos16, cos16, z32], axis=1) * (mla_scale * LOG2E)
    tab_b = jnp.concatenate([z64, -sin16, z16, z32], axis=1) * (mla_scale * LOG2E)
    tab_c = jnp.concatenate([z64, z16, sin16, z32], axis=1) * (mla_scale * LOG2E)

    h0 = _norm_fwd(x0, scale[0], shift[0], plus_one=True, out_dtype=BF16, name="ada_fwd0")
    q_mult = jnp.concatenate([jnp.full((1, nq), fox_scale * LOG2E, F32), jnp.ones((1, 2 * nq), F32)], axis=1)
    qkv = _matmul(h0, w_fox_qkv, out_dtype=BF16, col_scale=q_mult, name="fox_proj")
    fl = _matmul(h0, w_fox_f, name="fox_proj_f")[:, :N_HEADS]
    cum = _fox_gate_fwd(fl, fox_b_f, name="fox_gate_fwd")
    fox_cfg = dict(qc=0, kc=N_HEADS // 2, vc=N_HEADS, width=HEAD_DIM)
    kb2 = cum * -LOG2E
    att_blk = _blk(s, ATT_BLOCK)
    fvt = _matmul(w_fox_qkv[:, 2 * nq:], h0, ta=True, tb=True, out_dtype=BF16, name="fox_proj_vt")
    fo, fox_lse = _attn_fwd(qkv, qkv, _key_blocks(fvt, att_blk), _bias_lane_terms(kb2), qc=0, kc=N_HEADS // 2,
                            width=HEAD_DIM, name="fox_attn_fwd")
    fox_res = (qkv, qkv, qkv, kb2, fo, fox_lse)
    y0 = _matmul(fo, w_fox_o, name="fox_out")
    x1 = _resid_fwd(x0, y0, gate[0], name="resid_fwd0")

    def ffn_fwd(xin, li, sub):
        hh = _norm_fwd(xin, scale[sub], shift[sub], plus_one=True, out_dtype=BF16, name=f"ada_fwd{sub}")
        u = _matmul(hh, w_ffn_in[li], name=f"ffn_up{li}")
        g = _conv_gate_fwd(u, conv_w[li], conv_b[li:li + 1], name=f"conv_fwd{li}")
        y = _matmul(g, w_ffn_out[li], name=f"ffn_down{li}")
        return _resid_fwd(xin, y, gate[sub], name=f"resid_fwd{sub}"), (hh, u, g, y)

    x2, ffn0_res = ffn_fwd(x1, 0, 1)

    h2 = _norm_fwd(x2, scale[2], shift[2], plus_one=True, out_dtype=BF16, name="ada_fwd2")
    a = _matmul(h2, w_a, name="mla_a")
    a_q, a_kv = a[:, :MLA_Q_RANK], a[:, MLA_Q_RANK:MLA_Q_RANK + MLA_KV_RANK]
    kr1 = a[:, MLA_Q_RANK + MLA_KV_RANK:MLA_Q_RANK + MLA_KV_RANK + MLA_ROPE_HALF]
    kr2 = a[:, MLA_Q_RANK + MLA_KV_RANK + MLA_ROPE_HALF:MLA_Q_RANK + MLA_KV_RANK + MLA_ROPE_DIM]
    cq = _norm_fwd(a_q, g_q, jnp.zeros_like(g_q), plus_one=False, out_dtype=BF16, name="mla_norm_q")
    ckv = _norm_fwd(a_kv, g_kv, jnp.zeros_like(g_kv), plus_one=False, out_dtype=BF16, name="mla_norm_kv")
    qf = _matmul(cq, w_uq, name="mla_uq")
    kvf = _matmul(ckv, w_ukv, out_dtype=BF16, name="mla_ukv")
    mq = _rope_heads(qf, tab_a, tab_b, tab_c, out_dtype=BF16, name="rope_q")
    kk1, kk2 = _rope(kr1, kr2, cos16, sin16, negate=False, name="rope_k")
    k_tail = jnp.concatenate([kk1, kk2, z32], axis=1).astype(BF16)
    mk = jnp.concatenate([kvf[:, :nq].reshape(s, N_HEADS, HEAD_DIM),
                          jnp.broadcast_to(k_tail[:, None, :], (s, N_HEADS, HEAD_DIM))], axis=-1).reshape(s, N_HEADS * LANES)
    mla_cfg = dict(qc=0, kc=0, vc=N_HEADS // 2, width=LANES)
    mvt = _matmul(w_ukv[:, nq:], ckv, ta=True, tb=True, out_dtype=BF16, name="mla_ukv_vt")
    mo, mla_lse = _attn_fwd(mq, mk, _key_blocks(mvt, att_blk), None, qc=0, kc=0, width=LANES, name="mla_attn_fwd")
    mla_res = (mq, mk, kvf, None, mo, mla_lse)
    y2 = _matmul(mo, w_mla_o, name="mla_out")
    x3 = _resid_fwd(x2, y2, gate[2], name="resid_fwd2")

    x4, ffn1_res = ffn_fwd(x3, 1, 3)

    loss_vec, dx4, d_final_g = _final_loss(x4, final_g.reshape(1, d), target, name="final_loss")
    loss = lax.psum(loss_vec[0, 0], ("x", "y", "c"))

    grads = {}
    dmod = [None] * 4

    def ffn_bwd(dx_out, xin, li, sub, res):
        hh, u, g, y = res
        dy, dgate = _resid_bwd(dx_out, y, gate[sub], name=f"resid_bwd{sub}")
        gw_out = _matmul(g, dy, ta=True, out_dtype=BF16, name=f"ffn_down_dw{li}")
        dg = _matmul(dy, w_ffn_out[li], tb=True, name=f"ffn_down_dx{li}")
        du, dcw, dcb = _conv_gate_bwd(u, conv_w[li], conv_b[li:li + 1], dg, name=f"conv_bwd{li}")
        gw_in = _matmul(hh, du, ta=True, out_dtype=BF16, name=f"ffn_up_dw{li}")
        dh = _matmul(du, w_ffn_in[li], tb=True, out_dtype=BF16, name=f"ffn_up_dx{li}")
        dx_in, dscale, dshift = _norm_bwd(xin, scale[sub], dh, dx_out, plus_one=True, name=f"ada_bwd{sub}")
        dmod[sub] = jnp.concatenate([dshift, dscale, dgate], axis=1)
        return dx_in, gw_in, dcw, dcb, gw_out

    dx3, gw_in1, dcw1, dcb1, gw_out1 = ffn_bwd(dx4, x3, 1, 3, ffn1_res)

    dy2, dgate2 = _resid_bwd(dx3, y2, gate[2], name="resid_bwd2")
    grads["mla_w_o"] = _matmul(mo, dy2, ta=True, out_dtype=BF16, name="mla_out_dw")[None]
    dmo = _matmul(dy2, w_mla_o, tb=True, out_dtype=BF16, name="mla_out_dx")
    dmq, dmk, dmv, _ = _attention_bwd(mla_res, dmo, dq_mult=1.0 / LOG2E, dk_mult=1.0 / LOG2E, out_dtype=F32,
                                      name="mla_attn", **mla_cfg)
    dqf = _rope_heads(dmq, tab_a, -tab_b, -tab_c, out_dtype=BF16, name="rope_q_bwd")
    g_uq = _matmul(cq, dqf, ta=True, out_dtype=BF16, name="mla_uq_dw")
    dcq = _matmul(dqf, w_uq, tb=True, name="mla_uq_dx")
    dmk3 = dmk.reshape(s, N_HEADS, LANES)
    dkr = _group_sum(dmk, name="mla_krope_sum")
    dkr1, dkr2 = _rope(dkr[:, HEAD_DIM:HEAD_DIM + MLA_ROPE_HALF], dkr[:, HEAD_DIM + MLA_ROPE_HALF:HEAD_DIM + MLA_ROPE_DIM],
                       cos16, sin16, negate=True, name="rope_k_bwd")
    dkvf = jnp.concatenate([dmk3[:, :, :HEAD_DIM].reshape(s, nq).astype(BF16), dmv.astype(BF16)], axis=1)
    g_ukv = _matmul(ckv, dkvf, ta=True, out_dtype=BF16, name="mla_ukv_dw")
    dckv = _matmul(dkvf, w_ukv, tb=True, name="mla_ukv_dx")
    da_q, dg_q, _ = _norm_bwd(a_q, g_q, dcq, None, plus_one=False, name="mla_norm_q_bwd")
    da_kv, dg_kv, _ = _norm_bwd(a_kv, g_kv, dckv, None, plus_one=False, name="mla_norm_kv_bwd")
    da = jnp.concatenate([da_q, da_kv, dkr1, dkr2, jnp.zeros((s, w_a.shape[1] - 672), F32)], axis=1).astype(BF16)
    grads["mla_w_a"] = _matmul(h2, da, ta=True, out_dtype=BF16, name="mla_a_dw")[None, :, :672]
    dh2 = _matmul(da, w_a, tb=True, out_dtype=BF16, name="mla_a_dx")
    dx2, dscale2, dshift2 = _norm_bwd(x2, scale[2], dh2, dx3, plus_one=True, name="ada_bwd2")
    dmod[2] = jnp.concatenate([dshift2, dscale2, dgate2], axis=1)
    grads["mla_w_uq"] = g_uq.reshape(MLA_Q_RANK, N_HEADS, LANES)[:, :, :HEAD_DIM + MLA_ROPE_DIM].reshape(1, MLA_Q_RANK, -1)
    grads["mla_w_ukv"] = _per_head([g_ukv[:, :nq], g_ukv[:, nq:]], MLA_KV_RANK)[None]
    grads["mla_g_q"], grads["mla_g_kv"] = dg_q, dg_kv

    dx1, gw_in0, dcw0, dcb0, gw_out0 = ffn_bwd(dx2, x1, 0, 1, ffn0_res)
    grads["ffn_w_in"] = jnp.stack([gw_in0, gw_in1])
    grads["ffn_w_out"] = jnp.stack([gw_out0, gw_out1])
    grads["ffn_conv_w"] = jnp.stack([dcw0, dcw1])
    g_conv_b = jnp.concatenate([dcb0, dcb1], axis=0)

    dy0, dgate0 = _resid_bwd(dx1, y0, gate[0], name="resid_bwd0")
    grads["fox_w_o"] = _matmul(fo, dy0, ta=True, out_dtype=BF16, name="fox_out_dw")[None]
    dfo = _matmul(dy0, w_fox_o, tb=True, out_dtype=BF16, name="fox_out_dx")
    dfq, dfk, dfv, dcum = _attention_bwd(fox_res, dfo, dq_mult=fox_scale, dk_mult=1.0 / LOG2E, out_dtype=BF16,
                                         name="fox_attn", **fox_cfg)
    dfl, g_b_f = _fox_gate_bwd(fl, fox_b_f, dcum, name="fox_gate_bwd")
    dproj = jnp.concatenate([dfq.astype(BF16), dfk, dfv, _pad_axis(dfl, 1, LANES).astype(BF16)], axis=1)
    grads["fox_w_in"] = _matmul(h0, dproj, ta=True, out_dtype=BF16, name="fox_proj_dw")[None, :, :3 * nq + N_HEADS]
    dh0 = _matmul(dproj, w_fox_in, tb=True, out_dtype=BF16, name="fox_proj_dx")
    dx0, dscale0, dshift0 = _norm_bwd(x0, scale[0], dh0, dx1, plus_one=True, name="ada_bwd0")
    dmod[0] = jnp.concatenate([dshift0, dscale0, dgate0], axis=1)

    dmod_send = _pad_axis(jnp.stack(dmod, axis=0).reshape(4, N_DEV, cols).transpose(1, 0, 2), 1, 8)
    dmod_recv, = _exchange([dmod_send], same_src=False, name="scatter_dmod")
    dmod_all = dmod_recv[:, :4]
    dmod_pad = _pad_axis(dmod_all, 0, LANES)
    g_ada_w = jnp.stack([_matmul(silu_c, dmod_pad[:, i], ta=True, name=f"ada_dw{i}") for i in range(4)])
    grads["ada_w"] = g_ada_w.reshape(ada_w.shape)
    grads["ada_b"] = _sum_slabs(dmod_recv, name="ada_db")[:4].reshape(ada_b.shape)

    sharded = list(_BIG) + list(_SMALL)
    axes = {**_BIG, **_SMALL}
    recv = _exchange([_full_to_chunks(grads[n], axes[n]) for n in sharded], same_src=False, name="scatter_grads")
    grads.update(dict(zip(sharded, recv)))
    repl = _exchange([g_b_f, g_conv_b, d_final_g], same_src=True, name="gather_repl_grads")
    grads.update(dict(zip(_REPL, repl)))

    grad_out, deltas, new_m, new_v = {}, {}, {}, {}
    for n in order:
        grad_out[n], deltas[n], new_m[n], new_v[n] = _adamw(
            weights[n], grads[n], mom_m[n], mom_v[n], slabs=n in axes or n in _REPL, name=f"adamw_{n}")

    grad_x = dx0[None]
    return (loss, grad_x, *[grad_out[n] for n in order], *[deltas[n] for n in order],
            *[new_m[n] for n in order], *[new_v[n] for n in order])
```

```python
import jax
import jax.numpy as jnp
from jax import lax
from jax.experimental import pallas as pl
from jax.experimental.pallas import tpu as pltpu

F32 = jnp.float32
BF16 = jnp.bfloat16
HIGHEST = lax.Precision.HIGHEST

N_DEV = 8
D_MODEL = 1024
N_HEADS = 16
HEAD_DIM = 64
MLA_ROPE_HALF = 16
MLA_Q_RANK = 384
MLA_KV_RANK = 256
MLA_ROPE_DIM = 32
NORM_EPS = 1e-6
ROPE_BASE = 10000.0
ADAM_LR = 0.001
ADAM_B1 = 0.9
ADAM_B2 = 0.999
ADAM_EPS = 1e-08
ADAM_WD = 0.01
ADAM_STEP = 10

LANES = 128
VMEM_LIMIT_BYTES = 56 * 1024 * 1024
ROW_BLOCK = 512
ATT_BLOCK = 512
CONV_ROWS = 2048
MM_BM, MM_BN, MM_BK = 512, 1408, 2048
MM_K_WHOLE = 3328
LOG2E = 1.4426950408889634


def _params(*sem):
    return pltpu.CompilerParams(dimension_semantics=sem or None, vmem_limit_bytes=VMEM_LIMIT_BYTES)


def _blk(dim, pref):
    if dim <= pref:
        return dim
    b = pref - pref % LANES
    while b >= LANES:
        if dim % b == 0:
            return b
        b -= LANES
    raise ValueError(f"no block for {dim}")


def _row_blk(rows, pref):
    if rows <= pref:
        return rows
    for b in range(pref - pref % 8, 7, -8):
        if rows % b == 0:
            return b
    return rows


def _pad_axis(a, axis, mult):
    pad = (-a.shape[axis]) % mult
    if pad == 0:
        return a
    widths = [(0, 0)] * a.ndim
    widths[axis] = (0, pad)
    return jnp.pad(a, widths)


def _matmul(a, b, *, ta=False, tb=False, out_dtype=F32, col_scale=None, name):
    m, k = (a.shape[1], a.shape[0]) if ta else a.shape
    n = b.shape[0] if tb else b.shape[1]
    assert (b.shape[1] if tb else b.shape[0]) == k, (a.shape, b.shape, ta, tb)
    bm, bn = _blk(m, 2 * MM_BM), _blk(n, MM_BN)
    if ta and bm < MM_BM and m % MM_BN == 0:
        bm = MM_BN
    bk = k if k <= MM_K_WHOLE else _blk(k, MM_BK)
    nk = k // bk
    dims = (((0 if ta else 1,), (1 if tb else 0,)), ((), ()))
    has_scale = col_scale is not None
    use_acc = nk > 1 and (out_dtype != F32 or has_scale)

    def body(*refs):
        a_ref, b_ref = refs[0], refs[1]
        s_ref = refs[2] if has_scale else None
        o_ref = refs[3] if has_scale else refs[2]
        acc_ref = refs[-1] if use_acc else o_ref
        kk = pl.program_id(2)
        part = lax.dot_general(a_ref[...].astype(BF16), b_ref[...].astype(BF16), dims, preferred_element_type=F32)

        def finish(val):
            if has_scale:
                val = val * s_ref[...]
            o_ref[...] = val.astype(out_dtype)

        if nk == 1:
            finish(part)
            return

        @pl.when(kk == 0)
        def _():
            acc_ref[...] = part

        @pl.when(kk > 0)
        def _():
            acc_ref[...] += part

        if use_acc:
            @pl.when(kk == nk - 1)
            def _():
                finish(acc_ref[...])

    a_spec = pl.BlockSpec((bk, bm), lambda i, j, kk: (kk, i)) if ta else pl.BlockSpec((bm, bk), lambda i, j, kk: (i, kk))
    b_spec = pl.BlockSpec((bn, bk), lambda i, j, kk: (j, kk)) if tb else pl.BlockSpec((bk, bn), lambda i, j, kk: (kk, j))
    return pl.pallas_call(
        body, name=name, grid=(m // bm, n // bn, nk),
        in_specs=[a_spec, b_spec] + ([pl.BlockSpec((1, bn), lambda i, j, kk: (0, j))] if has_scale else []),
        out_specs=pl.BlockSpec((bm, bn), lambda i, j, kk: (i, j)),
        out_shape=jax.ShapeDtypeStruct((m, n), out_dtype),
        scratch_shapes=[pltpu.VMEM((bm, bn), F32)] if use_acc else [],
        compiler_params=_params("parallel", "parallel", "arbitrary"),
    )(*([a, b] + ([col_scale] if has_scale else [])))


def _norm_fwd(x, mul, add, *, plus_one, out_dtype, name):
    s, n = x.shape
    bs = _blk(s, ROW_BLOCK)

    def body(x_ref, m_ref, a_ref, o_ref):
        xv = x_ref[...]
        r = lax.rsqrt(jnp.mean(xv * xv, axis=-1, keepdims=True) + NORM_EPS)
        mv = m_ref[...] + 1.0 if plus_one else m_ref[...]
        o_ref[...] = (xv * r * mv + a_ref[...]).astype(out_dtype)

    row = pl.BlockSpec((bs, n), lambda i: (i, 0))
    vec = pl.BlockSpec((1, n), lambda i: (0, 0))
    return pl.pallas_call(
        body, name=name, grid=(s // bs,), in_specs=[row, vec, vec], out_specs=row,
        out_shape=jax.ShapeDtypeStruct((s, n), out_dtype), compiler_params=_params("parallel"),
    )(x, mul, add)


def _norm_bwd(x, mul, dy, dres, *, plus_one, name):
    s, n = x.shape
    bs = _blk(s, ROW_BLOCK)
    has_res = dres is not None

    def body(*refs):
        if has_res:
            x_ref, m_ref, dy_ref, dres_ref, dx_ref, dm_ref, da_ref = refs
        else:
            x_ref, m_ref, dy_ref, dx_ref, dm_ref, da_ref = refs
        xv = x_ref[...]
        dyv = dy_ref[...].astype(F32)
        r = lax.rsqrt(jnp.mean(xv * xv, axis=-1, keepdims=True) + NORM_EPS)
        xn = xv * r
        mv = m_ref[...] + 1.0 if plus_one else m_ref[...]
        g = dyv * mv
        dx = r * (g - xn * jnp.mean(g * xn, axis=-1, keepdims=True))
        if has_res:
            dx = dx + dres_ref[...]
        dx_ref[...] = dx

        @pl.when(pl.program_id(0) == 0)
        def _():
            dm_ref[...] = jnp.zeros_like(dm_ref)
            da_ref[...] = jnp.zeros_like(da_ref)

        dm_ref[...] += jnp.sum(dyv * xn, axis=0, keepdims=True)
        da_ref[...] += jnp.sum(dyv, axis=0, keepdims=True)

    row = pl.BlockSpec((bs, n), lambda i: (i, 0))
    vec = pl.BlockSpec((1, n), lambda i: (0, 0))
    ins = [x, mul, dy] + ([dres] if has_res else [])
    return pl.pallas_call(
        body, name=name, grid=(s // bs,),
        in_specs=[row, vec, row] + ([row] if has_res else []), out_specs=[row, vec, vec],
        out_shape=[jax.ShapeDtypeStruct((s, n), F32), jax.ShapeDtypeStruct((1, n), F32), jax.ShapeDtypeStruct((1, n), F32)],
        compiler_params=_params("arbitrary"),
    )(*ins)


def _resid_fwd(x, y, gate, *, name):
    s, n = x.shape
    bs = _blk(s, ROW_BLOCK)

    def body(x_ref, y_ref, g_ref, o_ref):
        o_ref[...] = x_ref[...] + g_ref[...] * y_ref[...]

    row = pl.BlockSpec((bs, n), lambda i: (i, 0))
    vec = pl.BlockSpec((1, n), lambda i: (0, 0))
    return pl.pallas_call(
        body, name=name, grid=(s // bs,), in_specs=[row, row, vec], out_specs=row,
        out_shape=jax.ShapeDtypeStruct((s, n), F32), compiler_params=_params("parallel"),
    )(x, y, gate)


def _resid_bwd(dx, y, gate, *, name):
    s, n = dx.shape
    bs = _blk(s, ROW_BLOCK)

    def body(dx_ref, y_ref, g_ref, dy_ref, dg_ref):
        dxv = dx_ref[...]
        dy_ref[...] = (g_ref[...] * dxv).astype(BF16)

        @pl.when(pl.program_id(0) == 0)
        def _():
            dg_ref[...] = jnp.zeros_like(dg_ref)

        dg_ref[...] += jnp.sum(dxv * y_ref[...], axis=0, keepdims=True)

    row = pl.BlockSpec((bs, n), lambda i: (i, 0))
    vec = pl.BlockSpec((1, n), lambda i: (0, 0))
    return pl.pallas_call(
        body, name=name, grid=(s // bs,), in_specs=[row, row, vec], out_specs=[row, vec],
        out_shape=[jax.ShapeDtypeStruct((s, n), BF16), jax.ShapeDtypeStruct((1, n), F32)],
        compiler_params=_params("arbitrary"),
    )(dx, y, gate)


def _final_loss(x, g, target, *, name):
    s, n = x.shape
    bs = _blk(s, ROW_BLOCK)

    def body(x_ref, g_ref, t_ref, loss_ref, dx_ref, dg_ref):
        xv = x_ref[...]
        r = lax.rsqrt(jnp.mean(xv * xv, axis=-1, keepdims=True) + NORM_EPS)
        xn = xv * r
        gv = g_ref[...]
        err = xn * gv - t_ref[...]
        dout = err * (1.0 / n)
        gg = dout * gv
        dx_ref[...] = r * (gg - xn * jnp.mean(gg * xn, axis=-1, keepdims=True))

        @pl.when(pl.program_id(0) == 0)
        def _():
            loss_ref[...] = jnp.zeros_like(loss_ref)
            dg_ref[...] = jnp.zeros_like(dg_ref)

        part = jnp.sum(jnp.sum(err * err, axis=-1, keepdims=True), axis=0, keepdims=True) * (0.5 / n)
        loss_ref[...] += jnp.broadcast_to(part, loss_ref.shape)
        dg_ref[...] += jnp.sum(dout * xn, axis=0, keepdims=True)

    row = pl.BlockSpec((bs, n), lambda i: (i, 0))
    vec = pl.BlockSpec((1, n), lambda i: (0, 0))
    return pl.pallas_call(
        body, name=name, grid=(s // bs,), in_specs=[row, vec, row],
        out_specs=[pl.BlockSpec((1, LANES), lambda i: (0, 0)), row, vec],
        out_shape=[jax.ShapeDtypeStruct((1, LANES), F32), jax.ShapeDtypeStruct((s, n), F32), jax.ShapeDtypeStruct((1, n), F32)],
        compiler_params=_params("arbitrary"),
    )(x, g, target)


def _lane_lt64(shape):
    return lax.broadcasted_iota(jnp.int32, shape, 1) < HEAD_DIM


def _keep_low(x):
    return jnp.where(_lane_lt64(x.shape), x.astype(F32), 0.0).astype(x.dtype)


def _keep_high(x):
    return jnp.where(_lane_lt64(x.shape), 0.0, x.astype(F32)).astype(x.dtype)


def _lane_merge(a, b):
    n = max(a.shape[0], b.shape[0])
    return jnp.where(_lane_lt64((n, LANES)), a, b)


def _pair(x, width, masked):
    if width == HEAD_DIM:
        return (_keep_low(x), _keep_high(x)) if masked else (x, x)
    return x[:, :LANES], x[:, LANES:]


def _qk_t(a, b):
    return lax.dot_general(a, b, (((1,), (1,)), ((), ())), preferred_element_type=F32)


def _attn_specs(s, blk, width, cols, resident):
    w = 2 * width
    if resident:
        return pl.BlockSpec((s, w), lambda p, i: (0, cols + p))
    return pl.BlockSpec((blk, w), lambda p, i: (i, cols + p))


BIAS_TERMS = 3


def _key_blocks(vt, blk):
    return vt.reshape(vt.shape[0], vt.shape[1] // blk, blk).transpose(1, 0, 2)


def _attn_fwd(q, k, vt, kbl, *, qc, kc, width, name):
    s = q.shape[0]
    blk = _blk(s, ATT_BLOCK)
    nb = s // blk
    has_bias = kbl is not None
    assert has_bias == (width == HEAD_DIM)

    def body(*refs):
        if has_bias:
            q_ref, k_ref, vt_ref, kbl_ref, o_ref, lse_ref = refs
        else:
            q_ref, k_ref, vt_ref, o_ref, lse_ref = refs
        i = pl.program_id(1)
        q2 = q_ref[...]
        if has_bias:
            lane = lax.broadcasted_iota(jnp.int32, (blk, LANES), 1)
            qf = q2.astype(F32)
            qh = (jnp.where(lane < HEAD_DIM, qf, jnp.where(lane < HEAD_DIM + BIAS_TERMS, 1.0, 0.0)).astype(BF16),
                  jnp.where(lane >= HEAD_DIM, qf, jnp.where(lane < BIAS_TERMS, 1.0, 0.0)).astype(BF16))
        else:
            qh = (q2[:, :LANES], q2[:, LANES:])

        def step(j, carry, nblk, diag):
            rows = pl.ds(pl.multiple_of(j * blk, blk), nblk * blk)
            vt1 = [jnp.concatenate([vt_ref[j + b], jnp.ones((16, blk), BF16)], axis=0) for b in range(nblk)]
            k2 = k_ref[rows, :]
            if has_bias:
                low = _lane_lt64(k2.shape)
                kf, bf = k2.astype(F32), kbl_ref[rows, :].astype(F32)
                kh = (jnp.where(low, kf, bf).astype(BF16), jnp.where(low, bf, kf).astype(BF16))
            else:
                kh = (k2[:, :LANES], k2[:, LANES:])
            out = []
            for hd in range(2):
                m, acc = carry[hd]
                st = _qk_t(kh[hd], qh[hd])
                if diag:
                    row = lax.broadcasted_iota(jnp.int32, (blk, blk), 0)
                    colq = lax.broadcasted_iota(jnp.int32, (blk, blk), 1)
                    st = jnp.where(row <= colq, st, -1e30)
                m_new = jnp.maximum(m, jnp.max(st, axis=0, keepdims=True))
                alpha = jnp.exp2(m - m_new)
                pt = jnp.exp2(st - m_new).astype(BF16)
                acc = alpha * acc
                for b in range(nblk):
                    acc = acc + jnp.dot(vt1[b], pt[b * blk:(b + 1) * blk], preferred_element_type=F32)
                out.append((m_new, acc))
            return tuple(out)

        one = (jnp.full((1, blk), -1e30, F32), jnp.zeros((LANES + 16, blk), F32))
        carry = lax.fori_loop(0, i // 8, lambda j, c: step(8 * j, c, 8, False), (one, one))
        carry = lax.fori_loop(0, (i % 8) // 4, lambda _, c: step(i - i % 8, c, 4, False), carry)
        carry = lax.fori_loop(0, (i % 4) // 2, lambda _, c: step(i - i % 4, c, 2, False), carry)
        carry = lax.fori_loop(0, i % 2, lambda _, c: step(i - 1, c, 1, False), carry)
        (ma, acca), (mb, accb) = step(i, carry, 1, True)
        la = jnp.max(acca[LANES:LANES + 8], axis=0, keepdims=True)
        lb = jnp.max(accb[LANES:LANES + 8], axis=0, keepdims=True)
        acca, accb = acca[0:LANES], accb[0:LANES]
        low = lax.broadcasted_iota(jnp.int32, (LANES, blk), 0) < HEAD_DIM
        o_ref[...] = jnp.where(low, acca / la, accb / lb).T
        lse_ref[0, 0] = ma + jnp.log(la) * LOG2E
        lse_ref[1, 0] = mb + jnp.log(lb) * LOG2E

    ins = [q, k, vt] + ([kbl] if has_bias else [])
    return pl.pallas_call(
        body, name=name, grid=(N_HEADS // 2, nb),
        in_specs=[_attn_specs(s, blk, width, qc, False), _attn_specs(s, blk, width, kc, True),
                  pl.BlockSpec((nb, LANES, blk), lambda p, i: (0, p, 0))]
                 + ([_attn_specs(s, blk, HEAD_DIM, 0, True)] if has_bias else []),
        out_specs=[pl.BlockSpec((blk, LANES), lambda p, i: (i, p)), pl.BlockSpec((2, 1, 1, blk), lambda p, i: (p, i, 0, 0))],
        out_shape=[jax.ShapeDtypeStruct((s, N_HEADS * HEAD_DIM), F32), jax.ShapeDtypeStruct((N_HEADS, nb, 1, blk), F32)],
        compiler_params=_params("parallel", "parallel"),
    )(*ins)


def _bias_lane_terms(kb2):
    terms, rest = [], kb2
    for _ in range(BIAS_TERMS):
        t = lax.reduce_precision(rest, 8, 7)
        terms.append(t.astype(BF16))
        rest = rest - t
    place = [[0.0] * (N_HEADS * HEAD_DIM) for _ in range(LANES)]
    for t in range(BIAS_TERMS):
        for h in range(N_HEADS):
            place[t * N_HEADS + h][(h // 2) * LANES + (HEAD_DIM if h % 2 == 0 else 0) + t] = 1.0
    return _matmul(_pad_axis(jnp.concatenate(terms, axis=1), 1, LANES), jnp.asarray(place, BF16), out_dtype=BF16,
                   name="fox_bias_lanes")


def _attn_delta(o, do, *, name):
    s, n = o.shape
    bs = _blk(s, ROW_BLOCK)

    def body(o_ref, do_ref, d_ref):
        for g in range(n // LANES):
            prod = do_ref[:, g * LANES:(g + 1) * LANES].astype(F32) * o_ref[:, g * LANES:(g + 1) * LANES]
            low = _lane_lt64(prod.shape)
            d_ref[:, g * LANES:(g + 1) * LANES] = _lane_merge(
                jnp.sum(jnp.where(low, prod, 0.0), axis=-1, keepdims=True),
                jnp.sum(jnp.where(low, 0.0, prod), axis=-1, keepdims=True))

    row = pl.BlockSpec((bs, n), lambda i: (i, 0))
    return pl.pallas_call(
        body, name=name, grid=(s // bs,), in_specs=[row, row], out_specs=row,
        out_shape=jax.ShapeDtypeStruct((s, n), F32), compiler_params=_params("parallel"),
    )(o, do)


def _attn_bwd(q, k, v, kb_col, do, lse_row, delta_row, *, qc, kc, vc, width, dq_mult, dk_mult, out_dtype, name):
    s = q.shape[0]
    blk = _blk(s, ATT_BLOCK)
    nb = s // blk
    has_bias = kb_col is not None

    def body(*refs):
        if has_bias:
            q_ref, k_ref, v_ref, kb_ref, do_ref, lse_ref, dl_ref, dk_ref, dv_ref, db_ref, dq_ref, dr_ref = refs
        else:
            q_ref, k_ref, v_ref, do_ref, lse_ref, dl_ref, dk_ref, dv_ref, db_ref, dq_ref = refs
        j = pl.program_id(1)

        @pl.when(j == 0)
        def _():
            dq_ref[...] = jnp.zeros_like(dq_ref)
            if has_bias:
                dr_ref[...] = jnp.zeros_like(dr_ref)

        kh = _pair(k_ref[...], width, True)
        v2 = v_ref[...]
        vh = (_keep_low(v2), _keep_high(v2))
        if has_bias:
            kb2 = kb_ref[0]
            kbh = (kb2[:, 0:1], kb2[:, 1:2])

        def step(i, carry, nblk, diag):
            rows = pl.ds(pl.multiple_of(i * blk, blk), nblk * blk)
            qh = _pair(q_ref[rows, :], width, False)
            doi = do_ref[rows, :]
            out, dq_parts = [], []
            for hd in range(2):
                dk, dvv, db = carry[hd]
                st = _qk_t(kh[hd], qh[hd])
                if has_bias:
                    st = st + kbh[hd]
                if diag:
                    row = lax.broadcasted_iota(jnp.int32, (blk, blk), 0)
                    colq = lax.broadcasted_iota(jnp.int32, (blk, blk), 1)
                    st = jnp.where(row <= colq, st, -1e30)
                lse_i = jnp.concatenate([lse_ref[hd, i + b] for b in range(nblk)], axis=1)
                delta_i = jnp.concatenate([dl_ref[hd, i + b] for b in range(nblk)], axis=1)
                pt = jnp.exp2(st - lse_i)
                dvv = dvv + jnp.dot(pt.astype(BF16), doi, preferred_element_type=F32)
                dst = pt * (_qk_t(vh[hd], doi) - delta_i)
                dsb = dst.astype(BF16)
                dk = dk + jnp.dot(dsb, qh[hd], preferred_element_type=F32)
                db = db + jnp.sum(dst, axis=-1, keepdims=True)
                dq_parts.append(lax.dot_general(dsb, kh[hd], (((0,), (0,)), ((), ())), preferred_element_type=F32))
                if has_bias:
                    rsum = jnp.sum(dst, axis=0, keepdims=True)
                    for b in range(nblk):
                        dr_ref[hd, i + b] += rsum[:, b * blk:(b + 1) * blk]
                out.append((dk, dvv, db))
            if width == HEAD_DIM:
                dq_ref[rows, :] += (dq_parts[0] + dq_parts[1]) * dq_mult
            else:
                dq_ref[rows, 0:LANES] += dq_parts[0] * dq_mult
                dq_ref[rows, LANES:2 * LANES] += dq_parts[1] * dq_mult
            return tuple(out)

        one = (jnp.zeros((blk, LANES), F32), jnp.zeros((blk, LANES), F32), jnp.zeros((blk, 1), F32))
        carry = step(j, (one, one), 1, True)
        rest = nb - 1 - j
        carry = lax.fori_loop(0, rest // 4, lambda t, c: step(j + 1 + 4 * t, c, 4, False), carry)
        carry = lax.fori_loop(0, (rest % 4) // 2, lambda _, c: step(nb - rest % 4, c, 2, False), carry)
        (dka, dva, dba), (dkb, dvb, dbb) = lax.fori_loop(0, rest % 2, lambda _, c: step(nb - 1, c, 1, False), carry)
        if width == HEAD_DIM:
            dk = _lane_merge(dka, dkb)
        else:
            dk = jnp.concatenate([dka, dkb], axis=1)
        dk_ref[...] = (dk * dk_mult).astype(out_dtype)
        dv_ref[...] = _lane_merge(dva, dvb).astype(out_dtype)
        db_ref[...] = _lane_merge(dba, dbb)

    stat = pl.BlockSpec((blk, LANES), lambda p, jj: (jj, p))
    rows = pl.BlockSpec((2, nb, 1, blk), lambda p, jj: (p, 0, 0, 0))
    ins = [q, k, v] + ([kb_col] if has_bias else []) + [do, lse_row, delta_row]
    return pl.pallas_call(
        body, name=name, grid=(N_HEADS // 2, nb),
        in_specs=[_attn_specs(s, blk, width, qc, True), _attn_specs(s, blk, width, kc, False),
                  _attn_specs(s, blk, HEAD_DIM, vc, False)]
                 + ([pl.BlockSpec((1, blk, 2), lambda p, jj: (p, jj, 0))] if has_bias else [])
                 + [pl.BlockSpec((s, LANES), lambda p, jj: (0, p)), rows, rows],
        out_specs=[pl.BlockSpec((blk, 2 * width), lambda p, jj: (jj, p)), stat, stat,
                   pl.BlockSpec((s, 2 * width), lambda p, jj: (0, p))] + ([rows] if has_bias else []),
        out_shape=[jax.ShapeDtypeStruct((s, N_HEADS * width), out_dtype), jax.ShapeDtypeStruct((s, N_HEADS * HEAD_DIM), out_dtype),
                   jax.ShapeDtypeStruct((s, N_HEADS * HEAD_DIM), F32), jax.ShapeDtypeStruct((s, N_HEADS * width), F32)]
                  + ([jax.ShapeDtypeStruct((N_HEADS, nb, 1, blk), F32)] if has_bias else []),
        compiler_params=_params("parallel", "arbitrary"),
    )(*ins)


def _head_stat(t):
    return t[:, ::HEAD_DIM]


def _stat_rows(t16, blk):
    s = t16.shape[0]
    return t16.T.reshape(N_HEADS, s // blk, 1, blk)


def _attention_bwd(res, do, *, qc, kc, vc, width, dq_mult, dk_mult, out_dtype, name):
    q, k, v, bias, o, lse_row = res
    s = q.shape[0]
    blk = _blk(s, ATT_BLOCK)
    kb_col = None if bias is None else bias.reshape(s, N_HEADS // 2, 2).transpose(1, 0, 2)
    delta_row = _stat_rows(_head_stat(_attn_delta(o, do, name=name + "_delta")), blk)
    outs = _attn_bwd(q, k, v, kb_col, do, lse_row, delta_row, qc=qc, kc=kc, vc=vc, width=width, dq_mult=dq_mult,
                     dk_mult=dk_mult, out_dtype=out_dtype, name=name + "_bwd")
    dk, dv, dcol, dq = outs[:4]
    if bias is None:
        return dq, dk, dv, None
    return dq, dk, dv, outs[4].reshape(N_HEADS, s).T - _head_stat(dcol)


def _fox_gate_fwd(fl, bf, *, name):
    s, n = fl.shape
    bs = _blk(s, ROW_BLOCK)

    def body(fl_ref, bf_ref, cum_ref, carry_ref):
        @pl.when(pl.program_id(0) == 0)
        def _():
            carry_ref[...] = jnp.zeros_like(carry_ref)

        z = fl_ref[...] + bf_ref[...]
        lf = jnp.minimum(z, 0.0) - jnp.log1p(jnp.exp(-jnp.abs(z)))
        row = lax.broadcasted_iota(jnp.int32, (bs, bs), 0)
        col = lax.broadcasted_iota(jnp.int32, (bs, bs), 1)
        tri = (col <= row).astype(F32)
        cum_ref[...] = jnp.dot(tri, lf, preferred_element_type=F32, precision=HIGHEST) + carry_ref[...]
        carry_ref[...] += jnp.sum(lf, axis=0, keepdims=True)

    return pl.pallas_call(
        body, name=name, grid=(s // bs,),
        in_specs=[pl.BlockSpec((bs, n), lambda i: (i, 0)), pl.BlockSpec((1, n), lambda i: (0, 0))],
        out_specs=pl.BlockSpec((bs, n), lambda i: (i, 0)),
        out_shape=jax.ShapeDtypeStruct((s, n), F32), scratch_shapes=[pltpu.VMEM((1, n), F32)],
        compiler_params=_params("arbitrary"),
    )(fl, bf)


def _fox_gate_bwd(fl, bf, dcum, *, name):
    s, n = fl.shape
    bs = _blk(s, ROW_BLOCK)
    nb = s // bs

    def body(fl_ref, bf_ref, dc_ref, dz_ref, dbf_ref, carry_ref):
        @pl.when(pl.program_id(0) == 0)
        def _():
            carry_ref[...] = jnp.zeros_like(carry_ref)
            dbf_ref[...] = jnp.zeros_like(dbf_ref)

        dc = dc_ref[...]
        row = lax.broadcasted_iota(jnp.int32, (bs, bs), 0)
        col = lax.broadcasted_iota(jnp.int32, (bs, bs), 1)
        tri = (col >= row).astype(F32)
        dlf = jnp.dot(tri, dc, preferred_element_type=F32, precision=HIGHEST) + carry_ref[...]
        carry_ref[...] += jnp.sum(dc, axis=0, keepdims=True)
        z = fl_ref[...] + bf_ref[...]
        dz = dlf / (1.0 + jnp.exp(z))
        dz_ref[...] = dz
        dbf_ref[...] += jnp.sum(dz, axis=0, keepdims=True)

    rev = pl.BlockSpec((bs, n), lambda i: (nb - 1 - i, 0))
    vec = pl.BlockSpec((1, n), lambda i: (0, 0))
    return pl.pallas_call(
        body, name=name, grid=(nb,), in_specs=[rev, vec, rev], out_specs=[rev, vec],
        out_shape=[jax.ShapeDtypeStruct((s, n), F32), jax.ShapeDtypeStruct((1, n), F32)],
        scratch_shapes=[pltpu.VMEM((1, n), F32)], compiler_params=_params("arbitrary"),
    )(fl, bf, dcum)


def _rope(x1, x2, cos, sin, *, negate, name):
    s, n = x1.shape
    bs = _blk(s, ROW_BLOCK)

    def body(a_ref, b_ref, c_ref, s_ref, o1_ref, o2_ref):
        a, b, cv = a_ref[...], b_ref[...], c_ref[...]
        sv = -s_ref[...] if negate else s_ref[...]
        o1_ref[...] = a * cv - b * sv
        o2_ref[...] = b * cv + a * sv

    row = pl.BlockSpec((bs, n), lambda i: (i, 0))
    return pl.pallas_call(
        body, name=name, grid=(s // bs,), in_specs=[row] * 4, out_specs=[row, row],
        out_shape=[jax.ShapeDtypeStruct((s, n), F32)] * 2, compiler_params=_params("parallel"),
    )(x1, x2, cos, sin)


def _rope_heads(x, ta, tb, tc, *, out_dtype, name):
    s, n = x.shape
    bs = _blk(s, ROW_BLOCK)

    def body(x_ref, a_ref, b_ref, c_ref, o_ref):
        av, bv, cv = a_ref[...], b_ref[...], c_ref[...]
        for g in range(n // LANES):
            xg = x_ref[:, g * LANES:(g + 1) * LANES]
            og = xg * av + pltpu.roll(xg, LANES - MLA_ROPE_HALF, 1) * bv + pltpu.roll(xg, MLA_ROPE_HALF, 1) * cv
            o_ref[:, g * LANES:(g + 1) * LANES] = og.astype(out_dtype)

    row = pl.BlockSpec((bs, n), lambda i: (i, 0))
    tab = pl.BlockSpec((bs, LANES), lambda i: (i, 0))
    return pl.pallas_call(
        body, name=name, grid=(s // bs,), in_specs=[row, tab, tab, tab], out_specs=row,
        out_shape=jax.ShapeDtypeStruct((s, n), out_dtype), compiler_params=_params("parallel"),
    )(x, ta, tb, tc)


def _group_sum(x, *, name):
    s, n = x.shape
    bs = _blk(s, ROW_BLOCK)

    def body(x_ref, o_ref):
        acc = x_ref[:, 0:LANES]
        for g in range(1, n // LANES):
            acc = acc + x_ref[:, g * LANES:(g + 1) * LANES]
        o_ref[...] = acc

    return pl.pallas_call(
        body, name=name, grid=(s // bs,), in_specs=[pl.BlockSpec((bs, n), lambda i: (i, 0))],
        out_specs=pl.BlockSpec((bs, LANES), lambda i: (i, 0)),
        out_shape=jax.ShapeDtypeStruct((s, LANES), F32), compiler_params=_params("parallel"),
    )(x)


def _shift_down(x, k):
    return pltpu.roll(x, k, 0)


def _conv_rows(ext, w_ref, b_ref, rows):
    y = b_ref[...] + w_ref[0:1, :] * _shift_down(ext, 2) + w_ref[1:2, :] * _shift_down(ext, 1) + w_ref[2:3, :] * ext
    return y[8:8 + rows]


def _conv_gate_fwd(u, cw, cb, *, name):
    s, f2 = u.shape
    f = f2 // 2
    nf = f // LANES
    r = _blk(s, CONV_ROWS)
    r8 = r // 8

    def body(ug_ref, ugp_ref, uv_ref, uvp_ref, wg_ref, wv_ref, bg_ref, bv_ref, o_ref):
        first = pl.program_id(1) == 0

        def conv(cur_ref, prev_ref, w_ref, b_ref):
            prev = jnp.where(first, 0.0, prev_ref[...])
            return _conv_rows(jnp.concatenate([prev, cur_ref[...]], axis=0), w_ref, b_ref, r)

        yg = conv(ug_ref, ugp_ref, wg_ref, bg_ref)
        yv = conv(uv_ref, uvp_ref, wv_ref, bv_ref)
        o_ref[...] = (yg * jax.nn.sigmoid(yg) * yv).astype(BF16)

    def cur(off):
        return pl.BlockSpec((r, LANES), lambda c, i: (i, c + off))

    def prev(off):
        return pl.BlockSpec((8, LANES), lambda c, i: (jnp.maximum(i * r8 - 1, 0), c + off))

    def wspec(rows, off):
        return pl.BlockSpec((rows, LANES), lambda c, i: (0, c + off))

    return pl.pallas_call(
        body, name=name, grid=(nf, s // r),
        in_specs=[cur(0), prev(0), cur(nf), prev(nf), wspec(3, 0), wspec(3, nf), wspec(1, 0), wspec(1, nf)],
        out_specs=pl.BlockSpec((r, LANES), lambda c, i: (i, c)),
        out_shape=jax.ShapeDtypeStruct((s, f), BF16), compiler_params=_params("parallel", "parallel"),
    )(u, u, u, u, cw, cw, cb, cb)


def _conv_gate_bwd(u, cw, cb, dg, *, name):
    s, f2 = u.shape
    f = f2 // 2
    nf = f // LANES
    r = _blk(s, CONV_ROWS)
    r8 = r // 8
    nr = s // r

    def body(ug_ref, ugp_ref, ugn_ref, uv_ref, uvp_ref, uvn_ref, wg_ref, wv_ref, bg_ref, bv_ref, dg_ref, dgn_ref,
             dug_ref, duv_ref, dwg_ref, dwv_ref, dbg_ref, dbv_ref):
        i = pl.program_id(1)
        first, last = i == 0, i == nr - 1

        def ext_of(cur_ref, prev_ref, next_ref):
            prev = jnp.where(first, 0.0, prev_ref[...])
            return jnp.concatenate([prev, cur_ref[...], next_ref[...]], axis=0)

        eg, ev = ext_of(ug_ref, ugp_ref, ugn_ref), ext_of(uv_ref, uvp_ref, uvn_ref)
        yg = _conv_rows(eg, wg_ref, bg_ref, r + 8)
        yv = _conv_rows(ev, wv_ref, bv_ref, r + 8)
        dgn = jnp.where(last, 0.0, dgn_ref[...])
        dgx = jnp.concatenate([dg_ref[...], dgn], axis=0)
        sg = jax.nn.sigmoid(yg)
        dyg = dgx * yv * (sg * (1.0 + yg * (1.0 - sg)))
        dyv = dgx * (yg * sg)

        @pl.when(i == 0)
        def _():
            for ref in (dwg_ref, dwv_ref, dbg_ref, dbv_ref):
                ref[...] = jnp.zeros_like(ref)

        def grads(dy, ext, w_ref, du_ref, dw_ref, db_ref):
            n = r + 8
            du = w_ref[2:3, :] * dy + w_ref[1:2, :] * pltpu.roll(dy, n - 1, 0) + w_ref[0:1, :] * pltpu.roll(dy, n - 2, 0)
            du_ref[...] = du[0:r].astype(BF16)
            dyc = dy[0:r]
            db_ref[...] += jnp.sum(dyc, axis=0, keepdims=True)
            ext_c = ext[0:r + 8]
            dw_ref[0:1, :] += jnp.sum(dyc * _shift_down(ext_c, 2)[8:], axis=0, keepdims=True)
            dw_ref[1:2, :] += jnp.sum(dyc * _shift_down(ext_c, 1)[8:], axis=0, keepdims=True)
            dw_ref[2:3, :] += jnp.sum(dyc * ext_c[8:], axis=0, keepdims=True)

        grads(dyg, eg, wg_ref, dug_ref, dwg_ref, dbg_ref)
        grads(dyv, ev, wv_ref, duv_ref, dwv_ref, dbv_ref)

    def cur(off):
        return pl.BlockSpec((r, LANES), lambda c, i: (i, c + off))

    def prev(off):
        return pl.BlockSpec((8, LANES), lambda c, i: (jnp.maximum(i * r8 - 1, 0), c + off))

    def nxt(off):
        return pl.BlockSpec((8, LANES), lambda c, i: (jnp.minimum((i + 1) * r8, s // 8 - 1), c + off))

    def wspec(rows, off):
        return pl.BlockSpec((rows, LANES), lambda c, i: (0, c + off))

    outs = pl.pallas_call(
        body, name=name, grid=(nf, nr),
        in_specs=[cur(0), prev(0), nxt(0), cur(nf), prev(nf), nxt(nf), wspec(3, 0), wspec(3, nf), wspec(1, 0), wspec(1, nf),
                  cur(0), nxt(0)],
        out_specs=[cur(0), cur(0), wspec(3, 0), wspec(3, 0), wspec(1, 0), wspec(1, 0)],
        out_shape=[jax.ShapeDtypeStruct((s, f), BF16), jax.ShapeDtypeStruct((s, f), BF16),
                   jax.ShapeDtypeStruct((3, f), F32), jax.ShapeDtypeStruct((3, f), F32),
                   jax.ShapeDtypeStruct((1, f), F32), jax.ShapeDtypeStruct((1, f), F32)],
        compiler_params=_params("parallel", "arbitrary"),
    )(u, u, u, u, u, u, cw, cw, cb, cb, dg, dg)
    dug, duv, dwg, dwv, dbg, dbv = outs
    return jnp.concatenate([dug, duv], axis=1), jnp.concatenate([dwg, dwv], axis=1), jnp.concatenate([dbg, dbv], axis=1)


def _adamw(w, g, m, v, *, slabs, name):
    shape = w.shape
    cols = shape[-1]
    rows = w.size // cols
    w2, m2, v2 = (t.reshape(rows, cols) for t in (w, m, v))
    g2 = g.reshape((N_DEV, rows, cols) if slabs else (rows, cols))
    br = _row_blk(rows, ROW_BLOCK // 2 if slabs else ROW_BLOCK)

    def body(w_ref, g_ref, m_ref, v_ref, go_ref, d_ref, nm_ref, nv_ref):
        if slabs:
            gv = g_ref[0].astype(F32)
            for p in range(1, N_DEV):
                gv = gv + g_ref[p].astype(F32)
        else:
            gv = g_ref[...]
        nm = ADAM_B1 * m_ref[...] + (1.0 - ADAM_B1) * gv
        nv = ADAM_B2 * v_ref[...] + (1.0 - ADAM_B2) * (gv * gv)
        m_hat = nm / (1.0 - ADAM_B1 ** ADAM_STEP)
        v_hat = nv / (1.0 - ADAM_B2 ** ADAM_STEP)
        go_ref[...] = gv
        d_ref[...] = -ADAM_LR * (m_hat / (jnp.sqrt(v_hat) + ADAM_EPS) + ADAM_WD * w_ref[...])
        nm_ref[...] = nm
        nv_ref[...] = nv

    spec = pl.BlockSpec((br, cols), lambda i: (i, 0))
    gspec = pl.BlockSpec((N_DEV, br, cols), lambda i: (0, i, 0)) if slabs else spec
    outs = pl.pallas_call(
        body, name=name, grid=(rows // br,), in_specs=[spec, gspec, spec, spec], out_specs=[spec] * 4,
        out_shape=[jax.ShapeDtypeStruct((rows, cols), F32)] * 4, compiler_params=_params("parallel"),
    )(w2, g2, m2, v2)
    return tuple(t.reshape(shape) for t in outs)


def _exchange(xs, *, same_src, name):
    n = len(xs)
    slabs = [x.shape if same_src else x.shape[1:] for x in xs]

    def body(*refs):
        x_refs, o_refs = refs[:n], refs[n:2 * n]
        send_sems, recv_sems, loc_sems = refs[2 * n:]
        ix, iy, ic = lax.axis_index("x"), lax.axis_index("y"), lax.axis_index("c")
        me = 4 * ix + 2 * iy + ic
        local, sends, recvs = [], [], []
        for a in range(n):
            def src(p, a=a):
                return x_refs[a] if same_src else x_refs[a].at[p]

            local.append(pltpu.make_async_copy(src(me), o_refs[a].at[me], loc_sems.at[a]))
            for k in (1, 2, 4, 3, 5, 6, 7):
                px = 1 - ix if k & 4 else ix
                py = 1 - iy if k & 2 else iy
                pc = 1 - ic if k & 1 else ic
                p = 4 * px + 2 * py + pc
                for dst, out in ((me, sends), (p, recvs)):
                    out.append(pltpu.make_async_remote_copy(
                        src_ref=src(p), dst_ref=o_refs[a].at[dst], send_sem=send_sems.at[a, k - 1],
                        recv_sem=recv_sems.at[a, k - 1], device_id=(px, py, pc), device_id_type=pl.DeviceIdType.MESH))
        for cp in local + sends:
            cp.start()
        for cp in recvs:
            cp.wait_recv()
        for cp in sends:
            cp.wait_send()
        for cp in local:
            cp.wait()

    return pl.pallas_call(
        body, name=name,
        in_specs=[pl.BlockSpec(memory_space=pl.ANY)] * n, out_specs=[pl.BlockSpec(memory_space=pl.ANY)] * n,
        out_shape=[jax.ShapeDtypeStruct((N_DEV,) + tuple(sl), x.dtype) for sl, x in zip(slabs, xs)],
        scratch_shapes=[pltpu.SemaphoreType.DMA((n, N_DEV - 1)), pltpu.SemaphoreType.DMA((n, N_DEV - 1)),
                        pltpu.SemaphoreType.DMA((n,))],
        compiler_params=pltpu.CompilerParams(has_side_effects=True, vmem_limit_bytes=VMEM_LIMIT_BYTES),
    )(*xs)


def _gather_two_level(xs, *, name):
    n = len(xs)

    def body(*refs):
        x_refs, o_refs = refs[:n], refs[n:2 * n]
        send_sems, recv_sems, loc_sems = refs[2 * n:]
        ix, iy, ic = lax.axis_index("x"), lax.axis_index("y"), lax.axis_index("c")
        me = 4 * ix + 2 * iy + ic
        sib = 4 * ix + 2 * iy + (1 - ic)
        chips = [(1 - ix if ch & 2 else ix, 1 - iy if ch & 1 else iy) for ch in (1, 2, 3)]

        def copy(a, pos, src, slot, to):
            return pltpu.make_async_remote_copy(
                src_ref=src, dst_ref=o_refs[a].at[slot], send_sem=send_sems.at[a, pos], recv_sem=recv_sems.at[a, pos],
                device_id=to, device_id_type=pl.DeviceIdType.MESH)

        local = [pltpu.make_async_copy(x_refs[a], o_refs[a].at[me], loc_sems.at[a]) for a in range(n)]
        first, passed, arrive = [], [], []
        for a in range(n):
            first.append(copy(a, 0, x_refs[a], me, (ix, iy, 1 - ic)))
            arrive.append(copy(a, 0, x_refs[a], sib, (ix, iy, 1 - ic)))
            for ch, (px, py) in enumerate(chips, start=1):
                same, other = 4 * px + 2 * py + ic, 4 * px + 2 * py + (1 - ic)
                first.append(copy(a, 2 * ch - 1, x_refs[a], me, (px, py, ic)))
                passed.append((copy(a, 2 * ch - 1, x_refs[a], same, (px, py, ic)),
                               copy(a, 2 * ch, o_refs[a].at[same], same, (ix, iy, 1 - ic))))
                arrive.append(copy(a, 2 * ch, x_refs[a], other, (ix, iy, 1 - ic)))
        for cp in local + first:
            cp.start()
        for landed, onward in passed:
            landed.wait_recv()
            onward.start()
        for cp in arrive:
            cp.wait_recv()
        for cp in first + [onward for _, onward in passed]:
            cp.wait_send()
        for cp in local:
            cp.wait()

    return pl.pallas_call(
        body, name=name,
        in_specs=[pl.BlockSpec(memory_space=pl.ANY)] * n, out_specs=[pl.BlockSpec(memory_space=pl.ANY)] * n,
        out_shape=[jax.ShapeDtypeStruct((N_DEV,) + tuple(x.shape), x.dtype) for x in xs],
        scratch_shapes=[pltpu.SemaphoreType.DMA((n, N_DEV - 1)), pltpu.SemaphoreType.DMA((n, N_DEV - 1)),
                        pltpu.SemaphoreType.DMA((n,))],
        compiler_params=pltpu.CompilerParams(has_side_effects=True, vmem_limit_bytes=VMEM_LIMIT_BYTES),
    )(*xs)


def _sum_slabs(x, *, name):
    n, r, c = x.shape
    br = _row_blk(r, ROW_BLOCK)

    def body(x_ref, o_ref):
        acc = x_ref[0]
        for p in range(1, n):
            acc = acc + x_ref[p]
        o_ref[...] = acc

    return pl.pallas_call(
        body, name=name, grid=(r // br,), in_specs=[pl.BlockSpec((n, br, c), lambda i: (0, i, 0))],
        out_specs=pl.BlockSpec((br, c), lambda i: (i, 0)),
        out_shape=jax.ShapeDtypeStruct((r, c), F32), compiler_params=_params("parallel"),
    )(x)


def _silu(x, *, name):
    def body(x_ref, o_ref):
        xv = x_ref[...]
        o_ref[...] = (xv * jax.nn.sigmoid(xv)).astype(BF16)

    return pl.pallas_call(body, name=name, out_shape=jax.ShapeDtypeStruct(x.shape, BF16),
                          compiler_params=_params())(x)


_BIG = {"fox_w_in": 2, "fox_w_o": 1, "mla_w_a": 1, "mla_w_uq": 2, "mla_w_ukv": 2, "mla_w_o": 1, "ffn_w_in": 2, "ffn_w_out": 1}
_SMALL = {"mla_g_q": 1, "mla_g_kv": 1, "ffn_conv_w": 2}
_REPL = ("fox_b_f", "ffn_conv_b", "final_g")


def _gathered_to_full(g, axis):
    full = jnp.moveaxis(g, 0, axis)
    shape = list(full.shape)
    shape[axis:axis + 2] = [shape[axis] * shape[axis + 1]]
    return full.reshape(shape)


def _full_to_chunks(full, axis):
    shape = list(full.shape)
    shape[axis:axis + 1] = [N_DEV, shape[axis] // N_DEV]
    return jnp.moveaxis(full.reshape(shape), axis, 0)


def _per_head(parts, s_or_rows):
    return jnp.concatenate([p.reshape(s_or_rows, N_HEADS, -1) for p in parts], axis=-1).reshape(s_or_rows, -1)


def kernel(x, c, ada_w, ada_b, fox_w_in, fox_b_f, fox_w_o, mla_w_a, mla_g_q, mla_g_kv, mla_w_uq, mla_w_ukv, mla_w_o, ffn_w_in, ffn_conv_w, ffn_conv_b, ffn_w_out, final_g, loss_target, m_ada_w, m_ada_b, m_fox_w_in, m_fox_b_f, m_fox_w_o, m_mla_w_a, m_mla_g_q, m_mla_g_kv, m_mla_w_uq, m_mla_w_ukv, m_mla_w_o, m_ffn_w_in, m_ffn_conv_w, m_ffn_conv_b, m_ffn_w_out, m_final_g, v_ada_w, v_ada_b, v_fox_w_in, v_fox_b_f, v_fox_w_o, v_mla_w_a, v_mla_g_q, v_mla_g_kv, v_mla_w_uq, v_mla_w_ukv, v_mla_w_o, v_ffn_w_in, v_ffn_conv_w, v_ffn_conv_b, v_ffn_w_out, v_final_g):
    weights = dict(ada_w=ada_w, ada_b=ada_b, fox_w_in=fox_w_in, fox_b_f=fox_b_f, fox_w_o=fox_w_o, mla_w_a=mla_w_a,
                   mla_g_q=mla_g_q, mla_g_kv=mla_g_kv, mla_w_uq=mla_w_uq, mla_w_ukv=mla_w_ukv, mla_w_o=mla_w_o,
                   ffn_w_in=ffn_w_in, ffn_conv_w=ffn_conv_w, ffn_conv_b=ffn_conv_b, ffn_w_out=ffn_w_out, final_g=final_g)
    mom_m = dict(ada_w=m_ada_w, ada_b=m_ada_b, fox_w_in=m_fox_w_in, fox_b_f=m_fox_b_f, fox_w_o=m_fox_w_o, mla_w_a=m_mla_w_a,
                 mla_g_q=m_mla_g_q, mla_g_kv=m_mla_g_kv, mla_w_uq=m_mla_w_uq, mla_w_ukv=m_mla_w_ukv, mla_w_o=m_mla_w_o,
                 ffn_w_in=m_ffn_w_in, ffn_conv_w=m_ffn_conv_w, ffn_conv_b=m_ffn_conv_b, ffn_w_out=m_ffn_w_out, final_g=m_final_g)
    mom_v = dict(ada_w=v_ada_w, ada_b=v_ada_b, fox_w_in=v_fox_w_in, fox_b_f=v_fox_b_f, fox_w_o=v_fox_w_o, mla_w_a=v_mla_w_a,
                 mla_g_q=v_mla_g_q, mla_g_kv=v_mla_g_kv, mla_w_uq=v_mla_w_uq, mla_w_ukv=v_mla_w_ukv, mla_w_o=v_mla_w_o,
                 ffn_w_in=v_ffn_w_in, ffn_conv_w=v_ffn_conv_w, ffn_conv_b=v_ffn_conv_b, ffn_w_out=v_ffn_w_out, final_g=v_final_g)
    order = list(weights)
    x0 = x[0]
    target = loss_target[0]
    s = x0.shape[0]
    d = D_MODEL
    cols = ada_w.shape[-1]
    nq = N_HEADS * HEAD_DIM

    small_names = ["c"] + list(_SMALL)
    small_all = dict(zip(small_names, _exchange([c] + [weights[n] for n in _SMALL], same_src=True, name="gather_small")))
    c_all = small_all["c"].reshape(N_DEV, d)
    g_q = _gathered_to_full(small_all["mla_g_q"], 1)
    g_kv = _gathered_to_full(small_all["mla_g_kv"], 1)
    conv_w = _gathered_to_full(small_all["ffn_conv_w"], 2)

    c_pad = _pad_axis(c_all, 0, LANES)
    silu_c = _silu(c_pad, name="silu_c")
    w_ada = ada_w.reshape(4, d, cols)
    b_ada = ada_b.reshape(4, 1, cols)
    mods = [_matmul(silu_c, w_ada[i], name=f"ada_mod{i}")[:N_DEV] + b_ada[i] for i in range(4)]
    mod_send = _pad_axis(jnp.stack(mods, axis=1), 1, 8)
    mod_recv, = _exchange([mod_send], same_src=False, name="scatter_mod")
    mod = mod_recv[:, :4].transpose(1, 0, 2).reshape(4, 3 * d)
    shift = [mod[i:i + 1, 0:d] for i in range(4)]
    scale = [mod[i:i + 1, d:2 * d] for i in range(4)]
    gate = [mod[i:i + 1, 2 * d:3 * d] for i in range(4)]

    big_all = _gather_two_level([weights[n].astype(BF16) for n in _BIG], name="gather_weights")
    wfull = {n: _gathered_to_full(g, _BIG[n]) for n, g in zip(_BIG, big_all)}

    w_fox_in = _pad_axis(wfull["fox_w_in"][0], 1, LANES)
    w_fox_qkv, w_fox_f = w_fox_in[:, :3 * nq], w_fox_in[:, 3 * nq:]
    w_fox_o = wfull["fox_w_o"][0]
    w_a = _pad_axis(wfull["mla_w_a"][0], 1, LANES)
    wq = wfull["mla_w_uq"][0].reshape(MLA_Q_RANK, N_HEADS, HEAD_DIM + MLA_ROPE_DIM)
    w_uq = _pad_axis(wq, 2, LANES).reshape(MLA_Q_RANK, N_HEADS * LANES)
    wkv = wfull["mla_w_ukv"][0].reshape(MLA_KV_RANK, N_HEADS, 2 * HEAD_DIM)
    w_ukv = jnp.concatenate([wkv[:, :, :HEAD_DIM].reshape(MLA_KV_RANK, -1), wkv[:, :, HEAD_DIM:].reshape(MLA_KV_RANK, -1)], axis=1)
    w_mla_o = wfull["mla_w_o"][0]
    w_ffn_in = wfull["ffn_w_in"]
    w_ffn_out = wfull["ffn_w_out"]
    conv_b = ffn_conv_b

    fox_scale = HEAD_DIM ** -0.5
    mla_scale = (HEAD_DIM + MLA_ROPE_DIM) ** -0.5
    pos = jnp.arange(s, dtype=F32)
    inv_freq = ROPE_BASE ** (-jnp.arange(0, MLA_ROPE_DIM, 2, dtype=F32) / MLA_ROPE_DIM)
    ang = pos[:, None] * inv_freq[None, :]
    cos16, sin16 = jnp.cos(ang), jnp.sin(ang)
    z16, z32, z64 = jnp.zeros((s, 16), F32), jnp.zeros((s, 32), F32), jnp.zeros((s, 64), F32)
    tab_a = jnp.concatenate([jnp.ones((s, 64), F32), cos16, cos16, z32], axis=1) * (mla_scale * LOG2E)
    tab_b = jnp.concatenate([z64, -sin16, z16, z32], axis=1) * (mla_scale * LOG2E)
    tab_c = jnp.concatenate([z64, z16, sin16, z32], axis=1) * (mla_scale * LOG2E)

    h0 = _norm_fwd(x0, scale[0], shift[0], plus_one=True, out_dtype=BF16, name="ada_fwd0")
    q_mult = jnp.concatenate([jnp.full((1, nq), fox_scale * LOG2E, F32), jnp.ones((1, 2 * nq), F32)], axis=1)
    qkv = _matmul(h0, w_fox_qkv, out_dtype=BF16, col_scale=q_mult, name="fox_proj")
    fl = _matmul(h0, w_fox_f, name="fox_proj_f")[:, :N_HEADS]
    cum = _fox_gate_fwd(fl, fox_b_f, name="fox_gate_fwd")
    fox_cfg = dict(qc=0, kc=N_HEADS // 2, vc=N_HEADS, width=HEAD_DIM)
    kb2 = cum * -LOG2E
    att_blk = _blk(s, ATT_BLOCK)
    fvt = _matmul(w_fox_qkv[:, 2 * nq:], h0, ta=True, tb=True, out_dtype=BF16, name="fox_proj_vt")
    fo, fox_lse = _attn_fwd(qkv, qkv, _key_blocks(fvt, att_blk), _bias_lane_terms(kb2), qc=0, kc=N_HEADS // 2,
                            width=HEAD_DIM, name="fox_attn_fwd")
    fox_res = (qkv, qkv, qkv, kb2, fo, fox_lse)
    y0 = _matmul(fo, w_fox_o, name="fox_out")
    x1 = _resid_fwd(x0, y0, gate[0], name="resid_fwd0")

    def ffn_fwd(xin, li, sub):
        hh = _norm_fwd(xin, scale[sub], shift[sub], plus_one=True, out_dtype=BF16, name=f"ada_fwd{sub}")
        u = _matmul(hh, w_ffn_in[li], name=f"ffn_up{li}")
        g = _conv_gate_fwd(u, conv_w[li], conv_b[li:li + 1], name=f"conv_fwd{li}")
        y = _matmul(g, w_ffn_out[li], name=f"ffn_down{li}")
        return _resid_fwd(xin, y, gate[sub], name=f"resid_fwd{sub}"), (hh, u, g, y)

    x2, ffn0_res = ffn_fwd(x1, 0, 1)

    h2 = _norm_fwd(x2, scale[2], shift[2], plus_one=True, out_dtype=BF16, name="ada_fwd2")
    a = _matmul(h2, w_a, name="mla_a")
    a_q, a_kv = a[:, :MLA_Q_RANK], a[:, MLA_Q_RANK:MLA_Q_RANK + MLA_KV_RANK]
    kr1 = a[:, MLA_Q_RANK + MLA_KV_RANK:MLA_Q_RANK + MLA_KV_RANK + MLA_ROPE_HALF]
    kr2 = a[:, MLA_Q_RANK + MLA_KV_RANK + MLA_ROPE_HALF:MLA_Q_RANK + MLA_KV_RANK + MLA_ROPE_DIM]
    cq = _norm_fwd(a_q, g_q, jnp.zeros_like(g_q), plus_one=False, out_dtype=BF16, name="mla_norm_q")
    ckv = _norm_fwd(a_kv, g_kv, jnp.zeros_like(g_kv), plus_one=False, out_dtype=BF16, name="mla_norm_kv")
    qf = _matmul(cq, w_uq, name="mla_uq")
    kvf = _matmul(ckv, w_ukv, out_dtype=BF16, name="mla_ukv")
    mq = _rope_heads(qf, tab_a, tab_b, tab_c, out_dtype=BF16, name="rope_q")
    kk1, kk2 = _rope(kr1, kr2, cos16, sin16, negate=False, name="rope_k")
    k_tail = jnp.concatenate([kk1, kk2, z32], axis=1).astype(BF16)
    mk = jnp.concatenate([kvf[:, :nq].reshape(s, N_HEADS, HEAD_DIM),
                          jnp.broadcast_to(k_tail[:, None, :], (s, N_HEADS, HEAD_DIM))], axis=-1).reshape(s, N_HEADS * LANES)
    mla_cfg = dict(qc=0, kc=0, vc=N_HEADS // 2, width=LANES)
    mvt = _matmul(w_ukv[:, nq:], ckv, ta=True, tb=True, out_dtype=BF16, name="mla_ukv_vt")
    mo, mla_lse = _attn_fwd(mq, mk, _key_blocks(mvt, att_blk), None, qc=0, kc=0, width=LANES, name="mla_attn_fwd")
    mla_res = (mq, mk, kvf, None, mo, mla_lse)
    y2 = _matmul(mo, w_mla_o, name="mla_out")
    x3 = _resid_fwd(x2, y2, gate[2], name="resid_fwd2")

    x4, ffn1_res = ffn_fwd(x3, 1, 3)

    loss_vec, dx4, d_final_g = _final_loss(x4, final_g.reshape(1, d), target, name="final_loss")
    loss = lax.psum(loss_vec[0, 0], ("x", "y", "c"))

    grads = {}
    dmod = [None] * 4

    def ffn_bwd(dx_out, xin, li, sub, res):
        hh, u, g, y = res
        dy, dgate = _resid_bwd(dx_out, y, gate[sub], name=f"resid_bwd{sub}")
        gw_out = _matmul(g, dy, ta=True, out_dtype=BF16, name=f"ffn_down_dw{li}")
        dg = _matmul(dy, w_ffn_out[li], tb=True, name=f"ffn_down_dx{li}")
        du, dcw, dcb = _conv_gate_bwd(u, conv_w[li], conv_b[li:li + 1], dg, name=f"conv_bwd{li}")
        gw_in = _matmul(hh, du, ta=True, out_dtype=BF16, name=f"ffn_up_dw{li}")
        dh = _matmul(du, w_ffn_in[li], tb=True, out_dtype=BF16, name=f"ffn_up_dx{li}")
        dx_in, dscale, dshift = _norm_bwd(xin, scale[sub], dh, dx_out, plus_one=True, name=f"ada_bwd{sub}")
        dmod[sub] = jnp.concatenate([dshift, dscale, dgate], axis=1)
        return dx_in, gw_in, dcw, dcb, gw_out

    dx3, gw_in1, dcw1, dcb1, gw_out1 = ffn_bwd(dx4, x3, 1, 3, ffn1_res)

    dy2, dgate2 = _resid_bwd(dx3, y2, gate[2], name="resid_bwd2")
    grads["mla_w_o"] = _matmul(mo, dy2, ta=True, out_dtype=BF16, name="mla_out_dw")[None]
    dmo = _matmul(dy2, w_mla_o, tb=True, out_dtype=BF16, name="mla_out_dx")
    dmq, dmk, dmv, _ = _attention_bwd(mla_res, dmo, dq_mult=1.0 / LOG2E, dk_mult=1.0 / LOG2E, out_dtype=F32,
                                      name="mla_attn", **mla_cfg)
    dqf = _rope_heads(dmq, tab_a, -tab_b, -tab_c, out_dtype=BF16, name="rope_q_bwd")
    g_uq = _matmul(cq, dqf, ta=True, out_dtype=BF16, name="mla_uq_dw")
    dcq = _matmul(dqf, w_uq, tb=True, name="mla_uq_dx")
    dmk3 = dmk.reshape(s, N_HEADS, LANES)
    dkr = _group_sum(dmk, name="mla_krope_sum")
    dkr1, dkr2 = _rope(dkr[:, HEAD_DIM:HEAD_DIM + MLA_ROPE_HALF], dkr[:, HEAD_DIM + MLA_ROPE_HALF:HEAD_DIM + MLA_ROPE_DIM],
                       cos16, sin16, negate=True, name="rope_k_bwd")
    dkvf = jnp.concatenate([dmk3[:, :, :HEAD_DIM].reshape(s, nq).astype(BF16), dmv.astype(BF16)], axis=1)
    g_ukv = _matmul(ckv, dkvf, ta=True, out_dtype=BF16, name="mla_ukv_dw")
    dckv = _matmul(dkvf, w_ukv, tb=True, name="mla_ukv_dx")
    da_q, dg_q, _ = _norm_bwd(a_q, g_q, dcq, None, plus_one=False, name="mla_norm_q_bwd")
    da_kv, dg_kv, _ = _norm_bwd(a_kv, g_kv, dckv, None, plus_one=False, name="mla_norm_kv_bwd")
    da = jnp.concatenate([da_q, da_kv, dkr1, dkr2, jnp.zeros((s, w_a.shape[1] - 672), F32)], axis=1).astype(BF16)
    grads["mla_w_a"] = _matmul(h2, da, ta=True, out_dtype=BF16, name="mla_a_dw")[None, :, :672]
    dh2 = _matmul(da, w_a, tb=True, out_dtype=BF16, name="mla_a_dx")
    dx2, dscale2, dshift2 = _norm_bwd(x2, scale[2], dh2, dx3, plus_one=True, name="ada_bwd2")
    dmod[2] = jnp.concatenate([dshift2, dscale2, dgate2], axis=1)
    grads["mla_w_uq"] = g_uq.reshape(MLA_Q_RANK, N_HEADS, LANES)[:, :, :HEAD_DIM + MLA_ROPE_DIM].reshape(1, MLA_Q_RANK, -1)
    grads["mla_w_ukv"] = _per_head([g_ukv[:, :nq], g_ukv[:, nq:]], MLA_KV_RANK)[None]
    grads["mla_g_q"], grads["mla_g_kv"] = dg_q, dg_kv

    dx1, gw_in0, dcw0, dcb0, gw_out0 = ffn_bwd(dx2, x1, 0, 1, ffn0_res)
    grads["ffn_w_in"] = jnp.stack([gw_in0, gw_in1])
    grads["ffn_w_out"] = jnp.stack([gw_out0, gw_out1])
    grads["ffn_conv_w"] = jnp.stack([dcw0, dcw1])
    g_conv_b = jnp.concatenate([dcb0, dcb1], axis=0)

    dy0, dgate0 = _resid_bwd(dx1, y0, gate[0], name="resid_bwd0")
    grads["fox_w_o"] = _matmul(fo, dy0, ta=True, out_dtype=BF16, name="fox_out_dw")[None]
    dfo = _matmul(dy0, w_fox_o, tb=True, out_dtype=BF16, name="fox_out_dx")
    dfq, dfk, dfv, dcum = _attention_bwd(fox_res, dfo, dq_mult=fox_scale, dk_mult=1.0 / LOG2E, out_dtype=BF16,
                                         name="fox_attn", **fox_cfg)
    dfl, g_b_f = _fox_gate_bwd(fl, fox_b_f, dcum, name="fox_gate_bwd")
    dproj = jnp.concatenate([dfq.astype(BF16), dfk, dfv, _pad_axis(dfl, 1, LANES).astype(BF16)], axis=1)
    grads["fox_w_in"] = _matmul(h0, dproj, ta=True, out_dtype=BF16, name="fox_proj_dw")[None, :, :3 * nq + N_HEADS]
    dh0 = _matmul(dproj, w_fox_in, tb=True, out_dtype=BF16, name="fox_proj_dx")
    dx0, dscale0, dshift0 = _norm_bwd(x0, scale[0], dh0, dx1, plus_one=True, name="ada_bwd0")
    dmod[0] = jnp.concatenate([dshift0, dscale0, dgate0], axis=1)

    dmod_send = _pad_axis(jnp.stack(dmod, axis=0).reshape(4, N_DEV, cols).transpose(1, 0, 2), 1, 8)
    dmod_recv, = _exchange([dmod_send], same_src=False, name="scatter_dmod")
    dmod_all = dmod_recv[:, :4]
    dmod_pad = _pad_axis(dmod_all, 0, LANES)
    g_ada_w = jnp.stack([_matmul(silu_c, dmod_pad[:, i], ta=True, name=f"ada_dw{i}") for i in range(4)])
    grads["ada_w"] = g_ada_w.reshape(ada_w.shape)
    grads["ada_b"] = _sum_slabs(dmod_recv, name="ada_db")[:4].reshape(ada_b.shape)

    sharded = list(_BIG) + list(_SMALL)
    axes = {**_BIG, **_SMALL}
    recv = _exchange([_full_to_chunks(grads[n], axes[n]) for n in sharded], same_src=False, name="scatter_grads")
    grads.update(dict(zip(sharded, recv)))
    repl = _exchange([g_b_f, g_conv_b, d_final_g], same_src=True, name="gather_repl_grads")
    grads.update(dict(zip(_REPL, repl)))

    grad_out, deltas, new_m, new_v = {}, {}, {}, {}
    for n in order:
        grad_out[n], deltas[n], new_m[n], new_v[n] = _adamw(
            weights[n], grads[n], mom_m[n], mom_v[n], slabs=n in axes or n in _REPL, name=f"adamw_{n}")

    grad_x = dx0[None]
    return (loss, grad_x, *[grad_out[n] for n in order], *[deltas[n] for n in order],
            *[new_m[n] for n in order], *[new_v[n] for n in order])
```

```python
import jax
import jax.numpy as jnp
from jax import lax
from jax.experimental import pallas as pl
from jax.experimental.pallas import tpu as pltpu

F32 = jnp.float32
BF16 = jnp.bfloat16
HIGHEST = lax.Precision.HIGHEST

N_DEV = 8
D_MODEL = 1024
N_HEADS = 16
HEAD_DIM = 64
MLA_ROPE_HALF = 16
MLA_Q_RANK = 384
MLA_KV_RANK = 256
MLA_ROPE_DIM = 32
NORM_EPS = 1e-6
ROPE_BASE = 10000.0
ADAM_LR = 0.001
ADAM_B1 = 0.9
ADAM_B2 = 0.999
ADAM_EPS = 1e-08
ADAM_WD = 0.01
ADAM_STEP = 10

LANES = 128
VMEM_LIMIT_BYTES = 56 * 1024 * 1024
ROW_BLOCK = 512
ATT_BLOCK = 512
CONV_ROWS = 2048
MM_BM, MM_BN, MM_BK = 512, 1408, 2048
MM_K_WHOLE = 3328
LOG2E = 1.4426950408889634


def _params(*sem):
    return pltpu.CompilerParams(dimension_semantics=sem or None, vmem_limit_bytes=VMEM_LIMIT_BYTES)


def _blk(dim, pref):
    if dim <= pref:
        return dim
    b = pref - pref % LANES
    while b >= LANES:
        if dim % b == 0:
            return b
        b -= LANES
    raise ValueError(f"no block for {dim}")


def _row_blk(rows, pref):
    if rows <= pref:
        return rows
    for b in range(pref - pref % 8, 7, -8):
        if rows % b == 0:
            return b
    return rows


def _pad_axis(a, axis, mult):
    pad = (-a.shape[axis]) % mult
    if pad == 0:
        return a
    widths = [(0, 0)] * a.ndim
    widths[axis] = (0, pad)
    return jnp.pad(a, widths)


def _matmul(a, b, *, ta=False, tb=False, out_dtype=F32, col_scale=None, name):
    m, k = (a.shape[1], a.shape[0]) if ta else a.shape
    n = b.shape[0] if tb else b.shape[1]
    assert (b.shape[1] if tb else b.shape[0]) == k, (a.shape, b.shape, ta, tb)
    bm, bn = _blk(m, 2 * MM_BM), _blk(n, MM_BN)
    if ta and bm < MM_BM and m % MM_BN == 0:
        bm = MM_BN
    bk = k if k <= MM_K_WHOLE else _blk(k, MM_BK)
    nk = k // bk
    dims = (((0 if ta else 1,), (1 if tb else 0,)), ((), ()))
    has_scale = col_scale is not None
    use_acc = nk > 1 and (out_dtype != F32 or has_scale)

    def body(*refs):
        a_ref, b_ref = refs[0], refs[1]
        s_ref = refs[2] if has_scale else None
        o_ref = refs[3] if has_scale else refs[2]
        acc_ref = refs[-1] if use_acc else o_ref
        kk = pl.program_id(2)
        part = lax.dot_general(a_ref[...].astype(BF16), b_ref[...].astype(BF16), dims, preferred_element_type=F32)

        def finish(val):
            if has_scale:
                val = val * s_ref[...]
            o_ref[...] = val.astype(out_dtype)

        if nk == 1:
            finish(part)
            return

        @pl.when(kk == 0)
        def _():
            acc_ref[...] = part

        @pl.when(kk > 0)
        def _():
            acc_ref[...] += part

        if use_acc:
            @pl.when(kk == nk - 1)
            def _():
                finish(acc_ref[...])

    a_spec = pl.BlockSpec((bk, bm), lambda i, j, kk: (kk, i)) if ta else pl.BlockSpec((bm, bk), lambda i, j, kk: (i, kk))
    b_spec = pl.BlockSpec((bn, bk), lambda i, j, kk: (j, kk)) if tb else pl.BlockSpec((bk, bn), lambda i, j, kk: (kk, j))
    return pl.pallas_call(
        body, name=name, grid=(m // bm, n // bn, nk),
        in_specs=[a_spec, b_spec] + ([pl.BlockSpec((1, bn), lambda i, j, kk: (0, j))] if has_scale else []),
        out_specs=pl.BlockSpec((bm, bn), lambda i, j, kk: (i, j)),
        out_shape=jax.ShapeDtypeStruct((m, n), out_dtype),
        scratch_shapes=[pltpu.VMEM((bm, bn), F32)] if use_acc else [],
        compiler_params=_params("parallel", "parallel", "arbitrary"),
    )(*([a, b] + ([col_scale] if has_scale else [])))


def _norm_fwd(x, mul, add, *, plus_one, out_dtype, name):
    s, n = x.shape
    bs = _blk(s, ROW_BLOCK)

    def body(x_ref, m_ref, a_ref, o_ref):
        xv = x_ref[...]
        r = lax.rsqrt(jnp.mean(xv * xv, axis=-1, keepdims=True) + NORM_EPS)
        mv = m_ref[...] + 1.0 if plus_one else m_ref[...]
        o_ref[...] = (xv * r * mv + a_ref[...]).astype(out_dtype)

    row = pl.BlockSpec((bs, n), lambda i: (i, 0))
    vec = pl.BlockSpec((1, n), lambda i: (0, 0))
    return pl.pallas_call(
        body, name=name, grid=(s // bs,), in_specs=[row, vec, vec], out_specs=row,
        out_shape=jax.ShapeDtypeStruct((s, n), out_dtype), compiler_params=_params("parallel"),
    )(x, mul, add)


def _norm_bwd(x, mul, dy, dres, *, plus_one, name):
    s, n = x.shape
    bs = _blk(s, ROW_BLOCK)
    has_res = dres is not None

    def body(*refs):
        if has_res:
            x_ref, m_ref, dy_ref, dres_ref, dx_ref, dm_ref, da_ref = refs
        else:
            x_ref, m_ref, dy_ref, dx_ref, dm_ref, da_ref = refs
        xv = x_ref[...]
        dyv = dy_ref[...].astype(F32)
        r = lax.rsqrt(jnp.mean(xv * xv, axis=-1, keepdims=True) + NORM_EPS)
        xn = xv * r
        mv = m_ref[...] + 1.0 if plus_one else m_ref[...]
        g = dyv * mv
        dx = r * (g - xn * jnp.mean(g * xn, axis=-1, keepdims=True))
        if has_res:
            dx = dx + dres_ref[...]
        dx_ref[...] = dx

        @pl.when(pl.program_id(0) == 0)
        def _():
            dm_ref[...] = jnp.zeros_like(dm_ref)
            da_ref[...] = jnp.zeros_like(da_ref)

        dm_ref[...] += jnp.sum(dyv * xn, axis=0, keepdims=True)
        da_ref[...] += jnp.sum(dyv, axis=0, keepdims=True)

    row = pl.BlockSpec((bs, n), lambda i: (i, 0))
    vec = pl.BlockSpec((1, n), lambda i: (0, 0))
    ins = [x, mul, dy] + ([dres] if has_res else [])
    return pl.pallas_call(
        body, name=name, grid=(s // bs,),
        in_specs=[row, vec, row] + ([row] if has_res else []), out_specs=[row, vec, vec],
        out_shape=[jax.ShapeDtypeStruct((s, n), F32), jax.ShapeDtypeStruct((1, n), F32), jax.ShapeDtypeStruct((1, n), F32)],
        compiler_params=_params("arbitrary"),
    )(*ins)


def _resid_fwd(x, y, gate, *, name):
    s, n = x.shape
    bs = _blk(s, ROW_BLOCK)

    def body(x_ref, y_ref, g_ref, o_ref):
        o_ref[...] = x_ref[...] + g_ref[...] * y_ref[...]

    row = pl.BlockSpec((bs, n), lambda i: (i, 0))
    vec = pl.BlockSpec((1, n), lambda i: (0, 0))
    return pl.pallas_call(
        body, name=name, grid=(s // bs,), in_specs=[row, row, vec], out_specs=row,
        out_shape=jax.ShapeDtypeStruct((s, n), F32), compiler_params=_params("parallel"),
    )(x, y, gate)


def _resid_bwd(dx, y, gate, *, name):
    s, n = dx.shape
    bs = _blk(s, ROW_BLOCK)

    def body(dx_ref, y_ref, g_ref, dy_ref, dg_ref):
        dxv = dx_ref[...]
        dy_ref[...] = (g_ref[...] * dxv).astype(BF16)

        @pl.when(pl.program_id(0) == 0)
        def _():
            dg_ref[...] = jnp.zeros_like(dg_ref)

        dg_ref[...] += jnp.sum(dxv * y_ref[...], axis=0, keepdims=True)

    row = pl.BlockSpec((bs, n), lambda i: (i, 0))
    vec = pl.BlockSpec((1, n), lambda i: (0, 0))
    return pl.pallas_call(
        body, name=name, grid=(s // bs,), in_specs=[row, row, vec], out_specs=[row, vec],
        out_shape=[jax.ShapeDtypeStruct((s, n), BF16), jax.ShapeDtypeStruct((1, n), F32)],
        compiler_params=_params("arbitrary"),
    )(dx, y, gate)


def _final_loss(x, g, target, *, name):
    s, n = x.shape
    bs = _blk(s, ROW_BLOCK)

    def body(x_ref, g_ref, t_ref, loss_ref, dx_ref, dg_ref):
        xv = x_ref[...]
        r = lax.rsqrt(jnp.mean(xv * xv, axis=-1, keepdims=True) + NORM_EPS)
        xn = xv * r
        gv = g_ref[...]
        err = xn * gv - t_ref[...]
        dout = err * (1.0 / n)
        gg = dout * gv
        dx_ref[...] = r * (gg - xn * jnp.mean(gg * xn, axis=-1, keepdims=True))

        @pl.when(pl.program_id(0) == 0)
        def _():
            loss_ref[...] = jnp.zeros_like(loss_ref)
            dg_ref[...] = jnp.zeros_like(dg_ref)

        part = jnp.sum(jnp.sum(err * err, axis=-1, keepdims=True), axis=0, keepdims=True) * (0.5 / n)
        loss_ref[...] += jnp.broadcast_to(part, loss_ref.shape)
        dg_ref[...] += jnp.sum(dout * xn, axis=0, keepdims=True)

    row = pl.BlockSpec((bs, n), lambda i: (i, 0))
    vec = pl.BlockSpec((1, n), lambda i: (0, 0))
    return pl.pallas_call(
        body, name=name, grid=(s // bs,), in_specs=[row, vec, row],
        out_specs=[pl.BlockSpec((1, LANES), lambda i: (0, 0)), row, vec],
        out_shape=[jax.ShapeDtypeStruct((1, LANES), F32), jax.ShapeDtypeStruct((s, n), F32), jax.ShapeDtypeStruct((1, n), F32)],
        compiler_params=_params("arbitrary"),
    )(x, g, target)


def _lane_lt64(shape):
    return lax.broadcasted_iota(jnp.int32, shape, 1) < HEAD_DIM


def _keep_low(x):
    return jnp.where(_lane_lt64(x.shape), x.astype(F32), 0.0).astype(x.dtype)


def _keep_high(x):
    return jnp.where(_lane_lt64(x.shape), 0.0, x.astype(F32)).astype(x.dtype)


def _lane_merge(a, b):
    n = max(a.shape[0], b.shape[0])
    return jnp.where(_lane_lt64((n, LANES)), a, b)


def _pair(x, width, masked):
    if width == HEAD_DIM:
        return (_keep_low(x), _keep_high(x)) if masked else (x, x)
    return x[:, :LANES], x[:, LANES:]


def _qk_t(a, b):
    return lax.dot_general(a, b, (((1,), (1,)), ((), ())), preferred_element_type=F32)


def _attn_specs(s, blk, width, cols, resident):
    w = 2 * width
    if resident:
        return pl.BlockSpec((s, w), lambda p, i: (0, cols + p))
    return pl.BlockSpec((blk, w), lambda p, i: (i, cols + p))


BIAS_TERMS = 3


def _key_blocks(vt, blk):
    return vt.reshape(vt.shape[0], vt.shape[1] // blk, blk).transpose(1, 0, 2)


def _attn_fwd(q, k, vt, kbl, *, qc, kc, width, name):
    s = q.shape[0]
    blk = _blk(s, ATT_BLOCK)
    nb = s // blk
    has_bias = kbl is not None
    assert has_bias == (width == HEAD_DIM)

    def body(*refs):
        if has_bias:
            q_ref, k_ref, vt_ref, kbl_ref, o_ref, lse_ref = refs
        else:
            q_ref, k_ref, vt_ref, o_ref, lse_ref = refs
        i = pl.program_id(1)
        q2 = q_ref[...]
        if has_bias:
            lane = lax.broadcasted_iota(jnp.int32, (blk, LANES), 1)
            qf = q2.astype(F32)
            qh = (jnp.where(lane < HEAD_DIM, qf, jnp.where(lane < HEAD_DIM + BIAS_TERMS, 1.0, 0.0)).astype(BF16),
                  jnp.where(lane >= HEAD_DIM, qf, jnp.where(lane < BIAS_TERMS, 1.0, 0.0)).astype(BF16))
        else:
            qh = (q2[:, :LANES], q2[:, LANES:])

        def step(j, carry, nblk, diag):
            rows = pl.ds(pl.multiple_of(j * blk, blk), nblk * blk)
            vt1 = [jnp.concatenate([vt_ref[j + b], jnp.ones((16, blk), BF16)], axis=0) for b in range(nblk)]
            k2 = k_ref[rows, :]
            if has_bias:
                low = _lane_lt64(k2.shape)
                kf, bf = k2.astype(F32), kbl_ref[rows, :].astype(F32)
                kh = (jnp.where(low, kf, bf).astype(BF16), jnp.where(low, bf, kf).astype(BF16))
            else:
                kh = (k2[:, :LANES], k2[:, LANES:])
            out = []
            for hd in range(2):
                m, acc = carry[hd]
                st = _qk_t(kh[hd], qh[hd])
                if diag:
                    row = lax.broadcasted_iota(jnp.int32, (blk, blk), 0)
                    colq = lax.broadcasted_iota(jnp.int32, (blk, blk), 1)
                    st = jnp.where(row <= colq, st, -1e30)
                m_new = jnp.maximum(m, jnp.max(st, axis=0, keepdims=True))
                alpha = jnp.exp2(m - m_new)
                pt = jnp.exp2(st - m_new).astype(BF16)
                acc = alpha * acc
                for b in range(nblk):
                    acc = acc + jnp.dot(vt1[b], pt[b * blk:(b + 1) * blk], preferred_element_type=F32)
                out.append((m_new, acc))
            return tuple(out)

        one = (jnp.full((1, blk), -1e30, F32), jnp.zeros((LANES + 16, blk), F32))
        carry = lax.fori_loop(0, i // 8, lambda j, c: step(8 * j, c, 8, False), (one, one))
        carry = lax.fori_loop(0, (i % 8) // 4, lambda _, c: step(i - i % 8, c, 4, False), carry)
        carry = lax.fori_loop(0, (i % 4) // 2, lambda _, c: step(i - i % 4, c, 2, False), carry)
        carry = lax.fori_loop(0, i % 2, lambda _, c: step(i - 1, c, 1, False), carry)
        (ma, acca), (mb, accb) = step(i, carry, 1, True)
        la = jnp.max(acca[LANES:LANES + 8], axis=0, keepdims=True)
        lb = jnp.max(accb[LANES:LANES + 8], axis=0, keepdims=True)
        acca, accb = acca[0:LANES], accb[0:LANES]
        low = lax.broadcasted_iota(jnp.int32, (LANES, blk), 0) < HEAD_DIM
        o_ref[...] = jnp.where(low, acca / la, accb / lb).T
        lse_ref[0, 0] = ma + jnp.log(la) * LOG2E
        lse_ref[1, 0] = mb + jnp.log(lb) * LOG2E

    ins = [q, k, vt] + ([kbl] if has_bias else [])
    return pl.pallas_call(
        body, name=name, grid=(N_HEADS // 2, nb),
        in_specs=[_attn_specs(s, blk, width, qc, False), _attn_specs(s, blk, width, kc, True),
                  pl.BlockSpec((nb, LANES, blk), lambda p, i: (0, p, 0))]
                 + ([_attn_specs(s, blk, HEAD_DIM, 0, True)] if has_bias else []),
        out_specs=[pl.BlockSpec((blk, LANES), lambda p, i: (i, p)), pl.BlockSpec((2, 1, 1, blk), lambda p, i: (p, i, 0, 0))],
        out_shape=[jax.ShapeDtypeStruct((s, N_HEADS * HEAD_DIM), F32), jax.ShapeDtypeStruct((N_HEADS, nb, 1, blk), F32)],
        compiler_params=_params("parallel", "parallel"),
    )(*ins)


def _bias_lane_terms(kb2):
    terms, rest = [], kb2
    for _ in range(BIAS_TERMS):
        t = lax.reduce_precision(rest, 8, 7)
        terms.append(t.astype(BF16))
        rest = rest - t
    place = [[0.0] * (N_HEADS * HEAD_DIM) for _ in range(LANES)]
    for t in range(BIAS_TERMS):
        for h in range(N_HEADS):
            place[t * N_HEADS + h][(h // 2) * LANES + (HEAD_DIM if h % 2 == 0 else 0) + t] = 1.0
    return _matmul(_pad_axis(jnp.concatenate(terms, axis=1), 1, LANES), jnp.asarray(place, BF16), out_dtype=BF16,
                   name="fox_bias_lanes")


def _attn_delta(o, do, *, name):
    s, n = o.shape
    bs = _blk(s, ROW_BLOCK)

    def body(o_ref, do_ref, d_ref):
        for g in range(n // LANES):
            prod = do_ref[:, g * LANES:(g + 1) * LANES].astype(F32) * o_ref[:, g * LANES:(g + 1) * LANES]
            low = _lane_lt64(prod.shape)
            d_ref[:, g * LANES:(g + 1) * LANES] = _lane_merge(
                jnp.sum(jnp.where(low, prod, 0.0), axis=-1, keepdims=True),
                jnp.sum(jnp.where(low, 0.0, prod), axis=-1, keepdims=True))

    row = pl.BlockSpec((bs, n), lambda i: (i, 0))
    return pl.pallas_call(
        body, name=name, grid=(s // bs,), in_specs=[row, row], out_specs=row,
        out_shape=jax.ShapeDtypeStruct((s, n), F32), compiler_params=_params("parallel"),
    )(o, do)


def _attn_bwd(q, k, v, kb_col, do, lse_row, delta_row, *, qc, kc, vc, width, dq_mult, dk_mult, out_dtype, name):
    s = q.shape[0]
    blk = _blk(s, ATT_BLOCK)
    nb = s // blk
    has_bias = kb_col is not None

    def body(*refs):
        if has_bias:
            q_ref, k_ref, v_ref, kb_ref, do_ref, lse_ref, dl_ref, dk_ref, dv_ref, db_ref, dq_ref, dr_ref = refs
        else:
            q_ref, k_ref, v_ref, do_ref, lse_ref, dl_ref, dk_ref, dv_ref, db_ref, dq_ref = refs
        j = pl.program_id(1)

        @pl.when(j == 0)
        def _():
            dq_ref[...] = jnp.zeros_like(dq_ref)
            if has_bias:
                dr_ref[...] = jnp.zeros_like(dr_ref)

        kh = _pair(k_ref[...], width, True)
        v2 = v_ref[...]
        vh = (_keep_low(v2), _keep_high(v2))
        if has_bias:
            kb2 = kb_ref[0]
            kbh = (kb2[:, 0:1], kb2[:, 1:2])

        def step(i, carry, nblk, diag):
            rows = pl.ds(pl.multiple_of(i * blk, blk), nblk * blk)
            qh = _pair(q_ref[rows, :], width, False)
            doi = do_ref[rows, :]
            out, dq_parts = [], []
            for hd in range(2):
                dk, dvv, db = carry[hd]
                st = _qk_t(kh[hd], qh[hd])
                if has_bias:
                    st = st + kbh[hd]
                if diag:
                    row = lax.broadcasted_iota(jnp.int32, (blk, blk), 0)
                    colq = lax.broadcasted_iota(jnp.int32, (blk, blk), 1)
                    st = jnp.where(row <= colq, st, -1e30)
                lse_i = jnp.concatenate([lse_ref[hd, i + b] for b in range(nblk)], axis=1)
                delta_i = jnp.concatenate([dl_ref[hd, i + b] for b in range(nblk)], axis=1)
                pt = jnp.exp2(st - lse_i)
                dvv = dvv + jnp.dot(pt.astype(BF16), doi, preferred_element_type=F32)
                dst = pt * (_qk_t(vh[hd], doi) - delta_i)
                dsb = dst.astype(BF16)
                dk = dk + jnp.dot(dsb, qh[hd], preferred_element_type=F32)
                db = db + jnp.sum(dst, axis=-1, keepdims=True)
                dq_parts.append(lax.dot_general(dsb, kh[hd], (((0,), (0,)), ((), ())), preferred_element_type=F32))
                if has_bias:
                    rsum = jnp.sum(dst, axis=0, keepdims=True)
                    for b in range(nblk):
                        dr_ref[hd, i + b] += rsum[:, b * blk:(b + 1) * blk]
                out.append((dk, dvv, db))
            if width == HEAD_DIM:
                dq_ref[rows, :] += (dq_parts[0] + dq_parts[1]) * dq_mult
            else:
                dq_ref[rows, 0:LANES] += dq_parts[0] * dq_mult
                dq_ref[rows, LANES:2 * LANES] += dq_parts[1] * dq_mult
            return tuple(out)

        one = (jnp.zeros((blk, LANES), F32), jnp.zeros((blk, LANES), F32), jnp.zeros((blk, 1), F32))
        carry = step(j, (one, one), 1, True)
        rest = nb - 1 - j
        carry = lax.fori_loop(0, rest // 8, lambda t, c: step(j + 1 + 8 * t, c, 8, False), carry)
        carry = lax.fori_loop(0, (rest % 8) // 4, lambda _, c: step(nb - rest % 8, c, 4, False), carry)
        carry = lax.fori_loop(0, (rest % 4) // 2, lambda _, c: step(nb - rest % 4, c, 2, False), carry)
        (dka, dva, dba), (dkb, dvb, dbb) = lax.fori_loop(0, rest % 2, lambda _, c: step(nb - 1, c, 1, False), carry)
        if width == HEAD_DIM:
            dk = _lane_merge(dka, dkb)
        else:
            dk = jnp.concatenate([dka, dkb], axis=1)
        dk_ref[...] = (dk * dk_mult).astype(out_dtype)
        dv_ref[...] = _lane_merge(dva, dvb).astype(out_dtype)
        db_ref[...] = _lane_merge(dba, dbb)

    stat = pl.BlockSpec((blk, LANES), lambda p, jj: (jj, p))
    rows = pl.BlockSpec((2, nb, 1, blk), lambda p, jj: (p, 0, 0, 0))
    ins = [q, k, v] + ([kb_col] if has_bias else []) + [do, lse_row, delta_row]
    return pl.pallas_call(
        body, name=name, grid=(N_HEADS // 2, nb),
        in_specs=[_attn_specs(s, blk, width, qc, True), _attn_specs(s, blk, width, kc, False),
                  _attn_specs(s, blk, HEAD_DIM, vc, False)]
                 + ([pl.BlockSpec((1, blk, 2), lambda p, jj: (p, jj, 0))] if has_bias else [])
                 + [pl.BlockSpec((s, LANES), lambda p, jj: (0, p)), rows, rows],
        out_specs=[pl.BlockSpec((blk, 2 * width), lambda p, jj: (jj, p)), stat, stat,
                   pl.BlockSpec((s, 2 * width), lambda p, jj: (0, p))] + ([rows] if has_bias else []),
        out_shape=[jax.ShapeDtypeStruct((s, N_HEADS * width), out_dtype), jax.ShapeDtypeStruct((s, N_HEADS * HEAD_DIM), out_dtype),
                   jax.ShapeDtypeStruct((s, N_HEADS * HEAD_DIM), F32), jax.ShapeDtypeStruct((s, N_HEADS * width), F32)]
                  + ([jax.ShapeDtypeStruct((N_HEADS, nb, 1, blk), F32)] if has_bias else []),
        compiler_params=_params("parallel", "arbitrary"),
    )(*ins)


def _head_stat(t):
    return t[:, ::HEAD_DIM]


def _stat_rows(t16, blk):
    s = t16.shape[0]
    return t16.T.reshape(N_HEADS, s // blk, 1, blk)


def _attention_bwd(res, do, *, qc, kc, vc, width, dq_mult, dk_mult, out_dtype, name):
    q, k, v, bias, o, lse_row = res
    s = q.shape[0]
    blk = _blk(s, ATT_BLOCK)
    kb_col = None if bias is None else bias.reshape(s, N_HEADS // 2, 2).transpose(1, 0, 2)
    delta_row = _stat_rows(_head_stat(_attn_delta(o, do, name=name + "_delta")), blk)
    outs = _attn_bwd(q, k, v, kb_col, do, lse_row, delta_row, qc=qc, kc=kc, vc=vc, width=width, dq_mult=dq_mult,
                     dk_mult=dk_mult, out_dtype=out_dtype, name=name + "_bwd")
    dk, dv, dcol, dq = outs[:4]
    if bias is None:
        return dq, dk, dv, None
    return dq, dk, dv, outs[4].reshape(N_HEADS, s).T - _head_stat(dcol)


def _fox_gate_fwd(fl, bf, *, name):
    s, n = fl.shape
    bs = _blk(s, ROW_BLOCK)

    def body(fl_ref, bf_ref, cum_ref, carry_ref):
        @pl.when(pl.program_id(0) == 0)
        def _():
            carry_ref[...] = jnp.zeros_like(carry_ref)

        z = fl_ref[...] + bf_ref[...]
        lf = jnp.minimum(z, 0.0) - jnp.log1p(jnp.exp(-jnp.abs(z)))
        row = lax.broadcasted_iota(jnp.int32, (bs, bs), 0)
        col = lax.broadcasted_iota(jnp.int32, (bs, bs), 1)
        tri = (col <= row).astype(F32)
        cum_ref[...] = jnp.dot(tri, lf, preferred_element_type=F32, precision=HIGHEST) + carry_ref[...]
        carry_ref[...] += jnp.sum(lf, axis=0, keepdims=True)

    return pl.pallas_call(
        body, name=name, grid=(s // bs,),
        in_specs=[pl.BlockSpec((bs, n), lambda i: (i, 0)), pl.BlockSpec((1, n), lambda i: (0, 0))],
        out_specs=pl.BlockSpec((bs, n), lambda i: (i, 0)),
        out_shape=jax.ShapeDtypeStruct((s, n), F32), scratch_shapes=[pltpu.VMEM((1, n), F32)],
        compiler_params=_params("arbitrary"),
    )(fl, bf)


def _fox_gate_bwd(fl, bf, dcum, *, name):
    s, n = fl.shape
    bs = _blk(s, ROW_BLOCK)
    nb = s // bs

    def body(fl_ref, bf_ref, dc_ref, dz_ref, dbf_ref, carry_ref):
        @pl.when(pl.program_id(0) == 0)
        def _():
            carry_ref[...] = jnp.zeros_like(carry_ref)
            dbf_ref[...] = jnp.zeros_like(dbf_ref)

        dc = dc_ref[...]
        row = lax.broadcasted_iota(jnp.int32, (bs, bs), 0)
        col = lax.broadcasted_iota(jnp.int32, (bs, bs), 1)
        tri = (col >= row).astype(F32)
        dlf = jnp.dot(tri, dc, preferred_element_type=F32, precision=HIGHEST) + carry_ref[...]
        carry_ref[...] += jnp.sum(dc, axis=0, keepdims=True)
        z = fl_ref[...] + bf_ref[...]
        dz = dlf / (1.0 + jnp.exp(z))
        dz_ref[...] = dz
        dbf_ref[...] += jnp.sum(dz, axis=0, keepdims=True)

    rev = pl.BlockSpec((bs, n), lambda i: (nb - 1 - i, 0))
    vec = pl.BlockSpec((1, n), lambda i: (0, 0))
    return pl.pallas_call(
        body, name=name, grid=(nb,), in_specs=[rev, vec, rev], out_specs=[rev, vec],
        out_shape=[jax.ShapeDtypeStruct((s, n), F32), jax.ShapeDtypeStruct((1, n), F32)],
        scratch_shapes=[pltpu.VMEM((1, n), F32)], compiler_params=_params("arbitrary"),
    )(fl, bf, dcum)


def _rope(x1, x2, cos, sin, *, negate, name):
    s, n = x1.shape
    bs = _blk(s, ROW_BLOCK)

    def body(a_ref, b_ref, c_ref, s_ref, o1_ref, o2_ref):
        a, b, cv = a_ref[...], b_ref[...], c_ref[...]
        sv = -s_ref[...] if negate else s_ref[...]
        o1_ref[...] = a * cv - b * sv
        o2_ref[...] = b * cv + a * sv

    row = pl.BlockSpec((bs, n), lambda i: (i, 0))
    return pl.pallas_call(
        body, name=name, grid=(s // bs,), in_specs=[row] * 4, out_specs=[row, row],
        out_shape=[jax.ShapeDtypeStruct((s, n), F32)] * 2, compiler_params=_params("parallel"),
    )(x1, x2, cos, sin)


def _rope_heads(x, ta, tb, tc, *, out_dtype, name):
    s, n = x.shape
    bs = _blk(s, ROW_BLOCK)

    def body(x_ref, a_ref, b_ref, c_ref, o_ref):
        av, bv, cv = a_ref[...], b_ref[...], c_ref[...]
        for g in range(n // LANES):
            xg = x_ref[:, g * LANES:(g + 1) * LANES]
            og = xg * av + pltpu.roll(xg, LANES - MLA_ROPE_HALF, 1) * bv + pltpu.roll(xg, MLA_ROPE_HALF, 1) * cv
            o_ref[:, g * LANES:(g + 1) * LANES] = og.astype(out_dtype)

    row = pl.BlockSpec((bs, n), lambda i: (i, 0))
    tab = pl.BlockSpec((bs, LANES), lambda i: (i, 0))
    return pl.pallas_call(
        body, name=name, grid=(s // bs,), in_specs=[row, tab, tab, tab], out_specs=row,
        out_shape=jax.ShapeDtypeStruct((s, n), out_dtype), compiler_params=_params("parallel"),
    )(x, ta, tb, tc)


def _group_sum(x, *, name):
    s, n = x.shape
    bs = _blk(s, ROW_BLOCK)

    def body(x_ref, o_ref):
        acc = x_ref[:, 0:LANES]
        for g in range(1, n // LANES):
            acc = acc + x_ref[:, g * LANES:(g + 1) * LANES]
        o_ref[...] = acc

    return pl.pallas_call(
        body, name=name, grid=(s // bs,), in_specs=[pl.BlockSpec((bs, n), lambda i: (i, 0))],
        out_specs=pl.BlockSpec((bs, LANES), lambda i: (i, 0)),
        out_shape=jax.ShapeDtypeStruct((s, LANES), F32), compiler_params=_params("parallel"),
    )(x)


def _shift_down(x, k):
    return pltpu.roll(x, k, 0)


def _conv_rows(ext, w_ref, b_ref, rows):
    y = b_ref[...] + w_ref[0:1, :] * _shift_down(ext, 2) + w_ref[1:2, :] * _shift_down(ext, 1) + w_ref[2:3, :] * ext
    return y[8:8 + rows]


def _conv_gate_fwd(u, cw, cb, *, name):
    s, f2 = u.shape
    f = f2 // 2
    nf = f // LANES
    r = _blk(s, CONV_ROWS)
    r8 = r // 8

    def body(ug_ref, ugp_ref, uv_ref, uvp_ref, wg_ref, wv_ref, bg_ref, bv_ref, o_ref):
        first = pl.program_id(1) == 0

        def conv(cur_ref, prev_ref, w_ref, b_ref):
            prev = jnp.where(first, 0.0, prev_ref[...])
            return _conv_rows(jnp.concatenate([prev, cur_ref[...]], axis=0), w_ref, b_ref, r)

        yg = conv(ug_ref, ugp_ref, wg_ref, bg_ref)
        yv = conv(uv_ref, uvp_ref, wv_ref, bv_ref)
        o_ref[...] = (yg * jax.nn.sigmoid(yg) * yv).astype(BF16)

    def cur(off):
        return pl.BlockSpec((r, LANES), lambda c, i: (i, c + off))

    def prev(off):
        return pl.BlockSpec((8, LANES), lambda c, i: (jnp.maximum(i * r8 - 1, 0), c + off))

    def wspec(rows, off):
        return pl.BlockSpec((rows, LANES), lambda c, i: (0, c + off))

    return pl.pallas_call(
        body, name=name, grid=(nf, s // r),
        in_specs=[cur(0), prev(0), cur(nf), prev(nf), wspec(3, 0), wspec(3, nf), wspec(1, 0), wspec(1, nf)],
        out_specs=pl.BlockSpec((r, LANES), lambda c, i: (i, c)),
        out_shape=jax.ShapeDtypeStruct((s, f), BF16), compiler_params=_params("parallel", "parallel"),
    )(u, u, u, u, cw, cw, cb, cb)


def _conv_gate_bwd(u, cw, cb, dg, *, name):
    s, f2 = u.shape
    f = f2 // 2
    nf = f // LANES
    r = _blk(s, CONV_ROWS)
    r8 = r // 8
    nr = s // r

    def body(ug_ref, ugp_ref, ugn_ref, uv_ref, uvp_ref, uvn_ref, wg_ref, wv_ref, bg_ref, bv_ref, dg_ref, dgn_ref,
             dug_ref, duv_ref, dwg_ref, dwv_ref, dbg_ref, dbv_ref):
        i = pl.program_id(1)
        first, last = i == 0, i == nr - 1

        def ext_of(cur_ref, prev_ref, next_ref):
            prev = jnp.where(first, 0.0, prev_ref[...])
            return jnp.concatenate([prev, cur_ref[...], next_ref[...]], axis=0)

        eg, ev = ext_of(ug_ref, ugp_ref, ugn_ref), ext_of(uv_ref, uvp_ref, uvn_ref)
        yg = _conv_rows(eg, wg_ref, bg_ref, r + 8)
        yv = _conv_rows(ev, wv_ref, bv_ref, r + 8)
        dgn = jnp.where(last, 0.0, dgn_ref[...])
        dgx = jnp.concatenate([dg_ref[...], dgn], axis=0)
        sg = jax.nn.sigmoid(yg)
        dyg = dgx * yv * (sg * (1.0 + yg * (1.0 - sg)))
        dyv = dgx * (yg * sg)

        @pl.when(i == 0)
        def _():
            for ref in (dwg_ref, dwv_ref, dbg_ref, dbv_ref):
                ref[...] = jnp.zeros_like(ref)

        def grads(dy, ext, w_ref, du_ref, dw_ref, db_ref):
            n = r + 8
            du = w_ref[2:3, :] * dy + w_ref[1:2, :] * pltpu.roll(dy, n - 1, 0) + w_ref[0:1, :] * pltpu.roll(dy, n - 2, 0)
            du_ref[...] = du[0:r].astype(BF16)
            dyc = dy[0:r]
            db_ref[...] += jnp.sum(dyc, axis=0, keepdims=True)
            ext_c = ext[0:r + 8]
            dw_ref[0:1, :] += jnp.sum(dyc * _shift_down(ext_c, 2)[8:], axis=0, keepdims=True)
            dw_ref[1:2, :] += jnp.sum(dyc * _shift_down(ext_c, 1)[8:], axis=0, keepdims=True)
            dw_ref[2:3, :] += jnp.sum(dyc * ext_c[8:], axis=0, keepdims=True)

        grads(dyg, eg, wg_ref, dug_ref, dwg_ref, dbg_ref)
        grads(dyv, ev, wv_ref, duv_ref, dwv_ref, dbv_ref)

    def cur(off):
        return pl.BlockSpec((r, LANES), lambda c, i: (i, c + off))

    def prev(off):
        return pl.BlockSpec((8, LANES), lambda c, i: (jnp.maximum(i * r8 - 1, 0), c + off))

    def nxt(off):
        return pl.BlockSpec((8, LANES), lambda c, i: (jnp.minimum((i + 1) * r8, s // 8 - 1), c + off))

    def wspec(rows, off):
        return pl.BlockSpec((rows, LANES), lambda c, i: (0, c + off))

    outs = pl.pallas_call(
        body, name=name, grid=(nf, nr),
        in_specs=[cur(0), prev(0), nxt(0), cur(nf), prev(nf), nxt(nf), wspec(3, 0), wspec(3, nf), wspec(1, 0), wspec(1, nf),
                  cur(0), nxt(0)],
        out_specs=[cur(0), cur(0), wspec(3, 0), wspec(3, 0), wspec(1, 0), wspec(1, 0)],
        out_shape=[jax.ShapeDtypeStruct((s, f), BF16), jax.ShapeDtypeStruct((s, f), BF16),
                   jax.ShapeDtypeStruct((3, f), F32), jax.ShapeDtypeStruct((3, f), F32),
                   jax.ShapeDtypeStruct((1, f), F32), jax.ShapeDtypeStruct((1, f), F32)],
        compiler_params=_params("parallel", "arbitrary"),
    )(u, u, u, u, u, u, cw, cw, cb, cb, dg, dg)
    dug, duv, dwg, dwv, dbg, dbv = outs
    return jnp.concatenate([dug, duv], axis=1), jnp.concatenate([dwg, dwv], axis=1), jnp.concatenate([dbg, dbv], axis=1)


def _adamw(w, g, m, v, *, slabs, name):
    shape = w.shape
    cols = shape[-1]
    rows = w.size // cols
    w2, m2, v2 = (t.reshape(rows, cols) for t in (w, m, v))
    g2 = g.reshape((N_DEV, rows, cols) if slabs else (rows, cols))
    br = _row_blk(rows, ROW_BLOCK // 2 if slabs else ROW_BLOCK)

    def body(w_ref, g_ref, m_ref, v_ref, go_ref, d_ref, nm_ref, nv_ref):
        if slabs:
            gv = g_ref[0].astype(F32)
            for p in range(1, N_DEV):
                gv = gv + g_ref[p].astype(F32)
        else:
            gv = g_ref[...]
        nm = ADAM_B1 * m_ref[...] + (1.0 - ADAM_B1) * gv
        nv = ADAM_B2 * v_ref[...] + (1.0 - ADAM_B2) * (gv * gv)
        m_hat = nm / (1.0 - ADAM_B1 ** ADAM_STEP)
        v_hat = nv / (1.0 - ADAM_B2 ** ADAM_STEP)
        go_ref[...] = gv
        d_ref[...] = -ADAM_LR * (m_hat / (jnp.sqrt(v_hat) + ADAM_EPS) + ADAM_WD * w_ref[...])
        nm_ref[...] = nm
        nv_ref[...] = nv

    spec = pl.BlockSpec((br, cols), lambda i: (i, 0))
    gspec = pl.BlockSpec((N_DEV, br, cols), lambda i: (0, i, 0)) if slabs else spec
    outs = pl.pallas_call(
        body, name=name, grid=(rows // br,), in_specs=[spec, gspec, spec, spec], out_specs=[spec] * 4,
        out_shape=[jax.ShapeDtypeStruct((rows, cols), F32)] * 4, compiler_params=_params("parallel"),
    )(w2, g2, m2, v2)
    return tuple(t.reshape(shape) for t in outs)


def _exchange(xs, *, same_src, name):
    n = len(xs)
    slabs = [x.shape if same_src else x.shape[1:] for x in xs]

    def body(*refs):
        x_refs, o_refs = refs[:n], refs[n:2 * n]
        send_sems, recv_sems, loc_sems = refs[2 * n:]
        ix, iy, ic = lax.axis_index("x"), lax.axis_index("y"), lax.axis_index("c")
        me = 4 * ix + 2 * iy + ic
        local, sends, recvs = [], [], []
        for a in range(n):
            def src(p, a=a):
                return x_refs[a] if same_src else x_refs[a].at[p]

            local.append(pltpu.make_async_copy(src(me), o_refs[a].at[me], loc_sems.at[a]))
            for k in (1, 2, 4, 3, 5, 6, 7):
                px = 1 - ix if k & 4 else ix
                py = 1 - iy if k & 2 else iy
                pc = 1 - ic if k & 1 else ic
                p = 4 * px + 2 * py + pc
                for dst, out in ((me, sends), (p, recvs)):
                    out.append(pltpu.make_async_remote_copy(
                        src_ref=src(p), dst_ref=o_refs[a].at[dst], send_sem=send_sems.at[a, k - 1],
                        recv_sem=recv_sems.at[a, k - 1], device_id=(px, py, pc), device_id_type=pl.DeviceIdType.MESH))
        for cp in local + sends:
            cp.start()
        for cp in recvs:
            cp.wait_recv()
        for cp in sends:
            cp.wait_send()
        for cp in local:
            cp.wait()

    return pl.pallas_call(
        body, name=name,
        in_specs=[pl.BlockSpec(memory_space=pl.ANY)] * n, out_specs=[pl.BlockSpec(memory_space=pl.ANY)] * n,
        out_shape=[jax.ShapeDtypeStruct((N_DEV,) + tuple(sl), x.dtype) for sl, x in zip(slabs, xs)],
        scratch_shapes=[pltpu.SemaphoreType.DMA((n, N_DEV - 1)), pltpu.SemaphoreType.DMA((n, N_DEV - 1)),
                        pltpu.SemaphoreType.DMA((n,))],
        compiler_params=pltpu.CompilerParams(has_side_effects=True, vmem_limit_bytes=VMEM_LIMIT_BYTES),
    )(*xs)


def _gather_two_level(xs, *, name):
    n = len(xs)

    def body(*refs):
        x_refs, o_refs = refs[:n], refs[n:2 * n]
        send_sems, recv_sems, loc_sems = refs[2 * n:]
        ix, iy, ic = lax.axis_index("x"), lax.axis_index("y"), lax.axis_index("c")
        me = 4 * ix + 2 * iy + ic
        sib = 4 * ix + 2 * iy + (1 - ic)
        chips = [(1 - ix if ch & 2 else ix, 1 - iy if ch & 1 else iy) for ch in (1, 2, 3)]

        def copy(a, pos, src, slot, to):
            return pltpu.make_async_remote_copy(
                src_ref=src, dst_ref=o_refs[a].at[slot], send_sem=send_sems.at[a, pos], recv_sem=recv_sems.at[a, pos],
                device_id=to, device_id_type=pl.DeviceIdType.MESH)

        local = [pltpu.make_async_copy(x_refs[a], o_refs[a].at[me], loc_sems.at[a]) for a in range(n)]
        first, passed, arrive = [], [], []
        for a in range(n):
            first.append(copy(a, 0, x_refs[a], me, (ix, iy, 1 - ic)))
            arrive.append(copy(a, 0, x_refs[a], sib, (ix, iy, 1 - ic)))
            for ch, (px, py) in enumerate(chips, start=1):
                same, other = 4 * px + 2 * py + ic, 4 * px + 2 * py + (1 - ic)
                first.append(copy(a, 2 * ch - 1, x_refs[a], me, (px, py, ic)))
                passed.append((copy(a, 2 * ch - 1, x_refs[a], same, (px, py, ic)),
                               copy(a, 2 * ch, o_refs[a].at[same], same, (ix, iy, 1 - ic))))
                arrive.append(copy(a, 2 * ch, x_refs[a], other, (ix, iy, 1 - ic)))
        for cp in local + first:
            cp.start()
        for landed, onward in passed:
            landed.wait_recv()
            onward.start()
        for cp in arrive:
            cp.wait_recv()
        for cp in first + [onward for _, onward in passed]:
            cp.wait_send()
        for cp in local:
            cp.wait()

    return pl.pallas_call(
        body, name=name,
        in_specs=[pl.BlockSpec(memory_space=pl.ANY)] * n, out_specs=[pl.BlockSpec(memory_space=pl.ANY)] * n,
        out_shape=[jax.ShapeDtypeStruct((N_DEV,) + tuple(x.shape), x.dtype) for x in xs],
        scratch_shapes=[pltpu.SemaphoreType.DMA((n, N_DEV - 1)), pltpu.SemaphoreType.DMA((n, N_DEV - 1)),
                        pltpu.SemaphoreType.DMA((n,))],
        compiler_params=pltpu.CompilerParams(has_side_effects=True, vmem_limit_bytes=VMEM_LIMIT_BYTES),
    )(*xs)


def _sum_slabs(x, *, name):
    n, r, c = x.shape
    br = _row_blk(r, ROW_BLOCK)

    def body(x_ref, o_ref):
        acc = x_ref[0]
        for p in range(1, n):
            acc = acc + x_ref[p]
        o_ref[...] = acc

    return pl.pallas_call(
        body, name=name, grid=(r // br,), in_specs=[pl.BlockSpec((n, br, c), lambda i: (0, i, 0))],
        out_specs=pl.BlockSpec((br, c), lambda i: (i, 0)),
        out_shape=jax.ShapeDtypeStruct((r, c), F32), compiler_params=_params("parallel"),
    )(x)


def _silu(x, *, name):
    def body(x_ref, o_ref):
        xv = x_ref[...]
        o_ref[...] = (xv * jax.nn.sigmoid(xv)).astype(BF16)

    return pl.pallas_call(body, name=name, out_shape=jax.ShapeDtypeStruct(x.shape, BF16),
                          compiler_params=_params())(x)


_BIG = {"fox_w_in": 2, "fox_w_o": 1, "mla_w_a": 1, "mla_w_uq": 2, "mla_w_ukv": 2, "mla_w_o": 1, "ffn_w_in": 2, "ffn_w_out": 1}
_SMALL = {"mla_g_q": 1, "mla_g_kv": 1, "ffn_conv_w": 2}
_REPL = ("fox_b_f", "ffn_conv_b", "final_g")


def _gathered_to_full(g, axis):
    full = jnp.moveaxis(g, 0, axis)
    shape = list(full.shape)
    shape[axis:axis + 2] = [shape[axis] * shape[axis + 1]]
    return full.reshape(shape)


def _full_to_chunks(full, axis):
    shape = list(full.shape)
    shape[axis:axis + 1] = [N_DEV, shape[axis] // N_DEV]
    return jnp.moveaxis(full.reshape(shape), axis, 0)


def _per_head(parts, s_or_rows):
    return jnp.concatenate([p.reshape(s_or_rows, N_HEADS, -1) for p in parts], axis=-1).reshape(s_or_rows, -1)


def kernel(x, c, ada_w, ada_b, fox_w_in, fox_b_f, fox_w_o, mla_w_a, mla_g_q, mla_g_kv, mla_w_uq, mla_w_ukv, mla_w_o, ffn_w_in, ffn_conv_w, ffn_conv_b, ffn_w_out, final_g, loss_target, m_ada_w, m_ada_b, m_fox_w_in, m_fox_b_f, m_fox_w_o, m_mla_w_a, m_mla_g_q, m_mla_g_kv, m_mla_w_uq, m_mla_w_ukv, m_mla_w_o, m_ffn_w_in, m_ffn_conv_w, m_ffn_conv_b, m_ffn_w_out, m_final_g, v_ada_w, v_ada_b, v_fox_w_in, v_fox_b_f, v_fox_w_o, v_mla_w_a, v_mla_g_q, v_mla_g_kv, v_mla_w_uq, v_mla_w_ukv, v_mla_w_o, v_ffn_w_in, v_ffn_conv_w, v_ffn_conv_b, v_ffn_w_out, v_final_g):
    weights = dict(ada_w=ada_w, ada_b=ada_b, fox_w_in=fox_w_in, fox_b_f=fox_b_f, fox_w_o=fox_w_o, mla_w_a=mla_w_a,
                   mla_g_q=mla_g_q, mla_g_kv=mla_g_kv, mla_w_uq=mla_w_uq, mla_w_ukv=mla_w_ukv, mla_w_o=mla_w_o,
                   ffn_w_in=ffn_w_in, ffn_conv_w=ffn_conv_w, ffn_conv_b=ffn_conv_b, ffn_w_out=ffn_w_out, final_g=final_g)
    mom_m = dict(ada_w=m_ada_w, ada_b=m_ada_b, fox_w_in=m_fox_w_in, fox_b_f=m_fox_b_f, fox_w_o=m_fox_w_o, mla_w_a=m_mla_w_a,
                 mla_g_q=m_mla_g_q, mla_g_kv=m_mla_g_kv, mla_w_uq=m_mla_w_uq, mla_w_ukv=m_mla_w_ukv, mla_w_o=m_mla_w_o,
                 ffn_w_in=m_ffn_w_in, ffn_conv_w=m_ffn_conv_w, ffn_conv_b=m_ffn_conv_b, ffn_w_out=m_ffn_w_out, final_g=m_final_g)
    mom_v = dict(ada_w=v_ada_w, ada_b=v_ada_b, fox_w_in=v_fox_w_in, fox_b_f=v_fox_b_f, fox_w_o=v_fox_w_o, mla_w_a=v_mla_w_a,
                 mla_g_q=v_mla_g_q, mla_g_kv=v_mla_g_kv, mla_w_uq=v_mla_w_uq, mla_w_ukv=v_mla_w_ukv, mla_w_o=v_mla_w_o,
                 ffn_w_in=v_ffn_w_in, ffn_conv_w=v_ffn_conv_w, ffn_conv_b=v_ffn_conv_b, ffn_w_out=v_ffn_w_out, final_g=v_final_g)
    order = list(weights)
    x0 = x[0]
    target = loss_target[0]
    s = x0.shape[0]
    d = D_MODEL
    cols = ada_w.shape[-1]
    nq = N_HEADS * HEAD_DIM

    small_names = ["c"] + list(_SMALL)
    small_all = dict(zip(small_names, _exchange([c] + [weights[n] for n in _SMALL], same_src=True, name="gather_small")))
    c_all = small_all["c"].reshape(N_DEV, d)
    g_q = _gathered_to_full(small_all["mla_g_q"], 1)
    g_kv = _gathered_to_full(small_all["mla_g_kv"], 1)
    conv_w = _gathered_to_full(small_all["ffn_conv_w"], 2)

    c_pad = _pad_axis(c_all, 0, LANES)
    silu_c = _silu(c_pad, name="silu_c")
    w_ada = ada_w.reshape(4, d, cols)
    b_ada = ada_b.reshape(4, 1, cols)
    mods = [_matmul(silu_c, w_ada[i], name=f"ada_mod{i}")[:N_DEV] + b_ada[i] for i in range(4)]
    mod_send = _pad_axis(jnp.stack(mods, axis=1), 1, 8)
    mod_recv, = _exchange([mod_send], same_src=False, name="scatter_mod")
    mod = mod_recv[:, :4].transpose(1, 0, 2).reshape(4, 3 * d)
    shift = [mod[i:i + 1, 0:d] for i in range(4)]
    scale = [mod[i:i + 1, d:2 * d] for i in range(4)]
    gate = [mod[i:i + 1, 2 * d:3 * d] for i in range(4)]

    big_all = _gather_two_level([weights[n].astype(BF16) for n in _BIG], name="gather_weights")
    wfull = {n: _gathered_to_full(g, _BIG[n]) for n, g in zip(_BIG, big_all)}

    w_fox_in = _pad_axis(wfull["fox_w_in"][0], 1, LANES)
    w_fox_qkv, w_fox_f = w_fox_in[:, :3 * nq], w_fox_in[:, 3 * nq:]
    w_fox_o = wfull["fox_w_o"][0]
    w_a = _pad_axis(wfull["mla_w_a"][0], 1, LANES)
    wq = wfull["mla_w_uq"][0].reshape(MLA_Q_RANK, N_HEADS, HEAD_DIM + MLA_ROPE_DIM)
    w_uq = _pad_axis(wq, 2, LANES).reshape(MLA_Q_RANK, N_HEADS * LANES)
    wkv = wfull["mla_w_ukv"][0].reshape(MLA_KV_RANK, N_HEADS, 2 * HEAD_DIM)
    w_ukv = jnp.concatenate([wkv[:, :, :HEAD_DIM].reshape(MLA_KV_RANK, -1), wkv[:, :, HEAD_DIM:].reshape(MLA_KV_RANK, -1)], axis=1)
    w_mla_o = wfull["mla_w_o"][0]
    w_ffn_in = wfull["ffn_w_in"]
    w_ffn_out = wfull["ffn_w_out"]
    conv_b = ffn_conv_b

    fox_scale = HEAD_DIM ** -0.5
    mla_scale = (HEAD_DIM + MLA_ROPE_DIM) ** -0.5
    pos = jnp.arange(s, dtype=F32)
    inv_freq = ROPE_BASE ** (-jnp.arange(0, MLA_ROPE_DIM, 2, dtype=F32) / MLA_ROPE_DIM)
    ang = pos[:, None] * inv_freq[None, :]
    cos16, sin16 = jnp.cos(ang), jnp.sin(ang)
    z16, z32, z64 = jnp.zeros((s, 16), F32), jnp.zeros((s, 32), F32), jnp.zeros((s, 64), F32)
    tab_a = jnp.concatenate([jnp.ones((s, 64), F32), cos16, cos16, z32], axis=1) * (mla_scale * LOG2E)
    tab_b = jnp.concatenate([z64, -sin16, z16, z32], axis=1) * (mla_scale * LOG2E)
    tab_c = jnp.concatenate([z64, z16, sin16, z32], axis=1) * (mla_scale * LOG2E)

    h0 = _norm_fwd(x0, scale[0], shift[0], plus_one=True, out_dtype=BF16, name="ada_fwd0")
    q_mult = jnp.concatenate([jnp.full((1, nq), fox_scale * LOG2E, F32), jnp.ones((1, 2 * nq), F32)], axis=1)
    qkv = _matmul(h0, w_fox_qkv, out_dtype=BF16, col_scale=q_mult, name="fox_proj")
    fl = _matmul(h0, w_fox_f, name="fox_proj_f")[:, :N_HEADS]
    cum = _fox_gate_fwd(fl, fox_b_f, name="fox_gate_fwd")
    fox_cfg = dict(qc=0, kc=N_HEADS // 2, vc=N_HEADS, width=HEAD_DIM)
    kb2 = cum * -LOG2E
    att_blk = _blk(s, ATT_BLOCK)
    fvt = _matmul(w_fox_qkv[:, 2 * nq:], h0, ta=True, tb=True, out_dtype=BF16, name="fox_proj_vt")
    fo, fox_lse = _attn_fwd(qkv, qkv, _key_blocks(fvt, att_blk), _bias_lane_terms(kb2), qc=0, kc=N_HEADS // 2,
                            width=HEAD_DIM, name="fox_attn_fwd")
    fox_res = (qkv, qkv, qkv, kb2, fo, fox_lse)
    y0 = _matmul(fo, w_fox_o, name="fox_out")
    x1 = _resid_fwd(x0, y0, gate[0], name="resid_fwd0")

    def ffn_fwd(xin, li, sub):
        hh = _norm_fwd(xin, scale[sub], shift[sub], plus_one=True, out_dtype=BF16, name=f"ada_fwd{sub}")
        u = _matmul(hh, w_ffn_in[li], name=f"ffn_up{li}")
        g = _conv_gate_fwd(u, conv_w[li], conv_b[li:li + 1], name=f"conv_fwd{li}")
        y = _matmul(g, w_ffn_out[li], name=f"ffn_down{li}")
        return _resid_fwd(xin, y, gate[sub], name=f"resid_fwd{sub}"), (hh, u, g, y)

    x2, ffn0_res = ffn_fwd(x1, 0, 1)

    h2 = _norm_fwd(x2, scale[2], shift[2], plus_one=True, out_dtype=BF16, name="ada_fwd2")
    a = _matmul(h2, w_a, name="mla_a")
    a_q, a_kv = a[:, :MLA_Q_RANK], a[:, MLA_Q_RANK:MLA_Q_RANK + MLA_KV_RANK]
    kr1 = a[:, MLA_Q_RANK + MLA_KV_RANK:MLA_Q_RANK + MLA_KV_RANK + MLA_ROPE_HALF]
    kr2 = a[:, MLA_Q_RANK + MLA_KV_RANK + MLA_ROPE_HALF:MLA_Q_RANK + MLA_KV_RANK + MLA_ROPE_DIM]
    cq = _norm_fwd(a_q, g_q, jnp.zeros_like(g_q), plus_one=False, out_dtype=BF16, name="mla_norm_q")
    ckv = _norm_fwd(a_kv, g_kv, jnp.zeros_like(g_kv), plus_one=False, out_dtype=BF16, name="mla_norm_kv")
    qf = _matmul(cq, w_uq, name="mla_uq")
    kvf = _matmul(ckv, w_ukv, out_dtype=BF16, name="mla_ukv")
    mq = _rope_heads(qf, tab_a, tab_b, tab_c, out_dtype=BF16, name="rope_q")
    kk1, kk2 = _rope(kr1, kr2, cos16, sin16, negate=False, name="rope_k")
    k_tail = jnp.concatenate([kk1, kk2, z32], axis=1).astype(BF16)
    mk = jnp.concatenate([kvf[:, :nq].reshape(s, N_HEADS, HEAD_DIM),
                          jnp.broadcast_to(k_tail[:, None, :], (s, N_HEADS, HEAD_DIM))], axis=-1).reshape(s, N_HEADS * LANES)
    mla_cfg = dict(qc=0, kc=0, vc=N_HEADS // 2, width=LANES)
    mvt = _matmul(w_ukv[:, nq:], ckv, ta=True, tb=True, out_dtype=BF16, name="mla_ukv_vt")
    mo, mla_lse = _attn_fwd(mq, mk, _key_blocks(mvt, att_blk), None, qc=0, kc=0, width=LANES, name="mla_attn_fwd")
    mla_res = (mq, mk, kvf, None, mo, mla_lse)
    y2 = _matmul(mo, w_mla_o, name="mla_out")
    x3 = _resid_fwd(x2, y2, gate[2], name="resid_fwd2")

    x4, ffn1_res = ffn_fwd(x3, 1, 3)

    loss_vec, dx4, d_final_g = _final_loss(x4, final_g.reshape(1, d), target, name="final_loss")
    loss = lax.psum(loss_vec[0, 0], ("x", "y", "c"))

    grads = {}
    dmod = [None] * 4

    def ffn_bwd(dx_out, xin, li, sub, res):
        hh, u, g, y = res
        dy, dgate = _resid_bwd(dx_out, y, gate[sub], name=f"resid_bwd{sub}")
        gw_out = _matmul(g, dy, ta=True, out_dtype=BF16, name=f"ffn_down_dw{li}")
        dg = _matmul(dy, w_ffn_out[li], tb=True, name=f"ffn_down_dx{li}")
        du, dcw, dcb = _conv_gate_bwd(u, conv_w[li], conv_b[li:li + 1], dg, name=f"conv_bwd{li}")
        gw_in = _matmul(hh, du, ta=True, out_dtype=BF16, name=f"ffn_up_dw{li}")
        dh = _matmul(du, w_ffn_in[li], tb=True, out_dtype=BF16, name=f"ffn_up_dx{li}")
        dx_in, dscale, dshift = _norm_bwd(xin, scale[sub], dh, dx_out, plus_one=True, name=f"ada_bwd{sub}")
        dmod[sub] = jnp.concatenate([dshift, dscale, dgate], axis=1)
        return dx_in, gw_in, dcw, dcb, gw_out

    dx3, gw_in1, dcw1, dcb1, gw_out1 = ffn_bwd(dx4, x3, 1, 3, ffn1_res)

    dy2, dgate2 = _resid_bwd(dx3, y2, gate[2], name="resid_bwd2")
    grads["mla_w_o"] = _matmul(mo, dy2, ta=True, out_dtype=BF16, name="mla_out_dw")[None]
    dmo = _matmul(dy2, w_mla_o, tb=True, out_dtype=BF16, name="mla_out_dx")
    dmq, dmk, dmv, _ = _attention_bwd(mla_res, dmo, dq_mult=1.0 / LOG2E, dk_mult=1.0 / LOG2E, out_dtype=F32,
                                      name="mla_attn", **mla_cfg)
    dqf = _rope_heads(dmq, tab_a, -tab_b, -tab_c, out_dtype=BF16, name="rope_q_bwd")
    g_uq = _matmul(cq, dqf, ta=True, out_dtype=BF16, name="mla_uq_dw")
    dcq = _matmul(dqf, w_uq, tb=True, name="mla_uq_dx")
    dmk3 = dmk.reshape(s, N_HEADS, LANES)
    dkr = _group_sum(dmk, name="mla_krope_sum")
    dkr1, dkr2 = _rope(dkr[:, HEAD_DIM:HEAD_DIM + MLA_ROPE_HALF], dkr[:, HEAD_DIM + MLA_ROPE_HALF:HEAD_DIM + MLA_ROPE_DIM],
                       cos16, sin16, negate=True, name="rope_k_bwd")
    dkvf = jnp.concatenate([dmk3[:, :, :HEAD_DIM].reshape(s, nq).astype(BF16), dmv.astype(BF16)], axis=1)
    g_ukv = _matmul(ckv, dkvf, ta=True, out_dtype=BF16, name="mla_ukv_dw")
    dckv = _matmul(dkvf, w_ukv, tb=True, name="mla_ukv_dx")
    da_q, dg_q, _ = _norm_bwd(a_q, g_q, dcq, None, plus_one=False, name="mla_norm_q_bwd")
    da_kv, dg_kv, _ = _norm_bwd(a_kv, g_kv, dckv, None, plus_one=False, name="mla_norm_kv_bwd")
    da = jnp.concatenate([da_q, da_kv, dkr1, dkr2, jnp.zeros((s, w_a.shape[1] - 672), F32)], axis=1).astype(BF16)
    grads["mla_w_a"] = _matmul(h2, da, ta=True, out_dtype=BF16, name="mla_a_dw")[None, :, :672]
    dh2 = _matmul(da, w_a, tb=True, out_dtype=BF16, name="mla_a_dx")
    dx2, dscale2, dshift2 = _norm_bwd(x2, scale[2], dh2, dx3, plus_one=True, name="ada_bwd2")
    dmod[2] = jnp.concatenate([dshift2, dscale2, dgate2], axis=1)
    grads["mla_w_uq"] = g_uq.reshape(MLA_Q_RANK, N_HEADS, LANES)[:, :, :HEAD_DIM + MLA_ROPE_DIM].reshape(1, MLA_Q_RANK, -1)
    grads["mla_w_ukv"] = _per_head([g_ukv[:, :nq], g_ukv[:, nq:]], MLA_KV_RANK)[None]
    grads["mla_g_q"], grads["mla_g_kv"] = dg_q, dg_kv

    dx1, gw_in0, dcw0, dcb0, gw_out0 = ffn_bwd(dx2, x1, 0, 1, ffn0_res)
    grads["ffn_w_in"] = jnp.stack([gw_in0, gw_in1])
    grads["ffn_w_out"] = jnp.stack([gw_out0, gw_out1])
    grads["ffn_conv_w"] = jnp.stack([dcw0, dcw1])
    g_conv_b = jnp.concatenate([dcb0, dcb1], axis=0)

    dy0, dgate0 = _resid_bwd(dx1, y0, gate[0], name="resid_bwd0")
    grads["fox_w_o"] = _matmul(fo, dy0, ta=True, out_dtype=BF16, name="fox_out_dw")[None]
    dfo = _matmul(dy0, w_fox_o, tb=True, out_dtype=BF16, name="fox_out_dx")
    dfq, dfk, dfv, dcum = _attention_bwd(fox_res, dfo, dq_mult=fox_scale, dk_mult=1.0 / LOG2E, out_dtype=BF16,
                                         name="fox_attn", **fox_cfg)
    dfl, g_b_f = _fox_gate_bwd(fl, fox_b_f, dcum, name="fox_gate_bwd")
    dproj = jnp.concatenate([dfq.astype(BF16), dfk, dfv, _pad_axis(dfl, 1, LANES).astype(BF16)], axis=1)
    grads["fox_w_in"] = _matmul(h0, dproj, ta=True, out_dtype=BF16, name="fox_proj_dw")[None, :, :3 * nq + N_HEADS]
    dh0 = _matmul(dproj, w_fox_in, tb=True, out_dtype=BF16, name="fox_proj_dx")
    dx0, dscale0, dshift0 = _norm_bwd(x0, scale[0], dh0, dx1, plus_one=True, name="ada_bwd0")
    dmod[0] = jnp.concatenate([dshift0, dscale0, dgate0], axis=1)

    dmod_send = _pad_axis(jnp.stack(dmod, axis=0).reshape(4, N_DEV, cols).transpose(1, 0, 2), 1, 8)
    dmod_recv, = _exchange([dmod_send], same_src=False, name="scatter_dmod")
    dmod_all = dmod_recv[:, :4]
    dmod_pad = _pad_axis(dmod_all, 0, LANES)
    g_ada_w = jnp.stack([_matmul(silu_c, dmod_pad[:, i], ta=True, name=f"ada_dw{i}") for i in range(4)])
    grads["ada_w"] = g_ada_w.reshape(ada_w.shape)
    grads["ada_b"] = _sum_slabs(dmod_recv, name="ada_db")[:4].reshape(ada_b.shape)

    sharded = list(_BIG) + list(_SMALL)
    axes = {**_BIG, **_SMALL}
    recv = _exchange([_full_to_chunks(grads[n], axes[n]) for n in sharded], same_src=False, name="scatter_grads")
    grads.update(dict(zip(sharded, recv)))
    repl = _exchange([g_b_f, g_conv_b, d_final_g], same_src=True, name="gather_repl_grads")
    grads.update(dict(zip(_REPL, repl)))

    grad_out, deltas, new_m, new_v = {}, {}, {}, {}
    for n in order:
        grad_out[n], deltas[n], new_m[n], new_v[n] = _adamw(
            weights[n], grads[n], mom_m[n], mom_v[n], slabs=n in axes or n in _REPL, name=f"adamw_{n}")

    grad_x = dx0[None]
    return (loss, grad_x, *[grad_out[n] for n in order], *[deltas[n] for n in order],
            *[new_m[n] for n in order], *[new_v[n] for n in order])
```
